```python
import math
import jax, jax.numpy as jnp
from jax import lax
import numpy as np

D_MODEL = 1024
BATCH = 16
SEQ = 2048
DEPTH = 1

D_MIX = D_MODEL
CONV_CH = D_MIX // 2
CONV_GROUPS = 8
CONV_K = 31
FOX_HEADS = 8
FOX_HEAD_DIM = 64
FOX_W = FOX_HEADS * FOX_HEAD_DIM
Q_BLOCK = 128
MEM_LEN = 256
MEM_HEADS = 4
MEM_HEAD_DIM = D_MODEL // MEM_HEADS
D_FF = ((8 * D_MODEL // 3 + 255) // 256) * 256
EPS = 1e-6

OFF_U = 0
OFF_G = OFF_U + CONV_CH
OFF_Q = OFF_G + CONV_CH
OFF_K = OFF_Q + FOX_W
OFF_V = OFF_K + FOX_W
OFF_F = OFF_V + FOX_W
D_IN = OFF_F + FOX_HEADS

kernel_name = "hybrid_conformer_fox_memory_block"


def rmsnorm(x, g):
    xf = x.astype(jnp.float32)
    y = xf * lax.rsqrt(jnp.mean(xf * xf, axis=-1, keepdims=True) + EPS)
    return (y * g.astype(jnp.float32)).astype(x.dtype)


def layernorm(x, g, b):
    xf = x.astype(jnp.float32)
    mu = jnp.mean(xf, axis=-1, keepdims=True)
    xc = xf - mu
    y = xc * lax.rsqrt(jnp.mean(xc * xc, axis=-1, keepdims=True) + EPS)
    return (y * g.astype(jnp.float32) + b.astype(jnp.float32)).astype(x.dtype)


def conformer_conv(u, gate, conv_w, conv_b, ln_g, ln_b):
    a = u * jax.nn.sigmoid(gate)
    y = lax.conv_general_dilated(
        a, conv_w[:, None, :].astype(a.dtype),
        window_strides=(1,), padding=[(CONV_K - 1, 0)],
        dimension_numbers=("NWC", "WIO", "NWC"),
        feature_group_count=CONV_CH) + conv_b.astype(a.dtype)
    return jax.nn.silu(layernorm(y, ln_g, ln_b))


def forgetting_attention(q, k, v, logf):
    b, s, h, dh = q.shape
    scale = 1.0 / math.sqrt(dh)
    qh = jnp.transpose(q, (0, 2, 1, 3))
    kh = jnp.transpose(k, (0, 2, 1, 3))
    vh = jnp.transpose(v, (0, 2, 1, 3))
    c = jnp.transpose(jnp.cumsum(logf, axis=1), (0, 2, 1))
    outs = []
    for i in range(s // Q_BLOCK):
        q0, end = i * Q_BLOCK, (i + 1) * Q_BLOCK
        logits = jnp.einsum("bhqd,bhkd->bhqk", qh[:, :, q0:end], kh[:, :, :end],
                            preferred_element_type=jnp.float32) * scale
        logits = logits + (c[:, :, q0:end, None] - c[:, :, None, :end])
        causal = jnp.arange(end)[None, :] <= (q0 + jnp.arange(Q_BLOCK))[:, None]
        logits = jnp.where(causal[None, None], logits, -jnp.inf)
        p = jax.nn.softmax(logits, axis=-1)
        outs.append(jnp.einsum("bhqk,bhkd->bhqd", p.astype(vh.dtype), vh[:, :, :end]))
    o = jnp.concatenate(outs, axis=2)
    return jnp.transpose(o, (0, 2, 1, 3)).reshape(b, s, h * dh)


def memory_cross_attention(hx, mem_n, w_mq, w_mkv, w_mo):
    b, s, _ = hx.shape
    m = mem_n.shape[1]
    q = (hx @ w_mq).reshape(b, s, MEM_HEADS, MEM_HEAD_DIM)
    kv = mem_n @ w_mkv
    k = kv[..., :D_MODEL].reshape(b, m, MEM_HEADS, MEM_HEAD_DIM)
    v = kv[..., D_MODEL:].reshape(b, m, MEM_HEADS, MEM_HEAD_DIM)
    logits = jnp.einsum("bshd,bmhd->bhsm", q, k,
                        preferred_element_type=jnp.float32) / math.sqrt(MEM_HEAD_DIM)
    p = jax.nn.softmax(logits, axis=-1)
    o = jnp.einsum("bhsm,bmhd->bshd", p.astype(v.dtype), v).reshape(b, s, D_MODEL)
    return o @ w_mo


def _fwd_setup_inputs(seed: int = 0) -> dict:
    key = jax.random.key(seed)
    ks = jax.random.split(key, 24)
    f32 = jnp.float32

    def nrm(k, shape, fan_in):
        return jax.random.normal(k, shape, f32) * (fan_in ** -0.5)

    def gain(k, shape):
        return 1.0 + 0.02 * jax.random.normal(k, shape, f32)

    def small(k, shape, s=0.02):
        return s * jax.random.normal(k, shape, f32)

    return {
        "x": jax.random.normal(ks[0], (BATCH, SEQ, D_MODEL), f32),
        "mem": jax.random.normal(ks[1], (BATCH, MEM_LEN, D_MODEL), f32),
        "g_mix": gain(ks[2], (DEPTH, D_MODEL)),
        "w_in": nrm(ks[3], (DEPTH, D_MODEL, D_IN), D_MODEL),
        "b_f": 2.0 + small(ks[4], (DEPTH, FOX_HEADS), 0.5),
        "conv_w": nrm(ks[5], (DEPTH, CONV_K, CONV_CH), CONV_K),
        "conv_b": small(ks[6], (DEPTH, CONV_CH)),
        "ln_g": gain(ks[7], (DEPTH, CONV_CH)),
        "ln_b": small(ks[8], (DEPTH, CONV_CH)),
        "w_out": nrm(ks[9], (DEPTH, D_MIX, D_MODEL), D_MIX),
        "g_x": gain(ks[10], (DEPTH, D_MODEL)),
        "g_mem": gain(ks[11], (D_MODEL,)),
        "w_mq": nrm(ks[12], (DEPTH, D_MODEL, D_MODEL), D_MODEL),
        "w_mkv": nrm(ks[13], (DEPTH, D_MODEL, 2 * D_MODEL), D_MODEL),
        "w_mo": nrm(ks[14], (DEPTH, D_MODEL, D_MODEL), D_MODEL),
        "g_ffn": gain(ks[15], (DEPTH, D_MODEL)),
        "w_gu": nrm(ks[16], (DEPTH, D_MODEL, 2 * D_FF), D_MODEL),
        "w_down": nrm(ks[17], (DEPTH, D_FF, D_MODEL), D_FF),
        "g_final": gain(ks[18], (D_MODEL,)),
    }


def _fwd_reference(x, mem, g_mix, w_in, b_f, conv_w, conv_b, ln_g, ln_b, w_out,
              g_x, g_mem, w_mq, w_mkv, w_mo, g_ffn, w_gu, w_down, g_final):
    b, s, _ = x.shape
    mem_n = rmsnorm(mem, g_mem)
    for l in range(DEPTH):
        h = rmsnorm(x, g_mix[l])
        z = h @ w_in[l]
        conv_out = conformer_conv(z[..., OFF_U:OFF_G], z[..., OFF_G:OFF_Q],
                                  conv_w[l], conv_b[l], ln_g[l], ln_b[l])
        q = z[..., OFF_Q:OFF_K].reshape(b, s, FOX_HEADS, FOX_HEAD_DIM)
        k = z[..., OFF_K:OFF_V].reshape(b, s, FOX_HEADS, FOX_HEAD_DIM)
        v = z[..., OFF_V:OFF_F].reshape(b, s, FOX_HEADS, FOX_HEAD_DIM)
        logf = jax.nn.log_sigmoid((z[..., OFF_F:] + b_f[l]).astype(jnp.float32))
        att_out = forgetting_attention(q, k, v, logf)
        x = x + jnp.concatenate([conv_out, att_out], axis=-1) @ w_out[l]
        x = x + memory_cross_attention(rmsnorm(x, g_x[l]), mem_n, w_mq[l], w_mkv[l], w_mo[l])
        gu = rmsnorm(x, g_ffn[l]) @ w_gu[l]
        x = x + (jax.nn.silu(gu[..., :D_FF]) * gu[..., D_FF:]) @ w_down[l]
    return rmsnorm(x, g_final)


import jax as _jax
import jax.numpy as _jnp

TWIN_FORMAT = 'train_step'
FWD_PARAMS = ['x', 'mem', 'g_mix', 'w_in', 'b_f', 'conv_w', 'conv_b', 'ln_g', 'ln_b', 'w_out', 'g_x', 'g_mem', 'w_mq', 'w_mkv', 'w_mo', 'g_ffn', 'w_gu', 'w_down', 'g_final']
TWIN_WEIGHTS = ['g_mix', 'w_in', 'b_f', 'conv_w', 'conv_b', 'ln_g', 'ln_b', 'w_out', 'g_x', 'g_mem', 'w_mq', 'w_mkv', 'w_mo', 'g_ffn', 'w_gu', 'w_down', 'g_final']
TWIN_DIFF_INPUT = 'x'
TWIN_INPUTS = ['x', 'mem', 'g_mix', 'w_in', 'b_f', 'conv_w', 'conv_b', 'ln_g', 'ln_b', 'w_out', 'g_x', 'g_mem', 'w_mq', 'w_mkv', 'w_mo', 'g_ffn', 'w_gu', 'w_down', 'g_final', 'loss_target', 'm_g_mix', 'm_w_in', 'm_b_f', 'm_conv_w', 'm_conv_b', 'm_ln_g', 'm_ln_b', 'm_w_out', 'm_g_x', 'm_g_mem', 'm_w_mq', 'm_w_mkv', 'm_w_mo', 'm_g_ffn', 'm_w_gu', 'm_w_down', 'm_g_final', 'v_g_mix', 'v_w_in', 'v_b_f', 'v_conv_w', 'v_conv_b', 'v_ln_g', 'v_ln_b', 'v_w_out', 'v_g_x', 'v_g_mem', 'v_w_mq', 'v_w_mkv', 'v_w_mo', 'v_g_ffn', 'v_w_gu', 'v_w_down', 'v_g_final']
TWIN_OUTPUTS = ['loss', 'grad_x', 'grad_g_mix', 'grad_w_in', 'grad_b_f', 'grad_conv_w', 'grad_conv_b', 'grad_ln_g', 'grad_ln_b', 'grad_w_out', 'grad_g_x', 'grad_g_mem', 'grad_w_mq', 'grad_w_mkv', 'grad_w_mo', 'grad_g_ffn', 'grad_w_gu', 'grad_w_down', 'grad_g_final', 'delta_g_mix', 'delta_w_in', 'delta_b_f', 'delta_conv_w', 'delta_conv_b', 'delta_ln_g', 'delta_ln_b', 'delta_w_out', 'delta_g_x', 'delta_g_mem', 'delta_w_mq', 'delta_w_mkv', 'delta_w_mo', 'delta_g_ffn', 'delta_w_gu', 'delta_w_down', 'delta_g_final', 'new_m_g_mix', 'new_m_w_in', 'new_m_b_f', 'new_m_conv_w', 'new_m_conv_b', 'new_m_ln_g', 'new_m_ln_b', 'new_m_w_out', 'new_m_g_x', 'new_m_g_mem', 'new_m_w_mq', 'new_m_w_mkv', 'new_m_w_mo', 'new_m_g_ffn', 'new_m_w_gu', 'new_m_w_down', 'new_m_g_final', 'new_v_g_mix', 'new_v_w_in', 'new_v_b_f', 'new_v_conv_w', 'new_v_conv_b', 'new_v_ln_g', 'new_v_ln_b', 'new_v_w_out', 'new_v_g_x', 'new_v_g_mem', 'new_v_w_mq', 'new_v_w_mkv', 'new_v_w_mo', 'new_v_g_ffn', 'new_v_w_gu', 'new_v_w_down', 'new_v_g_final']
TWIN_LEAF_KINDS = {'loss': 'loss', 'grad_x': 'grad_x', 'grad_g_mix': 'grad_w', 'grad_w_in': 'grad_w', 'grad_b_f': 'grad_w', 'grad_conv_w': 'grad_w', 'grad_conv_b': 'grad_w', 'grad_ln_g': 'grad_w', 'grad_ln_b': 'grad_w', 'grad_w_out': 'grad_w', 'grad_g_x': 'grad_w', 'grad_g_mem': 'grad_w', 'grad_w_mq': 'grad_w', 'grad_w_mkv': 'grad_w', 'grad_w_mo': 'grad_w', 'grad_g_ffn': 'grad_w', 'grad_w_gu': 'grad_w', 'grad_w_down': 'grad_w', 'grad_g_final': 'grad_w', 'delta_g_mix': 'delta_w', 'delta_w_in': 'delta_w', 'delta_b_f': 'delta_w', 'delta_conv_w': 'delta_w', 'delta_conv_b': 'delta_w', 'delta_ln_g': 'delta_w', 'delta_ln_b': 'delta_w', 'delta_w_out': 'delta_w', 'delta_g_x': 'delta_w', 'delta_g_mem': 'delta_w', 'delta_w_mq': 'delta_w', 'delta_w_mkv': 'delta_w', 'delta_w_mo': 'delta_w', 'delta_g_ffn': 'delta_w', 'delta_w_gu': 'delta_w', 'delta_w_down': 'delta_w', 'delta_g_final': 'delta_w', 'new_m_g_mix': 'new_m', 'new_m_w_in': 'new_m', 'new_m_b_f': 'new_m', 'new_m_conv_w': 'new_m', 'new_m_conv_b': 'new_m', 'new_m_ln_g': 'new_m', 'new_m_ln_b': 'new_m', 'new_m_w_out': 'new_m', 'new_m_g_x': 'new_m', 'new_m_g_mem': 'new_m', 'new_m_w_mq': 'new_m', 'new_m_w_mkv': 'new_m', 'new_m_w_mo': 'new_m', 'new_m_g_ffn': 'new_m', 'new_m_w_gu': 'new_m', 'new_m_w_down': 'new_m', 'new_m_g_final': 'new_m', 'new_v_g_mix': 'new_v', 'new_v_w_in': 'new_v', 'new_v_b_f': 'new_v', 'new_v_conv_w': 'new_v', 'new_v_conv_b': 'new_v', 'new_v_ln_g': 'new_v', 'new_v_ln_b': 'new_v', 'new_v_w_out': 'new_v', 'new_v_g_x': 'new_v', 'new_v_g_mem': 'new_v', 'new_v_w_mq': 'new_v', 'new_v_w_mkv': 'new_v', 'new_v_w_mo': 'new_v', 'new_v_g_ffn': 'new_v', 'new_v_w_gu': 'new_v', 'new_v_w_down': 'new_v', 'new_v_g_final': 'new_v'}


def _forward(args):
    return _fwd_reference(*[args[k] for k in FWD_PARAMS])


def _output_shape():
    out = _jax.eval_shape(lambda: _forward(_fwd_setup_inputs(0)))
    return out.shape, out.dtype

N_MICROBATCH = 1
ADAM_LR = 0.001
ADAM_B1 = 0.9
ADAM_B2 = 0.999
ADAM_EPS = 1e-08
ADAM_WD = 0.01
ADAM_STEP = 10
PER_EXAMPLE_BATCH_AXIS = {'x': 0, 'mem': 0, 'loss_target': 0}
SHARED_INPUTS = []
_WEIGHT_DTYPES = {'g_mix': _jnp.float32, 'w_in': _jnp.float32, 'b_f': _jnp.float32, 'conv_w': _jnp.float32, 'conv_b': _jnp.float32, 'ln_g': _jnp.float32, 'ln_b': _jnp.float32, 'w_out': _jnp.float32, 'g_x': _jnp.float32, 'g_mem': _jnp.float32, 'w_mq': _jnp.float32, 'w_mkv': _jnp.float32, 'w_mo': _jnp.float32, 'g_ffn': _jnp.float32, 'w_gu': _jnp.float32, 'w_down': _jnp.float32, 'g_final': _jnp.float32}
MOMENT_SCALE = {'g_mix': 1.121417e-01, 'w_in': 7.129382e-02, 'b_f': 3.988392e-01, 'conv_w': 1.095560e-01, 'conv_b': 2.232427e-01, 'ln_g': 1.296340e-01, 'ln_b': 1.075607e-01, 'w_out': 9.123175e-02, 'g_x': 1.779143e-02, 'g_mem': 2.675411e-02, 'w_mq': 1.825998e-02, 'w_mkv': 1.861103e-02, 'w_mo': 1.858433e-02, 'g_ffn': 1.231162e-01, 'w_gu': 5.065154e-02, 'w_down': 8.260648e-02, 'g_final': 3.198092e+01}


def _to_microbatches(a, axis):
    t = _jnp.moveaxis(a, axis, 0)
    t = t.reshape((N_MICROBATCH, t.shape[0] // N_MICROBATCH) + t.shape[1:])
    return _jnp.moveaxis(t, 1, axis + 1)


def setup_inputs(seed: int = 0) -> dict:
    inp = _fwd_setup_inputs(seed)
    key = _jax.random.fold_in(_jax.random.key(seed), 7919)
    shape, _ = _output_shape()
    out = dict(inp)
    out["loss_target"] = _jax.random.normal(_jax.random.fold_in(key, 0), shape, _jnp.float32)
    for i, name in enumerate(TWIN_WEIGHTS):
        w = inp[name].astype(_jnp.float32)
        if MOMENT_SCALE is None:
            s = _jnp.sqrt(_jnp.mean(_jnp.square(w)) + 1e-30)
        else:
            s = MOMENT_SCALE[name]
        km, kv = _jax.random.split(_jax.random.fold_in(key, i + 1))
        out[name] = w
        out["m_" + name] = s * _jax.random.normal(km, w.shape, _jnp.float32)
        out["v_" + name] = (s * s) * _jax.random.uniform(kv, w.shape, _jnp.float32, 0.5, 1.5)
    if N_MICROBATCH > 1:
        for name, axis in PER_EXAMPLE_BATCH_AXIS.items():
            out[name] = _to_microbatches(out[name], axis)
    return {'x': out['x'], 'mem': out['mem'], 'g_mix': out['g_mix'], 'w_in': out['w_in'], 'b_f': out['b_f'], 'conv_w': out['conv_w'], 'conv_b': out['conv_b'], 'ln_g': out['ln_g'], 'ln_b': out['ln_b'], 'w_out': out['w_out'], 'g_x': out['g_x'], 'g_mem': out['g_mem'], 'w_mq': out['w_mq'], 'w_mkv': out['w_mkv'], 'w_mo': out['w_mo'], 'g_ffn': out['g_ffn'], 'w_gu': out['w_gu'], 'w_down': out['w_down'], 'g_final': out['g_final'], 'loss_target': out['loss_target'], 'm_g_mix': out['m_g_mix'], 'm_w_in': out['m_w_in'], 'm_b_f': out['m_b_f'], 'm_conv_w': out['m_conv_w'], 'm_conv_b': out['m_conv_b'], 'm_ln_g': out['m_ln_g'], 'm_ln_b': out['m_ln_b'], 'm_w_out': out['m_w_out'], 'm_g_x': out['m_g_x'], 'm_g_mem': out['m_g_mem'], 'm_w_mq': out['m_w_mq'], 'm_w_mkv': out['m_w_mkv'], 'm_w_mo': out['m_w_mo'], 'm_g_ffn': out['m_g_ffn'], 'm_w_gu': out['m_w_gu'], 'm_w_down': out['m_w_down'], 'm_g_final': out['m_g_final'], 'v_g_mix': out['v_g_mix'], 'v_w_in': out['v_w_in'], 'v_b_f': out['v_b_f'], 'v_conv_w': out['v_conv_w'], 'v_conv_b': out['v_conv_b'], 'v_ln_g': out['v_ln_g'], 'v_ln_b': out['v_ln_b'], 'v_w_out': out['v_w_out'], 'v_g_x': out['v_g_x'], 'v_g_mem': out['v_g_mem'], 'v_w_mq': out['v_w_mq'], 'v_w_mkv': out['v_w_mkv'], 'v_w_mo': out['v_w_mo'], 'v_g_ffn': out['v_g_ffn'], 'v_w_gu': out['v_w_gu'], 'v_w_down': out['v_w_down'], 'v_g_final': out['v_g_final']}


def _loss(weights, diff, rest, loss_target):
    with _jax.named_scope("forward"):
        args = {**rest, TWIN_DIFF_INPUT: diff, **{k: w.astype(_WEIGHT_DTYPES[k]) for k, w in weights.items()}}
        y = _forward(args)
    with _jax.named_scope("loss_head"):
        err = _jnp.square(y.astype(_jnp.float32) - loss_target)
        return 0.5 * _jnp.sum(_jnp.mean(err, axis=-1)) if err.ndim else 0.5 * err


def _adamw(w, g, m, v):
    m = ADAM_B1 * m + (1.0 - ADAM_B1) * g
    v = ADAM_B2 * v + (1.0 - ADAM_B2) * _jnp.square(g)
    m_hat = m / (1.0 - ADAM_B1 ** ADAM_STEP)
    v_hat = v / (1.0 - ADAM_B2 ** ADAM_STEP)
    delta = -ADAM_LR * (m_hat / (_jnp.sqrt(v_hat) + ADAM_EPS) + ADAM_WD * w)
    return delta, m, v


def reference(x, mem, g_mix, w_in, b_f, conv_w, conv_b, ln_g, ln_b, w_out, g_x, g_mem, w_mq, w_mkv, w_mo, g_ffn, w_gu, w_down, g_final, loss_target, m_g_mix, m_w_in, m_b_f, m_conv_w, m_conv_b, m_ln_g, m_ln_b, m_w_out, m_g_x, m_g_mem, m_w_mq, m_w_mkv, m_w_mo, m_g_ffn, m_w_gu, m_w_down, m_g_final, v_g_mix, v_w_in, v_b_f, v_conv_w, v_conv_b, v_ln_g, v_ln_b, v_w_out, v_g_x, v_g_mem, v_w_mq, v_w_mkv, v_w_mo, v_g_ffn, v_w_gu, v_w_down, v_g_final):
    given = dict(x=x, mem=mem, g_mix=g_mix, w_in=w_in, b_f=b_f, conv_w=conv_w, conv_b=conv_b, ln_g=ln_g, ln_b=ln_b, w_out=w_out, g_x=g_x, g_mem=g_mem, w_mq=w_mq, w_mkv=w_mkv, w_mo=w_mo, g_ffn=g_ffn, w_gu=w_gu, w_down=w_down, g_final=g_final, loss_target=loss_target, m_g_mix=m_g_mix, m_w_in=m_w_in, m_b_f=m_b_f, m_conv_w=m_conv_w, m_conv_b=m_conv_b, m_ln_g=m_ln_g, m_ln_b=m_ln_b, m_w_out=m_w_out, m_g_x=m_g_x, m_g_mem=m_g_mem, m_w_mq=m_w_mq, m_w_mkv=m_w_mkv, m_w_mo=m_w_mo, m_g_ffn=m_g_ffn, m_w_gu=m_w_gu, m_w_down=m_w_down, m_g_final=m_g_final, v_g_mix=v_g_mix, v_w_in=v_w_in, v_b_f=v_b_f, v_conv_w=v_conv_w, v_conv_b=v_conv_b, v_ln_g=v_ln_g, v_ln_b=v_ln_b, v_w_out=v_w_out, v_g_x=v_g_x, v_g_mem=v_g_mem, v_w_mq=v_w_mq, v_w_mkv=v_w_mkv, v_w_mo=v_w_mo, v_g_ffn=v_g_ffn, v_w_gu=v_w_gu, v_w_down=v_w_down, v_g_final=v_g_final)
    weights = {n: given[n] for n in TWIN_WEIGHTS}
    shared = {n: given[n] for n in SHARED_INPUTS}
    per_example = {n: given[n] for n in ['x', 'mem']}
    grad_fn = _jax.value_and_grad(_loss, argnums=(0, 1))

    def one_microbatch(ex, loss_target):
        ex = dict(ex)
        diff = ex.pop(TWIN_DIFF_INPUT)
        return grad_fn(weights, diff, {**shared, **ex}, loss_target)

    if N_MICROBATCH == 1:
        loss, (grad_w, grad_x) = one_microbatch(per_example, given["loss_target"])
    else:
        def body(carry, xs):
            loss_sum, grad_sum = carry
            l_k, (gw_k, gx_k) = one_microbatch(xs[0], xs[1])
            with _jax.named_scope("update"):
                return (loss_sum + l_k, _jax.tree.map(_jnp.add, grad_sum, gw_k)), gx_k

        init = (_jnp.zeros((), _jnp.float32), _jax.tree.map(_jnp.zeros_like, weights))
        (loss, grad_w), grad_x = _jax.lax.scan(body, init, (per_example, given["loss_target"]))
    with _jax.named_scope("update"):
        delta_w, new_m, new_v = {}, {}, {}
        for n in TWIN_WEIGHTS:
            delta_w[n], new_m[n], new_v[n] = _adamw(weights[n], grad_w[n], given["m_" + n], given["v_" + n])
    return (loss, grad_x, *[grad_w[n] for n in TWIN_WEIGHTS], *[delta_w[n] for n in TWIN_WEIGHTS],
            *[new_m[n] for n in TWIN_WEIGHTS], *[new_v[n] for n in TWIN_WEIGHTS])
```

```python
import functools
import math

import jax
import jax.numpy as jnp
from jax import lax
from jax.experimental import pallas as pl
from jax.experimental.pallas import tpu as pltpu

F32, BF16 = jnp.float32, jnp.bfloat16
HIGHEST = lax.Precision.HIGHEST
MESH = pl.DeviceIdType.MESH

D = 1024
CONV_CH = 512
CONV_K = 31
CONV_HALO = 32
FOX_W = 512
HEAD_D = 64
N_PAIR = 4
MEM_LEN = 256
MEM_HEADS = 4
MEM_HD = 256
D_FF = 2816
FF_CHUNK = 1408
D_IN = 2568
D_IN_PAD = 2688
OFF_F = 2560
EPS = 1e-6
LANES = 128

ADAM_LR, ADAM_B1, ADAM_B2, ADAM_EPS, ADAM_WD, ADAM_STEP = 0.001, 0.9, 0.999, 1e-08, 0.01, 10

PACK_ROWS = 4096
HALF_ROWS = PACK_ROWS // 2
VMEM_LIMIT = 60 * 1024 * 1024

BIG = (("w_out", 256, 1024, False), ("w_mq", 256, 1024, False), ("w_mkv", 1024, 512, True),
       ("w_mo", 256, 1024, False), ("w_gu", 1024, 1408, True), ("w_down", 704, 1024, False),
       ("w_in", 1024, 642, True))

ANY = pl.BlockSpec(memory_space=pl.ANY)


def _sig(x):
    return 1.0 / (1.0 + jnp.exp(-x))


def _dot(a, b):
    return jnp.dot(a, b, preferred_element_type=F32)


def _dot_nt(a, b):
    return lax.dot_general(a, b, (((1,), (1,)), ((), ())), preferred_element_type=F32)


def _dot_tn(a, b):
    return lax.dot_general(a, b, (((0,), (0,)), ((), ())), preferred_element_type=F32)


def _dot_hi(a, b):
    return jnp.dot(a, b, precision=HIGHEST, preferred_element_type=F32)


def _resident(a):
    nd = a.ndim
    return pl.BlockSpec(a.shape, lambda *_: (0,) * nd, pipeline_mode=pl.Buffered(1))


def _acc_spec(shape):
    nd = len(shape)
    return pl.BlockSpec(shape, lambda *_: (0,) * nd)


def _params(n_grid):
    return pltpu.CompilerParams(dimension_semantics=("arbitrary",) * n_grid, vmem_limit_bytes=VMEM_LIMIT)


def _sds(shape, dtype):
    return jax.ShapeDtypeStruct(shape, dtype)


def _rms(x):
    r = lax.rsqrt(jnp.mean(x * x, axis=-1, keepdims=True) + EPS)
    return r, x * r


def _rms_bwd(dy, xh, r, g):
    dxh = dy * g
    dx = r * (dxh - xh * jnp.mean(dxh * xh, axis=-1, keepdims=True))
    return dx, dy * xh


def _head_expand(rows, cols):
    hd = lax.broadcasted_iota(jnp.int32, (rows, cols), 1) // HEAD_D
    hr = lax.broadcasted_iota(jnp.int32, (rows, cols), 0)
    return (hd == hr).astype(F32)


def _head_sum(n):
    hc = lax.broadcasted_iota(jnp.int32, (n, n), 1) // HEAD_D
    hr = lax.broadcasted_iota(jnp.int32, (n, n), 0) // HEAD_D
    return (hc == hr).astype(F32)


def _fwd_in(x2, g_mix, w_in, bf_pad, B, S):
    T = B * S
    TB = min(512, S)
    nb = S // TB

    def body(x_ref, g_ref, w_ref, bf_ref, h_ref, u_ref, gt_ref, q_ref, k_ref, v_ref, zf_ref, c_ref, cq_ref,
             carry):
        j = pl.program_id(1)

        @pl.when(j == 0)
        def _():
            carry[...] = jnp.zeros_like(carry)

        _, xh = _rms(x_ref[...])
        h = (xh * g_ref[...]).astype(BF16)
        h_ref[...] = h
        u_ref[...] = _dot(h, w_ref[:, 0:512])
        gt_ref[...] = _dot(h, w_ref[:, 512:1024])
        q_ref[...] = _dot(h, w_ref[:, 1024:1536]).astype(BF16)
        k_ref[...] = _dot(h, w_ref[:, 1536:2048]).astype(BF16)
        v_ref[...] = _dot(h, w_ref[:, 2048:2560]).astype(BF16)
        zf = _dot(h, w_ref[:, OFF_F:D_IN_PAD]) + bf_ref[...]
        zf_ref[...] = zf
        lane = lax.broadcasted_iota(jnp.int32, zf.shape, 1)
        logf = jnp.where(lane < 8, jnp.minimum(zf, 0.0) - jnp.log(1.0 + jnp.exp(-jnp.abs(zf))), 0.0)
        row = lax.broadcasted_iota(jnp.int32, (TB, TB), 0)
        col = lax.broadcasted_iota(jnp.int32, (TB, TB), 1)
        c = _dot_hi((row >= col).astype(F32), logf) + carry[0:1, :]
        carry[0:1, :] = c[TB - 1:TB, :]
        c_ref[...] = c
        cq_ref[...] = _dot_hi(c, _head_expand(LANES, FOX_W))

    tok = lambda w: pl.BlockSpec((TB, w), lambda b, j: (b * nb + j, 0))
    outs = [(D, BF16), (512, F32), (512, F32), (512, BF16), (512, BF16), (512, BF16), (LANES, F32),
            (LANES, F32), (FOX_W, F32)]
    return pl.pallas_call(
        body, name="fwd_in", grid=(B, nb),
        in_specs=[tok(D), _resident(g_mix), _resident(w_in), _resident(bf_pad)],
        out_specs=[tok(w) for w, _ in outs],
        out_shape=[_sds((T, w), dt) for w, dt in outs],
        scratch_shapes=[pltpu.VMEM((8, LANES), F32)],
        compiler_params=_params(2),
    )(x2, g_mix, w_in, bf_pad)


def _layernorm_silu(y, lg, lb):
    mu = jnp.mean(y, axis=-1, keepdims=True)
    yc = y - mu
    rs = lax.rsqrt(jnp.mean(yc * yc, axis=-1, keepdims=True) + EPS)
    n = yc * rs
    l = n * lg + lb
    return rs, n, l


def _conv_fwd(u, gt, cw, cb, lng, lnb, B, S):
    T = B * S
    CB = min(256, S)
    nb = S // CB

    def body(u_ref, gt_ref, w_ref, cb_ref, lg_ref, lb_ref, y_ref, co_ref, acat):
        j = pl.program_id(1)

        @pl.when(j == 0)
        def _():
            acat[0:CONV_HALO, :] = jnp.zeros((CONV_HALO, CONV_CH), F32)

        acat[CONV_HALO:CONV_HALO + CB, :] = u_ref[...] * _sig(gt_ref[...])
        acc = jnp.zeros((CB, CONV_CH), F32) + cb_ref[...]
        for k in range(CONV_K):
            off = CONV_HALO - (CONV_K - 1) + k
            acc = acc + w_ref[k:k + 1, :] * acat[off:off + CB, :]
        acat[0:CONV_HALO, :] = acat[CB:CB + CONV_HALO, :]
        y_ref[...] = acc
        _, _, l = _layernorm_silu(acc, lg_ref[...], lb_ref[...])
        co_ref[...] = (l * _sig(l)).astype(BF16)

    tok = lambda w: pl.BlockSpec((CB, w), lambda b, j: (b * nb + j, 0))
    return pl.pallas_call(
        body, name="conv_fwd", grid=(B, nb),
        in_specs=[tok(CONV_CH), tok(CONV_CH), _resident(cw), _resident(cb), _resident(lng), _resident(lnb)],
        out_specs=[tok(CONV_CH), tok(CONV_CH)],
        out_shape=[_sds((T, CONV_CH), F32), _sds((T, CONV_CH), BF16)],
        scratch_shapes=[pltpu.VMEM((CONV_HALO + CB, CONV_CH), F32)],
        compiler_params=_params(2),
    )(u, gt, cw, cb, lng, lnb)


def _fox_fwd(q, k, v, cq, ckT, B, S):
    T = B * S
    TQ = min(256, S)
    nq = S // TQ
    scale = 1.0 / math.sqrt(HEAD_D)

    def body(q_ref, k_ref, v_ref, cq_ref, ck_ref, o_ref, lse_ref):
        i = pl.program_id(2)
        lane = lax.broadcasted_iota(jnp.int32, (TQ, LANES), 1)
        lo = lane < HEAD_D
        q2 = q_ref[...]
        zero = jnp.zeros_like(q2)
        qh = (jnp.where(lo, q2, zero), jnp.where(lo, zero, q2))
        cqv = cq_ref[...]
        cqh = (cqv[:, 0:1], cqv[:, HEAD_D:HEAD_D + 1])
        row = lax.broadcasted_iota(jnp.int32, (TQ, TQ), 0)
        col = lax.broadcasted_iota(jnp.int32, (TQ, TQ), 1)

        def step(j, carry):
            start = pl.multiple_of(j * TQ, TQ)
            kj = k_ref[pl.ds(start, TQ), :]
            vj = v_ref[pl.ds(start, TQ), :]
            valid = (j * TQ + col) <= (i * TQ + row)
            out = []
            for h in range(2):
                m, l, acc = carry[3 * h:3 * h + 3]
                s = _dot_nt(qh[h], kj) * scale + (cqh[h] - ck_ref[0, 0, h:h + 1, pl.ds(start, TQ)])
                s = jnp.where(valid, s, -1e30)
                m_new = jnp.maximum(m, jnp.max(s, axis=-1, keepdims=True))
                alpha = jnp.exp(m - m_new)
                p = jnp.exp(s - m_new)
                l = alpha * l + jnp.sum(p, axis=-1, keepdims=True)
                acc = alpha * acc + _dot(p.astype(BF16), vj)
                out += [m_new, l, acc]
            return tuple(out)

        init = (jnp.full((TQ, 1), -1e30, F32), jnp.zeros((TQ, 1), F32), jnp.zeros((TQ, LANES), F32)) * 2
        ma, la, acca, mb, lb, accb = lax.fori_loop(0, i + 1, step, init)
        o_ref[...] = jnp.where(lo, acca / la, accb / lb)
        lse_ref[...] = jnp.where(lo, ma + jnp.log(la), mb + jnp.log(lb))

    qspec = pl.BlockSpec((TQ, LANES), lambda b, p, i: (b * nq + i, p))
    kspec = pl.BlockSpec((S, LANES), lambda b, p, i: (b, p))
    return pl.pallas_call(
        body, name="fox_fwd", grid=(B, N_PAIR, nq),
        in_specs=[qspec, kspec, kspec, qspec, pl.BlockSpec((1, 1, 8, S), lambda b, p, i: (b, p, 0, 0))],
        out_specs=[qspec, qspec],
        out_shape=[_sds((T, FOX_W), F32), _sds((T, FOX_W), F32)],
        compiler_params=_params(3),
    )(q, k, v, cq, ckT)


def _mem_kv(mem2, g_mem, w_mkv, B):
    def body(m_ref, g_ref, w_ref, mn_ref, km_ref, vm_ref):
        _, xh = _rms(m_ref[...])
        mn = (xh * g_ref[...]).astype(BF16)
        mn_ref[...] = mn
        km_ref[...] = _dot(mn, w_ref[:, 0:D]).astype(BF16)
        vm_ref[...] = _dot(mn, w_ref[:, D:2 * D]).astype(BF16)

    blk = pl.BlockSpec((MEM_LEN, D), lambda b: (b, 0))
    return pl.pallas_call(
        body, name="mem_kv", grid=(B,),
        in_specs=[blk, _resident(g_mem), _resident(w_mkv)],
        out_specs=[blk, blk, blk],
        out_shape=[_sds((B * MEM_LEN, D), BF16)] * 3,
        compiler_params=_params(1),
    )(mem2, g_mem, w_mkv)


def _mem_probs(qm, km):
    ps = []
    for h in range(MEM_HEADS):
        hs = slice(h * MEM_HD, (h + 1) * MEM_HD)
        lg = _dot_nt(qm[:, hs], km[:, hs]) * (1.0 / math.sqrt(MEM_HD))
        e = jnp.exp(lg - jnp.max(lg, axis=-1, keepdims=True))
        ps.append(e / jnp.sum(e, axis=-1, keepdims=True))
    return ps


def _fwd_mid(x2, co, o, km, vm, w_out, w_mq, w_mo, g_x, B, S):
    T = B * S
    TB = min(512, S)
    nb = S // TB

    def body(x_ref, co_ref, o_ref, km_ref, vm_ref, wo_ref, wq_ref, wm_ref, g_ref,
             x1_ref, hx_ref, qm_ref, om_ref, x2_ref):
        x1 = x_ref[...] + _dot(co_ref[...], wo_ref[0:CONV_CH, :]) + _dot(o_ref[...].astype(BF16), wo_ref[CONV_CH:D, :])
        x1_ref[...] = x1
        _, xh = _rms(x1)
        hx = (xh * g_ref[...]).astype(BF16)
        hx_ref[...] = hx
        qm = _dot(hx, wq_ref[...]).astype(BF16)
        qm_ref[...] = qm
        ps = _mem_probs(qm, km_ref[...])
        vmv = vm_ref[...]
        for h in range(MEM_HEADS):
            hs = slice(h * MEM_HD, (h + 1) * MEM_HD)
            om_ref[:, hs] = _dot(ps[h].astype(BF16), vmv[:, hs]).astype(BF16)
        x2_ref[...] = x1 + _dot(om_ref[...], wm_ref[...])

    tok = lambda w: pl.BlockSpec((TB, w), lambda b, j: (b * nb + j, 0))
    memb = pl.BlockSpec((MEM_LEN, D), lambda b, j: (b, 0))
    outs = [(D, F32), (D, BF16), (D, BF16), (D, BF16), (D, F32)]
    return pl.pallas_call(
        body, name="fwd_mid", grid=(B, nb),
        in_specs=[tok(D), tok(CONV_CH), tok(FOX_W), memb, memb, _resident(w_out), _resident(w_mq), _resident(w_mo),
                  _resident(g_x)],
        out_specs=[tok(w) for w, _ in outs],
        out_shape=[_sds((T, w), dt) for w, dt in outs],
        compiler_params=_params(2),
    )(x2, co, o, km, vm, w_out, w_mq, w_mo, g_x)


def _fwd_ffn(x2, tgt, w_gu, w_down, g_ffn, g_final, T):
    TB = min(256, T)
    nb = T // TB

    def body(x_ref, t_ref, wgu_ref, wd_ref, gf_ref, gl_ref, hf_ref, gu_ref, act_ref, dx3_ref, loss_ref, dgl_ref):
        i = pl.program_id(0)

        @pl.when(i == 0)
        def _():
            loss_ref[...] = jnp.zeros_like(loss_ref)
            dgl_ref[...] = jnp.zeros_like(dgl_ref)

        x2v = x_ref[...]
        _, xh = _rms(x2v)
        hf = (xh * gf_ref[...]).astype(BF16)
        hf_ref[...] = hf
        x3 = x2v
        for ch in range(D_FF // FF_CHUNK):
            c0 = ch * FF_CHUNK
            g = _dot(hf, wgu_ref[:, c0:c0 + FF_CHUNK])
            u = _dot(hf, wgu_ref[:, D_FF + c0:D_FF + c0 + FF_CHUNK])
            gu_ref[:, c0:c0 + FF_CHUNK] = g
            gu_ref[:, D_FF + c0:D_FF + c0 + FF_CHUNK] = u
            act = (g * _sig(g) * u).astype(BF16)
            act_ref[:, c0:c0 + FF_CHUNK] = act
            x3 = x3 + _dot(act, wd_ref[c0:c0 + FF_CHUNK, :])
        r3, xh3 = _rms(x3)
        gl = gl_ref[...]
        e = xh3 * gl - t_ref[...]
        loss_ref[...] += jnp.sum(e * e) * (0.5 / D)
        dy = e * (1.0 / D)
        dx3, dgl = _rms_bwd(dy, xh3, r3, gl)
        dx3_ref[...] = dx3
        dgl_ref[...] += jnp.sum(dgl, axis=0, keepdims=True)

    tok = lambda w: pl.BlockSpec((TB, w), lambda i: (i, 0))
    return pl.pallas_call(
        body, name="fwd_ffn", grid=(nb,),
        in_specs=[tok(D), tok(D), _resident(w_gu), _resident(w_down), _resident(g_ffn), _resident(g_final)],
        out_specs=[tok(D), tok(2 * D_FF), tok(D_FF), tok(D), _acc_spec((1, LANES)), _acc_spec((1, D))],
        out_shape=[_sds((T, D), BF16), _sds((T, 2 * D_FF), F32), _sds((T, D_FF), BF16), _sds((T, D), F32),
                   _sds((1, LANES), F32), _sds((1, D), F32)],
        compiler_params=_params(1),
    )(x2, tgt, w_gu, w_down, g_ffn, g_final)


def _bwd_ffn(dx3, gu, x2, w_gu, w_down, g_ffn, T):
    TB = min(256, T)
    nb = T // TB

    def body(d_ref, gu_ref, x_ref, wgu_ref, wd_ref, gf_ref, dgu_ref, dx2_ref, dgf_ref):
        i = pl.program_id(0)

        @pl.when(i == 0)
        def _():
            dgf_ref[...] = jnp.zeros_like(dgf_ref)

        dx3v = d_ref[...]
        db = dx3v.astype(BF16)
        dhf = jnp.zeros((TB, D), F32)
        for ch in range(D_FF // FF_CHUNK):
            c0 = ch * FF_CHUNK
            dact = _dot_nt(db, wd_ref[c0:c0 + FF_CHUNK, :])
            g = gu_ref[:, c0:c0 + FF_CHUNK]
            u = gu_ref[:, D_FF + c0:D_FF + c0 + FF_CHUNK]
            sg = _sig(g)
            dg = (dact * u * sg * (1.0 + g * (1.0 - sg))).astype(BF16)
            du = (dact * g * sg).astype(BF16)
            dgu_ref[:, c0:c0 + FF_CHUNK] = dg
            dgu_ref[:, D_FF + c0:D_FF + c0 + FF_CHUNK] = du
            dhf = dhf + _dot_nt(dg, wgu_ref[:, c0:c0 + FF_CHUNK]) + _dot_nt(du, wgu_ref[:, D_FF + c0:D_FF + c0 + FF_CHUNK])
        r2, xh2 = _rms(x_ref[...])
        dx, dg_tok = _rms_bwd(dhf, xh2, r2, gf_ref[...])
        dx2_ref[...] = dx3v + dx
        dgf_ref[...] += jnp.sum(dg_tok, axis=0, keepdims=True)

    tok = lambda w: pl.BlockSpec((TB, w), lambda i: (i, 0))
    return pl.pallas_call(
        body, name="bwd_ffn", grid=(nb,),
        in_specs=[tok(D), tok(2 * D_FF), tok(D), _resident(w_gu), _resident(w_down), _resident(g_ffn)],
        out_specs=[tok(2 * D_FF), tok(D), _acc_spec((1, D))],
        out_shape=[_sds((T, 2 * D_FF), BF16), _sds((T, D), F32), _sds((1, D), F32)],
        compiler_params=_params(1),
    )(dx3, gu, x2, w_gu, w_down, g_ffn)


def _bwd_mid(dx2, x1, qm, km, vm, o, w_mo, w_mq, w_out, g_x, B, S):
    T = B * S
    TB = min(512, S)
    nb = S // TB
    inv = 1.0 / math.sqrt(MEM_HD)

    def body(d_ref, x1_ref, qm_ref, km_ref, vm_ref, o_ref, wm_ref, wq_ref, wo_ref, g_ref,
             dx1_ref, dqm_ref, dco_ref, do_ref, dd_ref, dkm_ref, dvm_ref, dgx_ref):
        b = pl.program_id(0)
        j = pl.program_id(1)

        @pl.when((b == 0) & (j == 0))
        def _():
            dgx_ref[...] = jnp.zeros_like(dgx_ref)

        @pl.when(j == 0)
        def _():
            dkm_ref[...] = jnp.zeros_like(dkm_ref)
            dvm_ref[...] = jnp.zeros_like(dvm_ref)

        dx2v = d_ref[...]
        dom = _dot_nt(dx2v.astype(BF16), wm_ref[...]).astype(BF16)
        qmv = qm_ref[...]
        kmv = km_ref[...]
        vmv = vm_ref[...]
        ps = _mem_probs(qmv, kmv)
        for h in range(MEM_HEADS):
            hs = slice(h * MEM_HD, (h + 1) * MEM_HD)
            p = ps[h]
            dp = _dot_nt(dom[:, hs], vmv[:, hs])
            ds = (p * (dp - jnp.sum(p * dp, axis=-1, keepdims=True))).astype(BF16)
            dqm_ref[:, hs] = (_dot(ds, kmv[:, hs]) * inv).astype(BF16)
            dkm_ref[:, hs] += _dot_tn(ds, qmv[:, hs]) * inv
            dvm_ref[:, hs] += _dot_tn(p.astype(BF16), dom[:, hs])
        dhx = _dot_nt(dqm_ref[...], wq_ref[...])
        r1, xh1 = _rms(x1_ref[...])
        dx, dg_tok = _rms_bwd(dhx, xh1, r1, g_ref[...])
        dx1 = dx2v + dx
        dx1_ref[...] = dx1
        dgx_ref[...] += jnp.sum(dg_tok, axis=0, keepdims=True)
        d1b = dx1.astype(BF16)
        dco_ref[...] = _dot_nt(d1b, wo_ref[0:CONV_CH, :])
        do = _dot_nt(d1b, wo_ref[CONV_CH:D, :])
        dob = do.astype(BF16)
        do_ref[...] = dob
        dd_ref[...] = _dot_hi(dob.astype(F32) * o_ref[...], _head_sum(FOX_W))

    tok = lambda w: pl.BlockSpec((TB, w), lambda b, j: (b * nb + j, 0))
    memb = pl.BlockSpec((MEM_LEN, D), lambda b, j: (b, 0))
    outs = [(D, F32), (D, BF16), (CONV_CH, F32), (FOX_W, BF16), (FOX_W, F32)]
    return pl.pallas_call(
        body, name="bwd_mid", grid=(B, nb),
        in_specs=[tok(D), tok(D), tok(D), memb, memb, tok(FOX_W), _resident(w_mo), _resident(w_mq), _resident(w_out),
                  _resident(g_x)],
        out_specs=[tok(w) for w, _ in outs] + [memb, memb, _acc_spec((1, D))],
        out_shape=[_sds((T, w), dt) for w, dt in outs] + [_sds((B * MEM_LEN, D), F32)] * 2 + [_sds((1, D), F32)],
        compiler_params=_params(2),
    )(dx2, x1, qm, km, vm, o, w_mo, w_mq, w_out, g_x)


def _mem_bwd(dkm, dvm, mem2, w_mkv, g_mem, B):
    def body(dk_ref, dv_ref, m_ref, w_ref, g_ref, dkv_ref, dg_ref):
        b = pl.program_id(0)

        @pl.when(b == 0)
        def _():
            dg_ref[...] = jnp.zeros_like(dg_ref)

        dk = dk_ref[...].astype(BF16)
        dv = dv_ref[...].astype(BF16)
        dkv_ref[:, 0:D] = dk
        dkv_ref[:, D:2 * D] = dv
        dmn = _dot_nt(dk, w_ref[:, 0:D]) + _dot_nt(dv, w_ref[:, D:2 * D])
        _, xh = _rms(m_ref[...])
        dg_ref[...] += jnp.sum(dmn * xh, axis=0, keepdims=True)

    blk = pl.BlockSpec((MEM_LEN, D), lambda b: (b, 0))
    return pl.pallas_call(
        body, name="mem_bwd", grid=(B,),
        in_specs=[blk, blk, blk, _resident(w_mkv), _resident(g_mem)],
        out_specs=[pl.BlockSpec((MEM_LEN, 2 * D), lambda b: (b, 0)), _acc_spec((1, D))],
        out_shape=[_sds((B * MEM_LEN, 2 * D), BF16), _sds((1, D), F32)],
        compiler_params=_params(1),
    )(dkm, dvm, mem2, w_mkv, g_mem)


def _fox_bwd(q, k, v, do, lse, dd, cq, ckT, B, S):
    T = B * S
    TK = min(256, S)
    nk = S // TK
    scale = 1.0 / math.sqrt(HEAD_D)

    def body(q_ref, k_ref, v_ref, do_ref, lse_ref, dd_ref, cq_ref, ck_ref, dq_ref, dk_ref, dv_ref, dc_ref, dcq_ref,
             dq_acc, dcq_acc):
        j = pl.program_id(2)

        @pl.when(j == 0)
        def _():
            dq_acc[...] = jnp.zeros_like(dq_acc)
            dcq_acc[...] = jnp.zeros_like(dcq_acc)

        lane = lax.broadcasted_iota(jnp.int32, (TK, LANES), 1)
        lo = lane < HEAD_D
        k2 = k_ref[...]
        v2 = v_ref[...]
        zero = jnp.zeros_like(k2)
        kh = (jnp.where(lo, k2, zero), jnp.where(lo, zero, k2))
        vh = (jnp.where(lo, v2, zero), jnp.where(lo, zero, v2))
        kstart = pl.multiple_of(j * TK, TK)
        ckh = tuple(ck_ref[0, 0, h:h + 1, pl.ds(kstart, TK)] for h in range(2))
        row = lax.broadcasted_iota(jnp.int32, (TK, TK), 0)
        col = lax.broadcasted_iota(jnp.int32, (TK, TK), 1)

        def step(i, carry):
            start = pl.multiple_of(i * TK, TK)
            qi = q_ref[pl.ds(start, TK), :]
            doi = do_ref[pl.ds(start, TK), :]
            lsei = lse_ref[pl.ds(start, TK), :]
            ddi = dd_ref[pl.ds(start, TK), :]
            cqi = cq_ref[pl.ds(start, TK), :]
            valid = (j * TK + col) <= (i * TK + row)
            out = []
            dq_i = []
            rs_i = []
            for h in range(2):
                dk_a, dv_a, dc_a = carry[3 * h:3 * h + 3]
                hc = slice(h * HEAD_D, h * HEAD_D + 1)
                s = _dot_nt(qi, kh[h]) * scale + (cqi[:, hc] - ckh[h])
                p = jnp.where(valid, jnp.exp(s - lsei[:, hc]), 0.0)
                dp = _dot_nt(doi, vh[h])
                ds = p * (dp - ddi[:, hc])
                dc_a = dc_a + jnp.sum(ds, axis=0, keepdims=True)
                rs_i.append(jnp.sum(ds, axis=1, keepdims=True))
                ds = ds.astype(BF16)
                dv_a = dv_a + _dot_tn(p.astype(BF16), doi)
                dk_a = dk_a + _dot_tn(ds, qi)
                dq_i.append(_dot(ds, kh[h]))
                out += [dk_a, dv_a, dc_a]
            dq_acc[pl.ds(start, TK), :] += jnp.where(lo, dq_i[0], dq_i[1]) * scale
            dcq_acc[pl.ds(start, TK), :] += jnp.where(lo, rs_i[0], rs_i[1])
            return tuple(out)

        init = (jnp.zeros((TK, LANES), F32), jnp.zeros((TK, LANES), F32), jnp.zeros((1, TK), F32)) * 2
        dka, dva, dca, dkb, dvb, dcb = lax.fori_loop(j, nk, step, init)
        dk_ref[...] = (jnp.where(lo, dka, dkb) * scale).astype(BF16)
        dv_ref[...] = jnp.where(lo, dva, dvb).astype(BF16)
        sub = lax.broadcasted_iota(jnp.int32, (8, TK), 0)
        dc_ref[0, 0] = jnp.where(sub == 0, -dca, jnp.where(sub == 1, -dcb, 0.0))

        @pl.when(j == nk - 1)
        def _():
            dq_ref[...] = dq_acc[...].astype(BF16)
            dcq_ref[...] = dcq_acc[...]

    full = pl.BlockSpec((S, LANES), lambda b, p, j: (b, p))
    blk = pl.BlockSpec((TK, LANES), lambda b, p, j: (b * nk + j, p))
    return pl.pallas_call(
        body, name="fox_bwd", grid=(B, N_PAIR, nk),
        in_specs=[full, blk, blk, full, full, full, full, pl.BlockSpec((1, 1, 8, S), lambda b, p, j: (b, p, 0, 0))],
        out_specs=[full, blk, blk, pl.BlockSpec((1, 1, 8, TK), lambda b, p, j: (b, p, 0, j)), full],
        out_shape=[_sds((T, FOX_W), BF16), _sds((T, FOX_W), BF16), _sds((T, FOX_W), BF16),
                   _sds((B, N_PAIR, 8, S), F32), _sds((T, FOX_W), F32)],
        scratch_shapes=[pltpu.VMEM((S, LANES), F32), pltpu.VMEM((S, LANES), F32)],
        compiler_params=_params(3),
    )(q, k, v, do, lse, dd, cq, ckT)


def _fgate_bwd(dc8, zf, B, S):
    T = B * S
    TB = min(512, S)
    nb = S // TB

    def body(dc_ref, zf_ref, dzf_ref, dbf_ref, carry):
        b = pl.program_id(0)
        j = pl.program_id(1)

        @pl.when((b == 0) & (j == 0))
        def _():
            dbf_ref[...] = jnp.zeros_like(dbf_ref)

        @pl.when(j == 0)
        def _():
            carry[...] = jnp.zeros_like(carry)

        dc = dc_ref[...]
        row = lax.broadcasted_iota(jnp.int32, (TB, TB), 0)
        col = lax.broadcasted_iota(jnp.int32, (TB, TB), 1)
        dlogf = _dot_hi((col >= row).astype(F32), dc) + carry[0:1, :]
        carry[0:1, :] = dlogf[0:1, :]
        lane = lax.broadcasted_iota(jnp.int32, dc.shape, 1)
        dzf = jnp.where(lane < 8, dlogf * _sig(-zf_ref[...]), 0.0)
        dzf_ref[...] = dzf.astype(BF16)
        dbf_ref[...] += jnp.sum(dzf, axis=0, keepdims=True)

    tok = pl.BlockSpec((TB, LANES), lambda b, j: (b * nb + (nb - 1 - j), 0))
    return pl.pallas_call(
        body, name="fgate_bwd", grid=(B, nb),
        in_specs=[tok, tok],
        out_specs=[tok, _acc_spec((1, LANES))],
        out_shape=[_sds((T, LANES), BF16), _sds((1, LANES), F32)],
        scratch_shapes=[pltpu.VMEM((8, LANES), F32)],
        compiler_params=_params(2),
    )(dc8, zf)


def _conv_bwd(dco, y, u, gt, cw, lng, lnb, B, S):
    T = B * S
    CB = min(256, S)
    nb = S // CB
    hb = CB // CONV_HALO

    def body(dco_ref, y_ref, u_ref, gt_ref, up_ref, gp_ref, w_ref, lg_ref, lb_ref,
             du_ref, dgt_ref, dw_ref, vec_ref, acat, dycat):
        b = pl.program_id(0)
        j = pl.program_id(1)
        jr = nb - 1 - j

        @pl.when((b == 0) & (j == 0))
        def _():
            dw_ref[...] = jnp.zeros_like(dw_ref)
            vec_ref[...] = jnp.zeros_like(vec_ref)

        @pl.when(j == 0)
        def _():
            dycat[CB:CB + CONV_HALO, :] = jnp.zeros((CONV_HALO, CONV_CH), F32)

        lg = lg_ref[...]
        rs, n, l = _layernorm_silu(y_ref[...], lg, lb_ref[...])
        sg = _sig(l)
        dl = dco_ref[...] * (sg * (1.0 + l * (1.0 - sg)))
        dn = dl * lg
        dy = rs * (dn - jnp.mean(dn, axis=-1, keepdims=True) - n * jnp.mean(dn * n, axis=-1, keepdims=True))
        vec_ref[0:1, :] += jnp.sum(dy, axis=0, keepdims=True)
        vec_ref[1:2, :] += jnp.sum(dl * n, axis=0, keepdims=True)
        vec_ref[2:3, :] += jnp.sum(dl, axis=0, keepdims=True)
        dycat[0:CB, :] = dy
        uv = u_ref[...]
        sgt = _sig(gt_ref[...])
        acat[0:CONV_HALO, :] = jnp.where(jr > 0, up_ref[...] * _sig(gp_ref[...]), 0.0)
        acat[CONV_HALO:CONV_HALO + CB, :] = uv * sgt
        da = jnp.zeros((CB, CONV_CH), F32)
        for k in range(CONV_K):
            off = CONV_HALO - (CONV_K - 1) + k
            da = da + w_ref[k:k + 1, :] * dycat[CONV_K - 1 - k:CONV_K - 1 - k + CB, :]
            dw_ref[k:k + 1, :] += jnp.sum(dy * acat[off:off + CB, :], axis=0, keepdims=True)
        dycat[CB:CB + CONV_HALO, :] = dycat[0:CONV_HALO, :]
        du_ref[...] = (da * sgt).astype(BF16)
        dgt_ref[...] = (da * uv * sgt * (1.0 - sgt)).astype(BF16)

    tok = lambda w: pl.BlockSpec((CB, w), lambda b, j: (b * nb + (nb - 1 - j), 0))
    prev = pl.BlockSpec((CONV_HALO, CONV_CH), lambda b, j: (jnp.maximum((b * nb + (nb - 1 - j)) * hb - 1, 0), 0))
    return pl.pallas_call(
        body, name="conv_bwd", grid=(B, nb),
        in_specs=[tok(CONV_CH), tok(CONV_CH), tok(CONV_CH), tok(CONV_CH), prev, prev, _resident(cw), _resident(lng),
                  _resident(lnb)],
        out_specs=[tok(CONV_CH), tok(CONV_CH), _acc_spec((CONV_HALO, CONV_CH)), _acc_spec((8, CONV_CH))],
        out_shape=[_sds((T, CONV_CH), BF16), _sds((T, CONV_CH), BF16), _sds((CONV_HALO, CONV_CH), F32),
                   _sds((8, CONV_CH), F32)],
        scratch_shapes=[pltpu.VMEM((CONV_HALO + CB, CONV_CH), F32), pltpu.VMEM((CB + CONV_HALO, CONV_CH), F32)],
        compiler_params=_params(2),
    )(dco, y, u, gt, u, gt, cw, lng, lnb)


def _bwd_in(dz, w_in, x2, dx1, g_mix, T):
    TB = min(512, T)
    nb = T // TB

    def body(dz_ref, w_ref, x_ref, d1_ref, g_ref, gx_ref, dg_ref):
        i = pl.program_id(0)

        @pl.when(i == 0)
        def _():
            dg_ref[...] = jnp.zeros_like(dg_ref)

        dh = _dot_nt(dz_ref[...], w_ref[...])
        r0, xh0 = _rms(x_ref[...])
        dx, dg_tok = _rms_bwd(dh, xh0, r0, g_ref[...])
        gx_ref[...] = d1_ref[...] + dx
        dg_ref[...] += jnp.sum(dg_tok, axis=0, keepdims=True)

    tok = lambda w: pl.BlockSpec((TB, w), lambda i: (i, 0))
    return pl.pallas_call(
        body, name="bwd_in", grid=(nb,),
        in_specs=[tok(D_IN_PAD), _resident(w_in), tok(D), tok(D), _resident(g_mix)],
        out_specs=[tok(D), _acc_spec((1, D))],
        out_shape=[_sds((T, D), F32), _sds((1, D), F32)],
        compiler_params=_params(1),
    )(dz, w_in, x2, dx1, g_mix)


def _dw(a, b, name, tn):
    T, K = a.shape
    N = b.shape[1]
    tk = K if K <= 1024 else K // 2
    tt = min(512, T)
    nt = T // tt

    def body(a_ref, b_ref, o_ref, acc):
        t = pl.program_id(2)

        @pl.when(t == 0)
        def _():
            acc[...] = jnp.zeros_like(acc)

        acc[...] += _dot_tn(a_ref[...].astype(BF16), b_ref[...].astype(BF16))

        @pl.when(t == nt - 1)
        def _():
            o_ref[...] = acc[...]

    return pl.pallas_call(
        body, name=name, grid=(K // tk, N // tn, nt),
        in_specs=[pl.BlockSpec((tt, tk), lambda i, j, t: (t, i)), pl.BlockSpec((tt, tn), lambda i, j, t: (t, j))],
        out_specs=pl.BlockSpec((tk, tn), lambda i, j, t: (i, j)),
        out_shape=_sds((K, N), F32),
        scratch_shapes=[pltpu.VMEM((tk, tn), F32)],
        compiler_params=_params(3),
    )(a, b)


def _pos():
    return lax.axis_index("x"), lax.axis_index("y"), lax.axis_index("c")


def _remote(src, dst, ssem, rsem, to):
    return pltpu.make_async_remote_copy(src_ref=src, dst_ref=dst, send_sem=ssem, recv_sem=rsem, device_id=to,
                                        device_id_type=MESH)


def _allgather_packed(wp):
    H = HALF_ROWS

    def body(x_ref, out_ref, send_sems, recv_sems, local_sem):
        x, y, c = _pos()
        me, sib = (x, y, c), (x, y, 1 - c)
        chips = [(1 - x, y), (x, 1 - y), (1 - x, 1 - y)]

        def rows(px, py, pc):
            return out_ref.at[pl.ds(pl.multiple_of((4 * px + 2 * py + pc) * H, H), H), :]

        mine_src = x_ref.at[pl.ds(pl.multiple_of(c * H, H), H), :]

        def copy(k, block, to, src=None):
            return _remote(rows(*block) if src is None else src, rows(*block), send_sems.at[k], recv_sems.at[k], to)

        mine = pltpu.make_async_copy(mine_src, rows(*me), local_sem)
        mine.start()
        first = [copy(0, me, sib, src=mine_src)]
        first += [copy(1 + j, me, (*chip, c), src=mine_src) for j, chip in enumerate(chips)]
        for cp in first:
            cp.start()
        passed = [copy(4 + j, (*chip, c), sib) for j, chip in enumerate(chips)]
        for j, chip in enumerate(chips):
            copy(1 + j, (*chip, c), me).wait_recv()
            passed[j].start()
        copy(0, sib, me).wait_recv()
        for j, chip in enumerate(chips):
            copy(4 + j, (*chip, 1 - c), me).wait_recv()
        for cp in first + passed:
            cp.wait_send()
        mine.wait()

    return pl.pallas_call(
        body, name="ag_weights", out_shape=_sds((8 * H, D), wp.dtype), in_specs=[ANY], out_specs=ANY,
        scratch_shapes=[pltpu.SemaphoreType.DMA((7,)), pltpu.SemaphoreType.DMA((7,)), pltpu.SemaphoreType.DMA],
    )(wp)


def _sibling_halves(g):
    H = HALF_ROWS

    def body(g_ref, r_ref, send_sems, recv_sems):
        x, y, c = _pos()
        cps = []
        for s in range(4):
            src = g_ref.at[s, pl.ds(pl.multiple_of((1 - c) * H, H), H), :]
            cps.append(_remote(src, r_ref.at[s], send_sems.at[s], recv_sems.at[s], (x, y, 1 - c)))
            cps[-1].start()
        for cp in cps:
            cp.wait()

    return pl.pallas_call(
        body, name="rs_sibling", out_shape=_sds((4, H, D), F32), in_specs=[ANY], out_specs=ANY,
        scratch_shapes=[pltpu.SemaphoreType.DMA((4,)), pltpu.SemaphoreType.DMA((4,))],
    )(g)


def _ici_exchange(pb):
    H = HALF_ROWS

    def body(p_ref, r_ref, send_sems, recv_sems):
        x, y, c = _pos()
        cps = []
        for j, (tx, ty) in enumerate([(1 - x, y), (x, 1 - y), (1 - x, 1 - y)]):
            cps.append(_remote(p_ref.at[2 * tx + ty], r_ref.at[j], send_sems.at[j], recv_sems.at[j], (tx, ty, c)))
            cps[-1].start()
        for cp in cps:
            cp.wait()

    return pl.pallas_call(
        body, name="rs_ici", out_shape=_sds((3, H, D), BF16), in_specs=[ANY], out_specs=ANY,
        scratch_shapes=[pltpu.SemaphoreType.DMA((3,)), pltpu.SemaphoreType.DMA((3,))],
    )(pb)


def _sibling_share(rh):
    H = HALF_ROWS

    def body(r_ref, o_ref, send_sem, recv_sem, local_sem):
        x, y, c = _pos()
        mine = o_ref.at[pl.ds(pl.multiple_of(c * H, H), H), :]
        loc = pltpu.make_async_copy(r_ref, mine, local_sem)
        loc.start()
        cp = _remote(r_ref, mine, send_sem, recv_sem, (x, y, 1 - c))
        cp.start()
        cp.wait()
        loc.wait()

    return pl.pallas_call(
        body, name="rs_share", out_shape=_sds((2 * H, D), F32), in_specs=[ANY], out_specs=ANY,
        scratch_shapes=[pltpu.SemaphoreType.DMA, pltpu.SemaphoreType.DMA, pltpu.SemaphoreType.DMA],
    )(rh)


def _small_allreduce(v, name):
    P = v.shape[0]
    vm = pl.BlockSpec(memory_space=pltpu.VMEM)

    def body(v_ref, o_ref, gath, send_sems, recv_sems):
        x, y, c = _pos()
        me = 4 * x + 2 * y + c
        gath[me] = v_ref[...]
        cps = []
        for r in range(1, 8):
            tx = (1 - x) if r & 4 else x
            ty = (1 - y) if r & 2 else y
            tc = (1 - c) if r & 1 else c
            cps.append(_remote(v_ref, gath.at[me], send_sems.at[r - 1], recv_sems.at[r - 1], (tx, ty, tc)))
            cps[-1].start()
        for cp in cps:
            cp.wait()
        acc = gath[0]
        for d in range(1, 8):
            acc = acc + gath[d]
        o_ref[...] = acc

    return pl.pallas_call(
        body, name=name, out_shape=_sds((P, LANES), F32), in_specs=[vm], out_specs=vm,
        scratch_shapes=[pltpu.VMEM((8, P, LANES), F32), pltpu.SemaphoreType.DMA((7,)), pltpu.SemaphoreType.DMA((7,))],
    )(v)


RB = 512


def _chip_sum(g, rcv, pos):
    H = HALF_ROWS
    nb = H // RB

    def body(pos_ref, g_ref, r_ref, o_ref):
        o_ref[...] = (g_ref[...] + r_ref[...]).astype(BF16)

    return pl.pallas_call(
        body, name="rs_chip_sum", out_shape=_sds((4, H, D), BF16),
        grid_spec=pltpu.PrefetchScalarGridSpec(
            num_scalar_prefetch=1, grid=(4, nb),
            in_specs=[pl.BlockSpec((1, RB, D), lambda s, i, pos: (s, pos[0] * nb + i, 0)),
                      pl.BlockSpec((1, RB, D), lambda s, i, pos: (s, i, 0))],
            out_specs=pl.BlockSpec((1, RB, D), lambda s, i, pos: (s, i, 0))),
        compiler_params=_params(2),
    )(pos, g, rcv)


def _final_sum(g, rcv, rc, pos):
    H = HALF_ROWS
    nb = H // RB

    def body(pos_ref, g_ref, r_ref, rc_ref, o_ref):
        acc = g_ref[0] + r_ref[0]
        for j in range(3):
            acc = acc + rc_ref[j].astype(F32)
        o_ref[...] = acc

    return pl.pallas_call(
        body, name="rs_final_sum", out_shape=_sds((H, D), F32),
        grid_spec=pltpu.PrefetchScalarGridSpec(
            num_scalar_prefetch=1, grid=(nb,),
            in_specs=[pl.BlockSpec((1, RB, D), lambda i, pos: (pos[1], pos[0] * nb + i, 0)),
                      pl.BlockSpec((1, RB, D), lambda i, pos: (pos[1], i, 0)),
                      pl.BlockSpec((3, RB, D), lambda i, pos: (0, i, 0))],
            out_specs=pl.BlockSpec((RB, D), lambda i, pos: (i, 0))),
        compiler_params=_params(1),
    )(pos, g, rcv, rc)


def _adamw_math(w, g, m, v):
    m = ADAM_B1 * m + (1.0 - ADAM_B1) * g
    v = ADAM_B2 * v + (1.0 - ADAM_B2) * (g * g)
    m_hat = m / (1.0 - ADAM_B1 ** ADAM_STEP)
    v_hat = v / (1.0 - ADAM_B2 ** ADAM_STEP)
    delta = -ADAM_LR * (m_hat / (jnp.sqrt(v_hat) + ADAM_EPS) + ADAM_WD * w)
    return delta, m, v


def _adamw(w, g, m, v, name, rb):
    R, C = w.shape

    def body(w_ref, g_ref, m_ref, v_ref, d_ref, nm_ref, nv_ref):
        d, nm, nv = _adamw_math(w_ref[...], g_ref[...], m_ref[...], v_ref[...])
        d_ref[...] = d
        nm_ref[...] = nm
        nv_ref[...] = nv

    blk = pl.BlockSpec((rb, C), lambda i: (i, 0))
    return pl.pallas_call(
        body, name=name, grid=(R // rb,), in_specs=[blk] * 4, out_specs=[blk] * 3,
        out_shape=[_sds((R, C), F32)] * 3, compiler_params=_params(1),
    )(w, g, m, v)


def _pack_big(parts):
    rows = [parts[name].reshape(-1, D) for name, _, _, _ in BIG]
    used = sum(r.shape[0] for r in rows)
    rows[-1] = jnp.pad(rows[-1], ((0, PACK_ROWS - used), (0, 0)))
    return jnp.concatenate(rows, axis=0)


def _unpack_big(p):
    out, off = {}, 0
    for name, r, c, _ in BIG:
        n = r * c // D
        out[name] = p[..., off:off + n, :].reshape(p.shape[:-2] + (r, c))
        off += n
    return out


def _full_from_slabs(slabs):
    out = {}
    for name, r, c, by_col in BIG:
        s = slabs[name]
        out[name] = jnp.transpose(s, (1, 0, 2)).reshape(r, 4 * c) if by_col else s.reshape(4 * r, c)
    return out


def _slabs_from_full(full):
    out = {}
    for name, r, c, by_col in BIG:
        f = full[name]
        out[name] = jnp.transpose(f.reshape(r, 4, c), (1, 0, 2)) if by_col else f.reshape(4, r, c)
    return out


SMALL = (("g_mix", 8), ("b_f", 8), ("conv_w", None), ("conv_b", 8), ("ln_g", 8), ("ln_b", 8), ("g_x", 8), ("g_mem", 8),
         ("g_ffn", 8), ("g_final", 8), ("loss", 8))


def _pack_small(parts, conv_rows):
    rows = []
    for name, n in SMALL:
        if name not in parts:
            continue
        n = conv_rows if n is None else n
        flat = parts[name].reshape(-1).astype(F32)
        flat = jnp.pad(flat, (0, n * LANES - flat.shape[0]))
        rows.append(flat.reshape(n, LANES))
    return jnp.concatenate(rows, axis=0)


def _unpack_small(p, shapes, conv_rows):
    out, off = {}, 0
    for name, n in SMALL:
        if name not in shapes:
            continue
        n = conv_rows if n is None else n
        size = math.prod(shapes[name])
        out[name] = p[off:off + n].reshape(-1)[:size].reshape(shapes[name])
        off += n
    return out


def kernel(x, mem, g_mix, w_in, b_f, conv_w, conv_b, ln_g, ln_b, w_out, g_x, g_mem, w_mq, w_mkv, w_mo, g_ffn, w_gu, w_down, g_final, loss_target, m_g_mix, m_w_in, m_b_f, m_conv_w, m_conv_b, m_ln_g, m_ln_b, m_w_out, m_g_x, m_g_mem, m_w_mq, m_w_mkv, m_w_mo, m_g_ffn, m_w_gu, m_w_down, m_g_final, v_g_mix, v_w_in, v_b_f, v_conv_w, v_conv_b, v_ln_g, v_ln_b, v_w_out, v_g_x, v_g_mem, v_w_mq, v_w_mkv, v_w_mo, v_g_ffn, v_w_gu, v_w_down, v_g_final):
    names = ["g_mix", "w_in", "b_f", "conv_w", "conv_b", "ln_g", "ln_b", "w_out", "g_x", "g_mem", "w_mq", "w_mkv",
             "w_mo", "g_ffn", "w_gu", "w_down", "g_final"]
    W = dict(zip(names, [g_mix, w_in, b_f, conv_w, conv_b, ln_g, ln_b, w_out, g_x, g_mem, w_mq, w_mkv, w_mo, g_ffn,
                         w_gu, w_down, g_final]))
    M = dict(zip(names, [m_g_mix, m_w_in, m_b_f, m_conv_w, m_conv_b, m_ln_g, m_ln_b, m_w_out, m_g_x, m_g_mem, m_w_mq,
                         m_w_mkv, m_w_mo, m_g_ffn, m_w_gu, m_w_down, m_g_final]))
    V = dict(zip(names, [v_g_mix, v_w_in, v_b_f, v_conv_w, v_conv_b, v_ln_g, v_ln_b, v_w_out, v_g_x, v_g_mem, v_w_mq,
                         v_w_mkv, v_w_mo, v_g_ffn, v_w_gu, v_w_down, v_g_final]))
    big_names = [n for n, _, _, _ in BIG]
    B, S, _ = x.shape
    T = B * S
    mx, my, mc = _pos()
    chip = 2 * mx + my
    pos = jnp.stack([mc, chip]).astype(jnp.int32)

    shard2d = lambda a: a.reshape(a.shape[-2], a.shape[-1])
    wp = _pack_big({n: shard2d(W[n]).astype(BF16) for n in big_names})
    gathered = _allgather_packed(wp).reshape(4, PACK_ROWS, D)
    full = _full_from_slabs(_unpack_big(gathered))
    w_in_f = jnp.pad(full["w_in"], ((0, 0), (0, D_IN_PAD - D_IN)))
    cw_mine = jnp.pad(shard2d(conv_w), ((0, 1), (0, 0)))
    cw_slot = lax.dynamic_update_slice(jnp.zeros((CONV_HALO, CONV_CH), F32), cw_mine, (0, chip * LANES))
    cw = _small_allreduce(cw_slot.reshape(CONV_HALO * 4, LANES) * 0.5, "gather_conv_w").reshape(CONV_HALO, CONV_CH)

    row = lambda a: a.reshape(1, -1)
    bf_pad = jnp.pad(row(b_f), ((0, 0), (0, LANES - 8)))
    x2d = x.reshape(T, D)
    mem2d = mem.reshape(B * MEM_LEN, D)
    tgt = loss_target.reshape(T, D)

    h, u, gt, q, k, v, zf, c, cq = _fwd_in(x2d, row(g_mix), w_in_f, bf_pad, B, S)
    ckT = jnp.transpose(c.reshape(B, S, LANES)[:, :, :8], (0, 2, 1)).reshape(B, N_PAIR, 2, S)
    ckT = jnp.pad(ckT, ((0, 0), (0, 0), (0, 6), (0, 0)))
    y, co = _conv_fwd(u, gt, cw, row(conv_b), row(ln_g), row(ln_b), B, S)
    o, lse = _fox_fwd(q, k, v, cq, ckT, B, S)
    mn, km, vm = _mem_kv(mem2d, row(g_mem), full["w_mkv"], B)
    x1, hx, qm, om, x2 = _fwd_mid(x2d, co, o, km, vm, full["w_out"], full["w_mq"], full["w_mo"], row(g_x), B, S)
    hf, gu, act, dx3, loss_p, dg_final = _fwd_ffn(x2, tgt, full["w_gu"], full["w_down"], row(g_ffn), row(g_final), T)

    dgu, dx2, dg_ffn = _bwd_ffn(dx3, gu, x2, full["w_gu"], full["w_down"], row(g_ffn), T)
    dx1, dqm, dco, do, dd, dkm, dvm, dg_x = _bwd_mid(dx2, x1, qm, km, vm, o, full["w_mo"], full["w_mq"], full["w_out"],
                                                     row(g_x), B, S)
    dkv, dg_mem = _mem_bwd(dkm, dvm, mem2d, full["w_mkv"], row(g_mem), B)
    dq, dk, dv, dc, dcq = _fox_bwd(q, k, v, do, lse, dd, cq, ckT, B, S)
    dc8 = jnp.transpose(dc[:, :, :2, :].reshape(B, 8, S), (0, 2, 1)).reshape(T, 8)
    dc8 = dc8 + dcq.reshape(T, 8, HEAD_D)[:, :, 0]
    dzf, dbf = _fgate_bwd(jnp.pad(dc8, ((0, 0), (0, LANES - 8))), zf, B, S)
    du, dgt, dcw, dvec = _conv_bwd(dco, y, u, gt, cw, row(ln_g), row(ln_b), B, S)
    dz = jnp.concatenate([du, dgt, dq, dk, dv, dzf], axis=1)
    grad_x, dg_mix = _bwd_in(dz, w_in_f, x2d, dx1, row(g_mix), T)

    gfull = {
        "w_in": _dw(h, dz, "dw_in", 384)[:, :D_IN],
        "w_out": jnp.concatenate([_dw(co, dx1, "dw_out_conv", 512), _dw(o, dx1, "dw_out_fox", 512)], axis=0),
        "w_mq": _dw(hx, dqm, "dw_mq", 512),
        "w_mkv": _dw(mn, dkv, "dw_mkv", 512),
        "w_mo": _dw(om, dx2, "dw_mo", 512),
        "w_gu": _dw(hf, dgu, "dw_gu", 512),
        "w_down": _dw(act, dx3, "dw_down", 512),
    }
    slabs = _slabs_from_full(gfull)
    gpack = jnp.stack([_pack_big({n: slabs[n][s] for n in big_names}) for s in range(4)])

    rcv = _sibling_halves(gpack)
    pb = _chip_sum(gpack, rcv, pos)
    rc = _ici_exchange(pb)
    rh = _final_sum(gpack, rcv, rc, pos)
    gshard = _sibling_share(rh)
    pk = lambda d: _pack_big({n: shard2d(d[n]) for n in big_names})
    delta_p, newm_p, newv_p = _adamw(pk(W), gshard, pk(M), pk(V), "adamw_big", RB)
    G, DL, NM, NV = (_unpack_big(a) for a in (gshard, delta_p, newm_p, newv_p))

    small_g = {"g_mix": dg_mix, "b_f": dbf[:, :8], "conv_w": dcw, "conv_b": dvec[0], "ln_g": dvec[1], "ln_b": dvec[2],
               "g_x": dg_x, "g_mem": dg_mem, "g_ffn": dg_ffn, "g_final": dg_final, "loss": loss_p[:, :1]}
    sg = _small_allreduce(_pack_small(small_g, CONV_HALO * 4), "allreduce_small")
    shapes = {n: W[n].shape for n in names if n not in big_names}
    shapes["conv_w"] = (CONV_HALO, CONV_CH)
    shapes["loss"] = (1,)
    sgrads = _unpack_small(sg, shapes, CONV_HALO * 4)
    loss = sgrads.pop("loss")[0]
    sgrads["conv_w"] = lax.dynamic_slice(sgrads["conv_w"], (0, chip * LANES), (CONV_K, LANES)).reshape(W["conv_w"].shape)
    spack = lambda d: _pack_small({n: d[n] for n in sgrads}, CONV_HALO)
    sd, snm, snv = _adamw(spack(W), spack(sgrads), spack(M), spack(V), "adamw_small", 8)
    sshapes = {n: W[n].shape for n in sgrads}
    SD, SNM, SNV = (_unpack_small(a, sshapes, CONV_HALO) for a in (sd, snm, snv))

    def collect(bigs, smalls):
        return [bigs[n].reshape(W[n].shape) if n in big_names else smalls[n] for n in names]

    return (loss, grad_x.reshape(x.shape), *collect(G, sgrads), *collect(DL, SD), *collect(NM, SNM), *collect(NV, SNV))
```

```python
import functools
import math

import jax
import jax.numpy as jnp
from jax import lax
from jax.experimental import pallas as pl
from jax.experimental.pallas import tpu as pltpu

F32, BF16 = jnp.float32, jnp.bfloat16
HIGHEST = lax.Precision.HIGHEST
MESH = pl.DeviceIdType.MESH

D = 1024
CONV_CH = 512
CONV_K = 31
CONV_HALO = 32
FOX_W = 512
HEAD_D = 64
N_PAIR = 4
MEM_LEN = 256
MEM_HEADS = 4
MEM_HD = 256
D_FF = 2816
FF_CHUNK = 1408
D_IN = 2568
D_IN_PAD = 2688
OFF_F = 2560
EPS = 1e-6
LANES = 128

ADAM_LR, ADAM_B1, ADAM_B2, ADAM_EPS, ADAM_WD, ADAM_STEP = 0.001, 0.9, 0.999, 1e-08, 0.01, 10

VMEM_LIMIT = 60 * 1024 * 1024

BIG = (("w_out", 256, 1024, False), ("w_mq", 256, 1024, False), ("w_mkv", 1024, 512, True),
       ("w_mo", 256, 1024, False), ("w_gu", 1024, 1408, True), ("w_down", 704, 1024, False),
       ("w_in", 1024, 642, True))

ANY = pl.BlockSpec(memory_space=pl.ANY)


def _sig(x):
    return 1.0 / (1.0 + jnp.exp(-x))


def _dot(a, b):
    return jnp.dot(a, b, preferred_element_type=F32)


def _dot_nt(a, b):
    return lax.dot_general(a, b, (((1,), (1,)), ((), ())), preferred_element_type=F32)


def _dot_tn(a, b):
    return lax.dot_general(a, b, (((0,), (0,)), ((), ())), preferred_element_type=F32)


def _dot_hi(a, b):
    return jnp.dot(a, b, precision=HIGHEST, preferred_element_type=F32)


def _resident(a):
    nd = a.ndim
    return pl.BlockSpec(a.shape, lambda *_: (0,) * nd, pipeline_mode=pl.Buffered(1))


def _acc_spec(shape):
    nd = len(shape)
    return pl.BlockSpec(shape, lambda *_: (0,) * nd)


def _params(n_grid):
    return pltpu.CompilerParams(dimension_semantics=("arbitrary",) * n_grid, vmem_limit_bytes=VMEM_LIMIT)


def _sds(shape, dtype):
    return jax.ShapeDtypeStruct(shape, dtype)


def _rms(x):
    r = lax.rsqrt(jnp.mean(x * x, axis=-1, keepdims=True) + EPS)
    return r, x * r


def _rms_bwd(dy, xh, r, g):
    dxh = dy * g
    dx = r * (dxh - xh * jnp.mean(dxh * xh, axis=-1, keepdims=True))
    return dx, dy * xh


def _head_expand(rows, cols):
    hd = lax.broadcasted_iota(jnp.int32, (rows, cols), 1) // HEAD_D
    hr = lax.broadcasted_iota(jnp.int32, (rows, cols), 0)
    return (hd == hr).astype(F32)


def _head_sum(n):
    hc = lax.broadcasted_iota(jnp.int32, (n, n), 1) // HEAD_D
    hr = lax.broadcasted_iota(jnp.int32, (n, n), 0) // HEAD_D
    return (hc == hr).astype(F32)


def _fwd_in(x2, g_mix, w_in, bf_pad, B, S):
    T = B * S
    TB = min(512, S)
    nb = S // TB

    def body(x_ref, g_ref, w_ref, bf_ref, h_ref, u_ref, gt_ref, q_ref, k_ref, v_ref, zf_ref, c_ref, cq_ref,
             carry):
        j = pl.program_id(1)

        @pl.when(j == 0)
        def _():
            carry[...] = jnp.zeros_like(carry)

        _, xh = _rms(x_ref[...])
        h = (xh * g_ref[...]).astype(BF16)
        h_ref[...] = h
        u_ref[...] = _dot(h, w_ref[:, 0:512])
        gt_ref[...] = _dot(h, w_ref[:, 512:1024])
        q_ref[...] = _dot(h, w_ref[:, 1024:1536]).astype(BF16)
        k_ref[...] = _dot(h, w_ref[:, 1536:2048]).astype(BF16)
        v_ref[...] = _dot(h, w_ref[:, 2048:2560]).astype(BF16)
        zf = _dot(h, w_ref[:, OFF_F:D_IN_PAD]) + bf_ref[...]
        zf_ref[...] = zf
        lane = lax.broadcasted_iota(jnp.int32, zf.shape, 1)
        logf = jnp.where(lane < 8, jnp.minimum(zf, 0.0) - jnp.log(1.0 + jnp.exp(-jnp.abs(zf))), 0.0)
        row = lax.broadcasted_iota(jnp.int32, (TB, TB), 0)
        col = lax.broadcasted_iota(jnp.int32, (TB, TB), 1)
        c = _dot_hi((row >= col).astype(F32), logf) + carry[0:1, :]
        carry[0:1, :] = c[TB - 1:TB, :]
        c_ref[...] = c
        cq_ref[...] = _dot_hi(c, _head_expand(LANES, FOX_W))

    tok = lambda w: pl.BlockSpec((TB, w), lambda b, j: (b * nb + j, 0))
    outs = [(D, BF16), (512, F32), (512, F32), (512, BF16), (512, BF16), (512, BF16), (LANES, F32),
            (LANES, F32), (FOX_W, F32)]
    return pl.pallas_call(
        body, name="fwd_in", grid=(B, nb),
        in_specs=[tok(D), _resident(g_mix), _resident(w_in), _resident(bf_pad)],
        out_specs=[tok(w) for w, _ in outs],
        out_shape=[_sds((T, w), dt) for w, dt in outs],
        scratch_shapes=[pltpu.VMEM((8, LANES), F32)],
        compiler_params=_params(2),
    )(x2, g_mix, w_in, bf_pad)


def _layernorm_silu(y, lg, lb):
    mu = jnp.mean(y, axis=-1, keepdims=True)
    yc = y - mu
    rs = lax.rsqrt(jnp.mean(yc * yc, axis=-1, keepdims=True) + EPS)
    n = yc * rs
    l = n * lg + lb
    return rs, n, l


def _conv_fwd(u, gt, cw, cb, lng, lnb, B, S):
    T = B * S
    CB = min(256, S)
    nb = S // CB

    def body(u_ref, gt_ref, w_ref, cb_ref, lg_ref, lb_ref, y_ref, co_ref, acat):
        j = pl.program_id(1)

        @pl.when(j == 0)
        def _():
            acat[0:CONV_HALO, :] = jnp.zeros((CONV_HALO, CONV_CH), F32)

        acat[CONV_HALO:CONV_HALO + CB, :] = u_ref[...] * _sig(gt_ref[...])
        acc = jnp.zeros((CB, CONV_CH), F32) + cb_ref[...]
        for k in range(CONV_K):
            off = CONV_HALO - (CONV_K - 1) + k
            acc = acc + w_ref[k:k + 1, :] * acat[off:off + CB, :]
        acat[0:CONV_HALO, :] = acat[CB:CB + CONV_HALO, :]
        y_ref[...] = acc
        _, _, l = _layernorm_silu(acc, lg_ref[...], lb_ref[...])
        co_ref[...] = (l * _sig(l)).astype(BF16)

    tok = lambda w: pl.BlockSpec((CB, w), lambda b, j: (b * nb + j, 0))
    return pl.pallas_call(
        body, name="conv_fwd", grid=(B, nb),
        in_specs=[tok(CONV_CH), tok(CONV_CH), _resident(cw), _resident(cb), _resident(lng), _resident(lnb)],
        out_specs=[tok(CONV_CH), tok(CONV_CH)],
        out_shape=[_sds((T, CONV_CH), F32), _sds((T, CONV_CH), BF16)],
        scratch_shapes=[pltpu.VMEM((CONV_HALO + CB, CONV_CH), F32)],
        compiler_params=_params(2),
    )(u, gt, cw, cb, lng, lnb)


def _fox_fwd(q, k, v, cq, ckT, B, S):
    T = B * S
    TQ = min(256, S)
    nq = S // TQ
    scale = 1.0 / math.sqrt(HEAD_D)

    def body(q_ref, k_ref, v_ref, cq_ref, ck_ref, o_ref, lse_ref):
        i = pl.program_id(2)
        lane = lax.broadcasted_iota(jnp.int32, (TQ, LANES), 1)
        lo = lane < HEAD_D
        q2 = q_ref[...]
        zero = jnp.zeros_like(q2)
        qh = (jnp.where(lo, q2, zero), jnp.where(lo, zero, q2))
        cqv = cq_ref[...]
        cqh = (cqv[:, 0:1], cqv[:, HEAD_D:HEAD_D + 1])
        row = lax.broadcasted_iota(jnp.int32, (TQ, TQ), 0)
        col = lax.broadcasted_iota(jnp.int32, (TQ, TQ), 1)

        def step(j, carry):
            start = pl.multiple_of(j * TQ, TQ)
            kj = k_ref[pl.ds(start, TQ), :]
            vj = v_ref[pl.ds(start, TQ), :]
            valid = (j * TQ + col) <= (i * TQ + row)
            out = []
            for h in range(2):
                m, l, acc = carry[3 * h:3 * h + 3]
                s = _dot_nt(qh[h], kj) * scale + (cqh[h] - ck_ref[0, 0, h:h + 1, pl.ds(start, TQ)])
                s = jnp.where(valid, s, -1e30)
                m_new = jnp.maximum(m, jnp.max(s, axis=-1, keepdims=True))
                alpha = jnp.exp(m - m_new)
                p = jnp.exp(s - m_new)
                l = alpha * l + jnp.sum(p, axis=-1, keepdims=True)
                acc = alpha * acc + _dot(p.astype(BF16), vj)
                out += [m_new, l, acc]
            return tuple(out)

        init = (jnp.full((TQ, 1), -1e30, F32), jnp.zeros((TQ, 1), F32), jnp.zeros((TQ, LANES), F32)) * 2
        ma, la, acca, mb, lb, accb = lax.fori_loop(0, i + 1, step, init)
        o_ref[...] = jnp.where(lo, acca / la, accb / lb)
        lse_ref[...] = jnp.where(lo, ma + jnp.log(la), mb + jnp.log(lb))

    qspec = pl.BlockSpec((TQ, LANES), lambda b, p, i: (b * nq + i, p))
    kspec = pl.BlockSpec((S, LANES), lambda b, p, i: (b, p))
    return pl.pallas_call(
        body, name="fox_fwd", grid=(B, N_PAIR, nq),
        in_specs=[qspec, kspec, kspec, qspec, pl.BlockSpec((1, 1, 8, S), lambda b, p, i: (b, p, 0, 0))],
        out_specs=[qspec, qspec],
        out_shape=[_sds((T, FOX_W), F32), _sds((T, FOX_W), F32)],
        compiler_params=_params(3),
    )(q, k, v, cq, ckT)


def _mem_kv(mem2, g_mem, w_mkv, B):
    def body(m_ref, g_ref, w_ref, mn_ref, km_ref, vm_ref):
        _, xh = _rms(m_ref[...])
        mn = (xh * g_ref[...]).astype(BF16)
        mn_ref[...] = mn
        for s in range(2):
            km_ref[:, 512 * s:512 * (s + 1)] = _dot(mn, w_ref[s]).astype(BF16)
            vm_ref[:, 512 * s:512 * (s + 1)] = _dot(mn, w_ref[2 + s]).astype(BF16)

    blk = pl.BlockSpec((MEM_LEN, D), lambda b: (b, 0))
    return pl.pallas_call(
        body, name="mem_kv", grid=(B,),
        in_specs=[blk, _resident(g_mem), _resident(w_mkv)],
        out_specs=[blk, blk, blk],
        out_shape=[_sds((B * MEM_LEN, D), BF16)] * 3,
        compiler_params=_params(1),
    )(mem2, g_mem, w_mkv)


def _mem_probs(qm, km):
    ps = []
    for h in range(MEM_HEADS):
        hs = slice(h * MEM_HD, (h + 1) * MEM_HD)
        lg = _dot_nt(qm[:, hs], km[:, hs]) * (1.0 / math.sqrt(MEM_HD))
        e = jnp.exp(lg - jnp.max(lg, axis=-1, keepdims=True))
        ps.append(e / jnp.sum(e, axis=-1, keepdims=True))
    return ps


def _fwd_mid(x2, co, o, km, vm, w_out, w_mq, w_mo, g_x, B, S):
    T = B * S
    TB = min(512, S)
    nb = S // TB

    def body(x_ref, co_ref, o_ref, km_ref, vm_ref, wo_ref, wq_ref, wm_ref, g_ref,
             x1_ref, hx_ref, qm_ref, om_ref, x2_ref, cat_ref):
        cat_ref[:, 0:CONV_CH] = co_ref[...]
        cat_ref[:, CONV_CH:D] = o_ref[...].astype(BF16)
        x1 = x_ref[...] + _dot(cat_ref[...], wo_ref[...])
        x1_ref[...] = x1
        _, xh = _rms(x1)
        hx = (xh * g_ref[...]).astype(BF16)
        hx_ref[...] = hx
        qm = _dot(hx, wq_ref[...]).astype(BF16)
        qm_ref[...] = qm
        ps = _mem_probs(qm, km_ref[...])
        vmv = vm_ref[...]
        for h in range(MEM_HEADS):
            hs = slice(h * MEM_HD, (h + 1) * MEM_HD)
            om_ref[:, hs] = _dot(ps[h].astype(BF16), vmv[:, hs]).astype(BF16)
        x2_ref[...] = x1 + _dot(om_ref[...], wm_ref[...])

    tok = lambda w: pl.BlockSpec((TB, w), lambda b, j: (b * nb + j, 0))
    memb = pl.BlockSpec((MEM_LEN, D), lambda b, j: (b, 0))
    outs = [(D, F32), (D, BF16), (D, BF16), (D, BF16), (D, F32), (D, BF16)]
    return pl.pallas_call(
        body, name="fwd_mid", grid=(B, nb),
        in_specs=[tok(D), tok(CONV_CH), tok(FOX_W), memb, memb, _resident(w_out), _resident(w_mq), _resident(w_mo),
                  _resident(g_x)],
        out_specs=[tok(w) for w, _ in outs],
        out_shape=[_sds((T, w), dt) for w, dt in outs],
        compiler_params=_params(2),
    )(x2, co, o, km, vm, w_out, w_mq, w_mo, g_x)


def _fwd_ffn(x2, tgt, w_gu, w_down, g_ffn, g_final, T):
    TB = min(256, T)
    nb = T // TB

    def body(x_ref, t_ref, wgu_ref, wd_ref, gf_ref, gl_ref, hf_ref, gu_ref, act_ref, dx3_ref, loss_ref, dgl_ref):
        i = pl.program_id(0)

        @pl.when(i == 0)
        def _():
            loss_ref[...] = jnp.zeros_like(loss_ref)
            dgl_ref[...] = jnp.zeros_like(dgl_ref)

        x2v = x_ref[...]
        _, xh = _rms(x2v)
        hf = (xh * gf_ref[...]).astype(BF16)
        hf_ref[...] = hf
        x3 = x2v
        for ch in range(D_FF // FF_CHUNK):
            c0 = ch * FF_CHUNK
            g = _dot(hf, wgu_ref[ch])
            u = _dot(hf, wgu_ref[2 + ch])
            gu_ref[:, c0:c0 + FF_CHUNK] = g
            gu_ref[:, D_FF + c0:D_FF + c0 + FF_CHUNK] = u
            act = (g * _sig(g) * u).astype(BF16)
            act_ref[:, c0:c0 + FF_CHUNK] = act
            x3 = x3 + _dot(act, wd_ref[c0:c0 + FF_CHUNK, :])
        r3, xh3 = _rms(x3)
        gl = gl_ref[...]
        e = xh3 * gl - t_ref[...]
        loss_ref[...] += jnp.sum(e * e) * (0.5 / D)
        dy = e * (1.0 / D)
        dx3, dgl = _rms_bwd(dy, xh3, r3, gl)
        dx3_ref[...] = dx3
        dgl_ref[...] += jnp.sum(dgl, axis=0, keepdims=True)

    tok = lambda w: pl.BlockSpec((TB, w), lambda i: (i, 0))
    return pl.pallas_call(
        body, name="fwd_ffn", grid=(nb,),
        in_specs=[tok(D), tok(D), _resident(w_gu), _resident(w_down), _resident(g_ffn), _resident(g_final)],
        out_specs=[tok(D), tok(2 * D_FF), tok(D_FF), tok(D), _acc_spec((1, LANES)), _acc_spec((1, D))],
        out_shape=[_sds((T, D), BF16), _sds((T, 2 * D_FF), F32), _sds((T, D_FF), BF16), _sds((T, D), F32),
                   _sds((1, LANES), F32), _sds((1, D), F32)],
        compiler_params=_params(1),
    )(x2, tgt, w_gu, w_down, g_ffn, g_final)


def _bwd_ffn(dx3, gu, x2, w_gu, w_down, g_ffn, T):
    TB = min(256, T)
    nb = T // TB

    def body(d_ref, gu_ref, x_ref, wgu_ref, wd_ref, gf_ref, dgu_ref, dx2_ref, dgf_ref):
        i = pl.program_id(0)

        @pl.when(i == 0)
        def _():
            dgf_ref[...] = jnp.zeros_like(dgf_ref)

        dx3v = d_ref[...]
        db = dx3v.astype(BF16)
        dhf = jnp.zeros((TB, D), F32)
        for ch in range(D_FF // FF_CHUNK):
            c0 = ch * FF_CHUNK
            dact = _dot_nt(db, wd_ref[c0:c0 + FF_CHUNK, :])
            g = gu_ref[:, c0:c0 + FF_CHUNK]
            u = gu_ref[:, D_FF + c0:D_FF + c0 + FF_CHUNK]
            sg = _sig(g)
            dg = (dact * u * sg * (1.0 + g * (1.0 - sg))).astype(BF16)
            du = (dact * g * sg).astype(BF16)
            dgu_ref[:, c0:c0 + FF_CHUNK] = dg
            dgu_ref[:, D_FF + c0:D_FF + c0 + FF_CHUNK] = du
            dhf = dhf + _dot_nt(dg, wgu_ref[ch]) + _dot_nt(du, wgu_ref[2 + ch])
        r2, xh2 = _rms(x_ref[...])
        dx, dg_tok = _rms_bwd(dhf, xh2, r2, gf_ref[...])
        dx2_ref[...] = dx3v + dx
        dgf_ref[...] += jnp.sum(dg_tok, axis=0, keepdims=True)

    tok = lambda w: pl.BlockSpec((TB, w), lambda i: (i, 0))
    return pl.pallas_call(
        body, name="bwd_ffn", grid=(nb,),
        in_specs=[tok(D), tok(2 * D_FF), tok(D), _resident(w_gu), _resident(w_down), _resident(g_ffn)],
        out_specs=[tok(2 * D_FF), tok(D), _acc_spec((1, D))],
        out_shape=[_sds((T, 2 * D_FF), BF16), _sds((T, D), F32), _sds((1, D), F32)],
        compiler_params=_params(1),
    )(dx3, gu, x2, w_gu, w_down, g_ffn)


def _bwd_mid(dx2, x1, qm, km, vm, o, w_mo, w_mq, w_out, g_x, B, S):
    T = B * S
    TB = min(512, S)
    nb = S // TB
    inv = 1.0 / math.sqrt(MEM_HD)

    def body(d_ref, x1_ref, qm_ref, km_ref, vm_ref, o_ref, wm_ref, wq_ref, wo_ref, g_ref,
             dx1_ref, dqm_ref, dco_ref, do_ref, dd_ref, dkm_ref, dvm_ref, dgx_ref):
        b = pl.program_id(0)
        j = pl.program_id(1)

        @pl.when((b == 0) & (j == 0))
        def _():
            dgx_ref[...] = jnp.zeros_like(dgx_ref)

        @pl.when(j == 0)
        def _():
            dkm_ref[...] = jnp.zeros_like(dkm_ref)
            dvm_ref[...] = jnp.zeros_like(dvm_ref)

        dx2v = d_ref[...]
        dom = _dot_nt(dx2v.astype(BF16), wm_ref[...]).astype(BF16)
        qmv = qm_ref[...]
        kmv = km_ref[...]
        vmv = vm_ref[...]
        ps = _mem_probs(qmv, kmv)
        for h in range(MEM_HEADS):
            hs = slice(h * MEM_HD, (h + 1) * MEM_HD)
            p = ps[h]
            dp = _dot_nt(dom[:, hs], vmv[:, hs])
            ds = (p * (dp - jnp.sum(p * dp, axis=-1, keepdims=True))).astype(BF16)
            dqm_ref[:, hs] = (_dot(ds, kmv[:, hs]) * inv).astype(BF16)
            dkm_ref[:, hs] += _dot_tn(ds, qmv[:, hs]) * inv
            dvm_ref[:, hs] += _dot_tn(p.astype(BF16), dom[:, hs])
        dhx = _dot_nt(dqm_ref[...], wq_ref[...])
        r1, xh1 = _rms(x1_ref[...])
        dx, dg_tok = _rms_bwd(dhx, xh1, r1, g_ref[...])
        dx1 = dx2v + dx
        dx1_ref[...] = dx1
        dgx_ref[...] += jnp.sum(dg_tok, axis=0, keepdims=True)
        d1b = dx1.astype(BF16)
        dco_ref[...] = _dot_nt(d1b, wo_ref[0:CONV_CH, :])
        do = _dot_nt(d1b, wo_ref[CONV_CH:D, :])
        dob = do.astype(BF16)
        do_ref[...] = dob
        dd_ref[...] = _dot_hi(dob.astype(F32) * o_ref[...], _head_sum(FOX_W))

    tok = lambda w: pl.BlockSpec((TB, w), lambda b, j: (b * nb + j, 0))
    memb = pl.BlockSpec((MEM_LEN, D), lambda b, j: (b, 0))
    outs = [(D, F32), (D, BF16), (CONV_CH, F32), (FOX_W, BF16), (FOX_W, F32)]
    return pl.pallas_call(
        body, name="bwd_mid", grid=(B, nb),
        in_specs=[tok(D), tok(D), tok(D), memb, memb, tok(FOX_W), _resident(w_mo), _resident(w_mq), _resident(w_out),
                  _resident(g_x)],
        out_specs=[tok(w) for w, _ in outs] + [memb, memb, _acc_spec((1, D))],
        out_shape=[_sds((T, w), dt) for w, dt in outs] + [_sds((B * MEM_LEN, D), F32)] * 2 + [_sds((1, D), F32)],
        compiler_params=_params(2),
    )(dx2, x1, qm, km, vm, o, w_mo, w_mq, w_out, g_x)


def _mem_bwd(dkm, dvm, mem2, w_mkv, g_mem, B):
    def body(dk_ref, dv_ref, m_ref, w_ref, g_ref, dkv_ref, dg_ref):
        b = pl.program_id(0)

        @pl.when(b == 0)
        def _():
            dg_ref[...] = jnp.zeros_like(dg_ref)

        dk = dk_ref[...].astype(BF16)
        dv = dv_ref[...].astype(BF16)
        dkv_ref[:, 0:D] = dk
        dkv_ref[:, D:2 * D] = dv
        dmn = jnp.zeros((MEM_LEN, D), F32)
        for s in range(2):
            dmn = dmn + _dot_nt(dk[:, 512 * s:512 * (s + 1)], w_ref[s]) + _dot_nt(dv[:, 512 * s:512 * (s + 1)], w_ref[2 + s])
        _, xh = _rms(m_ref[...])
        dg_ref[...] += jnp.sum(dmn * xh, axis=0, keepdims=True)

    blk = pl.BlockSpec((MEM_LEN, D), lambda b: (b, 0))
    return pl.pallas_call(
        body, name="mem_bwd", grid=(B,),
        in_specs=[blk, blk, blk, _resident(w_mkv), _resident(g_mem)],
        out_specs=[pl.BlockSpec((MEM_LEN, 2 * D), lambda b: (b, 0)), _acc_spec((1, D))],
        out_shape=[_sds((B * MEM_LEN, 2 * D), BF16), _sds((1, D), F32)],
        compiler_params=_params(1),
    )(dkm, dvm, mem2, w_mkv, g_mem)


def _fox_bwd(q, k, v, do, lse, dd, cq, ckT, B, S):
    T = B * S
    TK = min(256, S)
    nk = S // TK
    scale = 1.0 / math.sqrt(HEAD_D)

    def body(q_ref, k_ref, v_ref, do_ref, lse_ref, dd_ref, cq_ref, ck_ref, dq_ref, dk_ref, dv_ref, dc_ref, dcq_ref,
             dq_acc, dcq_acc):
        j = pl.program_id(2)

        @pl.when(j == 0)
        def _():
            dq_acc[...] = jnp.zeros_like(dq_acc)
            dcq_acc[...] = jnp.zeros_like(dcq_acc)

        lane = lax.broadcasted_iota(jnp.int32, (TK, LANES), 1)
        lo = lane < HEAD_D
        k2 = k_ref[...]
        v2 = v_ref[...]
        zero = jnp.zeros_like(k2)
        kh = (jnp.where(lo, k2, zero), jnp.where(lo, zero, k2))
        vh = (jnp.where(lo, v2, zero), jnp.where(lo, zero, v2))
        kstart = pl.multiple_of(j * TK, TK)
        ckh = tuple(ck_ref[0, 0, h:h + 1, pl.ds(kstart, TK)] for h in range(2))
        row = lax.broadcasted_iota(jnp.int32, (TK, TK), 0)
        col = lax.broadcasted_iota(jnp.int32, (TK, TK), 1)

        def step(i, carry):
            start = pl.multiple_of(i * TK, TK)
            qi = q_ref[pl.ds(start, TK), :]
            doi = do_ref[pl.ds(start, TK), :]
            lsei = lse_ref[pl.ds(start, TK), :]
            ddi = dd_ref[pl.ds(start, TK), :]
            cqi = cq_ref[pl.ds(start, TK), :]
            valid = (j * TK + col) <= (i * TK + row)
            out = []
            dq_i = []
            rs_i = []
            for h in range(2):
                dk_a, dv_a, dc_a = carry[3 * h:3 * h + 3]
                hc = slice(h * HEAD_D, h * HEAD_D + 1)
                s = _dot_nt(qi, kh[h]) * scale + (cqi[:, hc] - ckh[h])
                p = jnp.where(valid, jnp.exp(s - lsei[:, hc]), 0.0)
                dp = _dot_nt(doi, vh[h])
                ds = p * (dp - ddi[:, hc])
                dc_a = dc_a + jnp.sum(ds, axis=0, keepdims=True)
                rs_i.append(jnp.sum(ds, axis=1, keepdims=True))
                ds = ds.astype(BF16)
                dv_a = dv_a + _dot_tn(p.astype(BF16), doi)
                dk_a = dk_a + _dot_tn(ds, qi)
                dq_i.append(_dot(ds, kh[h]))
                out += [dk_a, dv_a, dc_a]
            dq_acc[pl.ds(start, TK), :] += jnp.where(lo, dq_i[0], dq_i[1]) * scale
            dcq_acc[pl.ds(start, TK), :] += jnp.where(lo, rs_i[0], rs_i[1])
            return tuple(out)

        init = (jnp.zeros((TK, LANES), F32), jnp.zeros((TK, LANES), F32), jnp.zeros((1, TK), F32)) * 2
        dka, dva, dca, dkb, dvb, dcb = lax.fori_loop(j, nk, step, init)
        dk_ref[...] = (jnp.where(lo, dka, dkb) * scale).astype(BF16)
        dv_ref[...] = jnp.where(lo, dva, dvb).astype(BF16)
        sub = lax.broadcasted_iota(jnp.int32, (8, TK), 0)
        dc_ref[0, 0] = jnp.where(sub == 0, -dca, jnp.where(sub == 1, -dcb, 0.0))

        @pl.when(j == nk - 1)
        def _():
            dq_ref[...] = dq_acc[...].astype(BF16)
            dcq_ref[...] = dcq_acc[...]

    full = pl.BlockSpec((S, LANES), lambda b, p, j: (b, p))
    blk = pl.BlockSpec((TK, LANES), lambda b, p, j: (b * nk + j, p))
    return pl.pallas_call(
        body, name="fox_bwd", grid=(B, N_PAIR, nk),
        in_specs=[full, blk, blk, full, full, full, full, pl.BlockSpec((1, 1, 8, S), lambda b, p, j: (b, p, 0, 0))],
        out_specs=[full, blk, blk, pl.BlockSpec((1, 1, 8, TK), lambda b, p, j: (b, p, 0, j)), full],
        out_shape=[_sds((T, FOX_W), BF16), _sds((T, FOX_W), BF16), _sds((T, FOX_W), BF16),
                   _sds((B, N_PAIR, 8, S), F32), _sds((T, FOX_W), F32)],
        scratch_shapes=[pltpu.VMEM((S, LANES), F32), pltpu.VMEM((S, LANES), F32)],
        compiler_params=_params(3),
    )(q, k, v, do, lse, dd, cq, ckT)


def _fgate_bwd(dc8, zf, B, S):
    T = B * S
    TB = min(512, S)
    nb = S // TB

    def body(dc_ref, zf_ref, dzf_ref, dbf_ref, carry):
        b = pl.program_id(0)
        j = pl.program_id(1)

        @pl.when((b == 0) & (j == 0))
        def _():
            dbf_ref[...] = jnp.zeros_like(dbf_ref)

        @pl.when(j == 0)
        def _():
            carry[...] = jnp.zeros_like(carry)

        dc = dc_ref[...]
        row = lax.broadcasted_iota(jnp.int32, (TB, TB), 0)
        col = lax.broadcasted_iota(jnp.int32, (TB, TB), 1)
        dlogf = _dot_hi((col >= row).astype(F32), dc) + carry[0:1, :]
        carry[0:1, :] = dlogf[0:1, :]
        lane = lax.broadcasted_iota(jnp.int32, dc.shape, 1)
        dzf = jnp.where(lane < 8, dlogf * _sig(-zf_ref[...]), 0.0)
        dzf_ref[...] = dzf.astype(BF16)
        dbf_ref[...] += jnp.sum(dzf, axis=0, keepdims=True)

    tok = pl.BlockSpec((TB, LANES), lambda b, j: (b * nb + (nb - 1 - j), 0))
    return pl.pallas_call(
        body, name="fgate_bwd", grid=(B, nb),
        in_specs=[tok, tok],
        out_specs=[tok, _acc_spec((1, LANES))],
        out_shape=[_sds((T, LANES), BF16), _sds((1, LANES), F32)],
        scratch_shapes=[pltpu.VMEM((8, LANES), F32)],
        compiler_params=_params(2),
    )(dc8, zf)


def _conv_bwd(dco, y, u, gt, cw, lng, lnb, B, S):
    T = B * S
    CB = min(256, S)
    nb = S // CB
    hb = CB // CONV_HALO

    def body(dco_ref, y_ref, u_ref, gt_ref, up_ref, gp_ref, w_ref, lg_ref, lb_ref,
             du_ref, dgt_ref, dw_ref, vec_ref, acat, dycat):
        b = pl.program_id(0)
        j = pl.program_id(1)
        jr = nb - 1 - j

        @pl.when((b == 0) & (j == 0))
        def _():
            dw_ref[...] = jnp.zeros_like(dw_ref)
            vec_ref[...] = jnp.zeros_like(vec_ref)

        @pl.when(j == 0)
        def _():
            dycat[CB:CB + CONV_HALO, :] = jnp.zeros((CONV_HALO, CONV_CH), F32)

        lg = lg_ref[...]
        rs, n, l = _layernorm_silu(y_ref[...], lg, lb_ref[...])
        sg = _sig(l)
        dl = dco_ref[...] * (sg * (1.0 + l * (1.0 - sg)))
        dn = dl * lg
        dy = rs * (dn - jnp.mean(dn, axis=-1, keepdims=True) - n * jnp.mean(dn * n, axis=-1, keepdims=True))
        vec_ref[0:1, :] += jnp.sum(dy, axis=0, keepdims=True)
        vec_ref[1:2, :] += jnp.sum(dl * n, axis=0, keepdims=True)
        vec_ref[2:3, :] += jnp.sum(dl, axis=0, keepdims=True)
        dycat[0:CB, :] = dy
        uv = u_ref[...]
        sgt = _sig(gt_ref[...])
        acat[0:CONV_HALO, :] = jnp.where(jr > 0, up_ref[...] * _sig(gp_ref[...]), 0.0)
        acat[CONV_HALO:CONV_HALO + CB, :] = uv * sgt
        da = jnp.zeros((CB, CONV_CH), F32)
        for k in range(CONV_K):
            off = CONV_HALO - (CONV_K - 1) + k
            da = da + w_ref[k:k + 1, :] * dycat[CONV_K - 1 - k:CONV_K - 1 - k + CB, :]
            dw_ref[k:k + 1, :] += jnp.sum(dy * acat[off:off + CB, :], axis=0, keepdims=True)
        dycat[CB:CB + CONV_HALO, :] = dycat[0:CONV_HALO, :]
        du_ref[...] = (da * sgt).astype(BF16)
        dgt_ref[...] = (da * uv * sgt * (1.0 - sgt)).astype(BF16)

    tok = lambda w: pl.BlockSpec((CB, w), lambda b, j: (b * nb + (nb - 1 - j), 0))
    prev = pl.BlockSpec((CONV_HALO, CONV_CH), lambda b, j: (jnp.maximum((b * nb + (nb - 1 - j)) * hb - 1, 0), 0))
    return pl.pallas_call(
        body, name="conv_bwd", grid=(B, nb),
        in_specs=[tok(CONV_CH), tok(CONV_CH), tok(CONV_CH), tok(CONV_CH), prev, prev, _resident(cw), _resident(lng),
                  _resident(lnb)],
        out_specs=[tok(CONV_CH), tok(CONV_CH), _acc_spec((CONV_HALO, CONV_CH)), _acc_spec((8, CONV_CH))],
        out_shape=[_sds((T, CONV_CH), BF16), _sds((T, CONV_CH), BF16), _sds((CONV_HALO, CONV_CH), F32),
                   _sds((8, CONV_CH), F32)],
        scratch_shapes=[pltpu.VMEM((CONV_HALO + CB, CONV_CH), F32), pltpu.VMEM((CB + CONV_HALO, CONV_CH), F32)],
        compiler_params=_params(2),
    )(dco, y, u, gt, u, gt, cw, lng, lnb)


def _bwd_in(dz, w_in, x2, dx1, g_mix, T):
    TB = min(512, T)
    nb = T // TB

    def body(dz_ref, w_ref, x_ref, d1_ref, g_ref, gx_ref, dg_ref):
        i = pl.program_id(0)

        @pl.when(i == 0)
        def _():
            dg_ref[...] = jnp.zeros_like(dg_ref)

        dh = _dot_nt(dz_ref[...], w_ref[...])
        r0, xh0 = _rms(x_ref[...])
        dx, dg_tok = _rms_bwd(dh, xh0, r0, g_ref[...])
        gx_ref[...] = d1_ref[...] + dx
        dg_ref[...] += jnp.sum(dg_tok, axis=0, keepdims=True)

    tok = lambda w: pl.BlockSpec((TB, w), lambda i: (i, 0))
    return pl.pallas_call(
        body, name="bwd_in", grid=(nb,),
        in_specs=[tok(D_IN_PAD), _resident(w_in), tok(D), tok(D), _resident(g_mix)],
        out_specs=[tok(D), _acc_spec((1, D))],
        out_shape=[_sds((T, D), F32), _sds((1, D), F32)],
        compiler_params=_params(1),
    )(dz, w_in, x2, dx1, g_mix)


def _dw(a, b, name, tn, slabs=False):
    T, K = a.shape
    N = b.shape[1]
    tk = K if K <= 1024 else K // 2
    tt = min(512, T)
    nt = T // tt

    def body(a_ref, b_ref, o_ref, acc):
        t = pl.program_id(2)

        @pl.when(t == 0)
        def _():
            acc[...] = jnp.zeros_like(acc)

        acc[...] += _dot_tn(a_ref[...].astype(BF16), b_ref[...].astype(BF16))

        @pl.when(t == nt - 1)
        def _():
            o_ref[...] = acc[...]

    return pl.pallas_call(
        body, name=name, grid=(K // tk, N // tn, nt),
        in_specs=[pl.BlockSpec((tt, tk), lambda i, j, t: (t, i)), pl.BlockSpec((tt, tn), lambda i, j, t: (t, j))],
        out_specs=(pl.BlockSpec((None, tk, tn), lambda i, j, t: (j, i, 0)) if slabs
                   else pl.BlockSpec((tk, tn), lambda i, j, t: (i, j))),
        out_shape=_sds((N // tn, K, tn) if slabs else (K, N), F32),
        scratch_shapes=[pltpu.VMEM((tk, tn), F32)],
        compiler_params=_params(3),
    )(a, b)


def _pos():
    return lax.axis_index("x"), lax.axis_index("y"), lax.axis_index("c")


def _remote(src, dst, ssem, rsem, to):
    return pltpu.make_async_remote_copy(src_ref=src, dst_ref=dst, send_sem=ssem, recv_sem=rsem, device_id=to,
                                        device_id_type=MESH)


def _half(ref_rows, c):
    H = ref_rows // 2
    return pl.ds(pl.multiple_of(c * H, 16), H)


def _allgather_weights(shards):
    n = len(shards)

    def body(*refs):
        ins, outs = refs[:n], refs[n:2 * n]
        send_sems, recv_sems, local_sems = refs[2 * n:]
        x, y, c = _pos()
        me, sib = (x, y, c), (x, y, 1 - c)
        chips = [(1 - x, y), (x, 1 - y), (1 - x, 1 - y)]

        def rows(w, px, py, pc):
            return outs[w].at[2 * px + py, _half(shards[w].shape[0], pc), :]

        def copy(w, k, block, to, src=None):
            return _remote(rows(w, *block) if src is None else src, rows(w, *block), send_sems.at[w, k],
                           recv_sems.at[w, k], to)

        mine, first, passed = [], [], []
        for w in range(n):
            src = ins[w].at[_half(shards[w].shape[0], c), :]
            mine.append(pltpu.make_async_copy(src, rows(w, *me), local_sems.at[w]))
            mine[-1].start()
            sends = [copy(w, 0, me, sib, src=src)] + [copy(w, 1 + j, me, (*chip, c), src=src) for j, chip in enumerate(chips)]
            for cp in sends:
                cp.start()
            first += sends
        for w in range(n):
            for j, chip in enumerate(chips):
                copy(w, 1 + j, (*chip, c), me).wait_recv()
                passed.append(copy(w, 4 + j, (*chip, c), sib))
                passed[-1].start()
        for w in range(n):
            copy(w, 0, sib, me).wait_recv()
            for j, chip in enumerate(chips):
                copy(w, 4 + j, (*chip, 1 - c), me).wait_recv()
        for cp in first + passed:
            cp.wait_send()
        for cp in mine:
            cp.wait()

    return pl.pallas_call(
        body, name="ag_weights", out_shape=[_sds((4,) + s.shape, s.dtype) for s in shards],
        in_specs=[ANY] * n, out_specs=[ANY] * n,
        scratch_shapes=[pltpu.SemaphoreType.DMA((n, 7)), pltpu.SemaphoreType.DMA((n, 7)), pltpu.SemaphoreType.DMA((n,))],
    )(*shards)


def _sibling_halves(gs):
    n = len(gs)

    def body(*refs):
        ins, outs = refs[:n], refs[n:2 * n]
        send_sems, recv_sems = refs[2 * n:]
        x, y, c = _pos()
        cps = []
        for w in range(n):
            for s in range(4):
                src = ins[w].at[s, _half(gs[w].shape[1], 1 - c), :]
                cps.append(_remote(src, outs[w].at[s], send_sems.at[w, s], recv_sems.at[w, s], (x, y, 1 - c)))
                cps[-1].start()
        for cp in cps:
            cp.wait()

    return pl.pallas_call(
        body, name="rs_sibling", out_shape=[_sds((4, g.shape[1] // 2, g.shape[2]), F32) for g in gs],
        in_specs=[ANY] * n, out_specs=[ANY] * n,
        scratch_shapes=[pltpu.SemaphoreType.DMA((n, 4)), pltpu.SemaphoreType.DMA((n, 4))],
    )(*gs)


def _ici_exchange(pbs):
    n = len(pbs)

    def body(*refs):
        ins, outs = refs[:n], refs[n:2 * n]
        send_sems, recv_sems = refs[2 * n:]
        x, y, c = _pos()
        cps = []
        for w in range(n):
            for j, (tx, ty) in enumerate([(1 - x, y), (x, 1 - y), (1 - x, 1 - y)]):
                cps.append(_remote(ins[w].at[2 * tx + ty], outs[w].at[j], send_sems.at[w, j], recv_sems.at[w, j],
                                   (tx, ty, c)))
                cps[-1].start()
        for cp in cps:
            cp.wait()

    return pl.pallas_call(
        body, name="rs_ici", out_shape=[_sds((3,) + p.shape[1:], BF16) for p in pbs],
        in_specs=[ANY] * n, out_specs=[ANY] * n,
        scratch_shapes=[pltpu.SemaphoreType.DMA((n, 3)), pltpu.SemaphoreType.DMA((n, 3))],
    )(*pbs)


def _sibling_share(gs):
    n = len(gs)

    def body(*refs):
        outs = refs[n:2 * n]
        send_sems, recv_sems = refs[2 * n:]
        x, y, c = _pos()
        cps = []
        for w in range(n):
            mine = outs[w].at[_half(gs[w].shape[0], c), :]
            cps.append(_remote(mine, mine, send_sems.at[w], recv_sems.at[w], (x, y, 1 - c)))
            cps[-1].start()
        for cp in cps:
            cp.wait()

    return pl.pallas_call(
        body, name="rs_share", out_shape=[_sds(g.shape, F32) for g in gs],
        in_specs=[ANY] * n, out_specs=[ANY] * n, input_output_aliases={w: w for w in range(n)},
        scratch_shapes=[pltpu.SemaphoreType.DMA((n,)), pltpu.SemaphoreType.DMA((n,))],
    )(*gs)


def _small_allreduce(v, name):
    P = v.shape[0]
    vm = pl.BlockSpec(memory_space=pltpu.VMEM)

    def body(v_ref, o_ref, gath, send_sems, recv_sems):
        x, y, c = _pos()
        me = 4 * x + 2 * y + c
        gath[me] = v_ref[...]
        cps = []
        for r in range(1, 8):
            tx = (1 - x) if r & 4 else x
            ty = (1 - y) if r & 2 else y
            tc = (1 - c) if r & 1 else c
            cps.append(_remote(v_ref, gath.at[me], send_sems.at[r - 1], recv_sems.at[r - 1], (tx, ty, tc)))
            cps[-1].start()
        for cp in cps:
            cp.wait()
        acc = gath[0]
        for d in range(1, 8):
            acc = acc + gath[d]
        o_ref[...] = acc

    return pl.pallas_call(
        body, name=name, out_shape=_sds((P, LANES), F32), in_specs=[vm], out_specs=vm,
        scratch_shapes=[pltpu.VMEM((8, P, LANES), F32), pltpu.SemaphoreType.DMA((7,)), pltpu.SemaphoreType.DMA((7,))],
    )(v)


def _chip_sum(g, rcv, pos, name):
    _, R, C = g.shape
    H = R // 2

    def body(pos_ref, g_ref, r_ref, o_ref):
        o_ref[...] = (g_ref[...] + r_ref[...]).astype(BF16)

    return pl.pallas_call(
        body, name=name, out_shape=_sds((4, H, C), BF16),
        grid_spec=pltpu.PrefetchScalarGridSpec(
            num_scalar_prefetch=1, grid=(4,),
            in_specs=[pl.BlockSpec((1, H, C), lambda s, pos: (s, pos[0], 0)),
                      pl.BlockSpec((1, H, C), lambda s, pos: (s, 0, 0))],
            out_specs=pl.BlockSpec((1, H, C), lambda s, pos: (s, 0, 0))),
        compiler_params=_params(1),
    )(pos, g, rcv)


def _final_sum(g, rcv, rc, pos, name):
    _, R, C = g.shape
    Q = R // 4

    def body(pos_ref, g_ref, r_ref, rc_ref, o_ref):
        acc = g_ref[0] + r_ref[0]
        for j in range(3):
            acc = acc + rc_ref[j].astype(F32)
        o_ref[...] = acc

    return pl.pallas_call(
        body, name=name, out_shape=_sds((R, C), F32),
        grid_spec=pltpu.PrefetchScalarGridSpec(
            num_scalar_prefetch=1, grid=(2,),
            in_specs=[pl.BlockSpec((1, Q, C), lambda i, pos: (pos[1], pos[0] * 2 + i, 0)),
                      pl.BlockSpec((1, Q, C), lambda i, pos: (pos[1], i, 0)),
                      pl.BlockSpec((3, Q, C), lambda i, pos: (0, i, 0))],
            out_specs=pl.BlockSpec((Q, C), lambda i, pos: (pos[0] * 2 + i, 0))),
        compiler_params=_params(1),
    )(pos, g, rcv, rc)


def _adamw_math(w, g, m, v):
    m = ADAM_B1 * m + (1.0 - ADAM_B1) * g
    v = ADAM_B2 * v + (1.0 - ADAM_B2) * (g * g)
    m_hat = m / (1.0 - ADAM_B1 ** ADAM_STEP)
    v_hat = v / (1.0 - ADAM_B2 ** ADAM_STEP)
    delta = -ADAM_LR * (m_hat / (jnp.sqrt(v_hat) + ADAM_EPS) + ADAM_WD * w)
    return delta, m, v


def _adamw(w, g, m, v, name, rb):
    R, C = w.shape

    def body(w_ref, g_ref, m_ref, v_ref, d_ref, nm_ref, nv_ref):
        d, nm, nv = _adamw_math(w_ref[...], g_ref[...], m_ref[...], v_ref[...])
        d_ref[...] = d
        nm_ref[...] = nm
        nv_ref[...] = nv

    blk = pl.BlockSpec((rb, C), lambda i: (i, 0))
    return pl.pallas_call(
        body, name=name, grid=(R // rb,), in_specs=[blk] * 4, out_specs=[blk] * 3,
        out_shape=[_sds((R, C), F32)] * 3, compiler_params=_params(1),
    )(w, g, m, v)


SMALL = (("g_mix", 8), ("b_f", 8), ("conv_w", None), ("conv_b", 8), ("ln_g", 8), ("ln_b", 8), ("g_x", 8), ("g_mem", 8),
         ("g_ffn", 8), ("g_final", 8), ("loss", 8))


def _pack_small(parts, conv_rows):
    rows = []
    for name, n in SMALL:
        if name not in parts:
            continue
        n = conv_rows if n is None else n
        flat = parts[name].reshape(-1).astype(F32)
        flat = jnp.pad(flat, (0, n * LANES - flat.shape[0]))
        rows.append(flat.reshape(n, LANES))
    return jnp.concatenate(rows, axis=0)


def _unpack_small(p, shapes, conv_rows):
    out, off = {}, 0
    for name, n in SMALL:
        if name not in shapes:
            continue
        n = conv_rows if n is None else n
        size = math.prod(shapes[name])
        out[name] = p[off:off + n].reshape(-1)[:size].reshape(shapes[name])
        off += n
    return out


def kernel(x, mem, g_mix, w_in, b_f, conv_w, conv_b, ln_g, ln_b, w_out, g_x, g_mem, w_mq, w_mkv, w_mo, g_ffn, w_gu, w_down, g_final, loss_target, m_g_mix, m_w_in, m_b_f, m_conv_w, m_conv_b, m_ln_g, m_ln_b, m_w_out, m_g_x, m_g_mem, m_w_mq, m_w_mkv, m_w_mo, m_g_ffn, m_w_gu, m_w_down, m_g_final, v_g_mix, v_w_in, v_b_f, v_conv_w, v_conv_b, v_ln_g, v_ln_b, v_w_out, v_g_x, v_g_mem, v_w_mq, v_w_mkv, v_w_mo, v_g_ffn, v_w_gu, v_w_down, v_g_final):
    names = ["g_mix", "w_in", "b_f", "conv_w", "conv_b", "ln_g", "ln_b", "w_out", "g_x", "g_mem", "w_mq", "w_mkv",
             "w_mo", "g_ffn", "w_gu", "w_down", "g_final"]
    W = dict(zip(names, [g_mix, w_in, b_f, conv_w, conv_b, ln_g, ln_b, w_out, g_x, g_mem, w_mq, w_mkv, w_mo, g_ffn,
                         w_gu, w_down, g_final]))
    M = dict(zip(names, [m_g_mix, m_w_in, m_b_f, m_conv_w, m_conv_b, m_ln_g, m_ln_b, m_w_out, m_g_x, m_g_mem, m_w_mq,
                         m_w_mkv, m_w_mo, m_g_ffn, m_w_gu, m_w_down, m_g_final]))
    V = dict(zip(names, [v_g_mix, v_w_in, v_b_f, v_conv_w, v_conv_b, v_ln_g, v_ln_b, v_w_out, v_g_x, v_g_mem, v_w_mq,
                         v_w_mkv, v_w_mo, v_g_ffn, v_w_gu, v_w_down, v_g_final]))
    big_names = [n for n, _, _, _ in BIG]
    B, S, _ = x.shape
    T = B * S
    mx, my, mc = _pos()
    chip = 2 * mx + my
    pos = jnp.stack([mc, chip]).astype(jnp.int32)

    shard2d = lambda a: a.reshape(a.shape[-2], a.shape[-1])
    slab = dict(zip(big_names, _allgather_weights([shard2d(W[n]).astype(BF16) for n in big_names])))
    full = {n: slab[n] if by_col else slab[n].reshape(4 * r, c) for n, r, c, by_col in BIG}
    w_in_f = jnp.pad(jnp.transpose(slab["w_in"], (1, 0, 2)).reshape(D, D_IN), ((0, 0), (0, D_IN_PAD - D_IN)))
    cw_mine = jnp.pad(shard2d(conv_w), ((0, 1), (0, 0)))
    cw_slot = lax.dynamic_update_slice(jnp.zeros((CONV_HALO, CONV_CH), F32), cw_mine, (0, chip * LANES))
    cw = _small_allreduce(cw_slot.reshape(CONV_HALO * 4, LANES) * 0.5, "gather_conv_w").reshape(CONV_HALO, CONV_CH)

    row = lambda a: a.reshape(1, -1)
    bf_pad = jnp.pad(row(b_f), ((0, 0), (0, LANES - 8)))
    x2d = x.reshape(T, D)
    mem2d = mem.reshape(B * MEM_LEN, D)
    tgt = loss_target.reshape(T, D)

    h, u, gt, q, k, v, zf, c, cq = _fwd_in(x2d, row(g_mix), w_in_f, bf_pad, B, S)
    ckT = jnp.transpose(c.reshape(B, S, LANES)[:, :, :8], (0, 2, 1)).reshape(B, N_PAIR, 2, S)
    ckT = jnp.pad(ckT, ((0, 0), (0, 0), (0, 6), (0, 0)))
    y, co = _conv_fwd(u, gt, cw, row(conv_b), row(ln_g), row(ln_b), B, S)
    o, lse = _fox_fwd(q, k, v, cq, ckT, B, S)
    mn, km, vm = _mem_kv(mem2d, row(g_mem), full["w_mkv"], B)
    x1, hx, qm, om, x2, cat = _fwd_mid(x2d, co, o, km, vm, full["w_out"], full["w_mq"], full["w_mo"], row(g_x), B, S)
    hf, gu, act, dx3, loss_p, dg_final = _fwd_ffn(x2, tgt, full["w_gu"], full["w_down"], row(g_ffn), row(g_final), T)

    dgu, dx2, dg_ffn = _bwd_ffn(dx3, gu, x2, full["w_gu"], full["w_down"], row(g_ffn), T)
    dx1, dqm, dco, do, dd, dkm, dvm, dg_x = _bwd_mid(dx2, x1, qm, km, vm, o, full["w_mo"], full["w_mq"], full["w_out"],
                                                     row(g_x), B, S)
    dkv, dg_mem = _mem_bwd(dkm, dvm, mem2d, full["w_mkv"], row(g_mem), B)
    dq, dk, dv, dc, dcq = _fox_bwd(q, k, v, do, lse, dd, cq, ckT, B, S)
    dc8 = jnp.transpose(dc[:, :, :2, :].reshape(B, 8, S), (0, 2, 1)).reshape(T, 8)
    dc8 = dc8 + dcq.reshape(T, 8, HEAD_D)[:, :, 0]
    dzf, dbf = _fgate_bwd(jnp.pad(dc8, ((0, 0), (0, LANES - 8))), zf, B, S)
    du, dgt, dcw, dvec = _conv_bwd(dco, y, u, gt, cw, row(ln_g), row(ln_b), B, S)
    dz = jnp.concatenate([du, dgt, dq, dk, dv, dzf], axis=1)
    grad_x, dg_mix = _bwd_in(dz, w_in_f, x2d, dx1, row(g_mix), T)

    dw_in = _dw(h, dz, "dw_in", 384)[:, :D_IN]
    gslab = {
        "w_in": jnp.transpose(dw_in.reshape(D, 4, D_IN // 4), (1, 0, 2)),
        "w_out": _dw(cat, dx1, "dw_out", 512).reshape(4, 256, D),
        "w_mq": _dw(hx, dqm, "dw_mq", 512).reshape(4, 256, D),
        "w_mkv": _dw(mn, dkv, "dw_mkv", 512, slabs=True),
        "w_mo": _dw(om, dx2, "dw_mo", 512).reshape(4, 256, D),
        "w_gu": _dw(hf, dgu, "dw_gu", FF_CHUNK, slabs=True),
        "w_down": _dw(act, dx3, "dw_down", 512).reshape(4, D_FF // 4, D),
    }

    gl = [gslab[n] for n in big_names]
    rcv = _sibling_halves(gl)
    pb = [_chip_sum(g, r, pos, "rs_chip_sum_" + n) for g, r, n in zip(gl, rcv, big_names)]
    rc = _ici_exchange(pb)
    rh = [_final_sum(g, r, q3, pos, "rs_final_sum_" + n) for g, r, q3, n in zip(gl, rcv, rc, big_names)]
    G = dict(zip(big_names, _sibling_share(rh)))
    DL, NM, NV = {}, {}, {}
    for n in big_names:
        DL[n], NM[n], NV[n] = _adamw(shard2d(W[n]), G[n], shard2d(M[n]), shard2d(V[n]), "adamw_" + n, G[n].shape[0] // 2)

    small_g = {"g_mix": dg_mix, "b_f": dbf[:, :8], "conv_w": dcw, "conv_b": dvec[0], "ln_g": dvec[1], "ln_b": dvec[2],
               "g_x": dg_x, "g_mem": dg_mem, "g_ffn": dg_ffn, "g_final": dg_final, "loss": loss_p[:, :1]}
    sg = _small_allreduce(_pack_small(small_g, CONV_HALO * 4), "allreduce_small")
    shapes = {n: W[n].shape for n in names if n not in big_names}
    shapes["conv_w"] = (CONV_HALO, CONV_CH)
    shapes["loss"] = (1,)
    sgrads = _unpack_small(sg, shapes, CONV_HALO * 4)
    loss = sgrads.pop("loss")[0]
    sgrads["conv_w"] = lax.dynamic_slice(sgrads["conv_w"], (0, chip * LANES), (CONV_K, LANES)).reshape(W["conv_w"].shape)
    spack = lambda d: _pack_small({n: d[n] for n in sgrads}, CONV_HALO)
    sd, snm, snv = _adamw(spack(W), spack(sgrads), spack(M), spack(V), "adamw_small", 8)
    sshapes = {n: W[n].shape for n in sgrads}
    SD, SNM, SNV = (_unpack_small(a, sshapes, CONV_HALO) for a in (sd, snm, snv))

    def collect(bigs, smalls):
        return [bigs[n].reshape(W[n].shape) if n in big_names else smalls[n] for n in names]

    return (loss, grad_x.reshape(x.shape), *collect(G, sgrads), *collect(DL, SD), *collect(NM, SNM), *collect(NV, SNV))
```

```python
import functools
import math

import jax
import jax.numpy as jnp
from jax import lax
from jax.experimental import pallas as pl
from jax.experimental.pallas import tpu as pltpu

F32, BF16 = jnp.float32, jnp.bfloat16
HIGHEST = lax.Precision.HIGHEST
MESH = pl.DeviceIdType.MESH

D = 1024
CONV_CH = 512
CONV_K = 31
CONV_HALO = 32
FOX_W = 512
HEAD_D = 64
N_PAIR = 4
MEM_LEN = 256
MEM_HEADS = 4
MEM_HD = 256
D_FF = 2816
FF_CHUNK = 1408
D_IN = 2568
D_IN_PAD = 2688
OFF_F = 2560
EPS = 1e-6
LANES = 128

ADAM_LR, ADAM_B1, ADAM_B2, ADAM_EPS, ADAM_WD, ADAM_STEP = 0.001, 0.9, 0.999, 1e-08, 0.01, 10

VMEM_LIMIT = 60 * 1024 * 1024

BIG = (("w_out", 256, 1024, False), ("w_mq", 256, 1024, False), ("w_mkv", 1024, 512, True),
       ("w_mo", 256, 1024, False), ("w_gu", 1024, 1408, True), ("w_down", 704, 1024, False),
       ("w_in", 1024, 642, True))

ANY = pl.BlockSpec(memory_space=pl.ANY)


def _sig(x):
    return 1.0 / (1.0 + jnp.exp(-x))


def _dot(a, b):
    return jnp.dot(a, b, preferred_element_type=F32)


def _dot_nt(a, b):
    return lax.dot_general(a, b, (((1,), (1,)), ((), ())), preferred_element_type=F32)


def _dot_tn(a, b):
    return lax.dot_general(a, b, (((0,), (0,)), ((), ())), preferred_element_type=F32)


def _dot_hi(a, b):
    return jnp.dot(a, b, precision=HIGHEST, preferred_element_type=F32)


def _resident(a):
    nd = a.ndim
    return pl.BlockSpec(a.shape, lambda *_: (0,) * nd, pipeline_mode=pl.Buffered(1))


def _acc_spec(shape):
    nd = len(shape)
    return pl.BlockSpec(shape, lambda *_: (0,) * nd)


def _params(n_grid):
    return pltpu.CompilerParams(dimension_semantics=("arbitrary",) * n_grid, vmem_limit_bytes=VMEM_LIMIT)


def _sds(shape, dtype):
    return jax.ShapeDtypeStruct(shape, dtype)


def _rms(x):
    r = lax.rsqrt(jnp.mean(x * x, axis=-1, keepdims=True) + EPS)
    return r, x * r


def _rms_bwd(dy, xh, r, g):
    dxh = dy * g
    dx = r * (dxh - xh * jnp.mean(dxh * xh, axis=-1, keepdims=True))
    return dx, dy * xh


def _head_expand(rows, cols):
    hd = lax.broadcasted_iota(jnp.int32, (rows, cols), 1) // HEAD_D
    hr = lax.broadcasted_iota(jnp.int32, (rows, cols), 0)
    return (hd == hr).astype(F32)


def _head_sum(n):
    hc = lax.broadcasted_iota(jnp.int32, (n, n), 1) // HEAD_D
    hr = lax.broadcasted_iota(jnp.int32, (n, n), 0) // HEAD_D
    return (hc == hr).astype(F32)


def _fwd_in(x2, g_mix, w_in, bf_pad, B, S):
    T = B * S
    TB = min(512, S)
    nb = S // TB

    def body(x_ref, g_ref, w_ref, bf_ref, h_ref, u_ref, gt_ref, q_ref, k_ref, v_ref, zf_ref, c_ref, cq_ref,
             carry):
        j = pl.program_id(1)

        @pl.when(j == 0)
        def _():
            carry[...] = jnp.zeros_like(carry)

        _, xh = _rms(x_ref[...])
        h = (xh * g_ref[...]).astype(BF16)
        h_ref[...] = h
        u_ref[...] = _dot(h, w_ref[:, 0:512])
        gt_ref[...] = _dot(h, w_ref[:, 512:1024])
        q_ref[...] = _dot(h, w_ref[:, 1024:1536]).astype(BF16)
        k_ref[...] = _dot(h, w_ref[:, 1536:2048]).astype(BF16)
        v_ref[...] = _dot(h, w_ref[:, 2048:2560]).astype(BF16)
        zf = _dot(h, w_ref[:, OFF_F:D_IN_PAD]) + bf_ref[...]
        zf_ref[...] = zf
        lane = lax.broadcasted_iota(jnp.int32, zf.shape, 1)
        logf = jnp.where(lane < 8, jnp.minimum(zf, 0.0) - jnp.log(1.0 + jnp.exp(-jnp.abs(zf))), 0.0)
        row = lax.broadcasted_iota(jnp.int32, (TB, TB), 0)
        col = lax.broadcasted_iota(jnp.int32, (TB, TB), 1)
        c = _dot_hi((row >= col).astype(F32), logf) + carry[0:1, :]
        carry[0:1, :] = c[TB - 1:TB, :]
        c_ref[...] = c
        cq_ref[...] = _dot_hi(c, _head_expand(LANES, FOX_W))

    tok = lambda w: pl.BlockSpec((TB, w), lambda b, j: (b * nb + j, 0))
    outs = [(D, BF16), (512, F32), (512, F32), (512, BF16), (512, BF16), (512, BF16), (LANES, F32),
            (LANES, F32), (FOX_W, F32)]
    return pl.pallas_call(
        body, name="fwd_in", grid=(B, nb),
        in_specs=[tok(D), _resident(g_mix), _resident(w_in), _resident(bf_pad)],
        out_specs=[tok(w) for w, _ in outs],
        out_shape=[_sds((T, w), dt) for w, dt in outs],
        scratch_shapes=[pltpu.VMEM((8, LANES), F32)],
        compiler_params=_params(2),
    )(x2, g_mix, w_in, bf_pad)


def _layernorm_silu(y, lg, lb):
    mu = jnp.mean(y, axis=-1, keepdims=True)
    yc = y - mu
    rs = lax.rsqrt(jnp.mean(yc * yc, axis=-1, keepdims=True) + EPS)
    n = yc * rs
    l = n * lg + lb
    return rs, n, l


def _conv_fwd(u, gt, cw, cb, lng, lnb, B, S):
    T = B * S
    CB = min(256, S)
    nb = S // CB

    def body(u_ref, gt_ref, w_ref, cb_ref, lg_ref, lb_ref, y_ref, co_ref, acat):
        j = pl.program_id(1)

        @pl.when(j == 0)
        def _():
            acat[0:CONV_HALO, :] = jnp.zeros((CONV_HALO, CONV_CH), F32)

        acat[CONV_HALO:CONV_HALO + CB, :] = u_ref[...] * _sig(gt_ref[...])
        acc = jnp.zeros((CB, CONV_CH), F32) + cb_ref[...]
        for k in range(CONV_K):
            off = CONV_HALO - (CONV_K - 1) + k
            acc = acc + w_ref[k:k + 1, :] * acat[off:off + CB, :]
        acat[0:CONV_HALO, :] = acat[CB:CB + CONV_HALO, :]
        y_ref[...] = acc
        _, _, l = _layernorm_silu(acc, lg_ref[...], lb_ref[...])
        co_ref[...] = (l * _sig(l)).astype(BF16)

    tok = lambda w: pl.BlockSpec((CB, w), lambda b, j: (b * nb + j, 0))
    return pl.pallas_call(
        body, name="conv_fwd", grid=(B, nb),
        in_specs=[tok(CONV_CH), tok(CONV_CH), _resident(cw), _resident(cb), _resident(lng), _resident(lnb)],
        out_specs=[tok(CONV_CH), tok(CONV_CH)],
        out_shape=[_sds((T, CONV_CH), F32), _sds((T, CONV_CH), BF16)],
        scratch_shapes=[pltpu.VMEM((CONV_HALO + CB, CONV_CH), F32)],
        compiler_params=_params(2),
    )(u, gt, cw, cb, lng, lnb)


def _fox_fwd(q, k, v, cq, ckT, B, S):
    T = B * S
    TQ = min(256, S)
    nq = S // TQ
    scale = 1.0 / math.sqrt(HEAD_D)
    one_lane = (HEAD_D, 0)

    def body(q_ref, k_ref, v_ref, cq_ref, ck_ref, o_ref, lse_ref, s_scr, s_odd, m_scr, acc_scr):
        i = pl.program_id(2)
        lane = lax.broadcasted_iota(jnp.int32, (TQ, LANES), 1)
        lo = lane < HEAD_D
        qs = q_ref[...] * jnp.asarray(scale, BF16)
        zero = jnp.zeros_like(qs)
        qh = (jnp.where(lo, qs, zero), jnp.where(lo, zero, qs))
        cqv = cq_ref[...]
        cq_rep = tuple(jnp.broadcast_to(cqv[:, h * HEAD_D:h * HEAD_D + 1], (TQ, LANES)) for h in range(2))
        m_scr[...] = jnp.full(m_scr.shape, -1e30, F32)
        acc_scr[...] = jnp.zeros_like(acc_scr)
        row = lax.broadcasted_iota(jnp.int32, (TQ, TQ), 0)
        col = lax.broadcasted_iota(jnp.int32, (TQ, TQ), 1)
        wide = lambda x: jnp.concatenate([x, x], axis=1) if TQ == 2 * LANES else jnp.tile(x, (1, TQ // LANES))

        def scores(j, s_buf):
            kj = k_ref[pl.ds(pl.multiple_of(j * TQ, TQ), TQ), :]
            for h in range(2):
                s_buf[h] = _dot_nt(qh[h], kj)

        def softmax_step(j, s_buf, diagonal):
            start = pl.multiple_of(j * TQ, TQ)
            vj = v_ref[pl.ds(start, TQ), :]
            for h in range(2):
                ck = ck_ref[0, 0, h:h + 1, pl.ds(start, TQ)]

                def logits():
                    t = (s_buf[h] - ck) + wide(cq_rep[h])
                    return jnp.where(col <= row, t, -1e30) if diagonal else t

                m_old = m_scr[h]
                m_new = jnp.maximum(m_old, jnp.max(logits(), axis=-1, keepdims=True))
                alpha = jnp.exp(m_old - m_new)
                m_scr[h] = m_new
                p = jnp.exp(logits() - wide(m_new)).astype(BF16)
                vx = jnp.where(lane == one_lane[h], jnp.ones_like(vj), jnp.where(lo if h == 0 else ~lo, vj, jnp.zeros_like(vj)))
                acc_scr[h] = alpha * acc_scr[h] + _dot(p, vx)

        def two_blocks(jj, carry):
            j = 2 * jj
            scores(j + 1, s_odd)
            softmax_step(j, s_scr, False)
            scores(j + 2, s_scr)
            softmax_step(j + 1, s_odd, False)
            return carry

        scores(0, s_scr)
        lax.fori_loop(0, i // 2, two_blocks, 0)

        @pl.when(i % 2 == 0)
        def _():
            softmax_step(i, s_scr, True)

        @pl.when(i % 2 == 1)
        def _():
            scores(i, s_odd)
            softmax_step(i - 1, s_scr, False)
            softmax_step(i, s_odd, True)

        acc_a, acc_b = acc_scr[0], acc_scr[1]
        l_a = acc_a[:, one_lane[0]:one_lane[0] + 1]
        l_b = acc_b[:, one_lane[1]:one_lane[1] + 1]
        o_ref[...] = jnp.where(lo, acc_a / l_a, acc_b / l_b)
        lse_ref[...] = cqv - jnp.where(lo, m_scr[0] + jnp.log(l_a), m_scr[1] + jnp.log(l_b))

    qspec = pl.BlockSpec((TQ, LANES), lambda b, p, i: (b * nq + i, p))
    kspec = pl.BlockSpec((S, LANES), lambda b, p, i: (b, p))
    return pl.pallas_call(
        body, name="fox_fwd", grid=(B, N_PAIR, nq),
        in_specs=[qspec, kspec, kspec, qspec, pl.BlockSpec((1, 1, 8, S), lambda b, p, i: (b, p, 0, 0))],
        out_specs=[qspec, qspec],
        out_shape=[_sds((T, FOX_W), F32), _sds((T, FOX_W), F32)],
        scratch_shapes=[pltpu.VMEM((2, TQ, TQ), F32), pltpu.VMEM((2, TQ, TQ), F32),
                        pltpu.VMEM((2, TQ, LANES), F32), pltpu.VMEM((2, TQ, LANES), F32)],
        compiler_params=_params(3),
    )(q, k, v, cq, ckT)


def _mem_kv(mem2, g_mem, w_mkv, B):
    def body(m_ref, g_ref, w_ref, mn_ref, km_ref, vm_ref):
        _, xh = _rms(m_ref[...])
        mn = (xh * g_ref[...]).astype(BF16)
        mn_ref[...] = mn
        for s in range(2):
            km_ref[:, 512 * s:512 * (s + 1)] = _dot(mn, w_ref[s]).astype(BF16)
            vm_ref[:, 512 * s:512 * (s + 1)] = _dot(mn, w_ref[2 + s]).astype(BF16)

    blk = pl.BlockSpec((MEM_LEN, D), lambda b: (b, 0))
    return pl.pallas_call(
        body, name="mem_kv", grid=(B,),
        in_specs=[blk, _resident(g_mem), _resident(w_mkv)],
        out_specs=[blk, blk, blk],
        out_shape=[_sds((B * MEM_LEN, D), BF16)] * 3,
        compiler_params=_params(1),
    )(mem2, g_mem, w_mkv)


def _mem_probs(qm, km):
    ps = []
    for h in range(MEM_HEADS):
        hs = slice(h * MEM_HD, (h + 1) * MEM_HD)
        lg = _dot_nt(qm[:, hs], km[:, hs]) * (1.0 / math.sqrt(MEM_HD))
        e = jnp.exp(lg - jnp.max(lg, axis=-1, keepdims=True))
        ps.append(e / jnp.sum(e, axis=-1, keepdims=True))
    return ps


def _fwd_mid(x2, co, o, km, vm, w_out, w_mq, w_mo, g_x, B, S):
    T = B * S
    TB = min(512, S)
    nb = S // TB

    def body(x_ref, co_ref, o_ref, km_ref, vm_ref, wo_ref, wq_ref, wm_ref, g_ref,
             x1_ref, hx_ref, qm_ref, om_ref, x2_ref, cat_ref):
        cat_ref[:, 0:CONV_CH] = co_ref[...]
        cat_ref[:, CONV_CH:D] = o_ref[...].astype(BF16)
        x1 = x_ref[...] + _dot(cat_ref[...], wo_ref[...])
        x1_ref[...] = x1
        _, xh = _rms(x1)
        hx = (xh * g_ref[...]).astype(BF16)
        hx_ref[...] = hx
        qm = _dot(hx, wq_ref[...]).astype(BF16)
        qm_ref[...] = qm
        ps = _mem_probs(qm, km_ref[...])
        vmv = vm_ref[...]
        for h in range(MEM_HEADS):
            hs = slice(h * MEM_HD, (h + 1) * MEM_HD)
            om_ref[:, hs] = _dot(ps[h].astype(BF16), vmv[:, hs]).astype(BF16)
        x2_ref[...] = x1 + _dot(om_ref[...], wm_ref[...])

    tok = lambda w: pl.BlockSpec((TB, w), lambda b, j: (b * nb + j, 0))
    memb = pl.BlockSpec((MEM_LEN, D), lambda b, j: (b, 0))
    outs = [(D, F32), (D, BF16), (D, BF16), (D, BF16), (D, F32), (D, BF16)]
    return pl.pallas_call(
        body, name="fwd_mid", grid=(B, nb),
        in_specs=[tok(D), tok(CONV_CH), tok(FOX_W), memb, memb, _resident(w_out), _resident(w_mq), _resident(w_mo),
                  _resident(g_x)],
        out_specs=[tok(w) for w, _ in outs],
        out_shape=[_sds((T, w), dt) for w, dt in outs],
        compiler_params=_params(2),
    )(x2, co, o, km, vm, w_out, w_mq, w_mo, g_x)


def _fwd_ffn(x2, tgt, w_gu, w_down, g_ffn, g_final, T):
    TB = min(256, T)
    nb = T // TB

    def body(x_ref, t_ref, wgu_ref, wd_ref, gf_ref, gl_ref, hf_ref, gu_ref, act_ref, dx3_ref, loss_ref, dgl_ref):
        i = pl.program_id(0)

        @pl.when(i == 0)
        def _():
            loss_ref[...] = jnp.zeros_like(loss_ref)
            dgl_ref[...] = jnp.zeros_like(dgl_ref)

        x2v = x_ref[...]
        _, xh = _rms(x2v)
        hf = (xh * gf_ref[...]).astype(BF16)
        hf_ref[...] = hf
        x3 = x2v
        for ch in range(D_FF // FF_CHUNK):
            c0 = ch * FF_CHUNK
            g = _dot(hf, wgu_ref[ch])
            u = _dot(hf, wgu_ref[2 + ch])
            gu_ref[:, c0:c0 + FF_CHUNK] = g
            gu_ref[:, D_FF + c0:D_FF + c0 + FF_CHUNK] = u
            act = (g * _sig(g) * u).astype(BF16)
            act_ref[:, c0:c0 + FF_CHUNK] = act
            x3 = x3 + _dot(act, wd_ref[c0:c0 + FF_CHUNK, :])
        r3, xh3 = _rms(x3)
        gl = gl_ref[...]
        e = xh3 * gl - t_ref[...]
        loss_ref[...] += jnp.sum(e * e) * (0.5 / D)
        dy = e * (1.0 / D)
        dx3, dgl = _rms_bwd(dy, xh3, r3, gl)
        dx3_ref[...] = dx3
        dgl_ref[...] += jnp.sum(dgl, axis=0, keepdims=True)

    tok = lambda w: pl.BlockSpec((TB, w), lambda i: (i, 0))
    return pl.pallas_call(
        body, name="fwd_ffn", grid=(nb,),
        in_specs=[tok(D), tok(D), _resident(w_gu), _resident(w_down), _resident(g_ffn), _resident(g_final)],
        out_specs=[tok(D), tok(2 * D_FF), tok(D_FF), tok(D), _acc_spec((1, LANES)), _acc_spec((1, D))],
        out_shape=[_sds((T, D), BF16), _sds((T, 2 * D_FF), F32), _sds((T, D_FF), BF16), _sds((T, D), F32),
                   _sds((1, LANES), F32), _sds((1, D), F32)],
        compiler_params=_params(1),
    )(x2, tgt, w_gu, w_down, g_ffn, g_final)


def _bwd_ffn(dx3, gu, x2, w_gu, w_down, g_ffn, T):
    TB = min(256, T)
    nb = T // TB

    def body(d_ref, gu_ref, x_ref, wgu_ref, wd_ref, gf_ref, dgu_ref, dx2_ref, dgf_ref):
        i = pl.program_id(0)

        @pl.when(i == 0)
        def _():
            dgf_ref[...] = jnp.zeros_like(dgf_ref)

        dx3v = d_ref[...]
        db = dx3v.astype(BF16)
        dhf = jnp.zeros((TB, D), F32)
        for ch in range(D_FF // FF_CHUNK):
            c0 = ch * FF_CHUNK
            dact = _dot_nt(db, wd_ref[c0:c0 + FF_CHUNK, :])
            g = gu_ref[:, c0:c0 + FF_CHUNK]
            u = gu_ref[:, D_FF + c0:D_FF + c0 + FF_CHUNK]
            sg = _sig(g)
            dg = (dact * u * sg * (1.0 + g * (1.0 - sg))).astype(BF16)
            du = (dact * g * sg).astype(BF16)
            dgu_ref[:, c0:c0 + FF_CHUNK] = dg
            dgu_ref[:, D_FF + c0:D_FF + c0 + FF_CHUNK] = du
            dhf = dhf + _dot_nt(dg, wgu_ref[ch]) + _dot_nt(du, wgu_ref[2 + ch])
        r2, xh2 = _rms(x_ref[...])
        dx, dg_tok = _rms_bwd(dhf, xh2, r2, gf_ref[...])
        dx2_ref[...] = dx3v + dx
        dgf_ref[...] += jnp.sum(dg_tok, axis=0, keepdims=True)

    tok = lambda w: pl.BlockSpec((TB, w), lambda i: (i, 0))
    return pl.pallas_call(
        body, name="bwd_ffn", grid=(nb,),
        in_specs=[tok(D), tok(2 * D_FF), tok(D), _resident(w_gu), _resident(w_down), _resident(g_ffn)],
        out_specs=[tok(2 * D_FF), tok(D), _acc_spec((1, D))],
        out_shape=[_sds((T, 2 * D_FF), BF16), _sds((T, D), F32), _sds((1, D), F32)],
        compiler_params=_params(1),
    )(dx3, gu, x2, w_gu, w_down, g_ffn)


def _bwd_mid(dx2, x1, qm, km, vm, o, w_mo, w_mq, w_out, g_x, B, S):
    T = B * S
    TB = min(512, S)
    nb = S // TB
    inv = 1.0 / math.sqrt(MEM_HD)

    def body(d_ref, x1_ref, qm_ref, km_ref, vm_ref, o_ref, wm_ref, wq_ref, wo_ref, g_ref,
             dx1_ref, dqm_ref, dco_ref, do_ref, dd_ref, dkm_ref, dvm_ref, dgx_ref):
        b = pl.program_id(0)
        j = pl.program_id(1)

        @pl.when((b == 0) & (j == 0))
        def _():
            dgx_ref[...] = jnp.zeros_like(dgx_ref)

        @pl.when(j == 0)
        def _():
            dkm_ref[...] = jnp.zeros_like(dkm_ref)
            dvm_ref[...] = jnp.zeros_like(dvm_ref)

        dx2v = d_ref[...]
        dom = _dot_nt(dx2v.astype(BF16), wm_ref[...]).astype(BF16)
        qmv = qm_ref[...]
        kmv = km_ref[...]
        vmv = vm_ref[...]
        ps = _mem_probs(qmv, kmv)
        for h in range(MEM_HEADS):
            hs = slice(h * MEM_HD, (h + 1) * MEM_HD)
            p = ps[h]
            dp = _dot_nt(dom[:, hs], vmv[:, hs])
            ds = (p * (dp - jnp.sum(p * dp, axis=-1, keepdims=True))).astype(BF16)
            dqm_ref[:, hs] = (_dot(ds, kmv[:, hs]) * inv).astype(BF16)
            dkm_ref[:, hs] += _dot_tn(ds, qmv[:, hs]) * inv
            dvm_ref[:, hs] += _dot_tn(p.astype(BF16), dom[:, hs])
        dhx = _dot_nt(dqm_ref[...], wq_ref[...])
        r1, xh1 = _rms(x1_ref[...])
        dx, dg_tok = _rms_bwd(dhx, xh1, r1, g_ref[...])
        dx1 = dx2v + dx
        dx1_ref[...] = dx1
        dgx_ref[...] += jnp.sum(dg_tok, axis=0, keepdims=True)
        d1b = dx1.astype(BF16)
        dco_ref[...] = _dot_nt(d1b, wo_ref[0:CONV_CH, :])
        do = _dot_nt(d1b, wo_ref[CONV_CH:D, :])
        dob = do.astype(BF16)
        do_ref[...] = dob
        dd_ref[...] = _dot_hi(dob.astype(F32) * o_ref[...], _head_sum(FOX_W))

    tok = lambda w: pl.BlockSpec((TB, w), lambda b, j: (b * nb + j, 0))
    memb = pl.BlockSpec((MEM_LEN, D), lambda b, j: (b, 0))
    outs = [(D, F32), (D, BF16), (CONV_CH, F32), (FOX_W, BF16), (FOX_W, F32)]
    return pl.pallas_call(
        body, name="bwd_mid", grid=(B, nb),
        in_specs=[tok(D), tok(D), tok(D), memb, memb, tok(FOX_W), _resident(w_mo), _resident(w_mq), _resident(w_out),
                  _resident(g_x)],
        out_specs=[tok(w) for w, _ in outs] + [memb, memb, _acc_spec((1, D))],
        out_shape=[_sds((T, w), dt) for w, dt in outs] + [_sds((B * MEM_LEN, D), F32)] * 2 + [_sds((1, D), F32)],
        compiler_params=_params(2),
    )(dx2, x1, qm, km, vm, o, w_mo, w_mq, w_out, g_x)


def _mem_bwd(dkm, dvm, mem2, w_mkv, g_mem, B):
    def body(dk_ref, dv_ref, m_ref, w_ref, g_ref, dkv_ref, dg_ref):
        b = pl.program_id(0)

        @pl.when(b == 0)
        def _():
            dg_ref[...] = jnp.zeros_like(dg_ref)

        dk = dk_ref[...].astype(BF16)
        dv = dv_ref[...].astype(BF16)
        dkv_ref[:, 0:D] = dk
        dkv_ref[:, D:2 * D] = dv
        dmn = jnp.zeros((MEM_LEN, D), F32)
        for s in range(2):
            dmn = dmn + _dot_nt(dk[:, 512 * s:512 * (s + 1)], w_ref[s]) + _dot_nt(dv[:, 512 * s:512 * (s + 1)], w_ref[2 + s])
        _, xh = _rms(m_ref[...])
        dg_ref[...] += jnp.sum(dmn * xh, axis=0, keepdims=True)

    blk = pl.BlockSpec((MEM_LEN, D), lambda b: (b, 0))
    return pl.pallas_call(
        body, name="mem_bwd", grid=(B,),
        in_specs=[blk, blk, blk, _resident(w_mkv), _resident(g_mem)],
        out_specs=[pl.BlockSpec((MEM_LEN, 2 * D), lambda b: (b, 0)), _acc_spec((1, D))],
        out_shape=[_sds((B * MEM_LEN, 2 * D), BF16), _sds((1, D), F32)],
        compiler_params=_params(1),
    )(dkm, dvm, mem2, w_mkv, g_mem)


def _fox_bwd(q, k, v, do, bias, dd, ckT, B, S):
    T = B * S
    TK = min(256, S)
    nk = S // TK
    scale = 1.0 / math.sqrt(HEAD_D)

    def body(q_ref, k_ref, v_ref, do_ref, bias_ref, dd_ref, ck_ref, dq_ref, dk_ref, dv_ref, dc_ref, dcq_ref,
             dq_acc, rs_acc, s_scr, dp_scr, s_odd, dp_odd, dk_acc, dv_acc, dc_acc):
        j = pl.program_id(2)

        @pl.when(j == 0)
        def _():
            dq_acc[...] = jnp.zeros_like(dq_acc)
            rs_acc[...] = jnp.zeros_like(rs_acc)

        dk_acc[...] = jnp.zeros_like(dk_acc)
        dv_acc[...] = jnp.zeros_like(dv_acc)
        dc_acc[...] = jnp.zeros_like(dc_acc)
        lane = lax.broadcasted_iota(jnp.int32, (TK, LANES), 1)
        lo = lane < HEAD_D
        ks = k_ref[...] * jnp.asarray(scale, BF16)
        v2 = v_ref[...]
        zero = jnp.zeros_like(ks)
        kh = (jnp.where(lo, ks, zero), jnp.where(lo, zero, ks))
        vh = (jnp.where(lo, v2, zero), jnp.where(lo, zero, v2))
        kstart = pl.multiple_of(j * TK, TK)
        ckh = tuple(ck_ref[0, 0, h:h + 1, pl.ds(kstart, TK)] for h in range(2))
        row = lax.broadcasted_iota(jnp.int32, (TK, TK), 0)
        col = lax.broadcasted_iota(jnp.int32, (TK, TK), 1)
        wide = lambda x: jnp.concatenate([x, x], axis=1) if TK == 2 * LANES else jnp.tile(x, (1, TK // LANES))

        def scores(i, s_buf, dp_buf):
            start = pl.multiple_of(i * TK, TK)
            qi = q_ref[pl.ds(start, TK), :]
            doi = do_ref[pl.ds(start, TK), :]
            for h in range(2):
                s_buf[h] = _dot_nt(qi, kh[h])
                dp_buf[h] = _dot_nt(doi, vh[h])

        def grads(i, s_buf, dp_buf, diagonal):
            start = pl.multiple_of(i * TK, TK)
            qi = q_ref[pl.ds(start, TK), :]
            doi = do_ref[pl.ds(start, TK), :]
            bias2 = bias_ref[pl.ds(start, TK), :]
            dd2 = dd_ref[pl.ds(start, TK), :]
            for h in range(2):
                hc = slice(h * HEAD_D, h * HEAD_D + 1)
                bias = jnp.broadcast_to(bias2[:, hc], (TK, LANES))
                ddh = jnp.broadcast_to(dd2[:, hc], (TK, LANES))
                p = jnp.exp((s_buf[h] - ckh[h]) + wide(bias))
                if diagonal:
                    p = jnp.where(col <= row, p, 0.0)
                ds = p * (dp_buf[h] - wide(ddh))
                dc_acc[h, 0:1, :] += jnp.sum(ds, axis=0, keepdims=True)
                rs_acc[h, pl.ds(start, TK), :] += jnp.sum(ds, axis=1, keepdims=True)
                pb = p.astype(BF16)
                dsb = ds.astype(BF16)
                dv_acc[h] += _dot_tn(pb, doi)
                dk_acc[h] += _dot_tn(dsb, qi)
                dq_acc[pl.ds(start, TK), :] += _dot(dsb, kh[h])

        n_off = nk - 1 - j
        block = lambda t: jnp.where(t < n_off, j + 1 + t, j)

        def two_blocks(tt, carry):
            t = 2 * tt
            scores(block(t + 1), s_odd, dp_odd)
            grads(block(t), s_scr, dp_scr, False)
            scores(block(t + 2), s_scr, dp_scr)
            grads(block(t + 1), s_odd, dp_odd, False)
            return carry

        scores(block(0), s_scr, dp_scr)
        lax.fori_loop(0, n_off // 2, two_blocks, 0)

        @pl.when(n_off % 2 == 0)
        def _():
            grads(j, s_scr, dp_scr, True)

        @pl.when(n_off % 2 == 1)
        def _():
            scores(j, s_odd, dp_odd)
            grads(nk - 1, s_scr, dp_scr, False)
            grads(j, s_odd, dp_odd, True)

        dk_ref[...] = (jnp.where(lo, dk_acc[0], dk_acc[1]) * scale).astype(BF16)
        dv_ref[...] = jnp.where(lo, dv_acc[0], dv_acc[1]).astype(BF16)
        sub = lax.broadcasted_iota(jnp.int32, (8, TK), 0)
        dca = dc_acc[0, 0:1, :]
        dcb = dc_acc[1, 0:1, :]
        dc_ref[0, 0] = jnp.where(sub == 0, -dca, jnp.where(sub == 1, -dcb, 0.0))

        @pl.when(j == nk - 1)
        def _():
            dq_ref[...] = dq_acc[...].astype(BF16)
            lo_s = lax.broadcasted_iota(jnp.int32, (S, LANES), 1) < HEAD_D
            dcq_ref[...] = jnp.where(lo_s, rs_acc[0], rs_acc[1])

    full = pl.BlockSpec((S, LANES), lambda b, p, j: (b, p))
    blk = pl.BlockSpec((TK, LANES), lambda b, p, j: (b * nk + j, p))
    return pl.pallas_call(
        body, name="fox_bwd", grid=(B, N_PAIR, nk),
        in_specs=[full, blk, blk, full, full, full, pl.BlockSpec((1, 1, 8, S), lambda b, p, j: (b, p, 0, 0))],
        out_specs=[full, blk, blk, pl.BlockSpec((1, 1, 8, TK), lambda b, p, j: (b, p, 0, j)), full],
        out_shape=[_sds((T, FOX_W), BF16), _sds((T, FOX_W), BF16), _sds((T, FOX_W), BF16),
                   _sds((B, N_PAIR, 8, S), F32), _sds((T, FOX_W), F32)],
        scratch_shapes=[pltpu.VMEM((S, LANES), F32), pltpu.VMEM((2, S, 1), F32),
                        pltpu.VMEM((2, TK, TK), F32), pltpu.VMEM((2, TK, TK), F32),
                        pltpu.VMEM((2, TK, TK), F32), pltpu.VMEM((2, TK, TK), F32),
                        pltpu.VMEM((2, TK, LANES), F32), pltpu.VMEM((2, TK, LANES), F32), pltpu.VMEM((2, 8, TK), F32)],
        compiler_params=_params(3),
    )(q, k, v, do, bias, dd, ckT)


def _fgate_bwd(dc8, zf, B, S):
    T = B * S
    TB = min(512, S)
    nb = S // TB

    def body(dc_ref, zf_ref, dzf_ref, dbf_ref, carry):
        b = pl.program_id(0)
        j = pl.program_id(1)

        @pl.when((b == 0) & (j == 0))
        def _():
            dbf_ref[...] = jnp.zeros_like(dbf_ref)

        @pl.when(j == 0)
        def _():
            carry[...] = jnp.zeros_like(carry)

        dc = dc_ref[...]
        row = lax.broadcasted_iota(jnp.int32, (TB, TB), 0)
        col = lax.broadcasted_iota(jnp.int32, (TB, TB), 1)
        dlogf = _dot_hi((col >= row).astype(F32), dc) + carry[0:1, :]
        carry[0:1, :] = dlogf[0:1, :]
        lane = lax.broadcasted_iota(jnp.int32, dc.shape, 1)
        dzf = jnp.where(lane < 8, dlogf * _sig(-zf_ref[...]), 0.0)
        dzf_ref[...] = dzf.astype(BF16)
        dbf_ref[...] += jnp.sum(dzf, axis=0, keepdims=True)

    tok = pl.BlockSpec((TB, LANES), lambda b, j: (b * nb + (nb - 1 - j), 0))
    return pl.pallas_call(
        body, name="fgate_bwd", grid=(B, nb),
        in_specs=[tok, tok],
        out_specs=[tok, _acc_spec((1, LANES))],
        out_shape=[_sds((T, LANES), BF16), _sds((1, LANES), F32)],
        scratch_shapes=[pltpu.VMEM((8, LANES), F32)],
        compiler_params=_params(2),
    )(dc8, zf)


def _conv_bwd(dco, y, u, gt, cw, lng, lnb, B, S):
    T = B * S
    CB = min(256, S)
    nb = S // CB
    hb = CB // CONV_HALO

    def body(dco_ref, y_ref, u_ref, gt_ref, up_ref, gp_ref, w_ref, lg_ref, lb_ref,
             du_ref, dgt_ref, dw_ref, vec_ref, acat, dycat):
        b = pl.program_id(0)
        j = pl.program_id(1)
        jr = nb - 1 - j

        @pl.when((b == 0) & (j == 0))
        def _():
            dw_ref[...] = jnp.zeros_like(dw_ref)
            vec_ref[...] = jnp.zeros_like(vec_ref)

        @pl.when(j == 0)
        def _():
            dycat[CB:CB + CONV_HALO, :] = jnp.zeros((CONV_HALO, CONV_CH), F32)

        lg = lg_ref[...]
        rs, n, l = _layernorm_silu(y_ref[...], lg, lb_ref[...])
        sg = _sig(l)
        dl = dco_ref[...] * (sg * (1.0 + l * (1.0 - sg)))
        dn = dl * lg
        dy = rs * (dn - jnp.mean(dn, axis=-1, keepdims=True) - n * jnp.mean(dn * n, axis=-1, keepdims=True))
        vec_ref[0:1, :] += jnp.sum(dy, axis=0, keepdims=True)
        vec_ref[1:2, :] += jnp.sum(dl * n, axis=0, keepdims=True)
        vec_ref[2:3, :] += jnp.sum(dl, axis=0, keepdims=True)
        dycat[0:CB, :] = dy
        uv = u_ref[...]
        sgt = _sig(gt_ref[...])
        acat[0:CONV_HALO, :] = jnp.where(jr > 0, up_ref[...] * _sig(gp_ref[...]), 0.0)
        acat[CONV_HALO:CONV_HALO + CB, :] = uv * sgt
        da = jnp.zeros((CB, CONV_CH), F32)
        for k in range(CONV_K):
            off = CONV_HALO - (CONV_K - 1) + k
            da = da + w_ref[k:k + 1, :] * dycat[CONV_K - 1 - k:CONV_K - 1 - k + CB, :]
            dw_ref[k:k + 1, :] += jnp.sum(dy * acat[off:off + CB, :], axis=0, keepdims=True)
        dycat[CB:CB + CONV_HALO, :] = dycat[0:CONV_HALO, :]
        du_ref[...] = (da * sgt).astype(BF16)
        dgt_ref[...] = (da * uv * sgt * (1.0 - sgt)).astype(BF16)

    tok = lambda w: pl.BlockSpec((CB, w), lambda b, j: (b * nb + (nb - 1 - j), 0))
    prev = pl.BlockSpec((CONV_HALO, CONV_CH), lambda b, j: (jnp.maximum((b * nb + (nb - 1 - j)) * hb - 1, 0), 0))
    return pl.pallas_call(
        body, name="conv_bwd", grid=(B, nb),
        in_specs=[tok(CONV_CH), tok(CONV_CH), tok(CONV_CH), tok(CONV_CH), prev, prev, _resident(cw), _resident(lng),
                  _resident(lnb)],
        out_specs=[tok(CONV_CH), tok(CONV_CH), _acc_spec((CONV_HALO, CONV_CH)), _acc_spec((8, CONV_CH))],
        out_shape=[_sds((T, CONV_CH), BF16), _sds((T, CONV_CH), BF16), _sds((CONV_HALO, CONV_CH), F32),
                   _sds((8, CONV_CH), F32)],
        scratch_shapes=[pltpu.VMEM((CONV_HALO + CB, CONV_CH), F32), pltpu.VMEM((CB + CONV_HALO, CONV_CH), F32)],
        compiler_params=_params(2),
    )(dco, y, u, gt, u, gt, cw, lng, lnb)


def _bwd_in(dz, w_in, x2, dx1, g_mix, T):
    TB = min(512, T)
    nb = T // TB

    def body(dz_ref, w_ref, x_ref, d1_ref, g_ref, gx_ref, dg_ref):
        i = pl.program_id(0)

        @pl.when(i == 0)
        def _():
            dg_ref[...] = jnp.zeros_like(dg_ref)

        dh = _dot_nt(dz_ref[...], w_ref[...])
        r0, xh0 = _rms(x_ref[...])
        dx, dg_tok = _rms_bwd(dh, xh0, r0, g_ref[...])
        gx_ref[...] = d1_ref[...] + dx
        dg_ref[...] += jnp.sum(dg_tok, axis=0, keepdims=True)

    tok = lambda w: pl.BlockSpec((TB, w), lambda i: (i, 0))
    return pl.pallas_call(
        body, name="bwd_in", grid=(nb,),
        in_specs=[tok(D_IN_PAD), _resident(w_in), tok(D), tok(D), _resident(g_mix)],
        out_specs=[tok(D), _acc_spec((1, D))],
        out_shape=[_sds((T, D), F32), _sds((1, D), F32)],
        compiler_params=_params(1),
    )(dz, w_in, x2, dx1, g_mix)


def _dw(a, b, name, tn, slabs=False):
    T, K = a.shape
    N = b.shape[1]
    tk = K if K <= 1024 else K // 2
    tt = min(512, T)
    nt = T // tt

    def body(a_ref, b_ref, o_ref, acc):
        t = pl.program_id(2)

        @pl.when(t == 0)
        def _():
            acc[...] = jnp.zeros_like(acc)

        acc[...] += _dot_tn(a_ref[...].astype(BF16), b_ref[...].astype(BF16))

        @pl.when(t == nt - 1)
        def _():
            o_ref[...] = acc[...]

    return pl.pallas_call(
        body, name=name, grid=(K // tk, N // tn, nt),
        in_specs=[pl.BlockSpec((tt, tk), lambda i, j, t: (t, i)), pl.BlockSpec((tt, tn), lambda i, j, t: (t, j))],
        out_specs=(pl.BlockSpec((None, tk, tn), lambda i, j, t: (j, i, 0)) if slabs
                   else pl.BlockSpec((tk, tn), lambda i, j, t: (i, j))),
        out_shape=_sds((N // tn, K, tn) if slabs else (K, N), F32),
        scratch_shapes=[pltpu.VMEM((tk, tn), F32)],
        compiler_params=_params(3),
    )(a, b)


def _pos():
    return lax.axis_index("x"), lax.axis_index("y"), lax.axis_index("c")


def _remote(src, dst, ssem, rsem, to):
    return pltpu.make_async_remote_copy(src_ref=src, dst_ref=dst, send_sem=ssem, recv_sem=rsem, device_id=to,
                                        device_id_type=MESH)


def _half(ref_rows, c):
    H = ref_rows // 2
    return pl.ds(pl.multiple_of(c * H, 16), H)


def _allgather_weights(shards):
    n = len(shards)

    def body(*refs):
        ins, outs = refs[:n], refs[n:2 * n]
        send_sems, recv_sems, local_sems = refs[2 * n:]
        x, y, c = _pos()
        me, sib = (x, y, c), (x, y, 1 - c)
        chips = [(1 - x, y), (x, 1 - y), (1 - x, 1 - y)]

        def rows(w, px, py, pc):
            return outs[w].at[2 * px + py, _half(shards[w].shape[0], pc), :]

        def copy(w, k, block, to, src=None):
            return _remote(rows(w, *block) if src is None else src, rows(w, *block), send_sems.at[w, k],
                           recv_sems.at[w, k], to)

        mine, first, passed = [], [], []
        for w in range(n):
            src = ins[w].at[_half(shards[w].shape[0], c), :]
            mine.append(pltpu.make_async_copy(src, rows(w, *me), local_sems.at[w]))
            mine[-1].start()
            sends = [copy(w, 0, me, sib, src=src)] + [copy(w, 1 + j, me, (*chip, c), src=src) for j, chip in enumerate(chips)]
            for cp in sends:
                cp.start()
            first += sends
        for w in range(n):
            for j, chip in enumerate(chips):
                copy(w, 1 + j, (*chip, c), me).wait_recv()
                passed.append(copy(w, 4 + j, (*chip, c), sib))
                passed[-1].start()
        for w in range(n):
            copy(w, 0, sib, me).wait_recv()
            for j, chip in enumerate(chips):
                copy(w, 4 + j, (*chip, 1 - c), me).wait_recv()
        for cp in first + passed:
            cp.wait_send()
        for cp in mine:
            cp.wait()

    return pl.pallas_call(
        body, name="ag_weights", out_shape=[_sds((4,) + s.shape, s.dtype) for s in shards],
        in_specs=[ANY] * n, out_specs=[ANY] * n,
        scratch_shapes=[pltpu.SemaphoreType.DMA((n, 7)), pltpu.SemaphoreType.DMA((n, 7)), pltpu.SemaphoreType.DMA((n,))],
    )(*shards)


def _sibling_halves(gs):
    n = len(gs)

    def body(*refs):
        ins, outs = refs[:n], refs[n:2 * n]
        send_sems, recv_sems = refs[2 * n:]
        x, y, c = _pos()
        cps = []
        for w in range(n):
            for s in range(4):
                src = ins[w].at[s, _half(gs[w].shape[1], 1 - c), :]
                cps.append(_remote(src, outs[w].at[s], send_sems.at[w, s], recv_sems.at[w, s], (x, y, 1 - c)))
                cps[-1].start()
        for cp in cps:
            cp.wait()

    return pl.pallas_call(
        body, name="rs_sibling", out_shape=[_sds((4, g.shape[1] // 2, g.shape[2]), F32) for g in gs],
        in_specs=[ANY] * n, out_specs=[ANY] * n,
        scratch_shapes=[pltpu.SemaphoreType.DMA((n, 4)), pltpu.SemaphoreType.DMA((n, 4))],
    )(*gs)


def _ici_exchange(pbs):
    n = len(pbs)

    def body(*refs):
        ins, outs = refs[:n], refs[n:2 * n]
        send_sems, recv_sems = refs[2 * n:]
        x, y, c = _pos()
        cps = []
        for w in range(n):
            for j, (tx, ty) in enumerate([(1 - x, y), (x, 1 - y), (1 - x, 1 - y)]):
                cps.append(_remote(ins[w].at[2 * tx + ty], outs[w].at[j], send_sems.at[w, j], recv_sems.at[w, j],
                                   (tx, ty, c)))
                cps[-1].start()
        for cp in cps:
            cp.wait()

    return pl.pallas_call(
        body, name="rs_ici", out_shape=[_sds((3,) + p.shape[1:], BF16) for p in pbs],
        in_specs=[ANY] * n, out_specs=[ANY] * n,
        scratch_shapes=[pltpu.SemaphoreType.DMA((n, 3)), pltpu.SemaphoreType.DMA((n, 3))],
    )(*pbs)


def _sibling_share(gs):
    n = len(gs)

    def body(*refs):
        outs = refs[n:2 * n]
        send_sems, recv_sems = refs[2 * n:]
        x, y, c = _pos()
        cps = []
        for w in range(n):
            mine = outs[w].at[_half(gs[w].shape[0], c), :]
            cps.append(_remote(mine, mine, send_sems.at[w], recv_sems.at[w], (x, y, 1 - c)))
            cps[-1].start()
        for cp in cps:
            cp.wait()

    return pl.pallas_call(
        body, name="rs_share", out_shape=[_sds(g.shape, F32) for g in gs],
        in_specs=[ANY] * n, out_specs=[ANY] * n, input_output_aliases={w: w for w in range(n)},
        scratch_shapes=[pltpu.SemaphoreType.DMA((n,)), pltpu.SemaphoreType.DMA((n,))],
    )(*gs)


def _small_allreduce(v, name):
    P = v.shape[0]
    vm = pl.BlockSpec(memory_space=pltpu.VMEM)

    def body(v_ref, o_ref, gath, send_sems, recv_sems):
        x, y, c = _pos()
        me = 4 * x + 2 * y + c
        gath[me] = v_ref[...]
        cps = []
        for r in range(1, 8):
            tx = (1 - x) if r & 4 else x
            ty = (1 - y) if r & 2 else y
            tc = (1 - c) if r & 1 else c
            cps.append(_remote(v_ref, gath.at[me], send_sems.at[r - 1], recv_sems.at[r - 1], (tx, ty, tc)))
            cps[-1].start()
        for cp in cps:
            cp.wait()
        acc = gath[0]
        for d in range(1, 8):
            acc = acc + gath[d]
        o_ref[...] = acc

    return pl.pallas_call(
        body, name=name, out_shape=_sds((P, LANES), F32), in_specs=[vm], out_specs=vm,
        scratch_shapes=[pltpu.VMEM((8, P, LANES), F32), pltpu.SemaphoreType.DMA((7,)), pltpu.SemaphoreType.DMA((7,))],
    )(v)


def _chip_sum(g, rcv, pos, name):
    _, R, C = g.shape
    H = R // 2

    def body(pos_ref, g_ref, r_ref, o_ref):
        o_ref[...] = (g_ref[...] + r_ref[...]).astype(BF16)

    return pl.pallas_call(
        body, name=name, out_shape=_sds((4, H, C), BF16),
        grid_spec=pltpu.PrefetchScalarGridSpec(
            num_scalar_prefetch=1, grid=(4,),
            in_specs=[pl.BlockSpec((1, H, C), lambda s, pos: (s, pos[0], 0)),
                      pl.BlockSpec((1, H, C), lambda s, pos: (s, 0, 0))],
            out_specs=pl.BlockSpec((1, H, C), lambda s, pos: (s, 0, 0))),
        compiler_params=_params(1),
    )(pos, g, rcv)


def _final_sum(g, rcv, rc, pos, name):
    _, R, C = g.shape
    Q = R // 4

    def body(pos_ref, g_ref, r_ref, rc_ref, o_ref):
        acc = g_ref[0] + r_ref[0]
        for j in range(3):
            acc = acc + rc_ref[j].astype(F32)
        o_ref[...] = acc

    return pl.pallas_call(
        body, name=name, out_shape=_sds((R, C), F32),
        grid_spec=pltpu.PrefetchScalarGridSpec(
            num_scalar_prefetch=1, grid=(2,),
            in_specs=[pl.BlockSpec((1, Q, C), lambda i, pos: (pos[1], pos[0] * 2 + i, 0)),
                      pl.BlockSpec((1, Q, C), lambda i, pos: (pos[1], i, 0)),
                      pl.BlockSpec((3, Q, C), lambda i, pos: (0, i, 0))],
            out_specs=pl.BlockSpec((Q, C), lambda i, pos: (pos[0] * 2 + i, 0))),
        compiler_params=_params(1),
    )(pos, g, rcv, rc)


def _adamw_math(w, g, m, v):
    m = ADAM_B1 * m + (1.0 - ADAM_B1) * g
    v = ADAM_B2 * v + (1.0 - ADAM_B2) * (g * g)
    m_hat = m / (1.0 - ADAM_B1 ** ADAM_STEP)
    v_hat = v / (1.0 - ADAM_B2 ** ADAM_STEP)
    delta = -ADAM_LR * (m_hat / (jnp.sqrt(v_hat) + ADAM_EPS) + ADAM_WD * w)
    return delta, m, v


def _adamw(w, g, m, v, name, rb):
    R, C = w.shape

    def body(w_ref, g_ref, m_ref, v_ref, d_ref, nm_ref, nv_ref):
        d, nm, nv = _adamw_math(w_ref[...], g_ref[...], m_ref[...], v_ref[...])
        d_ref[...] = d
        nm_ref[...] = nm
        nv_ref[...] = nv

    blk = pl.BlockSpec((rb, C), lambda i: (i, 0))
    return pl.pallas_call(
        body, name=name, grid=(R // rb,), in_specs=[blk] * 4, out_specs=[blk] * 3,
        out_shape=[_sds((R, C), F32)] * 3, compiler_params=_params(1),
    )(w, g, m, v)


SMALL = (("g_mix", 8), ("b_f", 8), ("conv_w", None), ("conv_b", 8), ("ln_g", 8), ("ln_b", 8), ("g_x", 8), ("g_mem", 8),
         ("g_ffn", 8), ("g_final", 8), ("loss", 8))


def _pack_small(parts, conv_rows):
    rows = []
    for name, n in SMALL:
        if name not in parts:
            continue
        n = conv_rows if n is None else n
        flat = parts[name].reshape(-1).astype(F32)
        flat = jnp.pad(flat, (0, n * LANES - flat.shape[0]))
        rows.append(flat.reshape(n, LANES))
    return jnp.concatenate(rows, axis=0)


def _unpack_small(p, shapes, conv_rows):
    out, off = {}, 0
    for name, n in SMALL:
        if name not in shapes:
            continue
        n = conv_rows if n is None else n
        size = math.prod(shapes[name])
        out[name] = p[off:off + n].reshape(-1)[:size].reshape(shapes[name])
        off += n
    return out


def kernel(x, mem, g_mix, w_in, b_f, conv_w, conv_b, ln_g, ln_b, w_out, g_x, g_mem, w_mq, w_mkv, w_mo, g_ffn, w_gu, w_down, g_final, loss_target, m_g_mix, m_w_in, m_b_f, m_conv_w, m_conv_b, m_ln_g, m_ln_b, m_w_out, m_g_x, m_g_mem, m_w_mq, m_w_mkv, m_w_mo, m_g_ffn, m_w_gu, m_w_down, m_g_final, v_g_mix, v_w_in, v_b_f, v_conv_w, v_conv_b, v_ln_g, v_ln_b, v_w_out, v_g_x, v_g_mem, v_w_mq, v_w_mkv, v_w_mo, v_g_ffn, v_w_gu, v_w_down, v_g_final):
    names = ["g_mix", "w_in", "b_f", "conv_w", "conv_b", "ln_g", "ln_b", "w_out", "g_x", "g_mem", "w_mq", "w_mkv",
             "w_mo", "g_ffn", "w_gu", "w_down", "g_final"]
    W = dict(zip(names, [g_mix, w_in, b_f, conv_w, conv_b, ln_g, ln_b, w_out, g_x, g_mem, w_mq, w_mkv, w_mo, g_ffn,
                         w_gu, w_down, g_final]))
    M = dict(zip(names, [m_g_mix, m_w_in, m_b_f, m_conv_w, m_conv_b, m_ln_g, m_ln_b, m_w_out, m_g_x, m_g_mem, m_w_mq,
                         m_w_mkv, m_w_mo, m_g_ffn, m_w_gu, m_w_down, m_g_final]))
    V = dict(zip(names, [v_g_mix, v_w_in, v_b_f, v_conv_w, v_conv_b, v_ln_g, v_ln_b, v_w_out, v_g_x, v_g_mem, v_w_mq,
                         v_w_mkv, v_w_mo, v_g_ffn, v_w_gu, v_w_down, v_g_final]))
    big_names = [n for n, _, _, _ in BIG]
    B, S, _ = x.shape
    T = B * S
    mx, my, mc = _pos()
    chip = 2 * mx + my
    pos = jnp.stack([mc, chip]).astype(jnp.int32)

    shard2d = lambda a: a.reshape(a.shape[-2], a.shape[-1])
    slab = dict(zip(big_names, _allgather_weights([shard2d(W[n]).astype(BF16) for n in big_names])))
    full = {n: slab[n] if by_col else slab[n].reshape(4 * r, c) for n, r, c, by_col in BIG}
    w_in_f = jnp.pad(jnp.transpose(slab["w_in"], (1, 0, 2)).reshape(D, D_IN), ((0, 0), (0, D_IN_PAD - D_IN)))
    cw_mine = jnp.pad(shard2d(conv_w), ((0, 1), (0, 0)))
    cw_slot = lax.dynamic_update_slice(jnp.zeros((CONV_HALO, CONV_CH), F32), cw_mine, (0, chip * LANES))
    cw = _small_allreduce(cw_slot.reshape(CONV_HALO * 4, LANES) * 0.5, "gather_conv_w").reshape(CONV_HALO, CONV_CH)

    row = lambda a: a.reshape(1, -1)
    bf_pad = jnp.pad(row(b_f), ((0, 0), (0, LANES - 8)))
    x2d = x.reshape(T, D)
    mem2d = mem.reshape(B * MEM_LEN, D)
    tgt = loss_target.reshape(T, D)

    h, u, gt, q, k, v, zf, c, cq = _fwd_in(x2d, row(g_mix), w_in_f, bf_pad, B, S)
    ckT = jnp.transpose(c.reshape(B, S, LANES)[:, :, :8], (0, 2, 1)).reshape(B, N_PAIR, 2, S)
    ckT = jnp.pad(ckT, ((0, 0), (0, 0), (0, 6), (0, 0)))
    y, co = _conv_fwd(u, gt, cw, row(conv_b), row(ln_g), row(ln_b), B, S)
    o, fox_bias = _fox_fwd(q, k, v, cq, ckT, B, S)
    mn, km, vm = _mem_kv(mem2d, row(g_mem), full["w_mkv"], B)
    x1, hx, qm, om, x2, cat = _fwd_mid(x2d, co, o, km, vm, full["w_out"], full["w_mq"], full["w_mo"], row(g_x), B, S)
    hf, gu, act, dx3, loss_p, dg_final = _fwd_ffn(x2, tgt, full["w_gu"], full["w_down"], row(g_ffn), row(g_final), T)

    dgu, dx2, dg_ffn = _bwd_ffn(dx3, gu, x2, full["w_gu"], full["w_down"], row(g_ffn), T)
    dx1, dqm, dco, do, dd, dkm, dvm, dg_x = _bwd_mid(dx2, x1, qm, km, vm, o, full["w_mo"], full["w_mq"], full["w_out"],
                                                     row(g_x), B, S)
    dkv, dg_mem = _mem_bwd(dkm, dvm, mem2d, full["w_mkv"], row(g_mem), B)
    dq, dk, dv, dc, dcq = _fox_bwd(q, k, v, do, fox_bias, dd, ckT, B, S)
    dc8 = jnp.transpose(dc[:, :, :2, :].reshape(B, 8, S), (0, 2, 1)).reshape(T, 8)
    dc8 = dc8 + dcq.reshape(T, 8, HEAD_D)[:, :, 0]
    dzf, dbf = _fgate_bwd(jnp.pad(dc8, ((0, 0), (0, LANES - 8))), zf, B, S)
    du, dgt, dcw, dvec = _conv_bwd(dco, y, u, gt, cw, row(ln_g), row(ln_b), B, S)
    dz = jnp.concatenate([du, dgt, dq, dk, dv, dzf], axis=1)
    grad_x, dg_mix = _bwd_in(dz, w_in_f, x2d, dx1, row(g_mix), T)

    dw_in = _dw(h, dz, "dw_in", 384)[:, :D_IN]
    gslab = {
        "w_in": jnp.transpose(dw_in.reshape(D, 4, D_IN // 4), (1, 0, 2)),
        "w_out": _dw(cat, dx1, "dw_out", 512).reshape(4, 256, D),
        "w_mq": _dw(hx, dqm, "dw_mq", 512).reshape(4, 256, D),
        "w_mkv": _dw(mn, dkv, "dw_mkv", 512, slabs=True),
        "w_mo": _dw(om, dx2, "dw_mo", 512).reshape(4, 256, D),
        "w_gu": _dw(hf, dgu, "dw_gu", FF_CHUNK, slabs=True),
        "w_down": _dw(act, dx3, "dw_down", 512).reshape(4, D_FF // 4, D),
    }

    gl = [gslab[n] for n in big_names]
    rcv = _sibling_halves(gl)
    pb = [_chip_sum(g, r, pos, "rs_chip_sum_" + n) for g, r, n in zip(gl, rcv, big_names)]
    rc = _ici_exchange(pb)
    rh = [_final_sum(g, r, q3, pos, "rs_final_sum_" + n) for g, r, q3, n in zip(gl, rcv, rc, big_names)]
    G = dict(zip(big_names, _sibling_share(rh)))
    DL, NM, NV = {}, {}, {}
    for n in big_names:
        DL[n], NM[n], NV[n] = _adamw(shard2d(W[n]), G[n], shard2d(M[n]), shard2d(V[n]), "adamw_" + n, G[n].shape[0] // 2)

    small_g = {"g_mix": dg_mix, "b_f": dbf[:, :8], "conv_w": dcw, "conv_b": dvec[0], "ln_g": dvec[1], "ln_b": dvec[2],
               "g_x": dg_x, "g_mem": dg_mem, "g_ffn": dg_ffn, "g_final": dg_final, "loss": loss_p[:, :1]}
    sg = _small_allreduce(_pack_small(small_g, CONV_HALO * 4), "allreduce_small")
    shapes = {n: W[n].shape for n in names if n not in big_names}
    shapes["conv_w"] = (CONV_HALO, CONV_CH)
    shapes["loss"] = (1,)
    sgrads = _unpack_small(sg, shapes, CONV_HALO * 4)
    loss = sgrads.pop("loss")[0]
    sgrads["conv_w"] = lax.dynamic_slice(sgrads["conv_w"], (0, chip * LANES), (CONV_K, LANES)).reshape(W["conv_w"].shape)
    spack = lambda d: _pack_small({n: d[n] for n in sgrads}, CONV_HALO)
    sd, snm, snv = _adamw(spack(W), spack(sgrads), spack(M), spack(V), "adamw_small", 8)
    sshapes = {n: W[n].shape for n in sgrads}
    SD, SNM, SNV = (_unpack_small(a, sshapes, CONV_HALO) for a in (sd, snm, snv))

    def collect(bigs, smalls):
        return [bigs[n].reshape(W[n].shape) if n in big_names else smalls[n] for n in names]

    return (loss, grad_x.reshape(x.shape), *collect(G, sgrads), *collect(DL, SD), *collect(NM, SNM), *collect(NV, SNV))
```

```python
import functools
import math

import jax
import jax.numpy as jnp
from jax import lax
from jax.experimental import pallas as pl
from jax.experimental.pallas import tpu as pltpu

F32, BF16 = jnp.float32, jnp.bfloat16
HIGHEST = lax.Precision.HIGHEST
MESH = pl.DeviceIdType.MESH

D = 1024
CONV_CH = 512
CONV_K = 31
CONV_HALO = 32
FOX_W = 512
HEAD_D = 64
N_PAIR = 4
MEM_LEN = 256
MEM_HEADS = 4
MEM_HD = 256
D_FF = 2816
FF_CHUNK = 1408
D_IN = 2568
D_IN_PAD = 2688
OFF_F = 2560
EPS = 1e-6
LANES = 128

ADAM_LR, ADAM_B1, ADAM_B2, ADAM_EPS, ADAM_WD, ADAM_STEP = 0.001, 0.9, 0.999, 1e-08, 0.01, 10

VMEM_LIMIT = 60 * 1024 * 1024

BIG = (("w_out", 256, 1024, False), ("w_mq", 256, 1024, False), ("w_mkv", 1024, 512, True),
       ("w_mo", 256, 1024, False), ("w_gu", 1024, 1408, True), ("w_down", 704, 1024, False),
       ("w_in", 1024, 642, True))

ANY = pl.BlockSpec(memory_space=pl.ANY)


def _sig(x):
    return 1.0 / (1.0 + jnp.exp(-x))


def _dot(a, b):
    return jnp.dot(a, b, preferred_element_type=F32)


def _dot_nt(a, b):
    return lax.dot_general(a, b, (((1,), (1,)), ((), ())), preferred_element_type=F32)


def _dot_tn(a, b):
    return lax.dot_general(a, b, (((0,), (0,)), ((), ())), preferred_element_type=F32)


def _dot_hi(a, b):
    return jnp.dot(a, b, precision=HIGHEST, preferred_element_type=F32)


def _resident(a):
    nd = a.ndim
    return pl.BlockSpec(a.shape, lambda *_: (0,) * nd, pipeline_mode=pl.Buffered(1))


def _acc_spec(shape):
    nd = len(shape)
    return pl.BlockSpec(shape, lambda *_: (0,) * nd)


def _params(n_grid):
    return pltpu.CompilerParams(dimension_semantics=("arbitrary",) * n_grid, vmem_limit_bytes=VMEM_LIMIT)


def _sds(shape, dtype):
    return jax.ShapeDtypeStruct(shape, dtype)


def _rms(x):
    r = lax.rsqrt(jnp.mean(x * x, axis=-1, keepdims=True) + EPS)
    return r, x * r


def _rms_bwd(dy, xh, r, g):
    dxh = dy * g
    dx = r * (dxh - xh * jnp.mean(dxh * xh, axis=-1, keepdims=True))
    return dx, dy * xh


def _head_expand(rows, cols):
    hd = lax.broadcasted_iota(jnp.int32, (rows, cols), 1) // HEAD_D
    hr = lax.broadcasted_iota(jnp.int32, (rows, cols), 0)
    return (hd == hr).astype(F32)


def _head_sum(n):
    hc = lax.broadcasted_iota(jnp.int32, (n, n), 1) // HEAD_D
    hr = lax.broadcasted_iota(jnp.int32, (n, n), 0) // HEAD_D
    return (hc == hr).astype(F32)


def _fwd_in(x2, g_mix, w_in, bf_pad, B, S):
    T = B * S
    TB = min(512, S)
    nb = S // TB

    def body(x_ref, g_ref, w_ref, bf_ref, h_ref, u_ref, gt_ref, q_ref, k_ref, v_ref, zf_ref, c_ref, cq_ref,
             carry):
        j = pl.program_id(1)

        @pl.when(j == 0)
        def _():
            carry[...] = jnp.zeros_like(carry)

        _, xh = _rms(x_ref[...])
        h = (xh * g_ref[...]).astype(BF16)
        h_ref[...] = h
        u_ref[...] = _dot(h, w_ref[:, 0:512])
        gt_ref[...] = _dot(h, w_ref[:, 512:1024])
        q_ref[...] = _dot(h, w_ref[:, 1024:1536]).astype(BF16)
        k_ref[...] = _dot(h, w_ref[:, 1536:2048]).astype(BF16)
        v_ref[...] = _dot(h, w_ref[:, 2048:2560]).astype(BF16)
        zf = _dot(h, w_ref[:, OFF_F:D_IN_PAD]) + bf_ref[...]
        zf_ref[...] = zf
        lane = lax.broadcasted_iota(jnp.int32, zf.shape, 1)
        logf = jnp.where(lane < 8, jnp.minimum(zf, 0.0) - jnp.log(1.0 + jnp.exp(-jnp.abs(zf))), 0.0)
        row = lax.broadcasted_iota(jnp.int32, (TB, TB), 0)
        col = lax.broadcasted_iota(jnp.int32, (TB, TB), 1)
        c = _dot_hi((row >= col).astype(F32), logf) + carry[0:1, :]
        carry[0:1, :] = c[TB - 1:TB, :]
        c_ref[...] = c
        cq_ref[...] = _dot_hi(c, _head_expand(LANES, FOX_W))

    tok = lambda w: pl.BlockSpec((TB, w), lambda b, j: (b * nb + j, 0))
    outs = [(D, BF16), (512, F32), (512, F32), (512, BF16), (512, BF16), (512, BF16), (LANES, F32),
            (LANES, F32), (FOX_W, F32)]
    return pl.pallas_call(
        body, name="fwd_in", grid=(B, nb),
        in_specs=[tok(D), _resident(g_mix), _resident(w_in), _resident(bf_pad)],
        out_specs=[tok(w) for w, _ in outs],
        out_shape=[_sds((T, w), dt) for w, dt in outs],
        scratch_shapes=[pltpu.VMEM((8, LANES), F32)],
        compiler_params=_params(2),
    )(x2, g_mix, w_in, bf_pad)


def _layernorm_silu(y, lg, lb):
    mu = jnp.mean(y, axis=-1, keepdims=True)
    yc = y - mu
    rs = lax.rsqrt(jnp.mean(yc * yc, axis=-1, keepdims=True) + EPS)
    n = yc * rs
    l = n * lg + lb
    return rs, n, l


def _conv_fwd(u, gt, cw, cb, lng, lnb, B, S, comm=None):
    T = B * S
    CB = min(256, S)
    nb = S // CB

    def body(u_ref, gt_ref, w_ref, cb_ref, lg_ref, lb_ref, y_ref, co_ref, acat):
        j = pl.program_id(1)

        @pl.when(j == 0)
        def _():
            acat[0:CONV_HALO, :] = jnp.zeros((CONV_HALO, CONV_CH), F32)

        acat[CONV_HALO:CONV_HALO + CB, :] = u_ref[...] * _sig(gt_ref[...])
        acc = jnp.zeros((CB, CONV_CH), F32) + cb_ref[...]
        for k in range(CONV_K):
            off = CONV_HALO - (CONV_K - 1) + k
            acc = acc + w_ref[k:k + 1, :] * acat[off:off + CB, :]
        acat[0:CONV_HALO, :] = acat[CB:CB + CONV_HALO, :]
        y_ref[...] = acc
        _, _, l = _layernorm_silu(acc, lg_ref[...], lb_ref[...])
        co_ref[...] = (l * _sig(l)).astype(BF16)

    tok = lambda w: pl.BlockSpec((CB, w), lambda b, j: (b * nb + j, 0))
    return _call(
        body, comm, name="conv_fwd", grid=(B, nb),
        in_specs=[tok(CONV_CH), tok(CONV_CH), _resident(cw), _resident(cb), _resident(lng), _resident(lnb)],
        out_specs=[tok(CONV_CH), tok(CONV_CH)],
        out_shape=[_sds((T, CONV_CH), F32), _sds((T, CONV_CH), BF16)],
        scratch_shapes=[pltpu.VMEM((CONV_HALO + CB, CONV_CH), F32)],
        args=(u, gt, cw, cb, lng, lnb))


def _fox_fwd(q, k, v, cq, ckT, B, S, comm=None):
    T = B * S
    TQ = min(256, S)
    nq = S // TQ
    scale = 1.0 / math.sqrt(HEAD_D)
    one_lane = (HEAD_D, 0)

    def body(q_ref, k_ref, v_ref, cq_ref, ck_ref, o_ref, lse_ref, s_scr, s_odd, m_scr, acc_scr):
        i = pl.program_id(2)
        lane = lax.broadcasted_iota(jnp.int32, (TQ, LANES), 1)
        lo = lane < HEAD_D
        qs = q_ref[...] * jnp.asarray(scale, BF16)
        zero = jnp.zeros_like(qs)
        qh = (jnp.where(lo, qs, zero), jnp.where(lo, zero, qs))
        cqv = cq_ref[...]
        cq_rep = tuple(jnp.broadcast_to(cqv[:, h * HEAD_D:h * HEAD_D + 1], (TQ, LANES)) for h in range(2))
        m_scr[...] = jnp.full(m_scr.shape, -1e30, F32)
        acc_scr[...] = jnp.zeros_like(acc_scr)
        row = lax.broadcasted_iota(jnp.int32, (TQ, TQ), 0)
        col = lax.broadcasted_iota(jnp.int32, (TQ, TQ), 1)
        wide = lambda x: jnp.concatenate([x, x], axis=1) if TQ == 2 * LANES else jnp.tile(x, (1, TQ // LANES))

        def scores(j, s_buf):
            kj = k_ref[pl.ds(pl.multiple_of(j * TQ, TQ), TQ), :]
            for h in range(2):
                s_buf[h] = _dot_nt(qh[h], kj)

        def softmax_step(j, s_buf, diagonal):
            start = pl.multiple_of(j * TQ, TQ)
            vj = v_ref[pl.ds(start, TQ), :]
            for h in range(2):
                ck = ck_ref[0, 0, h:h + 1, pl.ds(start, TQ)]

                def logits():
                    t = (s_buf[h] - ck) + wide(cq_rep[h])
                    return jnp.where(col <= row, t, -1e30) if diagonal else t

                m_old = m_scr[h]
                m_new = jnp.maximum(m_old, jnp.max(logits(), axis=-1, keepdims=True))
                alpha = jnp.exp(m_old - m_new)
                m_scr[h] = m_new
                p = jnp.exp(logits() - wide(m_new)).astype(BF16)
                vx = jnp.where(lane == one_lane[h], jnp.ones_like(vj), jnp.where(lo if h == 0 else ~lo, vj, jnp.zeros_like(vj)))
                acc_scr[h] = alpha * acc_scr[h] + _dot(p, vx)

        def two_blocks(jj, carry):
            j = 2 * jj
            scores(j + 1, s_odd)
            softmax_step(j, s_scr, False)
            scores(j + 2, s_scr)
            softmax_step(j + 1, s_odd, False)
            return carry

        scores(0, s_scr)
        lax.fori_loop(0, i // 2, two_blocks, 0)

        @pl.when(i % 2 == 0)
        def _():
            softmax_step(i, s_scr, True)

        @pl.when(i % 2 == 1)
        def _():
            scores(i, s_odd)
            softmax_step(i - 1, s_scr, False)
            softmax_step(i, s_odd, True)

        acc_a, acc_b = acc_scr[0], acc_scr[1]
        l_a = acc_a[:, one_lane[0]:one_lane[0] + 1]
        l_b = acc_b[:, one_lane[1]:one_lane[1] + 1]
        o_ref[...] = jnp.where(lo, acc_a / l_a, acc_b / l_b)
        lse_ref[...] = cqv - jnp.where(lo, m_scr[0] + jnp.log(l_a), m_scr[1] + jnp.log(l_b))

    qspec = pl.BlockSpec((TQ, LANES), lambda b, p, i: (b * nq + i, p))
    kspec = pl.BlockSpec((S, LANES), lambda b, p, i: (b, p))
    return _call(
        body, comm, name="fox_fwd", grid=(B, N_PAIR, nq),
        in_specs=[qspec, kspec, kspec, qspec, pl.BlockSpec((1, 1, 8, S), lambda b, p, i: (b, p, 0, 0))],
        out_specs=[qspec, qspec],
        out_shape=[_sds((T, FOX_W), F32), _sds((T, FOX_W), F32)],
        scratch_shapes=[pltpu.VMEM((2, TQ, TQ), F32), pltpu.VMEM((2, TQ, TQ), F32),
                        pltpu.VMEM((2, TQ, LANES), F32), pltpu.VMEM((2, TQ, LANES), F32)],
        args=(q, k, v, cq, ckT))


def _mem_kv(mem2, g_mem, w_mkv, B):
    def body(m_ref, g_ref, w_ref, mn_ref, km_ref, vm_ref):
        _, xh = _rms(m_ref[...])
        mn = (xh * g_ref[...]).astype(BF16)
        mn_ref[...] = mn
        for s in range(2):
            km_ref[:, 512 * s:512 * (s + 1)] = _dot(mn, w_ref[s]).astype(BF16)
            vm_ref[:, 512 * s:512 * (s + 1)] = _dot(mn, w_ref[2 + s]).astype(BF16)

    blk = pl.BlockSpec((MEM_LEN, D), lambda b: (b, 0))
    return pl.pallas_call(
        body, name="mem_kv", grid=(B,),
        in_specs=[blk, _resident(g_mem), _resident(w_mkv)],
        out_specs=[blk, blk, blk],
        out_shape=[_sds((B * MEM_LEN, D), BF16)] * 3,
        compiler_params=_params(1),
    )(mem2, g_mem, w_mkv)


def _mem_probs(qm, km):
    ps = []
    for h in range(MEM_HEADS):
        hs = slice(h * MEM_HD, (h + 1) * MEM_HD)
        lg = _dot_nt(qm[:, hs], km[:, hs]) * (1.0 / math.sqrt(MEM_HD))
        e = jnp.exp(lg - jnp.max(lg, axis=-1, keepdims=True))
        ps.append(e / jnp.sum(e, axis=-1, keepdims=True))
    return ps


def _fwd_mid(x2, co, o, km, vm, w_out, w_mq, w_mo, g_x, B, S):
    T = B * S
    TB = min(512, S)
    nb = S // TB

    def body(x_ref, co_ref, o_ref, km_ref, vm_ref, wo_ref, wq_ref, wm_ref, g_ref,
             x1_ref, hx_ref, qm_ref, om_ref, x2_ref, cat_ref):
        cat_ref[:, 0:CONV_CH] = co_ref[...]
        cat_ref[:, CONV_CH:D] = o_ref[...].astype(BF16)
        x1 = x_ref[...] + _dot(cat_ref[...], wo_ref[...])
        x1_ref[...] = x1
        _, xh = _rms(x1)
        hx = (xh * g_ref[...]).astype(BF16)
        hx_ref[...] = hx
        qm = _dot(hx, wq_ref[...]).astype(BF16)
        qm_ref[...] = qm
        ps = _mem_probs(qm, km_ref[...])
        vmv = vm_ref[...]
        for h in range(MEM_HEADS):
            hs = slice(h * MEM_HD, (h + 1) * MEM_HD)
            om_ref[:, hs] = _dot(ps[h].astype(BF16), vmv[:, hs]).astype(BF16)
        x2_ref[...] = x1 + _dot(om_ref[...], wm_ref[...])

    tok = lambda w: pl.BlockSpec((TB, w), lambda b, j: (b * nb + j, 0))
    memb = pl.BlockSpec((MEM_LEN, D), lambda b, j: (b, 0))
    outs = [(D, F32), (D, BF16), (D, BF16), (D, BF16), (D, F32), (D, BF16)]
    return pl.pallas_call(
        body, name="fwd_mid", grid=(B, nb),
        in_specs=[tok(D), tok(CONV_CH), tok(FOX_W), memb, memb, _resident(w_out), _resident(w_mq), _resident(w_mo),
                  _resident(g_x)],
        out_specs=[tok(w) for w, _ in outs],
        out_shape=[_sds((T, w), dt) for w, dt in outs],
        compiler_params=_params(2),
    )(x2, co, o, km, vm, w_out, w_mq, w_mo, g_x)


def _fwd_ffn(x2, tgt, w_gu, w_down, g_ffn, g_final, T):
    TB = min(256, T)
    nb = T // TB

    def body(x_ref, t_ref, wgu_ref, wd_ref, gf_ref, gl_ref, hf_ref, gu_ref, act_ref, dx3_ref, loss_ref, dgl_ref):
        i = pl.program_id(0)

        @pl.when(i == 0)
        def _():
            loss_ref[...] = jnp.zeros_like(loss_ref)
            dgl_ref[...] = jnp.zeros_like(dgl_ref)

        x2v = x_ref[...]
        _, xh = _rms(x2v)
        hf = (xh * gf_ref[...]).astype(BF16)
        hf_ref[...] = hf
        x3 = x2v
        for ch in range(D_FF // FF_CHUNK):
            c0 = ch * FF_CHUNK
            g = _dot(hf, wgu_ref[ch])
            u = _dot(hf, wgu_ref[2 + ch])
            gu_ref[:, c0:c0 + FF_CHUNK] = g
            gu_ref[:, D_FF + c0:D_FF + c0 + FF_CHUNK] = u
            act = (g * _sig(g) * u).astype(BF16)
            act_ref[:, c0:c0 + FF_CHUNK] = act
            x3 = x3 + _dot(act, wd_ref[c0:c0 + FF_CHUNK, :])
        r3, xh3 = _rms(x3)
        gl = gl_ref[...]
        e = xh3 * gl - t_ref[...]
        loss_ref[...] += jnp.sum(e * e) * (0.5 / D)
        dy = e * (1.0 / D)
        dx3, dgl = _rms_bwd(dy, xh3, r3, gl)
        dx3_ref[...] = dx3
        dgl_ref[...] += jnp.sum(dgl, axis=0, keepdims=True)

    tok = lambda w: pl.BlockSpec((TB, w), lambda i: (i, 0))
    return pl.pallas_call(
        body, name="fwd_ffn", grid=(nb,),
        in_specs=[tok(D), tok(D), _resident(w_gu), _resident(w_down), _resident(g_ffn), _resident(g_final)],
        out_specs=[tok(D), tok(2 * D_FF), tok(D_FF), tok(D), _acc_spec((1, LANES)), _acc_spec((1, D))],
        out_shape=[_sds((T, D), BF16), _sds((T, 2 * D_FF), F32), _sds((T, D_FF), BF16), _sds((T, D), F32),
                   _sds((1, LANES), F32), _sds((1, D), F32)],
        compiler_params=_params(1),
    )(x2, tgt, w_gu, w_down, g_ffn, g_final)


def _bwd_ffn(dx3, gu, x2, w_gu, w_down, g_ffn, T):
    TB = min(256, T)
    nb = T // TB

    def body(d_ref, gu_ref, x_ref, wgu_ref, wd_ref, gf_ref, dgu_ref, dx2_ref, dgf_ref):
        i = pl.program_id(0)

        @pl.when(i == 0)
        def _():
            dgf_ref[...] = jnp.zeros_like(dgf_ref)

        dx3v = d_ref[...]
        db = dx3v.astype(BF16)
        dhf = jnp.zeros((TB, D), F32)
        for ch in range(D_FF // FF_CHUNK):
            c0 = ch * FF_CHUNK
            dact = _dot_nt(db, wd_ref[c0:c0 + FF_CHUNK, :])
            g = gu_ref[:, c0:c0 + FF_CHUNK]
            u = gu_ref[:, D_FF + c0:D_FF + c0 + FF_CHUNK]
            sg = _sig(g)
            dg = (dact * u * sg * (1.0 + g * (1.0 - sg))).astype(BF16)
            du = (dact * g * sg).astype(BF16)
            dgu_ref[:, c0:c0 + FF_CHUNK] = dg
            dgu_ref[:, D_FF + c0:D_FF + c0 + FF_CHUNK] = du
            dhf = dhf + _dot_nt(dg, wgu_ref[ch]) + _dot_nt(du, wgu_ref[2 + ch])
        r2, xh2 = _rms(x_ref[...])
        dx, dg_tok = _rms_bwd(dhf, xh2, r2, gf_ref[...])
        dx2_ref[...] = dx3v + dx
        dgf_ref[...] += jnp.sum(dg_tok, axis=0, keepdims=True)

    tok = lambda w: pl.BlockSpec((TB, w), lambda i: (i, 0))
    return pl.pallas_call(
        body, name="bwd_ffn", grid=(nb,),
        in_specs=[tok(D), tok(2 * D_FF), tok(D), _resident(w_gu), _resident(w_down), _resident(g_ffn)],
        out_specs=[tok(2 * D_FF), tok(D), _acc_spec((1, D))],
        out_shape=[_sds((T, 2 * D_FF), BF16), _sds((T, D), F32), _sds((1, D), F32)],
        compiler_params=_params(1),
    )(dx3, gu, x2, w_gu, w_down, g_ffn)


def _bwd_mid(dx2, x1, qm, km, vm, o, w_mo, w_mq, w_out, g_x, B, S, comm=None):
    T = B * S
    TB = min(512, S)
    nb = S // TB
    inv = 1.0 / math.sqrt(MEM_HD)

    def body(d_ref, x1_ref, qm_ref, km_ref, vm_ref, o_ref, wm_ref, wq_ref, wo_ref, g_ref,
             dx1_ref, dqm_ref, dco_ref, do_ref, dd_ref, dkm_ref, dvm_ref, dgx_ref):
        b = pl.program_id(0)
        j = pl.program_id(1)

        @pl.when((b == 0) & (j == 0))
        def _():
            dgx_ref[...] = jnp.zeros_like(dgx_ref)

        @pl.when(j == 0)
        def _():
            dkm_ref[...] = jnp.zeros_like(dkm_ref)
            dvm_ref[...] = jnp.zeros_like(dvm_ref)

        dx2v = d_ref[...]
        dom = _dot_nt(dx2v.astype(BF16), wm_ref[...]).astype(BF16)
        qmv = qm_ref[...]
        kmv = km_ref[...]
        vmv = vm_ref[...]
        ps = _mem_probs(qmv, kmv)
        for h in range(MEM_HEADS):
            hs = slice(h * MEM_HD, (h + 1) * MEM_HD)
            p = ps[h]
            dp = _dot_nt(dom[:, hs], vmv[:, hs])
            ds = (p * (dp - jnp.sum(p * dp, axis=-1, keepdims=True))).astype(BF16)
            dqm_ref[:, hs] = (_dot(ds, kmv[:, hs]) * inv).astype(BF16)
            dkm_ref[:, hs] += _dot_tn(ds, qmv[:, hs]) * inv
            dvm_ref[:, hs] += _dot_tn(p.astype(BF16), dom[:, hs])
        dhx = _dot_nt(dqm_ref[...], wq_ref[...])
        r1, xh1 = _rms(x1_ref[...])
        dx, dg_tok = _rms_bwd(dhx, xh1, r1, g_ref[...])
        dx1 = dx2v + dx
        dx1_ref[...] = dx1
        dgx_ref[...] += jnp.sum(dg_tok, axis=0, keepdims=True)
        d1b = dx1.astype(BF16)
        dco_ref[...] = _dot_nt(d1b, wo_ref[0:CONV_CH, :])
        do = _dot_nt(d1b, wo_ref[CONV_CH:D, :])
        dob = do.astype(BF16)
        do_ref[...] = dob
        dd_ref[...] = _dot_hi(dob.astype(F32) * o_ref[...], _head_sum(FOX_W))

    tok = lambda w: pl.BlockSpec((TB, w), lambda b, j: (b * nb + j, 0))
    memb = pl.BlockSpec((MEM_LEN, D), lambda b, j: (b, 0))
    outs = [(D, F32), (D, BF16), (CONV_CH, F32), (FOX_W, BF16), (FOX_W, F32)]
    return _call(
        body, comm, name="bwd_mid", grid=(B, nb),
        in_specs=[tok(D), tok(D), tok(D), memb, memb, tok(FOX_W), _resident(w_mo), _resident(w_mq), _resident(w_out),
                  _resident(g_x)],
        out_specs=[tok(w) for w, _ in outs] + [memb, memb, _acc_spec((1, D))],
        out_shape=[_sds((T, w), dt) for w, dt in outs] + [_sds((B * MEM_LEN, D), F32)] * 2 + [_sds((1, D), F32)],
        scratch_shapes=[],
        args=(dx2, x1, qm, km, vm, o, w_mo, w_mq, w_out, g_x))


def _mem_bwd(dkm, dvm, mem2, w_mkv, g_mem, B):
    def body(dk_ref, dv_ref, m_ref, w_ref, g_ref, dkv_ref, dg_ref):
        b = pl.program_id(0)

        @pl.when(b == 0)
        def _():
            dg_ref[...] = jnp.zeros_like(dg_ref)

        dk = dk_ref[...].astype(BF16)
        dv = dv_ref[...].astype(BF16)
        dkv_ref[:, 0:D] = dk
        dkv_ref[:, D:2 * D] = dv
        dmn = jnp.zeros((MEM_LEN, D), F32)
        for s in range(2):
            dmn = dmn + _dot_nt(dk[:, 512 * s:512 * (s + 1)], w_ref[s]) + _dot_nt(dv[:, 512 * s:512 * (s + 1)], w_ref[2 + s])
        _, xh = _rms(m_ref[...])
        dg_ref[...] += jnp.sum(dmn * xh, axis=0, keepdims=True)

    blk = pl.BlockSpec((MEM_LEN, D), lambda b: (b, 0))
    return pl.pallas_call(
        body, name="mem_bwd", grid=(B,),
        in_specs=[blk, blk, blk, _resident(w_mkv), _resident(g_mem)],
        out_specs=[pl.BlockSpec((MEM_LEN, 2 * D), lambda b: (b, 0)), _acc_spec((1, D))],
        out_shape=[_sds((B * MEM_LEN, 2 * D), BF16), _sds((1, D), F32)],
        compiler_params=_params(1),
    )(dkm, dvm, mem2, w_mkv, g_mem)


def _fox_bwd(q, k, v, do, bias, dd, ckT, B, S, comm=None):
    T = B * S
    TK = min(256, S)
    nk = S // TK
    scale = 1.0 / math.sqrt(HEAD_D)

    def body(q_ref, k_ref, v_ref, do_ref, bias_ref, dd_ref, ck_ref, dq_ref, dk_ref, dv_ref, dc_ref, dcq_ref,
             dq_acc, rs_acc, s_scr, dp_scr, s_odd, dp_odd, dk_acc, dv_acc, dc_acc):
        j = pl.program_id(2)

        @pl.when(j == 0)
        def _():
            dq_acc[...] = jnp.zeros_like(dq_acc)
            rs_acc[...] = jnp.zeros_like(rs_acc)

        dk_acc[...] = jnp.zeros_like(dk_acc)
        dv_acc[...] = jnp.zeros_like(dv_acc)
        dc_acc[...] = jnp.zeros_like(dc_acc)
        lane = lax.broadcasted_iota(jnp.int32, (TK, LANES), 1)
        lo = lane < HEAD_D
        ks = k_ref[...] * jnp.asarray(scale, BF16)
        v2 = v_ref[...]
        zero = jnp.zeros_like(ks)
        kh = (jnp.where(lo, ks, zero), jnp.where(lo, zero, ks))
        vh = (jnp.where(lo, v2, zero), jnp.where(lo, zero, v2))
        kstart = pl.multiple_of(j * TK, TK)
        ckh = tuple(ck_ref[0, 0, h:h + 1, pl.ds(kstart, TK)] for h in range(2))
        row = lax.broadcasted_iota(jnp.int32, (TK, TK), 0)
        col = lax.broadcasted_iota(jnp.int32, (TK, TK), 1)
        wide = lambda x: jnp.concatenate([x, x], axis=1) if TK == 2 * LANES else jnp.tile(x, (1, TK // LANES))

        def scores(i, s_buf, dp_buf):
            start = pl.multiple_of(i * TK, TK)
            qi = q_ref[pl.ds(start, TK), :]
            doi = do_ref[pl.ds(start, TK), :]
            for h in range(2):
                s_buf[h] = _dot_nt(qi, kh[h])
                dp_buf[h] = _dot_nt(doi, vh[h])

        def grads(i, s_buf, dp_buf, diagonal):
            start = pl.multiple_of(i * TK, TK)
            qi = q_ref[pl.ds(start, TK), :]
            doi = do_ref[pl.ds(start, TK), :]
            bias2 = bias_ref[pl.ds(start, TK), :]
            dd2 = dd_ref[pl.ds(start, TK), :]
            for h in range(2):
                hc = slice(h * HEAD_D, h * HEAD_D + 1)
                bias = jnp.broadcast_to(bias2[:, hc], (TK, LANES))
                ddh = jnp.broadcast_to(dd2[:, hc], (TK, LANES))
                p = jnp.exp((s_buf[h] - ckh[h]) + wide(bias))
                if diagonal:
                    p = jnp.where(col <= row, p, 0.0)
                ds = p * (dp_buf[h] - wide(ddh))
                dc_acc[h, 0:1, :] += jnp.sum(ds, axis=0, keepdims=True)
                rs_acc[h, pl.ds(start, TK), :] += jnp.sum(ds, axis=1, keepdims=True)
                pb = p.astype(BF16)
                dsb = ds.astype(BF16)
                dv_acc[h] += _dot_tn(pb, doi)
                dk_acc[h] += _dot_tn(dsb, qi)
                dq_acc[pl.ds(start, TK), :] += _dot(dsb, kh[h])

        n_off = nk - 1 - j
        block = lambda t: jnp.where(t < n_off, j + 1 + t, j)

        def two_blocks(tt, carry):
            t = 2 * tt
            scores(block(t + 1), s_odd, dp_odd)
            grads(block(t), s_scr, dp_scr, False)
            scores(block(t + 2), s_scr, dp_scr)
            grads(block(t + 1), s_odd, dp_odd, False)
            return carry

        scores(block(0), s_scr, dp_scr)
        lax.fori_loop(0, n_off // 2, two_blocks, 0)

        @pl.when(n_off % 2 == 0)
        def _():
            grads(j, s_scr, dp_scr, True)

        @pl.when(n_off % 2 == 1)
        def _():
            scores(j, s_odd, dp_odd)
            grads(nk - 1, s_scr, dp_scr, False)
            grads(j, s_odd, dp_odd, True)

        dk_ref[...] = (jnp.where(lo, dk_acc[0], dk_acc[1]) * scale).astype(BF16)
        dv_ref[...] = jnp.where(lo, dv_acc[0], dv_acc[1]).astype(BF16)
        sub = lax.broadcasted_iota(jnp.int32, (8, TK), 0)
        dca = dc_acc[0, 0:1, :]
        dcb = dc_acc[1, 0:1, :]
        dc_ref[0, 0] = jnp.where(sub == 0, -dca, jnp.where(sub == 1, -dcb, 0.0))

        @pl.when(j == nk - 1)
        def _():
            dq_ref[...] = dq_acc[...].astype(BF16)
            lo_s = lax.broadcasted_iota(jnp.int32, (S, LANES), 1) < HEAD_D
            dcq_ref[...] = jnp.where(lo_s, rs_acc[0], rs_acc[1])

    full = pl.BlockSpec((S, LANES), lambda b, p, j: (b, p))
    blk = pl.BlockSpec((TK, LANES), lambda b, p, j: (b * nk + j, p))
    return _call(
        body, comm, name="fox_bwd", grid=(B, N_PAIR, nk),
        in_specs=[full, blk, blk, full, full, full, pl.BlockSpec((1, 1, 8, S), lambda b, p, j: (b, p, 0, 0))],
        out_specs=[full, blk, blk, pl.BlockSpec((1, 1, 8, TK), lambda b, p, j: (b, p, 0, j)), full],
        out_shape=[_sds((T, FOX_W), BF16), _sds((T, FOX_W), BF16), _sds((T, FOX_W), BF16),
                   _sds((B, N_PAIR, 8, S), F32), _sds((T, FOX_W), F32)],
        scratch_shapes=[pltpu.VMEM((S, LANES), F32), pltpu.VMEM((2, S, 1), F32),
                        pltpu.VMEM((2, TK, TK), F32), pltpu.VMEM((2, TK, TK), F32),
                        pltpu.VMEM((2, TK, TK), F32), pltpu.VMEM((2, TK, TK), F32),
                        pltpu.VMEM((2, TK, LANES), F32), pltpu.VMEM((2, TK, LANES), F32), pltpu.VMEM((2, 8, TK), F32)],
        args=(q, k, v, do, bias, dd, ckT))


def _fgate_bwd(dc8, zf, B, S):
    T = B * S
    TB = min(512, S)
    nb = S // TB

    def body(dc_ref, zf_ref, dzf_ref, dbf_ref, carry):
        b = pl.program_id(0)
        j = pl.program_id(1)

        @pl.when((b == 0) & (j == 0))
        def _():
            dbf_ref[...] = jnp.zeros_like(dbf_ref)

        @pl.when(j == 0)
        def _():
            carry[...] = jnp.zeros_like(carry)

        dc = dc_ref[...]
        row = lax.broadcasted_iota(jnp.int32, (TB, TB), 0)
        col = lax.broadcasted_iota(jnp.int32, (TB, TB), 1)
        dlogf = _dot_hi((col >= row).astype(F32), dc) + carry[0:1, :]
        carry[0:1, :] = dlogf[0:1, :]
        lane = lax.broadcasted_iota(jnp.int32, dc.shape, 1)
        dzf = jnp.where(lane < 8, dlogf * _sig(-zf_ref[...]), 0.0)
        dzf_ref[...] = dzf.astype(BF16)
        dbf_ref[...] += jnp.sum(dzf, axis=0, keepdims=True)

    tok = pl.BlockSpec((TB, LANES), lambda b, j: (b * nb + (nb - 1 - j), 0))
    return pl.pallas_call(
        body, name="fgate_bwd", grid=(B, nb),
        in_specs=[tok, tok],
        out_specs=[tok, _acc_spec((1, LANES))],
        out_shape=[_sds((T, LANES), BF16), _sds((1, LANES), F32)],
        scratch_shapes=[pltpu.VMEM((8, LANES), F32)],
        compiler_params=_params(2),
    )(dc8, zf)


def _conv_bwd(dco, y, u, gt, cw, lng, lnb, B, S, comm=None):
    T = B * S
    CB = min(256, S)
    nb = S // CB
    hb = CB // CONV_HALO

    def body(dco_ref, y_ref, u_ref, gt_ref, up_ref, gp_ref, w_ref, lg_ref, lb_ref,
             du_ref, dgt_ref, dw_ref, vec_ref, acat, dycat):
        b = pl.program_id(0)
        j = pl.program_id(1)
        jr = nb - 1 - j

        @pl.when((b == 0) & (j == 0))
        def _():
            dw_ref[...] = jnp.zeros_like(dw_ref)
            vec_ref[...] = jnp.zeros_like(vec_ref)

        @pl.when(j == 0)
        def _():
            dycat[CB:CB + CONV_HALO, :] = jnp.zeros((CONV_HALO, CONV_CH), F32)

        lg = lg_ref[...]
        rs, n, l = _layernorm_silu(y_ref[...], lg, lb_ref[...])
        sg = _sig(l)
        dl = dco_ref[...] * (sg * (1.0 + l * (1.0 - sg)))
        dn = dl * lg
        dy = rs * (dn - jnp.mean(dn, axis=-1, keepdims=True) - n * jnp.mean(dn * n, axis=-1, keepdims=True))
        vec_ref[0:1, :] += jnp.sum(dy, axis=0, keepdims=True)
        vec_ref[1:2, :] += jnp.sum(dl * n, axis=0, keepdims=True)
        vec_ref[2:3, :] += jnp.sum(dl, axis=0, keepdims=True)
        dycat[0:CB, :] = dy
        uv = u_ref[...]
        sgt = _sig(gt_ref[...])
        acat[0:CONV_HALO, :] = jnp.where(jr > 0, up_ref[...] * _sig(gp_ref[...]), 0.0)
        acat[CONV_HALO:CONV_HALO + CB, :] = uv * sgt
        da = jnp.zeros((CB, CONV_CH), F32)
        for k in range(CONV_K):
            off = CONV_HALO - (CONV_K - 1) + k
            da = da + w_ref[k:k + 1, :] * dycat[CONV_K - 1 - k:CONV_K - 1 - k + CB, :]
            dw_ref[k:k + 1, :] += jnp.sum(dy * acat[off:off + CB, :], axis=0, keepdims=True)
        dycat[CB:CB + CONV_HALO, :] = dycat[0:CONV_HALO, :]
        du_ref[...] = (da * sgt).astype(BF16)
        dgt_ref[...] = (da * uv * sgt * (1.0 - sgt)).astype(BF16)

    tok = lambda w: pl.BlockSpec((CB, w), lambda b, j: (b * nb + (nb - 1 - j), 0))
    prev = pl.BlockSpec((CONV_HALO, CONV_CH), lambda b, j: (jnp.maximum((b * nb + (nb - 1 - j)) * hb - 1, 0), 0))
    return _call(
        body, comm, name="conv_bwd", grid=(B, nb),
        in_specs=[tok(CONV_CH), tok(CONV_CH), tok(CONV_CH), tok(CONV_CH), prev, prev, _resident(cw), _resident(lng),
                  _resident(lnb)],
        out_specs=[tok(CONV_CH), tok(CONV_CH), _acc_spec((CONV_HALO, CONV_CH)), _acc_spec((8, CONV_CH))],
        out_shape=[_sds((T, CONV_CH), BF16), _sds((T, CONV_CH), BF16), _sds((CONV_HALO, CONV_CH), F32),
                   _sds((8, CONV_CH), F32)],
        scratch_shapes=[pltpu.VMEM((CONV_HALO + CB, CONV_CH), F32), pltpu.VMEM((CB + CONV_HALO, CONV_CH), F32)],
        args=(dco, y, u, gt, u, gt, cw, lng, lnb))


def _bwd_in(dz, w_in, x2, dx1, g_mix, T):
    TB = min(512, T)
    nb = T // TB

    def body(dz_ref, w_ref, x_ref, d1_ref, g_ref, gx_ref, dg_ref):
        i = pl.program_id(0)

        @pl.when(i == 0)
        def _():
            dg_ref[...] = jnp.zeros_like(dg_ref)

        dh = _dot_nt(dz_ref[...], w_ref[...])
        r0, xh0 = _rms(x_ref[...])
        dx, dg_tok = _rms_bwd(dh, xh0, r0, g_ref[...])
        gx_ref[...] = d1_ref[...] + dx
        dg_ref[...] += jnp.sum(dg_tok, axis=0, keepdims=True)

    tok = lambda w: pl.BlockSpec((TB, w), lambda i: (i, 0))
    return pl.pallas_call(
        body, name="bwd_in", grid=(nb,),
        in_specs=[tok(D_IN_PAD), _resident(w_in), tok(D), tok(D), _resident(g_mix)],
        out_specs=[tok(D), _acc_spec((1, D))],
        out_shape=[_sds((T, D), F32), _sds((1, D), F32)],
        compiler_params=_params(1),
    )(dz, w_in, x2, dx1, g_mix)


def _dw(a, b, name, tn, slabs=False):
    T, K = a.shape
    N = b.shape[1]
    tk = K if K <= 1024 else K // 2
    tt = min(512, T)
    nt = T // tt

    def body(a_ref, b_ref, o_ref, acc):
        t = pl.program_id(2)

        @pl.when(t == 0)
        def _():
            acc[...] = jnp.zeros_like(acc)

        acc[...] += _dot_tn(a_ref[...].astype(BF16), b_ref[...].astype(BF16))

        @pl.when(t == nt - 1)
        def _():
            o_ref[...] = acc[...]

    return pl.pallas_call(
        body, name=name, grid=(K // tk, N // tn, nt),
        in_specs=[pl.BlockSpec((tt, tk), lambda i, j, t: (t, i)), pl.BlockSpec((tt, tn), lambda i, j, t: (t, j))],
        out_specs=(pl.BlockSpec((None, tk, tn), lambda i, j, t: (j, i, 0)) if slabs
                   else pl.BlockSpec((tk, tn), lambda i, j, t: (i, j))),
        out_shape=_sds((N // tn, K, tn) if slabs else (K, N), F32),
        scratch_shapes=[pltpu.VMEM((tk, tn), F32)],
        compiler_params=_params(3),
    )(a, b)


def _pos():
    return lax.axis_index("x"), lax.axis_index("y"), lax.axis_index("c")


def _remote(src, dst, ssem, rsem, to):
    return pltpu.make_async_remote_copy(src_ref=src, dst_ref=dst, send_sem=ssem, recv_sem=rsem, device_id=to,
                                        device_id_type=MESH)


def _half(ref_rows, c):
    H = ref_rows // 2
    return pl.ds(pl.multiple_of(c * H, 16), H)


class _Comm:
    def __init__(self, ins, out_shapes, sems, start, finish):
        self.ins, self.out_shapes, self.sems, self.start, self.finish = list(ins), list(out_shapes), list(sems), start, finish


def _ag_comm(shards):
    n = len(shards)

    def parts(ins, outs, sems):
        send_sems, recv_sems, local_sems = sems
        x, y, c = _pos()
        me, sib = (x, y, c), (x, y, 1 - c)
        chips = [(1 - x, y), (x, 1 - y), (1 - x, 1 - y)]

        def rows(w, px, py, pc):
            return outs[w].at[2 * px + py, _half(shards[w].shape[0], pc), :]

        def copy(w, k, block, to, src=None):
            return _remote(rows(w, *block) if src is None else src, rows(w, *block), send_sems.at[w, k],
                           recv_sems.at[w, k], to)

        mine, first = [], []
        for w in range(n):
            src = ins[w].at[_half(shards[w].shape[0], c), :]
            mine.append(pltpu.make_async_copy(src, rows(w, *me), local_sems.at[w]))
            first += [copy(w, 0, me, sib, src=src)] + [copy(w, 1 + j, me, (*chip, c), src=src) for j, chip in enumerate(chips)]
        return c, me, sib, chips, copy, mine, first

    def start(ins, outs, sems):
        _, _, _, _, _, mine, first = parts(ins, outs, sems)
        for cp in mine + first:
            cp.start()

    def finish(ins, outs, sems):
        c, me, sib, chips, copy, mine, first = parts(ins, outs, sems)
        passed = []
        for w in range(n):
            for j, chip in enumerate(chips):
                copy(w, 1 + j, (*chip, c), me).wait_recv()
                passed.append(copy(w, 4 + j, (*chip, c), sib))
                passed[-1].start()
        for w in range(n):
            copy(w, 0, sib, me).wait_recv()
            for j, chip in enumerate(chips):
                copy(w, 4 + j, (*chip, 1 - c), me).wait_recv()
        for cp in first + passed:
            cp.wait_send()
        for cp in mine:
            cp.wait()

    D7 = pltpu.SemaphoreType.DMA((n, 7))
    return _Comm(shards, [_sds((4,) + s.shape, s.dtype) for s in shards], [D7, D7, pltpu.SemaphoreType.DMA((n,))],
                 start, finish)


def _sibling_comm(gs):
    n = len(gs)

    def copies(ins, outs, sems):
        send_sems, recv_sems = sems
        x, y, c = _pos()
        return [_remote(ins[w].at[s, _half(gs[w].shape[1], 1 - c), :], outs[w].at[s], send_sems.at[w, s],
                        recv_sems.at[w, s], (x, y, 1 - c)) for w in range(n) for s in range(4)]

    def start(ins, outs, sems):
        for cp in copies(ins, outs, sems):
            cp.start()

    def finish(ins, outs, sems):
        for cp in copies(ins, outs, sems):
            cp.wait()

    D4 = pltpu.SemaphoreType.DMA((n, 4))
    return _Comm(gs, [_sds((4, g.shape[1] // 2, g.shape[2]), F32) for g in gs], [D4, D4], start, finish)


def _ici_comm(pbs):
    n = len(pbs)

    def copies(ins, outs, sems):
        send_sems, recv_sems = sems
        x, y, c = _pos()
        return [_remote(ins[w].at[2 * tx + ty], outs[w].at[j], send_sems.at[w, j], recv_sems.at[w, j], (tx, ty, c))
                for w in range(n) for j, (tx, ty) in enumerate([(1 - x, y), (x, 1 - y), (1 - x, 1 - y)])]

    def start(ins, outs, sems):
        for cp in copies(ins, outs, sems):
            cp.start()

    def finish(ins, outs, sems):
        for cp in copies(ins, outs, sems):
            cp.wait()

    D3 = pltpu.SemaphoreType.DMA((n, 3))
    return _Comm(pbs, [_sds((3,) + p.shape[1:], BF16) for p in pbs], [D3, D3], start, finish)


def _run_comm(comm, name):
    ni, no = len(comm.ins), len(comm.out_shapes)

    def body(*refs):
        ins, outs, sems = refs[:ni], refs[ni:ni + no], refs[ni + no:]
        comm.start(ins, outs, sems)
        comm.finish(ins, outs, sems)

    return pl.pallas_call(body, name=name, out_shape=comm.out_shapes, in_specs=[ANY] * ni, out_specs=[ANY] * no,
                          scratch_shapes=comm.sems)(*comm.ins)


def _call(body, comm, *, name, grid, in_specs, out_specs, out_shape, scratch_shapes, args):
    n_grid = len(grid)
    if comm is None:
        res = pl.pallas_call(body, name=name, grid=grid, in_specs=in_specs, out_specs=out_specs, out_shape=out_shape,
                             scratch_shapes=scratch_shapes, compiler_params=_params(n_grid))(*args)
        return list(res), []
    n_in, n_out, n_scr = len(in_specs), len(out_specs), len(scratch_shapes)
    ni, no = len(comm.ins), len(comm.out_shapes)

    def carried(*refs):
        ins, refs = refs[:n_in], refs[n_in:]
        cins, refs = refs[:ni], refs[ni:]
        outs, refs = refs[:n_out], refs[n_out:]
        couts, refs = refs[:no], refs[no:]
        scr, csems = refs[:n_scr], refs[n_scr:]
        ids = [pl.program_id(ax) for ax in range(n_grid)]
        first = functools.reduce(jnp.logical_and, [i == 0 for i in ids])
        last = functools.reduce(jnp.logical_and, [i == g - 1 for i, g in zip(ids, grid)])

        @pl.when(first)
        def _():
            comm.start(cins, couts, csems)

        body(*ins, *outs, *scr)

        @pl.when(last)
        def _():
            comm.finish(cins, couts, csems)

    res = pl.pallas_call(
        carried, name=name, grid=grid, in_specs=list(in_specs) + [ANY] * ni, out_specs=list(out_specs) + [ANY] * no,
        out_shape=list(out_shape) + comm.out_shapes, scratch_shapes=list(scratch_shapes) + comm.sems,
        compiler_params=_params(n_grid))(*args, *comm.ins)
    return list(res[:n_out]), list(res[n_out:])


def _sibling_share(gs):
    n = len(gs)

    def body(*refs):
        outs = refs[n:2 * n]
        send_sems, recv_sems = refs[2 * n:]
        x, y, c = _pos()
        cps = []
        for w in range(n):
            mine = outs[w].at[_half(gs[w].shape[0], c), :]
            cps.append(_remote(mine, mine, send_sems.at[w], recv_sems.at[w], (x, y, 1 - c)))
            cps[-1].start()
        for cp in cps:
            cp.wait()

    return pl.pallas_call(
        body, name="rs_share", out_shape=[_sds(g.shape, F32) for g in gs],
        in_specs=[ANY] * n, out_specs=[ANY] * n, input_output_aliases={w: w for w in range(n)},
        scratch_shapes=[pltpu.SemaphoreType.DMA((n,)), pltpu.SemaphoreType.DMA((n,))],
    )(*gs)


def _small_allreduce(v, name):
    P = v.shape[0]
    vm = pl.BlockSpec(memory_space=pltpu.VMEM)

    def body(v_ref, o_ref, gath, send_sems, recv_sems):
        x, y, c = _pos()
        me = 4 * x + 2 * y + c
        gath[me] = v_ref[...]
        cps = []
        for r in range(1, 8):
            tx = (1 - x) if r & 4 else x
            ty = (1 - y) if r & 2 else y
            tc = (1 - c) if r & 1 else c
            cps.append(_remote(v_ref, gath.at[me], send_sems.at[r - 1], recv_sems.at[r - 1], (tx, ty, tc)))
            cps[-1].start()
        for cp in cps:
            cp.wait()
        acc = gath[0]
        for d in range(1, 8):
            acc = acc + gath[d]
        o_ref[...] = acc

    return pl.pallas_call(
        body, name=name, out_shape=_sds((P, LANES), F32), in_specs=[vm], out_specs=vm,
        scratch_shapes=[pltpu.VMEM((8, P, LANES), F32), pltpu.SemaphoreType.DMA((7,)), pltpu.SemaphoreType.DMA((7,))],
    )(v)


def _chip_sum(g, rcv, pos, name):
    _, R, C = g.shape
    H = R // 2

    def body(pos_ref, g_ref, r_ref, o_ref):
        o_ref[...] = (g_ref[...] + r_ref[...]).astype(BF16)

    return pl.pallas_call(
        body, name=name, out_shape=_sds((4, H, C), BF16),
        grid_spec=pltpu.PrefetchScalarGridSpec(
            num_scalar_prefetch=1, grid=(4,),
            in_specs=[pl.BlockSpec((1, H, C), lambda s, pos: (s, pos[0], 0)),
                      pl.BlockSpec((1, H, C), lambda s, pos: (s, 0, 0))],
            out_specs=pl.BlockSpec((1, H, C), lambda s, pos: (s, 0, 0))),
        compiler_params=_params(1),
    )(pos, g, rcv)


def _final_sum(g, rcv, rc, pos, name):
    _, R, C = g.shape
    Q = R // 4

    def body(pos_ref, g_ref, r_ref, rc_ref, o_ref):
        acc = g_ref[0] + r_ref[0]
        for j in range(3):
            acc = acc + rc_ref[j].astype(F32)
        o_ref[...] = acc

    return pl.pallas_call(
        body, name=name, out_shape=_sds((R, C), F32),
        grid_spec=pltpu.PrefetchScalarGridSpec(
            num_scalar_prefetch=1, grid=(2,),
            in_specs=[pl.BlockSpec((1, Q, C), lambda i, pos: (pos[1], pos[0] * 2 + i, 0)),
                      pl.BlockSpec((1, Q, C), lambda i, pos: (pos[1], i, 0)),
                      pl.BlockSpec((3, Q, C), lambda i, pos: (0, i, 0))],
            out_specs=pl.BlockSpec((Q, C), lambda i, pos: (pos[0] * 2 + i, 0))),
        compiler_params=_params(1),
    )(pos, g, rcv, rc)


def _adamw_math(w, g, m, v):
    m = ADAM_B1 * m + (1.0 - ADAM_B1) * g
    v = ADAM_B2 * v + (1.0 - ADAM_B2) * (g * g)
    m_hat = m / (1.0 - ADAM_B1 ** ADAM_STEP)
    v_hat = v / (1.0 - ADAM_B2 ** ADAM_STEP)
    delta = -ADAM_LR * (m_hat / (jnp.sqrt(v_hat) + ADAM_EPS) + ADAM_WD * w)
    return delta, m, v


def _adamw(w, g, m, v, name, rb):
    R, C = w.shape

    def body(w_ref, g_ref, m_ref, v_ref, d_ref, nm_ref, nv_ref):
        d, nm, nv = _adamw_math(w_ref[...], g_ref[...], m_ref[...], v_ref[...])
        d_ref[...] = d
        nm_ref[...] = nm
        nv_ref[...] = nv

    blk = pl.BlockSpec((rb, C), lambda i: (i, 0))
    return pl.pallas_call(
        body, name=name, grid=(R // rb,), in_specs=[blk] * 4, out_specs=[blk] * 3,
        out_shape=[_sds((R, C), F32)] * 3, compiler_params=_params(1),
    )(w, g, m, v)


SMALL = (("g_mix", 8), ("b_f", 8), ("conv_w", None), ("conv_b", 8), ("ln_g", 8), ("ln_b", 8), ("g_x", 8), ("g_mem", 8),
         ("g_ffn", 8), ("g_final", 8), ("loss", 8))


def _pack_small(parts, conv_rows):
    rows = []
    for name, n in SMALL:
        if name not in parts:
            continue
        n = conv_rows if n is None else n
        flat = parts[name].reshape(-1).astype(F32)
        flat = jnp.pad(flat, (0, n * LANES - flat.shape[0]))
        rows.append(flat.reshape(n, LANES))
    return jnp.concatenate(rows, axis=0)


def _unpack_small(p, shapes, conv_rows):
    out, off = {}, 0
    for name, n in SMALL:
        if name not in shapes:
            continue
        n = conv_rows if n is None else n
        size = math.prod(shapes[name])
        out[name] = p[off:off + n].reshape(-1)[:size].reshape(shapes[name])
        off += n
    return out


def kernel(x, mem, g_mix, w_in, b_f, conv_w, conv_b, ln_g, ln_b, w_out, g_x, g_mem, w_mq, w_mkv, w_mo, g_ffn, w_gu, w_down, g_final, loss_target, m_g_mix, m_w_in, m_b_f, m_conv_w, m_conv_b, m_ln_g, m_ln_b, m_w_out, m_g_x, m_g_mem, m_w_mq, m_w_mkv, m_w_mo, m_g_ffn, m_w_gu, m_w_down, m_g_final, v_g_mix, v_w_in, v_b_f, v_conv_w, v_conv_b, v_ln_g, v_ln_b, v_w_out, v_g_x, v_g_mem, v_w_mq, v_w_mkv, v_w_mo, v_g_ffn, v_w_gu, v_w_down, v_g_final):
    names = ["g_mix", "w_in", "b_f", "conv_w", "conv_b", "ln_g", "ln_b", "w_out", "g_x", "g_mem", "w_mq", "w_mkv",
             "w_mo", "g_ffn", "w_gu", "w_down", "g_final"]
    W = dict(zip(names, [g_mix, w_in, b_f, conv_w, conv_b, ln_g, ln_b, w_out, g_x, g_mem, w_mq, w_mkv, w_mo, g_ffn,
                         w_gu, w_down, g_final]))
    M = dict(zip(names, [m_g_mix, m_w_in, m_b_f, m_conv_w, m_conv_b, m_ln_g, m_ln_b, m_w_out, m_g_x, m_g_mem, m_w_mq,
                         m_w_mkv, m_w_mo, m_g_ffn, m_w_gu, m_w_down, m_g_final]))
    V = dict(zip(names, [v_g_mix, v_w_in, v_b_f, v_conv_w, v_conv_b, v_ln_g, v_ln_b, v_w_out, v_g_x, v_g_mem, v_w_mq,
                         v_w_mkv, v_w_mo, v_g_ffn, v_w_gu, v_w_down, v_g_final]))
    big_names = [n for n, _, _, _ in BIG]
    B, S, _ = x.shape
    T = B * S
    mx, my, mc = _pos()
    chip = 2 * mx + my
    pos = jnp.stack([mc, chip]).astype(jnp.int32)

    shard2d = lambda a: a.reshape(a.shape[-2], a.shape[-1])
    shard_bf = {n: shard2d(W[n]).astype(BF16) for n in big_names}
    ag_mid = ["w_mkv", "w_out", "w_mq", "w_mo"]
    ag_ffn = ["w_gu", "w_down"]
    slab = {"w_in": _run_comm(_ag_comm([shard_bf["w_in"]]), "ag_w_in")[0]}
    w_in_f = jnp.pad(jnp.transpose(slab["w_in"], (1, 0, 2)).reshape(D, D_IN), ((0, 0), (0, D_IN_PAD - D_IN)))
    cw_mine = jnp.pad(shard2d(conv_w), ((0, 1), (0, 0)))
    cw_slot = lax.dynamic_update_slice(jnp.zeros((CONV_HALO, CONV_CH), F32), cw_mine, (0, chip * LANES))
    cw = _small_allreduce(cw_slot.reshape(CONV_HALO * 4, LANES) * 0.5, "gather_conv_w").reshape(CONV_HALO, CONV_CH)

    row = lambda a: a.reshape(1, -1)
    bf_pad = jnp.pad(row(b_f), ((0, 0), (0, LANES - 8)))
    x2d = x.reshape(T, D)
    mem2d = mem.reshape(B * MEM_LEN, D)
    tgt = loss_target.reshape(T, D)

    h, u, gt, q, k, v, zf, c, cq = _fwd_in(x2d, row(g_mix), w_in_f, bf_pad, B, S)
    ckT = jnp.transpose(c.reshape(B, S, LANES)[:, :, :8], (0, 2, 1)).reshape(B, N_PAIR, 2, S)
    ckT = jnp.pad(ckT, ((0, 0), (0, 0), (0, 6), (0, 0)))
    (y, co), got = _conv_fwd(u, gt, cw, row(conv_b), row(ln_g), row(ln_b), B, S, comm=_ag_comm([shard_bf[n] for n in ag_mid]))
    slab.update(zip(ag_mid, got))
    (o, fox_bias), got = _fox_fwd(q, k, v, cq, ckT, B, S, comm=_ag_comm([shard_bf[n] for n in ag_ffn]))
    slab.update(zip(ag_ffn, got))
    full = {n: slab[n] if by_col else slab[n].reshape(4 * r, c) for n, r, c, by_col in BIG}
    mn, km, vm = _mem_kv(mem2d, row(g_mem), full["w_mkv"], B)
    x1, hx, qm, om, x2, cat = _fwd_mid(x2d, co, o, km, vm, full["w_out"], full["w_mq"], full["w_mo"], row(g_x), B, S)
    hf, gu, act, dx3, loss_p, dg_final = _fwd_ffn(x2, tgt, full["w_gu"], full["w_down"], row(g_ffn), row(g_final), T)

    pos_sum = lambda gs, rcvs, ns: [_chip_sum(g, r, pos, "rs_chip_sum_" + n) for g, r, n in zip(gs, rcvs, ns)]
    fin_sum = lambda gs, rcvs, rcs, ns: [_final_sum(g, r, q3, pos, "rs_final_sum_" + n)
                                         for g, r, q3, n in zip(gs, rcvs, rcs, ns)]
    RH = {}
    dgu, dx2, dg_ffn = _bwd_ffn(dx3, gu, x2, full["w_gu"], full["w_down"], row(g_ffn), T)
    g_ffn_w = [_dw(hf, dgu, "dw_gu", FF_CHUNK, slabs=True), _dw(act, dx3, "dw_down", 512).reshape(4, D_FF // 4, D)]
    (dx1, dqm, dco, do, dd, dkm, dvm, dg_x), rcv_ffn = _bwd_mid(dx2, x1, qm, km, vm, o, full["w_mo"], full["w_mq"],
                                                                full["w_out"], row(g_x), B, S, comm=_sibling_comm(g_ffn_w))
    pb_ffn = pos_sum(g_ffn_w, rcv_ffn, ag_ffn)
    dkv, dg_mem = _mem_bwd(dkm, dvm, mem2d, full["w_mkv"], row(g_mem), B)
    g_mid_w = [_dw(mn, dkv, "dw_mkv", 512, slabs=True), _dw(cat, dx1, "dw_out", 512).reshape(4, 256, D),
               _dw(hx, dqm, "dw_mq", 512).reshape(4, 256, D), _dw(om, dx2, "dw_mo", 512).reshape(4, 256, D)]
    (dq, dk, dv, dc, dcq), rc_ffn = _fox_bwd(q, k, v, do, fox_bias, dd, ckT, B, S, comm=_ici_comm(pb_ffn))
    RH.update(zip(ag_ffn, fin_sum(g_ffn_w, rcv_ffn, rc_ffn, ag_ffn)))
    rcv_mid = _run_comm(_sibling_comm(g_mid_w), "rs_sibling_mid")
    pb_mid = pos_sum(g_mid_w, rcv_mid, ag_mid)
    dc8 = jnp.transpose(dc[:, :, :2, :].reshape(B, 8, S), (0, 2, 1)).reshape(T, 8)
    dc8 = dc8 + dcq.reshape(T, 8, HEAD_D)[:, :, 0]
    dzf, dbf = _fgate_bwd(jnp.pad(dc8, ((0, 0), (0, LANES - 8))), zf, B, S)
    (du, dgt, dcw, dvec), rc_mid = _conv_bwd(dco, y, u, gt, cw, row(ln_g), row(ln_b), B, S, comm=_ici_comm(pb_mid))
    RH.update(zip(ag_mid, fin_sum(g_mid_w, rcv_mid, rc_mid, ag_mid)))
    dz = jnp.concatenate([du, dgt, dq, dk, dv, dzf], axis=1)
    grad_x, dg_mix = _bwd_in(dz, w_in_f, x2d, dx1, row(g_mix), T)
    dw_in = _dw(h, dz, "dw_in", 384)[:, :D_IN]
    g_in_w = [jnp.transpose(dw_in.reshape(D, 4, D_IN // 4), (1, 0, 2))]
    rcv_in = _run_comm(_sibling_comm(g_in_w), "rs_sibling_in")
    rc_in = _run_comm(_ici_comm(pos_sum(g_in_w, rcv_in, ["w_in"])), "rs_ici_in")
    RH.update(zip(["w_in"], fin_sum(g_in_w, rcv_in, rc_in, ["w_in"])))
    G = dict(zip(big_names, _sibling_share([RH[n] for n in big_names])))
    DL, NM, NV = {}, {}, {}
    for n in big_names:
        DL[n], NM[n], NV[n] = _adamw(shard2d(W[n]), G[n], shard2d(M[n]), shard2d(V[n]), "adamw_" + n, G[n].shape[0] // 2)

    small_g = {"g_mix": dg_mix, "b_f": dbf[:, :8], "conv_w": dcw, "conv_b": dvec[0], "ln_g": dvec[1], "ln_b": dvec[2],
               "g_x": dg_x, "g_mem": dg_mem, "g_ffn": dg_ffn, "g_final": dg_final, "loss": loss_p[:, :1]}
    sg = _small_allreduce(_pack_small(small_g, CONV_HALO * 4), "allreduce_small")
    shapes = {n: W[n].shape for n in names if n not in big_names}
    shapes["conv_w"] = (CONV_HALO, CONV_CH)
    shapes["loss"] = (1,)
    sgrads = _unpack_small(sg, shapes, CONV_HALO * 4)
    loss = sgrads.pop("loss")[0]
    sgrads["conv_w"] = lax.dynamic_slice(sgrads["conv_w"], (0, chip * LANES), (CONV_K, LANES)).reshape(W["conv_w"].shape)
    spack = lambda d: _pack_small({n: d[n] for n in sgrads}, CONV_HALO)
    sd, snm, snv = _adamw(spack(W), spack(sgrads), spack(M), spack(V), "adamw_small", 8)
    sshapes = {n: W[n].shape for n in sgrads}
    SD, SNM, SNV = (_unpack_small(a, sshapes, CONV_HALO) for a in (sd, snm, snv))

    def collect(bigs, smalls):
        return [bigs[n].reshape(W[n].shape) if n in big_names else smalls[n] for n in names]

    return (loss, grad_x.reshape(x.shape), *collect(G, sgrads), *collect(DL, SD), *collect(NM, SNM), *collect(NV, SNV))
```

```python
import functools
import math

import jax
import jax.numpy as jnp
from jax import lax
from jax.experimental import pallas as pl
from jax.experimental.pallas import tpu as pltpu

F32, BF16 = jnp.float32, jnp.bfloat16
HIGHEST = lax.Precision.HIGHEST
MESH = pl.DeviceIdType.MESH

D = 1024
CONV_CH = 512
CONV_K = 31
CONV_HALO = 32
FOX_W = 512
HEAD_D = 64
N_PAIR = 4
MEM_LEN = 256
MEM_HEADS = 4
MEM_HD = 256
D_FF = 2816
FF_CHUNK = 1408
D_IN = 2568
D_IN_PAD = 2688
OFF_F = 2560
EPS = 1e-6
LANES = 128

ADAM_LR, ADAM_B1, ADAM_B2, ADAM_EPS, ADAM_WD, ADAM_STEP = 0.001, 0.9, 0.999, 1e-08, 0.01, 10

VMEM_LIMIT = 60 * 1024 * 1024

BIG = (("w_out", 256, 1024, False), ("w_mq", 256, 1024, False), ("w_mkv", 1024, 512, True),
       ("w_mo", 256, 1024, False), ("w_gu", 1024, 1408, True), ("w_down", 704, 1024, False),
       ("w_in", 1024, 642, True))

ANY = pl.BlockSpec(memory_space=pl.ANY)


def _sig(x):
    return 1.0 / (1.0 + jnp.exp(-x))


def _dot(a, b):
    return jnp.dot(a, b, preferred_element_type=F32)


def _dot_nt(a, b):
    return lax.dot_general(a, b, (((1,), (1,)), ((), ())), preferred_element_type=F32)


def _dot_tn(a, b):
    return lax.dot_general(a, b, (((0,), (0,)), ((), ())), preferred_element_type=F32)


def _dot_hi(a, b):
    return jnp.dot(a, b, precision=HIGHEST, preferred_element_type=F32)


def _resident(a):
    nd = a.ndim
    return pl.BlockSpec(a.shape, lambda *_: (0,) * nd, pipeline_mode=pl.Buffered(1))


def _acc_spec(shape):
    nd = len(shape)
    return pl.BlockSpec(shape, lambda *_: (0,) * nd)


def _params(n_grid):
    return pltpu.CompilerParams(dimension_semantics=("arbitrary",) * n_grid, vmem_limit_bytes=VMEM_LIMIT)


def _sds(shape, dtype):
    return jax.ShapeDtypeStruct(shape, dtype)


def _rms(x):
    r = lax.rsqrt(jnp.mean(x * x, axis=-1, keepdims=True) + EPS)
    return r, x * r


def _rms_bwd(dy, xh, r, g):
    dxh = dy * g
    dx = r * (dxh - xh * jnp.mean(dxh * xh, axis=-1, keepdims=True))
    return dx, dy * xh


def _head_expand(rows, cols):
    hd = lax.broadcasted_iota(jnp.int32, (rows, cols), 1) // HEAD_D
    hr = lax.broadcasted_iota(jnp.int32, (rows, cols), 0)
    return (hd == hr).astype(F32)


def _head_sum(n):
    hc = lax.broadcasted_iota(jnp.int32, (n, n), 1) // HEAD_D
    hr = lax.broadcasted_iota(jnp.int32, (n, n), 0) // HEAD_D
    return (hc == hr).astype(F32)


def _fwd_in(x2, g_mix, w_in, bf_pad, B, S):
    T = B * S
    TB = min(512, S)
    nb = S // TB

    def body(x_ref, g_ref, w_ref, bf_ref, h_ref, u_ref, gt_ref, q_ref, k_ref, v_ref, zf_ref, c_ref, cq_ref,
             carry):
        j = pl.program_id(1)

        @pl.when(j == 0)
        def _():
            carry[...] = jnp.zeros_like(carry)

        _, xh = _rms(x_ref[...])
        h = (xh * g_ref[...]).astype(BF16)
        h_ref[...] = h
        u_ref[...] = _dot(h, w_ref[:, 0:512])
        gt_ref[...] = _dot(h, w_ref[:, 512:1024])
        q_ref[...] = _dot(h, w_ref[:, 1024:1536]).astype(BF16)
        k_ref[...] = _dot(h, w_ref[:, 1536:2048]).astype(BF16)
        v_ref[...] = _dot(h, w_ref[:, 2048:2560]).astype(BF16)
        zf = _dot(h, w_ref[:, OFF_F:D_IN_PAD]) + bf_ref[...]
        zf_ref[...] = zf
        lane = lax.broadcasted_iota(jnp.int32, zf.shape, 1)
        logf = jnp.where(lane < 8, jnp.minimum(zf, 0.0) - jnp.log(1.0 + jnp.exp(-jnp.abs(zf))), 0.0)
        row = lax.broadcasted_iota(jnp.int32, (TB, TB), 0)
        col = lax.broadcasted_iota(jnp.int32, (TB, TB), 1)
        c = _dot_hi((row >= col).astype(F32), logf) + carry[0:1, :]
        carry[0:1, :] = c[TB - 1:TB, :]
        c_ref[...] = c
        cq_ref[...] = _dot_hi(c, _head_expand(LANES, FOX_W))

    tok = lambda w: pl.BlockSpec((TB, w), lambda b, j: (b * nb + j, 0))
    outs = [(D, BF16), (512, F32), (512, F32), (512, BF16), (512, BF16), (512, BF16), (LANES, F32),
            (LANES, F32), (FOX_W, F32)]
    return pl.pallas_call(
        body, name="fwd_in", grid=(B, nb),
        in_specs=[tok(D), _resident(g_mix), _resident(w_in), _resident(bf_pad)],
        out_specs=[tok(w) for w, _ in outs],
        out_shape=[_sds((T, w), dt) for w, dt in outs],
        scratch_shapes=[pltpu.VMEM((8, LANES), F32)],
        compiler_params=_params(2),
    )(x2, g_mix, w_in, bf_pad)


def _layernorm_silu(y, lg, lb):
    mu = jnp.mean(y, axis=-1, keepdims=True)
    yc = y - mu
    rs = lax.rsqrt(jnp.mean(yc * yc, axis=-1, keepdims=True) + EPS)
    n = yc * rs
    l = n * lg + lb
    return rs, n, l


SUB = 8


def _shifted_copies(cat, sh, rows):
    for r in range(1, SUB):
        sh[r, 0:rows, :] = cat[r:r + rows, :]


def _tap(cat, sh, off, rows):
    r = off % SUB
    return cat[off:off + rows, :] if r == 0 else sh[r, off - r:off - r + rows, :]


def _conv_fwd(u, gt, cw, cb, lng, lnb, B, S, comm=None):
    T = B * S
    CB = min(256, S)
    nb = S // CB

    def body(u_ref, gt_ref, w_ref, cb_ref, lg_ref, lb_ref, y_ref, co_ref, acat, ash):
        j = pl.program_id(1)

        @pl.when(j == 0)
        def _():
            acat[0:CONV_HALO, :] = jnp.zeros((CONV_HALO, CONV_CH), F32)

        acat[CONV_HALO:CONV_HALO + CB, :] = u_ref[...] * _sig(gt_ref[...])
        _shifted_copies(acat, ash, CB + CONV_HALO - SUB)
        acc = jnp.zeros((CB, CONV_CH), F32) + cb_ref[...]
        for k in range(CONV_K):
            acc = acc + w_ref[k:k + 1, :] * _tap(acat, ash, CONV_HALO - (CONV_K - 1) + k, CB)
        acat[0:CONV_HALO, :] = acat[CB:CB + CONV_HALO, :]
        y_ref[...] = acc
        _, _, l = _layernorm_silu(acc, lg_ref[...], lb_ref[...])
        co_ref[...] = (l * _sig(l)).astype(BF16)

    tok = lambda w: pl.BlockSpec((CB, w), lambda b, j: (b * nb + j, 0))
    return _call(
        body, comm, name="conv_fwd", grid=(B, nb),
        in_specs=[tok(CONV_CH), tok(CONV_CH), _resident(cw), _resident(cb), _resident(lng), _resident(lnb)],
        out_specs=[tok(CONV_CH), tok(CONV_CH)],
        out_shape=[_sds((T, CONV_CH), F32), _sds((T, CONV_CH), BF16)],
        scratch_shapes=[pltpu.VMEM((CONV_HALO + CB, CONV_CH), F32),
                        pltpu.VMEM((SUB, CB + CONV_HALO - SUB, CONV_CH), F32)],
        args=(u, gt, cw, cb, lng, lnb))


def _fox_fwd(q, k, v, cq, ckT, B, S, comm=None):
    T = B * S
    TQ = min(256, S)
    nq = S // TQ
    scale = 1.0 / math.sqrt(HEAD_D)
    one_lane = (HEAD_D, 0)

    def body(q_ref, k_ref, v_ref, cq_ref, ck_ref, o_ref, lse_ref, s_scr, s_odd, m_scr, acc_scr):
        i = pl.program_id(2)
        lane = lax.broadcasted_iota(jnp.int32, (TQ, LANES), 1)
        lo = lane < HEAD_D
        qs = q_ref[...] * jnp.asarray(scale, BF16)
        zero = jnp.zeros_like(qs)
        qh = (jnp.where(lo, qs, zero), jnp.where(lo, zero, qs))
        cqv = cq_ref[...]
        cq_rep = tuple(jnp.broadcast_to(cqv[:, h * HEAD_D:h * HEAD_D + 1], (TQ, LANES)) for h in range(2))
        m_scr[...] = jnp.full(m_scr.shape, -1e30, F32)
        acc_scr[...] = jnp.zeros_like(acc_scr)
        row = lax.broadcasted_iota(jnp.int32, (TQ, TQ), 0)
        col = lax.broadcasted_iota(jnp.int32, (TQ, TQ), 1)
        wide = lambda x: jnp.concatenate([x, x], axis=1) if TQ == 2 * LANES else jnp.tile(x, (1, TQ // LANES))

        def scores(j, s_buf):
            kj = k_ref[pl.ds(pl.multiple_of(j * TQ, TQ), TQ), :]
            for h in range(2):
                s_buf[h] = _dot_nt(qh[h], kj)

        def softmax_step(j, s_buf, diagonal):
            start = pl.multiple_of(j * TQ, TQ)
            vj = v_ref[pl.ds(start, TQ), :]
            for h in range(2):
                ck = ck_ref[0, 0, h:h + 1, pl.ds(start, TQ)]

                def logits():
                    t = (s_buf[h] - ck) + wide(cq_rep[h])
                    return jnp.where(col <= row, t, -1e30) if diagonal else t

                m_old = m_scr[h]
                m_new = jnp.maximum(m_old, jnp.max(logits(), axis=-1, keepdims=True))
                alpha = jnp.exp(m_old - m_new)
                m_scr[h] = m_new
                p = jnp.exp(logits() - wide(m_new)).astype(BF16)
                vx = jnp.where(lane == one_lane[h], jnp.ones_like(vj), jnp.where(lo if h == 0 else ~lo, vj, jnp.zeros_like(vj)))
                acc_scr[h] = alpha * acc_scr[h] + _dot(p, vx)

        def two_blocks(jj, carry):
            j = 2 * jj
            scores(j + 1, s_odd)
            softmax_step(j, s_scr, False)
            scores(j + 2, s_scr)
            softmax_step(j + 1, s_odd, False)
            return carry

        scores(0, s_scr)
        lax.fori_loop(0, i // 2, two_blocks, 0)

        @pl.when(i % 2 == 0)
        def _():
            softmax_step(i, s_scr, True)

        @pl.when(i % 2 == 1)
        def _():
            scores(i, s_odd)
            softmax_step(i - 1, s_scr, False)
            softmax_step(i, s_odd, True)

        acc_a, acc_b = acc_scr[0], acc_scr[1]
        l_a = acc_a[:, one_lane[0]:one_lane[0] + 1]
        l_b = acc_b[:, one_lane[1]:one_lane[1] + 1]
        o_ref[...] = jnp.where(lo, acc_a / l_a, acc_b / l_b)
        lse_ref[...] = cqv - jnp.where(lo, m_scr[0] + jnp.log(l_a), m_scr[1] + jnp.log(l_b))

    qspec = pl.BlockSpec((TQ, LANES), lambda b, p, i: (b * nq + i, p))
    kspec = pl.BlockSpec((S, LANES), lambda b, p, i: (b, p))
    return _call(
        body, comm, name="fox_fwd", grid=(B, N_PAIR, nq),
        in_specs=[qspec, kspec, kspec, qspec, pl.BlockSpec((1, 1, 8, S), lambda b, p, i: (b, p, 0, 0))],
        out_specs=[qspec, qspec],
        out_shape=[_sds((T, FOX_W), F32), _sds((T, FOX_W), F32)],
        scratch_shapes=[pltpu.VMEM((2, TQ, TQ), F32), pltpu.VMEM((2, TQ, TQ), F32),
                        pltpu.VMEM((2, TQ, LANES), F32), pltpu.VMEM((2, TQ, LANES), F32)],
        args=(q, k, v, cq, ckT))


def _mem_kv(mem2, g_mem, w_mkv, B):
    def body(m_ref, g_ref, w_ref, mn_ref, km_ref, vm_ref):
        _, xh = _rms(m_ref[...])
        mn = (xh * g_ref[...]).astype(BF16)
        mn_ref[...] = mn
        for s in range(2):
            km_ref[:, 512 * s:512 * (s + 1)] = _dot(mn, w_ref[s]).astype(BF16)
            vm_ref[:, 512 * s:512 * (s + 1)] = _dot(mn, w_ref[2 + s]).astype(BF16)

    blk = pl.BlockSpec((MEM_LEN, D), lambda b: (b, 0))
    return pl.pallas_call(
        body, name="mem_kv", grid=(B,),
        in_specs=[blk, _resident(g_mem), _resident(w_mkv)],
        out_specs=[blk, blk, blk],
        out_shape=[_sds((B * MEM_LEN, D), BF16)] * 3,
        compiler_params=_params(1),
    )(mem2, g_mem, w_mkv)


def _mem_probs(qm, km):
    ps = []
    for h in range(MEM_HEADS):
        hs = slice(h * MEM_HD, (h + 1) * MEM_HD)
        lg = _dot_nt(qm[:, hs], km[:, hs]) * (1.0 / math.sqrt(MEM_HD))
        e = jnp.exp(lg - jnp.max(lg, axis=-1, keepdims=True))
        ps.append(e / jnp.sum(e, axis=-1, keepdims=True))
    return ps


def _fwd_mid(x2, co, o, km, vm, w_out, w_mq, w_mo, g_x, B, S):
    T = B * S
    TB = min(512, S)
    nb = S // TB

    def body(x_ref, co_ref, o_ref, km_ref, vm_ref, wo_ref, wq_ref, wm_ref, g_ref,
             x1_ref, hx_ref, qm_ref, om_ref, x2_ref, cat_ref):
        cat_ref[:, 0:CONV_CH] = co_ref[...]
        cat_ref[:, CONV_CH:D] = o_ref[...].astype(BF16)
        x1 = x_ref[...] + _dot(cat_ref[...], wo_ref[...])
        x1_ref[...] = x1
        _, xh = _rms(x1)
        hx = (xh * g_ref[...]).astype(BF16)
        hx_ref[...] = hx
        qm = _dot(hx, wq_ref[...]).astype(BF16)
        qm_ref[...] = qm
        ps = _mem_probs(qm, km_ref[...])
        vmv = vm_ref[...]
        for h in range(MEM_HEADS):
            hs = slice(h * MEM_HD, (h + 1) * MEM_HD)
            om_ref[:, hs] = _dot(ps[h].astype(BF16), vmv[:, hs]).astype(BF16)
        x2_ref[...] = x1 + _dot(om_ref[...], wm_ref[...])

    tok = lambda w: pl.BlockSpec((TB, w), lambda b, j: (b * nb + j, 0))
    memb = pl.BlockSpec((MEM_LEN, D), lambda b, j: (b, 0))
    outs = [(D, F32), (D, BF16), (D, BF16), (D, BF16), (D, F32), (D, BF16)]
    return pl.pallas_call(
        body, name="fwd_mid", grid=(B, nb),
        in_specs=[tok(D), tok(CONV_CH), tok(FOX_W), memb, memb, _resident(w_out), _resident(w_mq), _resident(w_mo),
                  _resident(g_x)],
        out_specs=[tok(w) for w, _ in outs],
        out_shape=[_sds((T, w), dt) for w, dt in outs],
        compiler_params=_params(2),
    )(x2, co, o, km, vm, w_out, w_mq, w_mo, g_x)


def _fwd_ffn(x2, tgt, w_gu, w_down, g_ffn, g_final, T):
    TB = min(256, T)
    nb = T // TB

    def body(x_ref, t_ref, wgu_ref, wd_ref, gf_ref, gl_ref, hf_ref, gu_ref, act_ref, dx3_ref, loss_ref, dgl_ref):
        i = pl.program_id(0)

        @pl.when(i == 0)
        def _():
            loss_ref[...] = jnp.zeros_like(loss_ref)
            dgl_ref[...] = jnp.zeros_like(dgl_ref)

        x2v = x_ref[...]
        _, xh = _rms(x2v)
        hf = (xh * gf_ref[...]).astype(BF16)
        hf_ref[...] = hf
        x3 = x2v
        for ch in range(D_FF // FF_CHUNK):
            c0 = ch * FF_CHUNK
            g = _dot(hf, wgu_ref[ch])
            u = _dot(hf, wgu_ref[2 + ch])
            gu_ref[:, c0:c0 + FF_CHUNK] = g
            gu_ref[:, D_FF + c0:D_FF + c0 + FF_CHUNK] = u
            act = (g * _sig(g) * u).astype(BF16)
            act_ref[:, c0:c0 + FF_CHUNK] = act
            x3 = x3 + _dot(act, wd_ref[c0:c0 + FF_CHUNK, :])
        r3, xh3 = _rms(x3)
        gl = gl_ref[...]
        e = xh3 * gl - t_ref[...]
        loss_ref[...] += jnp.sum(e * e) * (0.5 / D)
        dy = e * (1.0 / D)
        dx3, dgl = _rms_bwd(dy, xh3, r3, gl)
        dx3_ref[...] = dx3
        dgl_ref[...] += jnp.sum(dgl, axis=0, keepdims=True)

    tok = lambda w: pl.BlockSpec((TB, w), lambda i: (i, 0))
    return pl.pallas_call(
        body, name="fwd_ffn", grid=(nb,),
        in_specs=[tok(D), tok(D), _resident(w_gu), _resident(w_down), _resident(g_ffn), _resident(g_final)],
        out_specs=[tok(D), tok(2 * D_FF), tok(D_FF), tok(D), _acc_spec((1, LANES)), _acc_spec((1, D))],
        out_shape=[_sds((T, D), BF16), _sds((T, 2 * D_FF), F32), _sds((T, D_FF), BF16), _sds((T, D), F32),
                   _sds((1, LANES), F32), _sds((1, D), F32)],
        compiler_params=_params(1),
    )(x2, tgt, w_gu, w_down, g_ffn, g_final)


def _bwd_ffn(dx3, gu, x2, w_gu, w_down, g_ffn, T):
    TB = min(256, T)
    nb = T // TB

    def body(d_ref, gu_ref, x_ref, wgu_ref, wd_ref, gf_ref, dgu_ref, dx2_ref, dgf_ref):
        i = pl.program_id(0)

        @pl.when(i == 0)
        def _():
            dgf_ref[...] = jnp.zeros_like(dgf_ref)

        dx3v = d_ref[...]
        db = dx3v.astype(BF16)
        dhf = jnp.zeros((TB, D), F32)
        for ch in range(D_FF // FF_CHUNK):
            c0 = ch * FF_CHUNK
            dact = _dot_nt(db, wd_ref[c0:c0 + FF_CHUNK, :])
            g = gu_ref[:, c0:c0 + FF_CHUNK]
            u = gu_ref[:, D_FF + c0:D_FF + c0 + FF_CHUNK]
            sg = _sig(g)
            dg = (dact * u * sg * (1.0 + g * (1.0 - sg))).astype(BF16)
            du = (dact * g * sg).astype(BF16)
            dgu_ref[:, c0:c0 + FF_CHUNK] = dg
            dgu_ref[:, D_FF + c0:D_FF + c0 + FF_CHUNK] = du
            dhf = dhf + _dot_nt(dg, wgu_ref[ch]) + _dot_nt(du, wgu_ref[2 + ch])
        r2, xh2 = _rms(x_ref[...])
        dx, dg_tok = _rms_bwd(dhf, xh2, r2, gf_ref[...])
        dx2_ref[...] = dx3v + dx
        dgf_ref[...] += jnp.sum(dg_tok, axis=0, keepdims=True)

    tok = lambda w: pl.BlockSpec((TB, w), lambda i: (i, 0))
    return pl.pallas_call(
        body, name="bwd_ffn", grid=(nb,),
        in_specs=[tok(D), tok(2 * D_FF), tok(D), _resident(w_gu), _resident(w_down), _resident(g_ffn)],
        out_specs=[tok(2 * D_FF), tok(D), _acc_spec((1, D))],
        out_shape=[_sds((T, 2 * D_FF), BF16), _sds((T, D), F32), _sds((1, D), F32)],
        compiler_params=_params(1),
    )(dx3, gu, x2, w_gu, w_down, g_ffn)


def _bwd_mid(dx2, x1, qm, km, vm, o, w_mo, w_mq, w_out, g_x, B, S, comm=None):
    T = B * S
    TB = min(512, S)
    nb = S // TB
    inv = 1.0 / math.sqrt(MEM_HD)

    def body(d_ref, x1_ref, qm_ref, km_ref, vm_ref, o_ref, wm_ref, wq_ref, wo_ref, g_ref,
             dx1_ref, dqm_ref, dco_ref, do_ref, dd_ref, dkm_ref, dvm_ref, dgx_ref):
        b = pl.program_id(0)
        j = pl.program_id(1)

        @pl.when((b == 0) & (j == 0))
        def _():
            dgx_ref[...] = jnp.zeros_like(dgx_ref)

        @pl.when(j == 0)
        def _():
            dkm_ref[...] = jnp.zeros_like(dkm_ref)
            dvm_ref[...] = jnp.zeros_like(dvm_ref)

        dx2v = d_ref[...]
        dom = _dot_nt(dx2v.astype(BF16), wm_ref[...]).astype(BF16)
        qmv = qm_ref[...]
        kmv = km_ref[...]
        vmv = vm_ref[...]
        ps = _mem_probs(qmv, kmv)
        for h in range(MEM_HEADS):
            hs = slice(h * MEM_HD, (h + 1) * MEM_HD)
            p = ps[h]
            dp = _dot_nt(dom[:, hs], vmv[:, hs])
            ds = (p * (dp - jnp.sum(p * dp, axis=-1, keepdims=True))).astype(BF16)
            dqm_ref[:, hs] = (_dot(ds, kmv[:, hs]) * inv).astype(BF16)
            dkm_ref[:, hs] += _dot_tn(ds, qmv[:, hs]) * inv
            dvm_ref[:, hs] += _dot_tn(p.astype(BF16), dom[:, hs])
        dhx = _dot_nt(dqm_ref[...], wq_ref[...])
        r1, xh1 = _rms(x1_ref[...])
        dx, dg_tok = _rms_bwd(dhx, xh1, r1, g_ref[...])
        dx1 = dx2v + dx
        dx1_ref[...] = dx1
        dgx_ref[...] += jnp.sum(dg_tok, axis=0, keepdims=True)
        d1b = dx1.astype(BF16)
        dco_ref[...] = _dot_nt(d1b, wo_ref[0:CONV_CH, :])
        do = _dot_nt(d1b, wo_ref[CONV_CH:D, :])
        dob = do.astype(BF16)
        do_ref[...] = dob
        dd_ref[...] = _dot_hi(dob.astype(F32) * o_ref[...], _head_sum(FOX_W))

    tok = lambda w: pl.BlockSpec((TB, w), lambda b, j: (b * nb + j, 0))
    memb = pl.BlockSpec((MEM_LEN, D), lambda b, j: (b, 0))
    outs = [(D, F32), (D, BF16), (CONV_CH, F32), (FOX_W, BF16), (FOX_W, F32)]
    return _call(
        body, comm, name="bwd_mid", grid=(B, nb),
        in_specs=[tok(D), tok(D), tok(D), memb, memb, tok(FOX_W), _resident(w_mo), _resident(w_mq), _resident(w_out),
                  _resident(g_x)],
        out_specs=[tok(w) for w, _ in outs] + [memb, memb, _acc_spec((1, D))],
        out_shape=[_sds((T, w), dt) for w, dt in outs] + [_sds((B * MEM_LEN, D), F32)] * 2 + [_sds((1, D), F32)],
        scratch_shapes=[],
        args=(dx2, x1, qm, km, vm, o, w_mo, w_mq, w_out, g_x))


def _mem_bwd(dkm, dvm, mem2, w_mkv, g_mem, B):
    def body(dk_ref, dv_ref, m_ref, w_ref, g_ref, dkv_ref, dg_ref):
        b = pl.program_id(0)

        @pl.when(b == 0)
        def _():
            dg_ref[...] = jnp.zeros_like(dg_ref)

        dk = dk_ref[...].astype(BF16)
        dv = dv_ref[...].astype(BF16)
        dkv_ref[:, 0:D] = dk
        dkv_ref[:, D:2 * D] = dv
        dmn = jnp.zeros((MEM_LEN, D), F32)
        for s in range(2):
            dmn = dmn + _dot_nt(dk[:, 512 * s:512 * (s + 1)], w_ref[s]) + _dot_nt(dv[:, 512 * s:512 * (s + 1)], w_ref[2 + s])
        _, xh = _rms(m_ref[...])
        dg_ref[...] += jnp.sum(dmn * xh, axis=0, keepdims=True)

    blk = pl.BlockSpec((MEM_LEN, D), lambda b: (b, 0))
    return pl.pallas_call(
        body, name="mem_bwd", grid=(B,),
        in_specs=[blk, blk, blk, _resident(w_mkv), _resident(g_mem)],
        out_specs=[pl.BlockSpec((MEM_LEN, 2 * D), lambda b: (b, 0)), _acc_spec((1, D))],
        out_shape=[_sds((B * MEM_LEN, 2 * D), BF16), _sds((1, D), F32)],
        compiler_params=_params(1),
    )(dkm, dvm, mem2, w_mkv, g_mem)


def _fox_bwd(q, k, v, do, bias, dd, ckT, B, S, comm=None):
    T = B * S
    TK = min(256, S)
    nk = S // TK
    scale = 1.0 / math.sqrt(HEAD_D)

    def body(q_ref, k_ref, v_ref, do_ref, bias_ref, dd_ref, ck_ref, dq_ref, dk_ref, dv_ref, dc_ref, dcq_ref,
             dq_acc, rs_acc, s_scr, dp_scr, s_odd, dp_odd, dk_acc, dv_acc, dc_acc):
        j = pl.program_id(2)

        @pl.when(j == 0)
        def _():
            dq_acc[...] = jnp.zeros_like(dq_acc)
            rs_acc[...] = jnp.zeros_like(rs_acc)

        dk_acc[...] = jnp.zeros_like(dk_acc)
        dv_acc[...] = jnp.zeros_like(dv_acc)
        dc_acc[...] = jnp.zeros_like(dc_acc)
        lane = lax.broadcasted_iota(jnp.int32, (TK, LANES), 1)
        lo = lane < HEAD_D
        ks = k_ref[...] * jnp.asarray(scale, BF16)
        v2 = v_ref[...]
        zero = jnp.zeros_like(ks)
        kh = (jnp.where(lo, ks, zero), jnp.where(lo, zero, ks))
        vh = (jnp.where(lo, v2, zero), jnp.where(lo, zero, v2))
        kstart = pl.multiple_of(j * TK, TK)
        ckh = tuple(ck_ref[0, 0, h:h + 1, pl.ds(kstart, TK)] for h in range(2))
        row = lax.broadcasted_iota(jnp.int32, (TK, TK), 0)
        col = lax.broadcasted_iota(jnp.int32, (TK, TK), 1)
        wide = lambda x: jnp.concatenate([x, x], axis=1) if TK == 2 * LANES else jnp.tile(x, (1, TK // LANES))

        def scores(i, s_buf, dp_buf):
            start = pl.multiple_of(i * TK, TK)
            qi = q_ref[pl.ds(start, TK), :]
            doi = do_ref[pl.ds(start, TK), :]
            for h in range(2):
                s_buf[h] = _dot_nt(qi, kh[h])
                dp_buf[h] = _dot_nt(doi, vh[h])

        def grads(i, s_buf, dp_buf, diagonal):
            start = pl.multiple_of(i * TK, TK)
            qi = q_ref[pl.ds(start, TK), :]
            doi = do_ref[pl.ds(start, TK), :]
            bias2 = bias_ref[pl.ds(start, TK), :]
            dd2 = dd_ref[pl.ds(start, TK), :]
            for h in range(2):
                hc = slice(h * HEAD_D, h * HEAD_D + 1)
                bias = jnp.broadcast_to(bias2[:, hc], (TK, LANES))
                ddh = jnp.broadcast_to(dd2[:, hc], (TK, LANES))
                p = jnp.exp((s_buf[h] - ckh[h]) + wide(bias))
                if diagonal:
                    p = jnp.where(col <= row, p, 0.0)
                ds = p * (dp_buf[h] - wide(ddh))
                dc_acc[h, 0:1, :] += jnp.sum(ds, axis=0, keepdims=True)
                rs_acc[h, pl.ds(start, TK), :] += jnp.sum(ds, axis=1, keepdims=True)
                pb = p.astype(BF16)
                dsb = ds.astype(BF16)
                dv_acc[h] += _dot_tn(pb, doi)
                dk_acc[h] += _dot_tn(dsb, qi)
                dq_acc[pl.ds(start, TK), :] += _dot(dsb, kh[h])

        n_off = nk - 1 - j
        block = lambda t: jnp.where(t < n_off, j + 1 + t, j)

        def two_blocks(tt, carry):
            t = 2 * tt
            scores(block(t + 1), s_odd, dp_odd)
            grads(block(t), s_scr, dp_scr, False)
            scores(block(t + 2), s_scr, dp_scr)
            grads(block(t + 1), s_odd, dp_odd, False)
            return carry

        scores(block(0), s_scr, dp_scr)
        lax.fori_loop(0, n_off // 2, two_blocks, 0)

        @pl.when(n_off % 2 == 0)
        def _():
            grads(j, s_scr, dp_scr, True)

        @pl.when(n_off % 2 == 1)
        def _():
            scores(j, s_odd, dp_odd)
            grads(nk - 1, s_scr, dp_scr, False)
            grads(j, s_odd, dp_odd, True)

        dk_ref[...] = (jnp.where(lo, dk_acc[0], dk_acc[1]) * scale).astype(BF16)
        dv_ref[...] = jnp.where(lo, dv_acc[0], dv_acc[1]).astype(BF16)
        sub = lax.broadcasted_iota(jnp.int32, (8, TK), 0)
        dca = dc_acc[0, 0:1, :]
        dcb = dc_acc[1, 0:1, :]
        dc_ref[0, 0] = jnp.where(sub == 0, -dca, jnp.where(sub == 1, -dcb, 0.0))

        @pl.when(j == nk - 1)
        def _():
            dq_ref[...] = dq_acc[...].astype(BF16)
            lo_s = lax.broadcasted_iota(jnp.int32, (S, LANES), 1) < HEAD_D
            dcq_ref[...] = jnp.where(lo_s, rs_acc[0], rs_acc[1])

    full = pl.BlockSpec((S, LANES), lambda b, p, j: (b, p))
    blk = pl.BlockSpec((TK, LANES), lambda b, p, j: (b * nk + j, p))
    return _call(
        body, comm, name="fox_bwd", grid=(B, N_PAIR, nk),
        in_specs=[full, blk, blk, full, full, full, pl.BlockSpec((1, 1, 8, S), lambda b, p, j: (b, p, 0, 0))],
        out_specs=[full, blk, blk, pl.BlockSpec((1, 1, 8, TK), lambda b, p, j: (b, p, 0, j)), full],
        out_shape=[_sds((T, FOX_W), BF16), _sds((T, FOX_W), BF16), _sds((T, FOX_W), BF16),
                   _sds((B, N_PAIR, 8, S), F32), _sds((T, FOX_W), F32)],
        scratch_shapes=[pltpu.VMEM((S, LANES), F32), pltpu.VMEM((2, S, 1), F32),
                        pltpu.VMEM((2, TK, TK), F32), pltpu.VMEM((2, TK, TK), F32),
                        pltpu.VMEM((2, TK, TK), F32), pltpu.VMEM((2, TK, TK), F32),
                        pltpu.VMEM((2, TK, LANES), F32), pltpu.VMEM((2, TK, LANES), F32), pltpu.VMEM((2, 8, TK), F32)],
        args=(q, k, v, do, bias, dd, ckT))


def _fgate_bwd(dc8, zf, B, S):
    T = B * S
    TB = min(512, S)
    nb = S // TB

    def body(dc_ref, zf_ref, dzf_ref, dbf_ref, carry):
        b = pl.program_id(0)
        j = pl.program_id(1)

        @pl.when((b == 0) & (j == 0))
        def _():
            dbf_ref[...] = jnp.zeros_like(dbf_ref)

        @pl.when(j == 0)
        def _():
            carry[...] = jnp.zeros_like(carry)

        dc = dc_ref[...]
        row = lax.broadcasted_iota(jnp.int32, (TB, TB), 0)
        col = lax.broadcasted_iota(jnp.int32, (TB, TB), 1)
        dlogf = _dot_hi((col >= row).astype(F32), dc) + carry[0:1, :]
        carry[0:1, :] = dlogf[0:1, :]
        lane = lax.broadcasted_iota(jnp.int32, dc.shape, 1)
        dzf = jnp.where(lane < 8, dlogf * _sig(-zf_ref[...]), 0.0)
        dzf_ref[...] = dzf.astype(BF16)
        dbf_ref[...] += jnp.sum(dzf, axis=0, keepdims=True)

    tok = pl.BlockSpec((TB, LANES), lambda b, j: (b * nb + (nb - 1 - j), 0))
    return pl.pallas_call(
        body, name="fgate_bwd", grid=(B, nb),
        in_specs=[tok, tok],
        out_specs=[tok, _acc_spec((1, LANES))],
        out_shape=[_sds((T, LANES), BF16), _sds((1, LANES), F32)],
        scratch_shapes=[pltpu.VMEM((8, LANES), F32)],
        compiler_params=_params(2),
    )(dc8, zf)


def _conv_bwd(dco, y, u, gt, cw, lng, lnb, B, S, comm=None):
    T = B * S
    CB = min(256, S)
    nb = S // CB
    hb = CB // CONV_HALO

    def body(dco_ref, y_ref, u_ref, gt_ref, up_ref, gp_ref, w_ref, lg_ref, lb_ref,
             du_ref, dgt_ref, dw_ref, vec_ref, acat, dycat, ash, dysh):
        b = pl.program_id(0)
        j = pl.program_id(1)
        jr = nb - 1 - j

        @pl.when((b == 0) & (j == 0))
        def _():
            dw_ref[...] = jnp.zeros_like(dw_ref)
            vec_ref[...] = jnp.zeros_like(vec_ref)

        @pl.when(j == 0)
        def _():
            dycat[CB:CB + CONV_HALO, :] = jnp.zeros((CONV_HALO, CONV_CH), F32)

        lg = lg_ref[...]
        rs, n, l = _layernorm_silu(y_ref[...], lg, lb_ref[...])
        sg = _sig(l)
        dl = dco_ref[...] * (sg * (1.0 + l * (1.0 - sg)))
        dn = dl * lg
        dy = rs * (dn - jnp.mean(dn, axis=-1, keepdims=True) - n * jnp.mean(dn * n, axis=-1, keepdims=True))
        vec_ref[0:1, :] += jnp.sum(dy, axis=0, keepdims=True)
        vec_ref[1:2, :] += jnp.sum(dl * n, axis=0, keepdims=True)
        vec_ref[2:3, :] += jnp.sum(dl, axis=0, keepdims=True)
        dycat[0:CB, :] = dy
        uv = u_ref[...]
        sgt = _sig(gt_ref[...])
        acat[0:CONV_HALO, :] = jnp.where(jr > 0, up_ref[...] * _sig(gp_ref[...]), 0.0)
        acat[CONV_HALO:CONV_HALO + CB, :] = uv * sgt
        _shifted_copies(acat, ash, CB + CONV_HALO - SUB)
        _shifted_copies(dycat, dysh, CB + CONV_HALO - SUB)
        da = jnp.zeros((CB, CONV_CH), F32)
        for k in range(CONV_K):
            da = da + w_ref[k:k + 1, :] * _tap(dycat, dysh, CONV_K - 1 - k, CB)
            dw_ref[k:k + 1, :] += jnp.sum(dy * _tap(acat, ash, CONV_HALO - (CONV_K - 1) + k, CB), axis=0, keepdims=True)
        dycat[CB:CB + CONV_HALO, :] = dycat[0:CONV_HALO, :]
        du_ref[...] = (da * sgt).astype(BF16)
        dgt_ref[...] = (da * uv * sgt * (1.0 - sgt)).astype(BF16)

    tok = lambda w: pl.BlockSpec((CB, w), lambda b, j: (b * nb + (nb - 1 - j), 0))
    prev = pl.BlockSpec((CONV_HALO, CONV_CH), lambda b, j: (jnp.maximum((b * nb + (nb - 1 - j)) * hb - 1, 0), 0))
    return _call(
        body, comm, name="conv_bwd", grid=(B, nb),
        in_specs=[tok(CONV_CH), tok(CONV_CH), tok(CONV_CH), tok(CONV_CH), prev, prev, _resident(cw), _resident(lng),
                  _resident(lnb)],
        out_specs=[tok(CONV_CH), tok(CONV_CH), _acc_spec((CONV_HALO, CONV_CH)), _acc_spec((8, CONV_CH))],
        out_shape=[_sds((T, CONV_CH), BF16), _sds((T, CONV_CH), BF16), _sds((CONV_HALO, CONV_CH), F32),
                   _sds((8, CONV_CH), F32)],
        scratch_shapes=[pltpu.VMEM((CONV_HALO + CB, CONV_CH), F32), pltpu.VMEM((CB + CONV_HALO, CONV_CH), F32),
                        pltpu.VMEM((SUB, CB + CONV_HALO - SUB, CONV_CH), F32),
                        pltpu.VMEM((SUB, CB + CONV_HALO - SUB, CONV_CH), F32)],
        args=(dco, y, u, gt, u, gt, cw, lng, lnb))


def _bwd_in(dz, w_in, x2, dx1, g_mix, T, comm=None):
    TB = min(512, T)
    nb = T // TB

    def body(dz_ref, w_ref, x_ref, d1_ref, g_ref, gx_ref, dg_ref):
        i = pl.program_id(0)

        @pl.when(i == 0)
        def _():
            dg_ref[...] = jnp.zeros_like(dg_ref)

        dh = _dot_nt(dz_ref[...], w_ref[...])
        r0, xh0 = _rms(x_ref[...])
        dx, dg_tok = _rms_bwd(dh, xh0, r0, g_ref[...])
        gx_ref[...] = d1_ref[...] + dx
        dg_ref[...] += jnp.sum(dg_tok, axis=0, keepdims=True)

    tok = lambda w: pl.BlockSpec((TB, w), lambda i: (i, 0))
    return _call(
        body, comm, name="bwd_in", grid=(nb,),
        in_specs=[tok(D_IN_PAD), _resident(w_in), tok(D), tok(D), _resident(g_mix)],
        out_specs=[tok(D), _acc_spec((1, D))],
        out_shape=[_sds((T, D), F32), _sds((1, D), F32)],
        scratch_shapes=[],
        args=(dz, w_in, x2, dx1, g_mix))


def _dw(a, b, name, tn, slabs=False):
    T, K = a.shape
    N = b.shape[1]
    tk = K if K <= 1024 else K // 2
    tt = min(512, T)
    nt = T // tt

    def body(a_ref, b_ref, o_ref, acc):
        t = pl.program_id(2)

        @pl.when(t == 0)
        def _():
            acc[...] = jnp.zeros_like(acc)

        acc[...] += _dot_tn(a_ref[...].astype(BF16), b_ref[...].astype(BF16))

        @pl.when(t == nt - 1)
        def _():
            o_ref[...] = acc[...]

    return pl.pallas_call(
        body, name=name, grid=(K // tk, N // tn, nt),
        in_specs=[pl.BlockSpec((tt, tk), lambda i, j, t: (t, i)), pl.BlockSpec((tt, tn), lambda i, j, t: (t, j))],
        out_specs=(pl.BlockSpec((None, tk, tn), lambda i, j, t: (j, i, 0)) if slabs
                   else pl.BlockSpec((tk, tn), lambda i, j, t: (i, j))),
        out_shape=_sds((N // tn, K, tn) if slabs else (K, N), F32),
        scratch_shapes=[pltpu.VMEM((tk, tn), F32)],
        compiler_params=_params(3),
    )(a, b)


def _pos():
    return lax.axis_index("x"), lax.axis_index("y"), lax.axis_index("c")


def _remote(src, dst, ssem, rsem, to):
    return pltpu.make_async_remote_copy(src_ref=src, dst_ref=dst, send_sem=ssem, recv_sem=rsem, device_id=to,
                                        device_id_type=MESH)


def _half(ref_rows, c):
    H = ref_rows // 2
    return pl.ds(pl.multiple_of(c * H, 16), H)


class _Comm:
    def __init__(self, ins, out_shapes, sems, start, finish):
        self.ins, self.out_shapes, self.sems, self.start, self.finish = list(ins), list(out_shapes), list(sems), start, finish


def _ag_comm(shards):
    n = len(shards)

    def parts(ins, outs, sems):
        send_sems, recv_sems, local_sems = sems
        x, y, c = _pos()
        me, sib = (x, y, c), (x, y, 1 - c)
        chips = [(1 - x, y), (x, 1 - y), (1 - x, 1 - y)]

        def rows(w, px, py, pc):
            return outs[w].at[2 * px + py, _half(shards[w].shape[0], pc), :]

        def copy(w, k, block, to, src=None):
            return _remote(rows(w, *block) if src is None else src, rows(w, *block), send_sems.at[w, k],
                           recv_sems.at[w, k], to)

        mine, first = [], []
        for w in range(n):
            src = ins[w].at[_half(shards[w].shape[0], c), :]
            mine.append(pltpu.make_async_copy(src, rows(w, *me), local_sems.at[w]))
            first += [copy(w, 0, me, sib, src=src)] + [copy(w, 1 + j, me, (*chip, c), src=src) for j, chip in enumerate(chips)]
        return c, me, sib, chips, copy, mine, first

    def start(ins, outs, sems):
        _, _, _, _, _, mine, first = parts(ins, outs, sems)
        for cp in mine + first:
            cp.start()

    def finish(ins, outs, sems):
        c, me, sib, chips, copy, mine, first = parts(ins, outs, sems)
        passed = []
        for w in range(n):
            for j, chip in enumerate(chips):
                copy(w, 1 + j, (*chip, c), me).wait_recv()
                passed.append(copy(w, 4 + j, (*chip, c), sib))
                passed[-1].start()
        for w in range(n):
            copy(w, 0, sib, me).wait_recv()
            for j, chip in enumerate(chips):
                copy(w, 4 + j, (*chip, 1 - c), me).wait_recv()
        for cp in first + passed:
            cp.wait_send()
        for cp in mine:
            cp.wait()

    D7 = pltpu.SemaphoreType.DMA((n, 7))
    return _Comm(shards, [_sds((4,) + s.shape, s.dtype) for s in shards], [D7, D7, pltpu.SemaphoreType.DMA((n,))],
                 start, finish)


def _sibling_comm(gs):
    n = len(gs)

    def copies(ins, outs, sems):
        send_sems, recv_sems = sems
        x, y, c = _pos()
        return [_remote(ins[w].at[s, _half(gs[w].shape[1], 1 - c), :], outs[w].at[s], send_sems.at[w, s],
                        recv_sems.at[w, s], (x, y, 1 - c)) for w in range(n) for s in range(4)]

    def start(ins, outs, sems):
        for cp in copies(ins, outs, sems):
            cp.start()

    def finish(ins, outs, sems):
        for cp in copies(ins, outs, sems):
            cp.wait()

    D4 = pltpu.SemaphoreType.DMA((n, 4))
    return _Comm(gs, [_sds((4, g.shape[1] // 2, g.shape[2]), F32) for g in gs], [D4, D4], start, finish)


def _ici_comm(pbs):
    n = len(pbs)

    def copies(ins, outs, sems):
        send_sems, recv_sems = sems
        x, y, c = _pos()
        return [_remote(ins[w].at[2 * tx + ty], outs[w].at[j], send_sems.at[w, j], recv_sems.at[w, j], (tx, ty, c))
                for w in range(n) for j, (tx, ty) in enumerate([(1 - x, y), (x, 1 - y), (1 - x, 1 - y)])]

    def start(ins, outs, sems):
        for cp in copies(ins, outs, sems):
            cp.start()

    def finish(ins, outs, sems):
        for cp in copies(ins, outs, sems):
            cp.wait()

    D3 = pltpu.SemaphoreType.DMA((n, 3))
    return _Comm(pbs, [_sds((3,) + p.shape[1:], BF16) for p in pbs], [D3, D3], start, finish)


def _join(*comms):
    counts = [(len(c.ins), len(c.out_shapes), len(c.sems)) for c in comms]

    def each(which):
        def run(ins, outs, sems):
            i = o = k = 0
            for c, (ni, no, nk) in zip(comms, counts):
                getattr(c, which)(ins[i:i + ni], outs[o:o + no], sems[k:k + nk])
                i, o, k = i + ni, o + no, k + nk
        return run

    return _Comm(sum((c.ins for c in comms), []), sum((c.out_shapes for c in comms), []),
                 sum((c.sems for c in comms), []), each("start"), each("finish"))


def _run_comm(comm, name):
    ni, no = len(comm.ins), len(comm.out_shapes)

    def body(*refs):
        ins, outs, sems = refs[:ni], refs[ni:ni + no], refs[ni + no:]
        comm.start(ins, outs, sems)
        comm.finish(ins, outs, sems)

    return pl.pallas_call(body, name=name, out_shape=comm.out_shapes, in_specs=[ANY] * ni, out_specs=[ANY] * no,
                          scratch_shapes=comm.sems)(*comm.ins)


def _call(body, comm, *, name, grid, in_specs, out_specs, out_shape, scratch_shapes, args):
    n_grid = len(grid)
    if comm is None:
        res = pl.pallas_call(body, name=name, grid=grid, in_specs=in_specs, out_specs=out_specs, out_shape=out_shape,
                             scratch_shapes=scratch_shapes, compiler_params=_params(n_grid))(*args)
        return list(res), []
    n_in, n_out, n_scr = len(in_specs), len(out_specs), len(scratch_shapes)
    ni, no = len(comm.ins), len(comm.out_shapes)

    def carried(*refs):
        ins, refs = refs[:n_in], refs[n_in:]
        cins, refs = refs[:ni], refs[ni:]
        outs, refs = refs[:n_out], refs[n_out:]
        couts, refs = refs[:no], refs[no:]
        scr, csems = refs[:n_scr], refs[n_scr:]
        ids = [pl.program_id(ax) for ax in range(n_grid)]
        first = functools.reduce(jnp.logical_and, [i == 0 for i in ids])
        last = functools.reduce(jnp.logical_and, [i == g - 1 for i, g in zip(ids, grid)])

        @pl.when(first)
        def _():
            comm.start(cins, couts, csems)

        body(*ins, *outs, *scr)

        @pl.when(last)
        def _():
            comm.finish(cins, couts, csems)

    res = pl.pallas_call(
        carried, name=name, grid=grid, in_specs=list(in_specs) + [ANY] * ni, out_specs=list(out_specs) + [ANY] * no,
        out_shape=list(out_shape) + comm.out_shapes, scratch_shapes=list(scratch_shapes) + comm.sems,
        compiler_params=_params(n_grid))(*args, *comm.ins)
    return list(res[:n_out]), list(res[n_out:])


def _sibling_share(gs):
    n = len(gs)

    def body(*refs):
        outs = refs[n:2 * n]
        send_sems, recv_sems = refs[2 * n:]
        x, y, c = _pos()
        cps = []
        for w in range(n):
            mine = outs[w].at[_half(gs[w].shape[0], c), :]
            cps.append(_remote(mine, mine, send_sems.at[w], recv_sems.at[w], (x, y, 1 - c)))
            cps[-1].start()
        for cp in cps:
            cp.wait()

    return pl.pallas_call(
        body, name="rs_share", out_shape=[_sds(g.shape, F32) for g in gs],
        in_specs=[ANY] * n, out_specs=[ANY] * n, input_output_aliases={w: w for w in range(n)},
        scratch_shapes=[pltpu.SemaphoreType.DMA((n,)), pltpu.SemaphoreType.DMA((n,))],
    )(*gs)


def _small_allreduce(v, name):
    P = v.shape[0]
    vm = pl.BlockSpec(memory_space=pltpu.VMEM)

    def body(v_ref, o_ref, gath, send_sems, recv_sems):
        x, y, c = _pos()
        me = 4 * x + 2 * y + c
        gath[me] = v_ref[...]
        cps = []
        for r in range(1, 8):
            tx = (1 - x) if r & 4 else x
            ty = (1 - y) if r & 2 else y
            tc = (1 - c) if r & 1 else c
            cps.append(_remote(v_ref, gath.at[me], send_sems.at[r - 1], recv_sems.at[r - 1], (tx, ty, tc)))
            cps[-1].start()
        for cp in cps:
            cp.wait()
        acc = gath[0]
        for d in range(1, 8):
            acc = acc + gath[d]
        o_ref[...] = acc

    return pl.pallas_call(
        body, name=name, out_shape=_sds((P, LANES), F32), in_specs=[vm], out_specs=vm,
        scratch_shapes=[pltpu.VMEM((8, P, LANES), F32), pltpu.SemaphoreType.DMA((7,)), pltpu.SemaphoreType.DMA((7,))],
    )(v)


def _chip_sum(g, rcv, pos, name):
    _, R, C = g.shape
    H = R // 2

    def body(pos_ref, g_ref, r_ref, o_ref):
        o_ref[...] = (g_ref[...] + r_ref[...]).astype(BF16)

    return pl.pallas_call(
        body, name=name, out_shape=_sds((4, H, C), BF16),
        grid_spec=pltpu.PrefetchScalarGridSpec(
            num_scalar_prefetch=1, grid=(4,),
            in_specs=[pl.BlockSpec((1, H, C), lambda s, pos: (s, pos[0], 0)),
                      pl.BlockSpec((1, H, C), lambda s, pos: (s, 0, 0))],
            out_specs=pl.BlockSpec((1, H, C), lambda s, pos: (s, 0, 0))),
        compiler_params=_params(1),
    )(pos, g, rcv)


def _final_sum(g, rcv, rc, pos, name):
    _, R, C = g.shape
    Q = R // 4

    def body(pos_ref, g_ref, r_ref, rc_ref, o_ref):
        acc = g_ref[0] + r_ref[0]
        for j in range(3):
            acc = acc + rc_ref[j].astype(F32)
        o_ref[...] = acc

    return pl.pallas_call(
        body, name=name, out_shape=_sds((R, C), F32),
        grid_spec=pltpu.PrefetchScalarGridSpec(
            num_scalar_prefetch=1, grid=(2,),
            in_specs=[pl.BlockSpec((1, Q, C), lambda i, pos: (pos[1], pos[0] * 2 + i, 0)),
                      pl.BlockSpec((1, Q, C), lambda i, pos: (pos[1], i, 0)),
                      pl.BlockSpec((3, Q, C), lambda i, pos: (0, i, 0))],
            out_specs=pl.BlockSpec((Q, C), lambda i, pos: (pos[0] * 2 + i, 0))),
        compiler_params=_params(1),
    )(pos, g, rcv, rc)


def _adamw_math(w, g, m, v):
    m = ADAM_B1 * m + (1.0 - ADAM_B1) * g
    v = ADAM_B2 * v + (1.0 - ADAM_B2) * (g * g)
    m_hat = m / (1.0 - ADAM_B1 ** ADAM_STEP)
    v_hat = v / (1.0 - ADAM_B2 ** ADAM_STEP)
    delta = -ADAM_LR * (m_hat / (jnp.sqrt(v_hat) + ADAM_EPS) + ADAM_WD * w)
    return delta, m, v


def _adamw(w, g, m, v, name, rb):
    R, C = w.shape

    def body(w_ref, g_ref, m_ref, v_ref, d_ref, nm_ref, nv_ref):
        d, nm, nv = _adamw_math(w_ref[...], g_ref[...], m_ref[...], v_ref[...])
        d_ref[...] = d
        nm_ref[...] = nm
        nv_ref[...] = nv

    blk = pl.BlockSpec((rb, C), lambda i: (i, 0))
    return pl.pallas_call(
        body, name=name, grid=(R // rb,), in_specs=[blk] * 4, out_specs=[blk] * 3,
        out_shape=[_sds((R, C), F32)] * 3, compiler_params=_params(1),
    )(w, g, m, v)


SMALL = (("g_mix", 8), ("b_f", 8), ("conv_w", None), ("conv_b", 8), ("ln_g", 8), ("ln_b", 8), ("g_x", 8), ("g_mem", 8),
         ("g_ffn", 8), ("g_final", 8), ("loss", 8))


def _pack_small(parts, conv_rows):
    rows = []
    for name, n in SMALL:
        if name not in parts:
            continue
        n = conv_rows if n is None else n
        flat = parts[name].reshape(-1).astype(F32)
        flat = jnp.pad(flat, (0, n * LANES - flat.shape[0]))
        rows.append(flat.reshape(n, LANES))
    return jnp.concatenate(rows, axis=0)


def _unpack_small(p, shapes, conv_rows):
    out, off = {}, 0
    for name, n in SMALL:
        if name not in shapes:
            continue
        n = conv_rows if n is None else n
        size = math.prod(shapes[name])
        out[name] = p[off:off + n].reshape(-1)[:size].reshape(shapes[name])
        off += n
    return out


def kernel(x, mem, g_mix, w_in, b_f, conv_w, conv_b, ln_g, ln_b, w_out, g_x, g_mem, w_mq, w_mkv, w_mo, g_ffn, w_gu, w_down, g_final, loss_target, m_g_mix, m_w_in, m_b_f, m_conv_w, m_conv_b, m_ln_g, m_ln_b, m_w_out, m_g_x, m_g_mem, m_w_mq, m_w_mkv, m_w_mo, m_g_ffn, m_w_gu, m_w_down, m_g_final, v_g_mix, v_w_in, v_b_f, v_conv_w, v_conv_b, v_ln_g, v_ln_b, v_w_out, v_g_x, v_g_mem, v_w_mq, v_w_mkv, v_w_mo, v_g_ffn, v_w_gu, v_w_down, v_g_final):
    names = ["g_mix", "w_in", "b_f", "conv_w", "conv_b", "ln_g", "ln_b", "w_out", "g_x", "g_mem", "w_mq", "w_mkv",
             "w_mo", "g_ffn", "w_gu", "w_down", "g_final"]
    W = dict(zip(names, [g_mix, w_in, b_f, conv_w, conv_b, ln_g, ln_b, w_out, g_x, g_mem, w_mq, w_mkv, w_mo, g_ffn,
                         w_gu, w_down, g_final]))
    M = dict(zip(names, [m_g_mix, m_w_in, m_b_f, m_conv_w, m_conv_b, m_ln_g, m_ln_b, m_w_out, m_g_x, m_g_mem, m_w_mq,
                         m_w_mkv, m_w_mo, m_g_ffn, m_w_gu, m_w_down, m_g_final]))
    V = dict(zip(names, [v_g_mix, v_w_in, v_b_f, v_conv_w, v_conv_b, v_ln_g, v_ln_b, v_w_out, v_g_x, v_g_mem, v_w_mq,
                         v_w_mkv, v_w_mo, v_g_ffn, v_w_gu, v_w_down, v_g_final]))
    big_names = [n for n, _, _, _ in BIG]
    B, S, _ = x.shape
    T = B * S
    mx, my, mc = _pos()
    chip = 2 * mx + my
    pos = jnp.stack([mc, chip]).astype(jnp.int32)

    shard2d = lambda a: a.reshape(a.shape[-2], a.shape[-1])
    shard_bf = {n: shard2d(W[n]).astype(BF16) for n in big_names}
    ag_mid = ["w_mkv", "w_out", "w_mq", "w_mo"]
    ag_ffn = ["w_gu", "w_down"]
    slab = {"w_in": _run_comm(_ag_comm([shard_bf["w_in"]]), "ag_w_in")[0]}
    w_in_f = jnp.pad(jnp.transpose(slab["w_in"], (1, 0, 2)).reshape(D, D_IN), ((0, 0), (0, D_IN_PAD - D_IN)))
    cw_mine = jnp.pad(shard2d(conv_w), ((0, 1), (0, 0)))
    cw_slot = lax.dynamic_update_slice(jnp.zeros((CONV_HALO, CONV_CH), F32), cw_mine, (0, chip * LANES))
    cw = _small_allreduce(cw_slot.reshape(CONV_HALO * 4, LANES) * 0.5, "gather_conv_w").reshape(CONV_HALO, CONV_CH)

    row = lambda a: a.reshape(1, -1)
    bf_pad = jnp.pad(row(b_f), ((0, 0), (0, LANES - 8)))
    x2d = x.reshape(T, D)
    mem2d = mem.reshape(B * MEM_LEN, D)
    tgt = loss_target.reshape(T, D)

    h, u, gt, q, k, v, zf, c, cq = _fwd_in(x2d, row(g_mix), w_in_f, bf_pad, B, S)
    ckT = jnp.transpose(c.reshape(B, S, LANES)[:, :, :8], (0, 2, 1)).reshape(B, N_PAIR, 2, S)
    ckT = jnp.pad(ckT, ((0, 0), (0, 0), (0, 6), (0, 0)))
    (y, co), got = _conv_fwd(u, gt, cw, row(conv_b), row(ln_g), row(ln_b), B, S, comm=_ag_comm([shard_bf[n] for n in ag_mid]))
    slab.update(zip(ag_mid, got))
    (o, fox_bias), got = _fox_fwd(q, k, v, cq, ckT, B, S, comm=_ag_comm([shard_bf[n] for n in ag_ffn]))
    slab.update(zip(ag_ffn, got))
    full = {n: slab[n] if by_col else slab[n].reshape(4 * r, c) for n, r, c, by_col in BIG}
    mn, km, vm = _mem_kv(mem2d, row(g_mem), full["w_mkv"], B)
    x1, hx, qm, om, x2, cat = _fwd_mid(x2d, co, o, km, vm, full["w_out"], full["w_mq"], full["w_mo"], row(g_x), B, S)
    hf, gu, act, dx3, loss_p, dg_final = _fwd_ffn(x2, tgt, full["w_gu"], full["w_down"], row(g_ffn), row(g_final), T)

    pos_sum = lambda gs, rcvs, ns: [_chip_sum(g, r, pos, "rs_chip_sum_" + n) for g, r, n in zip(gs, rcvs, ns)]
    fin_sum = lambda gs, rcvs, rcs, ns: [_final_sum(g, r, q3, pos, "rs_final_sum_" + n)
                                         for g, r, q3, n in zip(gs, rcvs, rcs, ns)]
    RH = {}
    dgu, dx2, dg_ffn = _bwd_ffn(dx3, gu, x2, full["w_gu"], full["w_down"], row(g_ffn), T)
    g_ffn_w = [_dw(hf, dgu, "dw_gu", FF_CHUNK, slabs=True), _dw(act, dx3, "dw_down", 512).reshape(4, D_FF // 4, D)]
    (dx1, dqm, dco, do, dd, dkm, dvm, dg_x), rcv_ffn = _bwd_mid(dx2, x1, qm, km, vm, o, full["w_mo"], full["w_mq"],
                                                                full["w_out"], row(g_x), B, S, comm=_sibling_comm(g_ffn_w))
    pb_ffn = pos_sum(g_ffn_w, rcv_ffn, ag_ffn)
    dkv, dg_mem = _mem_bwd(dkm, dvm, mem2d, full["w_mkv"], row(g_mem), B)
    g_mid_w = [_dw(mn, dkv, "dw_mkv", 512, slabs=True), _dw(cat, dx1, "dw_out", 512).reshape(4, 256, D),
               _dw(hx, dqm, "dw_mq", 512).reshape(4, 256, D), _dw(om, dx2, "dw_mo", 512).reshape(4, 256, D)]
    (dq, dk, dv, dc, dcq), got = _fox_bwd(q, k, v, do, fox_bias, dd, ckT, B, S,
                                          comm=_join(_ici_comm(pb_ffn), _sibling_comm(g_mid_w)))
    rc_ffn, rcv_mid = got[:len(pb_ffn)], got[len(pb_ffn):]
    RH.update(zip(ag_ffn, fin_sum(g_ffn_w, rcv_ffn, rc_ffn, ag_ffn)))
    pb_mid = pos_sum(g_mid_w, rcv_mid, ag_mid)
    dc8 = jnp.transpose(dc[:, :, :2, :].reshape(B, 8, S), (0, 2, 1)).reshape(T, 8)
    dc8 = dc8 + dcq.reshape(T, 8, HEAD_D)[:, :, 0]
    dzf, dbf = _fgate_bwd(jnp.pad(dc8, ((0, 0), (0, LANES - 8))), zf, B, S)
    (du, dgt, dcw, dvec), rc_mid = _conv_bwd(dco, y, u, gt, cw, row(ln_g), row(ln_b), B, S, comm=_ici_comm(pb_mid))
    RH.update(zip(ag_mid, fin_sum(g_mid_w, rcv_mid, rc_mid, ag_mid)))
    dz = jnp.concatenate([du, dgt, dq, dk, dv, dzf], axis=1)
    dw_in = _dw(h, dz, "dw_in", 384)[:, :D_IN]
    g_in_w = [jnp.transpose(dw_in.reshape(D, 4, D_IN // 4), (1, 0, 2))]
    rcv_in = _run_comm(_sibling_comm(g_in_w), "rs_sibling_in")
    (grad_x, dg_mix), rc_in = _bwd_in(dz, w_in_f, x2d, dx1, row(g_mix), T,
                                      comm=_ici_comm(pos_sum(g_in_w, rcv_in, ["w_in"])))
    RH.update(zip(["w_in"], fin_sum(g_in_w, rcv_in, rc_in, ["w_in"])))
    G = dict(zip(big_names, _sibling_share([RH[n] for n in big_names])))
    DL, NM, NV = {}, {}, {}
    for n in big_names:
        DL[n], NM[n], NV[n] = _adamw(shard2d(W[n]), G[n], shard2d(M[n]), shard2d(V[n]), "adamw_" + n, G[n].shape[0] // 2)

    small_g = {"g_mix": dg_mix, "b_f": dbf[:, :8], "conv_w": dcw, "conv_b": dvec[0], "ln_g": dvec[1], "ln_b": dvec[2],
               "g_x": dg_x, "g_mem": dg_mem, "g_ffn": dg_ffn, "g_final": dg_final, "loss": loss_p[:, :1]}
    sg = _small_allreduce(_pack_small(small_g, CONV_HALO * 4), "allreduce_small")
    shapes = {n: W[n].shape for n in names if n not in big_names}
    shapes["conv_w"] = (CONV_HALO, CONV_CH)
    shapes["loss"] = (1,)
    sgrads = _unpack_small(sg, shapes, CONV_HALO * 4)
    loss = sgrads.pop("loss")[0]
    sgrads["conv_w"] = lax.dynamic_slice(sgrads["conv_w"], (0, chip * LANES), (CONV_K, LANES)).reshape(W["conv_w"].shape)
    spack = lambda d: _pack_small({n: d[n] for n in sgrads}, CONV_HALO)
    sd, snm, snv = _adamw(spack(W), spack(sgrads), spack(M), spack(V), "adamw_small", 8)
    sshapes = {n: W[n].shape for n in sgrads}
    SD, SNM, SNV = (_unpack_small(a, sshapes, CONV_HALO) for a in (sd, snm, snv))

    def collect(bigs, smalls):
        return [bigs[n].reshape(W[n].shape) if n in big_names else smalls[n] for n in names]

    return (loss, grad_x.reshape(x.shape), *collect(G, sgrads), *collect(DL, SD), *collect(NM, SNM), *collect(NV, SNV))
```

```python
import functools
import math

import jax
import jax.numpy as jnp
from jax import lax
from jax.experimental import pallas as pl
from jax.experimental.pallas import tpu as pltpu

F32, BF16 = jnp.float32, jnp.bfloat16
HIGHEST = lax.Precision.HIGHEST
MESH = pl.DeviceIdType.MESH

D = 1024
CONV_CH = 512
CONV_K = 31
CONV_HALO = 32
FOX_W = 512
HEAD_D = 64
N_PAIR = 4
MEM_LEN = 256
MEM_HEADS = 4
MEM_HD = 256
D_FF = 2816
FF_CHUNK = 1408
D_IN = 2568
D_IN_PAD = 2688
OFF_F = 2560
EPS = 1e-6
LANES = 128

ADAM_LR, ADAM_B1, ADAM_B2, ADAM_EPS, ADAM_WD, ADAM_STEP = 0.001, 0.9, 0.999, 1e-08, 0.01, 10

VMEM_LIMIT = 60 * 1024 * 1024

BIG = (("w_out", 256, 1024, False), ("w_mq", 256, 1024, False), ("w_mkv", 1024, 512, True),
       ("w_mo", 256, 1024, False), ("w_gu", 1024, 1408, True), ("w_down", 704, 1024, False),
       ("w_in", 1024, 642, True))

ANY = pl.BlockSpec(memory_space=pl.ANY)


def _sig(x):
    return 1.0 / (1.0 + jnp.exp(-x))


def _dot(a, b):
    return jnp.dot(a, b, preferred_element_type=F32)


def _dot_nt(a, b):
    return lax.dot_general(a, b, (((1,), (1,)), ((), ())), preferred_element_type=F32)


def _dot_tn(a, b):
    return lax.dot_general(a, b, (((0,), (0,)), ((), ())), preferred_element_type=F32)


def _dot_hi(a, b):
    return jnp.dot(a, b, precision=HIGHEST, preferred_element_type=F32)


def _resident(a):
    nd = a.ndim
    return pl.BlockSpec(a.shape, lambda *_: (0,) * nd, pipeline_mode=pl.Buffered(1))


def _acc_spec(shape):
    nd = len(shape)
    return pl.BlockSpec(shape, lambda *_: (0,) * nd)


def _params(n_grid):
    return pltpu.CompilerParams(dimension_semantics=("arbitrary",) * n_grid, vmem_limit_bytes=VMEM_LIMIT)


def _sds(shape, dtype):
    return jax.ShapeDtypeStruct(shape, dtype)


def _rms(x):
    r = lax.rsqrt(jnp.mean(x * x, axis=-1, keepdims=True) + EPS)
    return r, x * r


def _rms_bwd(dy, xh, r, g):
    dxh = dy * g
    dx = r * (dxh - xh * jnp.mean(dxh * xh, axis=-1, keepdims=True))
    return dx, dy * xh


def _head_expand(rows, cols):
    hd = lax.broadcasted_iota(jnp.int32, (rows, cols), 1) // HEAD_D
    hr = lax.broadcasted_iota(jnp.int32, (rows, cols), 0)
    return (hd == hr).astype(F32)


def _fwd_in(x2, g_mix, w_in, bf_pad, B, S, comm=None):
    T = B * S
    TB = min(512, S)
    nb = S // TB

    def body(x_ref, g_ref, w_ref, bf_ref, h_ref, u_ref, gt_ref, q_ref, k_ref, v_ref, zf_ref, c_ref, cq_ref,
             carry):
        j = pl.program_id(1)

        @pl.when(j == 0)
        def _():
            carry[...] = jnp.zeros_like(carry)

        _, xh = _rms(x_ref[...])
        h = (xh * g_ref[...]).astype(BF16)
        h_ref[...] = h
        u_ref[...] = _dot(h, w_ref[:, 0:512])
        gt_ref[...] = _dot(h, w_ref[:, 512:1024])
        q_ref[...] = _dot(h, w_ref[:, 1024:1536]).astype(BF16)
        k_ref[...] = _dot(h, w_ref[:, 1536:2048]).astype(BF16)
        v_ref[...] = _dot(h, w_ref[:, 2048:2560]).astype(BF16)
        zf = _dot(h, w_ref[:, OFF_F:D_IN_PAD]) + bf_ref[...]
        zf_ref[...] = zf
        lane = lax.broadcasted_iota(jnp.int32, zf.shape, 1)
        logf = jnp.where(lane < 8, jnp.minimum(zf, 0.0) - jnp.log(1.0 + jnp.exp(-jnp.abs(zf))), 0.0)
        row = lax.broadcasted_iota(jnp.int32, (TB, TB), 0)
        col = lax.broadcasted_iota(jnp.int32, (TB, TB), 1)
        c = _dot_hi((row >= col).astype(F32), logf) + carry[0:1, :]
        carry[0:1, :] = c[TB - 1:TB, :]
        c_ref[...] = c
        cq_ref[...] = _dot_hi(c, _head_expand(LANES, FOX_W))

    tok = lambda w: pl.BlockSpec((TB, w), lambda b, j: (b * nb + j, 0))
    outs = [(D, BF16), (512, F32), (512, F32), (512, BF16), (512, BF16), (512, BF16), (LANES, F32),
            (LANES, F32), (FOX_W, F32)]
    return _call(
        body, comm, name="fwd_in", grid=(B, nb),
        in_specs=[tok(D), _resident(g_mix), _resident(w_in), _resident(bf_pad)],
        out_specs=[tok(w) for w, _ in outs],
        out_shape=[_sds((T, w), dt) for w, dt in outs],
        scratch_shapes=[pltpu.VMEM((8, LANES), F32)],
        args=(x2, g_mix, w_in, bf_pad))


def _head_collect(rows, cols):
    hr = lax.broadcasted_iota(jnp.int32, (rows, cols), 0) // HEAD_D
    hc = lax.broadcasted_iota(jnp.int32, (rows, cols), 1)
    return (hr == hc).astype(F32)


def _layernorm_silu(y, lg, lb):
    mu = jnp.mean(y, axis=-1, keepdims=True)
    yc = y - mu
    rs = lax.rsqrt(jnp.mean(yc * yc, axis=-1, keepdims=True) + EPS)
    n = yc * rs
    l = n * lg + lb
    return rs, n, l


SUB = 8


def _shifted_copies(cat, sh, rows):
    for r in range(1, SUB):
        sh[r, 0:rows, :] = cat[r:r + rows, :]


def _tap(cat, sh, off, rows):
    r = off % SUB
    return cat[off:off + rows, :] if r == 0 else sh[r, off - r:off - r + rows, :]


def _conv_fwd(u, gt, cw, cb, lng, lnb, B, S, comm=None):
    T = B * S
    CB = min(256, S)
    nb = S // CB

    def body(u_ref, gt_ref, w_ref, cb_ref, lg_ref, lb_ref, y_ref, co_ref, acat, ash):
        j = pl.program_id(1)

        @pl.when(j == 0)
        def _():
            acat[0:CONV_HALO, :] = jnp.zeros((CONV_HALO, CONV_CH), F32)

        acat[CONV_HALO:CONV_HALO + CB, :] = u_ref[...] * _sig(gt_ref[...])
        _shifted_copies(acat, ash, CB + CONV_HALO - SUB)
        acc = jnp.zeros((CB, CONV_CH), F32) + cb_ref[...]
        for k in range(CONV_K):
            acc = acc + w_ref[k:k + 1, :] * _tap(acat, ash, CONV_HALO - (CONV_K - 1) + k, CB)
        acat[0:CONV_HALO, :] = acat[CB:CB + CONV_HALO, :]
        y_ref[...] = acc
        _, _, l = _layernorm_silu(acc, lg_ref[...], lb_ref[...])
        co_ref[...] = (l * _sig(l)).astype(BF16)

    tok = lambda w: pl.BlockSpec((CB, w), lambda b, j: (b * nb + j, 0))
    return _call(
        body, comm, name="conv_fwd", grid=(B, nb),
        in_specs=[tok(CONV_CH), tok(CONV_CH), _resident(cw), _resident(cb), _resident(lng), _resident(lnb)],
        out_specs=[tok(CONV_CH), tok(CONV_CH)],
        out_shape=[_sds((T, CONV_CH), F32), _sds((T, CONV_CH), BF16)],
        scratch_shapes=[pltpu.VMEM((CONV_HALO + CB, CONV_CH), F32),
                        pltpu.VMEM((SUB, CB + CONV_HALO - SUB, CONV_CH), F32)],
        args=(u, gt, cw, cb, lng, lnb))


def _fox_fwd(q, k, v, cq, ckT, B, S, comm=None):
    T = B * S
    TQ = min(256, S)
    nq = S // TQ
    scale = 1.0 / math.sqrt(HEAD_D)
    one_lane = (HEAD_D, 0)

    def body(q_ref, k_ref, v_ref, cq_ref, ck_ref, o_ref, lse_ref, s_scr, s_odd, m_scr, acc_scr):
        i = pl.program_id(2)
        lane = lax.broadcasted_iota(jnp.int32, (TQ, LANES), 1)
        lo = lane < HEAD_D
        qs = q_ref[...] * jnp.asarray(scale, BF16)
        zero = jnp.zeros_like(qs)
        qh = (jnp.where(lo, qs, zero), jnp.where(lo, zero, qs))
        cqv = cq_ref[...]
        cq_rep = tuple(jnp.broadcast_to(cqv[:, h * HEAD_D:h * HEAD_D + 1], (TQ, LANES)) for h in range(2))
        m_scr[...] = jnp.full(m_scr.shape, -1e30, F32)
        acc_scr[...] = jnp.zeros_like(acc_scr)
        row = lax.broadcasted_iota(jnp.int32, (TQ, TQ), 0)
        col = lax.broadcasted_iota(jnp.int32, (TQ, TQ), 1)
        wide = lambda x: jnp.concatenate([x, x], axis=1) if TQ == 2 * LANES else jnp.tile(x, (1, TQ // LANES))

        def scores(j, s_buf):
            kj = k_ref[pl.ds(pl.multiple_of(j * TQ, TQ), TQ), :]
            for h in range(2):
                s_buf[h] = _dot_nt(qh[h], kj)

        def softmax_step(j, s_buf, diagonal):
            start = pl.multiple_of(j * TQ, TQ)
            vj = v_ref[pl.ds(start, TQ), :]
            for h in range(2):
                ck = ck_ref[0, 0, h:h + 1, pl.ds(start, TQ)]

                def logits():
                    t = (s_buf[h] - ck) + wide(cq_rep[h])
                    return jnp.where(col <= row, t, -1e30) if diagonal else t

                m_old = m_scr[h]
                m_new = jnp.maximum(m_old, jnp.max(logits(), axis=-1, keepdims=True))
                alpha = jnp.exp(m_old - m_new)
                m_scr[h] = m_new
                p = jnp.exp(logits() - wide(m_new)).astype(BF16)
                vx = jnp.where(lane == one_lane[h], jnp.ones_like(vj), jnp.where(lo if h == 0 else ~lo, vj, jnp.zeros_like(vj)))
                acc_scr[h] = alpha * acc_scr[h] + _dot(p, vx)

        def two_blocks(jj, carry):
            j = 2 * jj
            scores(j + 1, s_odd)
            softmax_step(j, s_scr, False)
            scores(j + 2, s_scr)
            softmax_step(j + 1, s_odd, False)
            return carry

        scores(0, s_scr)
        lax.fori_loop(0, i // 2, two_blocks, 0)

        @pl.when(i % 2 == 0)
        def _():
            softmax_step(i, s_scr, True)

        @pl.when(i % 2 == 1)
        def _():
            scores(i, s_odd)
            softmax_step(i - 1, s_scr, False)
            softmax_step(i, s_odd, True)

        acc_a, acc_b = acc_scr[0], acc_scr[1]
        l_a = acc_a[:, one_lane[0]:one_lane[0] + 1]
        l_b = acc_b[:, one_lane[1]:one_lane[1] + 1]
        o_ref[...] = jnp.where(lo, acc_a / l_a, acc_b / l_b)
        lse_ref[...] = cqv - jnp.where(lo, m_scr[0] + jnp.log(l_a), m_scr[1] + jnp.log(l_b))

    qspec = pl.BlockSpec((TQ, LANES), lambda b, p, i: (b * nq + i, p))
    kspec = pl.BlockSpec((S, LANES), lambda b, p, i: (b, p))
    return _call(
        body, comm, name="fox_fwd", grid=(B, N_PAIR, nq),
        in_specs=[qspec, kspec, kspec, qspec, pl.BlockSpec((1, 1, 8, S), lambda b, p, i: (b, p, 0, 0))],
        out_specs=[qspec, qspec],
        out_shape=[_sds((T, FOX_W), F32), _sds((T, FOX_W), F32)],
        scratch_shapes=[pltpu.VMEM((2, TQ, TQ), F32), pltpu.VMEM((2, TQ, TQ), F32),
                        pltpu.VMEM((2, TQ, LANES), F32), pltpu.VMEM((2, TQ, LANES), F32)],
        args=(q, k, v, cq, ckT))


def _mem_kv(mem2, g_mem, w_mkv, B):
    def body(m_ref, g_ref, w_ref, mn_ref, km_ref, vm_ref):
        _, xh = _rms(m_ref[...])
        mn = (xh * g_ref[...]).astype(BF16)
        mn_ref[...] = mn
        for s in range(2):
            km_ref[:, 512 * s:512 * (s + 1)] = _dot(mn, w_ref[s]).astype(BF16)
            vm_ref[:, 512 * s:512 * (s + 1)] = _dot(mn, w_ref[2 + s]).astype(BF16)

    blk = pl.BlockSpec((MEM_LEN, D), lambda b: (b, 0))
    return pl.pallas_call(
        body, name="mem_kv", grid=(B,),
        in_specs=[blk, _resident(g_mem), _resident(w_mkv)],
        out_specs=[blk, blk, blk],
        out_shape=[_sds((B * MEM_LEN, D), BF16)] * 3,
        compiler_params=_params(1),
    )(mem2, g_mem, w_mkv)


def _mem_probs(qm, km):
    ps = []
    for h in range(MEM_HEADS):
        hs = slice(h * MEM_HD, (h + 1) * MEM_HD)
        lg = _dot_nt(qm[:, hs], km[:, hs]) * (1.0 / math.sqrt(MEM_HD))
        e = jnp.exp(lg - jnp.max(lg, axis=-1, keepdims=True))
        ps.append(e / jnp.sum(e, axis=-1, keepdims=True))
    return ps


def _fwd_mid(x2, co, o, km, vm, w_out, w_mq, w_mo, g_x, B, S):
    T = B * S
    TB = min(512, S)
    nb = S // TB

    def body(x_ref, co_ref, o_ref, km_ref, vm_ref, wo_ref, wq_ref, wm_ref, g_ref,
             x1_ref, hx_ref, qm_ref, om_ref, x2_ref, cat_ref):
        cat_ref[:, 0:CONV_CH] = co_ref[...]
        cat_ref[:, CONV_CH:D] = o_ref[...].astype(BF16)
        x1 = x_ref[...] + _dot(cat_ref[...], wo_ref[...])
        x1_ref[...] = x1
        _, xh = _rms(x1)
        hx = (xh * g_ref[...]).astype(BF16)
        hx_ref[...] = hx
        qm = _dot(hx, wq_ref[...]).astype(BF16)
        qm_ref[...] = qm
        ps = _mem_probs(qm, km_ref[...])
        vmv = vm_ref[...]
        for h in range(MEM_HEADS):
            hs = slice(h * MEM_HD, (h + 1) * MEM_HD)
            om_ref[:, hs] = _dot(ps[h].astype(BF16), vmv[:, hs]).astype(BF16)
        x2_ref[...] = x1 + _dot(om_ref[...], wm_ref[...])

    tok = lambda w: pl.BlockSpec((TB, w), lambda b, j: (b * nb + j, 0))
    memb = pl.BlockSpec((MEM_LEN, D), lambda b, j: (b, 0))
    outs = [(D, F32), (D, BF16), (D, BF16), (D, BF16), (D, F32), (D, BF16)]
    return pl.pallas_call(
        body, name="fwd_mid", grid=(B, nb),
        in_specs=[tok(D), tok(CONV_CH), tok(FOX_W), memb, memb, _resident(w_out), _resident(w_mq), _resident(w_mo),
                  _resident(g_x)],
        out_specs=[tok(w) for w, _ in outs],
        out_shape=[_sds((T, w), dt) for w, dt in outs],
        compiler_params=_params(2),
    )(x2, co, o, km, vm, w_out, w_mq, w_mo, g_x)


def _fwd_ffn(x2, tgt, w_gu, w_down, g_ffn, g_final, T):
    TB = min(256, T)
    nb = T // TB

    def body(x_ref, t_ref, wgu_ref, wd_ref, gf_ref, gl_ref, hf_ref, gu_ref, act_ref, dx3_ref, loss_ref, dgl_ref):
        i = pl.program_id(0)

        @pl.when(i == 0)
        def _():
            loss_ref[...] = jnp.zeros_like(loss_ref)
            dgl_ref[...] = jnp.zeros_like(dgl_ref)

        x2v = x_ref[...]
        _, xh = _rms(x2v)
        hf = (xh * gf_ref[...]).astype(BF16)
        hf_ref[...] = hf
        x3 = x2v
        for ch in range(D_FF // FF_CHUNK):
            c0 = ch * FF_CHUNK
            g = _dot(hf, wgu_ref[ch])
            u = _dot(hf, wgu_ref[2 + ch])
            gu_ref[:, c0:c0 + FF_CHUNK] = g
            gu_ref[:, D_FF + c0:D_FF + c0 + FF_CHUNK] = u
            act = (g * _sig(g) * u).astype(BF16)
            act_ref[:, c0:c0 + FF_CHUNK] = act
            x3 = x3 + _dot(act, wd_ref[c0:c0 + FF_CHUNK, :])
        r3, xh3 = _rms(x3)
        gl = gl_ref[...]
        e = xh3 * gl - t_ref[...]
        loss_ref[...] += jnp.sum(e * e) * (0.5 / D)
        dy = e * (1.0 / D)
        dx3, dgl = _rms_bwd(dy, xh3, r3, gl)
        dx3_ref[...] = dx3
        dgl_ref[...] += jnp.sum(dgl, axis=0, keepdims=True)

    tok = lambda w: pl.BlockSpec((TB, w), lambda i: (i, 0))
    return pl.pallas_call(
        body, name="fwd_ffn", grid=(nb,),
        in_specs=[tok(D), tok(D), _resident(w_gu), _resident(w_down), _resident(g_ffn), _resident(g_final)],
        out_specs=[tok(D), tok(2 * D_FF), tok(D_FF), tok(D), _acc_spec((1, LANES)), _acc_spec((1, D))],
        out_shape=[_sds((T, D), BF16), _sds((T, 2 * D_FF), F32), _sds((T, D_FF), BF16), _sds((T, D), F32),
                   _sds((1, LANES), F32), _sds((1, D), F32)],
        compiler_params=_params(1),
    )(x2, tgt, w_gu, w_down, g_ffn, g_final)


def _bwd_ffn(dx3, gu, x2, w_gu, w_down, g_ffn, T):
    TB = min(256, T)
    nb = T // TB

    def body(d_ref, gu_ref, x_ref, wgu_ref, wd_ref, gf_ref, dgu_ref, dx2_ref, dgf_ref):
        i = pl.program_id(0)

        @pl.when(i == 0)
        def _():
            dgf_ref[...] = jnp.zeros_like(dgf_ref)

        dx3v = d_ref[...]
        db = dx3v.astype(BF16)
        dhf = jnp.zeros((TB, D), F32)
        for ch in range(D_FF // FF_CHUNK):
            c0 = ch * FF_CHUNK
            dact = _dot_nt(db, wd_ref[c0:c0 + FF_CHUNK, :])
            g = gu_ref[:, c0:c0 + FF_CHUNK]
            u = gu_ref[:, D_FF + c0:D_FF + c0 + FF_CHUNK]
            sg = _sig(g)
            dg = (dact * u * sg * (1.0 + g * (1.0 - sg))).astype(BF16)
            du = (dact * g * sg).astype(BF16)
            dgu_ref[:, c0:c0 + FF_CHUNK] = dg
            dgu_ref[:, D_FF + c0:D_FF + c0 + FF_CHUNK] = du
            dhf = dhf + _dot_nt(dg, wgu_ref[ch]) + _dot_nt(du, wgu_ref[2 + ch])
        r2, xh2 = _rms(x_ref[...])
        dx, dg_tok = _rms_bwd(dhf, xh2, r2, gf_ref[...])
        dx2_ref[...] = dx3v + dx
        dgf_ref[...] += jnp.sum(dg_tok, axis=0, keepdims=True)

    tok = lambda w: pl.BlockSpec((TB, w), lambda i: (i, 0))
    return pl.pallas_call(
        body, name="bwd_ffn", grid=(nb,),
        in_specs=[tok(D), tok(2 * D_FF), tok(D), _resident(w_gu), _resident(w_down), _resident(g_ffn)],
        out_specs=[tok(2 * D_FF), tok(D), _acc_spec((1, D))],
        out_shape=[_sds((T, 2 * D_FF), BF16), _sds((T, D), F32), _sds((1, D), F32)],
        compiler_params=_params(1),
    )(dx3, gu, x2, w_gu, w_down, g_ffn)


def _bwd_mid(dx2, x1, qm, km, vm, o, w_mo, w_mq, w_out, g_x, B, S, comm=None):
    T = B * S
    TB = min(512, S)
    nb = S // TB
    inv = 1.0 / math.sqrt(MEM_HD)

    def body(d_ref, x1_ref, qm_ref, km_ref, vm_ref, o_ref, wm_ref, wq_ref, wo_ref, g_ref,
             dx1_ref, dqm_ref, dco_ref, do_ref, dd_ref, dkm_ref, dvm_ref, dgx_ref):
        b = pl.program_id(0)
        j = pl.program_id(1)

        @pl.when((b == 0) & (j == 0))
        def _():
            dgx_ref[...] = jnp.zeros_like(dgx_ref)

        @pl.when(j == 0)
        def _():
            dkm_ref[...] = jnp.zeros_like(dkm_ref)
            dvm_ref[...] = jnp.zeros_like(dvm_ref)

        dx2v = d_ref[...]
        dom = _dot_nt(dx2v.astype(BF16), wm_ref[...]).astype(BF16)
        qmv = qm_ref[...]
        kmv = km_ref[...]
        vmv = vm_ref[...]
        ps = _mem_probs(qmv, kmv)
        for h in range(MEM_HEADS):
            hs = slice(h * MEM_HD, (h + 1) * MEM_HD)
            p = ps[h]
            dp = _dot_nt(dom[:, hs], vmv[:, hs])
            ds = (p * (dp - jnp.sum(p * dp, axis=-1, keepdims=True))).astype(BF16)
            dqm_ref[:, hs] = (_dot(ds, kmv[:, hs]) * inv).astype(BF16)
            dkm_ref[:, hs] += _dot_tn(ds, qmv[:, hs]) * inv
            dvm_ref[:, hs] += _dot_tn(p.astype(BF16), dom[:, hs])
        dhx = _dot_nt(dqm_ref[...], wq_ref[...])
        r1, xh1 = _rms(x1_ref[...])
        dx, dg_tok = _rms_bwd(dhx, xh1, r1, g_ref[...])
        dx1 = dx2v + dx
        dx1_ref[...] = dx1
        dgx_ref[...] += jnp.sum(dg_tok, axis=0, keepdims=True)
        d1b = dx1.astype(BF16)
        dco_ref[...] = _dot_nt(d1b, wo_ref[0:CONV_CH, :])
        do = _dot_nt(d1b, wo_ref[CONV_CH:D, :])
        dob = do.astype(BF16)
        do_ref[...] = dob
        per_head = _dot_hi(dob.astype(F32) * o_ref[...], _head_collect(FOX_W, LANES))
        dd_ref[...] = _dot_hi(per_head, _head_expand(LANES, FOX_W))

    tok = lambda w: pl.BlockSpec((TB, w), lambda b, j: (b * nb + j, 0))
    memb = pl.BlockSpec((MEM_LEN, D), lambda b, j: (b, 0))
    outs = [(D, F32), (D, BF16), (CONV_CH, F32), (FOX_W, BF16), (FOX_W, F32)]
    return _call(
        body, comm, name="bwd_mid", grid=(B, nb),
        in_specs=[tok(D), tok(D), tok(D), memb, memb, tok(FOX_W), _resident(w_mo), _resident(w_mq), _resident(w_out),
                  _resident(g_x)],
        out_specs=[tok(w) for w, _ in outs] + [memb, memb, _acc_spec((1, D))],
        out_shape=[_sds((T, w), dt) for w, dt in outs] + [_sds((B * MEM_LEN, D), F32)] * 2 + [_sds((1, D), F32)],
        scratch_shapes=[],
        args=(dx2, x1, qm, km, vm, o, w_mo, w_mq, w_out, g_x))


def _mem_bwd(dkm, dvm, mem2, w_mkv, g_mem, B):
    def body(dk_ref, dv_ref, m_ref, w_ref, g_ref, dkv_ref, dg_ref):
        b = pl.program_id(0)

        @pl.when(b == 0)
        def _():
            dg_ref[...] = jnp.zeros_like(dg_ref)

        dk = dk_ref[...].astype(BF16)
        dv = dv_ref[...].astype(BF16)
        dkv_ref[:, 0:D] = dk
        dkv_ref[:, D:2 * D] = dv
        dmn = jnp.zeros((MEM_LEN, D), F32)
        for s in range(2):
            dmn = dmn + _dot_nt(dk[:, 512 * s:512 * (s + 1)], w_ref[s]) + _dot_nt(dv[:, 512 * s:512 * (s + 1)], w_ref[2 + s])
        _, xh = _rms(m_ref[...])
        dg_ref[...] += jnp.sum(dmn * xh, axis=0, keepdims=True)

    blk = pl.BlockSpec((MEM_LEN, D), lambda b: (b, 0))
    return pl.pallas_call(
        body, name="mem_bwd", grid=(B,),
        in_specs=[blk, blk, blk, _resident(w_mkv), _resident(g_mem)],
        out_specs=[pl.BlockSpec((MEM_LEN, 2 * D), lambda b: (b, 0)), _acc_spec((1, D))],
        out_shape=[_sds((B * MEM_LEN, 2 * D), BF16), _sds((1, D), F32)],
        compiler_params=_params(1),
    )(dkm, dvm, mem2, w_mkv, g_mem)


def _fox_bwd(q, k, v, do, bias, dd, ckT, B, S, comm=None):
    T = B * S
    TK = min(256, S)
    nk = S // TK
    scale = 1.0 / math.sqrt(HEAD_D)

    def body(q_ref, k_ref, v_ref, do_ref, bias_ref, dd_ref, ck_ref, dq_ref, dk_ref, dv_ref, dc_ref, dcq_ref,
             dq_acc, rs_acc, s_scr, dp_scr, s_odd, dp_odd, dk_acc, dv_acc, dc_acc):
        j = pl.program_id(2)

        @pl.when(j == 0)
        def _():
            dq_acc[...] = jnp.zeros_like(dq_acc)
            rs_acc[...] = jnp.zeros_like(rs_acc)

        dk_acc[...] = jnp.zeros_like(dk_acc)
        dv_acc[...] = jnp.zeros_like(dv_acc)
        dc_acc[...] = jnp.zeros_like(dc_acc)
        lane = lax.broadcasted_iota(jnp.int32, (TK, LANES), 1)
        lo = lane < HEAD_D
        ks = k_ref[...] * jnp.asarray(scale, BF16)
        v2 = v_ref[...]
        zero = jnp.zeros_like(ks)
        kh = (jnp.where(lo, ks, zero), jnp.where(lo, zero, ks))
        vh = (jnp.where(lo, v2, zero), jnp.where(lo, zero, v2))
        kstart = pl.multiple_of(j * TK, TK)
        ckh = tuple(ck_ref[0, 0, h:h + 1, pl.ds(kstart, TK)] for h in range(2))
        row = lax.broadcasted_iota(jnp.int32, (TK, TK), 0)
        col = lax.broadcasted_iota(jnp.int32, (TK, TK), 1)
        wide = lambda x: jnp.concatenate([x, x], axis=1) if TK == 2 * LANES else jnp.tile(x, (1, TK // LANES))

        def scores(i, s_buf, dp_buf):
            start = pl.multiple_of(i * TK, TK)
            qi = q_ref[pl.ds(start, TK), :]
            doi = do_ref[pl.ds(start, TK), :]
            for h in range(2):
                s_buf[h] = _dot_nt(qi, kh[h])
                dp_buf[h] = _dot_nt(doi, vh[h])

        def grads(i, s_buf, dp_buf, diagonal):
            start = pl.multiple_of(i * TK, TK)
            qi = q_ref[pl.ds(start, TK), :]
            doi = do_ref[pl.ds(start, TK), :]
            bias2 = bias_ref[pl.ds(start, TK), :]
            dd2 = dd_ref[pl.ds(start, TK), :]
            for h in range(2):
                hc = slice(h * HEAD_D, h * HEAD_D + 1)
                bias = jnp.broadcast_to(bias2[:, hc], (TK, LANES))
                ddh = jnp.broadcast_to(dd2[:, hc], (TK, LANES))
                p = jnp.exp((s_buf[h] - ckh[h]) + wide(bias))
                if diagonal:
                    p = jnp.where(col <= row, p, 0.0)
                ds = p * (dp_buf[h] - wide(ddh))
                dc_acc[h, 0:1, :] += jnp.sum(ds, axis=0, keepdims=True)
                rs_acc[h, pl.ds(start, TK), :] += jnp.sum(ds, axis=1, keepdims=True)
                pb = p.astype(BF16)
                dsb = ds.astype(BF16)
                dv_acc[h] += _dot_tn(pb, doi)
                dk_acc[h] += _dot_tn(dsb, qi)
                dq_acc[pl.ds(start, TK), :] += _dot(dsb, kh[h])

        n_off = nk - 1 - j
        block = lambda t: jnp.where(t < n_off, j + 1 + t, j)

        def two_blocks(tt, carry):
            t = 2 * tt
            scores(block(t + 1), s_odd, dp_odd)
            grads(block(t), s_scr, dp_scr, False)
            scores(block(t + 2), s_scr, dp_scr)
            grads(block(t + 1), s_odd, dp_odd, False)
            return carry

        scores(block(0), s_scr, dp_scr)
        lax.fori_loop(0, n_off // 2, two_blocks, 0)

        @pl.when(n_off % 2 == 0)
        def _():
            grads(j, s_scr, dp_scr, True)

        @pl.when(n_off % 2 == 1)
        def _():
            scores(j, s_odd, dp_odd)
            grads(nk - 1, s_scr, dp_scr, False)
            grads(j, s_odd, dp_odd, True)

        dk_ref[...] = (jnp.where(lo, dk_acc[0], dk_acc[1]) * scale).astype(BF16)
        dv_ref[...] = jnp.where(lo, dv_acc[0], dv_acc[1]).astype(BF16)
        sub = lax.broadcasted_iota(jnp.int32, (8, TK), 0)
        dca = dc_acc[0, 0:1, :]
        dcb = dc_acc[1, 0:1, :]
        dc_ref[0, 0] = jnp.where(sub == 0, -dca, jnp.where(sub == 1, -dcb, 0.0))

        @pl.when(j == nk - 1)
        def _():
            dq_ref[...] = dq_acc[...].astype(BF16)
            lo_s = lax.broadcasted_iota(jnp.int32, (S, LANES), 1) < HEAD_D
            dcq_ref[...] = jnp.where(lo_s, rs_acc[0], rs_acc[1])

    full = pl.BlockSpec((S, LANES), lambda b, p, j: (b, p))
    blk = pl.BlockSpec((TK, LANES), lambda b, p, j: (b * nk + j, p))
    return _call(
        body, comm, name="fox_bwd", grid=(B, N_PAIR, nk),
        in_specs=[full, blk, blk, full, full, full, pl.BlockSpec((1, 1, 8, S), lambda b, p, j: (b, p, 0, 0))],
        out_specs=[full, blk, blk, pl.BlockSpec((1, 1, 8, TK), lambda b, p, j: (b, p, 0, j)), full],
        out_shape=[_sds((T, FOX_W), BF16), _sds((T, FOX_W), BF16), _sds((T, FOX_W), BF16),
                   _sds((B, N_PAIR, 8, S), F32), _sds((T, FOX_W), F32)],
        scratch_shapes=[pltpu.VMEM((S, LANES), F32), pltpu.VMEM((2, S, 1), F32),
                        pltpu.VMEM((2, TK, TK), F32), pltpu.VMEM((2, TK, TK), F32),
                        pltpu.VMEM((2, TK, TK), F32), pltpu.VMEM((2, TK, TK), F32),
                        pltpu.VMEM((2, TK, LANES), F32), pltpu.VMEM((2, TK, LANES), F32), pltpu.VMEM((2, 8, TK), F32)],
        args=(q, k, v, do, bias, dd, ckT))


def _fgate_bwd(dc8, zf, B, S):
    T = B * S
    TB = min(512, S)
    nb = S // TB

    def body(dc_ref, zf_ref, dzf_ref, dbf_ref, carry):
        b = pl.program_id(0)
        j = pl.program_id(1)

        @pl.when((b == 0) & (j == 0))
        def _():
            dbf_ref[...] = jnp.zeros_like(dbf_ref)

        @pl.when(j == 0)
        def _():
            carry[...] = jnp.zeros_like(carry)

        dc = dc_ref[...]
        row = lax.broadcasted_iota(jnp.int32, (TB, TB), 0)
        col = lax.broadcasted_iota(jnp.int32, (TB, TB), 1)
        dlogf = _dot_hi((col >= row).astype(F32), dc) + carry[0:1, :]
        carry[0:1, :] = dlogf[0:1, :]
        lane = lax.broadcasted_iota(jnp.int32, dc.shape, 1)
        dzf = jnp.where(lane < 8, dlogf * _sig(-zf_ref[...]), 0.0)
        dzf_ref[...] = dzf.astype(BF16)
        dbf_ref[...] += jnp.sum(dzf, axis=0, keepdims=True)

    tok = pl.BlockSpec((TB, LANES), lambda b, j: (b * nb + (nb - 1 - j), 0))
    return pl.pallas_call(
        body, name="fgate_bwd", grid=(B, nb),
        in_specs=[tok, tok],
        out_specs=[tok, _acc_spec((1, LANES))],
        out_shape=[_sds((T, LANES), BF16), _sds((1, LANES), F32)],
        scratch_shapes=[pltpu.VMEM((8, LANES), F32)],
        compiler_params=_params(2),
    )(dc8, zf)


def _conv_bwd(dco, y, u, gt, cw, lng, lnb, B, S, comm=None):
    T = B * S
    CB = min(256, S)
    nb = S // CB
    hb = CB // CONV_HALO

    def body(dco_ref, y_ref, u_ref, gt_ref, up_ref, gp_ref, w_ref, lg_ref, lb_ref,
             du_ref, dgt_ref, dw_ref, vec_ref, acat, dycat, ash, dysh):
        b = pl.program_id(0)
        j = pl.program_id(1)
        jr = nb - 1 - j

        @pl.when((b == 0) & (j == 0))
        def _():
            dw_ref[...] = jnp.zeros_like(dw_ref)
            vec_ref[...] = jnp.zeros_like(vec_ref)

        @pl.when(j == 0)
        def _():
            dycat[CB:CB + CONV_HALO, :] = jnp.zeros((CONV_HALO, CONV_CH), F32)

        lg = lg_ref[...]
        rs, n, l = _layernorm_silu(y_ref[...], lg, lb_ref[...])
        sg = _sig(l)
        dl = dco_ref[...] * (sg * (1.0 + l * (1.0 - sg)))
        dn = dl * lg
        dy = rs * (dn - jnp.mean(dn, axis=-1, keepdims=True) - n * jnp.mean(dn * n, axis=-1, keepdims=True))
        vec_ref[0:1, :] += jnp.sum(dy, axis=0, keepdims=True)
        vec_ref[1:2, :] += jnp.sum(dl * n, axis=0, keepdims=True)
        vec_ref[2:3, :] += jnp.sum(dl, axis=0, keepdims=True)
        dycat[0:CB, :] = dy
        uv = u_ref[...]
        sgt = _sig(gt_ref[...])
        acat[0:CONV_HALO, :] = jnp.where(jr > 0, up_ref[...] * _sig(gp_ref[...]), 0.0)
        acat[CONV_HALO:CONV_HALO + CB, :] = uv * sgt
        _shifted_copies(acat, ash, CB + CONV_HALO - SUB)
        _shifted_copies(dycat, dysh, CB + CONV_HALO - SUB)
        da = jnp.zeros((CB, CONV_CH), F32)
        for k in range(CONV_K):
            da = da + w_ref[k:k + 1, :] * _tap(dycat, dysh, CONV_K - 1 - k, CB)
            dw_ref[k:k + 1, :] += jnp.sum(dy * _tap(acat, ash, CONV_HALO - (CONV_K - 1) + k, CB), axis=0, keepdims=True)
        dycat[CB:CB + CONV_HALO, :] = dycat[0:CONV_HALO, :]
        du_ref[...] = (da * sgt).astype(BF16)
        dgt_ref[...] = (da * uv * sgt * (1.0 - sgt)).astype(BF16)

    tok = lambda w: pl.BlockSpec((CB, w), lambda b, j: (b * nb + (nb - 1 - j), 0))
    prev = pl.BlockSpec((CONV_HALO, CONV_CH), lambda b, j: (jnp.maximum((b * nb + (nb - 1 - j)) * hb - 1, 0), 0))
    return _call(
        body, comm, name="conv_bwd", grid=(B, nb),
        in_specs=[tok(CONV_CH), tok(CONV_CH), tok(CONV_CH), tok(CONV_CH), prev, prev, _resident(cw), _resident(lng),
                  _resident(lnb)],
        out_specs=[tok(CONV_CH), tok(CONV_CH), _acc_spec((CONV_HALO, CONV_CH)), _acc_spec((8, CONV_CH))],
        out_shape=[_sds((T, CONV_CH), BF16), _sds((T, CONV_CH), BF16), _sds((CONV_HALO, CONV_CH), F32),
                   _sds((8, CONV_CH), F32)],
        scratch_shapes=[pltpu.VMEM((CONV_HALO + CB, CONV_CH), F32), pltpu.VMEM((CB + CONV_HALO, CONV_CH), F32),
                        pltpu.VMEM((SUB, CB + CONV_HALO - SUB, CONV_CH), F32),
                        pltpu.VMEM((SUB, CB + CONV_HALO - SUB, CONV_CH), F32)],
        args=(dco, y, u, gt, u, gt, cw, lng, lnb))


def _bwd_in(dz, w_in, x2, dx1, g_mix, T, comm=None):
    TB = min(512, T)
    nb = T // TB

    def body(dz_ref, w_ref, x_ref, d1_ref, g_ref, gx_ref, dg_ref):
        i = pl.program_id(0)

        @pl.when(i == 0)
        def _():
            dg_ref[...] = jnp.zeros_like(dg_ref)

        dh = _dot_nt(dz_ref[...], w_ref[...])
        r0, xh0 = _rms(x_ref[...])
        dx, dg_tok = _rms_bwd(dh, xh0, r0, g_ref[...])
        gx_ref[...] = d1_ref[...] + dx
        dg_ref[...] += jnp.sum(dg_tok, axis=0, keepdims=True)

    tok = lambda w: pl.BlockSpec((TB, w), lambda i: (i, 0))
    return _call(
        body, comm, name="bwd_in", grid=(nb,),
        in_specs=[tok(D_IN_PAD), _resident(w_in), tok(D), tok(D), _resident(g_mix)],
        out_specs=[tok(D), _acc_spec((1, D))],
        out_shape=[_sds((T, D), F32), _sds((1, D), F32)],
        scratch_shapes=[],
        args=(dz, w_in, x2, dx1, g_mix))


def _dw(a, b, name, tn, slabs=False):
    T, K = a.shape
    N = b.shape[1]
    tk = K if K <= 1024 else K // 2
    tt = min(512, T)
    nt = T // tt

    def body(a_ref, b_ref, o_ref, acc):
        t = pl.program_id(2)

        @pl.when(t == 0)
        def _():
            acc[...] = jnp.zeros_like(acc)

        acc[...] += _dot_tn(a_ref[...].astype(BF16), b_ref[...].astype(BF16))

        @pl.when(t == nt - 1)
        def _():
            o_ref[...] = acc[...]

    return pl.pallas_call(
        body, name=name, grid=(K // tk, N // tn, nt),
        in_specs=[pl.BlockSpec((tt, tk), lambda i, j, t: (t, i)), pl.BlockSpec((tt, tn), lambda i, j, t: (t, j))],
        out_specs=(pl.BlockSpec((None, tk, tn), lambda i, j, t: (j, i, 0)) if slabs
                   else pl.BlockSpec((tk, tn), lambda i, j, t: (i, j))),
        out_shape=_sds((N // tn, K, tn) if slabs else (K, N), F32),
        scratch_shapes=[pltpu.VMEM((tk, tn), F32)],
        compiler_params=_params(3),
    )(a, b)


def _pos():
    return lax.axis_index("x"), lax.axis_index("y"), lax.axis_index("c")


def _remote(src, dst, ssem, rsem, to):
    return pltpu.make_async_remote_copy(src_ref=src, dst_ref=dst, send_sem=ssem, recv_sem=rsem, device_id=to,
                                        device_id_type=MESH)


def _half(ref_rows, c):
    H = ref_rows // 2
    return pl.ds(pl.multiple_of(c * H, 16), H)


class _Comm:
    def __init__(self, ins, out_shapes, sems, start, finish):
        self.ins, self.out_shapes, self.sems, self.start, self.finish = list(ins), list(out_shapes), list(sems), start, finish


def _ag_comm(shards):
    n = len(shards)

    def parts(ins, outs, sems):
        send_sems, recv_sems, local_sems = sems
        x, y, c = _pos()
        me, sib = (x, y, c), (x, y, 1 - c)
        chips = [(1 - x, y), (x, 1 - y), (1 - x, 1 - y)]

        def rows(w, px, py, pc):
            return outs[w].at[2 * px + py, _half(shards[w].shape[0], pc), :]

        def copy(w, k, block, to, src=None):
            return _remote(rows(w, *block) if src is None else src, rows(w, *block), send_sems.at[w, k],
                           recv_sems.at[w, k], to)

        mine, first = [], []
        for w in range(n):
            src = ins[w].at[_half(shards[w].shape[0], c), :]
            mine.append(pltpu.make_async_copy(src, rows(w, *me), local_sems.at[w]))
            first += [copy(w, 0, me, sib, src=src)] + [copy(w, 1 + j, me, (*chip, c), src=src) for j, chip in enumerate(chips)]
        return c, me, sib, chips, copy, mine, first

    def start(ins, outs, sems):
        _, _, _, _, _, mine, first = parts(ins, outs, sems)
        for cp in mine + first:
            cp.start()

    def finish(ins, outs, sems):
        c, me, sib, chips, copy, mine, first = parts(ins, outs, sems)
        passed = []
        for w in range(n):
            for j, chip in enumerate(chips):
                copy(w, 1 + j, (*chip, c), me).wait_recv()
                passed.append(copy(w, 4 + j, (*chip, c), sib))
                passed[-1].start()
        for w in range(n):
            copy(w, 0, sib, me).wait_recv()
            for j, chip in enumerate(chips):
                copy(w, 4 + j, (*chip, 1 - c), me).wait_recv()
        for cp in first + passed:
            cp.wait_send()
        for cp in mine:
            cp.wait()

    D7 = pltpu.SemaphoreType.DMA((n, 7))
    return _Comm(shards, [_sds((4,) + s.shape, s.dtype) for s in shards], [D7, D7, pltpu.SemaphoreType.DMA((n,))],
                 start, finish)


def _sibling_comm(gs):
    n = len(gs)

    def copies(ins, outs, sems):
        send_sems, recv_sems = sems
        x, y, c = _pos()
        return [_remote(ins[w].at[s, _half(gs[w].shape[1], 1 - c), :], outs[w].at[s], send_sems.at[w, s],
                        recv_sems.at[w, s], (x, y, 1 - c)) for w in range(n) for s in range(4)]

    def start(ins, outs, sems):
        for cp in copies(ins, outs, sems):
            cp.start()

    def finish(ins, outs, sems):
        for cp in copies(ins, outs, sems):
            cp.wait()

    D4 = pltpu.SemaphoreType.DMA((n, 4))
    return _Comm(gs, [_sds((4, g.shape[1] // 2, g.shape[2]), F32) for g in gs], [D4, D4], start, finish)


def _ici_comm(pbs):
    n = len(pbs)

    def copies(ins, outs, sems):
        send_sems, recv_sems = sems
        x, y, c = _pos()
        return [_remote(ins[w].at[2 * tx + ty], outs[w].at[j], send_sems.at[w, j], recv_sems.at[w, j], (tx, ty, c))
                for w in range(n) for j, (tx, ty) in enumerate([(1 - x, y), (x, 1 - y), (1 - x, 1 - y)])]

    def start(ins, outs, sems):
        for cp in copies(ins, outs, sems):
            cp.start()

    def finish(ins, outs, sems):
        for cp in copies(ins, outs, sems):
            cp.wait()

    D3 = pltpu.SemaphoreType.DMA((n, 3))
    return _Comm(pbs, [_sds((3,) + p.shape[1:], BF16) for p in pbs], [D3, D3], start, finish)


def _join(*comms):
    counts = [(len(c.ins), len(c.out_shapes), len(c.sems)) for c in comms]

    def each(which):
        def run(ins, outs, sems):
            i = o = k = 0
            for c, (ni, no, nk) in zip(comms, counts):
                getattr(c, which)(ins[i:i + ni], outs[o:o + no], sems[k:k + nk])
                i, o, k = i + ni, o + no, k + nk
        return run

    return _Comm(sum((c.ins for c in comms), []), sum((c.out_shapes for c in comms), []),
                 sum((c.sems for c in comms), []), each("start"), each("finish"))


def _run_comm(comm, name):
    ni, no = len(comm.ins), len(comm.out_shapes)

    def body(*refs):
        ins, outs, sems = refs[:ni], refs[ni:ni + no], refs[ni + no:]
        comm.start(ins, outs, sems)
        comm.finish(ins, outs, sems)

    return pl.pallas_call(body, name=name, out_shape=comm.out_shapes, in_specs=[ANY] * ni, out_specs=[ANY] * no,
                          scratch_shapes=comm.sems)(*comm.ins)


def _call(body, comm, *, name, grid, in_specs, out_specs, out_shape, scratch_shapes, args):
    n_grid = len(grid)
    if comm is None:
        res = pl.pallas_call(body, name=name, grid=grid, in_specs=in_specs, out_specs=out_specs, out_shape=out_shape,
                             scratch_shapes=scratch_shapes, compiler_params=_params(n_grid))(*args)
        return list(res), []
    n_in, n_out, n_scr = len(in_specs), len(out_specs), len(scratch_shapes)
    ni, no = len(comm.ins), len(comm.out_shapes)

    def carried(*refs):
        ins, refs = refs[:n_in], refs[n_in:]
        cins, refs = refs[:ni], refs[ni:]
        outs, refs = refs[:n_out], refs[n_out:]
        couts, refs = refs[:no], refs[no:]
        scr, csems = refs[:n_scr], refs[n_scr:]
        ids = [pl.program_id(ax) for ax in range(n_grid)]
        first = functools.reduce(jnp.logical_and, [i == 0 for i in ids])
        last = functools.reduce(jnp.logical_and, [i == g - 1 for i, g in zip(ids, grid)])

        @pl.when(first)
        def _():
            comm.start(cins, couts, csems)

        body(*ins, *outs, *scr)

        @pl.when(last)
        def _():
            comm.finish(cins, couts, csems)

    res = pl.pallas_call(
        carried, name=name, grid=grid, in_specs=list(in_specs) + [ANY] * ni, out_specs=list(out_specs) + [ANY] * no,
        out_shape=list(out_shape) + comm.out_shapes, scratch_shapes=list(scratch_shapes) + comm.sems,
        compiler_params=_params(n_grid))(*args, *comm.ins)
    return list(res[:n_out]), list(res[n_out:])


def _sibling_share(gs):
    n = len(gs)

    def body(*refs):
        outs = refs[n:2 * n]
        send_sems, recv_sems = refs[2 * n:]
        x, y, c = _pos()
        cps = []
        for w in range(n):
            mine = outs[w].at[_half(gs[w].shape[0], c), :]
            cps.append(_remote(mine, mine, send_sems.at[w], recv_sems.at[w], (x, y, 1 - c)))
            cps[-1].start()
        for cp in cps:
            cp.wait()

    return pl.pallas_call(
        body, name="rs_share", out_shape=[_sds(g.shape, F32) for g in gs],
        in_specs=[ANY] * n, out_specs=[ANY] * n, input_output_aliases={w: w for w in range(n)},
        scratch_shapes=[pltpu.SemaphoreType.DMA((n,)), pltpu.SemaphoreType.DMA((n,))],
    )(*gs)


def _small_allreduce(v, name):
    P = v.shape[0]
    vm = pl.BlockSpec(memory_space=pltpu.VMEM)

    def body(v_ref, o_ref, gath, send_sems, recv_sems):
        x, y, c = _pos()
        me = 4 * x + 2 * y + c
        gath[me] = v_ref[...]
        cps = []
        for r in range(1, 8):
            tx = (1 - x) if r & 4 else x
            ty = (1 - y) if r & 2 else y
            tc = (1 - c) if r & 1 else c
            cps.append(_remote(v_ref, gath.at[me], send_sems.at[r - 1], recv_sems.at[r - 1], (tx, ty, tc)))
            cps[-1].start()
        for cp in cps:
            cp.wait()
        acc = gath[0]
        for d in range(1, 8):
            acc = acc + gath[d]
        o_ref[...] = acc

    return pl.pallas_call(
        body, name=name, out_shape=_sds((P, LANES), F32), in_specs=[vm], out_specs=vm,
        scratch_shapes=[pltpu.VMEM((8, P, LANES), F32), pltpu.SemaphoreType.DMA((7,)), pltpu.SemaphoreType.DMA((7,))],
    )(v)


def _chip_sum(g, rcv, pos, name):
    _, R, C = g.shape
    H = R // 2

    def body(pos_ref, g_ref, r_ref, o_ref):
        o_ref[...] = (g_ref[...] + r_ref[...]).astype(BF16)

    return pl.pallas_call(
        body, name=name, out_shape=_sds((4, H, C), BF16),
        grid_spec=pltpu.PrefetchScalarGridSpec(
            num_scalar_prefetch=1, grid=(4,),
            in_specs=[pl.BlockSpec((1, H, C), lambda s, pos: (s, pos[0], 0)),
                      pl.BlockSpec((1, H, C), lambda s, pos: (s, 0, 0))],
            out_specs=pl.BlockSpec((1, H, C), lambda s, pos: (s, 0, 0))),
        compiler_params=_params(1),
    )(pos, g, rcv)


def _final_sum(g, rcv, rc, pos, name):
    _, R, C = g.shape
    Q = R // 4

    def body(pos_ref, g_ref, r_ref, rc_ref, o_ref):
        acc = g_ref[0] + r_ref[0]
        for j in range(3):
            acc = acc + rc_ref[j].astype(F32)
        o_ref[...] = acc

    return pl.pallas_call(
        body, name=name, out_shape=_sds((R, C), F32),
        grid_spec=pltpu.PrefetchScalarGridSpec(
            num_scalar_prefetch=1, grid=(2,),
            in_specs=[pl.BlockSpec((1, Q, C), lambda i, pos: (pos[1], pos[0] * 2 + i, 0)),
                      pl.BlockSpec((1, Q, C), lambda i, pos: (pos[1], i, 0)),
                      pl.BlockSpec((3, Q, C), lambda i, pos: (0, i, 0))],
            out_specs=pl.BlockSpec((Q, C), lambda i, pos: (pos[0] * 2 + i, 0))),
        compiler_params=_params(1),
    )(pos, g, rcv, rc)


def _adamw_math(w, g, m, v):
    m = ADAM_B1 * m + (1.0 - ADAM_B1) * g
    v = ADAM_B2 * v + (1.0 - ADAM_B2) * (g * g)
    m_hat = m / (1.0 - ADAM_B1 ** ADAM_STEP)
    v_hat = v / (1.0 - ADAM_B2 ** ADAM_STEP)
    delta = -ADAM_LR * (m_hat / (jnp.sqrt(v_hat) + ADAM_EPS) + ADAM_WD * w)
    return delta, m, v


def _adamw(w, g, m, v, name, rb):
    R, C = w.shape

    def body(w_ref, g_ref, m_ref, v_ref, go_ref, d_ref, nm_ref, nv_ref):
        g = g_ref[...]
        d, nm, nv = _adamw_math(w_ref[...], g, m_ref[...], v_ref[...])
        go_ref[...] = g
        d_ref[...] = d
        nm_ref[...] = nm
        nv_ref[...] = nv

    blk = pl.BlockSpec((rb, C), lambda i: (i, 0))
    return pl.pallas_call(
        body, name=name, grid=(R // rb,), in_specs=[blk] * 4, out_specs=[blk] * 4,
        out_shape=[_sds((R, C), F32)] * 4, compiler_params=_params(1),
    )(w, g, m, v)


SMALL = (("g_mix", 8), ("b_f", 8), ("conv_w", None), ("conv_b", 8), ("ln_g", 8), ("ln_b", 8), ("g_x", 8), ("g_mem", 8),
         ("g_ffn", 8), ("g_final", 8), ("loss", 8))


def _pack_small(parts, conv_rows):
    rows = []
    for name, n in SMALL:
        if name not in parts:
            continue
        n = conv_rows if n is None else n
        flat = parts[name].reshape(-1).astype(F32)
        flat = jnp.pad(flat, (0, n * LANES - flat.shape[0]))
        rows.append(flat.reshape(n, LANES))
    return jnp.concatenate(rows, axis=0)


def _unpack_small(p, shapes, conv_rows):
    out, off = {}, 0
    for name, n in SMALL:
        if name not in shapes:
            continue
        n = conv_rows if n is None else n
        size = math.prod(shapes[name])
        out[name] = p[off:off + n].reshape(-1)[:size].reshape(shapes[name])
        off += n
    return out


def kernel(x, mem, g_mix, w_in, b_f, conv_w, conv_b, ln_g, ln_b, w_out, g_x, g_mem, w_mq, w_mkv, w_mo, g_ffn, w_gu, w_down, g_final, loss_target, m_g_mix, m_w_in, m_b_f, m_conv_w, m_conv_b, m_ln_g, m_ln_b, m_w_out, m_g_x, m_g_mem, m_w_mq, m_w_mkv, m_w_mo, m_g_ffn, m_w_gu, m_w_down, m_g_final, v_g_mix, v_w_in, v_b_f, v_conv_w, v_conv_b, v_ln_g, v_ln_b, v_w_out, v_g_x, v_g_mem, v_w_mq, v_w_mkv, v_w_mo, v_g_ffn, v_w_gu, v_w_down, v_g_final):
    names = ["g_mix", "w_in", "b_f", "conv_w", "conv_b", "ln_g", "ln_b", "w_out", "g_x", "g_mem", "w_mq", "w_mkv",
             "w_mo", "g_ffn", "w_gu", "w_down", "g_final"]
    W = dict(zip(names, [g_mix, w_in, b_f, conv_w, conv_b, ln_g, ln_b, w_out, g_x, g_mem, w_mq, w_mkv, w_mo, g_ffn,
                         w_gu, w_down, g_final]))
    M = dict(zip(names, [m_g_mix, m_w_in, m_b_f, m_conv_w, m_conv_b, m_ln_g, m_ln_b, m_w_out, m_g_x, m_g_mem, m_w_mq,
                         m_w_mkv, m_w_mo, m_g_ffn, m_w_gu, m_w_down, m_g_final]))
    V = dict(zip(names, [v_g_mix, v_w_in, v_b_f, v_conv_w, v_conv_b, v_ln_g, v_ln_b, v_w_out, v_g_x, v_g_mem, v_w_mq,
                         v_w_mkv, v_w_mo, v_g_ffn, v_w_gu, v_w_down, v_g_final]))
    big_names = [n for n, _, _, _ in BIG]
    B, S, _ = x.shape
    T = B * S
    mx, my, mc = _pos()
    chip = 2 * mx + my
    pos = jnp.stack([mc, chip]).astype(jnp.int32)

    shard2d = lambda a: a.reshape(a.shape[-2], a.shape[-1])
    shard_bf = {n: shard2d(W[n]).astype(BF16) for n in big_names}
    ag_mid = ["w_mkv", "w_out", "w_mq", "w_mo"]
    ag_ffn = ["w_gu", "w_down"]
    slab = {"w_in": _run_comm(_ag_comm([shard_bf["w_in"]]), "ag_w_in")[0]}
    w_in_f = jnp.pad(jnp.transpose(slab["w_in"], (1, 0, 2)).reshape(D, D_IN), ((0, 0), (0, D_IN_PAD - D_IN)))
    cw_mine = jnp.pad(shard2d(conv_w), ((0, 1), (0, 0)))
    cw_slot = lax.dynamic_update_slice(jnp.zeros((CONV_HALO, CONV_CH), F32), cw_mine, (0, chip * LANES))
    cw = _small_allreduce(cw_slot.reshape(CONV_HALO * 4, LANES) * 0.5, "gather_conv_w").reshape(CONV_HALO, CONV_CH)

    row = lambda a: a.reshape(1, -1)
    bf_pad = jnp.pad(row(b_f), ((0, 0), (0, LANES - 8)))
    x2d = x.reshape(T, D)
    mem2d = mem.reshape(B * MEM_LEN, D)
    tgt = loss_target.reshape(T, D)

    (h, u, gt, q, k, v, zf, c, cq), got = _fwd_in(x2d, row(g_mix), w_in_f, bf_pad, B, S,
                                                  comm=_ag_comm([shard_bf[n] for n in ag_mid[:2]]))
    slab.update(zip(ag_mid[:2], got))
    ckT = jnp.transpose(c.reshape(B, S, LANES)[:, :, :8], (0, 2, 1)).reshape(B, N_PAIR, 2, S)
    ckT = jnp.pad(ckT, ((0, 0), (0, 0), (0, 6), (0, 0)))
    (y, co), got = _conv_fwd(u, gt, cw, row(conv_b), row(ln_g), row(ln_b), B, S,
                             comm=_ag_comm([shard_bf[n] for n in ag_mid[2:]]))
    slab.update(zip(ag_mid[2:], got))
    (o, fox_bias), got = _fox_fwd(q, k, v, cq, ckT, B, S, comm=_ag_comm([shard_bf[n] for n in ag_ffn]))
    slab.update(zip(ag_ffn, got))
    full = {n: slab[n] if by_col else slab[n].reshape(4 * r, c) for n, r, c, by_col in BIG}
    mn, km, vm = _mem_kv(mem2d, row(g_mem), full["w_mkv"], B)
    x1, hx, qm, om, x2, cat = _fwd_mid(x2d, co, o, km, vm, full["w_out"], full["w_mq"], full["w_mo"], row(g_x), B, S)
    hf, gu, act, dx3, loss_p, dg_final = _fwd_ffn(x2, tgt, full["w_gu"], full["w_down"], row(g_ffn), row(g_final), T)

    pos_sum = lambda gs, rcvs, ns: [_chip_sum(g, r, pos, "rs_chip_sum_" + n) for g, r, n in zip(gs, rcvs, ns)]
    fin_sum = lambda gs, rcvs, rcs, ns: [_final_sum(g, r, q3, pos, "rs_final_sum_" + n)
                                         for g, r, q3, n in zip(gs, rcvs, rcs, ns)]
    RH = {}
    dgu, dx2, dg_ffn = _bwd_ffn(dx3, gu, x2, full["w_gu"], full["w_down"], row(g_ffn), T)
    g_ffn_w = [_dw(hf, dgu, "dw_gu", FF_CHUNK, slabs=True), _dw(act, dx3, "dw_down", 512).reshape(4, D_FF // 4, D)]
    (dx1, dqm, dco, do, dd, dkm, dvm, dg_x), rcv_ffn = _bwd_mid(dx2, x1, qm, km, vm, o, full["w_mo"], full["w_mq"],
                                                                full["w_out"], row(g_x), B, S, comm=_sibling_comm(g_ffn_w))
    pb_ffn = pos_sum(g_ffn_w, rcv_ffn, ag_ffn)
    dkv, dg_mem = _mem_bwd(dkm, dvm, mem2d, full["w_mkv"], row(g_mem), B)
    g_mid_w = [_dw(mn, dkv, "dw_mkv", 512, slabs=True), _dw(cat, dx1, "dw_out", 512).reshape(4, 256, D),
               _dw(hx, dqm, "dw_mq", 512).reshape(4, 256, D), _dw(om, dx2, "dw_mo", 512).reshape(4, 256, D)]
    (dq, dk, dv, dc, dcq), got = _fox_bwd(q, k, v, do, fox_bias, dd, ckT, B, S,
                                          comm=_join(_ici_comm(pb_ffn), _sibling_comm(g_mid_w)))
    rc_ffn, rcv_mid = got[:len(pb_ffn)], got[len(pb_ffn):]
    RH.update(zip(ag_ffn, fin_sum(g_ffn_w, rcv_ffn, rc_ffn, ag_ffn)))
    pb_mid = pos_sum(g_mid_w, rcv_mid, ag_mid)
    dc8 = jnp.transpose(dc[:, :, :2, :].reshape(B, 8, S), (0, 2, 1)).reshape(T, 8)
    dc8 = dc8 + dcq.reshape(T, 8, HEAD_D)[:, :, 0]
    dzf, dbf = _fgate_bwd(jnp.pad(dc8, ((0, 0), (0, LANES - 8))), zf, B, S)
    (du, dgt, dcw, dvec), rc_mid = _conv_bwd(dco, y, u, gt, cw, row(ln_g), row(ln_b), B, S, comm=_ici_comm(pb_mid))
    RH.update(zip(ag_mid, fin_sum(g_mid_w, rcv_mid, rc_mid, ag_mid)))
    dz = jnp.concatenate([du, dgt, dq, dk, dv, dzf], axis=1)
    dw_in = _dw(h, dz, "dw_in", 384)[:, :D_IN]
    g_in_w = [jnp.transpose(dw_in.reshape(D, 4, D_IN // 4), (1, 0, 2))]
    rcv_in = _run_comm(_sibling_comm(g_in_w), "rs_sibling_in")
    (grad_x, dg_mix), rc_in = _bwd_in(dz, w_in_f, x2d, dx1, row(g_mix), T,
                                      comm=_ici_comm(pos_sum(g_in_w, rcv_in, ["w_in"])))
    RH.update(zip(["w_in"], fin_sum(g_in_w, rcv_in, rc_in, ["w_in"])))
    shared = dict(zip(big_names, _sibling_share([RH[n] for n in big_names])))
    G, DL, NM, NV = {}, {}, {}, {}
    for n in big_names:
        G[n], DL[n], NM[n], NV[n] = _adamw(shard2d(W[n]), shared[n], shard2d(M[n]), shard2d(V[n]), "adamw_" + n,
                                           shared[n].shape[0] // 2)

    small_g = {"g_mix": dg_mix, "b_f": dbf[:, :8], "conv_w": dcw, "conv_b": dvec[0], "ln_g": dvec[1], "ln_b": dvec[2],
               "g_x": dg_x, "g_mem": dg_mem, "g_ffn": dg_ffn, "g_final": dg_final, "loss": loss_p[:, :1]}
    sg = _small_allreduce(_pack_small(small_g, CONV_HALO * 4), "allreduce_small")
    shapes = {n: W[n].shape for n in names if n not in big_names}
    shapes["conv_w"] = (CONV_HALO, CONV_CH)
    shapes["loss"] = (1,)
    sgrads = _unpack_small(sg, shapes, CONV_HALO * 4)
    loss = sgrads.pop("loss")[0]
    sgrads["conv_w"] = lax.dynamic_slice(sgrads["conv_w"], (0, chip * LANES), (CONV_K, LANES)).reshape(W["conv_w"].shape)
    spack = lambda d: _pack_small({n: d[n] for n in sgrads}, CONV_HALO)
    _, sd, snm, snv = _adamw(spack(W), spack(sgrads), spack(M), spack(V), "adamw_small", 8)
    sshapes = {n: W[n].shape for n in sgrads}
    SD, SNM, SNV = (_unpack_small(a, sshapes, CONV_HALO) for a in (sd, snm, snv))

    def collect(bigs, smalls):
        return [bigs[n].reshape(W[n].shape) if n in big_names else smalls[n] for n in names]

    return (loss, grad_x.reshape(x.shape), *collect(G, sgrads), *collect(DL, SD), *collect(NM, SNM), *collect(NV, SNV))
```

```python
import functools
import math

import jax
import jax.numpy as jnp
from jax import lax
from jax.experimental import pallas as pl
from jax.experimental.pallas import tpu as pltpu

F32, BF16 = jnp.float32, jnp.bfloat16
HIGHEST = lax.Precision.HIGHEST
MESH = pl.DeviceIdType.MESH

D = 1024
CONV_CH = 512
CONV_K = 31
CONV_HALO = 32
FOX_W = 512
HEAD_D = 64
N_PAIR = 4
MEM_LEN = 256
MEM_HEADS = 4
MEM_HD = 256
D_FF = 2816
FF_CHUNK = 1408
D_IN = 2568
D_IN_PAD = 2688
OFF_F = 2560
EPS = 1e-6
LANES = 128

ADAM_LR, ADAM_B1, ADAM_B2, ADAM_EPS, ADAM_WD, ADAM_STEP = 0.001, 0.9, 0.999, 1e-08, 0.01, 10

VMEM_LIMIT = 60 * 1024 * 1024

BIG = (("w_out", 256, 1024, False), ("w_mq", 256, 1024, False), ("w_mkv", 1024, 512, True),
       ("w_mo", 256, 1024, False), ("w_gu", 1024, 1408, True), ("w_down", 704, 1024, False),
       ("w_in", 642, 1024, False))

ANY = pl.BlockSpec(memory_space=pl.ANY)


def _sig(x):
    return 1.0 / (1.0 + jnp.exp(-x))


def _dot(a, b):
    return jnp.dot(a, b, preferred_element_type=F32)


def _dot_nt(a, b):
    return lax.dot_general(a, b, (((1,), (1,)), ((), ())), preferred_element_type=F32)


def _dot_tn(a, b):
    return lax.dot_general(a, b, (((0,), (0,)), ((), ())), preferred_element_type=F32)


def _dot_hi(a, b):
    return jnp.dot(a, b, precision=HIGHEST, preferred_element_type=F32)


def _resident(a):
    nd = a.ndim
    return pl.BlockSpec(a.shape, lambda *_: (0,) * nd, pipeline_mode=pl.Buffered(1))


def _acc_spec(shape):
    nd = len(shape)
    return pl.BlockSpec(shape, lambda *_: (0,) * nd)


def _params(n_grid):
    return pltpu.CompilerParams(dimension_semantics=("arbitrary",) * n_grid, vmem_limit_bytes=VMEM_LIMIT)


def _sds(shape, dtype):
    return jax.ShapeDtypeStruct(shape, dtype)


def _rms(x):
    r = lax.rsqrt(jnp.mean(x * x, axis=-1, keepdims=True) + EPS)
    return r, x * r


def _rms_bwd(dy, xh, r, g):
    dxh = dy * g
    dx = r * (dxh - xh * jnp.mean(dxh * xh, axis=-1, keepdims=True))
    return dx, dy * xh


def _head_expand(rows, cols):
    hd = lax.broadcasted_iota(jnp.int32, (rows, cols), 1) // HEAD_D
    hr = lax.broadcasted_iota(jnp.int32, (rows, cols), 0)
    return (hd == hr).astype(F32)


def _fwd_in(x2, g_mix, w_int, w_ft, bf_pad, B, S, comm=None):
    T = B * S
    TB = min(512, S)
    nb = S // TB

    def body(x_ref, g_ref, w_ref, wf_ref, bf_ref, h_ref, u_ref, gt_ref, q_ref, k_ref, v_ref, zf_ref, c_ref, cq_ref,
             carry):
        j = pl.program_id(1)

        @pl.when(j == 0)
        def _():
            carry[...] = jnp.zeros_like(carry)

        _, xh = _rms(x_ref[...])
        h = (xh * g_ref[...]).astype(BF16)
        h_ref[...] = h
        u_ref[...] = _dot_nt(h, w_ref[0:512, :])
        gt_ref[...] = _dot_nt(h, w_ref[512:1024, :])
        q_ref[...] = _dot_nt(h, w_ref[1024:1536, :]).astype(BF16)
        k_ref[...] = _dot_nt(h, w_ref[1536:2048, :]).astype(BF16)
        v_ref[...] = _dot_nt(h, w_ref[2048:2560, :]).astype(BF16)
        zf = _dot_nt(h, wf_ref[...]) + bf_ref[...]
        zf_ref[...] = zf
        lane = lax.broadcasted_iota(jnp.int32, zf.shape, 1)
        logf = jnp.where(lane < 8, jnp.minimum(zf, 0.0) - jnp.log(1.0 + jnp.exp(-jnp.abs(zf))), 0.0)
        row = lax.broadcasted_iota(jnp.int32, (TB, TB), 0)
        col = lax.broadcasted_iota(jnp.int32, (TB, TB), 1)
        c = _dot_hi((row >= col).astype(F32), logf) + carry[0:1, :]
        carry[0:1, :] = c[TB - 1:TB, :]
        c_ref[...] = c
        cq_ref[...] = _dot_hi(c, _head_expand(LANES, FOX_W))

    tok = lambda w: pl.BlockSpec((TB, w), lambda b, j: (b * nb + j, 0))
    outs = [(D, BF16), (512, F32), (512, F32), (512, BF16), (512, BF16), (512, BF16), (LANES, F32),
            (LANES, F32), (FOX_W, F32)]
    return _call(
        body, comm, name="fwd_in", grid=(B, nb),
        in_specs=[tok(D), _resident(g_mix), _resident(w_int), _resident(w_ft), _resident(bf_pad)],
        out_specs=[tok(w) for w, _ in outs],
        out_shape=[_sds((T, w), dt) for w, dt in outs],
        scratch_shapes=[pltpu.VMEM((8, LANES), F32)],
        args=(x2, g_mix, w_int, w_ft, bf_pad))


def _head_sum(n):
    hc = lax.broadcasted_iota(jnp.int32, (n, n), 1) // HEAD_D
    hr = lax.broadcasted_iota(jnp.int32, (n, n), 0) // HEAD_D
    return (hc == hr).astype(F32)


def _layernorm_silu(y, lg, lb):
    mu = jnp.mean(y, axis=-1, keepdims=True)
    yc = y - mu
    rs = lax.rsqrt(jnp.mean(yc * yc, axis=-1, keepdims=True) + EPS)
    n = yc * rs
    l = n * lg + lb
    return rs, n, l


SUB = 8


def _shifted_copies(cat, sh, rows):
    for r in range(1, SUB):
        sh[r, 0:rows, :] = cat[r:r + rows, :]


def _tap(cat, sh, off, rows):
    r = off % SUB
    return cat[off:off + rows, :] if r == 0 else sh[r, off - r:off - r + rows, :]


def _conv_fwd(u, gt, cw, cb, lng, lnb, B, S, comm=None):
    T = B * S
    CB = min(256, S)
    nb = S // CB

    def body(u_ref, gt_ref, w_ref, cb_ref, lg_ref, lb_ref, y_ref, co_ref, acat, ash):
        j = pl.program_id(1)

        @pl.when(j == 0)
        def _():
            acat[0:CONV_HALO, :] = jnp.zeros((CONV_HALO, CONV_CH), F32)

        acat[CONV_HALO:CONV_HALO + CB, :] = u_ref[...] * _sig(gt_ref[...])
        _shifted_copies(acat, ash, CB + CONV_HALO - SUB)
        acc = jnp.zeros((CB, CONV_CH), F32) + cb_ref[...]
        for k in range(CONV_K):
            acc = acc + w_ref[k:k + 1, :] * _tap(acat, ash, CONV_HALO - (CONV_K - 1) + k, CB)
        acat[0:CONV_HALO, :] = acat[CB:CB + CONV_HALO, :]
        y_ref[...] = acc
        _, _, l = _layernorm_silu(acc, lg_ref[...], lb_ref[...])
        co_ref[...] = (l * _sig(l)).astype(BF16)

    tok = lambda w: pl.BlockSpec((CB, w), lambda b, j: (b * nb + j, 0))
    return _call(
        body, comm, name="conv_fwd", grid=(B, nb),
        in_specs=[tok(CONV_CH), tok(CONV_CH), _resident(cw), _resident(cb), _resident(lng), _resident(lnb)],
        out_specs=[tok(CONV_CH), tok(CONV_CH)],
        out_shape=[_sds((T, CONV_CH), F32), _sds((T, CONV_CH), BF16)],
        scratch_shapes=[pltpu.VMEM((CONV_HALO + CB, CONV_CH), F32),
                        pltpu.VMEM((SUB, CB + CONV_HALO - SUB, CONV_CH), F32)],
        args=(u, gt, cw, cb, lng, lnb))


def _fox_fwd(q, k, v, cq, ckT, B, S, comm=None):
    T = B * S
    TQ = min(256, S)
    nq = S // TQ
    scale = 1.0 / math.sqrt(HEAD_D)
    one_lane = (HEAD_D, 0)

    def body(q_ref, k_ref, v_ref, cq_ref, ck_ref, o_ref, lse_ref, s_scr, s_odd, m_scr, acc_scr):
        i = pl.program_id(2)
        lane = lax.broadcasted_iota(jnp.int32, (TQ, LANES), 1)
        lo = lane < HEAD_D
        qs = q_ref[...] * jnp.asarray(scale, BF16)
        zero = jnp.zeros_like(qs)
        qh = (jnp.where(lo, qs, zero), jnp.where(lo, zero, qs))
        cqv = cq_ref[...]
        cq_rep = tuple(jnp.broadcast_to(cqv[:, h * HEAD_D:h * HEAD_D + 1], (TQ, LANES)) for h in range(2))
        m_scr[...] = jnp.full(m_scr.shape, -1e30, F32)
        acc_scr[...] = jnp.zeros_like(acc_scr)
        row = lax.broadcasted_iota(jnp.int32, (TQ, TQ), 0)
        col = lax.broadcasted_iota(jnp.int32, (TQ, TQ), 1)
        wide = lambda x: jnp.concatenate([x, x], axis=1) if TQ == 2 * LANES else jnp.tile(x, (1, TQ // LANES))

        def scores(j, s_buf):
            kj = k_ref[pl.ds(pl.multiple_of(j * TQ, TQ), TQ), :]
            for h in range(2):
                s_buf[h] = _dot_nt(qh[h], kj)

        def softmax_step(j, s_buf, diagonal):
            start = pl.multiple_of(j * TQ, TQ)
            vj = v_ref[pl.ds(start, TQ), :]
            for h in range(2):
                ck = ck_ref[0, 0, h:h + 1, pl.ds(start, TQ)]

                def logits():
                    t = (s_buf[h] - ck) + wide(cq_rep[h])
                    return jnp.where(col <= row, t, -1e30) if diagonal else t

                m_old = m_scr[h]
                m_new = jnp.maximum(m_old, jnp.max(logits(), axis=-1, keepdims=True))
                alpha = jnp.exp(m_old - m_new)
                m_scr[h] = m_new
                p = jnp.exp(logits() - wide(m_new)).astype(BF16)
                vx = jnp.where(lane == one_lane[h], jnp.ones_like(vj), jnp.where(lo if h == 0 else ~lo, vj, jnp.zeros_like(vj)))
                acc_scr[h] = alpha * acc_scr[h] + _dot(p, vx)

        def two_blocks(jj, carry):
            j = 2 * jj
            scores(j + 1, s_odd)
            softmax_step(j, s_scr, False)
            scores(j + 2, s_scr)
            softmax_step(j + 1, s_odd, False)
            return carry

        scores(0, s_scr)
        lax.fori_loop(0, i // 2, two_blocks, 0)

        @pl.when(i % 2 == 0)
        def _():
            softmax_step(i, s_scr, True)

        @pl.when(i % 2 == 1)
        def _():
            scores(i, s_odd)
            softmax_step(i - 1, s_scr, False)
            softmax_step(i, s_odd, True)

        acc_a, acc_b = acc_scr[0], acc_scr[1]
        l_a = acc_a[:, one_lane[0]:one_lane[0] + 1]
        l_b = acc_b[:, one_lane[1]:one_lane[1] + 1]
        o_ref[...] = jnp.where(lo, acc_a / l_a, acc_b / l_b)
        lse_ref[...] = cqv - jnp.where(lo, m_scr[0] + jnp.log(l_a), m_scr[1] + jnp.log(l_b))

    qspec = pl.BlockSpec((TQ, LANES), lambda b, p, i: (b * nq + i, p))
    kspec = pl.BlockSpec((S, LANES), lambda b, p, i: (b, p))
    return _call(
        body, comm, name="fox_fwd", grid=(B, N_PAIR, nq),
        in_specs=[qspec, kspec, kspec, qspec, pl.BlockSpec((1, 1, 8, S), lambda b, p, i: (b, p, 0, 0))],
        out_specs=[qspec, qspec],
        out_shape=[_sds((T, FOX_W), F32), _sds((T, FOX_W), F32)],
        scratch_shapes=[pltpu.VMEM((2, TQ, TQ), F32), pltpu.VMEM((2, TQ, TQ), F32),
                        pltpu.VMEM((2, TQ, LANES), F32), pltpu.VMEM((2, TQ, LANES), F32)],
        args=(q, k, v, cq, ckT))


def _mem_kv(mem2, g_mem, w_mkv, B):
    def body(m_ref, g_ref, w_ref, mn_ref, km_ref, vm_ref):
        _, xh = _rms(m_ref[...])
        mn = (xh * g_ref[...]).astype(BF16)
        mn_ref[...] = mn
        for s in range(2):
            km_ref[:, 512 * s:512 * (s + 1)] = _dot(mn, w_ref[s]).astype(BF16)
            vm_ref[:, 512 * s:512 * (s + 1)] = _dot(mn, w_ref[2 + s]).astype(BF16)

    blk = pl.BlockSpec((MEM_LEN, D), lambda b: (b, 0))
    return pl.pallas_call(
        body, name="mem_kv", grid=(B,),
        in_specs=[blk, _resident(g_mem), _resident(w_mkv)],
        out_specs=[blk, blk, blk],
        out_shape=[_sds((B * MEM_LEN, D), BF16)] * 3,
        compiler_params=_params(1),
    )(mem2, g_mem, w_mkv)


def _mem_probs(qm, km):
    ps = []
    for h in range(MEM_HEADS):
        hs = slice(h * MEM_HD, (h + 1) * MEM_HD)
        lg = _dot_nt(qm[:, hs], km[:, hs]) * (1.0 / math.sqrt(MEM_HD))
        e = jnp.exp(lg - jnp.max(lg, axis=-1, keepdims=True))
        ps.append(e / jnp.sum(e, axis=-1, keepdims=True))
    return ps


def _fwd_mid(x2, co, o, km, vm, w_out, w_mq, w_mo, g_x, B, S):
    T = B * S
    TB = min(512, S)
    nb = S // TB

    def body(x_ref, co_ref, o_ref, km_ref, vm_ref, wo_ref, wq_ref, wm_ref, g_ref,
             x1_ref, hx_ref, qm_ref, om_ref, x2_ref, cat_ref):
        cat_ref[:, 0:CONV_CH] = co_ref[...]
        cat_ref[:, CONV_CH:D] = o_ref[...].astype(BF16)
        x1 = x_ref[...] + _dot(cat_ref[...], wo_ref[...])
        x1_ref[...] = x1
        _, xh = _rms(x1)
        hx = (xh * g_ref[...]).astype(BF16)
        hx_ref[...] = hx
        qm = _dot(hx, wq_ref[...]).astype(BF16)
        qm_ref[...] = qm
        ps = _mem_probs(qm, km_ref[...])
        vmv = vm_ref[...]
        for h in range(MEM_HEADS):
            hs = slice(h * MEM_HD, (h + 1) * MEM_HD)
            om_ref[:, hs] = _dot(ps[h].astype(BF16), vmv[:, hs]).astype(BF16)
        x2_ref[...] = x1 + _dot(om_ref[...], wm_ref[...])

    tok = lambda w: pl.BlockSpec((TB, w), lambda b, j: (b * nb + j, 0))
    memb = pl.BlockSpec((MEM_LEN, D), lambda b, j: (b, 0))
    outs = [(D, F32), (D, BF16), (D, BF16), (D, BF16), (D, F32), (D, BF16)]
    return pl.pallas_call(
        body, name="fwd_mid", grid=(B, nb),
        in_specs=[tok(D), tok(CONV_CH), tok(FOX_W), memb, memb, _resident(w_out), _resident(w_mq), _resident(w_mo),
                  _resident(g_x)],
        out_specs=[tok(w) for w, _ in outs],
        out_shape=[_sds((T, w), dt) for w, dt in outs],
        compiler_params=_params(2),
    )(x2, co, o, km, vm, w_out, w_mq, w_mo, g_x)


def _fwd_ffn(x2, tgt, w_gu, w_down, g_ffn, g_final, T):
    TB = min(256, T)
    nb = T // TB

    def body(x_ref, t_ref, wgu_ref, wd_ref, gf_ref, gl_ref, hf_ref, gu_ref, act_ref, dx3_ref, loss_ref, dgl_ref):
        i = pl.program_id(0)

        @pl.when(i == 0)
        def _():
            loss_ref[...] = jnp.zeros_like(loss_ref)
            dgl_ref[...] = jnp.zeros_like(dgl_ref)

        x2v = x_ref[...]
        _, xh = _rms(x2v)
        hf = (xh * gf_ref[...]).astype(BF16)
        hf_ref[...] = hf
        x3 = x2v
        for ch in range(D_FF // FF_CHUNK):
            c0 = ch * FF_CHUNK
            g = _dot(hf, wgu_ref[ch])
            u = _dot(hf, wgu_ref[2 + ch])
            gu_ref[:, c0:c0 + FF_CHUNK] = g
            gu_ref[:, D_FF + c0:D_FF + c0 + FF_CHUNK] = u
            act = (g * _sig(g) * u).astype(BF16)
            act_ref[:, c0:c0 + FF_CHUNK] = act
            x3 = x3 + _dot(act, wd_ref[c0:c0 + FF_CHUNK, :])
        r3, xh3 = _rms(x3)
        gl = gl_ref[...]
        e = xh3 * gl - t_ref[...]
        loss_ref[...] += jnp.sum(e * e) * (0.5 / D)
        dy = e * (1.0 / D)
        dx3, dgl = _rms_bwd(dy, xh3, r3, gl)
        dx3_ref[...] = dx3
        dgl_ref[...] += jnp.sum(dgl, axis=0, keepdims=True)

    tok = lambda w: pl.BlockSpec((TB, w), lambda i: (i, 0))
    return pl.pallas_call(
        body, name="fwd_ffn", grid=(nb,),
        in_specs=[tok(D), tok(D), _resident(w_gu), _resident(w_down), _resident(g_ffn), _resident(g_final)],
        out_specs=[tok(D), tok(2 * D_FF), tok(D_FF), tok(D), _acc_spec((1, LANES)), _acc_spec((1, D))],
        out_shape=[_sds((T, D), BF16), _sds((T, 2 * D_FF), F32), _sds((T, D_FF), BF16), _sds((T, D), F32),
                   _sds((1, LANES), F32), _sds((1, D), F32)],
        compiler_params=_params(1),
    )(x2, tgt, w_gu, w_down, g_ffn, g_final)


def _bwd_ffn(dx3, gu, x2, w_gu, w_down, g_ffn, T):
    TB = min(256, T)
    nb = T // TB

    def body(d_ref, gu_ref, x_ref, wgu_ref, wd_ref, gf_ref, dgu_ref, dx2_ref, dgf_ref):
        i = pl.program_id(0)

        @pl.when(i == 0)
        def _():
            dgf_ref[...] = jnp.zeros_like(dgf_ref)

        dx3v = d_ref[...]
        db = dx3v.astype(BF16)
        dhf = jnp.zeros((TB, D), F32)
        for ch in range(D_FF // FF_CHUNK):
            c0 = ch * FF_CHUNK
            dact = _dot_nt(db, wd_ref[c0:c0 + FF_CHUNK, :])
            g = gu_ref[:, c0:c0 + FF_CHUNK]
            u = gu_ref[:, D_FF + c0:D_FF + c0 + FF_CHUNK]
            sg = _sig(g)
            dg = (dact * u * sg * (1.0 + g * (1.0 - sg))).astype(BF16)
            du = (dact * g * sg).astype(BF16)
            dgu_ref[:, c0:c0 + FF_CHUNK] = dg
            dgu_ref[:, D_FF + c0:D_FF + c0 + FF_CHUNK] = du
            dhf = dhf + _dot_nt(dg, wgu_ref[ch]) + _dot_nt(du, wgu_ref[2 + ch])
        r2, xh2 = _rms(x_ref[...])
        dx, dg_tok = _rms_bwd(dhf, xh2, r2, gf_ref[...])
        dx2_ref[...] = dx3v + dx
        dgf_ref[...] += jnp.sum(dg_tok, axis=0, keepdims=True)

    tok = lambda w: pl.BlockSpec((TB, w), lambda i: (i, 0))
    return pl.pallas_call(
        body, name="bwd_ffn", grid=(nb,),
        in_specs=[tok(D), tok(2 * D_FF), tok(D), _resident(w_gu), _resident(w_down), _resident(g_ffn)],
        out_specs=[tok(2 * D_FF), tok(D), _acc_spec((1, D))],
        out_shape=[_sds((T, 2 * D_FF), BF16), _sds((T, D), F32), _sds((1, D), F32)],
        compiler_params=_params(1),
    )(dx3, gu, x2, w_gu, w_down, g_ffn)


def _bwd_mid(dx2, x1, qm, km, vm, o, w_mo, w_mq, w_out, g_x, B, S, comm=None):
    T = B * S
    TB = min(512, S)
    nb = S // TB
    inv = 1.0 / math.sqrt(MEM_HD)

    def body(d_ref, x1_ref, qm_ref, km_ref, vm_ref, o_ref, wm_ref, wq_ref, wo_ref, g_ref,
             dx1_ref, dqm_ref, dco_ref, do_ref, dd_ref, dkm_ref, dvm_ref, dgx_ref):
        b = pl.program_id(0)
        j = pl.program_id(1)

        @pl.when((b == 0) & (j == 0))
        def _():
            dgx_ref[...] = jnp.zeros_like(dgx_ref)

        @pl.when(j == 0)
        def _():
            dkm_ref[...] = jnp.zeros_like(dkm_ref)
            dvm_ref[...] = jnp.zeros_like(dvm_ref)

        dx2v = d_ref[...]
        dom = _dot_nt(dx2v.astype(BF16), wm_ref[...]).astype(BF16)
        qmv = qm_ref[...]
        kmv = km_ref[...]
        vmv = vm_ref[...]
        ps = _mem_probs(qmv, kmv)
        for h in range(MEM_HEADS):
            hs = slice(h * MEM_HD, (h + 1) * MEM_HD)
            p = ps[h]
            dp = _dot_nt(dom[:, hs], vmv[:, hs])
            ds = (p * (dp - jnp.sum(p * dp, axis=-1, keepdims=True))).astype(BF16)
            dqm_ref[:, hs] = (_dot(ds, kmv[:, hs]) * inv).astype(BF16)
            dkm_ref[:, hs] += _dot_tn(ds, qmv[:, hs]) * inv
            dvm_ref[:, hs] += _dot_tn(p.astype(BF16), dom[:, hs])
        dhx = _dot_nt(dqm_ref[...], wq_ref[...])
        r1, xh1 = _rms(x1_ref[...])
        dx, dg_tok = _rms_bwd(dhx, xh1, r1, g_ref[...])
        dx1 = dx2v + dx
        dx1_ref[...] = dx1
        dgx_ref[...] += jnp.sum(dg_tok, axis=0, keepdims=True)
        d1b = dx1.astype(BF16)
        dco_ref[...] = _dot_nt(d1b, wo_ref[0:CONV_CH, :])
        do = _dot_nt(d1b, wo_ref[CONV_CH:D, :])
        dob = do.astype(BF16)
        do_ref[...] = dob
        dd_ref[...] = _dot_hi(dob.astype(F32) * o_ref[...], _head_sum(FOX_W))

    tok = lambda w: pl.BlockSpec((TB, w), lambda b, j: (b * nb + j, 0))
    memb = pl.BlockSpec((MEM_LEN, D), lambda b, j: (b, 0))
    outs = [(D, F32), (D, BF16), (CONV_CH, F32), (FOX_W, BF16), (FOX_W, F32)]
    return _call(
        body, comm, name="bwd_mid", grid=(B, nb),
        in_specs=[tok(D), tok(D), tok(D), memb, memb, tok(FOX_W), _resident(w_mo), _resident(w_mq), _resident(w_out),
                  _resident(g_x)],
        out_specs=[tok(w) for w, _ in outs] + [memb, memb, _acc_spec((1, D))],
        out_shape=[_sds((T, w), dt) for w, dt in outs] + [_sds((B * MEM_LEN, D), F32)] * 2 + [_sds((1, D), F32)],
        scratch_shapes=[],
        args=(dx2, x1, qm, km, vm, o, w_mo, w_mq, w_out, g_x))


def _mem_bwd(dkm, dvm, mem2, w_mkv, g_mem, B):
    def body(dk_ref, dv_ref, m_ref, w_ref, g_ref, dkv_ref, dg_ref):
        b = pl.program_id(0)

        @pl.when(b == 0)
        def _():
            dg_ref[...] = jnp.zeros_like(dg_ref)

        dk = dk_ref[...].astype(BF16)
        dv = dv_ref[...].astype(BF16)
        dkv_ref[:, 0:D] = dk
        dkv_ref[:, D:2 * D] = dv
        dmn = jnp.zeros((MEM_LEN, D), F32)
        for s in range(2):
            dmn = dmn + _dot_nt(dk[:, 512 * s:512 * (s + 1)], w_ref[s]) + _dot_nt(dv[:, 512 * s:512 * (s + 1)], w_ref[2 + s])
        _, xh = _rms(m_ref[...])
        dg_ref[...] += jnp.sum(dmn * xh, axis=0, keepdims=True)

    blk = pl.BlockSpec((MEM_LEN, D), lambda b: (b, 0))
    return pl.pallas_call(
        body, name="mem_bwd", grid=(B,),
        in_specs=[blk, blk, blk, _resident(w_mkv), _resident(g_mem)],
        out_specs=[pl.BlockSpec((MEM_LEN, 2 * D), lambda b: (b, 0)), _acc_spec((1, D))],
        out_shape=[_sds((B * MEM_LEN, 2 * D), BF16), _sds((1, D), F32)],
        compiler_params=_params(1),
    )(dkm, dvm, mem2, w_mkv, g_mem)


def _fox_bwd(q, k, v, do, bias, dd, ckT, B, S, comm=None):
    T = B * S
    TK = min(256, S)
    nk = S // TK
    scale = 1.0 / math.sqrt(HEAD_D)

    def body(q_ref, k_ref, v_ref, do_ref, bias_ref, dd_ref, ck_ref, dq_ref, dk_ref, dv_ref, dc_ref, dcq_ref,
             dq_acc, rs_acc, s_scr, dp_scr, s_odd, dp_odd, dk_acc, dv_acc, dc_acc):
        j = pl.program_id(2)

        @pl.when(j == 0)
        def _():
            dq_acc[...] = jnp.zeros_like(dq_acc)
            rs_acc[...] = jnp.zeros_like(rs_acc)

        dk_acc[...] = jnp.zeros_like(dk_acc)
        dv_acc[...] = jnp.zeros_like(dv_acc)
        dc_acc[...] = jnp.zeros_like(dc_acc)
        lane = lax.broadcasted_iota(jnp.int32, (TK, LANES), 1)
        lo = lane < HEAD_D
        ks = k_ref[...] * jnp.asarray(scale, BF16)
        v2 = v_ref[...]
        zero = jnp.zeros_like(ks)
        kh = (jnp.where(lo, ks, zero), jnp.where(lo, zero, ks))
        vh = (jnp.where(lo, v2, zero), jnp.where(lo, zero, v2))
        kstart = pl.multiple_of(j * TK, TK)
        ckh = tuple(ck_ref[0, 0, h:h + 1, pl.ds(kstart, TK)] for h in range(2))
        row = lax.broadcasted_iota(jnp.int32, (TK, TK), 0)
        col = lax.broadcasted_iota(jnp.int32, (TK, TK), 1)
        wide = lambda x: jnp.concatenate([x, x], axis=1) if TK == 2 * LANES else jnp.tile(x, (1, TK // LANES))

        def scores(i, s_buf, dp_buf):
            start = pl.multiple_of(i * TK, TK)
            qi = q_ref[pl.ds(start, TK), :]
            doi = do_ref[pl.ds(start, TK), :]
            for h in range(2):
                s_buf[h] = _dot_nt(qi, kh[h])
                dp_buf[h] = _dot_nt(doi, vh[h])

        def grads(i, s_buf, dp_buf, diagonal):
            start = pl.multiple_of(i * TK, TK)
            qi = q_ref[pl.ds(start, TK), :]
            doi = do_ref[pl.ds(start, TK), :]
            bias2 = bias_ref[pl.ds(start, TK), :]
            dd2 = dd_ref[pl.ds(start, TK), :]
            for h in range(2):
                hc = slice(h * HEAD_D, h * HEAD_D + 1)
                bias = jnp.broadcast_to(bias2[:, hc], (TK, LANES))
                ddh = jnp.broadcast_to(dd2[:, hc], (TK, LANES))
                p = jnp.exp((s_buf[h] - ckh[h]) + wide(bias))
                if diagonal:
                    p = jnp.where(col <= row, p, 0.0)
                ds = p * (dp_buf[h] - wide(ddh))
                dc_acc[h, 0:1, :] += jnp.sum(ds, axis=0, keepdims=True)
                rs_acc[h, pl.ds(start, TK), :] += jnp.sum(ds, axis=1, keepdims=True)
                pb = p.astype(BF16)
                dsb = ds.astype(BF16)
                dv_acc[h] += _dot_tn(pb, doi)
                dk_acc[h] += _dot_tn(dsb, qi)
                dq_acc[pl.ds(start, TK), :] += _dot(dsb, kh[h])

        n_off = nk - 1 - j
        block = lambda t: jnp.where(t < n_off, j + 1 + t, j)

        def two_blocks(tt, carry):
            t = 2 * tt
            scores(block(t + 1), s_odd, dp_odd)
            grads(block(t), s_scr, dp_scr, False)
            scores(block(t + 2), s_scr, dp_scr)
            grads(block(t + 1), s_odd, dp_odd, False)
            return carry

        scores(block(0), s_scr, dp_scr)
        lax.fori_loop(0, n_off // 2, two_blocks, 0)

        @pl.when(n_off % 2 == 0)
        def _():
            grads(j, s_scr, dp_scr, True)

        @pl.when(n_off % 2 == 1)
        def _():
            scores(j, s_odd, dp_odd)
            grads(nk - 1, s_scr, dp_scr, False)
            grads(j, s_odd, dp_odd, True)

        dk_ref[...] = (jnp.where(lo, dk_acc[0], dk_acc[1]) * scale).astype(BF16)
        dv_ref[...] = jnp.where(lo, dv_acc[0], dv_acc[1]).astype(BF16)
        sub = lax.broadcasted_iota(jnp.int32, (8, TK), 0)
        dca = dc_acc[0, 0:1, :]
        dcb = dc_acc[1, 0:1, :]
        dc_ref[0, 0] = jnp.where(sub == 0, -dca, jnp.where(sub == 1, -dcb, 0.0))

        @pl.when(j == nk - 1)
        def _():
            dq_ref[...] = dq_acc[...].astype(BF16)
            lo_s = lax.broadcasted_iota(jnp.int32, (S, LANES), 1) < HEAD_D
            dcq_ref[...] = jnp.where(lo_s, rs_acc[0], rs_acc[1])

    full = pl.BlockSpec((S, LANES), lambda b, p, j: (b, p))
    blk = pl.BlockSpec((TK, LANES), lambda b, p, j: (b * nk + j, p))
    return _call(
        body, comm, name="fox_bwd", grid=(B, N_PAIR, nk),
        in_specs=[full, blk, blk, full, full, full, pl.BlockSpec((1, 1, 8, S), lambda b, p, j: (b, p, 0, 0))],
        out_specs=[full, blk, blk, pl.BlockSpec((1, 1, 8, TK), lambda b, p, j: (b, p, 0, j)), full],
        out_shape=[_sds((T, FOX_W), BF16), _sds((T, FOX_W), BF16), _sds((T, FOX_W), BF16),
                   _sds((B, N_PAIR, 8, S), F32), _sds((T, FOX_W), F32)],
        scratch_shapes=[pltpu.VMEM((S, LANES), F32), pltpu.VMEM((2, S, 1), F32),
                        pltpu.VMEM((2, TK, TK), F32), pltpu.VMEM((2, TK, TK), F32),
                        pltpu.VMEM((2, TK, TK), F32), pltpu.VMEM((2, TK, TK), F32),
                        pltpu.VMEM((2, TK, LANES), F32), pltpu.VMEM((2, TK, LANES), F32), pltpu.VMEM((2, 8, TK), F32)],
        args=(q, k, v, do, bias, dd, ckT))


def _fgate_bwd(dc8, zf, B, S):
    T = B * S
    TB = min(512, S)
    nb = S // TB

    def body(dc_ref, zf_ref, dzf_ref, dbf_ref, carry):
        b = pl.program_id(0)
        j = pl.program_id(1)

        @pl.when((b == 0) & (j == 0))
        def _():
            dbf_ref[...] = jnp.zeros_like(dbf_ref)

        @pl.when(j == 0)
        def _():
            carry[...] = jnp.zeros_like(carry)

        dc = dc_ref[...]
        row = lax.broadcasted_iota(jnp.int32, (TB, TB), 0)
        col = lax.broadcasted_iota(jnp.int32, (TB, TB), 1)
        dlogf = _dot_hi((col >= row).astype(F32), dc) + carry[0:1, :]
        carry[0:1, :] = dlogf[0:1, :]
        lane = lax.broadcasted_iota(jnp.int32, dc.shape, 1)
        dzf = jnp.where(lane < 8, dlogf * _sig(-zf_ref[...]), 0.0)
        dzf_ref[...] = dzf.astype(BF16)
        dbf_ref[...] += jnp.sum(dzf, axis=0, keepdims=True)

    tok = pl.BlockSpec((TB, LANES), lambda b, j: (b * nb + (nb - 1 - j), 0))
    return pl.pallas_call(
        body, name="fgate_bwd", grid=(B, nb),
        in_specs=[tok, tok],
        out_specs=[tok, _acc_spec((1, LANES))],
        out_shape=[_sds((T, LANES), BF16), _sds((1, LANES), F32)],
        scratch_shapes=[pltpu.VMEM((8, LANES), F32)],
        compiler_params=_params(2),
    )(dc8, zf)


def _conv_bwd(dco, y, u, gt, cw, lng, lnb, B, S, comm=None):
    T = B * S
    CB = min(256, S)
    nb = S // CB
    hb = CB // CONV_HALO

    def body(dco_ref, y_ref, u_ref, gt_ref, up_ref, gp_ref, w_ref, lg_ref, lb_ref,
             du_ref, dgt_ref, dw_ref, vec_ref, acat, dycat, ash, dysh):
        b = pl.program_id(0)
        j = pl.program_id(1)
        jr = nb - 1 - j

        @pl.when((b == 0) & (j == 0))
        def _():
            dw_ref[...] = jnp.zeros_like(dw_ref)
            vec_ref[...] = jnp.zeros_like(vec_ref)

        @pl.when(j == 0)
        def _():
            dycat[CB:CB + CONV_HALO, :] = jnp.zeros((CONV_HALO, CONV_CH), F32)

        lg = lg_ref[...]
        rs, n, l = _layernorm_silu(y_ref[...], lg, lb_ref[...])
        sg = _sig(l)
        dl = dco_ref[...] * (sg * (1.0 + l * (1.0 - sg)))
        dn = dl * lg
        dy = rs * (dn - jnp.mean(dn, axis=-1, keepdims=True) - n * jnp.mean(dn * n, axis=-1, keepdims=True))
        vec_ref[0:1, :] += jnp.sum(dy, axis=0, keepdims=True)
        vec_ref[1:2, :] += jnp.sum(dl * n, axis=0, keepdims=True)
        vec_ref[2:3, :] += jnp.sum(dl, axis=0, keepdims=True)
        dycat[0:CB, :] = dy
        uv = u_ref[...]
        sgt = _sig(gt_ref[...])
        acat[0:CONV_HALO, :] = jnp.where(jr > 0, up_ref[...] * _sig(gp_ref[...]), 0.0)
        acat[CONV_HALO:CONV_HALO + CB, :] = uv * sgt
        _shifted_copies(acat, ash, CB + CONV_HALO - SUB)
        _shifted_copies(dycat, dysh, CB + CONV_HALO - SUB)
        da = jnp.zeros((CB, CONV_CH), F32)
        for k in range(CONV_K):
            da = da + w_ref[k:k + 1, :] * _tap(dycat, dysh, CONV_K - 1 - k, CB)
            dw_ref[k:k + 1, :] += jnp.sum(dy * _tap(acat, ash, CONV_HALO - (CONV_K - 1) + k, CB), axis=0, keepdims=True)
        dycat[CB:CB + CONV_HALO, :] = dycat[0:CONV_HALO, :]
        du_ref[...] = (da * sgt).astype(BF16)
        dgt_ref[...] = (da * uv * sgt * (1.0 - sgt)).astype(BF16)

    tok = lambda w: pl.BlockSpec((CB, w), lambda b, j: (b * nb + (nb - 1 - j), 0))
    prev = pl.BlockSpec((CONV_HALO, CONV_CH), lambda b, j: (jnp.maximum((b * nb + (nb - 1 - j)) * hb - 1, 0), 0))
    return _call(
        body, comm, name="conv_bwd", grid=(B, nb),
        in_specs=[tok(CONV_CH), tok(CONV_CH), tok(CONV_CH), tok(CONV_CH), prev, prev, _resident(cw), _resident(lng),
                  _resident(lnb)],
        out_specs=[tok(CONV_CH), tok(CONV_CH), _acc_spec((CONV_HALO, CONV_CH)), _acc_spec((8, CONV_CH))],
        out_shape=[_sds((T, CONV_CH), BF16), _sds((T, CONV_CH), BF16), _sds((CONV_HALO, CONV_CH), F32),
                   _sds((8, CONV_CH), F32)],
        scratch_shapes=[pltpu.VMEM((CONV_HALO + CB, CONV_CH), F32), pltpu.VMEM((CB + CONV_HALO, CONV_CH), F32),
                        pltpu.VMEM((SUB, CB + CONV_HALO - SUB, CONV_CH), F32),
                        pltpu.VMEM((SUB, CB + CONV_HALO - SUB, CONV_CH), F32)],
        args=(dco, y, u, gt, u, gt, cw, lng, lnb))


def _bwd_in(dz, w_int, w_ft, x2, dx1, g_mix, T, comm=None):
    TB = min(512, T)
    nb = T // TB

    def body(dz_ref, w_ref, wf_ref, x_ref, d1_ref, g_ref, gx_ref, dg_ref):
        i = pl.program_id(0)

        @pl.when(i == 0)
        def _():
            dg_ref[...] = jnp.zeros_like(dg_ref)

        dh = _dot(dz_ref[:, 0:OFF_F], w_ref[0:OFF_F, :]) + _dot(dz_ref[:, OFF_F:D_IN_PAD], wf_ref[...])
        r0, xh0 = _rms(x_ref[...])
        dx, dg_tok = _rms_bwd(dh, xh0, r0, g_ref[...])
        gx_ref[...] = d1_ref[...] + dx
        dg_ref[...] += jnp.sum(dg_tok, axis=0, keepdims=True)

    tok = lambda w: pl.BlockSpec((TB, w), lambda i: (i, 0))
    return _call(
        body, comm, name="bwd_in", grid=(nb,),
        in_specs=[tok(D_IN_PAD), _resident(w_int), _resident(w_ft), tok(D), tok(D), _resident(g_mix)],
        out_specs=[tok(D), _acc_spec((1, D))],
        out_shape=[_sds((T, D), F32), _sds((1, D), F32)],
        scratch_shapes=[],
        args=(dz, w_int, w_ft, x2, dx1, g_mix))


def _dw(a, b, name, tn, slabs=False, tk=None):
    T, K = a.shape
    N = b.shape[1]
    tk = tk or (K if K <= 1024 else K // 2)
    tt = min(512, T)
    nt = T // tt

    def body(a_ref, b_ref, o_ref, acc):
        t = pl.program_id(2)

        @pl.when(t == 0)
        def _():
            acc[...] = jnp.zeros_like(acc)

        acc[...] += _dot_tn(a_ref[...].astype(BF16), b_ref[...].astype(BF16))

        @pl.when(t == nt - 1)
        def _():
            o_ref[...] = acc[...]

    return pl.pallas_call(
        body, name=name, grid=(K // tk, N // tn, nt),
        in_specs=[pl.BlockSpec((tt, tk), lambda i, j, t: (t, i)), pl.BlockSpec((tt, tn), lambda i, j, t: (t, j))],
        out_specs=(pl.BlockSpec((None, tk, tn), lambda i, j, t: (j, i, 0)) if slabs
                   else pl.BlockSpec((tk, tn), lambda i, j, t: (i, j))),
        out_shape=_sds((N // tn, K, tn) if slabs else (K, N), F32),
        scratch_shapes=[pltpu.VMEM((tk, tn), F32)],
        compiler_params=_params(3),
    )(a, b)


def _pos():
    return lax.axis_index("x"), lax.axis_index("y"), lax.axis_index("c")


def _remote(src, dst, ssem, rsem, to):
    return pltpu.make_async_remote_copy(src_ref=src, dst_ref=dst, send_sem=ssem, recv_sem=rsem, device_id=to,
                                        device_id_type=MESH)


def _split_axis(shape):
    return 0 if shape[0] % 32 == 0 else 1


def _half_shape(shape, parts=2):
    return (shape[0] // parts, shape[1]) if _split_axis(shape) == 0 else (shape[0], shape[1] // parts)


def _half(shape, c):
    R, C = shape
    if _split_axis(shape) == 0:
        return (pl.ds(pl.multiple_of(c * (R // 2), 16), R // 2), slice(None))
    return (slice(None), pl.ds(pl.multiple_of(c * (C // 2), LANES), C // 2))


def _half_block(shape, parts, lead, which):
    blk = _half_shape(shape, parts)
    idx = (which, 0) if _split_axis(shape) == 0 else (0, which)
    return blk, tuple(lead) + idx


class _Comm:
    def __init__(self, ins, out_shapes, sems, start, finish):
        self.ins, self.out_shapes, self.sems, self.start, self.finish = list(ins), list(out_shapes), list(sems), start, finish


def _ag_comm(shards):
    n = len(shards)

    def parts(ins, outs, sems):
        send_sems, recv_sems, local_sems = sems
        x, y, c = _pos()
        me, sib = (x, y, c), (x, y, 1 - c)
        chips = [(1 - x, y), (x, 1 - y), (1 - x, 1 - y)]

        def rows(w, px, py, pc):
            return outs[w].at[(2 * px + py,) + _half(shards[w].shape, pc)]

        def copy(w, k, block, to, src=None):
            return _remote(rows(w, *block) if src is None else src, rows(w, *block), send_sems.at[w, k],
                           recv_sems.at[w, k], to)

        mine, first = [], []
        for w in range(n):
            src = ins[w].at[_half(shards[w].shape, c)]
            mine.append(pltpu.make_async_copy(src, rows(w, *me), local_sems.at[w]))
            first += [copy(w, 0, me, sib, src=src)] + [copy(w, 1 + j, me, (*chip, c), src=src) for j, chip in enumerate(chips)]
        return c, me, sib, chips, copy, mine, first

    def start(ins, outs, sems):
        _, _, _, _, _, mine, first = parts(ins, outs, sems)
        for cp in mine + first:
            cp.start()

    def finish(ins, outs, sems):
        c, me, sib, chips, copy, mine, first = parts(ins, outs, sems)
        passed = []
        for w in range(n):
            for j, chip in enumerate(chips):
                copy(w, 1 + j, (*chip, c), me).wait_recv()
                passed.append(copy(w, 4 + j, (*chip, c), sib))
                passed[-1].start()
        for w in range(n):
            copy(w, 0, sib, me).wait_recv()
            for j, chip in enumerate(chips):
                copy(w, 4 + j, (*chip, 1 - c), me).wait_recv()
        for cp in first + passed:
            cp.wait_send()
        for cp in mine:
            cp.wait()

    D7 = pltpu.SemaphoreType.DMA((n, 7))
    return _Comm(shards, [_sds((4,) + s.shape, s.dtype) for s in shards], [D7, D7, pltpu.SemaphoreType.DMA((n,))],
                 start, finish)


def _sibling_comm(gs):
    n = len(gs)

    def copies(ins, outs, sems):
        send_sems, recv_sems = sems
        x, y, c = _pos()
        return [_remote(ins[w].at[(s,) + _half(gs[w].shape[1:], 1 - c)], outs[w].at[s], send_sems.at[w, s],
                        recv_sems.at[w, s], (x, y, 1 - c)) for w in range(n) for s in range(4)]

    def start(ins, outs, sems):
        for cp in copies(ins, outs, sems):
            cp.start()

    def finish(ins, outs, sems):
        for cp in copies(ins, outs, sems):
            cp.wait()

    D4 = pltpu.SemaphoreType.DMA((n, 4))
    return _Comm(gs, [_sds((4,) + _half_shape(g.shape[1:]), F32) for g in gs], [D4, D4], start, finish)


def _ici_comm(pbs):
    n = len(pbs)

    def copies(ins, outs, sems):
        send_sems, recv_sems = sems
        x, y, c = _pos()
        return [_remote(ins[w].at[2 * tx + ty], outs[w].at[j], send_sems.at[w, j], recv_sems.at[w, j], (tx, ty, c))
                for w in range(n) for j, (tx, ty) in enumerate([(1 - x, y), (x, 1 - y), (1 - x, 1 - y)])]

    def start(ins, outs, sems):
        for cp in copies(ins, outs, sems):
            cp.start()

    def finish(ins, outs, sems):
        for cp in copies(ins, outs, sems):
            cp.wait()

    D3 = pltpu.SemaphoreType.DMA((n, 3))
    return _Comm(pbs, [_sds((3,) + p.shape[1:], BF16) for p in pbs], [D3, D3], start, finish)


def _join(*comms):
    counts = [(len(c.ins), len(c.out_shapes), len(c.sems)) for c in comms]

    def each(which):
        def run(ins, outs, sems):
            i = o = k = 0
            for c, (ni, no, nk) in zip(comms, counts):
                getattr(c, which)(ins[i:i + ni], outs[o:o + no], sems[k:k + nk])
                i, o, k = i + ni, o + no, k + nk
        return run

    return _Comm(sum((c.ins for c in comms), []), sum((c.out_shapes for c in comms), []),
                 sum((c.sems for c in comms), []), each("start"), each("finish"))


def _run_comm(comm, name):
    ni, no = len(comm.ins), len(comm.out_shapes)

    def body(*refs):
        ins, outs, sems = refs[:ni], refs[ni:ni + no], refs[ni + no:]
        comm.start(ins, outs, sems)
        comm.finish(ins, outs, sems)

    return pl.pallas_call(body, name=name, out_shape=comm.out_shapes, in_specs=[ANY] * ni, out_specs=[ANY] * no,
                          scratch_shapes=comm.sems)(*comm.ins)


def _call(body, comm, *, name, grid, in_specs, out_specs, out_shape, scratch_shapes, args):
    n_grid = len(grid)
    if comm is None:
        res = pl.pallas_call(body, name=name, grid=grid, in_specs=in_specs, out_specs=out_specs, out_shape=out_shape,
                             scratch_shapes=scratch_shapes, compiler_params=_params(n_grid))(*args)
        return list(res), []
    n_in, n_out, n_scr = len(in_specs), len(out_specs), len(scratch_shapes)
    ni, no = len(comm.ins), len(comm.out_shapes)

    def carried(*refs):
        ins, refs = refs[:n_in], refs[n_in:]
        cins, refs = refs[:ni], refs[ni:]
        outs, refs = refs[:n_out], refs[n_out:]
        couts, refs = refs[:no], refs[no:]
        scr, csems = refs[:n_scr], refs[n_scr:]
        ids = [pl.program_id(ax) for ax in range(n_grid)]
        first = functools.reduce(jnp.logical_and, [i == 0 for i in ids])
        last = functools.reduce(jnp.logical_and, [i == g - 1 for i, g in zip(ids, grid)])

        @pl.when(first)
        def _():
            comm.start(cins, couts, csems)

        body(*ins, *outs, *scr)

        @pl.when(last)
        def _():
            comm.finish(cins, couts, csems)

    res = pl.pallas_call(
        carried, name=name, grid=grid, in_specs=list(in_specs) + [ANY] * ni, out_specs=list(out_specs) + [ANY] * no,
        out_shape=list(out_shape) + comm.out_shapes, scratch_shapes=list(scratch_shapes) + comm.sems,
        compiler_params=_params(n_grid))(*args, *comm.ins)
    return list(res[:n_out]), list(res[n_out:])


def _sibling_share(gs):
    n = len(gs)

    def body(*refs):
        outs = refs[n:2 * n]
        send_sems, recv_sems = refs[2 * n:]
        x, y, c = _pos()
        cps = []
        for w in range(n):
            mine = outs[w].at[_half(gs[w].shape, c)]
            cps.append(_remote(mine, mine, send_sems.at[w], recv_sems.at[w], (x, y, 1 - c)))
            cps[-1].start()
        for cp in cps:
            cp.wait()

    return pl.pallas_call(
        body, name="rs_share", out_shape=[_sds(g.shape, F32) for g in gs],
        in_specs=[ANY] * n, out_specs=[ANY] * n, input_output_aliases={w: w for w in range(n)},
        scratch_shapes=[pltpu.SemaphoreType.DMA((n,)), pltpu.SemaphoreType.DMA((n,))],
    )(*gs)


def _small_allreduce(v, name):
    P = v.shape[0]
    vm = pl.BlockSpec(memory_space=pltpu.VMEM)

    def body(v_ref, o_ref, gath, send_sems, recv_sems):
        x, y, c = _pos()
        me = 4 * x + 2 * y + c
        gath[me] = v_ref[...]
        cps = []
        for r in range(1, 8):
            tx = (1 - x) if r & 4 else x
            ty = (1 - y) if r & 2 else y
            tc = (1 - c) if r & 1 else c
            cps.append(_remote(v_ref, gath.at[me], send_sems.at[r - 1], recv_sems.at[r - 1], (tx, ty, tc)))
            cps[-1].start()
        for cp in cps:
            cp.wait()
        acc = gath[0]
        for d in range(1, 8):
            acc = acc + gath[d]
        o_ref[...] = acc

    return pl.pallas_call(
        body, name=name, out_shape=_sds((P, LANES), F32), in_specs=[vm], out_specs=vm,
        scratch_shapes=[pltpu.VMEM((8, P, LANES), F32), pltpu.SemaphoreType.DMA((7,)), pltpu.SemaphoreType.DMA((7,))],
    )(v)


def _chip_sum(g, rcv, pos, name):
    shard = g.shape[1:]
    hs = _half_shape(shard)

    def body(pos_ref, g_ref, r_ref, o_ref):
        o_ref[...] = (g_ref[...] + r_ref[...]).astype(BF16)

    return pl.pallas_call(
        body, name=name, out_shape=_sds((4,) + hs, BF16),
        grid_spec=pltpu.PrefetchScalarGridSpec(
            num_scalar_prefetch=1, grid=(4,),
            in_specs=[pl.BlockSpec((1,) + hs, lambda s, pos: _half_block(shard, 2, (s,), pos[0])[1]),
                      pl.BlockSpec((1,) + hs, lambda s, pos: (s, 0, 0))],
            out_specs=pl.BlockSpec((1,) + hs, lambda s, pos: (s, 0, 0))),
        compiler_params=_params(1),
    )(pos, g, rcv)


def _final_sum(g, rcv, rc, pos, name):
    shard = g.shape[1:]
    qs = _half_shape(shard, 4)

    def body(pos_ref, g_ref, r_ref, rc_ref, o_ref):
        acc = g_ref[0] + r_ref[0]
        for j in range(3):
            acc = acc + rc_ref[j].astype(F32)
        o_ref[...] = acc

    return pl.pallas_call(
        body, name=name, out_shape=_sds(shard, F32),
        grid_spec=pltpu.PrefetchScalarGridSpec(
            num_scalar_prefetch=1, grid=(2,),
            in_specs=[pl.BlockSpec((1,) + qs, lambda i, pos: _half_block(shard, 4, (pos[1],), pos[0] * 2 + i)[1]),
                      pl.BlockSpec((1,) + qs, lambda i, pos: _half_block(shard, 4, (pos[1],), i)[1]),
                      pl.BlockSpec((3,) + qs, lambda i, pos: _half_block(shard, 4, (0,), i)[1])],
            out_specs=pl.BlockSpec(qs, lambda i, pos: _half_block(shard, 4, (), pos[0] * 2 + i)[1])),
        compiler_params=_params(1),
    )(pos, g, rcv, rc)


def _adamw_math(w, g, m, v):
    m = ADAM_B1 * m + (1.0 - ADAM_B1) * g
    v = ADAM_B2 * v + (1.0 - ADAM_B2) * (g * g)
    m_hat = m / (1.0 - ADAM_B1 ** ADAM_STEP)
    v_hat = v / (1.0 - ADAM_B2 ** ADAM_STEP)
    delta = -ADAM_LR * (m_hat / (jnp.sqrt(v_hat) + ADAM_EPS) + ADAM_WD * w)
    return delta, m, v


def _adamw(w, g, m, v, name, blk_shape):
    R, C = w.shape

    def body(w_ref, g_ref, m_ref, v_ref, go_ref, d_ref, nm_ref, nv_ref):
        g = g_ref[...]
        d, nm, nv = _adamw_math(w_ref[...], g, m_ref[...], v_ref[...])
        go_ref[...] = g
        d_ref[...] = d
        nm_ref[...] = nm
        nv_ref[...] = nv

    blk = pl.BlockSpec(blk_shape, lambda i, j: (i, j))
    return pl.pallas_call(
        body, name=name, grid=(R // blk_shape[0], C // blk_shape[1]), in_specs=[blk] * 4, out_specs=[blk] * 4,
        out_shape=[_sds((R, C), F32)] * 4, compiler_params=_params(2),
    )(w, g, m, v)


SMALL = (("g_mix", 8), ("b_f", 8), ("conv_w", None), ("conv_b", 8), ("ln_g", 8), ("ln_b", 8), ("g_x", 8), ("g_mem", 8),
         ("g_ffn", 8), ("g_final", 8), ("loss", 8))


def _pack_small(parts, conv_rows):
    rows = []
    for name, n in SMALL:
        if name not in parts:
            continue
        n = conv_rows if n is None else n
        flat = parts[name].reshape(-1).astype(F32)
        flat = jnp.pad(flat, (0, n * LANES - flat.shape[0]))
        rows.append(flat.reshape(n, LANES))
    return jnp.concatenate(rows, axis=0)


def _unpack_small(p, shapes, conv_rows):
    out, off = {}, 0
    for name, n in SMALL:
        if name not in shapes:
            continue
        n = conv_rows if n is None else n
        size = math.prod(shapes[name])
        out[name] = p[off:off + n].reshape(-1)[:size].reshape(shapes[name])
        off += n
    return out


def kernel(x, mem, g_mix, w_in, b_f, conv_w, conv_b, ln_g, ln_b, w_out, g_x, g_mem, w_mq, w_mkv, w_mo, g_ffn, w_gu, w_down, g_final, loss_target, m_g_mix, m_w_in, m_b_f, m_conv_w, m_conv_b, m_ln_g, m_ln_b, m_w_out, m_g_x, m_g_mem, m_w_mq, m_w_mkv, m_w_mo, m_g_ffn, m_w_gu, m_w_down, m_g_final, v_g_mix, v_w_in, v_b_f, v_conv_w, v_conv_b, v_ln_g, v_ln_b, v_w_out, v_g_x, v_g_mem, v_w_mq, v_w_mkv, v_w_mo, v_g_ffn, v_w_gu, v_w_down, v_g_final):
    names = ["g_mix", "w_in", "b_f", "conv_w", "conv_b", "ln_g", "ln_b", "w_out", "g_x", "g_mem", "w_mq", "w_mkv",
             "w_mo", "g_ffn", "w_gu", "w_down", "g_final"]
    W = dict(zip(names, [g_mix, w_in, b_f, conv_w, conv_b, ln_g, ln_b, w_out, g_x, g_mem, w_mq, w_mkv, w_mo, g_ffn,
                         w_gu, w_down, g_final]))
    M = dict(zip(names, [m_g_mix, m_w_in, m_b_f, m_conv_w, m_conv_b, m_ln_g, m_ln_b, m_w_out, m_g_x, m_g_mem, m_w_mq,
                         m_w_mkv, m_w_mo, m_g_ffn, m_w_gu, m_w_down, m_g_final]))
    V = dict(zip(names, [v_g_mix, v_w_in, v_b_f, v_conv_w, v_conv_b, v_ln_g, v_ln_b, v_w_out, v_g_x, v_g_mem, v_w_mq,
                         v_w_mkv, v_w_mo, v_g_ffn, v_w_gu, v_w_down, v_g_final]))
    big_names = [n for n, _, _, _ in BIG]
    B, S, _ = x.shape
    T = B * S
    mx, my, mc = _pos()
    chip = 2 * mx + my
    pos = jnp.stack([mc, chip]).astype(jnp.int32)

    shard2d = lambda a: a.reshape(a.shape[-2], a.shape[-1])
    big2d = lambda d, n: shard2d(d[n]).T if n == "w_in" else shard2d(d[n])
    shard_bf = {n: big2d(W, n).astype(BF16) for n in big_names}
    ag_mid = ["w_mkv", "w_out", "w_mq", "w_mo"]
    ag_ffn = ["w_gu", "w_down"]
    slab = {"w_in": _run_comm(_ag_comm([shard_bf["w_in"]]), "ag_w_in")[0]}
    w_int = slab["w_in"].reshape(D_IN, D)
    w_ft = jnp.pad(w_int[OFF_F:D_IN], ((0, D_IN_PAD - D_IN), (0, 0)))
    cw_mine = jnp.pad(shard2d(conv_w), ((0, 1), (0, 0)))
    cw_slot = lax.dynamic_update_slice(jnp.zeros((CONV_HALO, CONV_CH), F32), cw_mine, (0, chip * LANES))
    cw = _small_allreduce(cw_slot.reshape(CONV_HALO * 4, LANES) * 0.5, "gather_conv_w").reshape(CONV_HALO, CONV_CH)

    row = lambda a: a.reshape(1, -1)
    bf_pad = jnp.pad(row(b_f), ((0, 0), (0, LANES - 8)))
    x2d = x.reshape(T, D)
    mem2d = mem.reshape(B * MEM_LEN, D)
    tgt = loss_target.reshape(T, D)

    (h, u, gt, q, k, v, zf, c, cq), got = _fwd_in(x2d, row(g_mix), w_int, w_ft, bf_pad, B, S,
                                                  comm=_ag_comm([shard_bf[n] for n in ag_mid[:2]]))
    slab.update(zip(ag_mid[:2], got))
    ckT = jnp.transpose(c.reshape(B, S, LANES)[:, :, :8], (0, 2, 1)).reshape(B, N_PAIR, 2, S)
    ckT = jnp.pad(ckT, ((0, 0), (0, 0), (0, 6), (0, 0)))
    (y, co), got = _conv_fwd(u, gt, cw, row(conv_b), row(ln_g), row(ln_b), B, S,
                             comm=_ag_comm([shard_bf[n] for n in ag_mid[2:]]))
    slab.update(zip(ag_mid[2:], got))
    (o, fox_bias), got = _fox_fwd(q, k, v, cq, ckT, B, S, comm=_ag_comm([shard_bf[n] for n in ag_ffn]))
    slab.update(zip(ag_ffn, got))
    full = {n: slab[n] if by_col else slab[n].reshape(4 * r, c) for n, r, c, by_col in BIG}
    mn, km, vm = _mem_kv(mem2d, row(g_mem), full["w_mkv"], B)
    x1, hx, qm, om, x2, cat = _fwd_mid(x2d, co, o, km, vm, full["w_out"], full["w_mq"], full["w_mo"], row(g_x), B, S)
    hf, gu, act, dx3, loss_p, dg_final = _fwd_ffn(x2, tgt, full["w_gu"], full["w_down"], row(g_ffn), row(g_final), T)

    pos_sum = lambda gs, rcvs, ns: [_chip_sum(g, r, pos, "rs_chip_sum_" + n) for g, r, n in zip(gs, rcvs, ns)]
    fin_sum = lambda gs, rcvs, rcs, ns: [_final_sum(g, r, q3, pos, "rs_final_sum_" + n)
                                         for g, r, q3, n in zip(gs, rcvs, rcs, ns)]
    RH = {}
    dgu, dx2, dg_ffn = _bwd_ffn(dx3, gu, x2, full["w_gu"], full["w_down"], row(g_ffn), T)
    g_ffn_w = [_dw(hf, dgu, "dw_gu", FF_CHUNK, slabs=True), _dw(act, dx3, "dw_down", 512).reshape(4, D_FF // 4, D)]
    (dx1, dqm, dco, do, dd, dkm, dvm, dg_x), rcv_ffn = _bwd_mid(dx2, x1, qm, km, vm, o, full["w_mo"], full["w_mq"],
                                                                full["w_out"], row(g_x), B, S, comm=_sibling_comm(g_ffn_w))
    pb_ffn = pos_sum(g_ffn_w, rcv_ffn, ag_ffn)
    dkv, dg_mem = _mem_bwd(dkm, dvm, mem2d, full["w_mkv"], row(g_mem), B)
    g_mid_w = [_dw(mn, dkv, "dw_mkv", 512, slabs=True), _dw(cat, dx1, "dw_out", 512).reshape(4, 256, D),
               _dw(hx, dqm, "dw_mq", 512).reshape(4, 256, D), _dw(om, dx2, "dw_mo", 512).reshape(4, 256, D)]
    (dq, dk, dv, dc, dcq), got = _fox_bwd(q, k, v, do, fox_bias, dd, ckT, B, S,
                                          comm=_join(_ici_comm(pb_ffn), _sibling_comm(g_mid_w)))
    rc_ffn, rcv_mid = got[:len(pb_ffn)], got[len(pb_ffn):]
    RH.update(zip(ag_ffn, fin_sum(g_ffn_w, rcv_ffn, rc_ffn, ag_ffn)))
    pb_mid = pos_sum(g_mid_w, rcv_mid, ag_mid)
    dc8 = jnp.transpose(dc[:, :, :2, :].reshape(B, 8, S), (0, 2, 1)).reshape(T, 8)
    dc8 = dc8 + dcq.reshape(T, 8, HEAD_D)[:, :, 0]
    dzf, dbf = _fgate_bwd(jnp.pad(dc8, ((0, 0), (0, LANES - 8))), zf, B, S)
    (du, dgt, dcw, dvec), rc_mid = _conv_bwd(dco, y, u, gt, cw, row(ln_g), row(ln_b), B, S, comm=_ici_comm(pb_mid))
    RH.update(zip(ag_mid, fin_sum(g_mid_w, rcv_mid, rc_mid, ag_mid)))
    dz = jnp.concatenate([du, dgt, dq, dk, dv, dzf], axis=1)
    g_in_w = [_dw(dz, h, "dw_in", 512, tk=D_IN_PAD // 3)[:D_IN].reshape(4, D_IN // 4, D)]
    rcv_in = _run_comm(_sibling_comm(g_in_w), "rs_sibling_in")
    (grad_x, dg_mix), rc_in = _bwd_in(dz, w_int, w_ft, x2d, dx1, row(g_mix), T,
                                      comm=_ici_comm(pos_sum(g_in_w, rcv_in, ["w_in"])))
    RH.update(zip(["w_in"], fin_sum(g_in_w, rcv_in, rc_in, ["w_in"])))
    shared = dict(zip(big_names, _sibling_share([RH[n] for n in big_names])))
    G, DL, NM, NV = {}, {}, {}, {}
    for n in big_names:
        G[n], DL[n], NM[n], NV[n] = _adamw(big2d(W, n), shared[n], big2d(M, n), big2d(V, n), "adamw_" + n,
                                           _half_shape(shared[n].shape))

    small_g = {"g_mix": dg_mix, "b_f": dbf[:, :8], "conv_w": dcw, "conv_b": dvec[0], "ln_g": dvec[1], "ln_b": dvec[2],
               "g_x": dg_x, "g_mem": dg_mem, "g_ffn": dg_ffn, "g_final": dg_final, "loss": loss_p[:, :1]}
    sg = _small_allreduce(_pack_small(small_g, CONV_HALO * 4), "allreduce_small")
    shapes = {n: W[n].shape for n in names if n not in big_names}
    shapes["conv_w"] = (CONV_HALO, CONV_CH)
    shapes["loss"] = (1,)
    sgrads = _unpack_small(sg, shapes, CONV_HALO * 4)
    loss = sgrads.pop("loss")[0]
    sgrads["conv_w"] = lax.dynamic_slice(sgrads["conv_w"], (0, chip * LANES), (CONV_K, LANES)).reshape(W["conv_w"].shape)
    spack = lambda d: _pack_small({n: d[n] for n in sgrads}, CONV_HALO)
    _, sd, snm, snv = _adamw(spack(W), spack(sgrads), spack(M), spack(V), "adamw_small", (8, LANES))
    sshapes = {n: W[n].shape for n in sgrads}
    SD, SNM, SNV = (_unpack_small(a, sshapes, CONV_HALO) for a in (sd, snm, snv))

    def collect(bigs, smalls):
        back = lambda n: (bigs[n].T if n == "w_in" else bigs[n]).reshape(W[n].shape)
        return [back(n) if n in big_names else smalls[n] for n in names]

    return (loss, grad_x.reshape(x.shape), *collect(G, sgrads), *collect(DL, SD), *collect(NM, SNM), *collect(NV, SNV))
```

```python
import functools
import math

import jax
import jax.numpy as jnp
from jax import lax
from jax.experimental import pallas as pl
from jax.experimental.pallas import tpu as pltpu

F32, BF16 = jnp.float32, jnp.bfloat16
HIGHEST = lax.Precision.HIGHEST
MESH = pl.DeviceIdType.MESH

D = 1024
CONV_CH = 512
CONV_K = 31
CONV_HALO = 32
FOX_W = 512
HEAD_D = 64
N_PAIR = 4
MEM_LEN = 256
MEM_HEADS = 4
MEM_HD = 256
D_FF = 2816
FF_CHUNK = 1408
D_IN = 2568
D_IN_PAD = 2688
OFF_F = 2560
EPS = 1e-6
LANES = 128

ADAM_LR, ADAM_B1, ADAM_B2, ADAM_EPS, ADAM_WD, ADAM_STEP = 0.001, 0.9, 0.999, 1e-08, 0.01, 10

VMEM_LIMIT = 60 * 1024 * 1024

BIG = (("w_out", 256, 1024, False), ("w_mq", 256, 1024, False), ("w_mkv", 1024, 512, True),
       ("w_mo", 256, 1024, False), ("w_gu", 1024, 1408, True), ("w_down", 704, 1024, False),
       ("w_in", 642, 1024, False))

ANY = pl.BlockSpec(memory_space=pl.ANY)


def _sig(x):
    return 1.0 / (1.0 + jnp.exp(-x))


def _dot(a, b):
    return jnp.dot(a, b, preferred_element_type=F32)


def _dot_nt(a, b):
    return lax.dot_general(a, b, (((1,), (1,)), ((), ())), preferred_element_type=F32)


def _dot_tn(a, b):
    return lax.dot_general(a, b, (((0,), (0,)), ((), ())), preferred_element_type=F32)


def _dot_hi(a, b):
    return jnp.dot(a, b, precision=HIGHEST, preferred_element_type=F32)


def _resident(a):
    nd = a.ndim
    return pl.BlockSpec(a.shape, lambda *_: (0,) * nd, pipeline_mode=pl.Buffered(1))


def _acc_spec(shape):
    nd = len(shape)
    return pl.BlockSpec(shape, lambda *_: (0,) * nd)


def _params(n_grid):
    return pltpu.CompilerParams(dimension_semantics=("arbitrary",) * n_grid, vmem_limit_bytes=VMEM_LIMIT)


def _sds(shape, dtype):
    return jax.ShapeDtypeStruct(shape, dtype)


def _rms(x):
    r = lax.rsqrt(jnp.mean(x * x, axis=-1, keepdims=True) + EPS)
    return r, x * r


def _rms_bwd(dy, xh, r, g):
    dxh = dy * g
    dx = r * (dxh - xh * jnp.mean(dxh * xh, axis=-1, keepdims=True))
    return dx, dy * xh


def _head_expand(rows, cols):
    hd = lax.broadcasted_iota(jnp.int32, (rows, cols), 1) // HEAD_D
    hr = lax.broadcasted_iota(jnp.int32, (rows, cols), 0)
    return (hd == hr).astype(F32)


def _fwd_in(x2, g_mix, w_int, w_ft, bf_pad, B, S, comm=None):
    T = B * S
    TB = min(512, S)
    nb = S // TB

    def body(x_ref, g_ref, w_ref, wf_ref, bf_ref, h_ref, u_ref, gt_ref, q_ref, k_ref, v_ref, zf_ref, c_ref, cq_ref,
             qx_ref, kx_ref, carry):
        j = pl.program_id(1)

        @pl.when(j == 0)
        def _():
            carry[...] = jnp.zeros_like(carry)

        _, xh = _rms(x_ref[...])
        h = (xh * g_ref[...]).astype(BF16)
        h_ref[...] = h
        u_ref[...] = _dot_nt(h, w_ref[0:512, :])
        gt_ref[...] = _dot_nt(h, w_ref[512:1024, :])
        qb = _dot_nt(h, w_ref[1024:1536, :]).astype(BF16)
        kb = _dot_nt(h, w_ref[1536:2048, :]).astype(BF16)
        q_ref[...] = qb
        k_ref[...] = kb
        v_ref[...] = _dot_nt(h, w_ref[2048:2560, :]).astype(BF16)
        zf = _dot_nt(h, wf_ref[...]) + bf_ref[...]
        zf_ref[...] = zf
        lane = lax.broadcasted_iota(jnp.int32, zf.shape, 1)
        logf = jnp.where(lane < 8, jnp.minimum(zf, 0.0) - jnp.log(1.0 + jnp.exp(-jnp.abs(zf))), 0.0)
        row = lax.broadcasted_iota(jnp.int32, (TB, TB), 0)
        col = lax.broadcasted_iota(jnp.int32, (TB, TB), 1)
        c = _dot_hi((row >= col).astype(F32), logf) + carry[0:1, :]
        carry[0:1, :] = c[TB - 1:TB, :]
        c_ref[...] = c
        cq = _dot_hi(c, _head_expand(LANES, FOX_W))
        cq_ref[...] = cq
        hl = lax.broadcasted_iota(jnp.int32, (TB, LANES), 1)
        for hd in range(2 * N_PAIR):
            grp = slice((hd // 2) * LANES, (hd // 2 + 1) * LANES)
            swap = (lambda t: t) if hd % 2 == 0 else (lambda t: pltpu.roll(t, HEAD_D, 1))
            qf = swap(qb[:, grp].astype(F32) * (1.0 / math.sqrt(HEAD_D)))
            kf = swap(kb[:, grp].astype(F32))
            cv = cq[:, grp] if hd % 2 == 1 else pltpu.roll(cq[:, grp], HEAD_D, 1)
            hi = cv.astype(BF16).astype(F32)
            mid = (cv - hi).astype(BF16).astype(F32)
            lo = (cv - hi - mid).astype(BF16).astype(F32)
            pick = lambda a, b, c3, one_from, one_to: jnp.where(hl == a[0], a[1], jnp.where(hl == b[0], b[1], jnp.where(
                hl == c3[0], c3[1], jnp.where((hl >= one_from) & (hl < one_to), 1.0, 0.0))))
            qx = jnp.where(hl < HEAD_D, qf, pick((67, hi), (68, mid), (69, lo), 64, 67))
            kx = jnp.where(hl < HEAD_D, kf, pick((64, -hi), (65, -mid), (66, -lo), 67, 70))
            qx_ref[:, hd * LANES:(hd + 1) * LANES] = qx.astype(BF16)
            kx_ref[:, hd * LANES:(hd + 1) * LANES] = kx.astype(BF16)

    tok = lambda w: pl.BlockSpec((TB, w), lambda b, j: (b * nb + j, 0))
    outs = [(D, BF16), (512, F32), (512, F32), (512, BF16), (512, BF16), (512, BF16), (LANES, F32),
            (LANES, F32), (FOX_W, F32), (2 * FOX_W, BF16), (2 * FOX_W, BF16)]
    return _call(
        body, comm, name="fwd_in", grid=(B, nb),
        in_specs=[tok(D), _resident(g_mix), _resident(w_int), _resident(w_ft), _resident(bf_pad)],
        out_specs=[tok(w) for w, _ in outs],
        out_shape=[_sds((T, w), dt) for w, dt in outs],
        scratch_shapes=[pltpu.VMEM((8, LANES), F32)],
        args=(x2, g_mix, w_int, w_ft, bf_pad))


def _head_sum(n):
    hc = lax.broadcasted_iota(jnp.int32, (n, n), 1) // HEAD_D
    hr = lax.broadcasted_iota(jnp.int32, (n, n), 0) // HEAD_D
    return (hc == hr).astype(F32)


def _layernorm_silu(y, lg, lb):
    mu = jnp.mean(y, axis=-1, keepdims=True)
    yc = y - mu
    rs = lax.rsqrt(jnp.mean(yc * yc, axis=-1, keepdims=True) + EPS)
    n = yc * rs
    l = n * lg + lb
    return rs, n, l


SUB = 8


def _shifted_copies(cat, sh, rows):
    for r in range(1, SUB):
        sh[r, 0:rows, :] = cat[r:r + rows, :]


def _tap(cat, sh, off, rows):
    r = off % SUB
    return cat[off:off + rows, :] if r == 0 else sh[r, off - r:off - r + rows, :]


def _conv_fwd(u, gt, cw, cb, lng, lnb, B, S, comm=None):
    T = B * S
    CB = min(256, S)
    nb = S // CB

    def body(u_ref, gt_ref, w_ref, cb_ref, lg_ref, lb_ref, y_ref, co_ref, acat, ash):
        j = pl.program_id(1)

        @pl.when(j == 0)
        def _():
            acat[0:CONV_HALO, :] = jnp.zeros((CONV_HALO, CONV_CH), F32)

        acat[CONV_HALO:CONV_HALO + CB, :] = u_ref[...] * _sig(gt_ref[...])
        _shifted_copies(acat, ash, CB + CONV_HALO - SUB)
        acc = jnp.zeros((CB, CONV_CH), F32) + cb_ref[...]
        for k in range(CONV_K):
            acc = acc + w_ref[k:k + 1, :] * _tap(acat, ash, CONV_HALO - (CONV_K - 1) + k, CB)
        acat[0:CONV_HALO, :] = acat[CB:CB + CONV_HALO, :]
        y_ref[...] = acc
        _, _, l = _layernorm_silu(acc, lg_ref[...], lb_ref[...])
        co_ref[...] = (l * _sig(l)).astype(BF16)

    tok = lambda w: pl.BlockSpec((CB, w), lambda b, j: (b * nb + j, 0))
    return _call(
        body, comm, name="conv_fwd", grid=(B, nb),
        in_specs=[tok(CONV_CH), tok(CONV_CH), _resident(cw), _resident(cb), _resident(lng), _resident(lnb)],
        out_specs=[tok(CONV_CH), tok(CONV_CH)],
        out_shape=[_sds((T, CONV_CH), F32), _sds((T, CONV_CH), BF16)],
        scratch_shapes=[pltpu.VMEM((CONV_HALO + CB, CONV_CH), F32),
                        pltpu.VMEM((SUB, CB + CONV_HALO - SUB, CONV_CH), F32)],
        args=(u, gt, cw, cb, lng, lnb))


def _fox_fwd(qx, kx, v, cq, B, S, comm=None):
    T = B * S
    TQ = min(256, S)
    nq = S // TQ
    one_lane = (HEAD_D, 0)

    def body(qa_ref, qb_ref, ka_ref, kb_ref, v_ref, cq_ref, o_ref, lse_ref, s_scr, s_odd, m_scr, acc_scr):
        i = pl.program_id(2)
        lane = lax.broadcasted_iota(jnp.int32, (TQ, LANES), 1)
        lo = lane < HEAD_D
        qh = (qa_ref[...], qb_ref[...])
        kh = (ka_ref, kb_ref)
        m_scr[...] = jnp.full(m_scr.shape, -1e30, F32)
        acc_scr[...] = jnp.zeros_like(acc_scr)
        row = lax.broadcasted_iota(jnp.int32, (TQ, TQ), 0)
        col = lax.broadcasted_iota(jnp.int32, (TQ, TQ), 1)
        wide = lambda x: jnp.concatenate([x, x], axis=1) if TQ == 2 * LANES else jnp.tile(x, (1, TQ // LANES))

        def scores(j, s_buf):
            start = pl.multiple_of(j * TQ, TQ)
            for h in range(2):
                s_buf[h] = _dot_nt(qh[h], kh[h][pl.ds(start, TQ), :])

        def softmax_step(j, s_buf, diagonal):
            start = pl.multiple_of(j * TQ, TQ)
            vj = v_ref[pl.ds(start, TQ), :]
            for h in range(2):
                def logits():
                    return jnp.where(col <= row, s_buf[h], -1e30) if diagonal else s_buf[h]

                m_old = m_scr[h]
                m_new = jnp.maximum(m_old, jnp.max(logits(), axis=-1, keepdims=True))
                alpha = jnp.exp(m_old - m_new)
                m_scr[h] = m_new
                p = jnp.exp(logits() - wide(m_new)).astype(BF16)
                vx = jnp.where(lane == one_lane[h], jnp.ones_like(vj), jnp.where(lo if h == 0 else ~lo, vj, jnp.zeros_like(vj)))
                acc_scr[h] = alpha * acc_scr[h] + _dot(p, vx)

        def two_blocks(jj, carry):
            j = 2 * jj
            scores(j + 1, s_odd)
            softmax_step(j, s_scr, False)
            scores(j + 2, s_scr)
            softmax_step(j + 1, s_odd, False)
            return carry

        scores(0, s_scr)
        lax.fori_loop(0, i // 2, two_blocks, 0)

        @pl.when(i % 2 == 0)
        def _():
            softmax_step(i, s_scr, True)

        @pl.when(i % 2 == 1)
        def _():
            scores(i, s_odd)
            softmax_step(i - 1, s_scr, False)
            softmax_step(i, s_odd, True)

        acc_a, acc_b = acc_scr[0], acc_scr[1]
        l_a = acc_a[:, one_lane[0]:one_lane[0] + 1]
        l_b = acc_b[:, one_lane[1]:one_lane[1] + 1]
        o_ref[...] = jnp.where(lo, acc_a / l_a, acc_b / l_b)
        lse_ref[...] = cq_ref[...] - jnp.where(lo, m_scr[0] + jnp.log(l_a), m_scr[1] + jnp.log(l_b))

    qspec = pl.BlockSpec((TQ, LANES), lambda b, p, i: (b * nq + i, p))
    kspec = pl.BlockSpec((S, LANES), lambda b, p, i: (b, p))
    qhead = lambda h: pl.BlockSpec((TQ, LANES), lambda b, p, i: (b * nq + i, 2 * p + h))
    khead = lambda h: pl.BlockSpec((S, LANES), lambda b, p, i: (b, 2 * p + h))
    return _call(
        body, comm, name="fox_fwd", grid=(B, N_PAIR, nq),
        in_specs=[qhead(0), qhead(1), khead(0), khead(1), kspec, qspec],
        out_specs=[qspec, qspec],
        out_shape=[_sds((T, FOX_W), F32), _sds((T, FOX_W), F32)],
        scratch_shapes=[pltpu.VMEM((2, TQ, TQ), F32), pltpu.VMEM((2, TQ, TQ), F32),
                        pltpu.VMEM((2, TQ, LANES), F32), pltpu.VMEM((2, TQ, LANES), F32)],
        args=(qx, qx, kx, kx, v, cq))


def _mem_kv(mem2, g_mem, w_mkv, B):
    def body(m_ref, g_ref, w_ref, mn_ref, km_ref, vm_ref):
        _, xh = _rms(m_ref[...])
        mn = (xh * g_ref[...]).astype(BF16)
        mn_ref[...] = mn
        for s in range(2):
            km_ref[:, 512 * s:512 * (s + 1)] = _dot(mn, w_ref[s]).astype(BF16)
            vm_ref[:, 512 * s:512 * (s + 1)] = _dot(mn, w_ref[2 + s]).astype(BF16)

    blk = pl.BlockSpec((MEM_LEN, D), lambda b: (b, 0))
    return pl.pallas_call(
        body, name="mem_kv", grid=(B,),
        in_specs=[blk, _resident(g_mem), _resident(w_mkv)],
        out_specs=[blk, blk, blk],
        out_shape=[_sds((B * MEM_LEN, D), BF16)] * 3,
        compiler_params=_params(1),
    )(mem2, g_mem, w_mkv)


def _mem_probs(qm, km):
    ps = []
    for h in range(MEM_HEADS):
        hs = slice(h * MEM_HD, (h + 1) * MEM_HD)
        lg = _dot_nt(qm[:, hs], km[:, hs]) * (1.0 / math.sqrt(MEM_HD))
        e = jnp.exp(lg - jnp.max(lg, axis=-1, keepdims=True))
        ps.append(e / jnp.sum(e, axis=-1, keepdims=True))
    return ps


def _fwd_mid(x2, co, o, km, vm, w_out, w_mq, w_mo, g_x, B, S):
    T = B * S
    TB = min(512, S)
    nb = S // TB

    def body(x_ref, co_ref, o_ref, km_ref, vm_ref, wo_ref, wq_ref, wm_ref, g_ref,
             x1_ref, hx_ref, qm_ref, om_ref, x2_ref, cat_ref):
        cat_ref[:, 0:CONV_CH] = co_ref[...]
        cat_ref[:, CONV_CH:D] = o_ref[...].astype(BF16)
        x1 = x_ref[...] + _dot(cat_ref[...], wo_ref[...])
        x1_ref[...] = x1
        _, xh = _rms(x1)
        hx = (xh * g_ref[...]).astype(BF16)
        hx_ref[...] = hx
        qm = _dot(hx, wq_ref[...]).astype(BF16)
        qm_ref[...] = qm
        ps = _mem_probs(qm, km_ref[...])
        vmv = vm_ref[...]
        for h in range(MEM_HEADS):
            hs = slice(h * MEM_HD, (h + 1) * MEM_HD)
            om_ref[:, hs] = _dot(ps[h].astype(BF16), vmv[:, hs]).astype(BF16)
        x2_ref[...] = x1 + _dot(om_ref[...], wm_ref[...])

    tok = lambda w: pl.BlockSpec((TB, w), lambda b, j: (b * nb + j, 0))
    memb = pl.BlockSpec((MEM_LEN, D), lambda b, j: (b, 0))
    outs = [(D, F32), (D, BF16), (D, BF16), (D, BF16), (D, F32), (D, BF16)]
    return pl.pallas_call(
        body, name="fwd_mid", grid=(B, nb),
        in_specs=[tok(D), tok(CONV_CH), tok(FOX_W), memb, memb, _resident(w_out), _resident(w_mq), _resident(w_mo),
                  _resident(g_x)],
        out_specs=[tok(w) for w, _ in outs],
        out_shape=[_sds((T, w), dt) for w, dt in outs],
        compiler_params=_params(2),
    )(x2, co, o, km, vm, w_out, w_mq, w_mo, g_x)


def _fwd_ffn(x2, tgt, w_gu, w_down, g_ffn, g_final, T):
    TB = min(256, T)
    nb = T // TB

    def body(x_ref, t_ref, wgu_ref, wd_ref, gf_ref, gl_ref, hf_ref, gu_ref, act_ref, dx3_ref, loss_ref, dgl_ref):
        i = pl.program_id(0)

        @pl.when(i == 0)
        def _():
            loss_ref[...] = jnp.zeros_like(loss_ref)
            dgl_ref[...] = jnp.zeros_like(dgl_ref)

        x2v = x_ref[...]
        _, xh = _rms(x2v)
        hf = (xh * gf_ref[...]).astype(BF16)
        hf_ref[...] = hf
        x3 = x2v
        for ch in range(D_FF // FF_CHUNK):
            c0 = ch * FF_CHUNK
            g = _dot(hf, wgu_ref[ch])
            u = _dot(hf, wgu_ref[2 + ch])
            gu_ref[:, c0:c0 + FF_CHUNK] = g
            gu_ref[:, D_FF + c0:D_FF + c0 + FF_CHUNK] = u
            act = (g * _sig(g) * u).astype(BF16)
            act_ref[:, c0:c0 + FF_CHUNK] = act
            x3 = x3 + _dot(act, wd_ref[c0:c0 + FF_CHUNK, :])
        r3, xh3 = _rms(x3)
        gl = gl_ref[...]
        e = xh3 * gl - t_ref[...]
        loss_ref[...] += jnp.sum(e * e) * (0.5 / D)
        dy = e * (1.0 / D)
        dx3, dgl = _rms_bwd(dy, xh3, r3, gl)
        dx3_ref[...] = dx3
        dgl_ref[...] += jnp.sum(dgl, axis=0, keepdims=True)

    tok = lambda w: pl.BlockSpec((TB, w), lambda i: (i, 0))
    return pl.pallas_call(
        body, name="fwd_ffn", grid=(nb,),
        in_specs=[tok(D), tok(D), _resident(w_gu), _resident(w_down), _resident(g_ffn), _resident(g_final)],
        out_specs=[tok(D), tok(2 * D_FF), tok(D_FF), tok(D), _acc_spec((1, LANES)), _acc_spec((1, D))],
        out_shape=[_sds((T, D), BF16), _sds((T, 2 * D_FF), F32), _sds((T, D_FF), BF16), _sds((T, D), F32),
                   _sds((1, LANES), F32), _sds((1, D), F32)],
        compiler_params=_params(1),
    )(x2, tgt, w_gu, w_down, g_ffn, g_final)


def _bwd_ffn(dx3, gu, x2, w_gu, w_down, g_ffn, T):
    TB = min(256, T)
    nb = T // TB

    def body(d_ref, gu_ref, x_ref, wgu_ref, wd_ref, gf_ref, dgu_ref, dx2_ref, dgf_ref):
        i = pl.program_id(0)

        @pl.when(i == 0)
        def _():
            dgf_ref[...] = jnp.zeros_like(dgf_ref)

        dx3v = d_ref[...]
        db = dx3v.astype(BF16)
        dhf = jnp.zeros((TB, D), F32)
        for ch in range(D_FF // FF_CHUNK):
            c0 = ch * FF_CHUNK
            dact = _dot_nt(db, wd_ref[c0:c0 + FF_CHUNK, :])
            g = gu_ref[:, c0:c0 + FF_CHUNK]
            u = gu_ref[:, D_FF + c0:D_FF + c0 + FF_CHUNK]
            sg = _sig(g)
            dg = (dact * u * sg * (1.0 + g * (1.0 - sg))).astype(BF16)
            du = (dact * g * sg).astype(BF16)
            dgu_ref[:, c0:c0 + FF_CHUNK] = dg
            dgu_ref[:, D_FF + c0:D_FF + c0 + FF_CHUNK] = du
            dhf = dhf + _dot_nt(dg, wgu_ref[ch]) + _dot_nt(du, wgu_ref[2 + ch])
        r2, xh2 = _rms(x_ref[...])
        dx, dg_tok = _rms_bwd(dhf, xh2, r2, gf_ref[...])
        dx2_ref[...] = dx3v + dx
        dgf_ref[...] += jnp.sum(dg_tok, axis=0, keepdims=True)

    tok = lambda w: pl.BlockSpec((TB, w), lambda i: (i, 0))
    return pl.pallas_call(
        body, name="bwd_ffn", grid=(nb,),
        in_specs=[tok(D), tok(2 * D_FF), tok(D), _resident(w_gu), _resident(w_down), _resident(g_ffn)],
        out_specs=[tok(2 * D_FF), tok(D), _acc_spec((1, D))],
        out_shape=[_sds((T, 2 * D_FF), BF16), _sds((T, D), F32), _sds((1, D), F32)],
        compiler_params=_params(1),
    )(dx3, gu, x2, w_gu, w_down, g_ffn)


def _bwd_mid(dx2, x1, qm, km, vm, o, w_mo, w_mq, w_out, g_x, B, S, comm=None):
    T = B * S
    TB = min(512, S)
    nb = S // TB
    inv = 1.0 / math.sqrt(MEM_HD)

    def body(d_ref, x1_ref, qm_ref, km_ref, vm_ref, o_ref, wm_ref, wq_ref, wo_ref, g_ref,
             dx1_ref, dqm_ref, dco_ref, do_ref, dd_ref, dkm_ref, dvm_ref, dgx_ref):
        b = pl.program_id(0)
        j = pl.program_id(1)

        @pl.when((b == 0) & (j == 0))
        def _():
            dgx_ref[...] = jnp.zeros_like(dgx_ref)

        @pl.when(j == 0)
        def _():
            dkm_ref[...] = jnp.zeros_like(dkm_ref)
            dvm_ref[...] = jnp.zeros_like(dvm_ref)

        dx2v = d_ref[...]
        dom = _dot_nt(dx2v.astype(BF16), wm_ref[...]).astype(BF16)
        qmv = qm_ref[...]
        kmv = km_ref[...]
        vmv = vm_ref[...]
        ps = _mem_probs(qmv, kmv)
        for h in range(MEM_HEADS):
            hs = slice(h * MEM_HD, (h + 1) * MEM_HD)
            p = ps[h]
            dp = _dot_nt(dom[:, hs], vmv[:, hs])
            ds = (p * (dp - jnp.sum(p * dp, axis=-1, keepdims=True))).astype(BF16)
            dqm_ref[:, hs] = (_dot(ds, kmv[:, hs]) * inv).astype(BF16)
            dkm_ref[:, hs] += _dot_tn(ds, qmv[:, hs]) * inv
            dvm_ref[:, hs] += _dot_tn(p.astype(BF16), dom[:, hs])
        dhx = _dot_nt(dqm_ref[...], wq_ref[...])
        r1, xh1 = _rms(x1_ref[...])
        dx, dg_tok = _rms_bwd(dhx, xh1, r1, g_ref[...])
        dx1 = dx2v + dx
        dx1_ref[...] = dx1
        dgx_ref[...] += jnp.sum(dg_tok, axis=0, keepdims=True)
        d1b = dx1.astype(BF16)
        dco_ref[...] = _dot_nt(d1b, wo_ref[0:CONV_CH, :])
        do = _dot_nt(d1b, wo_ref[CONV_CH:D, :])
        dob = do.astype(BF16)
        do_ref[...] = dob
        dd_ref[...] = _dot_hi(dob.astype(F32) * o_ref[...], _head_sum(FOX_W))

    tok = lambda w: pl.BlockSpec((TB, w), lambda b, j: (b * nb + j, 0))
    memb = pl.BlockSpec((MEM_LEN, D), lambda b, j: (b, 0))
    outs = [(D, F32), (D, BF16), (CONV_CH, F32), (FOX_W, BF16), (FOX_W, F32)]
    return _call(
        body, comm, name="bwd_mid", grid=(B, nb),
        in_specs=[tok(D), tok(D), tok(D), memb, memb, tok(FOX_W), _resident(w_mo), _resident(w_mq), _resident(w_out),
                  _resident(g_x)],
        out_specs=[tok(w) for w, _ in outs] + [memb, memb, _acc_spec((1, D))],
        out_shape=[_sds((T, w), dt) for w, dt in outs] + [_sds((B * MEM_LEN, D), F32)] * 2 + [_sds((1, D), F32)],
        scratch_shapes=[],
        args=(dx2, x1, qm, km, vm, o, w_mo, w_mq, w_out, g_x))


def _mem_bwd(dkm, dvm, mem2, w_mkv, g_mem, B):
    def body(dk_ref, dv_ref, m_ref, w_ref, g_ref, dkv_ref, dg_ref):
        b = pl.program_id(0)

        @pl.when(b == 0)
        def _():
            dg_ref[...] = jnp.zeros_like(dg_ref)

        dk = dk_ref[...].astype(BF16)
        dv = dv_ref[...].astype(BF16)
        dkv_ref[:, 0:D] = dk
        dkv_ref[:, D:2 * D] = dv
        dmn = jnp.zeros((MEM_LEN, D), F32)
        for s in range(2):
            dmn = dmn + _dot_nt(dk[:, 512 * s:512 * (s + 1)], w_ref[s]) + _dot_nt(dv[:, 512 * s:512 * (s + 1)], w_ref[2 + s])
        _, xh = _rms(m_ref[...])
        dg_ref[...] += jnp.sum(dmn * xh, axis=0, keepdims=True)

    blk = pl.BlockSpec((MEM_LEN, D), lambda b: (b, 0))
    return pl.pallas_call(
        body, name="mem_bwd", grid=(B,),
        in_specs=[blk, blk, blk, _resident(w_mkv), _resident(g_mem)],
        out_specs=[pl.BlockSpec((MEM_LEN, 2 * D), lambda b: (b, 0)), _acc_spec((1, D))],
        out_shape=[_sds((B * MEM_LEN, 2 * D), BF16), _sds((1, D), F32)],
        compiler_params=_params(1),
    )(dkm, dvm, mem2, w_mkv, g_mem)


def _fox_bwd(q, k, v, do, bias, dd, ckT, B, S, comm=None):
    T = B * S
    TK = min(256, S)
    nk = S // TK
    scale = 1.0 / math.sqrt(HEAD_D)

    def body(q_ref, k_ref, v_ref, do_ref, bias_ref, dd_ref, ck_ref, dq_ref, dk_ref, dv_ref, dc_ref, dcq_ref,
             dq_acc, rs_acc, s_scr, dp_scr, s_odd, dp_odd, dk_acc, dv_acc, dc_acc):
        j = pl.program_id(2)

        @pl.when(j == 0)
        def _():
            dq_acc[...] = jnp.zeros_like(dq_acc)
            rs_acc[...] = jnp.zeros_like(rs_acc)

        dk_acc[...] = jnp.zeros_like(dk_acc)
        dv_acc[...] = jnp.zeros_like(dv_acc)
        dc_acc[...] = jnp.zeros_like(dc_acc)
        lane = lax.broadcasted_iota(jnp.int32, (TK, LANES), 1)
        lo = lane < HEAD_D
        ks = k_ref[...] * jnp.asarray(scale, BF16)
        v2 = v_ref[...]
        zero = jnp.zeros_like(ks)
        kh = (jnp.where(lo, ks, zero), jnp.where(lo, zero, ks))
        vh = (jnp.where(lo, v2, zero), jnp.where(lo, zero, v2))
        kstart = pl.multiple_of(j * TK, TK)
        ckh = tuple(ck_ref[0, 0, h:h + 1, pl.ds(kstart, TK)] for h in range(2))
        row = lax.broadcasted_iota(jnp.int32, (TK, TK), 0)
        col = lax.broadcasted_iota(jnp.int32, (TK, TK), 1)
        wide = lambda x: jnp.concatenate([x, x], axis=1) if TK == 2 * LANES else jnp.tile(x, (1, TK // LANES))

        def scores(i, s_buf, dp_buf):
            start = pl.multiple_of(i * TK, TK)
            qi = q_ref[pl.ds(start, TK), :]
            doi = do_ref[pl.ds(start, TK), :]
            for h in range(2):
                s_buf[h] = _dot_nt(qi, kh[h])
                dp_buf[h] = _dot_nt(doi, vh[h])

        def grads(i, s_buf, dp_buf, diagonal):
            start = pl.multiple_of(i * TK, TK)
            qi = q_ref[pl.ds(start, TK), :]
            doi = do_ref[pl.ds(start, TK), :]
            bias2 = bias_ref[pl.ds(start, TK), :]
            dd2 = dd_ref[pl.ds(start, TK), :]
            for h in range(2):
                hc = slice(h * HEAD_D, h * HEAD_D + 1)
                bias = jnp.broadcast_to(bias2[:, hc], (TK, LANES))
                ddh = jnp.broadcast_to(dd2[:, hc], (TK, LANES))
                p = jnp.exp((s_buf[h] - ckh[h]) + wide(bias))
                if diagonal:
                    p = jnp.where(col <= row, p, 0.0)
                ds = p * (dp_buf[h] - wide(ddh))
                dc_acc[h, 0:1, :] += jnp.sum(ds, axis=0, keepdims=True)
                rs_acc[h, pl.ds(start, TK), :] += jnp.sum(ds, axis=1, keepdims=True)
                pb = p.astype(BF16)
                dsb = ds.astype(BF16)
                dv_acc[h] += _dot_tn(pb, doi)
                dk_acc[h] += _dot_tn(dsb, qi)
                dq_acc[pl.ds(start, TK), :] += _dot(dsb, kh[h])

        n_off = nk - 1 - j
        block = lambda t: jnp.where(t < n_off, j + 1 + t, j)

        def two_blocks(tt, carry):
            t = 2 * tt
            scores(block(t + 1), s_odd, dp_odd)
            grads(block(t), s_scr, dp_scr, False)
            scores(block(t + 2), s_scr, dp_scr)
            grads(block(t + 1), s_odd, dp_odd, False)
            return carry

        scores(block(0), s_scr, dp_scr)
        lax.fori_loop(0, n_off // 2, two_blocks, 0)

        @pl.when(n_off % 2 == 0)
        def _():
            grads(j, s_scr, dp_scr, True)

        @pl.when(n_off % 2 == 1)
        def _():
            scores(j, s_odd, dp_odd)
            grads(nk - 1, s_scr, dp_scr, False)
            grads(j, s_odd, dp_odd, True)

        dk_ref[...] = (jnp.where(lo, dk_acc[0], dk_acc[1]) * scale).astype(BF16)
        dv_ref[...] = jnp.where(lo, dv_acc[0], dv_acc[1]).astype(BF16)
        sub = lax.broadcasted_iota(jnp.int32, (8, TK), 0)
        dca = dc_acc[0, 0:1, :]
        dcb = dc_acc[1, 0:1, :]
        dc_ref[0, 0] = jnp.where(sub == 0, -dca, jnp.where(sub == 1, -dcb, 0.0))

        @pl.when(j == nk - 1)
        def _():
            dq_ref[...] = dq_acc[...].astype(BF16)
            lo_s = lax.broadcasted_iota(jnp.int32, (S, LANES), 1) < HEAD_D
            dcq_ref[...] = jnp.where(lo_s, rs_acc[0], rs_acc[1])

    full = pl.BlockSpec((S, LANES), lambda b, p, j: (b, p))
    blk = pl.BlockSpec((TK, LANES), lambda b, p, j: (b * nk + j, p))
    return _call(
        body, comm, name="fox_bwd", grid=(B, N_PAIR, nk),
        in_specs=[full, blk, blk, full, full, full, pl.BlockSpec((1, 1, 8, S), lambda b, p, j: (b, p, 0, 0))],
        out_specs=[full, blk, blk, pl.BlockSpec((1, 1, 8, TK), lambda b, p, j: (b, p, 0, j)), full],
        out_shape=[_sds((T, FOX_W), BF16), _sds((T, FOX_W), BF16), _sds((T, FOX_W), BF16),
                   _sds((B, N_PAIR, 8, S), F32), _sds((T, FOX_W), F32)],
        scratch_shapes=[pltpu.VMEM((S, LANES), F32), pltpu.VMEM((2, S, 1), F32),
                        pltpu.VMEM((2, TK, TK), F32), pltpu.VMEM((2, TK, TK), F32),
                        pltpu.VMEM((2, TK, TK), F32), pltpu.VMEM((2, TK, TK), F32),
                        pltpu.VMEM((2, TK, LANES), F32), pltpu.VMEM((2, TK, LANES), F32), pltpu.VMEM((2, 8, TK), F32)],
        args=(q, k, v, do, bias, dd, ckT))


def _fgate_bwd(dc8, zf, B, S):
    T = B * S
    TB = min(512, S)
    nb = S // TB

    def body(dc_ref, zf_ref, dzf_ref, dbf_ref, carry):
        b = pl.program_id(0)
        j = pl.program_id(1)

        @pl.when((b == 0) & (j == 0))
        def _():
            dbf_ref[...] = jnp.zeros_like(dbf_ref)

        @pl.when(j == 0)
        def _():
            carry[...] = jnp.zeros_like(carry)

        dc = dc_ref[...]
        row = lax.broadcasted_iota(jnp.int32, (TB, TB), 0)
        col = lax.broadcasted_iota(jnp.int32, (TB, TB), 1)
        dlogf = _dot_hi((col >= row).astype(F32), dc) + carry[0:1, :]
        carry[0:1, :] = dlogf[0:1, :]
        lane = lax.broadcasted_iota(jnp.int32, dc.shape, 1)
        dzf = jnp.where(lane < 8, dlogf * _sig(-zf_ref[...]), 0.0)
        dzf_ref[...] = dzf.astype(BF16)
        dbf_ref[...] += jnp.sum(dzf, axis=0, keepdims=True)

    tok = pl.BlockSpec((TB, LANES), lambda b, j: (b * nb + (nb - 1 - j), 0))
    return pl.pallas_call(
        body, name="fgate_bwd", grid=(B, nb),
        in_specs=[tok, tok],
        out_specs=[tok, _acc_spec((1, LANES))],
        out_shape=[_sds((T, LANES), BF16), _sds((1, LANES), F32)],
        scratch_shapes=[pltpu.VMEM((8, LANES), F32)],
        compiler_params=_params(2),
    )(dc8, zf)


def _conv_bwd(dco, y, u, gt, cw, lng, lnb, B, S, comm=None):
    T = B * S
    CB = min(256, S)
    nb = S // CB
    hb = CB // CONV_HALO

    def body(dco_ref, y_ref, u_ref, gt_ref, up_ref, gp_ref, w_ref, lg_ref, lb_ref,
             du_ref, dgt_ref, dw_ref, vec_ref, acat, dycat, ash, dysh):
        b = pl.program_id(0)
        j = pl.program_id(1)
        jr = nb - 1 - j

        @pl.when((b == 0) & (j == 0))
        def _():
            dw_ref[...] = jnp.zeros_like(dw_ref)
            vec_ref[...] = jnp.zeros_like(vec_ref)

        @pl.when(j == 0)
        def _():
            dycat[CB:CB + CONV_HALO, :] = jnp.zeros((CONV_HALO, CONV_CH), F32)

        lg = lg_ref[...]
        rs, n, l = _layernorm_silu(y_ref[...], lg, lb_ref[...])
        sg = _sig(l)
        dl = dco_ref[...] * (sg * (1.0 + l * (1.0 - sg)))
        dn = dl * lg
        dy = rs * (dn - jnp.mean(dn, axis=-1, keepdims=True) - n * jnp.mean(dn * n, axis=-1, keepdims=True))
        vec_ref[0:1, :] += jnp.sum(dy, axis=0, keepdims=True)
        vec_ref[1:2, :] += jnp.sum(dl * n, axis=0, keepdims=True)
        vec_ref[2:3, :] += jnp.sum(dl, axis=0, keepdims=True)
        dycat[0:CB, :] = dy
        uv = u_ref[...]
        sgt = _sig(gt_ref[...])
        acat[0:CONV_HALO, :] = jnp.where(jr > 0, up_ref[...] * _sig(gp_ref[...]), 0.0)
        acat[CONV_HALO:CONV_HALO + CB, :] = uv * sgt
        _shifted_copies(acat, ash, CB + CONV_HALO - SUB)
        _shifted_copies(dycat, dysh, CB + CONV_HALO - SUB)
        da = jnp.zeros((CB, CONV_CH), F32)
        for k in range(CONV_K):
            da = da + w_ref[k:k + 1, :] * _tap(dycat, dysh, CONV_K - 1 - k, CB)
            dw_ref[k:k + 1, :] += jnp.sum(dy * _tap(acat, ash, CONV_HALO - (CONV_K - 1) + k, CB), axis=0, keepdims=True)
        dycat[CB:CB + CONV_HALO, :] = dycat[0:CONV_HALO, :]
        du_ref[...] = (da * sgt).astype(BF16)
        dgt_ref[...] = (da * uv * sgt * (1.0 - sgt)).astype(BF16)

    tok = lambda w: pl.BlockSpec((CB, w), lambda b, j: (b * nb + (nb - 1 - j), 0))
    prev = pl.BlockSpec((CONV_HALO, CONV_CH), lambda b, j: (jnp.maximum((b * nb + (nb - 1 - j)) * hb - 1, 0), 0))
    return _call(
        body, comm, name="conv_bwd", grid=(B, nb),
        in_specs=[tok(CONV_CH), tok(CONV_CH), tok(CONV_CH), tok(CONV_CH), prev, prev, _resident(cw), _resident(lng),
                  _resident(lnb)],
        out_specs=[tok(CONV_CH), tok(CONV_CH), _acc_spec((CONV_HALO, CONV_CH)), _acc_spec((8, CONV_CH))],
        out_shape=[_sds((T, CONV_CH), BF16), _sds((T, CONV_CH), BF16), _sds((CONV_HALO, CONV_CH), F32),
                   _sds((8, CONV_CH), F32)],
        scratch_shapes=[pltpu.VMEM((CONV_HALO + CB, CONV_CH), F32), pltpu.VMEM((CB + CONV_HALO, CONV_CH), F32),
                        pltpu.VMEM((SUB, CB + CONV_HALO - SUB, CONV_CH), F32),
                        pltpu.VMEM((SUB, CB + CONV_HALO - SUB, CONV_CH), F32)],
        args=(dco, y, u, gt, u, gt, cw, lng, lnb))


def _bwd_in(dz, w_int, w_ft, x2, dx1, g_mix, T, comm=None):
    TB = min(512, T)
    nb = T // TB

    def body(dz_ref, w_ref, wf_ref, x_ref, d1_ref, g_ref, gx_ref, dg_ref):
        i = pl.program_id(0)

        @pl.when(i == 0)
        def _():
            dg_ref[...] = jnp.zeros_like(dg_ref)

        dh = _dot(dz_ref[:, 0:OFF_F], w_ref[0:OFF_F, :]) + _dot(dz_ref[:, OFF_F:D_IN_PAD], wf_ref[...])
        r0, xh0 = _rms(x_ref[...])
        dx, dg_tok = _rms_bwd(dh, xh0, r0, g_ref[...])
        gx_ref[...] = d1_ref[...] + dx
        dg_ref[...] += jnp.sum(dg_tok, axis=0, keepdims=True)

    tok = lambda w: pl.BlockSpec((TB, w), lambda i: (i, 0))
    return _call(
        body, comm, name="bwd_in", grid=(nb,),
        in_specs=[tok(D_IN_PAD), _resident(w_int), _resident(w_ft), tok(D), tok(D), _resident(g_mix)],
        out_specs=[tok(D), _acc_spec((1, D))],
        out_shape=[_sds((T, D), F32), _sds((1, D), F32)],
        scratch_shapes=[],
        args=(dz, w_int, w_ft, x2, dx1, g_mix))


def _dw(a, b, name, tn, slabs=False, tk=None):
    T, K = a.shape
    N = b.shape[1]
    tk = tk or (K if K <= 1024 else K // 2)
    tt = min(1024, T)
    nt = T // tt

    def body(a_ref, b_ref, o_ref, acc):
        t = pl.program_id(2)

        @pl.when(t == 0)
        def _():
            acc[...] = jnp.zeros_like(acc)

        acc[...] += _dot_tn(a_ref[...].astype(BF16), b_ref[...].astype(BF16))

        @pl.when(t == nt - 1)
        def _():
            o_ref[...] = acc[...]

    return pl.pallas_call(
        body, name=name, grid=(K // tk, N // tn, nt),
        in_specs=[pl.BlockSpec((tt, tk), lambda i, j, t: (t, i)), pl.BlockSpec((tt, tn), lambda i, j, t: (t, j))],
        out_specs=(pl.BlockSpec((None, tk, tn), lambda i, j, t: (j, i, 0)) if slabs
                   else pl.BlockSpec((tk, tn), lambda i, j, t: (i, j))),
        out_shape=_sds((N // tn, K, tn) if slabs else (K, N), F32),
        scratch_shapes=[pltpu.VMEM((tk, tn), F32)],
        compiler_params=_params(3),
    )(a, b)


def _pos():
    return lax.axis_index("x"), lax.axis_index("y"), lax.axis_index("c")


def _remote(src, dst, ssem, rsem, to):
    return pltpu.make_async_remote_copy(src_ref=src, dst_ref=dst, send_sem=ssem, recv_sem=rsem, device_id=to,
                                        device_id_type=MESH)


def _split_axis(shape):
    return 0 if shape[0] % 32 == 0 else 1


def _half_shape(shape, parts=2):
    return (shape[0] // parts, shape[1]) if _split_axis(shape) == 0 else (shape[0], shape[1] // parts)


def _half(shape, c):
    R, C = shape
    if _split_axis(shape) == 0:
        return (pl.ds(pl.multiple_of(c * (R // 2), 16), R // 2), slice(None))
    return (slice(None), pl.ds(pl.multiple_of(c * (C // 2), LANES), C // 2))


def _half_block(shape, parts, lead, which):
    blk = _half_shape(shape, parts)
    idx = (which, 0) if _split_axis(shape) == 0 else (0, which)
    return blk, tuple(lead) + idx


class _Comm:
    def __init__(self, ins, out_shapes, sems, start, finish):
        self.ins, self.out_shapes, self.sems, self.start, self.finish = list(ins), list(out_shapes), list(sems), start, finish


def _ag_comm(shards):
    n = len(shards)

    def parts(ins, outs, sems):
        send_sems, recv_sems, local_sems = sems
        x, y, c = _pos()
        me, sib = (x, y, c), (x, y, 1 - c)
        chips = [(1 - x, y), (x, 1 - y), (1 - x, 1 - y)]

        def rows(w, px, py, pc):
            return outs[w].at[(2 * px + py,) + _half(shards[w].shape, pc)]

        def copy(w, k, block, to, src=None):
            return _remote(rows(w, *block) if src is None else src, rows(w, *block), send_sems.at[w, k],
                           recv_sems.at[w, k], to)

        mine, first = [], []
        for w in range(n):
            src = ins[w].at[_half(shards[w].shape, c)]
            mine.append(pltpu.make_async_copy(src, rows(w, *me), local_sems.at[w]))
            first += [copy(w, 0, me, sib, src=src)] + [copy(w, 1 + j, me, (*chip, c), src=src) for j, chip in enumerate(chips)]
        return c, me, sib, chips, copy, mine, first

    def start(ins, outs, sems):
        _, _, _, _, _, mine, first = parts(ins, outs, sems)
        for cp in mine + first:
            cp.start()

    def finish(ins, outs, sems):
        c, me, sib, chips, copy, mine, first = parts(ins, outs, sems)
        passed = []
        for w in range(n):
            for j, chip in enumerate(chips):
                copy(w, 1 + j, (*chip, c), me).wait_recv()
                passed.append(copy(w, 4 + j, (*chip, c), sib))
                passed[-1].start()
        for w in range(n):
            copy(w, 0, sib, me).wait_recv()
            for j, chip in enumerate(chips):
                copy(w, 4 + j, (*chip, 1 - c), me).wait_recv()
        for cp in first + passed:
            cp.wait_send()
        for cp in mine:
            cp.wait()

    D7 = pltpu.SemaphoreType.DMA((n, 7))
    return _Comm(shards, [_sds((4,) + s.shape, s.dtype) for s in shards], [D7, D7, pltpu.SemaphoreType.DMA((n,))],
                 start, finish)


def _sibling_comm(gs):
    n = len(gs)

    def copies(ins, outs, sems):
        send_sems, recv_sems = sems
        x, y, c = _pos()
        return [_remote(ins[w].at[(s,) + _half(gs[w].shape[1:], 1 - c)], outs[w].at[s], send_sems.at[w, s],
                        recv_sems.at[w, s], (x, y, 1 - c)) for w in range(n) for s in range(4)]

    def start(ins, outs, sems):
        for cp in copies(ins, outs, sems):
            cp.start()

    def finish(ins, outs, sems):
        for cp in copies(ins, outs, sems):
            cp.wait()

    D4 = pltpu.SemaphoreType.DMA((n, 4))
    return _Comm(gs, [_sds((4,) + _half_shape(g.shape[1:]), F32) for g in gs], [D4, D4], start, finish)


def _ici_comm(pbs):
    n = len(pbs)

    def copies(ins, outs, sems):
        send_sems, recv_sems = sems
        x, y, c = _pos()
        return [_remote(ins[w].at[2 * tx + ty], outs[w].at[j], send_sems.at[w, j], recv_sems.at[w, j], (tx, ty, c))
                for w in range(n) for j, (tx, ty) in enumerate([(1 - x, y), (x, 1 - y), (1 - x, 1 - y)])]

    def start(ins, outs, sems):
        for cp in copies(ins, outs, sems):
            cp.start()

    def finish(ins, outs, sems):
        for cp in copies(ins, outs, sems):
            cp.wait()

    D3 = pltpu.SemaphoreType.DMA((n, 3))
    return _Comm(pbs, [_sds((3,) + p.shape[1:], BF16) for p in pbs], [D3, D3], start, finish)


def _join(*comms):
    counts = [(len(c.ins), len(c.out_shapes), len(c.sems)) for c in comms]

    def each(which):
        def run(ins, outs, sems):
            i = o = k = 0
            for c, (ni, no, nk) in zip(comms, counts):
                getattr(c, which)(ins[i:i + ni], outs[o:o + no], sems[k:k + nk])
                i, o, k = i + ni, o + no, k + nk
        return run

    return _Comm(sum((c.ins for c in comms), []), sum((c.out_shapes for c in comms), []),
                 sum((c.sems for c in comms), []), each("start"), each("finish"))


def _run_comm(comm, name):
    ni, no = len(comm.ins), len(comm.out_shapes)

    def body(*refs):
        ins, outs, sems = refs[:ni], refs[ni:ni + no], refs[ni + no:]
        comm.start(ins, outs, sems)
        comm.finish(ins, outs, sems)

    return pl.pallas_call(body, name=name, out_shape=comm.out_shapes, in_specs=[ANY] * ni, out_specs=[ANY] * no,
                          scratch_shapes=comm.sems)(*comm.ins)


def _call(body, comm, *, name, grid, in_specs, out_specs, out_shape, scratch_shapes, args):
    n_grid = len(grid)
    if comm is None:
        res = pl.pallas_call(body, name=name, grid=grid, in_specs=in_specs, out_specs=out_specs, out_shape=out_shape,
                             scratch_shapes=scratch_shapes, compiler_params=_params(n_grid))(*args)
        return list(res), []
    n_in, n_out, n_scr = len(in_specs), len(out_specs), len(scratch_shapes)
    ni, no = len(comm.ins), len(comm.out_shapes)

    def carried(*refs):
        ins, refs = refs[:n_in], refs[n_in:]
        cins, refs = refs[:ni], refs[ni:]
        outs, refs = refs[:n_out], refs[n_out:]
        couts, refs = refs[:no], refs[no:]
        scr, csems = refs[:n_scr], refs[n_scr:]
        ids = [pl.program_id(ax) for ax in range(n_grid)]
        first = functools.reduce(jnp.logical_and, [i == 0 for i in ids])
        last = functools.reduce(jnp.logical_and, [i == g - 1 for i, g in zip(ids, grid)])

        @pl.when(first)
        def _():
            comm.start(cins, couts, csems)

        body(*ins, *outs, *scr)

        @pl.when(last)
        def _():
            comm.finish(cins, couts, csems)

    res = pl.pallas_call(
        carried, name=name, grid=grid, in_specs=list(in_specs) + [ANY] * ni, out_specs=list(out_specs) + [ANY] * no,
        out_shape=list(out_shape) + comm.out_shapes, scratch_shapes=list(scratch_shapes) + comm.sems,
        compiler_params=_params(n_grid))(*args, *comm.ins)
    return list(res[:n_out]), list(res[n_out:])


def _sibling_share(gs):
    n = len(gs)

    def body(*refs):
        outs = refs[n:2 * n]
        send_sems, recv_sems = refs[2 * n:]
        x, y, c = _pos()
        cps = []
        for w in range(n):
            mine = outs[w].at[_half(gs[w].shape, c)]
            cps.append(_remote(mine, mine, send_sems.at[w], recv_sems.at[w], (x, y, 1 - c)))
            cps[-1].start()
        for cp in cps:
            cp.wait()

    return pl.pallas_call(
        body, name="rs_share", out_shape=[_sds(g.shape, F32) for g in gs],
        in_specs=[ANY] * n, out_specs=[ANY] * n, input_output_aliases={w: w for w in range(n)},
        scratch_shapes=[pltpu.SemaphoreType.DMA((n,)), pltpu.SemaphoreType.DMA((n,))],
    )(*gs)


def _small_allreduce(v, name):
    P = v.shape[0]
    vm = pl.BlockSpec(memory_space=pltpu.VMEM)

    def body(v_ref, o_ref, gath, send_sems, recv_sems):
        x, y, c = _pos()
        me = 4 * x + 2 * y + c
        gath[me] = v_ref[...]
        cps = []
        for r in range(1, 8):
            tx = (1 - x) if r & 4 else x
            ty = (1 - y) if r & 2 else y
            tc = (1 - c) if r & 1 else c
            cps.append(_remote(v_ref, gath.at[me], send_sems.at[r - 1], recv_sems.at[r - 1], (tx, ty, tc)))
            cps[-1].start()
        for cp in cps:
            cp.wait()
        acc = gath[0]
        for d in range(1, 8):
            acc = acc + gath[d]
        o_ref[...] = acc

    return pl.pallas_call(
        body, name=name, out_shape=_sds((P, LANES), F32), in_specs=[vm], out_specs=vm,
        scratch_shapes=[pltpu.VMEM((8, P, LANES), F32), pltpu.SemaphoreType.DMA((7,)), pltpu.SemaphoreType.DMA((7,))],
    )(v)


def _chip_sum(g, rcv, pos, name):
    shard = g.shape[1:]
    hs = _half_shape(shard)

    def body(pos_ref, g_ref, r_ref, o_ref):
        o_ref[...] = (g_ref[...] + r_ref[...]).astype(BF16)

    return pl.pallas_call(
        body, name=name, out_shape=_sds((4,) + hs, BF16),
        grid_spec=pltpu.PrefetchScalarGridSpec(
            num_scalar_prefetch=1, grid=(4,),
            in_specs=[pl.BlockSpec((1,) + hs, lambda s, pos: _half_block(shard, 2, (s,), pos[0])[1]),
                      pl.BlockSpec((1,) + hs, lambda s, pos: (s, 0, 0))],
            out_specs=pl.BlockSpec((1,) + hs, lambda s, pos: (s, 0, 0))),
        compiler_params=_params(1),
    )(pos, g, rcv)


def _final_sum(g, rcv, rc, pos, name):
    shard = g.shape[1:]
    qs = _half_shape(shard, 4)

    def body(pos_ref, g_ref, r_ref, rc_ref, o_ref):
        acc = g_ref[0] + r_ref[0]
        for j in range(3):
            acc = acc + rc_ref[j].astype(F32)
        o_ref[...] = acc

    return pl.pallas_call(
        body, name=name, out_shape=_sds(shard, F32),
        grid_spec=pltpu.PrefetchScalarGridSpec(
            num_scalar_prefetch=1, grid=(2,),
            in_specs=[pl.BlockSpec((1,) + qs, lambda i, pos: _half_block(shard, 4, (pos[1],), pos[0] * 2 + i)[1]),
                      pl.BlockSpec((1,) + qs, lambda i, pos: _half_block(shard, 4, (pos[1],), i)[1]),
                      pl.BlockSpec((3,) + qs, lambda i, pos: _half_block(shard, 4, (0,), i)[1])],
            out_specs=pl.BlockSpec(qs, lambda i, pos: _half_block(shard, 4, (), pos[0] * 2 + i)[1])),
        compiler_params=_params(1),
    )(pos, g, rcv, rc)


def _adamw_math(w, g, m, v):
    m = ADAM_B1 * m + (1.0 - ADAM_B1) * g
    v = ADAM_B2 * v + (1.0 - ADAM_B2) * (g * g)
    m_hat = m / (1.0 - ADAM_B1 ** ADAM_STEP)
    v_hat = v / (1.0 - ADAM_B2 ** ADAM_STEP)
    delta = -ADAM_LR * (m_hat / (jnp.sqrt(v_hat) + ADAM_EPS) + ADAM_WD * w)
    return delta, m, v


def _adamw(w, g, m, v, name, blk_shape):
    R, C = w.shape

    def body(w_ref, g_ref, m_ref, v_ref, go_ref, d_ref, nm_ref, nv_ref):
        g = g_ref[...]
        d, nm, nv = _adamw_math(w_ref[...], g, m_ref[...], v_ref[...])
        go_ref[...] = g
        d_ref[...] = d
        nm_ref[...] = nm
        nv_ref[...] = nv

    blk = pl.BlockSpec(blk_shape, lambda i, j: (i, j))
    return pl.pallas_call(
        body, name=name, grid=(R // blk_shape[0], C // blk_shape[1]), in_specs=[blk] * 4, out_specs=[blk] * 4,
        out_shape=[_sds((R, C), F32)] * 4, compiler_params=_params(2),
    )(w, g, m, v)


SMALL = (("g_mix", 8), ("b_f", 8), ("conv_w", None), ("conv_b", 8), ("ln_g", 8), ("ln_b", 8), ("g_x", 8), ("g_mem", 8),
         ("g_ffn", 8), ("g_final", 8), ("loss", 8))


def _pack_small(parts, conv_rows):
    rows = []
    for name, n in SMALL:
        if name not in parts:
            continue
        n = conv_rows if n is None else n
        flat = parts[name].reshape(-1).astype(F32)
        flat = jnp.pad(flat, (0, n * LANES - flat.shape[0]))
        rows.append(flat.reshape(n, LANES))
    return jnp.concatenate(rows, axis=0)


def _unpack_small(p, shapes, conv_rows):
    out, off = {}, 0
    for name, n in SMALL:
        if name not in shapes:
            continue
        n = conv_rows if n is None else n
        size = math.prod(shapes[name])
        out[name] = p[off:off + n].reshape(-1)[:size].reshape(shapes[name])
        off += n
    return out


def kernel(x, mem, g_mix, w_in, b_f, conv_w, conv_b, ln_g, ln_b, w_out, g_x, g_mem, w_mq, w_mkv, w_mo, g_ffn, w_gu, w_down, g_final, loss_target, m_g_mix, m_w_in, m_b_f, m_conv_w, m_conv_b, m_ln_g, m_ln_b, m_w_out, m_g_x, m_g_mem, m_w_mq, m_w_mkv, m_w_mo, m_g_ffn, m_w_gu, m_w_down, m_g_final, v_g_mix, v_w_in, v_b_f, v_conv_w, v_conv_b, v_ln_g, v_ln_b, v_w_out, v_g_x, v_g_mem, v_w_mq, v_w_mkv, v_w_mo, v_g_ffn, v_w_gu, v_w_down, v_g_final):
    names = ["g_mix", "w_in", "b_f", "conv_w", "conv_b", "ln_g", "ln_b", "w_out", "g_x", "g_mem", "w_mq", "w_mkv",
             "w_mo", "g_ffn", "w_gu", "w_down", "g_final"]
    W = dict(zip(names, [g_mix, w_in, b_f, conv_w, conv_b, ln_g, ln_b, w_out, g_x, g_mem, w_mq, w_mkv, w_mo, g_ffn,
                         w_gu, w_down, g_final]))
    M = dict(zip(names, [m_g_mix, m_w_in, m_b_f, m_conv_w, m_conv_b, m_ln_g, m_ln_b, m_w_out, m_g_x, m_g_mem, m_w_mq,
                         m_w_mkv, m_w_mo, m_g_ffn, m_w_gu, m_w_down, m_g_final]))
    V = dict(zip(names, [v_g_mix, v_w_in, v_b_f, v_conv_w, v_conv_b, v_ln_g, v_ln_b, v_w_out, v_g_x, v_g_mem, v_w_mq,
                         v_w_mkv, v_w_mo, v_g_ffn, v_w_gu, v_w_down, v_g_final]))
    big_names = [n for n, _, _, _ in BIG]
    B, S, _ = x.shape
    T = B * S
    mx, my, mc = _pos()
    chip = 2 * mx + my
    pos = jnp.stack([mc, chip]).astype(jnp.int32)

    shard2d = lambda a: a.reshape(a.shape[-2], a.shape[-1])
    big2d = lambda d, n: shard2d(d[n]).T if n == "w_in" else shard2d(d[n])
    shard_bf = {n: big2d(W, n).astype(BF16) for n in big_names}
    ag_mid = ["w_mkv", "w_out", "w_mq", "w_mo"]
    ag_ffn = ["w_gu", "w_down"]
    slab = {"w_in": _run_comm(_ag_comm([shard_bf["w_in"]]), "ag_w_in")[0]}
    w_int = slab["w_in"].reshape(D_IN, D)
    w_ft = jnp.pad(w_int[OFF_F:D_IN], ((0, D_IN_PAD - D_IN), (0, 0)))
    cw_mine = jnp.pad(shard2d(conv_w), ((0, 1), (0, 0)))
    cw_slot = lax.dynamic_update_slice(jnp.zeros((CONV_HALO, CONV_CH), F32), cw_mine, (0, chip * LANES))
    cw = _small_allreduce(cw_slot.reshape(CONV_HALO * 4, LANES) * 0.5, "gather_conv_w").reshape(CONV_HALO, CONV_CH)

    row = lambda a: a.reshape(1, -1)
    bf_pad = jnp.pad(row(b_f), ((0, 0), (0, LANES - 8)))
    x2d = x.reshape(T, D)
    mem2d = mem.reshape(B * MEM_LEN, D)
    tgt = loss_target.reshape(T, D)

    (h, u, gt, q, k, v, zf, c, cq, qx, kx), got = _fwd_in(x2d, row(g_mix), w_int, w_ft, bf_pad, B, S,
                                                  comm=_ag_comm([shard_bf[n] for n in ag_mid[:2]]))
    slab.update(zip(ag_mid[:2], got))
    ckT = jnp.transpose(c.reshape(B, S, LANES)[:, :, :8], (0, 2, 1)).reshape(B, N_PAIR, 2, S)
    ckT = jnp.pad(ckT, ((0, 0), (0, 0), (0, 6), (0, 0)))
    (y, co), got = _conv_fwd(u, gt, cw, row(conv_b), row(ln_g), row(ln_b), B, S,
                             comm=_ag_comm([shard_bf[n] for n in ag_mid[2:]]))
    slab.update(zip(ag_mid[2:], got))
    (o, fox_bias), got = _fox_fwd(qx, kx, v, cq, B, S, comm=_ag_comm([shard_bf[n] for n in ag_ffn]))
    slab.update(zip(ag_ffn, got))
    full = {n: slab[n] if by_col else slab[n].reshape(4 * r, c) for n, r, c, by_col in BIG}
    mn, km, vm = _mem_kv(mem2d, row(g_mem), full["w_mkv"], B)
    x1, hx, qm, om, x2, cat = _fwd_mid(x2d, co, o, km, vm, full["w_out"], full["w_mq"], full["w_mo"], row(g_x), B, S)
    hf, gu, act, dx3, loss_p, dg_final = _fwd_ffn(x2, tgt, full["w_gu"], full["w_down"], row(g_ffn), row(g_final), T)

    pos_sum = lambda gs, rcvs, ns: [_chip_sum(g, r, pos, "rs_chip_sum_" + n) for g, r, n in zip(gs, rcvs, ns)]
    fin_sum = lambda gs, rcvs, rcs, ns: [_final_sum(g, r, q3, pos, "rs_final_sum_" + n)
                                         for g, r, q3, n in zip(gs, rcvs, rcs, ns)]
    RH = {}
    dgu, dx2, dg_ffn = _bwd_ffn(dx3, gu, x2, full["w_gu"], full["w_down"], row(g_ffn), T)
    g_ffn_w = [_dw(hf, dgu, "dw_gu", FF_CHUNK, slabs=True), _dw(act, dx3, "dw_down", 512).reshape(4, D_FF // 4, D)]
    (dx1, dqm, dco, do, dd, dkm, dvm, dg_x), rcv_ffn = _bwd_mid(dx2, x1, qm, km, vm, o, full["w_mo"], full["w_mq"],
                                                                full["w_out"], row(g_x), B, S, comm=_sibling_comm(g_ffn_w))
    pb_ffn = pos_sum(g_ffn_w, rcv_ffn, ag_ffn)
    dkv, dg_mem = _mem_bwd(dkm, dvm, mem2d, full["w_mkv"], row(g_mem), B)
    g_mid_w = [_dw(mn, dkv, "dw_mkv", 512, slabs=True), _dw(cat, dx1, "dw_out", 512).reshape(4, 256, D),
               _dw(hx, dqm, "dw_mq", 512).reshape(4, 256, D), _dw(om, dx2, "dw_mo", 512).reshape(4, 256, D)]
    (dq, dk, dv, dc, dcq), got = _fox_bwd(q, k, v, do, fox_bias, dd, ckT, B, S,
                                          comm=_join(_ici_comm(pb_ffn), _sibling_comm(g_mid_w)))
    rc_ffn, rcv_mid = got[:len(pb_ffn)], got[len(pb_ffn):]
    RH.update(zip(ag_ffn, fin_sum(g_ffn_w, rcv_ffn, rc_ffn, ag_ffn)))
    pb_mid = pos_sum(g_mid_w, rcv_mid, ag_mid)
    dc8 = jnp.transpose(dc[:, :, :2, :].reshape(B, 8, S), (0, 2, 1)).reshape(T, 8)
    dc8 = dc8 + dcq.reshape(T, 8, HEAD_D)[:, :, 0]
    dzf, dbf = _fgate_bwd(jnp.pad(dc8, ((0, 0), (0, LANES - 8))), zf, B, S)
    (du, dgt, dcw, dvec), rc_mid = _conv_bwd(dco, y, u, gt, cw, row(ln_g), row(ln_b), B, S, comm=_ici_comm(pb_mid))
    RH.update(zip(ag_mid, fin_sum(g_mid_w, rcv_mid, rc_mid, ag_mid)))
    dz = jnp.concatenate([du, dgt, dq, dk, dv, dzf], axis=1)
    g_in_w = [_dw(dz, h, "dw_in", 512, tk=D_IN_PAD // 3)[:D_IN].reshape(4, D_IN // 4, D)]
    rcv_in = _run_comm(_sibling_comm(g_in_w), "rs_sibling_in")
    (grad_x, dg_mix), rc_in = _bwd_in(dz, w_int, w_ft, x2d, dx1, row(g_mix), T,
                                      comm=_ici_comm(pos_sum(g_in_w, rcv_in, ["w_in"])))
    RH.update(zip(["w_in"], fin_sum(g_in_w, rcv_in, rc_in, ["w_in"])))
    shared = dict(zip(big_names, _sibling_share([RH[n] for n in big_names])))
    G, DL, NM, NV = {}, {}, {}, {}
    for n in big_names:
        G[n], DL[n], NM[n], NV[n] = _adamw(big2d(W, n), shared[n], big2d(M, n), big2d(V, n), "adamw_" + n,
                                           _half_shape(shared[n].shape))

    small_g = {"g_mix": dg_mix, "b_f": dbf[:, :8], "conv_w": dcw, "conv_b": dvec[0], "ln_g": dvec[1], "ln_b": dvec[2],
               "g_x": dg_x, "g_mem": dg_mem, "g_ffn": dg_ffn, "g_final": dg_final, "loss": loss_p[:, :1]}
    sg = _small_allreduce(_pack_small(small_g, CONV_HALO * 4), "allreduce_small")
    shapes = {n: W[n].shape for n in names if n not in big_names}
    shapes["conv_w"] = (CONV_HALO, CONV_CH)
    shapes["loss"] = (1,)
    sgrads = _unpack_small(sg, shapes, CONV_HALO * 4)
    loss = sgrads.pop("loss")[0]
    sgrads["conv_w"] = lax.dynamic_slice(sgrads["conv_w"], (0, chip * LANES), (CONV_K, LANES)).reshape(W["conv_w"].shape)
    spack = lambda d: _pack_small({n: d[n] for n in sgrads}, CONV_HALO)
    _, sd, snm, snv = _adamw(spack(W), spack(sgrads), spack(M), spack(V), "adamw_small", (8, LANES))
    sshapes = {n: W[n].shape for n in sgrads}
    SD, SNM, SNV = (_unpack_small(a, sshapes, CONV_HALO) for a in (sd, snm, snv))

    def collect(bigs, smalls):
        back = lambda n: (bigs[n].T if n == "w_in" else bigs[n]).reshape(W[n].shape)
        return [back(n) if n in big_names else smalls[n] for n in names]

    return (loss, grad_x.reshape(x.shape), *collect(G, sgrads), *collect(DL, SD), *collect(NM, SNM), *collect(NV, SNV))
```

```python
import functools
import math

import jax
import jax.numpy as jnp
from jax import lax
from jax.experimental import pallas as pl
from jax.experimental.pallas import tpu as pltpu

F32, BF16 = jnp.float32, jnp.bfloat16
HIGHEST = lax.Precision.HIGHEST
MESH = pl.DeviceIdType.MESH

D = 1024
CONV_CH = 512
CONV_K = 31
CONV_HALO = 32
FOX_W = 512
HEAD_D = 64
N_PAIR = 4
MEM_LEN = 256
MEM_HEADS = 4
MEM_HD = 256
D_FF = 2816
FF_CHUNK = 1408
D_IN = 2568
D_IN_PAD = 2688
OFF_F = 2560
EPS = 1e-6
LANES = 128

ADAM_LR, ADAM_B1, ADAM_B2, ADAM_EPS, ADAM_WD, ADAM_STEP = 0.001, 0.9, 0.999, 1e-08, 0.01, 10

VMEM_LIMIT = 60 * 1024 * 1024

BIG = (("w_out", 256, 1024, False), ("w_mq", 256, 1024, False), ("w_mkv", 1024, 512, True),
       ("w_mo", 256, 1024, False), ("w_gu", 1024, 1408, True), ("w_down", 704, 1024, False),
       ("w_in", 642, 1024, False))

ANY = pl.BlockSpec(memory_space=pl.ANY)


def _sig(x):
    return 1.0 / (1.0 + jnp.exp(-x))


def _dot(a, b):
    return jnp.dot(a, b, preferred_element_type=F32)


def _dot_nt(a, b):
    return lax.dot_general(a, b, (((1,), (1,)), ((), ())), preferred_element_type=F32)


def _dot_tn(a, b):
    return lax.dot_general(a, b, (((0,), (0,)), ((), ())), preferred_element_type=F32)


def _dot_hi(a, b):
    return jnp.dot(a, b, precision=HIGHEST, preferred_element_type=F32)


def _resident(a):
    nd = a.ndim
    return pl.BlockSpec(a.shape, lambda *_: (0,) * nd, pipeline_mode=pl.Buffered(1))


def _acc_spec(shape):
    nd = len(shape)
    return pl.BlockSpec(shape, lambda *_: (0,) * nd)


def _params(n_grid):
    return pltpu.CompilerParams(dimension_semantics=("arbitrary",) * n_grid, vmem_limit_bytes=VMEM_LIMIT)


def _sds(shape, dtype):
    return jax.ShapeDtypeStruct(shape, dtype)


def _rms(x):
    r = lax.rsqrt(jnp.mean(x * x, axis=-1, keepdims=True) + EPS)
    return r, x * r


def _rms_bwd(dy, xh, r, g):
    dxh = dy * g
    dx = r * (dxh - xh * jnp.mean(dxh * xh, axis=-1, keepdims=True))
    return dx, dy * xh


def _head_expand(rows, cols):
    hd = lax.broadcasted_iota(jnp.int32, (rows, cols), 1) // HEAD_D
    hr = lax.broadcasted_iota(jnp.int32, (rows, cols), 0)
    return (hd == hr).astype(F32)


def _fwd_in(x2, g_mix, w_int, w_ft, bf_pad, B, S, comm=None):
    T = B * S
    TB = min(512, S)
    nb = S // TB

    def body(x_ref, g_ref, w_ref, wf_ref, bf_ref, h_ref, u_ref, gt_ref, q_ref, k_ref, v_ref, zf_ref, c_ref, cq_ref,
             qx_ref, kx_ref, carry):
        j = pl.program_id(1)

        @pl.when(j == 0)
        def _():
            carry[...] = jnp.zeros_like(carry)

        _, xh = _rms(x_ref[...])
        h = (xh * g_ref[...]).astype(BF16)
        h_ref[...] = h
        u_ref[...] = _dot_nt(h, w_ref[0:512, :])
        gt_ref[...] = _dot_nt(h, w_ref[512:1024, :])
        qb = _dot_nt(h, w_ref[1024:1536, :]).astype(BF16)
        kb = _dot_nt(h, w_ref[1536:2048, :]).astype(BF16)
        q_ref[...] = qb
        k_ref[...] = kb
        v_ref[...] = _dot_nt(h, w_ref[2048:2560, :]).astype(BF16)
        zf = _dot_nt(h, wf_ref[...]) + bf_ref[...]
        zf_ref[...] = zf
        lane = lax.broadcasted_iota(jnp.int32, zf.shape, 1)
        logf = jnp.where(lane < 8, jnp.minimum(zf, 0.0) - jnp.log(1.0 + jnp.exp(-jnp.abs(zf))), 0.0)
        row = lax.broadcasted_iota(jnp.int32, (TB, TB), 0)
        col = lax.broadcasted_iota(jnp.int32, (TB, TB), 1)
        c = _dot_hi((row >= col).astype(F32), logf) + carry[0:1, :]
        carry[0:1, :] = c[TB - 1:TB, :]
        c_ref[...] = c
        cq = _dot_hi(c, _head_expand(LANES, FOX_W))
        cq_ref[...] = cq
        hl = lax.broadcasted_iota(jnp.int32, (TB, LANES), 1)
        for hd in range(2 * N_PAIR):
            grp = slice((hd // 2) * LANES, (hd // 2 + 1) * LANES)
            swap = (lambda t: t) if hd % 2 == 0 else (lambda t: pltpu.roll(t, HEAD_D, 1))
            qf = swap(qb[:, grp].astype(F32) * (1.0 / math.sqrt(HEAD_D)))
            kf = swap(kb[:, grp].astype(F32))
            cv = cq[:, grp] if hd % 2 == 1 else pltpu.roll(cq[:, grp], HEAD_D, 1)
            hi = cv.astype(BF16).astype(F32)
            mid = (cv - hi).astype(BF16).astype(F32)
            lo = (cv - hi - mid).astype(BF16).astype(F32)
            pick = lambda a, b, c3, one_from, one_to: jnp.where(hl == a[0], a[1], jnp.where(hl == b[0], b[1], jnp.where(
                hl == c3[0], c3[1], jnp.where((hl >= one_from) & (hl < one_to), 1.0, 0.0))))
            qx = jnp.where(hl < HEAD_D, qf, pick((67, hi), (68, mid), (69, lo), 64, 67))
            kx = jnp.where(hl < HEAD_D, kf, pick((64, -hi), (65, -mid), (66, -lo), 67, 70))
            qx_ref[:, hd * LANES:(hd + 1) * LANES] = qx.astype(BF16)
            kx_ref[:, hd * LANES:(hd + 1) * LANES] = kx.astype(BF16)

    tok = lambda w: pl.BlockSpec((TB, w), lambda b, j: (b * nb + j, 0))
    outs = [(D, BF16), (512, F32), (512, F32), (512, BF16), (512, BF16), (512, BF16), (LANES, F32),
            (LANES, F32), (FOX_W, F32), (2 * FOX_W, BF16), (2 * FOX_W, BF16)]
    return _call(
        body, comm, name="fwd_in", grid=(B, nb),
        in_specs=[tok(D), _resident(g_mix), _resident(w_int), _resident(w_ft), _resident(bf_pad)],
        out_specs=[tok(w) for w, _ in outs],
        out_shape=[_sds((T, w), dt) for w, dt in outs],
        scratch_shapes=[pltpu.VMEM((8, LANES), F32)],
        args=(x2, g_mix, w_int, w_ft, bf_pad))


def _head_sum(n):
    hc = lax.broadcasted_iota(jnp.int32, (n, n), 1) // HEAD_D
    hr = lax.broadcasted_iota(jnp.int32, (n, n), 0) // HEAD_D
    return (hc == hr).astype(F32)


def _layernorm_silu(y, lg, lb):
    mu = jnp.mean(y, axis=-1, keepdims=True)
    yc = y - mu
    rs = lax.rsqrt(jnp.mean(yc * yc, axis=-1, keepdims=True) + EPS)
    n = yc * rs
    l = n * lg + lb
    return rs, n, l


SUB = 8


def _shifted_copies(cat, sh, rows):
    for r in range(1, SUB):
        sh[r, 0:rows, :] = cat[r:r + rows, :]


def _tap(cat, sh, off, rows, cols=slice(None)):
    r = off % SUB
    return cat[off:off + rows, cols] if r == 0 else sh[r, off - r:off - r + rows, cols]


CONV_ROWS = 128


def _conv_pieces(CB):
    rows = min(CONV_ROWS, CB)
    return [(r0, rows, slice(c0, c0 + LANES)) for c0 in range(0, CONV_CH, LANES) for r0 in range(0, CB, rows)]


def _conv_fwd(u, gt, cw, cb, lng, lnb, B, S, comm=None):
    T = B * S
    CB = min(256, S)
    nb = S // CB

    def body(u_ref, gt_ref, w_ref, cb_ref, lg_ref, lb_ref, y_ref, co_ref, acat, ash):
        j = pl.program_id(1)

        @pl.when(j == 0)
        def _():
            acat[0:CONV_HALO, :] = jnp.zeros((CONV_HALO, CONV_CH), F32)

        acat[CONV_HALO:CONV_HALO + CB, :] = u_ref[...] * _sig(gt_ref[...])
        _shifted_copies(acat, ash, CB + CONV_HALO - SUB)
        for r0, rows, cs in _conv_pieces(CB):
            acc = jnp.zeros((rows, LANES), F32) + cb_ref[:, cs]
            for k in range(CONV_K):
                acc = acc + w_ref[k:k + 1, cs] * _tap(acat, ash, r0 + CONV_HALO - (CONV_K - 1) + k, rows, cs)
            y_ref[r0:r0 + rows, cs] = acc
        acat[0:CONV_HALO, :] = acat[CB:CB + CONV_HALO, :]
        _, _, l = _layernorm_silu(y_ref[...], lg_ref[...], lb_ref[...])
        co_ref[...] = (l * _sig(l)).astype(BF16)

    tok = lambda w: pl.BlockSpec((CB, w), lambda b, j: (b * nb + j, 0))
    return _call(
        body, comm, name="conv_fwd", grid=(B, nb),
        in_specs=[tok(CONV_CH), tok(CONV_CH), _resident(cw), _resident(cb), _resident(lng), _resident(lnb)],
        out_specs=[tok(CONV_CH), tok(CONV_CH)],
        out_shape=[_sds((T, CONV_CH), F32), _sds((T, CONV_CH), BF16)],
        scratch_shapes=[pltpu.VMEM((CONV_HALO + CB, CONV_CH), F32),
                        pltpu.VMEM((SUB, CB + CONV_HALO - SUB, CONV_CH), F32)],
        args=(u, gt, cw, cb, lng, lnb))


def _fox_fwd(qx, kx, v, cq, B, S, comm=None):
    T = B * S
    TQ = min(256, S)
    nq = S // TQ
    one_lane = (HEAD_D, 0)

    def body(qa_ref, qb_ref, ka_ref, kb_ref, v_ref, cq_ref, o_ref, lse_ref, s_scr, s_odd, m_scr, acc_scr):
        i = pl.program_id(2)
        lane = lax.broadcasted_iota(jnp.int32, (TQ, LANES), 1)
        lo = lane < HEAD_D
        qh = (qa_ref[...], qb_ref[...])
        kh = (ka_ref, kb_ref)
        m_scr[...] = jnp.full(m_scr.shape, -1e30, F32)
        acc_scr[...] = jnp.zeros_like(acc_scr)
        row = lax.broadcasted_iota(jnp.int32, (TQ, TQ), 0)
        col = lax.broadcasted_iota(jnp.int32, (TQ, TQ), 1)
        wide = lambda x: jnp.concatenate([x, x], axis=1) if TQ == 2 * LANES else jnp.tile(x, (1, TQ // LANES))

        def scores(j, s_buf):
            start = pl.multiple_of(j * TQ, TQ)
            for h in range(2):
                s_buf[h] = _dot_nt(qh[h], kh[h][pl.ds(start, TQ), :])

        def softmax_step(j, s_buf, diagonal):
            start = pl.multiple_of(j * TQ, TQ)
            vj = v_ref[pl.ds(start, TQ), :]
            for h in range(2):
                def logits():
                    return jnp.where(col <= row, s_buf[h], -1e30) if diagonal else s_buf[h]

                m_old = m_scr[h]
                m_new = jnp.maximum(m_old, jnp.max(logits(), axis=-1, keepdims=True))
                alpha = jnp.exp(m_old - m_new)
                m_scr[h] = m_new
                p = jnp.exp(logits() - wide(m_new)).astype(BF16)
                vx = jnp.where(lane == one_lane[h], jnp.ones_like(vj), jnp.where(lo if h == 0 else ~lo, vj, jnp.zeros_like(vj)))
                acc_scr[h] = alpha * acc_scr[h] + _dot(p, vx)

        def two_blocks(jj, carry):
            j = 2 * jj
            scores(j + 1, s_odd)
            softmax_step(j, s_scr, False)
            scores(j + 2, s_scr)
            softmax_step(j + 1, s_odd, False)
            return carry

        scores(0, s_scr)
        lax.fori_loop(0, i // 2, two_blocks, 0)

        @pl.when(i % 2 == 0)
        def _():
            softmax_step(i, s_scr, True)

        @pl.when(i % 2 == 1)
        def _():
            scores(i, s_odd)
            softmax_step(i - 1, s_scr, False)
            softmax_step(i, s_odd, True)

        acc_a, acc_b = acc_scr[0], acc_scr[1]
        l_a = acc_a[:, one_lane[0]:one_lane[0] + 1]
        l_b = acc_b[:, one_lane[1]:one_lane[1] + 1]
        o_ref[...] = jnp.where(lo, acc_a / l_a, acc_b / l_b)
        lse_ref[...] = cq_ref[...] - jnp.where(lo, m_scr[0] + jnp.log(l_a), m_scr[1] + jnp.log(l_b))

    qspec = pl.BlockSpec((TQ, LANES), lambda b, p, i: (b * nq + i, p))
    kspec = pl.BlockSpec((S, LANES), lambda b, p, i: (b, p))
    qhead = lambda h: pl.BlockSpec((TQ, LANES), lambda b, p, i: (b * nq + i, 2 * p + h))
    khead = lambda h: pl.BlockSpec((S, LANES), lambda b, p, i: (b, 2 * p + h))
    return _call(
        body, comm, name="fox_fwd", grid=(B, N_PAIR, nq),
        in_specs=[qhead(0), qhead(1), khead(0), khead(1), kspec, qspec],
        out_specs=[qspec, qspec],
        out_shape=[_sds((T, FOX_W), F32), _sds((T, FOX_W), F32)],
        scratch_shapes=[pltpu.VMEM((2, TQ, TQ), F32), pltpu.VMEM((2, TQ, TQ), F32),
                        pltpu.VMEM((2, TQ, LANES), F32), pltpu.VMEM((2, TQ, LANES), F32)],
        args=(qx, qx, kx, kx, v, cq))


def _mem_kv(mem2, g_mem, w_mkv, B):
    def body(m_ref, g_ref, w_ref, mn_ref, km_ref, vm_ref):
        _, xh = _rms(m_ref[...])
        mn = (xh * g_ref[...]).astype(BF16)
        mn_ref[...] = mn
        for s in range(2):
            km_ref[:, 512 * s:512 * (s + 1)] = _dot(mn, w_ref[s]).astype(BF16)
            vm_ref[:, 512 * s:512 * (s + 1)] = _dot(mn, w_ref[2 + s]).astype(BF16)

    blk = pl.BlockSpec((MEM_LEN, D), lambda b: (b, 0))
    return pl.pallas_call(
        body, name="mem_kv", grid=(B,),
        in_specs=[blk, _resident(g_mem), _resident(w_mkv)],
        out_specs=[blk, blk, blk],
        out_shape=[_sds((B * MEM_LEN, D), BF16)] * 3,
        compiler_params=_params(1),
    )(mem2, g_mem, w_mkv)


def _mem_probs(qm, km):
    ps = []
    for h in range(MEM_HEADS):
        hs = slice(h * MEM_HD, (h + 1) * MEM_HD)
        lg = _dot_nt(qm[:, hs], km[:, hs]) * (1.0 / math.sqrt(MEM_HD))
        e = jnp.exp(lg - jnp.max(lg, axis=-1, keepdims=True))
        ps.append(e / jnp.sum(e, axis=-1, keepdims=True))
    return ps


def _fwd_mid(x2, co, o, km, vm, w_out, w_mq, w_mo, g_x, B, S, comm=None):
    T = B * S
    TB = min(512, S)
    nb = S // TB

    def body(x_ref, co_ref, o_ref, km_ref, vm_ref, wo_ref, wq_ref, wm_ref, g_ref,
             x1_ref, hx_ref, qm_ref, om_ref, x2_ref, cat_ref):
        cat_ref[:, 0:CONV_CH] = co_ref[...]
        cat_ref[:, CONV_CH:D] = o_ref[...].astype(BF16)
        x1 = x_ref[...] + _dot(cat_ref[...], wo_ref[...])
        x1_ref[...] = x1
        _, xh = _rms(x1)
        hx = (xh * g_ref[...]).astype(BF16)
        hx_ref[...] = hx
        qm = _dot(hx, wq_ref[...]).astype(BF16)
        qm_ref[...] = qm
        ps = _mem_probs(qm, km_ref[...])
        vmv = vm_ref[...]
        for h in range(MEM_HEADS):
            hs = slice(h * MEM_HD, (h + 1) * MEM_HD)
            om_ref[:, hs] = _dot(ps[h].astype(BF16), vmv[:, hs]).astype(BF16)
        x2_ref[...] = x1 + _dot(om_ref[...], wm_ref[...])

    tok = lambda w: pl.BlockSpec((TB, w), lambda b, j: (b * nb + j, 0))
    memb = pl.BlockSpec((MEM_LEN, D), lambda b, j: (b, 0))
    outs = [(D, F32), (D, BF16), (D, BF16), (D, BF16), (D, F32), (D, BF16)]
    return _call(
        body, comm, name="fwd_mid", grid=(B, nb),
        in_specs=[tok(D), tok(CONV_CH), tok(FOX_W), memb, memb, _resident(w_out), _resident(w_mq), _resident(w_mo),
                  _resident(g_x)],
        out_specs=[tok(w) for w, _ in outs],
        out_shape=[_sds((T, w), dt) for w, dt in outs],
        scratch_shapes=[],
        args=(x2, co, o, km, vm, w_out, w_mq, w_mo, g_x))


def _fwd_ffn(x2, tgt, w_gu, w_down, g_ffn, g_final, T):
    TB = min(256, T)
    nb = T // TB

    def body(x_ref, t_ref, wgu_ref, wd_ref, gf_ref, gl_ref, hf_ref, gu_ref, act_ref, dx3_ref, loss_ref, dgl_ref):
        i = pl.program_id(0)

        @pl.when(i == 0)
        def _():
            loss_ref[...] = jnp.zeros_like(loss_ref)
            dgl_ref[...] = jnp.zeros_like(dgl_ref)

        x2v = x_ref[...]
        _, xh = _rms(x2v)
        hf = (xh * gf_ref[...]).astype(BF16)
        hf_ref[...] = hf
        x3 = x2v
        for ch in range(D_FF // FF_CHUNK):
            c0 = ch * FF_CHUNK
            g = _dot(hf, wgu_ref[ch])
            u = _dot(hf, wgu_ref[2 + ch])
            gu_ref[:, c0:c0 + FF_CHUNK] = g
            gu_ref[:, D_FF + c0:D_FF + c0 + FF_CHUNK] = u
            act = (g * _sig(g) * u).astype(BF16)
            act_ref[:, c0:c0 + FF_CHUNK] = act
            x3 = x3 + _dot(act, wd_ref[c0:c0 + FF_CHUNK, :])
        r3, xh3 = _rms(x3)
        gl = gl_ref[...]
        e = xh3 * gl - t_ref[...]
        loss_ref[...] += jnp.sum(e * e) * (0.5 / D)
        dy = e * (1.0 / D)
        dx3, dgl = _rms_bwd(dy, xh3, r3, gl)
        dx3_ref[...] = dx3
        dgl_ref[...] += jnp.sum(dgl, axis=0, keepdims=True)

    tok = lambda w: pl.BlockSpec((TB, w), lambda i: (i, 0))
    return pl.pallas_call(
        body, name="fwd_ffn", grid=(nb,),
        in_specs=[tok(D), tok(D), _resident(w_gu), _resident(w_down), _resident(g_ffn), _resident(g_final)],
        out_specs=[tok(D), tok(2 * D_FF), tok(D_FF), tok(D), _acc_spec((1, LANES)), _acc_spec((1, D))],
        out_shape=[_sds((T, D), BF16), _sds((T, 2 * D_FF), F32), _sds((T, D_FF), BF16), _sds((T, D), F32),
                   _sds((1, LANES), F32), _sds((1, D), F32)],
        compiler_params=_params(1),
    )(x2, tgt, w_gu, w_down, g_ffn, g_final)


def _bwd_ffn(dx3, gu, x2, w_gu, w_down, g_ffn, T):
    TB = min(256, T)
    nb = T // TB

    def body(d_ref, gu_ref, x_ref, wgu_ref, wd_ref, gf_ref, dgu_ref, dx2_ref, dgf_ref):
        i = pl.program_id(0)

        @pl.when(i == 0)
        def _():
            dgf_ref[...] = jnp.zeros_like(dgf_ref)

        dx3v = d_ref[...]
        db = dx3v.astype(BF16)
        dhf = jnp.zeros((TB, D), F32)
        for ch in range(D_FF // FF_CHUNK):
            c0 = ch * FF_CHUNK
            dact = _dot_nt(db, wd_ref[c0:c0 + FF_CHUNK, :])
            g = gu_ref[:, c0:c0 + FF_CHUNK]
            u = gu_ref[:, D_FF + c0:D_FF + c0 + FF_CHUNK]
            sg = _sig(g)
            dg = (dact * u * sg * (1.0 + g * (1.0 - sg))).astype(BF16)
            du = (dact * g * sg).astype(BF16)
            dgu_ref[:, c0:c0 + FF_CHUNK] = dg
            dgu_ref[:, D_FF + c0:D_FF + c0 + FF_CHUNK] = du
            dhf = dhf + _dot_nt(dg, wgu_ref[ch]) + _dot_nt(du, wgu_ref[2 + ch])
        r2, xh2 = _rms(x_ref[...])
        dx, dg_tok = _rms_bwd(dhf, xh2, r2, gf_ref[...])
        dx2_ref[...] = dx3v + dx
        dgf_ref[...] += jnp.sum(dg_tok, axis=0, keepdims=True)

    tok = lambda w: pl.BlockSpec((TB, w), lambda i: (i, 0))
    return pl.pallas_call(
        body, name="bwd_ffn", grid=(nb,),
        in_specs=[tok(D), tok(2 * D_FF), tok(D), _resident(w_gu), _resident(w_down), _resident(g_ffn)],
        out_specs=[tok(2 * D_FF), tok(D), _acc_spec((1, D))],
        out_shape=[_sds((T, 2 * D_FF), BF16), _sds((T, D), F32), _sds((1, D), F32)],
        compiler_params=_params(1),
    )(dx3, gu, x2, w_gu, w_down, g_ffn)


def _bwd_mid(dx2, x1, qm, km, vm, o, w_mo, w_mq, w_out, g_x, B, S, comm=None):
    T = B * S
    TB = min(512, S)
    nb = S // TB
    inv = 1.0 / math.sqrt(MEM_HD)

    def body(d_ref, x1_ref, qm_ref, km_ref, vm_ref, o_ref, wm_ref, wq_ref, wo_ref, g_ref,
             dx1_ref, dqm_ref, dco_ref, do_ref, dd_ref, dkm_ref, dvm_ref, dgx_ref):
        b = pl.program_id(0)
        j = pl.program_id(1)

        @pl.when((b == 0) & (j == 0))
        def _():
            dgx_ref[...] = jnp.zeros_like(dgx_ref)

        @pl.when(j == 0)
        def _():
            dkm_ref[...] = jnp.zeros_like(dkm_ref)
            dvm_ref[...] = jnp.zeros_like(dvm_ref)

        dx2v = d_ref[...]
        dom = _dot_nt(dx2v.astype(BF16), wm_ref[...]).astype(BF16)
        qmv = qm_ref[...]
        kmv = km_ref[...]
        vmv = vm_ref[...]
        ps = _mem_probs(qmv, kmv)
        for h in range(MEM_HEADS):
            hs = slice(h * MEM_HD, (h + 1) * MEM_HD)
            p = ps[h]
            dp = _dot_nt(dom[:, hs], vmv[:, hs])
            ds = (p * (dp - jnp.sum(p * dp, axis=-1, keepdims=True))).astype(BF16)
            dqm_ref[:, hs] = (_dot(ds, kmv[:, hs]) * inv).astype(BF16)
            dkm_ref[:, hs] += _dot_tn(ds, qmv[:, hs]) * inv
            dvm_ref[:, hs] += _dot_tn(p.astype(BF16), dom[:, hs])
        dhx = _dot_nt(dqm_ref[...], wq_ref[...])
        r1, xh1 = _rms(x1_ref[...])
        dx, dg_tok = _rms_bwd(dhx, xh1, r1, g_ref[...])
        dx1 = dx2v + dx
        dx1_ref[...] = dx1
        dgx_ref[...] += jnp.sum(dg_tok, axis=0, keepdims=True)
        d1b = dx1.astype(BF16)
        dco_ref[...] = _dot_nt(d1b, wo_ref[0:CONV_CH, :])
        do = _dot_nt(d1b, wo_ref[CONV_CH:D, :])
        dob = do.astype(BF16)
        do_ref[...] = dob
        dd_ref[...] = _dot_hi(dob.astype(F32) * o_ref[...], _head_sum(FOX_W))

    tok = lambda w: pl.BlockSpec((TB, w), lambda b, j: (b * nb + j, 0))
    memb = pl.BlockSpec((MEM_LEN, D), lambda b, j: (b, 0))
    outs = [(D, F32), (D, BF16), (CONV_CH, F32), (FOX_W, BF16), (FOX_W, F32)]
    return _call(
        body, comm, name="bwd_mid", grid=(B, nb),
        in_specs=[tok(D), tok(D), tok(D), memb, memb, tok(FOX_W), _resident(w_mo), _resident(w_mq), _resident(w_out),
                  _resident(g_x)],
        out_specs=[tok(w) for w, _ in outs] + [memb, memb, _acc_spec((1, D))],
        out_shape=[_sds((T, w), dt) for w, dt in outs] + [_sds((B * MEM_LEN, D), F32)] * 2 + [_sds((1, D), F32)],
        scratch_shapes=[],
        args=(dx2, x1, qm, km, vm, o, w_mo, w_mq, w_out, g_x))


def _mem_bwd(dkm, dvm, mem2, w_mkv, g_mem, B):
    def body(dk_ref, dv_ref, m_ref, w_ref, g_ref, dkv_ref, dg_ref):
        b = pl.program_id(0)

        @pl.when(b == 0)
        def _():
            dg_ref[...] = jnp.zeros_like(dg_ref)

        dk = dk_ref[...].astype(BF16)
        dv = dv_ref[...].astype(BF16)
        dkv_ref[:, 0:D] = dk
        dkv_ref[:, D:2 * D] = dv
        dmn = jnp.zeros((MEM_LEN, D), F32)
        for s in range(2):
            dmn = dmn + _dot_nt(dk[:, 512 * s:512 * (s + 1)], w_ref[s]) + _dot_nt(dv[:, 512 * s:512 * (s + 1)], w_ref[2 + s])
        _, xh = _rms(m_ref[...])
        dg_ref[...] += jnp.sum(dmn * xh, axis=0, keepdims=True)

    blk = pl.BlockSpec((MEM_LEN, D), lambda b: (b, 0))
    return pl.pallas_call(
        body, name="mem_bwd", grid=(B,),
        in_specs=[blk, blk, blk, _resident(w_mkv), _resident(g_mem)],
        out_specs=[pl.BlockSpec((MEM_LEN, 2 * D), lambda b: (b, 0)), _acc_spec((1, D))],
        out_shape=[_sds((B * MEM_LEN, 2 * D), BF16), _sds((1, D), F32)],
        compiler_params=_params(1),
    )(dkm, dvm, mem2, w_mkv, g_mem)


def _fox_bwd(q, k, v, do, bias, dd, ckT, B, S, comm=None):
    T = B * S
    TK = min(256, S)
    nk = S // TK
    scale = 1.0 / math.sqrt(HEAD_D)

    def body(q_ref, k_ref, v_ref, do_ref, bias_ref, dd_ref, ck_ref, dq_ref, dk_ref, dv_ref, dc_ref, dcq_ref,
             dq_acc, rs_acc, s_scr, dp_scr, s_odd, dp_odd, dk_acc, dv_acc, dc_acc):
        j = pl.program_id(2)

        @pl.when(j == 0)
        def _():
            dq_acc[...] = jnp.zeros_like(dq_acc)
            rs_acc[...] = jnp.zeros_like(rs_acc)

        dk_acc[...] = jnp.zeros_like(dk_acc)
        dv_acc[...] = jnp.zeros_like(dv_acc)
        dc_acc[...] = jnp.zeros_like(dc_acc)
        lane = lax.broadcasted_iota(jnp.int32, (TK, LANES), 1)
        lo = lane < HEAD_D
        ks = k_ref[...] * jnp.asarray(scale, BF16)
        v2 = v_ref[...]
        zero = jnp.zeros_like(ks)
        kh = (jnp.where(lo, ks, zero), jnp.where(lo, zero, ks))
        vh = (jnp.where(lo, v2, zero), jnp.where(lo, zero, v2))
        kstart = pl.multiple_of(j * TK, TK)
        ckh = tuple(ck_ref[0, 0, h:h + 1, pl.ds(kstart, TK)] for h in range(2))
        row = lax.broadcasted_iota(jnp.int32, (TK, TK), 0)
        col = lax.broadcasted_iota(jnp.int32, (TK, TK), 1)
        wide = lambda x: jnp.concatenate([x, x], axis=1) if TK == 2 * LANES else jnp.tile(x, (1, TK // LANES))

        def scores(i, s_buf, dp_buf):
            start = pl.multiple_of(i * TK, TK)
            qi = q_ref[pl.ds(start, TK), :]
            doi = do_ref[pl.ds(start, TK), :]
            for h in range(2):
                s_buf[h] = _dot_nt(qi, kh[h])
                dp_buf[h] = _dot_nt(doi, vh[h])

        def grads(i, s_buf, dp_buf, diagonal):
            start = pl.multiple_of(i * TK, TK)
            qi = q_ref[pl.ds(start, TK), :]
            doi = do_ref[pl.ds(start, TK), :]
            bias2 = bias_ref[pl.ds(start, TK), :]
            dd2 = dd_ref[pl.ds(start, TK), :]
            for h in range(2):
                hc = slice(h * HEAD_D, h * HEAD_D + 1)
                bias = jnp.broadcast_to(bias2[:, hc], (TK, LANES))
                ddh = jnp.broadcast_to(dd2[:, hc], (TK, LANES))
                p = jnp.exp((s_buf[h] - ckh[h]) + wide(bias))
                if diagonal:
                    p = jnp.where(col <= row, p, 0.0)
                ds = p * (dp_buf[h] - wide(ddh))
                dc_acc[h, 0:1, :] += jnp.sum(ds, axis=0, keepdims=True)
                rs_acc[h, pl.ds(start, TK), :] += jnp.sum(ds, axis=1, keepdims=True)
                pb = p.astype(BF16)
                dsb = ds.astype(BF16)
                dv_acc[h] += _dot_tn(pb, doi)
                dk_acc[h] += _dot_tn(dsb, qi)
                dq_acc[pl.ds(start, TK), :] += _dot(dsb, kh[h])

        n_off = nk - 1 - j
        block = lambda t: jnp.where(t < n_off, j + 1 + t, j)

        def two_blocks(tt, carry):
            t = 2 * tt
            scores(block(t + 1), s_odd, dp_odd)
            grads(block(t), s_scr, dp_scr, False)
            scores(block(t + 2), s_scr, dp_scr)
            grads(block(t + 1), s_odd, dp_odd, False)
            return carry

        scores(block(0), s_scr, dp_scr)
        lax.fori_loop(0, n_off // 2, two_blocks, 0)

        @pl.when(n_off % 2 == 0)
        def _():
            grads(j, s_scr, dp_scr, True)

        @pl.when(n_off % 2 == 1)
        def _():
            scores(j, s_odd, dp_odd)
            grads(nk - 1, s_scr, dp_scr, False)
            grads(j, s_odd, dp_odd, True)

        dk_ref[...] = (jnp.where(lo, dk_acc[0], dk_acc[1]) * scale).astype(BF16)
        dv_ref[...] = jnp.where(lo, dv_acc[0], dv_acc[1]).astype(BF16)
        sub = lax.broadcasted_iota(jnp.int32, (8, TK), 0)
        dca = dc_acc[0, 0:1, :]
        dcb = dc_acc[1, 0:1, :]
        dc_ref[0, 0] = jnp.where(sub == 0, -dca, jnp.where(sub == 1, -dcb, 0.0))

        @pl.when(j == nk - 1)
        def _():
            dq_ref[...] = dq_acc[...].astype(BF16)
            lo_s = lax.broadcasted_iota(jnp.int32, (S, LANES), 1) < HEAD_D
            dcq_ref[...] = jnp.where(lo_s, rs_acc[0], rs_acc[1])

    full = pl.BlockSpec((S, LANES), lambda b, p, j: (b, p))
    blk = pl.BlockSpec((TK, LANES), lambda b, p, j: (b * nk + j, p))
    return _call(
        body, comm, name="fox_bwd", grid=(B, N_PAIR, nk),
        in_specs=[full, blk, blk, full, full, full, pl.BlockSpec((1, 1, 8, S), lambda b, p, j: (b, p, 0, 0))],
        out_specs=[full, blk, blk, pl.BlockSpec((1, 1, 8, TK), lambda b, p, j: (b, p, 0, j)), full],
        out_shape=[_sds((T, FOX_W), BF16), _sds((T, FOX_W), BF16), _sds((T, FOX_W), BF16),
                   _sds((B, N_PAIR, 8, S), F32), _sds((T, FOX_W), F32)],
        scratch_shapes=[pltpu.VMEM((S, LANES), F32), pltpu.VMEM((2, S, 1), F32),
                        pltpu.VMEM((2, TK, TK), F32), pltpu.VMEM((2, TK, TK), F32),
                        pltpu.VMEM((2, TK, TK), F32), pltpu.VMEM((2, TK, TK), F32),
                        pltpu.VMEM((2, TK, LANES), F32), pltpu.VMEM((2, TK, LANES), F32), pltpu.VMEM((2, 8, TK), F32)],
        args=(q, k, v, do, bias, dd, ckT))


def _fgate_bwd(dc8, zf, B, S):
    T = B * S
    TB = min(512, S)
    nb = S // TB

    def body(dc_ref, zf_ref, dzf_ref, dbf_ref, carry):
        b = pl.program_id(0)
        j = pl.program_id(1)

        @pl.when((b == 0) & (j == 0))
        def _():
            dbf_ref[...] = jnp.zeros_like(dbf_ref)

        @pl.when(j == 0)
        def _():
            carry[...] = jnp.zeros_like(carry)

        dc = dc_ref[...]
        row = lax.broadcasted_iota(jnp.int32, (TB, TB), 0)
        col = lax.broadcasted_iota(jnp.int32, (TB, TB), 1)
        dlogf = _dot_hi((col >= row).astype(F32), dc) + carry[0:1, :]
        carry[0:1, :] = dlogf[0:1, :]
        lane = lax.broadcasted_iota(jnp.int32, dc.shape, 1)
        dzf = jnp.where(lane < 8, dlogf * _sig(-zf_ref[...]), 0.0)
        dzf_ref[...] = dzf.astype(BF16)
        dbf_ref[...] += jnp.sum(dzf, axis=0, keepdims=True)

    tok = pl.BlockSpec((TB, LANES), lambda b, j: (b * nb + (nb - 1 - j), 0))
    return pl.pallas_call(
        body, name="fgate_bwd", grid=(B, nb),
        in_specs=[tok, tok],
        out_specs=[tok, _acc_spec((1, LANES))],
        out_shape=[_sds((T, LANES), BF16), _sds((1, LANES), F32)],
        scratch_shapes=[pltpu.VMEM((8, LANES), F32)],
        compiler_params=_params(2),
    )(dc8, zf)


def _conv_bwd(dco, y, u, gt, cw, lng, lnb, B, S, comm=None):
    T = B * S
    CB = min(256, S)
    nb = S // CB
    hb = CB // CONV_HALO

    def body(dco_ref, y_ref, u_ref, gt_ref, up_ref, gp_ref, w_ref, lg_ref, lb_ref,
             du_ref, dgt_ref, dw_ref, vec_ref, acat, dycat, ash, dysh):
        b = pl.program_id(0)
        j = pl.program_id(1)
        jr = nb - 1 - j

        @pl.when((b == 0) & (j == 0))
        def _():
            dw_ref[...] = jnp.zeros_like(dw_ref)
            vec_ref[...] = jnp.zeros_like(vec_ref)

        @pl.when(j == 0)
        def _():
            dycat[CB:CB + CONV_HALO, :] = jnp.zeros((CONV_HALO, CONV_CH), F32)

        lg = lg_ref[...]
        rs, n, l = _layernorm_silu(y_ref[...], lg, lb_ref[...])
        sg = _sig(l)
        dl = dco_ref[...] * (sg * (1.0 + l * (1.0 - sg)))
        dn = dl * lg
        dy = rs * (dn - jnp.mean(dn, axis=-1, keepdims=True) - n * jnp.mean(dn * n, axis=-1, keepdims=True))
        vec_ref[0:1, :] += jnp.sum(dy, axis=0, keepdims=True)
        vec_ref[1:2, :] += jnp.sum(dl * n, axis=0, keepdims=True)
        vec_ref[2:3, :] += jnp.sum(dl, axis=0, keepdims=True)
        dycat[0:CB, :] = dy
        acat[0:CONV_HALO, :] = jnp.where(jr > 0, up_ref[...] * _sig(gp_ref[...]), 0.0)
        acat[CONV_HALO:CONV_HALO + CB, :] = u_ref[...] * _sig(gt_ref[...])
        _shifted_copies(acat, ash, CB + CONV_HALO - SUB)
        _shifted_copies(dycat, dysh, CB + CONV_HALO - SUB)
        for r0, rows, cs in _conv_pieces(CB):
            dyp = dycat[r0:r0 + rows, cs]
            da = jnp.zeros((rows, LANES), F32)
            for k in range(CONV_K):
                da = da + w_ref[k:k + 1, cs] * _tap(dycat, dysh, r0 + CONV_K - 1 - k, rows, cs)
                dw_ref[k:k + 1, cs] += jnp.sum(dyp * _tap(acat, ash, r0 + CONV_HALO - (CONV_K - 1) + k, rows, cs),
                                               axis=0, keepdims=True)
            uv = u_ref[r0:r0 + rows, cs]
            sgt = _sig(gt_ref[r0:r0 + rows, cs])
            du_ref[r0:r0 + rows, cs] = (da * sgt).astype(BF16)
            dgt_ref[r0:r0 + rows, cs] = (da * uv * sgt * (1.0 - sgt)).astype(BF16)
        dycat[CB:CB + CONV_HALO, :] = dycat[0:CONV_HALO, :]

    tok = lambda w: pl.BlockSpec((CB, w), lambda b, j: (b * nb + (nb - 1 - j), 0))
    prev = pl.BlockSpec((CONV_HALO, CONV_CH), lambda b, j: (jnp.maximum((b * nb + (nb - 1 - j)) * hb - 1, 0), 0))
    return _call(
        body, comm, name="conv_bwd", grid=(B, nb),
        in_specs=[tok(CONV_CH), tok(CONV_CH), tok(CONV_CH), tok(CONV_CH), prev, prev, _resident(cw), _resident(lng),
                  _resident(lnb)],
        out_specs=[tok(CONV_CH), tok(CONV_CH), _acc_spec((CONV_HALO, CONV_CH)), _acc_spec((8, CONV_CH))],
        out_shape=[_sds((T, CONV_CH), BF16), _sds((T, CONV_CH), BF16), _sds((CONV_HALO, CONV_CH), F32),
                   _sds((8, CONV_CH), F32)],
        scratch_shapes=[pltpu.VMEM((CONV_HALO + CB, CONV_CH), F32), pltpu.VMEM((CB + CONV_HALO, CONV_CH), F32),
                        pltpu.VMEM((SUB, CB + CONV_HALO - SUB, CONV_CH), F32),
                        pltpu.VMEM((SUB, CB + CONV_HALO - SUB, CONV_CH), F32)],
        args=(dco, y, u, gt, u, gt, cw, lng, lnb))


def _bwd_in(dz, w_int, w_ft, x2, dx1, g_mix, T, comm=None):
    TB = min(512, T)
    nb = T // TB

    def body(dz_ref, w_ref, wf_ref, x_ref, d1_ref, g_ref, gx_ref, dg_ref):
        i = pl.program_id(0)

        @pl.when(i == 0)
        def _():
            dg_ref[...] = jnp.zeros_like(dg_ref)

        dh = _dot(dz_ref[:, 0:OFF_F], w_ref[0:OFF_F, :]) + _dot(dz_ref[:, OFF_F:D_IN_PAD], wf_ref[...])
        r0, xh0 = _rms(x_ref[...])
        dx, dg_tok = _rms_bwd(dh, xh0, r0, g_ref[...])
        gx_ref[...] = d1_ref[...] + dx
        dg_ref[...] += jnp.sum(dg_tok, axis=0, keepdims=True)

    tok = lambda w: pl.BlockSpec((TB, w), lambda i: (i, 0))
    return _call(
        body, comm, name="bwd_in", grid=(nb,),
        in_specs=[tok(D_IN_PAD), _resident(w_int), _resident(w_ft), tok(D), tok(D), _resident(g_mix)],
        out_specs=[tok(D), _acc_spec((1, D))],
        out_shape=[_sds((T, D), F32), _sds((1, D), F32)],
        scratch_shapes=[],
        args=(dz, w_int, w_ft, x2, dx1, g_mix))


def _dw(a, b, name, tn, slabs=False, tk=None):
    T, K = a.shape
    N = b.shape[1]
    tk = tk or (K if K <= 1024 else K // 2)
    tt = min(1024, T)
    nt = T // tt

    def body(a_ref, b_ref, o_ref, acc):
        t = pl.program_id(2)

        @pl.when(t == 0)
        def _():
            acc[...] = jnp.zeros_like(acc)

        acc[...] += _dot_tn(a_ref[...].astype(BF16), b_ref[...].astype(BF16))

        @pl.when(t == nt - 1)
        def _():
            o_ref[...] = acc[...]

    return pl.pallas_call(
        body, name=name, grid=(K // tk, N // tn, nt),
        in_specs=[pl.BlockSpec((tt, tk), lambda i, j, t: (t, i)), pl.BlockSpec((tt, tn), lambda i, j, t: (t, j))],
        out_specs=(pl.BlockSpec((None, tk, tn), lambda i, j, t: (j, i, 0)) if slabs
                   else pl.BlockSpec((tk, tn), lambda i, j, t: (i, j))),
        out_shape=_sds((N // tn, K, tn) if slabs else (K, N), F32),
        scratch_shapes=[pltpu.VMEM((tk, tn), F32)],
        compiler_params=_params(3),
    )(a, b)


def _pos():
    return lax.axis_index("x"), lax.axis_index("y"), lax.axis_index("c")


def _remote(src, dst, ssem, rsem, to):
    return pltpu.make_async_remote_copy(src_ref=src, dst_ref=dst, send_sem=ssem, recv_sem=rsem, device_id=to,
                                        device_id_type=MESH)


def _split_axis(shape):
    return 0 if shape[0] % 32 == 0 else 1


def _half_shape(shape, parts=2):
    return (shape[0] // parts, shape[1]) if _split_axis(shape) == 0 else (shape[0], shape[1] // parts)


def _half(shape, c):
    R, C = shape
    if _split_axis(shape) == 0:
        return (pl.ds(pl.multiple_of(c * (R // 2), 16), R // 2), slice(None))
    return (slice(None), pl.ds(pl.multiple_of(c * (C // 2), LANES), C // 2))


def _half_block(shape, parts, lead, which):
    blk = _half_shape(shape, parts)
    idx = (which, 0) if _split_axis(shape) == 0 else (0, which)
    return blk, tuple(lead) + idx


class _Comm:
    def __init__(self, ins, out_shapes, sems, start, finish):
        self.ins, self.out_shapes, self.sems, self.start, self.finish = list(ins), list(out_shapes), list(sems), start, finish


def _ag_comm(shards):
    n = len(shards)

    def parts(ins, outs, sems):
        send_sems, recv_sems, local_sems = sems
        x, y, c = _pos()
        me, sib = (x, y, c), (x, y, 1 - c)
        chips = [(1 - x, y), (x, 1 - y), (1 - x, 1 - y)]

        def rows(w, px, py, pc):
            return outs[w].at[(2 * px + py,) + _half(shards[w].shape, pc)]

        def copy(w, k, block, to, src=None):
            return _remote(rows(w, *block) if src is None else src, rows(w, *block), send_sems.at[w, k],
                           recv_sems.at[w, k], to)

        mine, first = [], []
        for w in range(n):
            src = ins[w].at[_half(shards[w].shape, c)]
            mine.append(pltpu.make_async_copy(src, rows(w, *me), local_sems.at[w]))
            first += [copy(w, 0, me, sib, src=src)] + [copy(w, 1 + j, me, (*chip, c), src=src) for j, chip in enumerate(chips)]
        return c, me, sib, chips, copy, mine, first

    def start(ins, outs, sems):
        _, _, _, _, _, mine, first = parts(ins, outs, sems)
        for cp in mine + first:
            cp.start()

    def finish(ins, outs, sems):
        c, me, sib, chips, copy, mine, first = parts(ins, outs, sems)
        passed = []
        for w in range(n):
            for j, chip in enumerate(chips):
                copy(w, 1 + j, (*chip, c), me).wait_recv()
                passed.append(copy(w, 4 + j, (*chip, c), sib))
                passed[-1].start()
        for w in range(n):
            copy(w, 0, sib, me).wait_recv()
            for j, chip in enumerate(chips):
                copy(w, 4 + j, (*chip, 1 - c), me).wait_recv()
        for cp in first + passed:
            cp.wait_send()
        for cp in mine:
            cp.wait()

    D7 = pltpu.SemaphoreType.DMA((n, 7))
    return _Comm(shards, [_sds((4,) + s.shape, s.dtype) for s in shards], [D7, D7, pltpu.SemaphoreType.DMA((n,))],
                 start, finish)


def _sibling_comm(gs):
    n = len(gs)

    def copies(ins, outs, sems):
        send_sems, recv_sems = sems
        x, y, c = _pos()
        return [_remote(ins[w].at[(s,) + _half(gs[w].shape[1:], 1 - c)], outs[w].at[s], send_sems.at[w, s],
                        recv_sems.at[w, s], (x, y, 1 - c)) for w in range(n) for s in range(4)]

    def start(ins, outs, sems):
        for cp in copies(ins, outs, sems):
            cp.start()

    def finish(ins, outs, sems):
        for cp in copies(ins, outs, sems):
            cp.wait()

    D4 = pltpu.SemaphoreType.DMA((n, 4))
    return _Comm(gs, [_sds((4,) + _half_shape(g.shape[1:]), F32) for g in gs], [D4, D4], start, finish)


def _ici_comm(pbs):
    n = len(pbs)

    def copies(ins, outs, sems):
        send_sems, recv_sems = sems
        x, y, c = _pos()
        return [_remote(ins[w].at[2 * tx + ty], outs[w].at[j], send_sems.at[w, j], recv_sems.at[w, j], (tx, ty, c))
                for w in range(n) for j, (tx, ty) in enumerate([(1 - x, y), (x, 1 - y), (1 - x, 1 - y)])]

    def start(ins, outs, sems):
        for cp in copies(ins, outs, sems):
            cp.start()

    def finish(ins, outs, sems):
        for cp in copies(ins, outs, sems):
            cp.wait()

    D3 = pltpu.SemaphoreType.DMA((n, 3))
    return _Comm(pbs, [_sds((3,) + p.shape[1:], BF16) for p in pbs], [D3, D3], start, finish)


def _join(*comms):
    counts = [(len(c.ins), len(c.out_shapes), len(c.sems)) for c in comms]

    def each(which):
        def run(ins, outs, sems):
            i = o = k = 0
            for c, (ni, no, nk) in zip(comms, counts):
                getattr(c, which)(ins[i:i + ni], outs[o:o + no], sems[k:k + nk])
                i, o, k = i + ni, o + no, k + nk
        return run

    return _Comm(sum((c.ins for c in comms), []), sum((c.out_shapes for c in comms), []),
                 sum((c.sems for c in comms), []), each("start"), each("finish"))


def _run_comm(comm, name):
    ni, no = len(comm.ins), len(comm.out_shapes)

    def body(*refs):
        ins, outs, sems = refs[:ni], refs[ni:ni + no], refs[ni + no:]
        comm.start(ins, outs, sems)
        comm.finish(ins, outs, sems)

    return pl.pallas_call(body, name=name, out_shape=comm.out_shapes, in_specs=[ANY] * ni, out_specs=[ANY] * no,
                          scratch_shapes=comm.sems)(*comm.ins)


def _call(body, comm, *, name, grid, in_specs, out_specs, out_shape, scratch_shapes, args):
    n_grid = len(grid)
    if comm is None:
        res = pl.pallas_call(body, name=name, grid=grid, in_specs=in_specs, out_specs=out_specs, out_shape=out_shape,
                             scratch_shapes=scratch_shapes, compiler_params=_params(n_grid))(*args)
        return list(res), []
    n_in, n_out, n_scr = len(in_specs), len(out_specs), len(scratch_shapes)
    ni, no = len(comm.ins), len(comm.out_shapes)

    def carried(*refs):
        ins, refs = refs[:n_in], refs[n_in:]
        cins, refs = refs[:ni], refs[ni:]
        outs, refs = refs[:n_out], refs[n_out:]
        couts, refs = refs[:no], refs[no:]
        scr, csems = refs[:n_scr], refs[n_scr:]
        ids = [pl.program_id(ax) for ax in range(n_grid)]
        first = functools.reduce(jnp.logical_and, [i == 0 for i in ids])
        last = functools.reduce(jnp.logical_and, [i == g - 1 for i, g in zip(ids, grid)])

        @pl.when(first)
        def _():
            comm.start(cins, couts, csems)

        body(*ins, *outs, *scr)

        @pl.when(last)
        def _():
            comm.finish(cins, couts, csems)

    res = pl.pallas_call(
        carried, name=name, grid=grid, in_specs=list(in_specs) + [ANY] * ni, out_specs=list(out_specs) + [ANY] * no,
        out_shape=list(out_shape) + comm.out_shapes, scratch_shapes=list(scratch_shapes) + comm.sems,
        compiler_params=_params(n_grid))(*args, *comm.ins)
    return list(res[:n_out]), list(res[n_out:])


def _sibling_share(gs):
    n = len(gs)

    def body(*refs):
        outs = refs[n:2 * n]
        send_sems, recv_sems = refs[2 * n:]
        x, y, c = _pos()
        cps = []
        for w in range(n):
            mine = outs[w].at[_half(gs[w].shape, c)]
            cps.append(_remote(mine, mine, send_sems.at[w], recv_sems.at[w], (x, y, 1 - c)))
            cps[-1].start()
        for cp in cps:
            cp.wait()

    return pl.pallas_call(
        body, name="rs_share", out_shape=[_sds(g.shape, F32) for g in gs],
        in_specs=[ANY] * n, out_specs=[ANY] * n, input_output_aliases={w: w for w in range(n)},
        scratch_shapes=[pltpu.SemaphoreType.DMA((n,)), pltpu.SemaphoreType.DMA((n,))],
    )(*gs)


def _small_allreduce(v, name):
    P = v.shape[0]
    vm = pl.BlockSpec(memory_space=pltpu.VMEM)

    def body(v_ref, o_ref, gath, send_sems, recv_sems):
        x, y, c = _pos()
        me = 4 * x + 2 * y + c
        gath[me] = v_ref[...]
        cps = []
        for r in range(1, 8):
            tx = (1 - x) if r & 4 else x
            ty = (1 - y) if r & 2 else y
            tc = (1 - c) if r & 1 else c
            cps.append(_remote(v_ref, gath.at[me], send_sems.at[r - 1], recv_sems.at[r - 1], (tx, ty, tc)))
            cps[-1].start()
        for cp in cps:
            cp.wait()
        acc = gath[0]
        for d in range(1, 8):
            acc = acc + gath[d]
        o_ref[...] = acc

    return pl.pallas_call(
        body, name=name, out_shape=_sds((P, LANES), F32), in_specs=[vm], out_specs=vm,
        scratch_shapes=[pltpu.VMEM((8, P, LANES), F32), pltpu.SemaphoreType.DMA((7,)), pltpu.SemaphoreType.DMA((7,))],
    )(v)


def _chip_sum(g, rcv, pos, name):
    shard = g.shape[1:]
    hs = _half_shape(shard)

    def body(pos_ref, g_ref, r_ref, o_ref):
        o_ref[...] = (g_ref[...] + r_ref[...]).astype(BF16)

    return pl.pallas_call(
        body, name=name, out_shape=_sds((4,) + hs, BF16),
        grid_spec=pltpu.PrefetchScalarGridSpec(
            num_scalar_prefetch=1, grid=(4,),
            in_specs=[pl.BlockSpec((1,) + hs, lambda s, pos: _half_block(shard, 2, (s,), pos[0])[1]),
                      pl.BlockSpec((1,) + hs, lambda s, pos: (s, 0, 0))],
            out_specs=pl.BlockSpec((1,) + hs, lambda s, pos: (s, 0, 0))),
        compiler_params=_params(1),
    )(pos, g, rcv)


def _final_sum(g, rcv, rc, pos, name):
    shard = g.shape[1:]
    qs = _half_shape(shard, 4)

    def body(pos_ref, g_ref, r_ref, rc_ref, o_ref):
        acc = g_ref[0] + r_ref[0]
        for j in range(3):
            acc = acc + rc_ref[j].astype(F32)
        o_ref[...] = acc

    return pl.pallas_call(
        body, name=name, out_shape=_sds(shard, F32),
        grid_spec=pltpu.PrefetchScalarGridSpec(
            num_scalar_prefetch=1, grid=(2,),
            in_specs=[pl.BlockSpec((1,) + qs, lambda i, pos: _half_block(shard, 4, (pos[1],), pos[0] * 2 + i)[1]),
                      pl.BlockSpec((1,) + qs, lambda i, pos: _half_block(shard, 4, (pos[1],), i)[1]),
                      pl.BlockSpec((3,) + qs, lambda i, pos: _half_block(shard, 4, (0,), i)[1])],
            out_specs=pl.BlockSpec(qs, lambda i, pos: _half_block(shard, 4, (), pos[0] * 2 + i)[1])),
        compiler_params=_params(1),
    )(pos, g, rcv, rc)


def _adamw_math(w, g, m, v):
    m = ADAM_B1 * m + (1.0 - ADAM_B1) * g
    v = ADAM_B2 * v + (1.0 - ADAM_B2) * (g * g)
    m_hat = m / (1.0 - ADAM_B1 ** ADAM_STEP)
    v_hat = v / (1.0 - ADAM_B2 ** ADAM_STEP)
    delta = -ADAM_LR * (m_hat / (jnp.sqrt(v_hat) + ADAM_EPS) + ADAM_WD * w)
    return delta, m, v


def _adamw(w, g, m, v, name, blk_shape):
    R, C = w.shape

    def body(w_ref, g_ref, m_ref, v_ref, go_ref, d_ref, nm_ref, nv_ref):
        g = g_ref[...]
        d, nm, nv = _adamw_math(w_ref[...], g, m_ref[...], v_ref[...])
        go_ref[...] = g
        d_ref[...] = d
        nm_ref[...] = nm
        nv_ref[...] = nv

    blk = pl.BlockSpec(blk_shape, lambda i, j: (i, j))
    return pl.pallas_call(
        body, name=name, grid=(R // blk_shape[0], C // blk_shape[1]), in_specs=[blk] * 4, out_specs=[blk] * 4,
        out_shape=[_sds((R, C), F32)] * 4, compiler_params=_params(2),
    )(w, g, m, v)


SMALL = (("g_mix", 8), ("b_f", 8), ("conv_w", None), ("conv_b", 8), ("ln_g", 8), ("ln_b", 8), ("g_x", 8), ("g_mem", 8),
         ("g_ffn", 8), ("g_final", 8), ("loss", 8))


def _pack_small(parts, conv_rows):
    rows = []
    for name, n in SMALL:
        if name not in parts:
            continue
        n = conv_rows if n is None else n
        flat = parts[name].reshape(-1).astype(F32)
        flat = jnp.pad(flat, (0, n * LANES - flat.shape[0]))
        rows.append(flat.reshape(n, LANES))
    return jnp.concatenate(rows, axis=0)


def _unpack_small(p, shapes, conv_rows):
    out, off = {}, 0
    for name, n in SMALL:
        if name not in shapes:
            continue
        n = conv_rows if n is None else n
        size = math.prod(shapes[name])
        out[name] = p[off:off + n].reshape(-1)[:size].reshape(shapes[name])
        off += n
    return out


def kernel(x, mem, g_mix, w_in, b_f, conv_w, conv_b, ln_g, ln_b, w_out, g_x, g_mem, w_mq, w_mkv, w_mo, g_ffn, w_gu, w_down, g_final, loss_target, m_g_mix, m_w_in, m_b_f, m_conv_w, m_conv_b, m_ln_g, m_ln_b, m_w_out, m_g_x, m_g_mem, m_w_mq, m_w_mkv, m_w_mo, m_g_ffn, m_w_gu, m_w_down, m_g_final, v_g_mix, v_w_in, v_b_f, v_conv_w, v_conv_b, v_ln_g, v_ln_b, v_w_out, v_g_x, v_g_mem, v_w_mq, v_w_mkv, v_w_mo, v_g_ffn, v_w_gu, v_w_down, v_g_final):
    names = ["g_mix", "w_in", "b_f", "conv_w", "conv_b", "ln_g", "ln_b", "w_out", "g_x", "g_mem", "w_mq", "w_mkv",
             "w_mo", "g_ffn", "w_gu", "w_down", "g_final"]
    W = dict(zip(names, [g_mix, w_in, b_f, conv_w, conv_b, ln_g, ln_b, w_out, g_x, g_mem, w_mq, w_mkv, w_mo, g_ffn,
                         w_gu, w_down, g_final]))
    M = dict(zip(names, [m_g_mix, m_w_in, m_b_f, m_conv_w, m_conv_b, m_ln_g, m_ln_b, m_w_out, m_g_x, m_g_mem, m_w_mq,
                         m_w_mkv, m_w_mo, m_g_ffn, m_w_gu, m_w_down, m_g_final]))
    V = dict(zip(names, [v_g_mix, v_w_in, v_b_f, v_conv_w, v_conv_b, v_ln_g, v_ln_b, v_w_out, v_g_x, v_g_mem, v_w_mq,
                         v_w_mkv, v_w_mo, v_g_ffn, v_w_gu, v_w_down, v_g_final]))
    big_names = [n for n, _, _, _ in BIG]
    B, S, _ = x.shape
    T = B * S
    mx, my, mc = _pos()
    chip = 2 * mx + my
    pos = jnp.stack([mc, chip]).astype(jnp.int32)

    shard2d = lambda a: a.reshape(a.shape[-2], a.shape[-1])
    big2d = lambda d, n: shard2d(d[n]).T if n == "w_in" else shard2d(d[n])
    shard_bf = {n: big2d(W, n).astype(BF16) for n in big_names}
    ag_mid = ["w_mkv", "w_out", "w_mq", "w_mo"]
    ag_ffn = ["w_gu", "w_down"]
    slab = {"w_in": _run_comm(_ag_comm([shard_bf["w_in"]]), "ag_w_in")[0]}
    w_int = slab["w_in"].reshape(D_IN, D)
    w_ft = jnp.pad(w_int[OFF_F:D_IN], ((0, D_IN_PAD - D_IN), (0, 0)))
    cw_mine = jnp.pad(shard2d(conv_w), ((0, 1), (0, 0)))
    cw_slot = lax.dynamic_update_slice(jnp.zeros((CONV_HALO, CONV_CH), F32), cw_mine, (0, chip * LANES))
    cw = _small_allreduce(cw_slot.reshape(CONV_HALO * 4, LANES) * 0.5, "gather_conv_w").reshape(CONV_HALO, CONV_CH)

    row = lambda a: a.reshape(1, -1)
    bf_pad = jnp.pad(row(b_f), ((0, 0), (0, LANES - 8)))
    x2d = x.reshape(T, D)
    mem2d = mem.reshape(B * MEM_LEN, D)
    tgt = loss_target.reshape(T, D)

    (h, u, gt, q, k, v, zf, c, cq, qx, kx), got = _fwd_in(x2d, row(g_mix), w_int, w_ft, bf_pad, B, S,
                                                  comm=_ag_comm([shard_bf[n] for n in ag_mid[:2]]))
    slab.update(zip(ag_mid[:2], got))
    ckT = jnp.transpose(c.reshape(B, S, LANES)[:, :, :8], (0, 2, 1)).reshape(B, N_PAIR, 2, S)
    ckT = jnp.pad(ckT, ((0, 0), (0, 0), (0, 6), (0, 0)))
    (y, co), got = _conv_fwd(u, gt, cw, row(conv_b), row(ln_g), row(ln_b), B, S,
                             comm=_ag_comm([shard_bf[n] for n in ag_mid[2:]]))
    slab.update(zip(ag_mid[2:], got))
    (o, fox_bias), got = _fox_fwd(qx, kx, v, cq, B, S, comm=_ag_comm([shard_bf[n] for n in ag_ffn[:1]]))
    slab.update(zip(ag_ffn[:1], got))
    full = {n: slab[n] if by_col else slab[n].reshape(4 * r, c) for n, r, c, by_col in BIG if n in slab}
    mn, km, vm = _mem_kv(mem2d, row(g_mem), full["w_mkv"], B)
    (x1, hx, qm, om, x2, cat), got = _fwd_mid(x2d, co, o, km, vm, full["w_out"], full["w_mq"], full["w_mo"], row(g_x),
                                             B, S, comm=_ag_comm([shard_bf[n] for n in ag_ffn[1:]]))
    full["w_down"] = got[0].reshape(D_FF, D)
    hf, gu, act, dx3, loss_p, dg_final = _fwd_ffn(x2, tgt, full["w_gu"], full["w_down"], row(g_ffn), row(g_final), T)

    pos_sum = lambda gs, rcvs, ns: [_chip_sum(g, r, pos, "rs_chip_sum_" + n) for g, r, n in zip(gs, rcvs, ns)]
    fin_sum = lambda gs, rcvs, rcs, ns: [_final_sum(g, r, q3, pos, "rs_final_sum_" + n)
                                         for g, r, q3, n in zip(gs, rcvs, rcs, ns)]
    RH = {}
    dgu, dx2, dg_ffn = _bwd_ffn(dx3, gu, x2, full["w_gu"], full["w_down"], row(g_ffn), T)
    g_ffn_w = [_dw(hf, dgu, "dw_gu", FF_CHUNK, slabs=True), _dw(act, dx3, "dw_down", 512).reshape(4, D_FF // 4, D)]
    (dx1, dqm, dco, do, dd, dkm, dvm, dg_x), rcv_ffn = _bwd_mid(dx2, x1, qm, km, vm, o, full["w_mo"], full["w_mq"],
                                                                full["w_out"], row(g_x), B, S, comm=_sibling_comm(g_ffn_w))
    pb_ffn = pos_sum(g_ffn_w, rcv_ffn, ag_ffn)
    dkv, dg_mem = _mem_bwd(dkm, dvm, mem2d, full["w_mkv"], row(g_mem), B)
    g_mid_w = [_dw(mn, dkv, "dw_mkv", 512, slabs=True), _dw(cat, dx1, "dw_out", 512).reshape(4, 256, D),
               _dw(hx, dqm, "dw_mq", 512).reshape(4, 256, D), _dw(om, dx2, "dw_mo", 512).reshape(4, 256, D)]
    (dq, dk, dv, dc, dcq), got = _fox_bwd(q, k, v, do, fox_bias, dd, ckT, B, S,
                                          comm=_join(_ici_comm(pb_ffn), _sibling_comm(g_mid_w)))
    rc_ffn, rcv_mid = got[:len(pb_ffn)], got[len(pb_ffn):]
    RH.update(zip(ag_ffn, fin_sum(g_ffn_w, rcv_ffn, rc_ffn, ag_ffn)))
    pb_mid = pos_sum(g_mid_w, rcv_mid, ag_mid)
    dc8 = jnp.transpose(dc[:, :, :2, :].reshape(B, 8, S), (0, 2, 1)).reshape(T, 8)
    dc8 = dc8 + dcq.reshape(T, 8, HEAD_D)[:, :, 0]
    dzf, dbf = _fgate_bwd(jnp.pad(dc8, ((0, 0), (0, LANES - 8))), zf, B, S)
    (du, dgt, dcw, dvec), rc_mid = _conv_bwd(dco, y, u, gt, cw, row(ln_g), row(ln_b), B, S, comm=_ici_comm(pb_mid))
    RH.update(zip(ag_mid, fin_sum(g_mid_w, rcv_mid, rc_mid, ag_mid)))
    dz = jnp.concatenate([du, dgt, dq, dk, dv, dzf], axis=1)
    g_in_w = [_dw(dz, h, "dw_in", 512, tk=D_IN_PAD // 3)[:D_IN].reshape(4, D_IN // 4, D)]
    rcv_in = _run_comm(_sibling_comm(g_in_w), "rs_sibling_in")
    (grad_x, dg_mix), rc_in = _bwd_in(dz, w_int, w_ft, x2d, dx1, row(g_mix), T,
                                      comm=_ici_comm(pos_sum(g_in_w, rcv_in, ["w_in"])))
    RH.update(zip(["w_in"], fin_sum(g_in_w, rcv_in, rc_in, ["w_in"])))
    shared = dict(zip(big_names, _sibling_share([RH[n] for n in big_names])))
    G, DL, NM, NV = {}, {}, {}, {}
    for n in big_names:
        G[n], DL[n], NM[n], NV[n] = _adamw(big2d(W, n), shared[n], big2d(M, n), big2d(V, n), "adamw_" + n,
                                           _half_shape(shared[n].shape))

    small_g = {"g_mix": dg_mix, "b_f": dbf[:, :8], "conv_w": dcw, "conv_b": dvec[0], "ln_g": dvec[1], "ln_b": dvec[2],
               "g_x": dg_x, "g_mem": dg_mem, "g_ffn": dg_ffn, "g_final": dg_final, "loss": loss_p[:, :1]}
    sg = _small_allreduce(_pack_small(small_g, CONV_HALO * 4), "allreduce_small")
    shapes = {n: W[n].shape for n in names if n not in big_names}
    shapes["conv_w"] = (CONV_HALO, CONV_CH)
    shapes["loss"] = (1,)
    sgrads = _unpack_small(sg, shapes, CONV_HALO * 4)
    loss = sgrads.pop("loss")[0]
    sgrads["conv_w"] = lax.dynamic_slice(sgrads["conv_w"], (0, chip * LANES), (CONV_K, LANES)).reshape(W["conv_w"].shape)
    spack = lambda d: _pack_small({n: d[n] for n in sgrads}, CONV_HALO)
    _, sd, snm, snv = _adamw(spack(W), spack(sgrads), spack(M), spack(V), "adamw_small", (8, LANES))
    sshapes = {n: W[n].shape for n in sgrads}
    SD, SNM, SNV = (_unpack_small(a, sshapes, CONV_HALO) for a in (sd, snm, snv))

    def collect(bigs, smalls):
        back = lambda n: (bigs[n].T if n == "w_in" else bigs[n]).reshape(W[n].shape)
        return [back(n) if n in big_names else smalls[n] for n in names]

    return (loss, grad_x.reshape(x.shape), *collect(G, sgrads), *collect(DL, SD), *collect(NM, SNM), *collect(NV, SNV))
```

```python
import functools
import math

import jax
import jax.numpy as jnp
from jax import lax
from jax.experimental import pallas as pl
from jax.experimental.pallas import tpu as pltpu

F32, BF16 = jnp.float32, jnp.bfloat16
HIGHEST = lax.Precision.HIGHEST
MESH = pl.DeviceIdType.MESH

D = 1024
CONV_CH = 512
CONV_K = 31
CONV_HALO = 32
FOX_W = 512
HEAD_D = 64
N_PAIR = 4
MEM_LEN = 256
MEM_HEADS = 4
MEM_HD = 256
D_FF = 2816
FF_CHUNK = 1408
D_IN = 2568
D_IN_PAD = 2688
OFF_F = 2560
EPS = 1e-6
LANES = 128

ADAM_LR, ADAM_B1, ADAM_B2, ADAM_EPS, ADAM_WD, ADAM_STEP = 0.001, 0.9, 0.999, 1e-08, 0.01, 10

VMEM_LIMIT = 60 * 1024 * 1024

BIG = (("w_out", 256, 1024, False), ("w_mq", 256, 1024, False), ("w_mkv", 1024, 512, True),
       ("w_mo", 256, 1024, False), ("w_gu", 1024, 1408, True), ("w_down", 704, 1024, False),
       ("w_in", 642, 1024, False))

ANY = pl.BlockSpec(memory_space=pl.ANY)


def _sig(x):
    return 1.0 / (1.0 + jnp.exp(-x))


def _dot(a, b):
    return jnp.dot(a, b, preferred_element_type=F32)


def _dot_nt(a, b):
    return lax.dot_general(a, b, (((1,), (1,)), ((), ())), preferred_element_type=F32)


def _dot_tn(a, b):
    return lax.dot_general(a, b, (((0,), (0,)), ((), ())), preferred_element_type=F32)


def _dot_hi(a, b):
    return jnp.dot(a, b, precision=HIGHEST, preferred_element_type=F32)


def _resident(a):
    nd = a.ndim
    return pl.BlockSpec(a.shape, lambda *_: (0,) * nd, pipeline_mode=pl.Buffered(1))


def _acc_spec(shape):
    nd = len(shape)
    return pl.BlockSpec(shape, lambda *_: (0,) * nd)


def _params(n_grid):
    return pltpu.CompilerParams(dimension_semantics=("arbitrary",) * n_grid, vmem_limit_bytes=VMEM_LIMIT)


def _sds(shape, dtype):
    return jax.ShapeDtypeStruct(shape, dtype)


def _rms(x):
    r = lax.rsqrt(jnp.mean(x * x, axis=-1, keepdims=True) + EPS)
    return r, x * r


def _rms_bwd(dy, xh, r, g):
    dxh = dy * g
    dx = r * (dxh - xh * jnp.mean(dxh * xh, axis=-1, keepdims=True))
    return dx, dy * xh


def _head_expand(rows, cols):
    hd = lax.broadcasted_iota(jnp.int32, (rows, cols), 1) // HEAD_D
    hr = lax.broadcasted_iota(jnp.int32, (rows, cols), 0)
    return (hd == hr).astype(F32)


def _fwd_in(x2, g_mix, w_int, w_ft, bf_pad, B, S, comm=None):
    T = B * S
    TB = min(512, S)
    nb = S // TB

    def body(x_ref, g_ref, w_ref, wf_ref, bf_ref, h_ref, u_ref, gt_ref, q_ref, k_ref, v_ref, zf_ref, c_ref, cq_ref,
             qx_ref, kx_ref, carry):
        j = pl.program_id(1)

        @pl.when(j == 0)
        def _():
            carry[...] = jnp.zeros_like(carry)

        _, xh = _rms(x_ref[...])
        h = (xh * g_ref[...]).astype(BF16)
        h_ref[...] = h
        u_ref[...] = _dot_nt(h, w_ref[0:512, :])
        gt_ref[...] = _dot_nt(h, w_ref[512:1024, :])
        qb = _dot_nt(h, w_ref[1024:1536, :]).astype(BF16)
        kb = _dot_nt(h, w_ref[1536:2048, :]).astype(BF16)
        q_ref[...] = qb
        k_ref[...] = kb
        v_ref[...] = _dot_nt(h, w_ref[2048:2560, :]).astype(BF16)
        zf = _dot_nt(h, wf_ref[...]) + bf_ref[...]
        zf_ref[...] = zf
        lane = lax.broadcasted_iota(jnp.int32, zf.shape, 1)
        logf = jnp.where(lane < 8, jnp.minimum(zf, 0.0) - jnp.log(1.0 + jnp.exp(-jnp.abs(zf))), 0.0)
        row = lax.broadcasted_iota(jnp.int32, (TB, TB), 0)
        col = lax.broadcasted_iota(jnp.int32, (TB, TB), 1)
        c = _dot_hi((row >= col).astype(F32), logf) + carry[0:1, :]
        carry[0:1, :] = c[TB - 1:TB, :]
        c_ref[...] = c
        cq = _dot_hi(c, _head_expand(LANES, FOX_W))
        cq_ref[...] = cq
        hl = lax.broadcasted_iota(jnp.int32, (TB, LANES), 1)
        for hd in range(2 * N_PAIR):
            grp = slice((hd // 2) * LANES, (hd // 2 + 1) * LANES)
            swap = (lambda t: t) if hd % 2 == 0 else (lambda t: pltpu.roll(t, HEAD_D, 1))
            qf = swap(qb[:, grp].astype(F32) * (1.0 / math.sqrt(HEAD_D)))
            kf = swap(kb[:, grp].astype(F32))
            cv = cq[:, grp] if hd % 2 == 1 else pltpu.roll(cq[:, grp], HEAD_D, 1)
            hi = cv.astype(BF16).astype(F32)
            mid = (cv - hi).astype(BF16).astype(F32)
            lo = (cv - hi - mid).astype(BF16).astype(F32)
            pick = lambda a, b, c3, one_from, one_to: jnp.where(hl == a[0], a[1], jnp.where(hl == b[0], b[1], jnp.where(
                hl == c3[0], c3[1], jnp.where((hl >= one_from) & (hl < one_to), 1.0, 0.0))))
            qx = jnp.where(hl < HEAD_D, qf, pick((67, hi), (68, mid), (69, lo), 64, 67))
            kx = jnp.where(hl < HEAD_D, kf, pick((64, -hi), (65, -mid), (66, -lo), 67, 70))
            qx_ref[:, hd * LANES:(hd + 1) * LANES] = qx.astype(BF16)
            kx_ref[:, hd * LANES:(hd + 1) * LANES] = kx.astype(BF16)

    tok = lambda w: pl.BlockSpec((TB, w), lambda b, j: (b * nb + j, 0))
    outs = [(D, BF16), (512, F32), (512, F32), (512, BF16), (512, BF16), (512, BF16), (LANES, F32),
            (LANES, F32), (FOX_W, F32), (2 * FOX_W, BF16), (2 * FOX_W, BF16)]
    return _call(
        body, comm, name="fwd_in", grid=(B, nb),
        in_specs=[tok(D), _resident(g_mix), _resident(w_int), _resident(w_ft), _resident(bf_pad)],
        out_specs=[tok(w) for w, _ in outs],
        out_shape=[_sds((T, w), dt) for w, dt in outs],
        scratch_shapes=[pltpu.VMEM((8, LANES), F32)],
        args=(x2, g_mix, w_int, w_ft, bf_pad))


def _head_sum(n):
    hc = lax.broadcasted_iota(jnp.int32, (n, n), 1) // HEAD_D
    hr = lax.broadcasted_iota(jnp.int32, (n, n), 0) // HEAD_D
    return (hc == hr).astype(F32)


def _layernorm_silu(y, lg, lb):
    mu = jnp.mean(y, axis=-1, keepdims=True)
    yc = y - mu
    rs = lax.rsqrt(jnp.mean(yc * yc, axis=-1, keepdims=True) + EPS)
    n = yc * rs
    l = n * lg + lb
    return rs, n, l


SUB = 8


def _shifted_copies(cat, sh, rows):
    for r in range(1, SUB):
        sh[r, 0:rows, :] = cat[r:r + rows, :]


def _tap(cat, sh, off, rows, cols=slice(None)):
    r = off % SUB
    return cat[off:off + rows, cols] if r == 0 else sh[r, off - r:off - r + rows, cols]


CONV_ROWS = 128


def _conv_pieces(CB):
    rows = min(CONV_ROWS, CB)
    return [(r0, rows, slice(c0, c0 + LANES)) for c0 in range(0, CONV_CH, LANES) for r0 in range(0, CB, rows)]


def _conv_fwd(u, gt, cw, cb, lng, lnb, B, S, comm=None):
    T = B * S
    CB = min(256, S)
    nb = S // CB

    def body(u_ref, gt_ref, w_ref, cb_ref, lg_ref, lb_ref, y_ref, co_ref, acat, ash):
        j = pl.program_id(1)

        @pl.when(j == 0)
        def _():
            acat[0:CONV_HALO, :] = jnp.zeros((CONV_HALO, CONV_CH), F32)

        acat[CONV_HALO:CONV_HALO + CB, :] = u_ref[...] * _sig(gt_ref[...])
        _shifted_copies(acat, ash, CB + CONV_HALO - SUB)
        for r0, rows, cs in _conv_pieces(CB):
            acc = jnp.zeros((rows, LANES), F32) + cb_ref[:, cs]
            for k in range(CONV_K):
                acc = acc + w_ref[k:k + 1, cs] * _tap(acat, ash, r0 + CONV_HALO - (CONV_K - 1) + k, rows, cs)
            y_ref[r0:r0 + rows, cs] = acc
        acat[0:CONV_HALO, :] = acat[CB:CB + CONV_HALO, :]
        _, _, l = _layernorm_silu(y_ref[...], lg_ref[...], lb_ref[...])
        co_ref[...] = (l * _sig(l)).astype(BF16)

    tok = lambda w: pl.BlockSpec((CB, w), lambda b, j: (b * nb + j, 0))
    return _call(
        body, comm, name="conv_fwd", grid=(B, nb),
        in_specs=[tok(CONV_CH), tok(CONV_CH), _resident(cw), _resident(cb), _resident(lng), _resident(lnb)],
        out_specs=[tok(CONV_CH), tok(CONV_CH)],
        out_shape=[_sds((T, CONV_CH), F32), _sds((T, CONV_CH), BF16)],
        scratch_shapes=[pltpu.VMEM((CONV_HALO + CB, CONV_CH), F32),
                        pltpu.VMEM((SUB, CB + CONV_HALO - SUB, CONV_CH), F32)],
        args=(u, gt, cw, cb, lng, lnb))


def _fox_fwd(qx, kx, v, cq, B, S, comm=None):
    T = B * S
    TQ = min(256, S)
    nq = S // TQ
    one_lane = (HEAD_D, 0)

    def body(qa_ref, qb_ref, ka_ref, kb_ref, v_ref, cq_ref, o_ref, lse_ref, s_scr, s_odd, m_scr, acc_scr):
        i = pl.program_id(2)
        lane = lax.broadcasted_iota(jnp.int32, (TQ, LANES), 1)
        lo = lane < HEAD_D
        qh = (qa_ref[...], qb_ref[...])
        kh = (ka_ref, kb_ref)
        m_scr[...] = jnp.full(m_scr.shape, -1e30, F32)
        acc_scr[...] = jnp.zeros_like(acc_scr)
        row = lax.broadcasted_iota(jnp.int32, (TQ, TQ), 0)
        col = lax.broadcasted_iota(jnp.int32, (TQ, TQ), 1)
        wide = lambda x: jnp.concatenate([x, x], axis=1) if TQ == 2 * LANES else jnp.tile(x, (1, TQ // LANES))

        def scores(j, s_buf):
            start = pl.multiple_of(j * TQ, TQ)
            for h in range(2):
                s_buf[h] = _dot_nt(qh[h], kh[h][pl.ds(start, TQ), :])

        def softmax_step(j, s_buf, diagonal):
            start = pl.multiple_of(j * TQ, TQ)
            vj = v_ref[pl.ds(start, TQ), :]
            for h in range(2):
                def logits():
                    return jnp.where(col <= row, s_buf[h], -1e30) if diagonal else s_buf[h]

                m_old = m_scr[h]
                m_new = jnp.maximum(m_old, jnp.max(logits(), axis=-1, keepdims=True))
                alpha = jnp.exp(m_old - m_new)
                m_scr[h] = m_new
                p = jnp.exp(logits() - wide(m_new)).astype(BF16)
                vx = jnp.where(lane == one_lane[h], jnp.ones_like(vj), jnp.where(lo if h == 0 else ~lo, vj, jnp.zeros_like(vj)))
                acc_scr[h] = alpha * acc_scr[h] + _dot(p, vx)

        def two_blocks(jj, carry):
            j = 2 * jj
            scores(j + 1, s_odd)
            softmax_step(j, s_scr, False)
            scores(j + 2, s_scr)
            softmax_step(j + 1, s_odd, False)
            return carry

        scores(0, s_scr)
        lax.fori_loop(0, i // 2, two_blocks, 0)

        @pl.when(i % 2 == 0)
        def _():
            softmax_step(i, s_scr, True)

        @pl.when(i % 2 == 1)
        def _():
            scores(i, s_odd)
            softmax_step(i - 1, s_scr, False)
            softmax_step(i, s_odd, True)

        acc_a, acc_b = acc_scr[0], acc_scr[1]
        l_a = acc_a[:, one_lane[0]:one_lane[0] + 1]
        l_b = acc_b[:, one_lane[1]:one_lane[1] + 1]
        o_ref[...] = jnp.where(lo, acc_a / l_a, acc_b / l_b)
        lse_ref[...] = cq_ref[...] - jnp.where(lo, m_scr[0] + jnp.log(l_a), m_scr[1] + jnp.log(l_b))

    qspec = pl.BlockSpec((TQ, LANES), lambda b, p, i: (b * nq + i, p))
    kspec = pl.BlockSpec((S, LANES), lambda b, p, i: (b, p))
    qhead = lambda h: pl.BlockSpec((TQ, LANES), lambda b, p, i: (b * nq + i, 2 * p + h))
    khead = lambda h: pl.BlockSpec((S, LANES), lambda b, p, i: (b, 2 * p + h))
    return _call(
        body, comm, name="fox_fwd", grid=(B, N_PAIR, nq),
        in_specs=[qhead(0), qhead(1), khead(0), khead(1), kspec, qspec],
        out_specs=[qspec, qspec],
        out_shape=[_sds((T, FOX_W), F32), _sds((T, FOX_W), F32)],
        scratch_shapes=[pltpu.VMEM((2, TQ, TQ), F32), pltpu.VMEM((2, TQ, TQ), F32),
                        pltpu.VMEM((2, TQ, LANES), F32), pltpu.VMEM((2, TQ, LANES), F32)],
        args=(qx, qx, kx, kx, v, cq))


def _mem_kv(mem2, g_mem, w_mkv, B):
    def body(m_ref, g_ref, w_ref, mn_ref, km_ref, vm_ref):
        _, xh = _rms(m_ref[...])
        mn = (xh * g_ref[...]).astype(BF16)
        mn_ref[...] = mn
        for s in range(2):
            km_ref[:, 512 * s:512 * (s + 1)] = _dot(mn, w_ref[s]).astype(BF16)
            vm_ref[:, 512 * s:512 * (s + 1)] = _dot(mn, w_ref[2 + s]).astype(BF16)

    blk = pl.BlockSpec((MEM_LEN, D), lambda b: (b, 0))
    return pl.pallas_call(
        body, name="mem_kv", grid=(B,),
        in_specs=[blk, _resident(g_mem), _resident(w_mkv)],
        out_specs=[blk, blk, blk],
        out_shape=[_sds((B * MEM_LEN, D), BF16)] * 3,
        compiler_params=_params(1),
    )(mem2, g_mem, w_mkv)


def _mem_probs(qm, km):
    ps = []
    for h in range(MEM_HEADS):
        hs = slice(h * MEM_HD, (h + 1) * MEM_HD)
        lg = _dot_nt(qm[:, hs], km[:, hs]) * (1.0 / math.sqrt(MEM_HD))
        e = jnp.exp(lg - jnp.max(lg, axis=-1, keepdims=True))
        ps.append(e / jnp.sum(e, axis=-1, keepdims=True))
    return ps


def _fwd_mid(x2, co, o, km, vm, w_out, w_mq, w_mo, g_x, B, S, comm=None):
    T = B * S
    TB = min(512, S)
    nb = S // TB

    def body(x_ref, co_ref, o_ref, km_ref, vm_ref, wo_ref, wq_ref, wm_ref, g_ref,
             x1_ref, hx_ref, qm_ref, om_ref, x2_ref, cat_ref):
        cat_ref[:, 0:CONV_CH] = co_ref[...]
        cat_ref[:, CONV_CH:D] = o_ref[...].astype(BF16)
        x1 = x_ref[...] + _dot(cat_ref[...], wo_ref[...])
        x1_ref[...] = x1
        _, xh = _rms(x1)
        hx = (xh * g_ref[...]).astype(BF16)
        hx_ref[...] = hx
        qm = _dot(hx, wq_ref[...]).astype(BF16)
        qm_ref[...] = qm
        ps = _mem_probs(qm, km_ref[...])
        vmv = vm_ref[...]
        for h in range(MEM_HEADS):
            hs = slice(h * MEM_HD, (h + 1) * MEM_HD)
            om_ref[:, hs] = _dot(ps[h].astype(BF16), vmv[:, hs]).astype(BF16)
        x2_ref[...] = x1 + _dot(om_ref[...], wm_ref[...])

    tok = lambda w: pl.BlockSpec((TB, w), lambda b, j: (b * nb + j, 0))
    memb = pl.BlockSpec((MEM_LEN, D), lambda b, j: (b, 0))
    outs = [(D, F32), (D, BF16), (D, BF16), (D, BF16), (D, F32), (D, BF16)]
    return _call(
        body, comm, name="fwd_mid", grid=(B, nb),
        in_specs=[tok(D), tok(CONV_CH), tok(FOX_W), memb, memb, _resident(w_out), _resident(w_mq), _resident(w_mo),
                  _resident(g_x)],
        out_specs=[tok(w) for w, _ in outs],
        out_shape=[_sds((T, w), dt) for w, dt in outs],
        scratch_shapes=[],
        args=(x2, co, o, km, vm, w_out, w_mq, w_mo, g_x))


def _fwd_ffn(x2, tgt, w_gu, w_down, g_ffn, g_final, T):
    TB = min(256, T)
    nb = T // TB

    def body(x_ref, t_ref, wgu_ref, wd_ref, gf_ref, gl_ref, hf_ref, gu_ref, act_ref, dx3_ref, loss_ref, dgl_ref):
        i = pl.program_id(0)

        @pl.when(i == 0)
        def _():
            loss_ref[...] = jnp.zeros_like(loss_ref)
            dgl_ref[...] = jnp.zeros_like(dgl_ref)

        x2v = x_ref[...]
        _, xh = _rms(x2v)
        hf = (xh * gf_ref[...]).astype(BF16)
        hf_ref[...] = hf
        x3 = x2v
        for ch in range(D_FF // FF_CHUNK):
            c0 = ch * FF_CHUNK
            g = _dot(hf, wgu_ref[ch])
            u = _dot(hf, wgu_ref[2 + ch])
            gu_ref[:, c0:c0 + FF_CHUNK] = g
            gu_ref[:, D_FF + c0:D_FF + c0 + FF_CHUNK] = u
            act = (g * _sig(g) * u).astype(BF16)
            act_ref[:, c0:c0 + FF_CHUNK] = act
            x3 = x3 + _dot(act, wd_ref[c0:c0 + FF_CHUNK, :])
        r3, xh3 = _rms(x3)
        gl = gl_ref[...]
        e = xh3 * gl - t_ref[...]
        loss_ref[...] += jnp.sum(e * e) * (0.5 / D)
        dy = e * (1.0 / D)
        dx3, dgl = _rms_bwd(dy, xh3, r3, gl)
        dx3_ref[...] = dx3
        dgl_ref[...] += jnp.sum(dgl, axis=0, keepdims=True)

    tok = lambda w: pl.BlockSpec((TB, w), lambda i: (i, 0))
    return pl.pallas_call(
        body, name="fwd_ffn", grid=(nb,),
        in_specs=[tok(D), tok(D), _resident(w_gu), _resident(w_down), _resident(g_ffn), _resident(g_final)],
        out_specs=[tok(D), tok(2 * D_FF), tok(D_FF), tok(D), _acc_spec((1, LANES)), _acc_spec((1, D))],
        out_shape=[_sds((T, D), BF16), _sds((T, 2 * D_FF), F32), _sds((T, D_FF), BF16), _sds((T, D), F32),
                   _sds((1, LANES), F32), _sds((1, D), F32)],
        compiler_params=_params(1),
    )(x2, tgt, w_gu, w_down, g_ffn, g_final)


def _bwd_ffn(dx3, gu, x2, w_gu, w_down, g_ffn, T):
    TB = min(256, T)
    nb = T // TB

    def body(d_ref, gu_ref, x_ref, wgu_ref, wd_ref, gf_ref, dgu_ref, dx2_ref, dgf_ref):
        i = pl.program_id(0)

        @pl.when(i == 0)
        def _():
            dgf_ref[...] = jnp.zeros_like(dgf_ref)

        dx3v = d_ref[...]
        db = dx3v.astype(BF16)
        dhf = jnp.zeros((TB, D), F32)
        for ch in range(D_FF // FF_CHUNK):
            c0 = ch * FF_CHUNK
            dact = _dot_nt(db, wd_ref[c0:c0 + FF_CHUNK, :])
            g = gu_ref[:, c0:c0 + FF_CHUNK]
            u = gu_ref[:, D_FF + c0:D_FF + c0 + FF_CHUNK]
            sg = _sig(g)
            dg = (dact * u * sg * (1.0 + g * (1.0 - sg))).astype(BF16)
            du = (dact * g * sg).astype(BF16)
            dgu_ref[:, c0:c0 + FF_CHUNK] = dg
            dgu_ref[:, D_FF + c0:D_FF + c0 + FF_CHUNK] = du
            dhf = dhf + _dot_nt(dg, wgu_ref[ch]) + _dot_nt(du, wgu_ref[2 + ch])
        r2, xh2 = _rms(x_ref[...])
        dx, dg_tok = _rms_bwd(dhf, xh2, r2, gf_ref[...])
        dx2_ref[...] = dx3v + dx
        dgf_ref[...] += jnp.sum(dg_tok, axis=0, keepdims=True)

    tok = lambda w: pl.BlockSpec((TB, w), lambda i: (i, 0))
    return pl.pallas_call(
        body, name="bwd_ffn", grid=(nb,),
        in_specs=[tok(D), tok(2 * D_FF), tok(D), _resident(w_gu), _resident(w_down), _resident(g_ffn)],
        out_specs=[tok(2 * D_FF), tok(D), _acc_spec((1, D))],
        out_shape=[_sds((T, 2 * D_FF), BF16), _sds((T, D), F32), _sds((1, D), F32)],
        compiler_params=_params(1),
    )(dx3, gu, x2, w_gu, w_down, g_ffn)


def _bwd_mid(dx2, x1, qm, km, vm, o, w_mo, w_mq, w_out, g_x, B, S, comm=None):
    T = B * S
    TB = min(512, S)
    nb = S // TB
    inv = 1.0 / math.sqrt(MEM_HD)

    def body(d_ref, x1_ref, qm_ref, km_ref, vm_ref, o_ref, wm_ref, wq_ref, wo_ref, g_ref,
             dx1_ref, dqm_ref, dco_ref, do_ref, dd_ref, dkm_ref, dvm_ref, dgx_ref):
        b = pl.program_id(0)
        j = pl.program_id(1)

        @pl.when((b == 0) & (j == 0))
        def _():
            dgx_ref[...] = jnp.zeros_like(dgx_ref)

        @pl.when(j == 0)
        def _():
            dkm_ref[...] = jnp.zeros_like(dkm_ref)
            dvm_ref[...] = jnp.zeros_like(dvm_ref)

        dx2v = d_ref[...]
        dom = _dot_nt(dx2v.astype(BF16), wm_ref[...]).astype(BF16)
        qmv = qm_ref[...]
        kmv = km_ref[...]
        vmv = vm_ref[...]
        ps = _mem_probs(qmv, kmv)
        for h in range(MEM_HEADS):
            hs = slice(h * MEM_HD, (h + 1) * MEM_HD)
            p = ps[h]
            dp = _dot_nt(dom[:, hs], vmv[:, hs])
            ds = (p * (dp - jnp.sum(p * dp, axis=-1, keepdims=True))).astype(BF16)
            dqm_ref[:, hs] = (_dot(ds, kmv[:, hs]) * inv).astype(BF16)
            dkm_ref[:, hs] += _dot_tn(ds, qmv[:, hs]) * inv
            dvm_ref[:, hs] += _dot_tn(p.astype(BF16), dom[:, hs])
        dhx = _dot_nt(dqm_ref[...], wq_ref[...])
        r1, xh1 = _rms(x1_ref[...])
        dx, dg_tok = _rms_bwd(dhx, xh1, r1, g_ref[...])
        dx1 = dx2v + dx
        dx1_ref[...] = dx1
        dgx_ref[...] += jnp.sum(dg_tok, axis=0, keepdims=True)
        d1b = dx1.astype(BF16)
        dco_ref[...] = _dot_nt(d1b, wo_ref[0:CONV_CH, :])
        do = _dot_nt(d1b, wo_ref[CONV_CH:D, :])
        dob = do.astype(BF16)
        do_ref[...] = dob
        dd_ref[...] = _dot_hi(dob.astype(F32) * o_ref[...], _head_sum(FOX_W))

    tok = lambda w: pl.BlockSpec((TB, w), lambda b, j: (b * nb + j, 0))
    memb = pl.BlockSpec((MEM_LEN, D), lambda b, j: (b, 0))
    outs = [(D, F32), (D, BF16), (CONV_CH, F32), (FOX_W, BF16), (FOX_W, F32)]
    return _call(
        body, comm, name="bwd_mid", grid=(B, nb),
        in_specs=[tok(D), tok(D), tok(D), memb, memb, tok(FOX_W), _resident(w_mo), _resident(w_mq), _resident(w_out),
                  _resident(g_x)],
        out_specs=[tok(w) for w, _ in outs] + [memb, memb, _acc_spec((1, D))],
        out_shape=[_sds((T, w), dt) for w, dt in outs] + [_sds((B * MEM_LEN, D), F32)] * 2 + [_sds((1, D), F32)],
        scratch_shapes=[],
        args=(dx2, x1, qm, km, vm, o, w_mo, w_mq, w_out, g_x))


def _mem_bwd(dkm, dvm, mem2, w_mkv, g_mem, B):
    def body(dk_ref, dv_ref, m_ref, w_ref, g_ref, dkv_ref, dg_ref):
        b = pl.program_id(0)

        @pl.when(b == 0)
        def _():
            dg_ref[...] = jnp.zeros_like(dg_ref)

        dk = dk_ref[...].astype(BF16)
        dv = dv_ref[...].astype(BF16)
        dkv_ref[:, 0:D] = dk
        dkv_ref[:, D:2 * D] = dv
        dmn = jnp.zeros((MEM_LEN, D), F32)
        for s in range(2):
            dmn = dmn + _dot_nt(dk[:, 512 * s:512 * (s + 1)], w_ref[s]) + _dot_nt(dv[:, 512 * s:512 * (s + 1)], w_ref[2 + s])
        _, xh = _rms(m_ref[...])
        dg_ref[...] += jnp.sum(dmn * xh, axis=0, keepdims=True)

    blk = pl.BlockSpec((MEM_LEN, D), lambda b: (b, 0))
    return pl.pallas_call(
        body, name="mem_bwd", grid=(B,),
        in_specs=[blk, blk, blk, _resident(w_mkv), _resident(g_mem)],
        out_specs=[pl.BlockSpec((MEM_LEN, 2 * D), lambda b: (b, 0)), _acc_spec((1, D))],
        out_shape=[_sds((B * MEM_LEN, 2 * D), BF16), _sds((1, D), F32)],
        compiler_params=_params(1),
    )(dkm, dvm, mem2, w_mkv, g_mem)


def _fox_bwd(q, k, v, do, bias, dd, ckT, B, S, comm=None):
    T = B * S
    TK = min(256, S)
    nk = S // TK
    scale = 1.0 / math.sqrt(HEAD_D)

    def body(q_ref, k_ref, v_ref, do_ref, bias_ref, dd_ref, ck_ref, dq_ref, dk_ref, dv_ref, dc_ref, dcq_ref,
             dq_acc, rs_acc, s_scr, dp_scr, s_odd, dp_odd, dk_acc, dv_acc, dc_acc):
        j = pl.program_id(2)

        @pl.when(j == 0)
        def _():
            dq_acc[...] = jnp.zeros_like(dq_acc)
            rs_acc[...] = jnp.zeros_like(rs_acc)

        dk_acc[...] = jnp.zeros_like(dk_acc)
        dv_acc[...] = jnp.zeros_like(dv_acc)
        dc_acc[...] = jnp.zeros_like(dc_acc)
        lane = lax.broadcasted_iota(jnp.int32, (TK, LANES), 1)
        lo = lane < HEAD_D
        ks = k_ref[...] * jnp.asarray(scale, BF16)
        v2 = v_ref[...]
        zero = jnp.zeros_like(ks)
        kh = (jnp.where(lo, ks, zero), jnp.where(lo, zero, ks))
        vh = (jnp.where(lo, v2, zero), jnp.where(lo, zero, v2))
        kstart = pl.multiple_of(j * TK, TK)
        ckh = tuple(ck_ref[0, 0, h:h + 1, pl.ds(kstart, TK)] for h in range(2))
        row = lax.broadcasted_iota(jnp.int32, (TK, TK), 0)
        col = lax.broadcasted_iota(jnp.int32, (TK, TK), 1)
        wide = lambda x: jnp.concatenate([x, x], axis=1) if TK == 2 * LANES else jnp.tile(x, (1, TK // LANES))

        def scores(i, s_buf, dp_buf):
            start = pl.multiple_of(i * TK, TK)
            qi = q_ref[pl.ds(start, TK), :]
            doi = do_ref[pl.ds(start, TK), :]
            for h in range(2):
                s_buf[h] = _dot_nt(qi, kh[h])
                dp_buf[h] = _dot_nt(doi, vh[h])

        def grads(i, s_buf, dp_buf, diagonal):
            start = pl.multiple_of(i * TK, TK)
            qi = q_ref[pl.ds(start, TK), :]
            doi = do_ref[pl.ds(start, TK), :]
            bias2 = bias_ref[pl.ds(start, TK), :]
            dd2 = dd_ref[pl.ds(start, TK), :]
            for h in range(2):
                hc = slice(h * HEAD_D, h * HEAD_D + 1)
                bias = jnp.broadcast_to(bias2[:, hc], (TK, LANES))
                ddh = jnp.broadcast_to(dd2[:, hc], (TK, LANES))
                p = jnp.exp((s_buf[h] - ckh[h]) + wide(bias))
                if diagonal:
                    p = jnp.where(col <= row, p, 0.0)
                ds = p * (dp_buf[h] - wide(ddh))
                dc_acc[h, 0:1, :] += jnp.sum(ds, axis=0, keepdims=True)
                rs_acc[h, pl.ds(start, TK), :] += jnp.sum(ds, axis=1, keepdims=True)
                pb = p.astype(BF16)
                dsb = ds.astype(BF16)
                dv_acc[h] += _dot_tn(pb, doi)
                dk_acc[h] += _dot_tn(dsb, qi)
                dq_acc[pl.ds(start, TK), :] += _dot(dsb, kh[h])

        n_off = nk - 1 - j
        block = lambda t: jnp.where(t < n_off, j + 1 + t, j)

        def two_blocks(tt, carry):
            t = 2 * tt
            scores(block(t + 1), s_odd, dp_odd)
            grads(block(t), s_scr, dp_scr, False)
            scores(block(t + 2), s_scr, dp_scr)
            grads(block(t + 1), s_odd, dp_odd, False)
            return carry

        scores(block(0), s_scr, dp_scr)
        lax.fori_loop(0, n_off // 2, two_blocks, 0)

        @pl.when(n_off % 2 == 0)
        def _():
            grads(j, s_scr, dp_scr, True)

        @pl.when(n_off % 2 == 1)
        def _():
            scores(j, s_odd, dp_odd)
            grads(nk - 1, s_scr, dp_scr, False)
            grads(j, s_odd, dp_odd, True)

        dk_ref[...] = (jnp.where(lo, dk_acc[0], dk_acc[1]) * scale).astype(BF16)
        dv_ref[...] = jnp.where(lo, dv_acc[0], dv_acc[1]).astype(BF16)
        sub = lax.broadcasted_iota(jnp.int32, (8, TK), 0)
        dca = dc_acc[0, 0:1, :]
        dcb = dc_acc[1, 0:1, :]
        dc_ref[0, 0] = jnp.where(sub == 0, -dca, jnp.where(sub == 1, -dcb, 0.0))

        @pl.when(j == nk - 1)
        def _():
            dq_ref[...] = dq_acc[...].astype(BF16)
            lo_s = lax.broadcasted_iota(jnp.int32, (S, LANES), 1) < HEAD_D
            dcq_ref[...] = jnp.where(lo_s, rs_acc[0], rs_acc[1])

    full = pl.BlockSpec((S, LANES), lambda b, p, j: (b, p))
    blk = pl.BlockSpec((TK, LANES), lambda b, p, j: (b * nk + j, p))
    return _call(
        body, comm, name="fox_bwd", grid=(B, N_PAIR, nk),
        in_specs=[full, blk, blk, full, full, full, pl.BlockSpec((1, 1, 8, S), lambda b, p, j: (b, p, 0, 0))],
        out_specs=[full, blk, blk, pl.BlockSpec((1, 1, 8, TK), lambda b, p, j: (b, p, 0, j)), full],
        out_shape=[_sds((T, FOX_W), BF16), _sds((T, FOX_W), BF16), _sds((T, FOX_W), BF16),
                   _sds((B, N_PAIR, 8, S), F32), _sds((T, FOX_W), F32)],
        scratch_shapes=[pltpu.VMEM((S, LANES), F32), pltpu.VMEM((2, S, 1), F32),
                        pltpu.VMEM((2, TK, TK), F32), pltpu.VMEM((2, TK, TK), F32),
                        pltpu.VMEM((2, TK, TK), F32), pltpu.VMEM((2, TK, TK), F32),
                        pltpu.VMEM((2, TK, LANES), F32), pltpu.VMEM((2, TK, LANES), F32), pltpu.VMEM((2, 8, TK), F32)],
        args=(q, k, v, do, bias, dd, ckT))


def _fgate_bwd(dc8, zf, B, S):
    T = B * S
    TB = min(512, S)
    nb = S // TB

    def body(dc_ref, zf_ref, dzf_ref, dbf_ref, carry):
        b = pl.program_id(0)
        j = pl.program_id(1)

        @pl.when((b == 0) & (j == 0))
        def _():
            dbf_ref[...] = jnp.zeros_like(dbf_ref)

        @pl.when(j == 0)
        def _():
            carry[...] = jnp.zeros_like(carry)

        dc = dc_ref[...]
        row = lax.broadcasted_iota(jnp.int32, (TB, TB), 0)
        col = lax.broadcasted_iota(jnp.int32, (TB, TB), 1)
        dlogf = _dot_hi((col >= row).astype(F32), dc) + carry[0:1, :]
        carry[0:1, :] = dlogf[0:1, :]
        lane = lax.broadcasted_iota(jnp.int32, dc.shape, 1)
        dzf = jnp.where(lane < 8, dlogf * _sig(-zf_ref[...]), 0.0)
        dzf_ref[...] = dzf.astype(BF16)
        dbf_ref[...] += jnp.sum(dzf, axis=0, keepdims=True)

    tok = pl.BlockSpec((TB, LANES), lambda b, j: (b * nb + (nb - 1 - j), 0))
    return pl.pallas_call(
        body, name="fgate_bwd", grid=(B, nb),
        in_specs=[tok, tok],
        out_specs=[tok, _acc_spec((1, LANES))],
        out_shape=[_sds((T, LANES), BF16), _sds((1, LANES), F32)],
        scratch_shapes=[pltpu.VMEM((8, LANES), F32)],
        compiler_params=_params(2),
    )(dc8, zf)


def _conv_bwd(dco, y, u, gt, cw, lng, lnb, B, S, comm=None):
    T = B * S
    CB = min(256, S)
    nb = S // CB
    hb = CB // CONV_HALO

    def body(dco_ref, y_ref, u_ref, gt_ref, up_ref, gp_ref, w_ref, lg_ref, lb_ref,
             du_ref, dgt_ref, dw_ref, vec_ref, acat, dycat, ash, dysh):
        b = pl.program_id(0)
        j = pl.program_id(1)
        jr = nb - 1 - j

        @pl.when((b == 0) & (j == 0))
        def _():
            dw_ref[...] = jnp.zeros_like(dw_ref)
            vec_ref[...] = jnp.zeros_like(vec_ref)

        @pl.when(j == 0)
        def _():
            dycat[CB:CB + CONV_HALO, :] = jnp.zeros((CONV_HALO, CONV_CH), F32)

        lg = lg_ref[...]
        rs, n, l = _layernorm_silu(y_ref[...], lg, lb_ref[...])
        sg = _sig(l)
        dl = dco_ref[...] * (sg * (1.0 + l * (1.0 - sg)))
        dn = dl * lg
        dy = rs * (dn - jnp.mean(dn, axis=-1, keepdims=True) - n * jnp.mean(dn * n, axis=-1, keepdims=True))
        vec_ref[0:1, :] += jnp.sum(dy, axis=0, keepdims=True)
        vec_ref[1:2, :] += jnp.sum(dl * n, axis=0, keepdims=True)
        vec_ref[2:3, :] += jnp.sum(dl, axis=0, keepdims=True)
        dycat[0:CB, :] = dy
        acat[0:CONV_HALO, :] = jnp.where(jr > 0, up_ref[...] * _sig(gp_ref[...]), 0.0)
        acat[CONV_HALO:CONV_HALO + CB, :] = u_ref[...] * _sig(gt_ref[...])
        _shifted_copies(acat, ash, CB + CONV_HALO - SUB)
        _shifted_copies(dycat, dysh, CB + CONV_HALO - SUB)
        for r0, rows, cs in _conv_pieces(CB):
            dyp = dycat[r0:r0 + rows, cs]
            da = jnp.zeros((rows, LANES), F32)
            for k in range(CONV_K):
                da = da + w_ref[k:k + 1, cs] * _tap(dycat, dysh, r0 + CONV_K - 1 - k, rows, cs)
                dw_ref[k:k + 1, cs] += jnp.sum(dyp * _tap(acat, ash, r0 + CONV_HALO - (CONV_K - 1) + k, rows, cs),
                                               axis=0, keepdims=True)
            uv = u_ref[r0:r0 + rows, cs]
            sgt = _sig(gt_ref[r0:r0 + rows, cs])
            du_ref[r0:r0 + rows, cs] = (da * sgt).astype(BF16)
            dgt_ref[r0:r0 + rows, cs] = (da * uv * sgt * (1.0 - sgt)).astype(BF16)
        dycat[CB:CB + CONV_HALO, :] = dycat[0:CONV_HALO, :]

    tok = lambda w: pl.BlockSpec((CB, w), lambda b, j: (b * nb + (nb - 1 - j), 0))
    prev = pl.BlockSpec((CONV_HALO, CONV_CH), lambda b, j: (jnp.maximum((b * nb + (nb - 1 - j)) * hb - 1, 0), 0))
    return _call(
        body, comm, name="conv_bwd", grid=(B, nb),
        in_specs=[tok(CONV_CH), tok(CONV_CH), tok(CONV_CH), tok(CONV_CH), prev, prev, _resident(cw), _resident(lng),
                  _resident(lnb)],
        out_specs=[tok(CONV_CH), tok(CONV_CH), _acc_spec((CONV_HALO, CONV_CH)), _acc_spec((8, CONV_CH))],
        out_shape=[_sds((T, CONV_CH), BF16), _sds((T, CONV_CH), BF16), _sds((CONV_HALO, CONV_CH), F32),
                   _sds((8, CONV_CH), F32)],
        scratch_shapes=[pltpu.VMEM((CONV_HALO + CB, CONV_CH), F32), pltpu.VMEM((CB + CONV_HALO, CONV_CH), F32),
                        pltpu.VMEM((SUB, CB + CONV_HALO - SUB, CONV_CH), F32),
                        pltpu.VMEM((SUB, CB + CONV_HALO - SUB, CONV_CH), F32)],
        args=(dco, y, u, gt, u, gt, cw, lng, lnb))


def _bwd_in(dz, w_int, w_ft, x2, dx1, g_mix, T, comm=None):
    TB = min(512, T)
    nb = T // TB

    def body(dz_ref, w_ref, wf_ref, x_ref, d1_ref, g_ref, gx_ref, dg_ref):
        i = pl.program_id(0)

        @pl.when(i == 0)
        def _():
            dg_ref[...] = jnp.zeros_like(dg_ref)

        dh = _dot(dz_ref[:, 0:OFF_F], w_ref[0:OFF_F, :]) + _dot(dz_ref[:, OFF_F:D_IN_PAD], wf_ref[...])
        r0, xh0 = _rms(x_ref[...])
        dx, dg_tok = _rms_bwd(dh, xh0, r0, g_ref[...])
        gx_ref[...] = d1_ref[...] + dx
        dg_ref[...] += jnp.sum(dg_tok, axis=0, keepdims=True)

    tok = lambda w: pl.BlockSpec((TB, w), lambda i: (i, 0))
    return _call(
        body, comm, name="bwd_in", grid=(nb,),
        in_specs=[tok(D_IN_PAD), _resident(w_int), _resident(w_ft), tok(D), tok(D), _resident(g_mix)],
        out_specs=[tok(D), _acc_spec((1, D))],
        out_shape=[_sds((T, D), F32), _sds((1, D), F32)],
        scratch_shapes=[],
        args=(dz, w_int, w_ft, x2, dx1, g_mix))


def _dw(a, b, name, tn, slabs=False, tk=None, rows=None):
    T, K = a.shape
    N = b.shape[1]
    tk = tk or (K if K <= 1024 else K // 2)
    tt = min(1024, T)
    nt = T // tt

    def body(a_ref, b_ref, o_ref, acc):
        t = pl.program_id(2)

        @pl.when(t == 0)
        def _():
            acc[...] = jnp.zeros_like(acc)

        acc[...] += _dot_tn(a_ref[...].astype(BF16), b_ref[...].astype(BF16))

        @pl.when(t == nt - 1)
        def _():
            o_ref[...] = acc[...]

    return pl.pallas_call(
        body, name=name, grid=(K // tk, N // tn, nt),
        in_specs=[pl.BlockSpec((tt, tk), lambda i, j, t: (t, i)), pl.BlockSpec((tt, tn), lambda i, j, t: (t, j))],
        out_specs=(pl.BlockSpec((None, tk, tn), lambda i, j, t: (j, i, 0)) if slabs
                   else pl.BlockSpec((tk, tn), lambda i, j, t: (i, j))),
        out_shape=_sds((N // tn, K, tn) if slabs else (rows or K, N), F32),
        scratch_shapes=[pltpu.VMEM((tk, tn), F32)],
        compiler_params=_params(3),
    )(a, b)


def _pos():
    return lax.axis_index("x"), lax.axis_index("y"), lax.axis_index("c")


def _remote(src, dst, ssem, rsem, to):
    return pltpu.make_async_remote_copy(src_ref=src, dst_ref=dst, send_sem=ssem, recv_sem=rsem, device_id=to,
                                        device_id_type=MESH)


def _split_axis(shape):
    return 0 if shape[0] % 32 == 0 else 1


def _half_shape(shape, parts=2):
    return (shape[0] // parts, shape[1]) if _split_axis(shape) == 0 else (shape[0], shape[1] // parts)


def _half(shape, c):
    R, C = shape
    if _split_axis(shape) == 0:
        return (pl.ds(pl.multiple_of(c * (R // 2), 16), R // 2), slice(None))
    return (slice(None), pl.ds(pl.multiple_of(c * (C // 2), LANES), C // 2))


def _half_block(shape, parts, lead, which):
    blk = _half_shape(shape, parts)
    idx = (which, 0) if _split_axis(shape) == 0 else (0, which)
    return blk, tuple(lead) + idx


class _Comm:
    def __init__(self, ins, out_shapes, sems, start, finish):
        self.ins, self.out_shapes, self.sems, self.start, self.finish = list(ins), list(out_shapes), list(sems), start, finish


def _ag_comm(shards):
    n = len(shards)

    def parts(ins, outs, sems):
        send_sems, recv_sems, local_sems = sems
        x, y, c = _pos()
        me, sib = (x, y, c), (x, y, 1 - c)
        chips = [(1 - x, y), (x, 1 - y), (1 - x, 1 - y)]

        def rows(w, px, py, pc):
            return outs[w].at[(2 * px + py,) + _half(shards[w].shape, pc)]

        def copy(w, k, block, to, src=None):
            return _remote(rows(w, *block) if src is None else src, rows(w, *block), send_sems.at[w, k],
                           recv_sems.at[w, k], to)

        mine, first = [], []
        for w in range(n):
            src = ins[w].at[_half(shards[w].shape, c)]
            mine.append(pltpu.make_async_copy(src, rows(w, *me), local_sems.at[w]))
            first += [copy(w, 0, me, sib, src=src)] + [copy(w, 1 + j, me, (*chip, c), src=src) for j, chip in enumerate(chips)]
        return c, me, sib, chips, copy, mine, first

    def start(ins, outs, sems):
        _, _, _, _, _, mine, first = parts(ins, outs, sems)
        for cp in mine + first:
            cp.start()

    def finish(ins, outs, sems):
        c, me, sib, chips, copy, mine, first = parts(ins, outs, sems)
        passed = []
        for w in range(n):
            for j, chip in enumerate(chips):
                copy(w, 1 + j, (*chip, c), me).wait_recv()
                passed.append(copy(w, 4 + j, (*chip, c), sib))
                passed[-1].start()
        for w in range(n):
            copy(w, 0, sib, me).wait_recv()
            for j, chip in enumerate(chips):
                copy(w, 4 + j, (*chip, 1 - c), me).wait_recv()
        for cp in first + passed:
            cp.wait_send()
        for cp in mine:
            cp.wait()

    D7 = pltpu.SemaphoreType.DMA((n, 7))
    return _Comm(shards, [_sds((4,) + s.shape, s.dtype) for s in shards], [D7, D7, pltpu.SemaphoreType.DMA((n,))],
                 start, finish)


def _sibling_comm(gs):
    n = len(gs)

    def copies(ins, outs, sems):
        send_sems, recv_sems = sems
        x, y, c = _pos()
        return [_remote(ins[w].at[(s,) + _half(gs[w].shape[1:], 1 - c)], outs[w].at[s], send_sems.at[w, s],
                        recv_sems.at[w, s], (x, y, 1 - c)) for w in range(n) for s in range(4)]

    def start(ins, outs, sems):
        for cp in copies(ins, outs, sems):
            cp.start()

    def finish(ins, outs, sems):
        for cp in copies(ins, outs, sems):
            cp.wait()

    D4 = pltpu.SemaphoreType.DMA((n, 4))
    return _Comm(gs, [_sds((4,) + _half_shape(g.shape[1:]), F32) for g in gs], [D4, D4], start, finish)


def _ici_comm(pbs):
    n = len(pbs)

    def copies(ins, outs, sems):
        send_sems, recv_sems = sems
        x, y, c = _pos()
        return [_remote(ins[w].at[2 * tx + ty], outs[w].at[j], send_sems.at[w, j], recv_sems.at[w, j], (tx, ty, c))
                for w in range(n) for j, (tx, ty) in enumerate([(1 - x, y), (x, 1 - y), (1 - x, 1 - y)])]

    def start(ins, outs, sems):
        for cp in copies(ins, outs, sems):
            cp.start()

    def finish(ins, outs, sems):
        for cp in copies(ins, outs, sems):
            cp.wait()

    D3 = pltpu.SemaphoreType.DMA((n, 3))
    return _Comm(pbs, [_sds((3,) + p.shape[1:], BF16) for p in pbs], [D3, D3], start, finish)


def _join(*comms):
    counts = [(len(c.ins), len(c.out_shapes), len(c.sems)) for c in comms]

    def each(which):
        def run(ins, outs, sems):
            i = o = k = 0
            for c, (ni, no, nk) in zip(comms, counts):
                getattr(c, which)(ins[i:i + ni], outs[o:o + no], sems[k:k + nk])
                i, o, k = i + ni, o + no, k + nk
        return run

    return _Comm(sum((c.ins for c in comms), []), sum((c.out_shapes for c in comms), []),
                 sum((c.sems for c in comms), []), each("start"), each("finish"))


def _run_comm(comm, name):
    ni, no = len(comm.ins), len(comm.out_shapes)

    def body(*refs):
        ins, outs, sems = refs[:ni], refs[ni:ni + no], refs[ni + no:]
        comm.start(ins, outs, sems)
        comm.finish(ins, outs, sems)

    return pl.pallas_call(body, name=name, out_shape=comm.out_shapes, in_specs=[ANY] * ni, out_specs=[ANY] * no,
                          scratch_shapes=comm.sems)(*comm.ins)


def _call(body, comm, *, name, grid, in_specs, out_specs, out_shape, scratch_shapes, args):
    n_grid = len(grid)
    if comm is None:
        res = pl.pallas_call(body, name=name, grid=grid, in_specs=in_specs, out_specs=out_specs, out_shape=out_shape,
                             scratch_shapes=scratch_shapes, compiler_params=_params(n_grid))(*args)
        return list(res), []
    n_in, n_out, n_scr = len(in_specs), len(out_specs), len(scratch_shapes)
    ni, no = len(comm.ins), len(comm.out_shapes)

    def carried(*refs):
        ins, refs = refs[:n_in], refs[n_in:]
        cins, refs = refs[:ni], refs[ni:]
        outs, refs = refs[:n_out], refs[n_out:]
        couts, refs = refs[:no], refs[no:]
        scr, csems = refs[:n_scr], refs[n_scr:]
        ids = [pl.program_id(ax) for ax in range(n_grid)]
        first = functools.reduce(jnp.logical_and, [i == 0 for i in ids])
        last = functools.reduce(jnp.logical_and, [i == g - 1 for i, g in zip(ids, grid)])

        @pl.when(first)
        def _():
            comm.start(cins, couts, csems)

        body(*ins, *outs, *scr)

        @pl.when(last)
        def _():
            comm.finish(cins, couts, csems)

    res = pl.pallas_call(
        carried, name=name, grid=grid, in_specs=list(in_specs) + [ANY] * ni, out_specs=list(out_specs) + [ANY] * no,
        out_shape=list(out_shape) + comm.out_shapes, scratch_shapes=list(scratch_shapes) + comm.sems,
        compiler_params=_params(n_grid))(*args, *comm.ins)
    return list(res[:n_out]), list(res[n_out:])


def _sibling_share(gs):
    n = len(gs)

    def body(*refs):
        outs = refs[n:2 * n]
        send_sems, recv_sems = refs[2 * n:]
        x, y, c = _pos()
        cps = []
        for w in range(n):
            mine = outs[w].at[_half(gs[w].shape, c)]
            cps.append(_remote(mine, mine, send_sems.at[w], recv_sems.at[w], (x, y, 1 - c)))
            cps[-1].start()
        for cp in cps:
            cp.wait()

    return pl.pallas_call(
        body, name="rs_share", out_shape=[_sds(g.shape, F32) for g in gs],
        in_specs=[ANY] * n, out_specs=[ANY] * n, input_output_aliases={w: w for w in range(n)},
        scratch_shapes=[pltpu.SemaphoreType.DMA((n,)), pltpu.SemaphoreType.DMA((n,))],
    )(*gs)


def _small_allreduce(v, name):
    P = v.shape[0]
    vm = pl.BlockSpec(memory_space=pltpu.VMEM)

    def body(v_ref, o_ref, gath, send_sems, recv_sems):
        x, y, c = _pos()
        me = 4 * x + 2 * y + c
        gath[me] = v_ref[...]
        cps = []
        for r in range(1, 8):
            tx = (1 - x) if r & 4 else x
            ty = (1 - y) if r & 2 else y
            tc = (1 - c) if r & 1 else c
            cps.append(_remote(v_ref, gath.at[me], send_sems.at[r - 1], recv_sems.at[r - 1], (tx, ty, tc)))
            cps[-1].start()
        for cp in cps:
            cp.wait()
        acc = gath[0]
        for d in range(1, 8):
            acc = acc + gath[d]
        o_ref[...] = acc

    return pl.pallas_call(
        body, name=name, out_shape=_sds((P, LANES), F32), in_specs=[vm], out_specs=vm,
        scratch_shapes=[pltpu.VMEM((8, P, LANES), F32), pltpu.SemaphoreType.DMA((7,)), pltpu.SemaphoreType.DMA((7,))],
    )(v)


def _chip_sum(g, rcv, pos, name):
    shard = g.shape[1:]
    hs = _half_shape(shard)

    def body(pos_ref, g_ref, r_ref, o_ref):
        o_ref[...] = (g_ref[...] + r_ref[...]).astype(BF16)

    return pl.pallas_call(
        body, name=name, out_shape=_sds((4,) + hs, BF16),
        grid_spec=pltpu.PrefetchScalarGridSpec(
            num_scalar_prefetch=1, grid=(4,),
            in_specs=[pl.BlockSpec((1,) + hs, lambda s, pos: _half_block(shard, 2, (s,), pos[0])[1]),
                      pl.BlockSpec((1,) + hs, lambda s, pos: (s, 0, 0))],
            out_specs=pl.BlockSpec((1,) + hs, lambda s, pos: (s, 0, 0))),
        compiler_params=_params(1),
    )(pos, g, rcv)


def _final_sum(g, rcv, rc, pos, name):
    shard = g.shape[1:]
    qs = _half_shape(shard, 4)

    def body(pos_ref, g_ref, r_ref, rc_ref, o_ref):
        acc = g_ref[0] + r_ref[0]
        for j in range(3):
            acc = acc + rc_ref[j].astype(F32)
        o_ref[...] = acc

    return pl.pallas_call(
        body, name=name, out_shape=_sds(shard, F32),
        grid_spec=pltpu.PrefetchScalarGridSpec(
            num_scalar_prefetch=1, grid=(2,),
            in_specs=[pl.BlockSpec((1,) + qs, lambda i, pos: _half_block(shard, 4, (pos[1],), pos[0] * 2 + i)[1]),
                      pl.BlockSpec((1,) + qs, lambda i, pos: _half_block(shard, 4, (pos[1],), i)[1]),
                      pl.BlockSpec((3,) + qs, lambda i, pos: _half_block(shard, 4, (0,), i)[1])],
            out_specs=pl.BlockSpec(qs, lambda i, pos: _half_block(shard, 4, (), pos[0] * 2 + i)[1])),
        compiler_params=_params(1),
    )(pos, g, rcv, rc)


def _adamw_math(w, g, m, v):
    m = ADAM_B1 * m + (1.0 - ADAM_B1) * g
    v = ADAM_B2 * v + (1.0 - ADAM_B2) * (g * g)
    m_hat = m / (1.0 - ADAM_B1 ** ADAM_STEP)
    v_hat = v / (1.0 - ADAM_B2 ** ADAM_STEP)
    delta = -ADAM_LR * (m_hat / (jnp.sqrt(v_hat) + ADAM_EPS) + ADAM_WD * w)
    return delta, m, v


def _adamw(w, g, m, v, name, blk_shape):
    R, C = w.shape

    def body(w_ref, g_ref, m_ref, v_ref, go_ref, d_ref, nm_ref, nv_ref):
        g = g_ref[...]
        d, nm, nv = _adamw_math(w_ref[...], g, m_ref[...], v_ref[...])
        go_ref[...] = g
        d_ref[...] = d
        nm_ref[...] = nm
        nv_ref[...] = nv

    blk = pl.BlockSpec(blk_shape, lambda i, j: (i, j))
    return pl.pallas_call(
        body, name=name, grid=(R // blk_shape[0], C // blk_shape[1]), in_specs=[blk] * 4, out_specs=[blk] * 4,
        out_shape=[_sds((R, C), F32)] * 4, compiler_params=_params(2),
    )(w, g, m, v)


SMALL = (("g_mix", 8), ("b_f", 8), ("conv_w", None), ("conv_b", 8), ("ln_g", 8), ("ln_b", 8), ("g_x", 8), ("g_mem", 8),
         ("g_ffn", 8), ("g_final", 8), ("loss", 8))


def _pack_small(parts, conv_rows):
    rows = []
    for name, n in SMALL:
        if name not in parts:
            continue
        n = conv_rows if n is None else n
        flat = parts[name].reshape(-1).astype(F32)
        flat = jnp.pad(flat, (0, n * LANES - flat.shape[0]))
        rows.append(flat.reshape(n, LANES))
    return jnp.concatenate(rows, axis=0)


def _unpack_small(p, shapes, conv_rows):
    out, off = {}, 0
    for name, n in SMALL:
        if name not in shapes:
            continue
        n = conv_rows if n is None else n
        size = math.prod(shapes[name])
        out[name] = p[off:off + n].reshape(-1)[:size].reshape(shapes[name])
        off += n
    return out


def kernel(x, mem, g_mix, w_in, b_f, conv_w, conv_b, ln_g, ln_b, w_out, g_x, g_mem, w_mq, w_mkv, w_mo, g_ffn, w_gu, w_down, g_final, loss_target, m_g_mix, m_w_in, m_b_f, m_conv_w, m_conv_b, m_ln_g, m_ln_b, m_w_out, m_g_x, m_g_mem, m_w_mq, m_w_mkv, m_w_mo, m_g_ffn, m_w_gu, m_w_down, m_g_final, v_g_mix, v_w_in, v_b_f, v_conv_w, v_conv_b, v_ln_g, v_ln_b, v_w_out, v_g_x, v_g_mem, v_w_mq, v_w_mkv, v_w_mo, v_g_ffn, v_w_gu, v_w_down, v_g_final):
    names = ["g_mix", "w_in", "b_f", "conv_w", "conv_b", "ln_g", "ln_b", "w_out", "g_x", "g_mem", "w_mq", "w_mkv",
             "w_mo", "g_ffn", "w_gu", "w_down", "g_final"]
    W = dict(zip(names, [g_mix, w_in, b_f, conv_w, conv_b, ln_g, ln_b, w_out, g_x, g_mem, w_mq, w_mkv, w_mo, g_ffn,
                         w_gu, w_down, g_final]))
    M = dict(zip(names, [m_g_mix, m_w_in, m_b_f, m_conv_w, m_conv_b, m_ln_g, m_ln_b, m_w_out, m_g_x, m_g_mem, m_w_mq,
                         m_w_mkv, m_w_mo, m_g_ffn, m_w_gu, m_w_down, m_g_final]))
    V = dict(zip(names, [v_g_mix, v_w_in, v_b_f, v_conv_w, v_conv_b, v_ln_g, v_ln_b, v_w_out, v_g_x, v_g_mem, v_w_mq,
                         v_w_mkv, v_w_mo, v_g_ffn, v_w_gu, v_w_down, v_g_final]))
    big_names = [n for n, _, _, _ in BIG]
    B, S, _ = x.shape
    T = B * S
    mx, my, mc = _pos()
    chip = 2 * mx + my
    pos = jnp.stack([mc, chip]).astype(jnp.int32)

    shard2d = lambda a: a.reshape(a.shape[-2], a.shape[-1])
    big2d = lambda d, n: shard2d(d[n]).T if n == "w_in" else shard2d(d[n])
    shard_bf = {n: big2d(W, n).astype(BF16) for n in big_names}
    ag_mid = ["w_mkv", "w_out", "w_mq", "w_mo"]
    ag_ffn = ["w_gu", "w_down"]
    cw_mine = jnp.pad(shard2d(conv_w), ((0, 1), (0, 0)))
    w_in_slab, cw_slab = _run_comm(_ag_comm([shard_bf["w_in"], cw_mine]), "ag_w_in")
    slab = {"w_in": w_in_slab}
    w_int = w_in_slab.reshape(D_IN, D)
    w_ft = jnp.pad(w_int[OFF_F:D_IN], ((0, D_IN_PAD - D_IN), (0, 0)))
    cw = jnp.transpose(cw_slab, (1, 0, 2)).reshape(CONV_HALO, CONV_CH)

    row = lambda a: a.reshape(1, -1)
    bf_pad = jnp.pad(row(b_f), ((0, 0), (0, LANES - 8)))
    x2d = x.reshape(T, D)
    mem2d = mem.reshape(B * MEM_LEN, D)
    tgt = loss_target.reshape(T, D)

    (h, u, gt, q, k, v, zf, c, cq, qx, kx), got = _fwd_in(x2d, row(g_mix), w_int, w_ft, bf_pad, B, S,
                                                  comm=_ag_comm([shard_bf[n] for n in ag_mid[:2]]))
    slab.update(zip(ag_mid[:2], got))
    ckT = jnp.transpose(c.reshape(B, S, LANES)[:, :, :8], (0, 2, 1)).reshape(B, N_PAIR, 2, S)
    ckT = jnp.pad(ckT, ((0, 0), (0, 0), (0, 6), (0, 0)))
    (y, co), got = _conv_fwd(u, gt, cw, row(conv_b), row(ln_g), row(ln_b), B, S,
                             comm=_ag_comm([shard_bf[n] for n in ag_mid[2:]]))
    slab.update(zip(ag_mid[2:], got))
    (o, fox_bias), got = _fox_fwd(qx, kx, v, cq, B, S, comm=_ag_comm([shard_bf[n] for n in ag_ffn[:1]]))
    slab.update(zip(ag_ffn[:1], got))
    full = {n: slab[n] if by_col else slab[n].reshape(4 * r, c) for n, r, c, by_col in BIG if n in slab}
    mn, km, vm = _mem_kv(mem2d, row(g_mem), full["w_mkv"], B)
    (x1, hx, qm, om, x2, cat), got = _fwd_mid(x2d, co, o, km, vm, full["w_out"], full["w_mq"], full["w_mo"], row(g_x),
                                             B, S, comm=_ag_comm([shard_bf[n] for n in ag_ffn[1:]]))
    full["w_down"] = got[0].reshape(D_FF, D)
    hf, gu, act, dx3, loss_p, dg_final = _fwd_ffn(x2, tgt, full["w_gu"], full["w_down"], row(g_ffn), row(g_final), T)

    pos_sum = lambda gs, rcvs, ns: [_chip_sum(g, r, pos, "rs_chip_sum_" + n) for g, r, n in zip(gs, rcvs, ns)]
    fin_sum = lambda gs, rcvs, rcs, ns: [_final_sum(g, r, q3, pos, "rs_final_sum_" + n)
                                         for g, r, q3, n in zip(gs, rcvs, rcs, ns)]
    RH = {}
    dgu, dx2, dg_ffn = _bwd_ffn(dx3, gu, x2, full["w_gu"], full["w_down"], row(g_ffn), T)
    g_ffn_w = [_dw(hf, dgu, "dw_gu", FF_CHUNK, slabs=True), _dw(act, dx3, "dw_down", 512).reshape(4, D_FF // 4, D)]
    (dx1, dqm, dco, do, dd, dkm, dvm, dg_x), rcv_ffn = _bwd_mid(dx2, x1, qm, km, vm, o, full["w_mo"], full["w_mq"],
                                                                full["w_out"], row(g_x), B, S, comm=_sibling_comm(g_ffn_w))
    pb_ffn = pos_sum(g_ffn_w, rcv_ffn, ag_ffn)
    dkv, dg_mem = _mem_bwd(dkm, dvm, mem2d, full["w_mkv"], row(g_mem), B)
    g_mid_w = [_dw(mn, dkv, "dw_mkv", 512, slabs=True), _dw(cat, dx1, "dw_out", 512).reshape(4, 256, D),
               _dw(hx, dqm, "dw_mq", 512).reshape(4, 256, D), _dw(om, dx2, "dw_mo", 512).reshape(4, 256, D)]
    (dq, dk, dv, dc, dcq), got = _fox_bwd(q, k, v, do, fox_bias, dd, ckT, B, S,
                                          comm=_join(_ici_comm(pb_ffn), _sibling_comm(g_mid_w)))
    rc_ffn, rcv_mid = got[:len(pb_ffn)], got[len(pb_ffn):]
    RH.update(zip(ag_ffn, fin_sum(g_ffn_w, rcv_ffn, rc_ffn, ag_ffn)))
    pb_mid = pos_sum(g_mid_w, rcv_mid, ag_mid)
    dc8 = jnp.transpose(dc[:, :, :2, :].reshape(B, 8, S), (0, 2, 1)).reshape(T, 8)
    dc8 = dc8 + dcq.reshape(T, 8, HEAD_D)[:, :, 0]
    dzf, dbf = _fgate_bwd(jnp.pad(dc8, ((0, 0), (0, LANES - 8))), zf, B, S)
    (du, dgt, dcw, dvec), rc_mid = _conv_bwd(dco, y, u, gt, cw, row(ln_g), row(ln_b), B, S, comm=_ici_comm(pb_mid))
    RH.update(zip(ag_mid, fin_sum(g_mid_w, rcv_mid, rc_mid, ag_mid)))
    dz = jnp.concatenate([du, dgt, dq, dk, dv, dzf], axis=1)
    g_in_w = [_dw(dz, h, "dw_in", 512, tk=D_IN_PAD // 3, rows=D_IN).reshape(4, D_IN // 4, D)]
    rcv_in = _run_comm(_sibling_comm(g_in_w), "rs_sibling_in")
    (grad_x, dg_mix), rc_in = _bwd_in(dz, w_int, w_ft, x2d, dx1, row(g_mix), T,
                                      comm=_ici_comm(pos_sum(g_in_w, rcv_in, ["w_in"])))
    RH.update(zip(["w_in"], fin_sum(g_in_w, rcv_in, rc_in, ["w_in"])))
    shared = dict(zip(big_names, _sibling_share([RH[n] for n in big_names])))
    G, DL, NM, NV = {}, {}, {}, {}
    for n in big_names:
        G[n], DL[n], NM[n], NV[n] = _adamw(big2d(W, n), shared[n], big2d(M, n), big2d(V, n), "adamw_" + n,
                                           _half_shape(shared[n].shape))

    small_g = {"g_mix": dg_mix, "b_f": dbf[:, :8], "conv_w": dcw, "conv_b": dvec[0], "ln_g": dvec[1], "ln_b": dvec[2],
               "g_x": dg_x, "g_mem": dg_mem, "g_ffn": dg_ffn, "g_final": dg_final, "loss": loss_p[:, :1]}
    sg = _small_allreduce(_pack_small(small_g, CONV_HALO * 4), "allreduce_small")
    shapes = {n: W[n].shape for n in names if n not in big_names}
    shapes["conv_w"] = (CONV_HALO, CONV_CH)
    shapes["loss"] = (1,)
    sgrads = _unpack_small(sg, shapes, CONV_HALO * 4)
    loss = sgrads.pop("loss")[0]
    sgrads["conv_w"] = lax.dynamic_slice(sgrads["conv_w"], (0, chip * LANES), (CONV_K, LANES)).reshape(W["conv_w"].shape)
    spack = lambda d: _pack_small({n: d[n] for n in sgrads}, CONV_HALO)
    _, sd, snm, snv = _adamw(spack(W), spack(sgrads), spack(M), spack(V), "adamw_small", (8, LANES))
    sshapes = {n: W[n].shape for n in sgrads}
    SD, SNM, SNV = (_unpack_small(a, sshapes, CONV_HALO) for a in (sd, snm, snv))

    def collect(bigs, smalls):
        back = lambda n: (bigs[n].T if n == "w_in" else bigs[n]).reshape(W[n].shape)
        return [back(n) if n in big_names else smalls[n] for n in names]

    return (loss, grad_x.reshape(x.shape), *collect(G, sgrads), *collect(DL, SD), *collect(NM, SNM), *collect(NV, SNV))
```

```python
import functools
import math

import jax
import jax.numpy as jnp
from jax import lax
from jax.experimental import pallas as pl
from jax.experimental.pallas import tpu as pltpu

F32, BF16 = jnp.float32, jnp.bfloat16
MESH = pl.DeviceIdType.MESH

D = 1024
CONV_CH = 512
CONV_K = 31
CONV_HALO = 32
FOX_W = 512
HEAD_D = 64
N_PAIR = 4
MEM_LEN = 256
MEM_HEADS = 4
MEM_HD = 256
D_FF = 2816
FF_CHUNK = 1408
D_IN = 2568
D_IN_PAD = 2688
OFF_F = 2560
EPS = 1e-6
LANES = 128

ADAM_LR, ADAM_B1, ADAM_B2, ADAM_EPS, ADAM_WD, ADAM_STEP = 0.001, 0.9, 0.999, 1e-08, 0.01, 10

VMEM_LIMIT = 60 * 1024 * 1024

BIG = (("w_out", 256, 1024, False), ("w_mq", 256, 1024, False), ("w_mkv", 1024, 512, True),
       ("w_mo", 256, 1024, False), ("w_gu", 1024, 1408, True), ("w_down", 704, 1024, False),
       ("w_in", 642, 1024, False))

ANY = pl.BlockSpec(memory_space=pl.ANY)


def _sig(x):
    return 1.0 / (1.0 + jnp.exp(-x))


def _dot(a, b):
    return jnp.dot(a, b, preferred_element_type=F32)


def _dot_nt(a, b):
    return lax.dot_general(a, b, (((1,), (1,)), ((), ())), preferred_element_type=F32)


def _dot_tn(a, b):
    return lax.dot_general(a, b, (((0,), (0,)), ((), ())), preferred_element_type=F32)


def _split3(x):
    hi = x.astype(BF16)
    r = x - hi.astype(F32)
    mid = r.astype(BF16)
    return hi, mid, (r - mid.astype(F32)).astype(BF16)


def _dot_01(a, b):
    if a.dtype == jnp.bool_:
        return sum(_dot(a.astype(BF16), t) for t in _split3(b))
    return sum(_dot(t, b.astype(BF16)) for t in _split3(a))


def _resident(a):
    nd = a.ndim
    return pl.BlockSpec(a.shape, lambda *_: (0,) * nd, pipeline_mode=pl.Buffered(1))


def _acc_spec(shape):
    nd = len(shape)
    return pl.BlockSpec(shape, lambda *_: (0,) * nd)


def _params(n_grid):
    return pltpu.CompilerParams(dimension_semantics=("arbitrary",) * n_grid, vmem_limit_bytes=VMEM_LIMIT)


def _sds(shape, dtype):
    return jax.ShapeDtypeStruct(shape, dtype)


def _rms(x):
    r = lax.rsqrt(jnp.mean(x * x, axis=-1, keepdims=True) + EPS)
    return r, x * r


def _rms_bwd(dy, xh, r, g):
    dxh = dy * g
    dx = r * (dxh - xh * jnp.mean(dxh * xh, axis=-1, keepdims=True))
    return dx, dy * xh


def _head_expand(rows, cols):
    hd = lax.broadcasted_iota(jnp.int32, (rows, cols), 1) // HEAD_D
    hr = lax.broadcasted_iota(jnp.int32, (rows, cols), 0)
    return hd == hr


def _fwd_in(x2, g_mix, w_int, w_ft, bf_pad, B, S, comm=None):
    T = B * S
    TB = min(512, S)
    nb = S // TB

    def body(x_ref, g_ref, w_ref, wf_ref, bf_ref, h_ref, u_ref, gt_ref, q_ref, k_ref, v_ref, zf_ref, c_ref, cq_ref,
             qx_ref, kx_ref, carry):
        j = pl.program_id(1)

        @pl.when(j == 0)
        def _():
            carry[...] = jnp.zeros_like(carry)

        _, xh = _rms(x_ref[...])
        h = (xh * g_ref[...]).astype(BF16)
        h_ref[...] = h
        u_ref[...] = _dot_nt(h, w_ref[0:512, :])
        gt_ref[...] = _dot_nt(h, w_ref[512:1024, :])
        qb = _dot_nt(h, w_ref[1024:1536, :]).astype(BF16)
        kb = _dot_nt(h, w_ref[1536:2048, :]).astype(BF16)
        q_ref[...] = qb
        k_ref[...] = kb
        v_ref[...] = _dot_nt(h, w_ref[2048:2560, :]).astype(BF16)
        zf = _dot_nt(h, wf_ref[...]) + bf_ref[...]
        zf_ref[...] = zf
        lane = lax.broadcasted_iota(jnp.int32, zf.shape, 1)
        logf = jnp.where(lane < 8, jnp.minimum(zf, 0.0) - jnp.log(1.0 + jnp.exp(-jnp.abs(zf))), 0.0)
        row = lax.broadcasted_iota(jnp.int32, (TB, TB), 0)
        col = lax.broadcasted_iota(jnp.int32, (TB, TB), 1)
        c = _dot_01(row >= col, logf) + carry[0:1, :]
        carry[0:1, :] = c[TB - 1:TB, :]
        c_ref[...] = c
        cq = _dot_01(c, _head_expand(LANES, FOX_W))
        cq_ref[...] = cq
        hl = lax.broadcasted_iota(jnp.int32, (TB, LANES), 1)
        for hd in range(2 * N_PAIR):
            grp = slice((hd // 2) * LANES, (hd // 2 + 1) * LANES)
            swap = (lambda t: t) if hd % 2 == 0 else (lambda t: pltpu.roll(t, HEAD_D, 1))
            qf = swap(qb[:, grp].astype(F32) * (1.0 / math.sqrt(HEAD_D)))
            kf = swap(kb[:, grp].astype(F32))
            cv = cq[:, grp] if hd % 2 == 1 else pltpu.roll(cq[:, grp], HEAD_D, 1)
            hi = cv.astype(BF16).astype(F32)
            mid = (cv - hi).astype(BF16).astype(F32)
            lo = (cv - hi - mid).astype(BF16).astype(F32)
            pick = lambda a, b, c3, one_from, one_to: jnp.where(hl == a[0], a[1], jnp.where(hl == b[0], b[1], jnp.where(
                hl == c3[0], c3[1], jnp.where((hl >= one_from) & (hl < one_to), 1.0, 0.0))))
            qx = jnp.where(hl < HEAD_D, qf, pick((67, hi), (68, mid), (69, lo), 64, 67))
            kx = jnp.where(hl < HEAD_D, kf, pick((64, -hi), (65, -mid), (66, -lo), 67, 70))
            qx_ref[:, hd * LANES:(hd + 1) * LANES] = qx.astype(BF16)
            kx_ref[:, hd * LANES:(hd + 1) * LANES] = kx.astype(BF16)

    tok = lambda w: pl.BlockSpec((TB, w), lambda b, j: (b * nb + j, 0))
    outs = [(D, BF16), (512, F32), (512, F32), (512, BF16), (512, BF16), (512, BF16), (LANES, F32),
            (LANES, F32), (FOX_W, F32), (2 * FOX_W, BF16), (2 * FOX_W, BF16)]
    return _call(
        body, comm, name="fwd_in", grid=(B, nb),
        in_specs=[tok(D), _resident(g_mix), _resident(w_int), _resident(w_ft), _resident(bf_pad)],
        out_specs=[tok(w) for w, _ in outs],
        out_shape=[_sds((T, w), dt) for w, dt in outs],
        scratch_shapes=[pltpu.VMEM((8, LANES), F32)],
        args=(x2, g_mix, w_int, w_ft, bf_pad))


def _head_sum(n):
    hc = lax.broadcasted_iota(jnp.int32, (n, n), 1) // HEAD_D
    hr = lax.broadcasted_iota(jnp.int32, (n, n), 0) // HEAD_D
    return hc == hr


def _layernorm_silu(y, lg, lb):
    mu = jnp.mean(y, axis=-1, keepdims=True)
    yc = y - mu
    rs = lax.rsqrt(jnp.mean(yc * yc, axis=-1, keepdims=True) + EPS)
    n = yc * rs
    l = n * lg + lb
    return rs, n, l


SUB = 8


def _shifted_copies(cat, sh, rows):
    for r in range(1, SUB):
        sh[r, 0:rows, :] = cat[r:r + rows, :]


def _tap(cat, sh, off, rows, cols=slice(None)):
    r = off % SUB
    return cat[off:off + rows, cols] if r == 0 else sh[r, off - r:off - r + rows, cols]


CONV_ROWS = 128


def _conv_pieces(CB):
    rows = min(CONV_ROWS, CB)
    return [(r0, rows, slice(c0, c0 + LANES)) for c0 in range(0, CONV_CH, LANES) for r0 in range(0, CB, rows)]


def _conv_fwd(u, gt, cw, cb, lng, lnb, B, S, comm=None):
    T = B * S
    CB = min(256, S)
    nb = S // CB

    def body(u_ref, gt_ref, w_ref, cb_ref, lg_ref, lb_ref, y_ref, co_ref, acat, ash):
        j = pl.program_id(1)

        @pl.when(j == 0)
        def _():
            acat[0:CONV_HALO, :] = jnp.zeros((CONV_HALO, CONV_CH), F32)

        acat[CONV_HALO:CONV_HALO + CB, :] = u_ref[...] * _sig(gt_ref[...])
        _shifted_copies(acat, ash, CB + CONV_HALO - SUB)
        for r0, rows, cs in _conv_pieces(CB):
            acc = jnp.zeros((rows, LANES), F32) + cb_ref[:, cs]
            for k in range(CONV_K):
                acc = acc + w_ref[k:k + 1, cs] * _tap(acat, ash, r0 + CONV_HALO - (CONV_K - 1) + k, rows, cs)
            y_ref[r0:r0 + rows, cs] = acc
        acat[0:CONV_HALO, :] = acat[CB:CB + CONV_HALO, :]
        _, _, l = _layernorm_silu(y_ref[...], lg_ref[...], lb_ref[...])
        co_ref[...] = (l * _sig(l)).astype(BF16)

    tok = lambda w: pl.BlockSpec((CB, w), lambda b, j: (b * nb + j, 0))
    return _call(
        body, comm, name="conv_fwd", grid=(B, nb),
        in_specs=[tok(CONV_CH), tok(CONV_CH), _resident(cw), _resident(cb), _resident(lng), _resident(lnb)],
        out_specs=[tok(CONV_CH), tok(CONV_CH)],
        out_shape=[_sds((T, CONV_CH), F32), _sds((T, CONV_CH), BF16)],
        scratch_shapes=[pltpu.VMEM((CONV_HALO + CB, CONV_CH), F32),
                        pltpu.VMEM((SUB, CB + CONV_HALO - SUB, CONV_CH), F32)],
        args=(u, gt, cw, cb, lng, lnb))


def _fox_fwd(qx, kx, v, cq, B, S, comm=None):
    T = B * S
    TQ = min(256, S)
    nq = S // TQ
    one_lane = (HEAD_D, 0)

    def body(qa_ref, qb_ref, ka_ref, kb_ref, v_ref, cq_ref, o_ref, lse_ref, s_scr, s_odd, m_scr, acc_scr):
        i = pl.program_id(2)
        lane = lax.broadcasted_iota(jnp.int32, (TQ, LANES), 1)
        lo = lane < HEAD_D
        qh = (qa_ref[...], qb_ref[...])
        kh = (ka_ref, kb_ref)
        m_scr[...] = jnp.full(m_scr.shape, -1e30, F32)
        acc_scr[...] = jnp.zeros_like(acc_scr)
        row = lax.broadcasted_iota(jnp.int32, (TQ, TQ), 0)
        col = lax.broadcasted_iota(jnp.int32, (TQ, TQ), 1)
        wide = lambda x: jnp.concatenate([x, x], axis=1) if TQ == 2 * LANES else jnp.tile(x, (1, TQ // LANES))

        def scores(j, s_buf):
            start = pl.multiple_of(j * TQ, TQ)
            for h in range(2):
                s_buf[h] = _dot_nt(qh[h], kh[h][pl.ds(start, TQ), :])

        def softmax_step(j, s_buf, diagonal):
            start = pl.multiple_of(j * TQ, TQ)
            vj = v_ref[pl.ds(start, TQ), :]
            for h in range(2):
                def logits():
                    return jnp.where(col <= row, s_buf[h], -1e30) if diagonal else s_buf[h]

                m_old = m_scr[h]
                m_new = jnp.maximum(m_old, jnp.max(logits(), axis=-1, keepdims=True))
                alpha = jnp.exp(m_old - m_new)
                m_scr[h] = m_new
                p = jnp.exp(logits() - wide(m_new)).astype(BF16)
                vx = jnp.where(lane == one_lane[h], jnp.ones_like(vj), jnp.where(lo if h == 0 else ~lo, vj, jnp.zeros_like(vj)))
                acc_scr[h] = alpha * acc_scr[h] + _dot(p, vx)

        def two_blocks(jj, carry):
            j = 2 * jj
            scores(j + 1, s_odd)
            softmax_step(j, s_scr, False)
            scores(j + 2, s_scr)
            softmax_step(j + 1, s_odd, False)
            return carry

        scores(0, s_scr)
        lax.fori_loop(0, i // 2, two_blocks, 0)

        @pl.when(i % 2 == 0)
        def _():
            softmax_step(i, s_scr, True)

        @pl.when(i % 2 == 1)
        def _():
            scores(i, s_odd)
            softmax_step(i - 1, s_scr, False)
            softmax_step(i, s_odd, True)

        acc_a, acc_b = acc_scr[0], acc_scr[1]
        l_a = acc_a[:, one_lane[0]:one_lane[0] + 1]
        l_b = acc_b[:, one_lane[1]:one_lane[1] + 1]
        o_ref[...] = jnp.where(lo, acc_a / l_a, acc_b / l_b)
        lse_ref[...] = cq_ref[...] - jnp.where(lo, m_scr[0] + jnp.log(l_a), m_scr[1] + jnp.log(l_b))

    qspec = pl.BlockSpec((TQ, LANES), lambda b, p, i: (b * nq + i, p))
    kspec = pl.BlockSpec((S, LANES), lambda b, p, i: (b, p))
    qhead = lambda h: pl.BlockSpec((TQ, LANES), lambda b, p, i: (b * nq + i, 2 * p + h))
    khead = lambda h: pl.BlockSpec((S, LANES), lambda b, p, i: (b, 2 * p + h))
    return _call(
        body, comm, name="fox_fwd", grid=(B, N_PAIR, nq),
        in_specs=[qhead(0), qhead(1), khead(0), khead(1), kspec, qspec],
        out_specs=[qspec, qspec],
        out_shape=[_sds((T, FOX_W), F32), _sds((T, FOX_W), F32)],
        scratch_shapes=[pltpu.VMEM((2, TQ, TQ), F32), pltpu.VMEM((2, TQ, TQ), F32),
                        pltpu.VMEM((2, TQ, LANES), F32), pltpu.VMEM((2, TQ, LANES), F32)],
        args=(qx, qx, kx, kx, v, cq))


def _mem_kv(mem2, g_mem, w_mkv, B):
    def body(m_ref, g_ref, w_ref, mn_ref, km_ref, vm_ref):
        _, xh = _rms(m_ref[...])
        mn = (xh * g_ref[...]).astype(BF16)
        mn_ref[...] = mn
        for s in range(2):
            km_ref[:, 512 * s:512 * (s + 1)] = _dot(mn, w_ref[s]).astype(BF16)
            vm_ref[:, 512 * s:512 * (s + 1)] = _dot(mn, w_ref[2 + s]).astype(BF16)

    blk = pl.BlockSpec((MEM_LEN, D), lambda b: (b, 0))
    return pl.pallas_call(
        body, name="mem_kv", grid=(B,),
        in_specs=[blk, _resident(g_mem), _resident(w_mkv)],
        out_specs=[blk, blk, blk],
        out_shape=[_sds((B * MEM_LEN, D), BF16)] * 3,
        compiler_params=_params(1),
    )(mem2, g_mem, w_mkv)


def _mem_probs(qm, km):
    ps = []
    for h in range(MEM_HEADS):
        hs = slice(h * MEM_HD, (h + 1) * MEM_HD)
        lg = _dot_nt(qm[:, hs], km[:, hs]) * (1.0 / math.sqrt(MEM_HD))
        e = jnp.exp(lg - jnp.max(lg, axis=-1, keepdims=True))
        ps.append(e / jnp.sum(e, axis=-1, keepdims=True))
    return ps


def _fwd_mid(x2, co, o, km, vm, w_out, w_mq, w_mo, g_x, B, S, comm=None):
    T = B * S
    TB = min(512, S)
    nb = S // TB

    def body(x_ref, co_ref, o_ref, km_ref, vm_ref, wo_ref, wq_ref, wm_ref, g_ref,
             x1_ref, hx_ref, qm_ref, om_ref, x2_ref, cat_ref):
        cat_ref[:, 0:CONV_CH] = co_ref[...]
        cat_ref[:, CONV_CH:D] = o_ref[...].astype(BF16)
        x1 = x_ref[...] + _dot(cat_ref[...], wo_ref[...])
        x1_ref[...] = x1
        _, xh = _rms(x1)
        hx = (xh * g_ref[...]).astype(BF16)
        hx_ref[...] = hx
        qm = _dot(hx, wq_ref[...]).astype(BF16)
        qm_ref[...] = qm
        ps = _mem_probs(qm, km_ref[...])
        vmv = vm_ref[...]
        for h in range(MEM_HEADS):
            hs = slice(h * MEM_HD, (h + 1) * MEM_HD)
            om_ref[:, hs] = _dot(ps[h].astype(BF16), vmv[:, hs]).astype(BF16)
        x2_ref[...] = x1 + _dot(om_ref[...], wm_ref[...])

    tok = lambda w: pl.BlockSpec((TB, w), lambda b, j: (b * nb + j, 0))
    memb = pl.BlockSpec((MEM_LEN, D), lambda b, j: (b, 0))
    outs = [(D, F32), (D, BF16), (D, BF16), (D, BF16), (D, F32), (D, BF16)]
    return _call(
        body, comm, name="fwd_mid", grid=(B, nb),
        in_specs=[tok(D), tok(CONV_CH), tok(FOX_W), memb, memb, _resident(w_out), _resident(w_mq), _resident(w_mo),
                  _resident(g_x)],
        out_specs=[tok(w) for w, _ in outs],
        out_shape=[_sds((T, w), dt) for w, dt in outs],
        scratch_shapes=[],
        args=(x2, co, o, km, vm, w_out, w_mq, w_mo, g_x))


def _fwd_ffn(x2, tgt, w_gu, w_down, g_ffn, g_final, T):
    TB = min(256, T)
    nb = T // TB

    def body(x_ref, t_ref, wgu_ref, wd_ref, gf_ref, gl_ref, hf_ref, gu_ref, act_ref, dx3_ref, loss_ref, dgl_ref):
        i = pl.program_id(0)

        @pl.when(i == 0)
        def _():
            loss_ref[...] = jnp.zeros_like(loss_ref)
            dgl_ref[...] = jnp.zeros_like(dgl_ref)

        x2v = x_ref[...]
        _, xh = _rms(x2v)
        hf = (xh * gf_ref[...]).astype(BF16)
        hf_ref[...] = hf
        x3 = x2v
        for ch in range(D_FF // FF_CHUNK):
            c0 = ch * FF_CHUNK
            g = _dot(hf, wgu_ref[ch])
            u = _dot(hf, wgu_ref[2 + ch])
            gu_ref[:, c0:c0 + FF_CHUNK] = g
            gu_ref[:, D_FF + c0:D_FF + c0 + FF_CHUNK] = u
            act = (g * _sig(g) * u).astype(BF16)
            act_ref[:, c0:c0 + FF_CHUNK] = act
            x3 = x3 + _dot(act, wd_ref[c0:c0 + FF_CHUNK, :])
        r3, xh3 = _rms(x3)
        gl = gl_ref[...]
        e = xh3 * gl - t_ref[...]
        loss_ref[...] += jnp.sum(e * e) * (0.5 / D)
        dy = e * (1.0 / D)
        dx3, dgl = _rms_bwd(dy, xh3, r3, gl)
        dx3_ref[...] = dx3
        dgl_ref[...] += jnp.sum(dgl, axis=0, keepdims=True)

    tok = lambda w: pl.BlockSpec((TB, w), lambda i: (i, 0))
    return pl.pallas_call(
        body, name="fwd_ffn", grid=(nb,),
        in_specs=[tok(D), tok(D), _resident(w_gu), _resident(w_down), _resident(g_ffn), _resident(g_final)],
        out_specs=[tok(D), tok(2 * D_FF), tok(D_FF), tok(D), _acc_spec((1, LANES)), _acc_spec((1, D))],
        out_shape=[_sds((T, D), BF16), _sds((T, 2 * D_FF), F32), _sds((T, D_FF), BF16), _sds((T, D), F32),
                   _sds((1, LANES), F32), _sds((1, D), F32)],
        compiler_params=_params(1),
    )(x2, tgt, w_gu, w_down, g_ffn, g_final)


def _bwd_ffn(dx3, gu, x2, w_gu, w_down, g_ffn, T):
    TB = min(256, T)
    nb = T // TB

    def body(d_ref, gu_ref, x_ref, wgu_ref, wd_ref, gf_ref, dgu_ref, dx2_ref, dgf_ref):
        i = pl.program_id(0)

        @pl.when(i == 0)
        def _():
            dgf_ref[...] = jnp.zeros_like(dgf_ref)

        dx3v = d_ref[...]
        db = dx3v.astype(BF16)
        dhf = jnp.zeros((TB, D), F32)
        for ch in range(D_FF // FF_CHUNK):
            c0 = ch * FF_CHUNK
            dact = _dot_nt(db, wd_ref[c0:c0 + FF_CHUNK, :])
            g = gu_ref[:, c0:c0 + FF_CHUNK]
            u = gu_ref[:, D_FF + c0:D_FF + c0 + FF_CHUNK]
            sg = _sig(g)
            dg = (dact * u * sg * (1.0 + g * (1.0 - sg))).astype(BF16)
            du = (dact * g * sg).astype(BF16)
            dgu_ref[:, c0:c0 + FF_CHUNK] = dg
            dgu_ref[:, D_FF + c0:D_FF + c0 + FF_CHUNK] = du
            dhf = dhf + _dot_nt(dg, wgu_ref[ch]) + _dot_nt(du, wgu_ref[2 + ch])
        r2, xh2 = _rms(x_ref[...])
        dx, dg_tok = _rms_bwd(dhf, xh2, r2, gf_ref[...])
        dx2_ref[...] = dx3v + dx
        dgf_ref[...] += jnp.sum(dg_tok, axis=0, keepdims=True)

    tok = lambda w: pl.BlockSpec((TB, w), lambda i: (i, 0))
    return pl.pallas_call(
        body, name="bwd_ffn", grid=(nb,),
        in_specs=[tok(D), tok(2 * D_FF), tok(D), _resident(w_gu), _resident(w_down), _resident(g_ffn)],
        out_specs=[tok(2 * D_FF), tok(D), _acc_spec((1, D))],
        out_shape=[_sds((T, 2 * D_FF), BF16), _sds((T, D), F32), _sds((1, D), F32)],
        compiler_params=_params(1),
    )(dx3, gu, x2, w_gu, w_down, g_ffn)


def _bwd_mid(dx2, x1, qm, km, vm, o, w_mo, w_mq, w_out, g_x, B, S, comm=None):
    T = B * S
    TB = min(512, S)
    nb = S // TB
    inv = 1.0 / math.sqrt(MEM_HD)

    def body(d_ref, x1_ref, qm_ref, km_ref, vm_ref, o_ref, wm_ref, wq_ref, wo_ref, g_ref,
             dx1_ref, dqm_ref, dco_ref, do_ref, dd_ref, dkm_ref, dvm_ref, dgx_ref):
        b = pl.program_id(0)
        j = pl.program_id(1)

        @pl.when((b == 0) & (j == 0))
        def _():
            dgx_ref[...] = jnp.zeros_like(dgx_ref)

        @pl.when(j == 0)
        def _():
            dkm_ref[...] = jnp.zeros_like(dkm_ref)
            dvm_ref[...] = jnp.zeros_like(dvm_ref)

        dx2v = d_ref[...]
        dom = _dot_nt(dx2v.astype(BF16), wm_ref[...]).astype(BF16)
        qmv = qm_ref[...]
        kmv = km_ref[...]
        vmv = vm_ref[...]
        ps = _mem_probs(qmv, kmv)
        for h in range(MEM_HEADS):
            hs = slice(h * MEM_HD, (h + 1) * MEM_HD)
            p = ps[h]
            dp = _dot_nt(dom[:, hs], vmv[:, hs])
            ds = (p * (dp - jnp.sum(p * dp, axis=-1, keepdims=True))).astype(BF16)
            dqm_ref[:, hs] = (_dot(ds, kmv[:, hs]) * inv).astype(BF16)
            dkm_ref[:, hs] += _dot_tn(ds, qmv[:, hs]) * inv
            dvm_ref[:, hs] += _dot_tn(p.astype(BF16), dom[:, hs])
        dhx = _dot_nt(dqm_ref[...], wq_ref[...])
        r1, xh1 = _rms(x1_ref[...])
        dx, dg_tok = _rms_bwd(dhx, xh1, r1, g_ref[...])
        dx1 = dx2v + dx
        dx1_ref[...] = dx1
        dgx_ref[...] += jnp.sum(dg_tok, axis=0, keepdims=True)
        d1b = dx1.astype(BF16)
        dco_ref[...] = _dot_nt(d1b, wo_ref[0:CONV_CH, :])
        do = _dot_nt(d1b, wo_ref[CONV_CH:D, :])
        dob = do.astype(BF16)
        do_ref[...] = dob
        dd_ref[...] = _dot_01(dob.astype(F32) * o_ref[...], _head_sum(FOX_W))

    tok = lambda w: pl.BlockSpec((TB, w), lambda b, j: (b * nb + j, 0))
    memb = pl.BlockSpec((MEM_LEN, D), lambda b, j: (b, 0))
    outs = [(D, F32), (D, BF16), (CONV_CH, F32), (FOX_W, BF16), (FOX_W, F32)]
    return _call(
        body, comm, name="bwd_mid", grid=(B, nb),
        in_specs=[tok(D), tok(D), tok(D), memb, memb, tok(FOX_W), _resident(w_mo), _resident(w_mq), _resident(w_out),
                  _resident(g_x)],
        out_specs=[tok(w) for w, _ in outs] + [memb, memb, _acc_spec((1, D))],
        out_shape=[_sds((T, w), dt) for w, dt in outs] + [_sds((B * MEM_LEN, D), F32)] * 2 + [_sds((1, D), F32)],
        scratch_shapes=[],
        args=(dx2, x1, qm, km, vm, o, w_mo, w_mq, w_out, g_x))


def _mem_bwd(dkm, dvm, mem2, w_mkv, g_mem, B):
    def body(dk_ref, dv_ref, m_ref, w_ref, g_ref, dkv_ref, dg_ref):
        b = pl.program_id(0)

        @pl.when(b == 0)
        def _():
            dg_ref[...] = jnp.zeros_like(dg_ref)

        dk = dk_ref[...].astype(BF16)
        dv = dv_ref[...].astype(BF16)
        dkv_ref[:, 0:D] = dk
        dkv_ref[:, D:2 * D] = dv
        dmn = jnp.zeros((MEM_LEN, D), F32)
        for s in range(2):
            dmn = dmn + _dot_nt(dk[:, 512 * s:512 * (s + 1)], w_ref[s]) + _dot_nt(dv[:, 512 * s:512 * (s + 1)], w_ref[2 + s])
        _, xh = _rms(m_ref[...])
        dg_ref[...] += jnp.sum(dmn * xh, axis=0, keepdims=True)

    blk = pl.BlockSpec((MEM_LEN, D), lambda b: (b, 0))
    return pl.pallas_call(
        body, name="mem_bwd", grid=(B,),
        in_specs=[blk, blk, blk, _resident(w_mkv), _resident(g_mem)],
        out_specs=[pl.BlockSpec((MEM_LEN, 2 * D), lambda b: (b, 0)), _acc_spec((1, D))],
        out_shape=[_sds((B * MEM_LEN, 2 * D), BF16), _sds((1, D), F32)],
        compiler_params=_params(1),
    )(dkm, dvm, mem2, w_mkv, g_mem)


def _fox_bwd(q, k, v, do, bias, dd, ckT, B, S, comm=None):
    T = B * S
    TK = min(256, S)
    nk = S // TK
    scale = 1.0 / math.sqrt(HEAD_D)

    def body(q_ref, k_ref, v_ref, do_ref, bias_ref, dd_ref, ck_ref, dq_ref, dk_ref, dv_ref, dc_ref, dcq_ref,
             dq_acc, rs_acc, s_scr, dp_scr, s_odd, dp_odd, dk_acc, dv_acc, dc_acc):
        j = pl.program_id(2)

        @pl.when(j == 0)
        def _():
            dq_acc[...] = jnp.zeros_like(dq_acc)
            rs_acc[...] = jnp.zeros_like(rs_acc)

        dk_acc[...] = jnp.zeros_like(dk_acc)
        dv_acc[...] = jnp.zeros_like(dv_acc)
        dc_acc[...] = jnp.zeros_like(dc_acc)
        lane = lax.broadcasted_iota(jnp.int32, (TK, LANES), 1)
        lo = lane < HEAD_D
        ks = k_ref[...] * jnp.asarray(scale, BF16)
        v2 = v_ref[...]
        zero = jnp.zeros_like(ks)
        kh = (jnp.where(lo, ks, zero), jnp.where(lo, zero, ks))
        vh = (jnp.where(lo, v2, zero), jnp.where(lo, zero, v2))
        kstart = pl.multiple_of(j * TK, TK)
        ckh = tuple(ck_ref[0, 0, h:h + 1, pl.ds(kstart, TK)] for h in range(2))
        row = lax.broadcasted_iota(jnp.int32, (TK, TK), 0)
        col = lax.broadcasted_iota(jnp.int32, (TK, TK), 1)
        wide = lambda x: jnp.concatenate([x, x], axis=1) if TK == 2 * LANES else jnp.tile(x, (1, TK // LANES))

        def scores(i, s_buf, dp_buf):
            start = pl.multiple_of(i * TK, TK)
            qi = q_ref[pl.ds(start, TK), :]
            doi = do_ref[pl.ds(start, TK), :]
            for h in range(2):
                s_buf[h] = _dot_nt(qi, kh[h])
                dp_buf[h] = _dot_nt(doi, vh[h])

        def grads(i, s_buf, dp_buf, diagonal):
            start = pl.multiple_of(i * TK, TK)
            qi = q_ref[pl.ds(start, TK), :]
            doi = do_ref[pl.ds(start, TK), :]
            bias2 = bias_ref[pl.ds(start, TK), :]
            dd2 = dd_ref[pl.ds(start, TK), :]
            for h in range(2):
                hc = slice(h * HEAD_D, h * HEAD_D + 1)
                bias = jnp.broadcast_to(bias2[:, hc], (TK, LANES))
                ddh = jnp.broadcast_to(dd2[:, hc], (TK, LANES))
                p = jnp.exp((s_buf[h] - ckh[h]) + wide(bias))
                if diagonal:
                    p = jnp.where(col <= row, p, 0.0)
                ds = p * (dp_buf[h] - wide(ddh))
                dc_acc[h, 0:1, :] += jnp.sum(ds, axis=0, keepdims=True)
                rs_acc[h, pl.ds(start, TK), :] += jnp.sum(ds, axis=1, keepdims=True)
                pb = p.astype(BF16)
                dsb = ds.astype(BF16)
                dv_acc[h] += _dot_tn(pb, doi)
                dk_acc[h] += _dot_tn(dsb, qi)
                dq_acc[pl.ds(start, TK), :] += _dot(dsb, kh[h])

        n_off = nk - 1 - j
        block = lambda t: jnp.where(t < n_off, j + 1 + t, j)

        def two_blocks(tt, carry):
            t = 2 * tt
            scores(block(t + 1), s_odd, dp_odd)
            grads(block(t), s_scr, dp_scr, False)
            scores(block(t + 2), s_scr, dp_scr)
            grads(block(t + 1), s_odd, dp_odd, False)
            return carry

        scores(block(0), s_scr, dp_scr)
        lax.fori_loop(0, n_off // 2, two_blocks, 0)

        @pl.when(n_off % 2 == 0)
        def _():
            grads(j, s_scr, dp_scr, True)

        @pl.when(n_off % 2 == 1)
        def _():
            scores(j, s_odd, dp_odd)
            grads(nk - 1, s_scr, dp_scr, False)
            grads(j, s_odd, dp_odd, True)

        dk_ref[...] = (jnp.where(lo, dk_acc[0], dk_acc[1]) * scale).astype(BF16)
        dv_ref[...] = jnp.where(lo, dv_acc[0], dv_acc[1]).astype(BF16)
        sub = lax.broadcasted_iota(jnp.int32, (8, TK), 0)
        dca = dc_acc[0, 0:1, :]
        dcb = dc_acc[1, 0:1, :]
        dc_ref[0, 0] = jnp.where(sub == 0, -dca, jnp.where(sub == 1, -dcb, 0.0))

        @pl.when(j == nk - 1)
        def _():
            dq_ref[...] = dq_acc[...].astype(BF16)
            lo_s = lax.broadcasted_iota(jnp.int32, (S, LANES), 1) < HEAD_D
            dcq_ref[...] = jnp.where(lo_s, rs_acc[0], rs_acc[1])

    full = pl.BlockSpec((S, LANES), lambda b, p, j: (b, p))
    blk = pl.BlockSpec((TK, LANES), lambda b, p, j: (b * nk + j, p))
    return _call(
        body, comm, name="fox_bwd", grid=(B, N_PAIR, nk),
        in_specs=[full, blk, blk, full, full, full, pl.BlockSpec((1, 1, 8, S), lambda b, p, j: (b, p, 0, 0))],
        out_specs=[full, blk, blk, pl.BlockSpec((1, 1, 8, TK), lambda b, p, j: (b, p, 0, j)), full],
        out_shape=[_sds((T, FOX_W), BF16), _sds((T, FOX_W), BF16), _sds((T, FOX_W), BF16),
                   _sds((B, N_PAIR, 8, S), F32), _sds((T, FOX_W), F32)],
        scratch_shapes=[pltpu.VMEM((S, LANES), F32), pltpu.VMEM((2, S, 1), F32),
                        pltpu.VMEM((2, TK, TK), F32), pltpu.VMEM((2, TK, TK), F32),
                        pltpu.VMEM((2, TK, TK), F32), pltpu.VMEM((2, TK, TK), F32),
                        pltpu.VMEM((2, TK, LANES), F32), pltpu.VMEM((2, TK, LANES), F32), pltpu.VMEM((2, 8, TK), F32)],
        args=(q, k, v, do, bias, dd, ckT))


def _fgate_bwd(dc8, zf, B, S):
    T = B * S
    TB = min(512, S)
    nb = S // TB

    def body(dc_ref, zf_ref, dzf_ref, dbf_ref, carry):
        b = pl.program_id(0)
        j = pl.program_id(1)

        @pl.when((b == 0) & (j == 0))
        def _():
            dbf_ref[...] = jnp.zeros_like(dbf_ref)

        @pl.when(j == 0)
        def _():
            carry[...] = jnp.zeros_like(carry)

        dc = dc_ref[...]
        row = lax.broadcasted_iota(jnp.int32, (TB, TB), 0)
        col = lax.broadcasted_iota(jnp.int32, (TB, TB), 1)
        dlogf = _dot_01(col >= row, dc) + carry[0:1, :]
        carry[0:1, :] = dlogf[0:1, :]
        lane = lax.broadcasted_iota(jnp.int32, dc.shape, 1)
        dzf = jnp.where(lane < 8, dlogf * _sig(-zf_ref[...]), 0.0)
        dzf_ref[...] = dzf.astype(BF16)
        dbf_ref[...] += jnp.sum(dzf, axis=0, keepdims=True)

    tok = pl.BlockSpec((TB, LANES), lambda b, j: (b * nb + (nb - 1 - j), 0))
    return pl.pallas_call(
        body, name="fgate_bwd", grid=(B, nb),
        in_specs=[tok, tok],
        out_specs=[tok, _acc_spec((1, LANES))],
        out_shape=[_sds((T, LANES), BF16), _sds((1, LANES), F32)],
        scratch_shapes=[pltpu.VMEM((8, LANES), F32)],
        compiler_params=_params(2),
    )(dc8, zf)


def _conv_bwd(dco, y, u, gt, cw, lng, lnb, B, S, comm=None):
    T = B * S
    CB = min(256, S)
    nb = S // CB
    hb = CB // CONV_HALO

    def body(dco_ref, y_ref, u_ref, gt_ref, up_ref, gp_ref, w_ref, lg_ref, lb_ref,
             du_ref, dgt_ref, dw_ref, vec_ref, acat, dycat, ash, dysh):
        b = pl.program_id(0)
        j = pl.program_id(1)
        jr = nb - 1 - j

        @pl.when((b == 0) & (j == 0))
        def _():
            dw_ref[...] = jnp.zeros_like(dw_ref)
            vec_ref[...] = jnp.zeros_like(vec_ref)

        @pl.when(j == 0)
        def _():
            dycat[CB:CB + CONV_HALO, :] = jnp.zeros((CONV_HALO, CONV_CH), F32)

        lg = lg_ref[...]
        rs, n, l = _layernorm_silu(y_ref[...], lg, lb_ref[...])
        sg = _sig(l)
        dl = dco_ref[...] * (sg * (1.0 + l * (1.0 - sg)))
        dn = dl * lg
        dy = rs * (dn - jnp.mean(dn, axis=-1, keepdims=True) - n * jnp.mean(dn * n, axis=-1, keepdims=True))
        vec_ref[0:1, :] += jnp.sum(dy, axis=0, keepdims=True)
        vec_ref[1:2, :] += jnp.sum(dl * n, axis=0, keepdims=True)
        vec_ref[2:3, :] += jnp.sum(dl, axis=0, keepdims=True)
        dycat[0:CB, :] = dy
        acat[0:CONV_HALO, :] = jnp.where(jr > 0, up_ref[...] * _sig(gp_ref[...]), 0.0)
        acat[CONV_HALO:CONV_HALO + CB, :] = u_ref[...] * _sig(gt_ref[...])
        _shifted_copies(acat, ash, CB + CONV_HALO - SUB)
        _shifted_copies(dycat, dysh, CB + CONV_HALO - SUB)
        for r0, rows, cs in _conv_pieces(CB):
            dyp = dycat[r0:r0 + rows, cs]
            da = jnp.zeros((rows, LANES), F32)
            for k in range(CONV_K):
                da = da + w_ref[k:k + 1, cs] * _tap(dycat, dysh, r0 + CONV_K - 1 - k, rows, cs)
                dw_ref[k:k + 1, cs] += jnp.sum(dyp * _tap(acat, ash, r0 + CONV_HALO - (CONV_K - 1) + k, rows, cs),
                                               axis=0, keepdims=True)
            uv = u_ref[r0:r0 + rows, cs]
            sgt = _sig(gt_ref[r0:r0 + rows, cs])
            du_ref[r0:r0 + rows, cs] = (da * sgt).astype(BF16)
            dgt_ref[r0:r0 + rows, cs] = (da * uv * sgt * (1.0 - sgt)).astype(BF16)
        dycat[CB:CB + CONV_HALO, :] = dycat[0:CONV_HALO, :]

    tok = lambda w: pl.BlockSpec((CB, w), lambda b, j: (b * nb + (nb - 1 - j), 0))
    prev = pl.BlockSpec((CONV_HALO, CONV_CH), lambda b, j: (jnp.maximum((b * nb + (nb - 1 - j)) * hb - 1, 0), 0))
    return _call(
        body, comm, name="conv_bwd", grid=(B, nb),
        in_specs=[tok(CONV_CH), tok(CONV_CH), tok(CONV_CH), tok(CONV_CH), prev, prev, _resident(cw), _resident(lng),
                  _resident(lnb)],
        out_specs=[tok(CONV_CH), tok(CONV_CH), _acc_spec((CONV_HALO, CONV_CH)), _acc_spec((8, CONV_CH))],
        out_shape=[_sds((T, CONV_CH), BF16), _sds((T, CONV_CH), BF16), _sds((CONV_HALO, CONV_CH), F32),
                   _sds((8, CONV_CH), F32)],
        scratch_shapes=[pltpu.VMEM((CONV_HALO + CB, CONV_CH), F32), pltpu.VMEM((CB + CONV_HALO, CONV_CH), F32),
                        pltpu.VMEM((SUB, CB + CONV_HALO - SUB, CONV_CH), F32),
                        pltpu.VMEM((SUB, CB + CONV_HALO - SUB, CONV_CH), F32)],
        args=(dco, y, u, gt, u, gt, cw, lng, lnb))


def _bwd_in(dz, w_int, w_ft, x2, dx1, g_mix, T, comm=None):
    TB = min(512, T)
    nb = T // TB

    def body(dz_ref, w_ref, wf_ref, x_ref, d1_ref, g_ref, gx_ref, dg_ref):
        i = pl.program_id(0)

        @pl.when(i == 0)
        def _():
            dg_ref[...] = jnp.zeros_like(dg_ref)

        dh = _dot(dz_ref[:, 0:OFF_F], w_ref[0:OFF_F, :]) + _dot(dz_ref[:, OFF_F:D_IN_PAD], wf_ref[...])
        r0, xh0 = _rms(x_ref[...])
        dx, dg_tok = _rms_bwd(dh, xh0, r0, g_ref[...])
        gx_ref[...] = d1_ref[...] + dx
        dg_ref[...] += jnp.sum(dg_tok, axis=0, keepdims=True)

    tok = lambda w: pl.BlockSpec((TB, w), lambda i: (i, 0))
    return _call(
        body, comm, name="bwd_in", grid=(nb,),
        in_specs=[tok(D_IN_PAD), _resident(w_int), _resident(w_ft), tok(D), tok(D), _resident(g_mix)],
        out_specs=[tok(D), _acc_spec((1, D))],
        out_shape=[_sds((T, D), F32), _sds((1, D), F32)],
        scratch_shapes=[],
        args=(dz, w_int, w_ft, x2, dx1, g_mix))


def _dw(a, b, name, tn, slabs=False, tk=None, rows=None):
    T, K = a.shape
    N = b.shape[1]
    tk = tk or (K if K <= 1024 else K // 2)
    tt = min(1024, T)
    nt = T // tt

    def body(a_ref, b_ref, o_ref, acc):
        t = pl.program_id(2)

        @pl.when(t == 0)
        def _():
            acc[...] = jnp.zeros_like(acc)

        acc[...] += _dot_tn(a_ref[...].astype(BF16), b_ref[...].astype(BF16))

        @pl.when(t == nt - 1)
        def _():
            o_ref[...] = acc[...]

    return pl.pallas_call(
        body, name=name, grid=(K // tk, N // tn, nt),
        in_specs=[pl.BlockSpec((tt, tk), lambda i, j, t: (t, i)), pl.BlockSpec((tt, tn), lambda i, j, t: (t, j))],
        out_specs=(pl.BlockSpec((None, tk, tn), lambda i, j, t: (j, i, 0)) if slabs
                   else pl.BlockSpec((tk, tn), lambda i, j, t: (i, j))),
        out_shape=_sds((N // tn, K, tn) if slabs else (rows or K, N), F32),
        scratch_shapes=[pltpu.VMEM((tk, tn), F32)],
        compiler_params=_params(3),
    )(a, b)


def _pos():
    return lax.axis_index("x"), lax.axis_index("y"), lax.axis_index("c")


def _remote(src, dst, ssem, rsem, to):
    return pltpu.make_async_remote_copy(src_ref=src, dst_ref=dst, send_sem=ssem, recv_sem=rsem, device_id=to,
                                        device_id_type=MESH)


def _split_axis(shape):
    return 0 if shape[0] % 32 == 0 else 1


def _half_shape(shape, parts=2):
    return (shape[0] // parts, shape[1]) if _split_axis(shape) == 0 else (shape[0], shape[1] // parts)


def _half(shape, c):
    R, C = shape
    if _split_axis(shape) == 0:
        return (pl.ds(pl.multiple_of(c * (R // 2), 16), R // 2), slice(None))
    return (slice(None), pl.ds(pl.multiple_of(c * (C // 2), LANES), C // 2))


def _half_block(shape, parts, lead, which):
    blk = _half_shape(shape, parts)
    idx = (which, 0) if _split_axis(shape) == 0 else (0, which)
    return blk, tuple(lead) + idx


class _Comm:
    def __init__(self, ins, out_shapes, sems, start, finish):
        self.ins, self.out_shapes, self.sems, self.start, self.finish = list(ins), list(out_shapes), list(sems), start, finish


def _ag_comm(shards):
    n = len(shards)

    def parts(ins, outs, sems):
        send_sems, recv_sems, local_sems = sems
        x, y, c = _pos()
        me, sib = (x, y, c), (x, y, 1 - c)
        chips = [(1 - x, y), (x, 1 - y), (1 - x, 1 - y)]

        def rows(w, px, py, pc):
            return outs[w].at[(2 * px + py,) + _half(shards[w].shape, pc)]

        def copy(w, k, block, to, src=None):
            return _remote(rows(w, *block) if src is None else src, rows(w, *block), send_sems.at[w, k],
                           recv_sems.at[w, k], to)

        mine, first = [], []
        for w in range(n):
            src = ins[w].at[_half(shards[w].shape, c)]
            mine.append(pltpu.make_async_copy(src, rows(w, *me), local_sems.at[w]))
            first += [copy(w, 0, me, sib, src=src)] + [copy(w, 1 + j, me, (*chip, c), src=src) for j, chip in enumerate(chips)]
        return c, me, sib, chips, copy, mine, first

    def start(ins, outs, sems):
        _, _, _, _, _, mine, first = parts(ins, outs, sems)
        for cp in mine + first:
            cp.start()

    def finish(ins, outs, sems):
        c, me, sib, chips, copy, mine, first = parts(ins, outs, sems)
        passed = []
        for w in range(n):
            for j, chip in enumerate(chips):
                copy(w, 1 + j, (*chip, c), me).wait_recv()
                passed.append(copy(w, 4 + j, (*chip, c), sib))
                passed[-1].start()
        for w in range(n):
            copy(w, 0, sib, me).wait_recv()
            for j, chip in enumerate(chips):
                copy(w, 4 + j, (*chip, 1 - c), me).wait_recv()
        for cp in first + passed:
            cp.wait_send()
        for cp in mine:
            cp.wait()

    D7 = pltpu.SemaphoreType.DMA((n, 7))
    return _Comm(shards, [_sds((4,) + s.shape, s.dtype) for s in shards], [D7, D7, pltpu.SemaphoreType.DMA((n,))],
                 start, finish)


def _sibling_comm(gs):
    n = len(gs)

    def copies(ins, outs, sems):
        send_sems, recv_sems = sems
        x, y, c = _pos()
        return [_remote(ins[w].at[(s,) + _half(gs[w].shape[1:], 1 - c)], outs[w].at[s], send_sems.at[w, s],
                        recv_sems.at[w, s], (x, y, 1 - c)) for w in range(n) for s in range(4)]

    def start(ins, outs, sems):
        for cp in copies(ins, outs, sems):
            cp.start()

    def finish(ins, outs, sems):
        for cp in copies(ins, outs, sems):
            cp.wait()

    D4 = pltpu.SemaphoreType.DMA((n, 4))
    return _Comm(gs, [_sds((4,) + _half_shape(g.shape[1:]), F32) for g in gs], [D4, D4], start, finish)


def _ici_comm(pbs):
    n = len(pbs)

    def copies(ins, outs, sems):
        send_sems, recv_sems = sems
        x, y, c = _pos()
        return [_remote(ins[w].at[2 * tx + ty], outs[w].at[j], send_sems.at[w, j], recv_sems.at[w, j], (tx, ty, c))
                for w in range(n) for j, (tx, ty) in enumerate([(1 - x, y), (x, 1 - y), (1 - x, 1 - y)])]

    def start(ins, outs, sems):
        for cp in copies(ins, outs, sems):
            cp.start()

    def finish(ins, outs, sems):
        for cp in copies(ins, outs, sems):
            cp.wait()

    D3 = pltpu.SemaphoreType.DMA((n, 3))
    return _Comm(pbs, [_sds((3,) + p.shape[1:], BF16) for p in pbs], [D3, D3], start, finish)


def _join(*comms):
    counts = [(len(c.ins), len(c.out_shapes), len(c.sems)) for c in comms]

    def each(which):
        def run(ins, outs, sems):
            i = o = k = 0
            for c, (ni, no, nk) in zip(comms, counts):
                getattr(c, which)(ins[i:i + ni], outs[o:o + no], sems[k:k + nk])
                i, o, k = i + ni, o + no, k + nk
        return run

    return _Comm(sum((c.ins for c in comms), []), sum((c.out_shapes for c in comms), []),
                 sum((c.sems for c in comms), []), each("start"), each("finish"))


def _run_comm(comm, name):
    ni, no = len(comm.ins), len(comm.out_shapes)

    def body(*refs):
        ins, outs, sems = refs[:ni], refs[ni:ni + no], refs[ni + no:]
        comm.start(ins, outs, sems)
        comm.finish(ins, outs, sems)

    return pl.pallas_call(body, name=name, out_shape=comm.out_shapes, in_specs=[ANY] * ni, out_specs=[ANY] * no,
                          scratch_shapes=comm.sems)(*comm.ins)


def _call(body, comm, *, name, grid, in_specs, out_specs, out_shape, scratch_shapes, args):
    n_grid = len(grid)
    if comm is None:
        res = pl.pallas_call(body, name=name, grid=grid, in_specs=in_specs, out_specs=out_specs, out_shape=out_shape,
                             scratch_shapes=scratch_shapes, compiler_params=_params(n_grid))(*args)
        return list(res), []
    n_in, n_out, n_scr = len(in_specs), len(out_specs), len(scratch_shapes)
    ni, no = len(comm.ins), len(comm.out_shapes)

    def carried(*refs):
        ins, refs = refs[:n_in], refs[n_in:]
        cins, refs = refs[:ni], refs[ni:]
        outs, refs = refs[:n_out], refs[n_out:]
        couts, refs = refs[:no], refs[no:]
        scr, csems = refs[:n_scr], refs[n_scr:]
        ids = [pl.program_id(ax) for ax in range(n_grid)]
        first = functools.reduce(jnp.logical_and, [i == 0 for i in ids])
        last = functools.reduce(jnp.logical_and, [i == g - 1 for i, g in zip(ids, grid)])

        @pl.when(first)
        def _():
            comm.start(cins, couts, csems)

        body(*ins, *outs, *scr)

        @pl.when(last)
        def _():
            comm.finish(cins, couts, csems)

    res = pl.pallas_call(
        carried, name=name, grid=grid, in_specs=list(in_specs) + [ANY] * ni, out_specs=list(out_specs) + [ANY] * no,
        out_shape=list(out_shape) + comm.out_shapes, scratch_shapes=list(scratch_shapes) + comm.sems,
        compiler_params=_params(n_grid))(*args, *comm.ins)
    return list(res[:n_out]), list(res[n_out:])


def _sibling_share(gs):
    n = len(gs)

    def body(*refs):
        outs = refs[n:2 * n]
        send_sems, recv_sems = refs[2 * n:]
        x, y, c = _pos()
        cps = []
        for w in range(n):
            mine = outs[w].at[_half(gs[w].shape, c)]
            cps.append(_remote(mine, mine, send_sems.at[w], recv_sems.at[w], (x, y, 1 - c)))
            cps[-1].start()
        for cp in cps:
            cp.wait()

    return pl.pallas_call(
        body, name="rs_share", out_shape=[_sds(g.shape, F32) for g in gs],
        in_specs=[ANY] * n, out_specs=[ANY] * n, input_output_aliases={w: w for w in range(n)},
        scratch_shapes=[pltpu.SemaphoreType.DMA((n,)), pltpu.SemaphoreType.DMA((n,))],
    )(*gs)


def _small_allreduce(v, name):
    P = v.shape[0]
    vm = pl.BlockSpec(memory_space=pltpu.VMEM)

    def body(v_ref, o_ref, gath, send_sems, recv_sems):
        x, y, c = _pos()
        me = 4 * x + 2 * y + c
        gath[me] = v_ref[...]
        cps = []
        for r in range(1, 8):
            tx = (1 - x) if r & 4 else x
            ty = (1 - y) if r & 2 else y
            tc = (1 - c) if r & 1 else c
            cps.append(_remote(v_ref, gath.at[me], send_sems.at[r - 1], recv_sems.at[r - 1], (tx, ty, tc)))
            cps[-1].start()
        for cp in cps:
            cp.wait()
        acc = gath[0]
        for d in range(1, 8):
            acc = acc + gath[d]
        o_ref[...] = acc

    return pl.pallas_call(
        body, name=name, out_shape=_sds((P, LANES), F32), in_specs=[vm], out_specs=vm,
        scratch_shapes=[pltpu.VMEM((8, P, LANES), F32), pltpu.SemaphoreType.DMA((7,)), pltpu.SemaphoreType.DMA((7,))],
    )(v)


def _chip_sum(g, rcv, pos, name):
    shard = g.shape[1:]
    hs = _half_shape(shard)

    def body(pos_ref, g_ref, r_ref, o_ref):
        o_ref[...] = (g_ref[...] + r_ref[...]).astype(BF16)

    return pl.pallas_call(
        body, name=name, out_shape=_sds((4,) + hs, BF16),
        grid_spec=pltpu.PrefetchScalarGridSpec(
            num_scalar_prefetch=1, grid=(4,),
            in_specs=[pl.BlockSpec((1,) + hs, lambda s, pos: _half_block(shard, 2, (s,), pos[0])[1]),
                      pl.BlockSpec((1,) + hs, lambda s, pos: (s, 0, 0))],
            out_specs=pl.BlockSpec((1,) + hs, lambda s, pos: (s, 0, 0))),
        compiler_params=_params(1),
    )(pos, g, rcv)


def _final_sum(g, rcv, rc, pos, name):
    shard = g.shape[1:]
    qs = _half_shape(shard, 4)

    def body(pos_ref, g_ref, r_ref, rc_ref, o_ref):
        acc = g_ref[0] + r_ref[0]
        for j in range(3):
            acc = acc + rc_ref[j].astype(F32)
        o_ref[...] = acc

    return pl.pallas_call(
        body, name=name, out_shape=_sds(shard, F32),
        grid_spec=pltpu.PrefetchScalarGridSpec(
            num_scalar_prefetch=1, grid=(2,),
            in_specs=[pl.BlockSpec((1,) + qs, lambda i, pos: _half_block(shard, 4, (pos[1],), pos[0] * 2 + i)[1]),
                      pl.BlockSpec((1,) + qs, lambda i, pos: _half_block(shard, 4, (pos[1],), i)[1]),
                      pl.BlockSpec((3,) + qs, lambda i, pos: _half_block(shard, 4, (0,), i)[1])],
            out_specs=pl.BlockSpec(qs, lambda i, pos: _half_block(shard, 4, (), pos[0] * 2 + i)[1])),
        compiler_params=_params(1),
    )(pos, g, rcv, rc)


def _adamw_math(w, g, m, v):
    m = ADAM_B1 * m + (1.0 - ADAM_B1) * g
    v = ADAM_B2 * v + (1.0 - ADAM_B2) * (g * g)
    m_hat = m / (1.0 - ADAM_B1 ** ADAM_STEP)
    v_hat = v / (1.0 - ADAM_B2 ** ADAM_STEP)
    delta = -ADAM_LR * (m_hat / (jnp.sqrt(v_hat) + ADAM_EPS) + ADAM_WD * w)
    return delta, m, v


def _adamw(w, g, m, v, name, blk_shape):
    R, C = w.shape

    def body(w_ref, g_ref, m_ref, v_ref, go_ref, d_ref, nm_ref, nv_ref):
        g = g_ref[...]
        d, nm, nv = _adamw_math(w_ref[...], g, m_ref[...], v_ref[...])
        go_ref[...] = g
        d_ref[...] = d
        nm_ref[...] = nm
        nv_ref[...] = nv

    blk = pl.BlockSpec(blk_shape, lambda i, j: (i, j))
    return pl.pallas_call(
        body, name=name, grid=(R // blk_shape[0], C // blk_shape[1]), in_specs=[blk] * 4, out_specs=[blk] * 4,
        out_shape=[_sds((R, C), F32)] * 4, compiler_params=_params(2),
    )(w, g, m, v)


SMALL = (("g_mix", 8), ("b_f", 8), ("conv_w", None), ("conv_b", 8), ("ln_g", 8), ("ln_b", 8), ("g_x", 8), ("g_mem", 8),
         ("g_ffn", 8), ("g_final", 8), ("loss", 8))


def _pack_small(parts, conv_rows):
    rows = []
    for name, n in SMALL:
        if name not in parts:
            continue
        n = conv_rows if n is None else n
        flat = parts[name].reshape(-1).astype(F32)
        flat = jnp.pad(flat, (0, n * LANES - flat.shape[0]))
        rows.append(flat.reshape(n, LANES))
    return jnp.concatenate(rows, axis=0)


def _unpack_small(p, shapes, conv_rows):
    out, off = {}, 0
    for name, n in SMALL:
        if name not in shapes:
            continue
        n = conv_rows if n is None else n
        size = math.prod(shapes[name])
        out[name] = p[off:off + n].reshape(-1)[:size].reshape(shapes[name])
        off += n
    return out


def kernel(x, mem, g_mix, w_in, b_f, conv_w, conv_b, ln_g, ln_b, w_out, g_x, g_mem, w_mq, w_mkv, w_mo, g_ffn, w_gu, w_down, g_final, loss_target, m_g_mix, m_w_in, m_b_f, m_conv_w, m_conv_b, m_ln_g, m_ln_b, m_w_out, m_g_x, m_g_mem, m_w_mq, m_w_mkv, m_w_mo, m_g_ffn, m_w_gu, m_w_down, m_g_final, v_g_mix, v_w_in, v_b_f, v_conv_w, v_conv_b, v_ln_g, v_ln_b, v_w_out, v_g_x, v_g_mem, v_w_mq, v_w_mkv, v_w_mo, v_g_ffn, v_w_gu, v_w_down, v_g_final):
    names = ["g_mix", "w_in", "b_f", "conv_w", "conv_b", "ln_g", "ln_b", "w_out", "g_x", "g_mem", "w_mq", "w_mkv",
             "w_mo", "g_ffn", "w_gu", "w_down", "g_final"]
    W = dict(zip(names, [g_mix, w_in, b_f, conv_w, conv_b, ln_g, ln_b, w_out, g_x, g_mem, w_mq, w_mkv, w_mo, g_ffn,
                         w_gu, w_down, g_final]))
    M = dict(zip(names, [m_g_mix, m_w_in, m_b_f, m_conv_w, m_conv_b, m_ln_g, m_ln_b, m_w_out, m_g_x, m_g_mem, m_w_mq,
                         m_w_mkv, m_w_mo, m_g_ffn, m_w_gu, m_w_down, m_g_final]))
    V = dict(zip(names, [v_g_mix, v_w_in, v_b_f, v_conv_w, v_conv_b, v_ln_g, v_ln_b, v_w_out, v_g_x, v_g_mem, v_w_mq,
                         v_w_mkv, v_w_mo, v_g_ffn, v_w_gu, v_w_down, v_g_final]))
    big_names = [n for n, _, _, _ in BIG]
    B, S, _ = x.shape
    T = B * S
    mx, my, mc = _pos()
    chip = 2 * mx + my
    pos = jnp.stack([mc, chip]).astype(jnp.int32)

    shard2d = lambda a: a.reshape(a.shape[-2], a.shape[-1])
    big2d = lambda d, n: shard2d(d[n]).T if n == "w_in" else shard2d(d[n])
    shard_bf = {n: big2d(W, n).astype(BF16) for n in big_names}
    ag_mid = ["w_mkv", "w_out", "w_mq", "w_mo"]
    ag_ffn = ["w_gu", "w_down"]
    cw_mine = jnp.pad(shard2d(conv_w), ((0, 1), (0, 0)))
    w_in_slab, cw_slab = _run_comm(_ag_comm([shard_bf["w_in"], cw_mine]), "ag_w_in")
    slab = {"w_in": w_in_slab}
    w_int = w_in_slab.reshape(D_IN, D)
    w_ft = jnp.pad(w_int[OFF_F:D_IN], ((0, D_IN_PAD - D_IN), (0, 0)))
    cw = jnp.transpose(cw_slab, (1, 0, 2)).reshape(CONV_HALO, CONV_CH)

    row = lambda a: a.reshape(1, -1)
    bf_pad = jnp.pad(row(b_f), ((0, 0), (0, LANES - 8)))
    x2d = x.reshape(T, D)
    mem2d = mem.reshape(B * MEM_LEN, D)
    tgt = loss_target.reshape(T, D)

    (h, u, gt, q, k, v, zf, c, cq, qx, kx), got = _fwd_in(x2d, row(g_mix), w_int, w_ft, bf_pad, B, S,
                                                  comm=_ag_comm([shard_bf[n] for n in ag_mid[:2]]))
    slab.update(zip(ag_mid[:2], got))
    ckT = jnp.transpose(c.reshape(B, S, LANES)[:, :, :8], (0, 2, 1)).reshape(B, N_PAIR, 2, S)
    ckT = jnp.pad(ckT, ((0, 0), (0, 0), (0, 6), (0, 0)))
    (y, co), got = _conv_fwd(u, gt, cw, row(conv_b), row(ln_g), row(ln_b), B, S,
                             comm=_ag_comm([shard_bf[n] for n in ag_mid[2:]]))
    slab.update(zip(ag_mid[2:], got))
    (o, fox_bias), got = _fox_fwd(qx, kx, v, cq, B, S, comm=_ag_comm([shard_bf[n] for n in ag_ffn[:1]]))
    slab.update(zip(ag_ffn[:1], got))
    full = {n: slab[n] if by_col else slab[n].reshape(4 * r, c) for n, r, c, by_col in BIG if n in slab}
    mn, km, vm = _mem_kv(mem2d, row(g_mem), full["w_mkv"], B)
    (x1, hx, qm, om, x2, cat), got = _fwd_mid(x2d, co, o, km, vm, full["w_out"], full["w_mq"], full["w_mo"], row(g_x),
                                             B, S, comm=_ag_comm([shard_bf[n] for n in ag_ffn[1:]]))
    full["w_down"] = got[0].reshape(D_FF, D)
    hf, gu, act, dx3, loss_p, dg_final = _fwd_ffn(x2, tgt, full["w_gu"], full["w_down"], row(g_ffn), row(g_final), T)

    pos_sum = lambda gs, rcvs, ns: [_chip_sum(g, r, pos, "rs_chip_sum_" + n) for g, r, n in zip(gs, rcvs, ns)]
    fin_sum = lambda gs, rcvs, rcs, ns: [_final_sum(g, r, q3, pos, "rs_final_sum_" + n)
                                         for g, r, q3, n in zip(gs, rcvs, rcs, ns)]
    RH = {}
    dgu, dx2, dg_ffn = _bwd_ffn(dx3, gu, x2, full["w_gu"], full["w_down"], row(g_ffn), T)
    g_ffn_w = [_dw(hf, dgu, "dw_gu", FF_CHUNK, slabs=True), _dw(act, dx3, "dw_down", 512).reshape(4, D_FF // 4, D)]
    (dx1, dqm, dco, do, dd, dkm, dvm, dg_x), rcv_ffn = _bwd_mid(dx2, x1, qm, km, vm, o, full["w_mo"], full["w_mq"],
                                                                full["w_out"], row(g_x), B, S, comm=_sibling_comm(g_ffn_w))
    pb_ffn = pos_sum(g_ffn_w, rcv_ffn, ag_ffn)
    dkv, dg_mem = _mem_bwd(dkm, dvm, mem2d, full["w_mkv"], row(g_mem), B)
    g_mid_w = [_dw(mn, dkv, "dw_mkv", 512, slabs=True), _dw(cat, dx1, "dw_out", 512).reshape(4, 256, D),
               _dw(hx, dqm, "dw_mq", 512).reshape(4, 256, D), _dw(om, dx2, "dw_mo", 512).reshape(4, 256, D)]
    (dq, dk, dv, dc, dcq), got = _fox_bwd(q, k, v, do, fox_bias, dd, ckT, B, S,
                                          comm=_join(_ici_comm(pb_ffn), _sibling_comm(g_mid_w)))
    rc_ffn, rcv_mid = got[:len(pb_ffn)], got[len(pb_ffn):]
    RH.update(zip(ag_ffn, fin_sum(g_ffn_w, rcv_ffn, rc_ffn, ag_ffn)))
    pb_mid = pos_sum(g_mid_w, rcv_mid, ag_mid)
    dc8 = jnp.transpose(dc[:, :, :2, :].reshape(B, 8, S), (0, 2, 1)).reshape(T, 8)
    dc8 = dc8 + dcq.reshape(T, 8, HEAD_D)[:, :, 0]
    dzf, dbf = _fgate_bwd(jnp.pad(dc8, ((0, 0), (0, LANES - 8))), zf, B, S)
    (du, dgt, dcw, dvec), rc_mid = _conv_bwd(dco, y, u, gt, cw, row(ln_g), row(ln_b), B, S, comm=_ici_comm(pb_mid))
    RH.update(zip(ag_mid, fin_sum(g_mid_w, rcv_mid, rc_mid, ag_mid)))
    dz = jnp.concatenate([du, dgt, dq, dk, dv, dzf], axis=1)
    g_in_w = [_dw(dz, h, "dw_in", 512, tk=D_IN_PAD // 3, rows=D_IN).reshape(4, D_IN // 4, D)]
    rcv_in = _run_comm(_sibling_comm(g_in_w), "rs_sibling_in")
    (grad_x, dg_mix), rc_in = _bwd_in(dz, w_int, w_ft, x2d, dx1, row(g_mix), T,
                                      comm=_ici_comm(pos_sum(g_in_w, rcv_in, ["w_in"])))
    RH.update(zip(["w_in"], fin_sum(g_in_w, rcv_in, rc_in, ["w_in"])))
    shared = dict(zip(big_names, _sibling_share([RH[n] for n in big_names])))
    G, DL, NM, NV = {}, {}, {}, {}
    for n in big_names:
        G[n], DL[n], NM[n], NV[n] = _adamw(big2d(W, n), shared[n], big2d(M, n), big2d(V, n), "adamw_" + n,
                                           _half_shape(shared[n].shape))

    small_g = {"g_mix": dg_mix, "b_f": dbf[:, :8], "conv_w": dcw, "conv_b": dvec[0], "ln_g": dvec[1], "ln_b": dvec[2],
               "g_x": dg_x, "g_mem": dg_mem, "g_ffn": dg_ffn, "g_final": dg_final, "loss": loss_p[:, :1]}
    sg = _small_allreduce(_pack_small(small_g, CONV_HALO * 4), "allreduce_small")
    shapes = {n: W[n].shape for n in names if n not in big_names}
    shapes["conv_w"] = (CONV_HALO, CONV_CH)
    shapes["loss"] = (1,)
    sgrads = _unpack_small(sg, shapes, CONV_HALO * 4)
    loss = sgrads.pop("loss")[0]
    sgrads["conv_w"] = lax.dynamic_slice(sgrads["conv_w"], (0, chip * LANES), (CONV_K, LANES)).reshape(W["conv_w"].shape)
    spack = lambda d: _pack_small({n: d[n] for n in sgrads}, CONV_HALO)
    _, sd, snm, snv = _adamw(spack(W), spack(sgrads), spack(M), spack(V), "adamw_small", (8, LANES))
    sshapes = {n: W[n].shape for n in sgrads}
    SD, SNM, SNV = (_unpack_small(a, sshapes, CONV_HALO) for a in (sd, snm, snv))

    def collect(bigs, smalls):
        back = lambda n: (bigs[n].T if n == "w_in" else bigs[n]).reshape(W[n].shape)
        return [back(n) if n in big_names else smalls[n] for n in names]

    return (loss, grad_x.reshape(x.shape), *collect(G, sgrads), *collect(DL, SD), *collect(NM, SNM), *collect(NV, SNV))
```

```python
import functools
import math

import jax
import jax.numpy as jnp
from jax import lax
from jax.experimental import pallas as pl
from jax.experimental.pallas import tpu as pltpu

F32, BF16 = jnp.float32, jnp.bfloat16
MESH = pl.DeviceIdType.MESH

D = 1024
CONV_CH = 512
CONV_K = 31
CONV_HALO = 32
FOX_W = 512
HEAD_D = 64
N_PAIR = 4
MEM_LEN = 256
MEM_HEADS = 4
MEM_HD = 256
D_FF = 2816
FF_CHUNK = 1408
D_IN = 2568
D_IN_PAD = 2688
OFF_F = 2560
EPS = 1e-6
LANES = 128

ADAM_LR, ADAM_B1, ADAM_B2, ADAM_EPS, ADAM_WD, ADAM_STEP = 0.001, 0.9, 0.999, 1e-08, 0.01, 10

VMEM_LIMIT = 60 * 1024 * 1024

BIG = (("w_out", 256, 1024, False), ("w_mq", 256, 1024, False), ("w_mkv", 1024, 512, True),
       ("w_mo", 256, 1024, False), ("w_gu", 1024, 1408, True), ("w_down", 704, 1024, False),
       ("w_in", 642, 1024, False))

ANY = pl.BlockSpec(memory_space=pl.ANY)


def _sig(x):
    return 1.0 / (1.0 + jnp.exp(-x))


def _dot(a, b):
    return jnp.dot(a, b, preferred_element_type=F32)


def _dot_nt(a, b):
    return lax.dot_general(a, b, (((1,), (1,)), ((), ())), preferred_element_type=F32)


def _dot_tn(a, b):
    return lax.dot_general(a, b, (((0,), (0,)), ((), ())), preferred_element_type=F32)


def _split3(x):
    hi = x.astype(BF16)
    r = x - hi.astype(F32)
    mid = r.astype(BF16)
    return hi, mid, (r - mid.astype(F32)).astype(BF16)


def _dot_01(a, b):
    if a.dtype == jnp.bool_:
        return sum(_dot(a.astype(BF16), t) for t in _split3(b))
    return sum(_dot(t, b.astype(BF16)) for t in _split3(a))


def _resident(a):
    nd = a.ndim
    return pl.BlockSpec(a.shape, lambda *_: (0,) * nd, pipeline_mode=pl.Buffered(1))


def _acc_spec(shape):
    nd = len(shape)
    return pl.BlockSpec(shape, lambda *_: (0,) * nd)


def _params(n_grid):
    return pltpu.CompilerParams(dimension_semantics=("arbitrary",) * n_grid, vmem_limit_bytes=VMEM_LIMIT)


def _sds(shape, dtype):
    return jax.ShapeDtypeStruct(shape, dtype)


def _rms(x):
    r = lax.rsqrt(jnp.mean(x * x, axis=-1, keepdims=True) + EPS)
    return r, x * r


def _rms_bwd(dy, xh, r, g):
    dxh = dy * g
    dx = r * (dxh - xh * jnp.mean(dxh * xh, axis=-1, keepdims=True))
    return dx, dy * xh


def _head_expand(rows, cols):
    hd = lax.broadcasted_iota(jnp.int32, (rows, cols), 1) // HEAD_D
    hr = lax.broadcasted_iota(jnp.int32, (rows, cols), 0)
    return hd == hr


def _fwd_in(x2, g_mix, w_int, w_ft, bf_pad, B, S, comm=None):
    T = B * S
    TB = min(512, S)
    nb = S // TB

    def body(x_ref, g_ref, w_ref, wf_ref, bf_ref, h_ref, u_ref, gt_ref, q_ref, k_ref, v_ref, zf_ref, c_ref, cq_ref,
             qx_ref, kx_ref, carry):
        j = pl.program_id(1)

        @pl.when(j == 0)
        def _():
            carry[...] = jnp.zeros_like(carry)

        _, xh = _rms(x_ref[...])
        h = (xh * g_ref[...]).astype(BF16)
        h_ref[...] = h
        u_ref[...] = _dot_nt(h, w_ref[0:512, :])
        gt_ref[...] = _dot_nt(h, w_ref[512:1024, :])
        qb = _dot_nt(h, w_ref[1024:1536, :]).astype(BF16)
        kb = _dot_nt(h, w_ref[1536:2048, :]).astype(BF16)
        q_ref[...] = qb
        k_ref[...] = kb
        v_ref[...] = _dot_nt(h, w_ref[2048:2560, :]).astype(BF16)
        zf = _dot_nt(h, wf_ref[...]) + bf_ref[...]
        zf_ref[...] = zf
        lane = lax.broadcasted_iota(jnp.int32, zf.shape, 1)
        logf = jnp.where(lane < 8, jnp.minimum(zf, 0.0) - jnp.log(1.0 + jnp.exp(-jnp.abs(zf))), 0.0)
        row = lax.broadcasted_iota(jnp.int32, (TB, TB), 0)
        col = lax.broadcasted_iota(jnp.int32, (TB, TB), 1)
        c = _dot_01(row >= col, logf) + carry[0:1, :]
        carry[0:1, :] = c[TB - 1:TB, :]
        c_ref[...] = c
        cq = _dot_01(c, _head_expand(LANES, FOX_W))
        cq_ref[...] = cq
        hl = lax.broadcasted_iota(jnp.int32, (TB, LANES), 1)
        for hd in range(2 * N_PAIR):
            grp = slice((hd // 2) * LANES, (hd // 2 + 1) * LANES)
            swap = (lambda t: t) if hd % 2 == 0 else (lambda t: pltpu.roll(t, HEAD_D, 1))
            qf = swap(qb[:, grp].astype(F32) * (1.0 / math.sqrt(HEAD_D)))
            kf = swap(kb[:, grp].astype(F32))
            cv = cq[:, grp] if hd % 2 == 1 else pltpu.roll(cq[:, grp], HEAD_D, 1)
            hi = cv.astype(BF16).astype(F32)
            mid = (cv - hi).astype(BF16).astype(F32)
            lo = (cv - hi - mid).astype(BF16).astype(F32)
            pick = lambda a, b, c3, one_from, one_to: jnp.where(hl == a[0], a[1], jnp.where(hl == b[0], b[1], jnp.where(
                hl == c3[0], c3[1], jnp.where((hl >= one_from) & (hl < one_to), 1.0, 0.0))))
            qx = jnp.where(hl < HEAD_D, qf, pick((67, hi), (68, mid), (69, lo), 64, 67))
            kx = jnp.where(hl < HEAD_D, kf, pick((64, -hi), (65, -mid), (66, -lo), 67, 70))
            qx_ref[:, hd * LANES:(hd + 1) * LANES] = qx.astype(BF16)
            kx_ref[:, hd * LANES:(hd + 1) * LANES] = kx.astype(BF16)

    tok = lambda w: pl.BlockSpec((TB, w), lambda b, j: (b * nb + j, 0))
    outs = [(D, BF16), (512, F32), (512, F32), (512, BF16), (512, BF16), (512, BF16), (LANES, F32),
            (LANES, F32), (FOX_W, F32), (2 * FOX_W, BF16), (2 * FOX_W, BF16)]
    return _call(
        body, comm, name="fwd_in", grid=(B, nb),
        in_specs=[tok(D), _resident(g_mix), _resident(w_int), _resident(w_ft), _resident(bf_pad)],
        out_specs=[tok(w) for w, _ in outs],
        out_shape=[_sds((T, w), dt) for w, dt in outs],
        scratch_shapes=[pltpu.VMEM((8, LANES), F32)],
        args=(x2, g_mix, w_int, w_ft, bf_pad))


def _head_sum(n):
    hc = lax.broadcasted_iota(jnp.int32, (n, n), 1) // HEAD_D
    hr = lax.broadcasted_iota(jnp.int32, (n, n), 0) // HEAD_D
    return hc == hr


def _layernorm_silu(y, lg, lb):
    mu = jnp.mean(y, axis=-1, keepdims=True)
    yc = y - mu
    rs = lax.rsqrt(jnp.mean(yc * yc, axis=-1, keepdims=True) + EPS)
    n = yc * rs
    l = n * lg + lb
    return rs, n, l


SUB = 8


def _shifted_copies(cat, sh, rows):
    for r in range(1, SUB):
        sh[r, 0:rows, :] = cat[r:r + rows, :]


def _tap(cat, sh, off, rows, cols=slice(None)):
    r = off % SUB
    return cat[off:off + rows, cols] if r == 0 else sh[r, off - r:off - r + rows, cols]


CONV_ROWS = 128


def _conv_pieces(CB):
    rows = min(CONV_ROWS, CB)
    return [(r0, rows, slice(c0, c0 + LANES)) for c0 in range(0, CONV_CH, LANES) for r0 in range(0, CB, rows)]


def _conv_fwd(u, gt, cw, cb, lng, lnb, B, S, comm=None):
    T = B * S
    CB = min(256, S)
    nb = S // CB

    def body(u_ref, gt_ref, w_ref, cb_ref, lg_ref, lb_ref, y_ref, co_ref, acat, ash):
        j = pl.program_id(1)

        @pl.when(j == 0)
        def _():
            acat[0:CONV_HALO, :] = jnp.zeros((CONV_HALO, CONV_CH), F32)

        acat[CONV_HALO:CONV_HALO + CB, :] = u_ref[...] * _sig(gt_ref[...])
        _shifted_copies(acat, ash, CB + CONV_HALO - SUB)
        for r0, rows, cs in _conv_pieces(CB):
            acc = jnp.zeros((rows, LANES), F32) + cb_ref[:, cs]
            for k in range(CONV_K):
                acc = acc + w_ref[k:k + 1, cs] * _tap(acat, ash, r0 + CONV_HALO - (CONV_K - 1) + k, rows, cs)
            y_ref[r0:r0 + rows, cs] = acc
        acat[0:CONV_HALO, :] = acat[CB:CB + CONV_HALO, :]
        _, _, l = _layernorm_silu(y_ref[...], lg_ref[...], lb_ref[...])
        co_ref[...] = (l * _sig(l)).astype(BF16)

    tok = lambda w: pl.BlockSpec((CB, w), lambda b, j: (b * nb + j, 0))
    return _call(
        body, comm, name="conv_fwd", grid=(B, nb),
        in_specs=[tok(CONV_CH), tok(CONV_CH), _resident(cw), _resident(cb), _resident(lng), _resident(lnb)],
        out_specs=[tok(CONV_CH), tok(CONV_CH)],
        out_shape=[_sds((T, CONV_CH), F32), _sds((T, CONV_CH), BF16)],
        scratch_shapes=[pltpu.VMEM((CONV_HALO + CB, CONV_CH), F32),
                        pltpu.VMEM((SUB, CB + CONV_HALO - SUB, CONV_CH), F32)],
        args=(u, gt, cw, cb, lng, lnb))


def _fox_fwd(qx, kx, v, cq, B, S, comm=None):
    T = B * S
    TQ = min(256, S)
    nq = S // TQ
    one_lane = (HEAD_D, 0)

    def body(qa_ref, qb_ref, ka_ref, kb_ref, v_ref, cq_ref, o_ref, lse_ref, s_scr, s_odd, m_scr, acc_scr):
        i = pl.program_id(2)
        lane = lax.broadcasted_iota(jnp.int32, (TQ, LANES), 1)
        lo = lane < HEAD_D
        qh = (qa_ref[...], qb_ref[...])
        kh = (ka_ref, kb_ref)
        m_scr[...] = jnp.full(m_scr.shape, -1e30, F32)
        acc_scr[...] = jnp.zeros_like(acc_scr)
        row = lax.broadcasted_iota(jnp.int32, (TQ, TQ), 0)
        col = lax.broadcasted_iota(jnp.int32, (TQ, TQ), 1)
        wide = lambda x: jnp.concatenate([x, x], axis=1) if TQ == 2 * LANES else jnp.tile(x, (1, TQ // LANES))

        def scores(j, s_buf):
            start = pl.multiple_of(j * TQ, TQ)
            for h in range(2):
                s_buf[h] = _dot_nt(qh[h], kh[h][pl.ds(start, TQ), :])

        def softmax_step(j, s_buf, diagonal):
            start = pl.multiple_of(j * TQ, TQ)
            vj = v_ref[pl.ds(start, TQ), :]
            for h in range(2):
                def logits():
                    return jnp.where(col <= row, s_buf[h], -1e30) if diagonal else s_buf[h]

                m_old = m_scr[h]
                m_new = jnp.maximum(m_old, jnp.max(logits(), axis=-1, keepdims=True))
                alpha = jnp.exp(m_old - m_new)
                m_scr[h] = m_new
                p = jnp.exp(logits() - wide(m_new)).astype(BF16)
                vx = jnp.where(lane == one_lane[h], jnp.ones_like(vj), jnp.where(lo if h == 0 else ~lo, vj, jnp.zeros_like(vj)))
                acc_scr[h] = alpha * acc_scr[h] + _dot(p, vx)

        def two_blocks(jj, carry):
            j = 2 * jj
            scores(j + 1, s_odd)
            softmax_step(j, s_scr, False)
            scores(j + 2, s_scr)
            softmax_step(j + 1, s_odd, False)
            return carry

        scores(0, s_scr)
        lax.fori_loop(0, i // 2, two_blocks, 0)

        @pl.when(i % 2 == 0)
        def _():
            softmax_step(i, s_scr, True)

        @pl.when(i % 2 == 1)
        def _():
            scores(i, s_odd)
            softmax_step(i - 1, s_scr, False)
            softmax_step(i, s_odd, True)

        acc_a, acc_b = acc_scr[0], acc_scr[1]
        l_a = acc_a[:, one_lane[0]:one_lane[0] + 1]
        l_b = acc_b[:, one_lane[1]:one_lane[1] + 1]
        o_ref[...] = jnp.where(lo, acc_a / l_a, acc_b / l_b)
        lse_ref[...] = cq_ref[...] - jnp.where(lo, m_scr[0] + jnp.log(l_a), m_scr[1] + jnp.log(l_b))

    qspec = pl.BlockSpec((TQ, LANES), lambda b, p, i: (b * nq + i, p))
    kspec = pl.BlockSpec((S, LANES), lambda b, p, i: (b, p))
    qhead = lambda h: pl.BlockSpec((TQ, LANES), lambda b, p, i: (b * nq + i, 2 * p + h))
    khead = lambda h: pl.BlockSpec((S, LANES), lambda b, p, i: (b, 2 * p + h))
    return _call(
        body, comm, name="fox_fwd", grid=(B, N_PAIR, nq),
        in_specs=[qhead(0), qhead(1), khead(0), khead(1), kspec, qspec],
        out_specs=[qspec, qspec],
        out_shape=[_sds((T, FOX_W), F32), _sds((T, FOX_W), F32)],
        scratch_shapes=[pltpu.VMEM((2, TQ, TQ), F32), pltpu.VMEM((2, TQ, TQ), F32),
                        pltpu.VMEM((2, TQ, LANES), F32), pltpu.VMEM((2, TQ, LANES), F32)],
        args=(qx, qx, kx, kx, v, cq))


def _mem_kv(mem2, g_mem, w_mkv, B):
    def body(m_ref, g_ref, w_ref, mn_ref, km_ref, vm_ref):
        _, xh = _rms(m_ref[...])
        mn = (xh * g_ref[...]).astype(BF16)
        mn_ref[...] = mn
        for s in range(2):
            km_ref[:, 512 * s:512 * (s + 1)] = _dot(mn, w_ref[s]).astype(BF16)
            vm_ref[:, 512 * s:512 * (s + 1)] = _dot(mn, w_ref[2 + s]).astype(BF16)

    blk = pl.BlockSpec((MEM_LEN, D), lambda b: (b, 0))
    return pl.pallas_call(
        body, name="mem_kv", grid=(B,),
        in_specs=[blk, _resident(g_mem), _resident(w_mkv)],
        out_specs=[blk, blk, blk],
        out_shape=[_sds((B * MEM_LEN, D), BF16)] * 3,
        compiler_params=_params(1),
    )(mem2, g_mem, w_mkv)


def _mem_probs(qm, km):
    ps = []
    for h in range(MEM_HEADS):
        hs = slice(h * MEM_HD, (h + 1) * MEM_HD)
        lg = _dot_nt(qm[:, hs], km[:, hs]) * (1.0 / math.sqrt(MEM_HD))
        e = jnp.exp(lg - jnp.max(lg, axis=-1, keepdims=True))
        ps.append(e / jnp.sum(e, axis=-1, keepdims=True))
    return ps


def _fwd_mid(x2, co, o, km, vm, w_out, w_mq, w_mo, g_x, B, S, comm=None):
    T = B * S
    TB = min(512, S)
    nb = S // TB

    def body(x_ref, co_ref, o_ref, km_ref, vm_ref, wo_ref, wq_ref, wm_ref, g_ref,
             x1_ref, hx_ref, qm_ref, om_ref, x2_ref, cat_ref):
        cat_ref[:, 0:CONV_CH] = co_ref[...]
        cat_ref[:, CONV_CH:D] = o_ref[...].astype(BF16)
        x1 = x_ref[...] + _dot(cat_ref[...], wo_ref[...])
        x1_ref[...] = x1
        _, xh = _rms(x1)
        hx = (xh * g_ref[...]).astype(BF16)
        hx_ref[...] = hx
        qm = _dot(hx, wq_ref[...]).astype(BF16)
        qm_ref[...] = qm
        ps = _mem_probs(qm, km_ref[...])
        vmv = vm_ref[...]
        for h in range(MEM_HEADS):
            hs = slice(h * MEM_HD, (h + 1) * MEM_HD)
            om_ref[:, hs] = _dot(ps[h].astype(BF16), vmv[:, hs]).astype(BF16)
        x2_ref[...] = x1 + _dot(om_ref[...], wm_ref[...])

    tok = lambda w: pl.BlockSpec((TB, w), lambda b, j: (b * nb + j, 0))
    memb = pl.BlockSpec((MEM_LEN, D), lambda b, j: (b, 0))
    outs = [(D, F32), (D, BF16), (D, BF16), (D, BF16), (D, F32), (D, BF16)]
    return _call(
        body, comm, name="fwd_mid", grid=(B, nb),
        in_specs=[tok(D), tok(CONV_CH), tok(FOX_W), memb, memb, _resident(w_out), _resident(w_mq), _resident(w_mo),
                  _resident(g_x)],
        out_specs=[tok(w) for w, _ in outs],
        out_shape=[_sds((T, w), dt) for w, dt in outs],
        scratch_shapes=[],
        args=(x2, co, o, km, vm, w_out, w_mq, w_mo, g_x))


def _fwd_ffn(x2, tgt, w_gu, w_down, g_ffn, g_final, T):
    TB = min(256, T)
    nb = T // TB

    def body(x_ref, t_ref, wgu_ref, wd_ref, gf_ref, gl_ref, hf_ref, gu_ref, act_ref, dx3_ref, loss_ref, dgl_ref):
        i = pl.program_id(0)

        @pl.when(i == 0)
        def _():
            loss_ref[...] = jnp.zeros_like(loss_ref)
            dgl_ref[...] = jnp.zeros_like(dgl_ref)

        x2v = x_ref[...]
        _, xh = _rms(x2v)
        hf = (xh * gf_ref[...]).astype(BF16)
        hf_ref[...] = hf
        x3 = x2v
        for ch in range(D_FF // FF_CHUNK):
            c0 = ch * FF_CHUNK
            g = _dot(hf, wgu_ref[ch])
            u = _dot(hf, wgu_ref[2 + ch])
            gu_ref[:, c0:c0 + FF_CHUNK] = g
            gu_ref[:, D_FF + c0:D_FF + c0 + FF_CHUNK] = u
            act = (g * _sig(g) * u).astype(BF16)
            act_ref[:, c0:c0 + FF_CHUNK] = act
            x3 = x3 + _dot(act, wd_ref[c0:c0 + FF_CHUNK, :])
        r3, xh3 = _rms(x3)
        gl = gl_ref[...]
        e = xh3 * gl - t_ref[...]
        loss_ref[...] += jnp.sum(e * e) * (0.5 / D)
        dy = e * (1.0 / D)
        dx3, dgl = _rms_bwd(dy, xh3, r3, gl)
        dx3_ref[...] = dx3
        dgl_ref[...] += jnp.sum(dgl, axis=0, keepdims=True)

    tok = lambda w: pl.BlockSpec((TB, w), lambda i: (i, 0))
    return pl.pallas_call(
        body, name="fwd_ffn", grid=(nb,),
        in_specs=[tok(D), tok(D), _resident(w_gu), _resident(w_down), _resident(g_ffn), _resident(g_final)],
        out_specs=[tok(D), tok(2 * D_FF), tok(D_FF), tok(D), _acc_spec((1, LANES)), _acc_spec((1, D))],
        out_shape=[_sds((T, D), BF16), _sds((T, 2 * D_FF), F32), _sds((T, D_FF), BF16), _sds((T, D), F32),
                   _sds((1, LANES), F32), _sds((1, D), F32)],
        compiler_params=_params(1),
    )(x2, tgt, w_gu, w_down, g_ffn, g_final)


def _bwd_ffn(dx3, gu, x2, w_gu, w_down, g_ffn, T):
    TB = min(256, T)
    nb = T // TB

    def body(d_ref, gu_ref, x_ref, wgu_ref, wd_ref, gf_ref, dgu_ref, dx2_ref, dgf_ref):
        i = pl.program_id(0)

        @pl.when(i == 0)
        def _():
            dgf_ref[...] = jnp.zeros_like(dgf_ref)

        dx3v = d_ref[...]
        db = dx3v.astype(BF16)
        dhf = jnp.zeros((TB, D), F32)
        for ch in range(D_FF // FF_CHUNK):
            c0 = ch * FF_CHUNK
            dact = _dot_nt(db, wd_ref[c0:c0 + FF_CHUNK, :])
            g = gu_ref[:, c0:c0 + FF_CHUNK]
            u = gu_ref[:, D_FF + c0:D_FF + c0 + FF_CHUNK]
            sg = _sig(g)
            dg = (dact * u * sg * (1.0 + g * (1.0 - sg))).astype(BF16)
            du = (dact * g * sg).astype(BF16)
            dgu_ref[:, c0:c0 + FF_CHUNK] = dg
            dgu_ref[:, D_FF + c0:D_FF + c0 + FF_CHUNK] = du
            dhf = dhf + _dot_nt(dg, wgu_ref[ch]) + _dot_nt(du, wgu_ref[2 + ch])
        r2, xh2 = _rms(x_ref[...])
        dx, dg_tok = _rms_bwd(dhf, xh2, r2, gf_ref[...])
        dx2_ref[...] = dx3v + dx
        dgf_ref[...] += jnp.sum(dg_tok, axis=0, keepdims=True)

    tok = lambda w: pl.BlockSpec((TB, w), lambda i: (i, 0))
    return pl.pallas_call(
        body, name="bwd_ffn", grid=(nb,),
        in_specs=[tok(D), tok(2 * D_FF), tok(D), _resident(w_gu), _resident(w_down), _resident(g_ffn)],
        out_specs=[tok(2 * D_FF), tok(D), _acc_spec((1, D))],
        out_shape=[_sds((T, 2 * D_FF), BF16), _sds((T, D), F32), _sds((1, D), F32)],
        compiler_params=_params(1),
    )(dx3, gu, x2, w_gu, w_down, g_ffn)


def _bwd_mid(dx2, x1, qm, km, vm, o, w_mo, w_mq, w_out, g_x, B, S, comm=None):
    T = B * S
    TB = min(512, S)
    nb = S // TB
    inv = 1.0 / math.sqrt(MEM_HD)

    def body(d_ref, x1_ref, qm_ref, km_ref, vm_ref, o_ref, wm_ref, wq_ref, wo_ref, g_ref,
             dx1_ref, dqm_ref, dco_ref, do_ref, dd_ref, dkm_ref, dvm_ref, dgx_ref):
        b = pl.program_id(0)
        j = pl.program_id(1)

        @pl.when((b == 0) & (j == 0))
        def _():
            dgx_ref[...] = jnp.zeros_like(dgx_ref)

        @pl.when(j == 0)
        def _():
            dkm_ref[...] = jnp.zeros_like(dkm_ref)
            dvm_ref[...] = jnp.zeros_like(dvm_ref)

        dx2v = d_ref[...]
        dom = _dot_nt(dx2v.astype(BF16), wm_ref[...]).astype(BF16)
        qmv = qm_ref[...]
        kmv = km_ref[...]
        vmv = vm_ref[...]
        ps = _mem_probs(qmv, kmv)
        for h in range(MEM_HEADS):
            hs = slice(h * MEM_HD, (h + 1) * MEM_HD)
            p = ps[h]
            dp = _dot_nt(dom[:, hs], vmv[:, hs])
            ds = (p * (dp - jnp.sum(p * dp, axis=-1, keepdims=True))).astype(BF16)
            dqm_ref[:, hs] = (_dot(ds, kmv[:, hs]) * inv).astype(BF16)
            dkm_ref[:, hs] += _dot_tn(ds, qmv[:, hs]) * inv
            dvm_ref[:, hs] += _dot_tn(p.astype(BF16), dom[:, hs])
        dhx = _dot_nt(dqm_ref[...], wq_ref[...])
        r1, xh1 = _rms(x1_ref[...])
        dx, dg_tok = _rms_bwd(dhx, xh1, r1, g_ref[...])
        dx1 = dx2v + dx
        dx1_ref[...] = dx1
        dgx_ref[...] += jnp.sum(dg_tok, axis=0, keepdims=True)
        d1b = dx1.astype(BF16)
        dco_ref[...] = _dot_nt(d1b, wo_ref[0:CONV_CH, :])
        do = _dot_nt(d1b, wo_ref[CONV_CH:D, :])
        dob = do.astype(BF16)
        do_ref[...] = dob
        dd_ref[...] = _dot_01(dob.astype(F32) * o_ref[...], _head_sum(FOX_W))

    tok = lambda w: pl.BlockSpec((TB, w), lambda b, j: (b * nb + j, 0))
    memb = pl.BlockSpec((MEM_LEN, D), lambda b, j: (b, 0))
    outs = [(D, F32), (D, BF16), (CONV_CH, F32), (FOX_W, BF16), (FOX_W, F32)]
    return _call(
        body, comm, name="bwd_mid", grid=(B, nb),
        in_specs=[tok(D), tok(D), tok(D), memb, memb, tok(FOX_W), _resident(w_mo), _resident(w_mq), _resident(w_out),
                  _resident(g_x)],
        out_specs=[tok(w) for w, _ in outs] + [memb, memb, _acc_spec((1, D))],
        out_shape=[_sds((T, w), dt) for w, dt in outs] + [_sds((B * MEM_LEN, D), F32)] * 2 + [_sds((1, D), F32)],
        scratch_shapes=[],
        args=(dx2, x1, qm, km, vm, o, w_mo, w_mq, w_out, g_x))


def _mem_bwd(dkm, dvm, mem2, w_mkv, g_mem, B):
    def body(dk_ref, dv_ref, m_ref, w_ref, g_ref, dkv_ref, dg_ref):
        b = pl.program_id(0)

        @pl.when(b == 0)
        def _():
            dg_ref[...] = jnp.zeros_like(dg_ref)

        dk = dk_ref[...].astype(BF16)
        dv = dv_ref[...].astype(BF16)
        dkv_ref[:, 0:D] = dk
        dkv_ref[:, D:2 * D] = dv
        dmn = jnp.zeros((MEM_LEN, D), F32)
        for s in range(2):
            dmn = dmn + _dot_nt(dk[:, 512 * s:512 * (s + 1)], w_ref[s]) + _dot_nt(dv[:, 512 * s:512 * (s + 1)], w_ref[2 + s])
        _, xh = _rms(m_ref[...])
        dg_ref[...] += jnp.sum(dmn * xh, axis=0, keepdims=True)

    blk = pl.BlockSpec((MEM_LEN, D), lambda b: (b, 0))
    return pl.pallas_call(
        body, name="mem_bwd", grid=(B,),
        in_specs=[blk, blk, blk, _resident(w_mkv), _resident(g_mem)],
        out_specs=[pl.BlockSpec((MEM_LEN, 2 * D), lambda b: (b, 0)), _acc_spec((1, D))],
        out_shape=[_sds((B * MEM_LEN, 2 * D), BF16), _sds((1, D), F32)],
        compiler_params=_params(1),
    )(dkm, dvm, mem2, w_mkv, g_mem)


def _fox_bwd(q, k, v, do, bias, dd, ckT, B, S, comm=None):
    T = B * S
    TK = min(256, S)
    nk = S // TK
    scale = 1.0 / math.sqrt(HEAD_D)

    def body(q_ref, k_ref, v_ref, do_ref, bias_ref, dd_ref, ck_ref, dq_ref, dk_ref, dv_ref, dc_ref, dcq_ref,
             dq_acc, rs_acc, s_scr, dp_scr, s_odd, dp_odd, dk_acc, dv_acc, dc_acc):
        j = pl.program_id(2)

        @pl.when(j == 0)
        def _():
            dq_acc[...] = jnp.zeros_like(dq_acc)
            rs_acc[...] = jnp.zeros_like(rs_acc)

        dk_acc[...] = jnp.zeros_like(dk_acc)
        dv_acc[...] = jnp.zeros_like(dv_acc)
        dc_acc[...] = jnp.zeros_like(dc_acc)
        lane = lax.broadcasted_iota(jnp.int32, (TK, LANES), 1)
        lo = lane < HEAD_D
        ks = k_ref[...] * jnp.asarray(scale, BF16)
        v2 = v_ref[...]
        zero = jnp.zeros_like(ks)
        kh = (jnp.where(lo, ks, zero), jnp.where(lo, zero, ks))
        vh = (jnp.where(lo, v2, zero), jnp.where(lo, zero, v2))
        kstart = pl.multiple_of(j * TK, TK)
        ckh = tuple(ck_ref[0, 0, h:h + 1, pl.ds(kstart, TK)] for h in range(2))
        row = lax.broadcasted_iota(jnp.int32, (TK, TK), 0)
        col = lax.broadcasted_iota(jnp.int32, (TK, TK), 1)
        wide = lambda x: jnp.concatenate([x, x], axis=1) if TK == 2 * LANES else jnp.tile(x, (1, TK // LANES))

        def scores(i, s_buf, dp_buf):
            start = pl.multiple_of(i * TK, TK)
            qi = q_ref[pl.ds(start, TK), :]
            doi = do_ref[pl.ds(start, TK), :]
            for h in range(2):
                s_buf[h] = _dot_nt(qi, kh[h])
                dp_buf[h] = _dot_nt(doi, vh[h])

        def grads(i, s_buf, dp_buf, diagonal):
            start = pl.multiple_of(i * TK, TK)
            qi = q_ref[pl.ds(start, TK), :]
            doi = do_ref[pl.ds(start, TK), :]
            bias2 = bias_ref[pl.ds(start, TK), :]
            dd2 = dd_ref[pl.ds(start, TK), :]
            for h in range(2):
                hc = slice(h * HEAD_D, h * HEAD_D + 1)
                bias = jnp.broadcast_to(bias2[:, hc], (TK, LANES))
                ddh = jnp.broadcast_to(dd2[:, hc], (TK, LANES))
                p = jnp.exp((s_buf[h] - ckh[h]) + wide(bias))
                if diagonal:
                    p = jnp.where(col <= row, p, 0.0)
                ds = p * (dp_buf[h] - wide(ddh))
                dc_acc[h, 0:1, :] += jnp.sum(ds, axis=0, keepdims=True)
                rs_acc[h, pl.ds(start, TK), :] += jnp.sum(ds, axis=1, keepdims=True)
                pb = p.astype(BF16)
                dsb = ds.astype(BF16)
                dv_acc[h] += _dot_tn(pb, doi)
                dk_acc[h] += _dot_tn(dsb, qi)
                dq_acc[pl.ds(start, TK), :] += _dot(dsb, kh[h])

        n_off = nk - 1 - j
        block = lambda t: jnp.where(t < n_off, j + 1 + t, j)

        def two_blocks(tt, carry):
            t = 2 * tt
            scores(block(t + 1), s_odd, dp_odd)
            grads(block(t), s_scr, dp_scr, False)
            scores(block(t + 2), s_scr, dp_scr)
            grads(block(t + 1), s_odd, dp_odd, False)
            return carry

        scores(block(0), s_scr, dp_scr)
        lax.fori_loop(0, n_off // 2, two_blocks, 0)

        @pl.when(n_off % 2 == 0)
        def _():
            grads(j, s_scr, dp_scr, True)

        @pl.when(n_off % 2 == 1)
        def _():
            scores(j, s_odd, dp_odd)
            grads(nk - 1, s_scr, dp_scr, False)
            grads(j, s_odd, dp_odd, True)

        dk_ref[...] = (jnp.where(lo, dk_acc[0], dk_acc[1]) * scale).astype(BF16)
        dv_ref[...] = jnp.where(lo, dv_acc[0], dv_acc[1]).astype(BF16)
        sub = lax.broadcasted_iota(jnp.int32, (8, TK), 0)
        dca = dc_acc[0, 0:1, :]
        dcb = dc_acc[1, 0:1, :]
        dc_ref[0, 0] = jnp.where(sub == 0, -dca, jnp.where(sub == 1, -dcb, 0.0))

        @pl.when(j == nk - 1)
        def _():
            dq_ref[...] = dq_acc[...].astype(BF16)
            lo_s = lax.broadcasted_iota(jnp.int32, (S, LANES), 1) < HEAD_D
            dcq_ref[...] = jnp.where(lo_s, rs_acc[0], rs_acc[1])

    full = pl.BlockSpec((S, LANES), lambda b, p, j: (b, p))
    blk = pl.BlockSpec((TK, LANES), lambda b, p, j: (b * nk + j, p))
    return _call(
        body, comm, name="fox_bwd", grid=(B, N_PAIR, nk),
        in_specs=[full, blk, blk, full, full, full, pl.BlockSpec((1, 1, 8, S), lambda b, p, j: (b, p, 0, 0))],
        out_specs=[full, blk, blk, pl.BlockSpec((1, 1, 8, TK), lambda b, p, j: (b, p, 0, j)), full],
        out_shape=[_sds((T, FOX_W), BF16), _sds((T, FOX_W), BF16), _sds((T, FOX_W), BF16),
                   _sds((B, N_PAIR, 8, S), F32), _sds((T, FOX_W), F32)],
        scratch_shapes=[pltpu.VMEM((S, LANES), F32), pltpu.VMEM((2, S, 1), F32),
                        pltpu.VMEM((2, TK, TK), F32), pltpu.VMEM((2, TK, TK), F32),
                        pltpu.VMEM((2, TK, TK), F32), pltpu.VMEM((2, TK, TK), F32),
                        pltpu.VMEM((2, TK, LANES), F32), pltpu.VMEM((2, TK, LANES), F32), pltpu.VMEM((2, 8, TK), F32)],
        args=(q, k, v, do, bias, dd, ckT))


def _fgate_bwd(dc8, zf, B, S):
    T = B * S
    TB = min(512, S)
    nb = S // TB

    def body(dc_ref, zf_ref, dzf_ref, dbf_ref, carry):
        b = pl.program_id(0)
        j = pl.program_id(1)

        @pl.when((b == 0) & (j == 0))
        def _():
            dbf_ref[...] = jnp.zeros_like(dbf_ref)

        @pl.when(j == 0)
        def _():
            carry[...] = jnp.zeros_like(carry)

        dc = dc_ref[...]
        row = lax.broadcasted_iota(jnp.int32, (TB, TB), 0)
        col = lax.broadcasted_iota(jnp.int32, (TB, TB), 1)
        dlogf = _dot_01(col >= row, dc) + carry[0:1, :]
        carry[0:1, :] = dlogf[0:1, :]
        lane = lax.broadcasted_iota(jnp.int32, dc.shape, 1)
        dzf = jnp.where(lane < 8, dlogf * _sig(-zf_ref[...]), 0.0)
        dzf_ref[...] = dzf.astype(BF16)
        dbf_ref[...] += jnp.sum(dzf, axis=0, keepdims=True)

    tok = pl.BlockSpec((TB, LANES), lambda b, j: (b * nb + (nb - 1 - j), 0))
    return pl.pallas_call(
        body, name="fgate_bwd", grid=(B, nb),
        in_specs=[tok, tok],
        out_specs=[tok, _acc_spec((1, LANES))],
        out_shape=[_sds((T, LANES), BF16), _sds((1, LANES), F32)],
        scratch_shapes=[pltpu.VMEM((8, LANES), F32)],
        compiler_params=_params(2),
    )(dc8, zf)


def _conv_bwd(dco, y, u, gt, cw, lng, lnb, B, S, comm=None):
    T = B * S
    CB = min(256, S)
    nb = S // CB
    hb = CB // CONV_HALO

    def body(dco_ref, y_ref, u_ref, gt_ref, up_ref, gp_ref, w_ref, lg_ref, lb_ref,
             du_ref, dgt_ref, dw_ref, vec_ref, acat, dycat, ash, dysh):
        b = pl.program_id(0)
        j = pl.program_id(1)
        jr = nb - 1 - j

        @pl.when((b == 0) & (j == 0))
        def _():
            dw_ref[...] = jnp.zeros_like(dw_ref)
            vec_ref[...] = jnp.zeros_like(vec_ref)

        @pl.when(j == 0)
        def _():
            dycat[CB:CB + CONV_HALO, :] = jnp.zeros((CONV_HALO, CONV_CH), F32)

        lg = lg_ref[...]
        rs, n, l = _layernorm_silu(y_ref[...], lg, lb_ref[...])
        sg = _sig(l)
        dl = dco_ref[...] * (sg * (1.0 + l * (1.0 - sg)))
        dn = dl * lg
        dy = rs * (dn - jnp.mean(dn, axis=-1, keepdims=True) - n * jnp.mean(dn * n, axis=-1, keepdims=True))
        vec_ref[0:1, :] += jnp.sum(dy, axis=0, keepdims=True)
        vec_ref[1:2, :] += jnp.sum(dl * n, axis=0, keepdims=True)
        vec_ref[2:3, :] += jnp.sum(dl, axis=0, keepdims=True)
        dycat[0:CB, :] = dy
        acat[0:CONV_HALO, :] = jnp.where(jr > 0, up_ref[...] * _sig(gp_ref[...]), 0.0)
        acat[CONV_HALO:CONV_HALO + CB, :] = u_ref[...] * _sig(gt_ref[...])
        _shifted_copies(acat, ash, CB + CONV_HALO - SUB)
        _shifted_copies(dycat, dysh, CB + CONV_HALO - SUB)
        for r0, rows, cs in _conv_pieces(CB):
            dyp = dycat[r0:r0 + rows, cs]
            da = jnp.zeros((rows, LANES), F32)
            for k in range(CONV_K):
                da = da + w_ref[k:k + 1, cs] * _tap(dycat, dysh, r0 + CONV_K - 1 - k, rows, cs)
                dw_ref[k:k + 1, cs] += jnp.sum(dyp * _tap(acat, ash, r0 + CONV_HALO - (CONV_K - 1) + k, rows, cs),
                                               axis=0, keepdims=True)
            uv = u_ref[r0:r0 + rows, cs]
            sgt = _sig(gt_ref[r0:r0 + rows, cs])
            du_ref[r0:r0 + rows, cs] = (da * sgt).astype(BF16)
            dgt_ref[r0:r0 + rows, cs] = (da * uv * sgt * (1.0 - sgt)).astype(BF16)
        dycat[CB:CB + CONV_HALO, :] = dycat[0:CONV_HALO, :]

    tok = lambda w: pl.BlockSpec((CB, w), lambda b, j: (b * nb + (nb - 1 - j), 0))
    prev = pl.BlockSpec((CONV_HALO, CONV_CH), lambda b, j: (jnp.maximum((b * nb + (nb - 1 - j)) * hb - 1, 0), 0))
    return _call(
        body, comm, name="conv_bwd", grid=(B, nb),
        in_specs=[tok(CONV_CH), tok(CONV_CH), tok(CONV_CH), tok(CONV_CH), prev, prev, _resident(cw), _resident(lng),
                  _resident(lnb)],
        out_specs=[tok(CONV_CH), tok(CONV_CH), _acc_spec((CONV_HALO, CONV_CH)), _acc_spec((8, CONV_CH))],
        out_shape=[_sds((T, CONV_CH), BF16), _sds((T, CONV_CH), BF16), _sds((CONV_HALO, CONV_CH), F32),
                   _sds((8, CONV_CH), F32)],
        scratch_shapes=[pltpu.VMEM((CONV_HALO + CB, CONV_CH), F32), pltpu.VMEM((CB + CONV_HALO, CONV_CH), F32),
                        pltpu.VMEM((SUB, CB + CONV_HALO - SUB, CONV_CH), F32),
                        pltpu.VMEM((SUB, CB + CONV_HALO - SUB, CONV_CH), F32)],
        args=(dco, y, u, gt, u, gt, cw, lng, lnb))


def _bwd_in(dz, w_int, w_ft, x2, dx1, g_mix, T, comm=None):
    TB = min(512, T)
    nb = T // TB

    def body(dz_ref, w_ref, wf_ref, x_ref, d1_ref, g_ref, gx_ref, dg_ref):
        i = pl.program_id(0)

        @pl.when(i == 0)
        def _():
            dg_ref[...] = jnp.zeros_like(dg_ref)

        dh = _dot(dz_ref[:, 0:OFF_F], w_ref[0:OFF_F, :]) + _dot(dz_ref[:, OFF_F:D_IN_PAD], wf_ref[...])
        r0, xh0 = _rms(x_ref[...])
        dx, dg_tok = _rms_bwd(dh, xh0, r0, g_ref[...])
        gx_ref[...] = d1_ref[...] + dx
        dg_ref[...] += jnp.sum(dg_tok, axis=0, keepdims=True)

    tok = lambda w: pl.BlockSpec((TB, w), lambda i: (i, 0))
    return _call(
        body, comm, name="bwd_in", grid=(nb,),
        in_specs=[tok(D_IN_PAD), _resident(w_int), _resident(w_ft), tok(D), tok(D), _resident(g_mix)],
        out_specs=[tok(D), _acc_spec((1, D))],
        out_shape=[_sds((T, D), F32), _sds((1, D), F32)],
        scratch_shapes=[],
        args=(dz, w_int, w_ft, x2, dx1, g_mix))


def _dw(a, b, name, tn, slabs=False, tk=None, rows=None):
    T, K = a.shape
    N = b.shape[1]
    tk = tk or (K if K <= 1024 else K // 2)
    tt = min(1024, T)
    nt = T // tt

    def body(a_ref, b_ref, o_ref, acc):
        t = pl.program_id(2)

        @pl.when(t == 0)
        def _():
            acc[...] = jnp.zeros_like(acc)

        acc[...] += _dot_tn(a_ref[...].astype(BF16), b_ref[...].astype(BF16))

        @pl.when(t == nt - 1)
        def _():
            o_ref[...] = acc[...]

    return pl.pallas_call(
        body, name=name, grid=(K // tk, N // tn, nt),
        in_specs=[pl.BlockSpec((tt, tk), lambda i, j, t: (t, i)), pl.BlockSpec((tt, tn), lambda i, j, t: (t, j))],
        out_specs=(pl.BlockSpec((None, tk, tn), lambda i, j, t: (j, i, 0)) if slabs
                   else pl.BlockSpec((tk, tn), lambda i, j, t: (i, j))),
        out_shape=_sds((N // tn, K, tn) if slabs else (rows or K, N), F32),
        scratch_shapes=[pltpu.VMEM((tk, tn), F32)],
        compiler_params=_params(3),
    )(a, b)


def _pos():
    return lax.axis_index("x"), lax.axis_index("y"), lax.axis_index("c")


def _remote(src, dst, ssem, rsem, to):
    return pltpu.make_async_remote_copy(src_ref=src, dst_ref=dst, send_sem=ssem, recv_sem=rsem, device_id=to,
                                        device_id_type=MESH)


def _split_axis(shape):
    return 0 if shape[0] % 32 == 0 else 1


def _half_shape(shape, parts=2):
    return (shape[0] // parts, shape[1]) if _split_axis(shape) == 0 else (shape[0], shape[1] // parts)


def _half(shape, c):
    R, C = shape
    if _split_axis(shape) == 0:
        return (pl.ds(pl.multiple_of(c * (R // 2), 16), R // 2), slice(None))
    return (slice(None), pl.ds(pl.multiple_of(c * (C // 2), LANES), C // 2))


def _half_block(shape, parts, lead, which):
    blk = _half_shape(shape, parts)
    idx = (which, 0) if _split_axis(shape) == 0 else (0, which)
    return blk, tuple(lead) + idx


class _Comm:
    def __init__(self, ins, out_shapes, sems, start, finish):
        self.ins, self.out_shapes, self.sems, self.start, self.finish = list(ins), list(out_shapes), list(sems), start, finish


def _ag_comm(shards):
    n = len(shards)

    def parts(ins, outs, sems):
        send_sems, recv_sems, local_sems = sems
        x, y, c = _pos()
        me, sib = (x, y, c), (x, y, 1 - c)
        chips = [(1 - x, y), (x, 1 - y), (1 - x, 1 - y)]

        def rows(w, px, py, pc):
            return outs[w].at[(2 * px + py,) + _half(shards[w].shape, pc)]

        def copy(w, k, block, to, src=None):
            return _remote(rows(w, *block) if src is None else src, rows(w, *block), send_sems.at[w, k],
                           recv_sems.at[w, k], to)

        mine, first = [], []
        for w in range(n):
            src = ins[w].at[_half(shards[w].shape, c)]
            mine.append(pltpu.make_async_copy(src, rows(w, *me), local_sems.at[w]))
            first += [copy(w, 0, me, sib, src=src)] + [copy(w, 1 + j, me, (*chip, c), src=src) for j, chip in enumerate(chips)]
        return c, me, sib, chips, copy, mine, first

    def start(ins, outs, sems):
        _, _, _, _, _, mine, first = parts(ins, outs, sems)
        for cp in mine + first:
            cp.start()

    def finish(ins, outs, sems):
        c, me, sib, chips, copy, mine, first = parts(ins, outs, sems)
        passed = []
        for w in range(n):
            for j, chip in enumerate(chips):
                copy(w, 1 + j, (*chip, c), me).wait_recv()
                passed.append(copy(w, 4 + j, (*chip, c), sib))
                passed[-1].start()
        for w in range(n):
            copy(w, 0, sib, me).wait_recv()
            for j, chip in enumerate(chips):
                copy(w, 4 + j, (*chip, 1 - c), me).wait_recv()
        for cp in first + passed:
            cp.wait_send()
        for cp in mine:
            cp.wait()

    D7 = pltpu.SemaphoreType.DMA((n, 7))
    return _Comm(shards, [_sds((4,) + s.shape, s.dtype) for s in shards], [D7, D7, pltpu.SemaphoreType.DMA((n,))],
                 start, finish)


def _sibling_comm(gs):
    n = len(gs)

    def copies(ins, outs, sems):
        send_sems, recv_sems = sems
        x, y, c = _pos()
        return [_remote(ins[w].at[(s,) + _half(gs[w].shape[1:], 1 - c)], outs[w].at[s], send_sems.at[w, s],
                        recv_sems.at[w, s], (x, y, 1 - c)) for w in range(n) for s in range(4)]

    def start(ins, outs, sems):
        for cp in copies(ins, outs, sems):
            cp.start()

    def finish(ins, outs, sems):
        for cp in copies(ins, outs, sems):
            cp.wait()

    D4 = pltpu.SemaphoreType.DMA((n, 4))
    return _Comm(gs, [_sds((4,) + _half_shape(g.shape[1:]), F32) for g in gs], [D4, D4], start, finish)


def _ici_comm(pbs):
    n = len(pbs)

    def copies(ins, outs, sems):
        send_sems, recv_sems = sems
        x, y, c = _pos()
        return [_remote(ins[w].at[2 * tx + ty], outs[w].at[j], send_sems.at[w, j], recv_sems.at[w, j], (tx, ty, c))
                for w in range(n) for j, (tx, ty) in enumerate([(1 - x, y), (x, 1 - y), (1 - x, 1 - y)])]

    def start(ins, outs, sems):
        for cp in copies(ins, outs, sems):
            cp.start()

    def finish(ins, outs, sems):
        for cp in copies(ins, outs, sems):
            cp.wait()

    D3 = pltpu.SemaphoreType.DMA((n, 3))
    return _Comm(pbs, [_sds((3,) + p.shape[1:], BF16) for p in pbs], [D3, D3], start, finish)


def _join(*comms):
    counts = [(len(c.ins), len(c.out_shapes), len(c.sems)) for c in comms]

    def each(which):
        def run(ins, outs, sems):
            i = o = k = 0
            for c, (ni, no, nk) in zip(comms, counts):
                getattr(c, which)(ins[i:i + ni], outs[o:o + no], sems[k:k + nk])
                i, o, k = i + ni, o + no, k + nk
        return run

    return _Comm(sum((c.ins for c in comms), []), sum((c.out_shapes for c in comms), []),
                 sum((c.sems for c in comms), []), each("start"), each("finish"))


def _run_comm(comm, name):
    ni, no = len(comm.ins), len(comm.out_shapes)

    def body(*refs):
        ins, outs, sems = refs[:ni], refs[ni:ni + no], refs[ni + no:]
        comm.start(ins, outs, sems)
        comm.finish(ins, outs, sems)

    return pl.pallas_call(body, name=name, out_shape=comm.out_shapes, in_specs=[ANY] * ni, out_specs=[ANY] * no,
                          scratch_shapes=comm.sems)(*comm.ins)


def _call(body, comm, *, name, grid, in_specs, out_specs, out_shape, scratch_shapes, args):
    n_grid = len(grid)
    if comm is None:
        res = pl.pallas_call(body, name=name, grid=grid, in_specs=in_specs, out_specs=out_specs, out_shape=out_shape,
                             scratch_shapes=scratch_shapes, compiler_params=_params(n_grid))(*args)
        return list(res), []
    n_in, n_out, n_scr = len(in_specs), len(out_specs), len(scratch_shapes)
    ni, no = len(comm.ins), len(comm.out_shapes)

    def carried(*refs):
        ins, refs = refs[:n_in], refs[n_in:]
        cins, refs = refs[:ni], refs[ni:]
        outs, refs = refs[:n_out], refs[n_out:]
        couts, refs = refs[:no], refs[no:]
        scr, csems = refs[:n_scr], refs[n_scr:]
        ids = [pl.program_id(ax) for ax in range(n_grid)]
        first = functools.reduce(jnp.logical_and, [i == 0 for i in ids])
        last = functools.reduce(jnp.logical_and, [i == g - 1 for i, g in zip(ids, grid)])

        @pl.when(first)
        def _():
            comm.start(cins, couts, csems)

        body(*ins, *outs, *scr)

        @pl.when(last)
        def _():
            comm.finish(cins, couts, csems)

    res = pl.pallas_call(
        carried, name=name, grid=grid, in_specs=list(in_specs) + [ANY] * ni, out_specs=list(out_specs) + [ANY] * no,
        out_shape=list(out_shape) + comm.out_shapes, scratch_shapes=list(scratch_shapes) + comm.sems,
        compiler_params=_params(n_grid))(*args, *comm.ins)
    return list(res[:n_out]), list(res[n_out:])


def _small_allreduce(v, name, halves=()):
    P = v.shape[0]
    n = len(halves)
    vm = pl.BlockSpec(memory_space=pltpu.VMEM)

    def body(v_ref, *refs):
        o_ref, outs = refs[n], refs[n + 1:2 * n + 1]
        gath, send_sems, recv_sems, half_send, half_recv = refs[2 * n + 1:]
        x, y, c = _pos()
        me = 4 * x + 2 * y + c
        gath[me] = v_ref[...]
        cps = []
        for r in range(1, 8):
            tx = (1 - x) if r & 4 else x
            ty = (1 - y) if r & 2 else y
            tc = (1 - c) if r & 1 else c
            cps.append(_remote(v_ref, gath.at[me], send_sems.at[r - 1], recv_sems.at[r - 1], (tx, ty, tc)))
        for w in range(n):
            mine = outs[w].at[_half(halves[w].shape, c)]
            cps.append(_remote(mine, mine, half_send.at[w], half_recv.at[w], (x, y, 1 - c)))
        for cp in cps:
            cp.start()
        for cp in cps:
            cp.wait()
        acc = gath[0]
        for d in range(1, 8):
            acc = acc + gath[d]
        o_ref[...] = acc

    res = pl.pallas_call(
        body, name=name, out_shape=[_sds((P, LANES), F32)] + [_sds(g.shape, F32) for g in halves],
        in_specs=[vm] + [ANY] * n, out_specs=[vm] + [ANY] * n, input_output_aliases={1 + w: 1 + w for w in range(n)},
        scratch_shapes=[pltpu.VMEM((8, P, LANES), F32), pltpu.SemaphoreType.DMA((7,)), pltpu.SemaphoreType.DMA((7,)),
                        pltpu.SemaphoreType.DMA((max(n, 1),)), pltpu.SemaphoreType.DMA((max(n, 1),))],
    )(v, *halves)
    return res[0], list(res[1:])


def _chip_sum(g, rcv, pos, name):
    shard = g.shape[1:]
    hs = _half_shape(shard)

    def body(pos_ref, g_ref, r_ref, o_ref):
        o_ref[...] = (g_ref[...] + r_ref[...]).astype(BF16)

    return pl.pallas_call(
        body, name=name, out_shape=_sds((4,) + hs, BF16),
        grid_spec=pltpu.PrefetchScalarGridSpec(
            num_scalar_prefetch=1, grid=(4,),
            in_specs=[pl.BlockSpec((1,) + hs, lambda s, pos: _half_block(shard, 2, (s,), pos[0])[1]),
                      pl.BlockSpec((1,) + hs, lambda s, pos: (s, 0, 0))],
            out_specs=pl.BlockSpec((1,) + hs, lambda s, pos: (s, 0, 0))),
        compiler_params=_params(1),
    )(pos, g, rcv)


def _final_sum(g, rcv, rc, pos, name):
    shard = g.shape[1:]
    qs = _half_shape(shard, 4)

    def body(pos_ref, g_ref, r_ref, rc_ref, o_ref):
        acc = g_ref[0] + r_ref[0]
        for j in range(3):
            acc = acc + rc_ref[j].astype(F32)
        o_ref[...] = acc

    return pl.pallas_call(
        body, name=name, out_shape=_sds(shard, F32),
        grid_spec=pltpu.PrefetchScalarGridSpec(
            num_scalar_prefetch=1, grid=(2,),
            in_specs=[pl.BlockSpec((1,) + qs, lambda i, pos: _half_block(shard, 4, (pos[1],), pos[0] * 2 + i)[1]),
                      pl.BlockSpec((1,) + qs, lambda i, pos: _half_block(shard, 4, (pos[1],), i)[1]),
                      pl.BlockSpec((3,) + qs, lambda i, pos: _half_block(shard, 4, (0,), i)[1])],
            out_specs=pl.BlockSpec(qs, lambda i, pos: _half_block(shard, 4, (), pos[0] * 2 + i)[1])),
        compiler_params=_params(1),
    )(pos, g, rcv, rc)


def _adamw_math(w, g, m, v):
    m = ADAM_B1 * m + (1.0 - ADAM_B1) * g
    v = ADAM_B2 * v + (1.0 - ADAM_B2) * (g * g)
    m_hat = m / (1.0 - ADAM_B1 ** ADAM_STEP)
    v_hat = v / (1.0 - ADAM_B2 ** ADAM_STEP)
    delta = -ADAM_LR * (m_hat / (jnp.sqrt(v_hat) + ADAM_EPS) + ADAM_WD * w)
    return delta, m, v


def _adamw(w, g, m, v, name, blk_shape):
    R, C = w.shape

    def body(w_ref, g_ref, m_ref, v_ref, go_ref, d_ref, nm_ref, nv_ref):
        g = g_ref[...]
        d, nm, nv = _adamw_math(w_ref[...], g, m_ref[...], v_ref[...])
        go_ref[...] = g
        d_ref[...] = d
        nm_ref[...] = nm
        nv_ref[...] = nv

    blk = pl.BlockSpec(blk_shape, lambda i, j: (i, j))
    return pl.pallas_call(
        body, name=name, grid=(R // blk_shape[0], C // blk_shape[1]), in_specs=[blk] * 4, out_specs=[blk] * 4,
        out_shape=[_sds((R, C), F32)] * 4, compiler_params=_params(2),
    )(w, g, m, v)


SMALL = (("g_mix", 8), ("b_f", 8), ("conv_w", None), ("conv_b", 8), ("ln_g", 8), ("ln_b", 8), ("g_x", 8), ("g_mem", 8),
         ("g_ffn", 8), ("g_final", 8), ("loss", 8))


def _pack_small(parts, conv_rows):
    rows = []
    for name, n in SMALL:
        if name not in parts:
            continue
        n = conv_rows if n is None else n
        flat = parts[name].reshape(-1).astype(F32)
        flat = jnp.pad(flat, (0, n * LANES - flat.shape[0]))
        rows.append(flat.reshape(n, LANES))
    return jnp.concatenate(rows, axis=0)


def _unpack_small(p, shapes, conv_rows):
    out, off = {}, 0
    for name, n in SMALL:
        if name not in shapes:
            continue
        n = conv_rows if n is None else n
        size = math.prod(shapes[name])
        out[name] = p[off:off + n].reshape(-1)[:size].reshape(shapes[name])
        off += n
    return out


def kernel(x, mem, g_mix, w_in, b_f, conv_w, conv_b, ln_g, ln_b, w_out, g_x, g_mem, w_mq, w_mkv, w_mo, g_ffn, w_gu, w_down, g_final, loss_target, m_g_mix, m_w_in, m_b_f, m_conv_w, m_conv_b, m_ln_g, m_ln_b, m_w_out, m_g_x, m_g_mem, m_w_mq, m_w_mkv, m_w_mo, m_g_ffn, m_w_gu, m_w_down, m_g_final, v_g_mix, v_w_in, v_b_f, v_conv_w, v_conv_b, v_ln_g, v_ln_b, v_w_out, v_g_x, v_g_mem, v_w_mq, v_w_mkv, v_w_mo, v_g_ffn, v_w_gu, v_w_down, v_g_final):
    names = ["g_mix", "w_in", "b_f", "conv_w", "conv_b", "ln_g", "ln_b", "w_out", "g_x", "g_mem", "w_mq", "w_mkv",
             "w_mo", "g_ffn", "w_gu", "w_down", "g_final"]
    W = dict(zip(names, [g_mix, w_in, b_f, conv_w, conv_b, ln_g, ln_b, w_out, g_x, g_mem, w_mq, w_mkv, w_mo, g_ffn,
                         w_gu, w_down, g_final]))
    M = dict(zip(names, [m_g_mix, m_w_in, m_b_f, m_conv_w, m_conv_b, m_ln_g, m_ln_b, m_w_out, m_g_x, m_g_mem, m_w_mq,
                         m_w_mkv, m_w_mo, m_g_ffn, m_w_gu, m_w_down, m_g_final]))
    V = dict(zip(names, [v_g_mix, v_w_in, v_b_f, v_conv_w, v_conv_b, v_ln_g, v_ln_b, v_w_out, v_g_x, v_g_mem, v_w_mq,
                         v_w_mkv, v_w_mo, v_g_ffn, v_w_gu, v_w_down, v_g_final]))
    big_names = [n for n, _, _, _ in BIG]
    B, S, _ = x.shape
    T = B * S
    mx, my, mc = _pos()
    chip = 2 * mx + my
    pos = jnp.stack([mc, chip]).astype(jnp.int32)

    shard2d = lambda a: a.reshape(a.shape[-2], a.shape[-1])
    big2d = lambda d, n: shard2d(d[n]).T if n == "w_in" else shard2d(d[n])
    shard_bf = {n: big2d(W, n).astype(BF16) for n in big_names}
    ag_mid = ["w_mkv", "w_out", "w_mq", "w_mo"]
    ag_ffn = ["w_gu", "w_down"]
    cw_mine = jnp.pad(shard2d(conv_w), ((0, 1), (0, 0)))
    w_in_slab, cw_slab = _run_comm(_ag_comm([shard_bf["w_in"], cw_mine]), "ag_w_in")
    slab = {"w_in": w_in_slab}
    w_int = w_in_slab.reshape(D_IN, D)
    w_ft = jnp.pad(w_int[OFF_F:D_IN], ((0, D_IN_PAD - D_IN), (0, 0)))
    cw = jnp.transpose(cw_slab, (1, 0, 2)).reshape(CONV_HALO, CONV_CH)

    row = lambda a: a.reshape(1, -1)
    bf_pad = jnp.pad(row(b_f), ((0, 0), (0, LANES - 8)))
    x2d = x.reshape(T, D)
    mem2d = mem.reshape(B * MEM_LEN, D)
    tgt = loss_target.reshape(T, D)

    (h, u, gt, q, k, v, zf, c, cq, qx, kx), got = _fwd_in(x2d, row(g_mix), w_int, w_ft, bf_pad, B, S,
                                                  comm=_ag_comm([shard_bf[n] for n in ag_mid[:2]]))
    slab.update(zip(ag_mid[:2], got))
    ckT = jnp.transpose(c.reshape(B, S, LANES)[:, :, :8], (0, 2, 1)).reshape(B, N_PAIR, 2, S)
    ckT = jnp.pad(ckT, ((0, 0), (0, 0), (0, 6), (0, 0)))
    (y, co), got = _conv_fwd(u, gt, cw, row(conv_b), row(ln_g), row(ln_b), B, S,
                             comm=_ag_comm([shard_bf[n] for n in ag_mid[2:]]))
    slab.update(zip(ag_mid[2:], got))
    (o, fox_bias), got = _fox_fwd(qx, kx, v, cq, B, S, comm=_ag_comm([shard_bf[n] for n in ag_ffn]))
    slab.update(zip(ag_ffn, got))
    full = {n: slab[n] if by_col else slab[n].reshape(4 * r, c) for n, r, c, by_col in BIG}
    mn, km, vm = _mem_kv(mem2d, row(g_mem), full["w_mkv"], B)
    (x1, hx, qm, om, x2, cat), _ = _fwd_mid(x2d, co, o, km, vm, full["w_out"], full["w_mq"], full["w_mo"], row(g_x), B, S)
    hf, gu, act, dx3, loss_p, dg_final = _fwd_ffn(x2, tgt, full["w_gu"], full["w_down"], row(g_ffn), row(g_final), T)

    pos_sum = lambda gs, rcvs, ns: [_chip_sum(g, r, pos, "rs_chip_sum_" + n) for g, r, n in zip(gs, rcvs, ns)]
    fin_sum = lambda gs, rcvs, rcs, ns: [_final_sum(g, r, q3, pos, "rs_final_sum_" + n)
                                         for g, r, q3, n in zip(gs, rcvs, rcs, ns)]
    RH = {}
    dgu, dx2, dg_ffn = _bwd_ffn(dx3, gu, x2, full["w_gu"], full["w_down"], row(g_ffn), T)
    g_ffn_w = [_dw(hf, dgu, "dw_gu", FF_CHUNK, slabs=True), _dw(act, dx3, "dw_down", 512).reshape(4, D_FF // 4, D)]
    (dx1, dqm, dco, do, dd, dkm, dvm, dg_x), rcv_ffn = _bwd_mid(dx2, x1, qm, km, vm, o, full["w_mo"], full["w_mq"],
                                                                full["w_out"], row(g_x), B, S, comm=_sibling_comm(g_ffn_w))
    pb_ffn = pos_sum(g_ffn_w, rcv_ffn, ag_ffn)
    dkv, dg_mem = _mem_bwd(dkm, dvm, mem2d, full["w_mkv"], row(g_mem), B)
    g_mid_w = [_dw(mn, dkv, "dw_mkv", 512, slabs=True), _dw(cat, dx1, "dw_out", 512).reshape(4, 256, D),
               _dw(hx, dqm, "dw_mq", 512).reshape(4, 256, D), _dw(om, dx2, "dw_mo", 512).reshape(4, 256, D)]
    (dq, dk, dv, dc, dcq), got = _fox_bwd(q, k, v, do, fox_bias, dd, ckT, B, S,
                                          comm=_join(_ici_comm(pb_ffn), _sibling_comm(g_mid_w)))
    rc_ffn, rcv_mid = got[:len(pb_ffn)], got[len(pb_ffn):]
    RH.update(zip(ag_ffn, fin_sum(g_ffn_w, rcv_ffn, rc_ffn, ag_ffn)))
    pb_mid = pos_sum(g_mid_w, rcv_mid, ag_mid)
    dc8 = jnp.transpose(dc[:, :, :2, :].reshape(B, 8, S), (0, 2, 1)).reshape(T, 8)
    dc8 = dc8 + dcq.reshape(T, 8, HEAD_D)[:, :, 0]
    dzf, dbf = _fgate_bwd(jnp.pad(dc8, ((0, 0), (0, LANES - 8))), zf, B, S)
    (du, dgt, dcw, dvec), rc_mid = _conv_bwd(dco, y, u, gt, cw, row(ln_g), row(ln_b), B, S, comm=_ici_comm(pb_mid))
    RH.update(zip(ag_mid, fin_sum(g_mid_w, rcv_mid, rc_mid, ag_mid)))
    dz = jnp.concatenate([du, dgt, dq, dk, dv, dzf], axis=1)
    g_in_w = [_dw(dz, h, "dw_in", 512, tk=D_IN_PAD // 3, rows=D_IN).reshape(4, D_IN // 4, D)]
    rcv_in = _run_comm(_sibling_comm(g_in_w), "rs_sibling_in")
    (grad_x, dg_mix), rc_in = _bwd_in(dz, w_int, w_ft, x2d, dx1, row(g_mix), T,
                                      comm=_ici_comm(pos_sum(g_in_w, rcv_in, ["w_in"])))
    RH.update(zip(["w_in"], fin_sum(g_in_w, rcv_in, rc_in, ["w_in"])))

    small_g = {"g_mix": dg_mix, "b_f": dbf[:, :8], "conv_w": dcw, "conv_b": dvec[0], "ln_g": dvec[1], "ln_b": dvec[2],
               "g_x": dg_x, "g_mem": dg_mem, "g_ffn": dg_ffn, "g_final": dg_final, "loss": loss_p[:, :1]}
    sg, filled = _small_allreduce(_pack_small(small_g, CONV_HALO * 4), "allreduce_small", [RH[n] for n in big_names])
    shared = dict(zip(big_names, filled))
    G, DL, NM, NV = {}, {}, {}, {}
    for n in big_names:
        G[n], DL[n], NM[n], NV[n] = _adamw(big2d(W, n), shared[n], big2d(M, n), big2d(V, n), "adamw_" + n,
                                           _half_shape(shared[n].shape))
    shapes = {n: W[n].shape for n in names if n not in big_names}
    shapes["conv_w"] = (CONV_HALO, CONV_CH)
    shapes["loss"] = (1,)
    sgrads = _unpack_small(sg, shapes, CONV_HALO * 4)
    loss = sgrads.pop("loss")[0]
    sgrads["conv_w"] = lax.dynamic_slice(sgrads["conv_w"], (0, chip * LANES), (CONV_K, LANES)).reshape(W["conv_w"].shape)
    spack = lambda d: _pack_small({n: d[n] for n in sgrads}, CONV_HALO)
    _, sd, snm, snv = _adamw(spack(W), spack(sgrads), spack(M), spack(V), "adamw_small", (8, LANES))
    sshapes = {n: W[n].shape for n in sgrads}
    SD, SNM, SNV = (_unpack_small(a, sshapes, CONV_HALO) for a in (sd, snm, snv))

    def collect(bigs, smalls):
        back = lambda n: (bigs[n].T if n == "w_in" else bigs[n]).reshape(W[n].shape)
        return [back(n) if n in big_names else smalls[n] for n in names]

    return (loss, grad_x.reshape(x.shape), *collect(G, sgrads), *collect(DL, SD), *collect(NM, SNM), *collect(NV, SNV))
```

```python
import functools
import math

import jax
import jax.numpy as jnp
from jax import lax
from jax.experimental import pallas as pl
from jax.experimental.pallas import tpu as pltpu

F32, BF16 = jnp.float32, jnp.bfloat16
MESH = pl.DeviceIdType.MESH

D = 1024
CONV_CH = 512
CONV_K = 31
CONV_HALO = 32
FOX_W = 512
HEAD_D = 64
N_PAIR = 4
MEM_LEN = 256
MEM_HEADS = 4
MEM_HD = 256
D_FF = 2816
FF_CHUNK = 1408
D_IN = 2568
D_IN_PAD = 2688
OFF_F = 2560
EPS = 1e-6
LANES = 128

ADAM_LR, ADAM_B1, ADAM_B2, ADAM_EPS, ADAM_WD, ADAM_STEP = 0.001, 0.9, 0.999, 1e-08, 0.01, 10

VMEM_LIMIT = 60 * 1024 * 1024

BIG = (("w_out", 256, 1024, False), ("w_mq", 256, 1024, False), ("w_mkv", 1024, 512, True),
       ("w_mo", 256, 1024, False), ("w_gu", 1024, 1408, True), ("w_down", 704, 1024, False),
       ("w_in", 642, 1024, False))

ANY = pl.BlockSpec(memory_space=pl.ANY)


def _sig(x):
    return 1.0 / (1.0 + jnp.exp(-x))


def _dot(a, b):
    return jnp.dot(a, b, preferred_element_type=F32)


def _dot_nt(a, b):
    return lax.dot_general(a, b, (((1,), (1,)), ((), ())), preferred_element_type=F32)


def _dot_tn(a, b):
    return lax.dot_general(a, b, (((0,), (0,)), ((), ())), preferred_element_type=F32)


def _split3(x):
    hi = x.astype(BF16)
    r = x - hi.astype(F32)
    mid = r.astype(BF16)
    return hi, mid, (r - mid.astype(F32)).astype(BF16)


def _dot_01(a, b):
    if a.dtype == jnp.bool_:
        return sum(_dot(a.astype(BF16), t) for t in _split3(b))
    return sum(_dot(t, b.astype(BF16)) for t in _split3(a))


def _resident(a):
    nd = a.ndim
    return pl.BlockSpec(a.shape, lambda *_: (0,) * nd, pipeline_mode=pl.Buffered(1))


def _acc_spec(shape):
    nd = len(shape)
    return pl.BlockSpec(shape, lambda *_: (0,) * nd)


def _params(n_grid):
    return pltpu.CompilerParams(dimension_semantics=("arbitrary",) * n_grid, vmem_limit_bytes=VMEM_LIMIT)


def _sds(shape, dtype):
    return jax.ShapeDtypeStruct(shape, dtype)


def _rms(x):
    r = lax.rsqrt(jnp.mean(x * x, axis=-1, keepdims=True) + EPS)
    return r, x * r


def _rms_bwd(dy, xh, r, g):
    dxh = dy * g
    dx = r * (dxh - xh * jnp.mean(dxh * xh, axis=-1, keepdims=True))
    return dx, dy * xh


def _head_expand(rows, cols):
    hd = lax.broadcasted_iota(jnp.int32, (rows, cols), 1) // HEAD_D
    hr = lax.broadcasted_iota(jnp.int32, (rows, cols), 0)
    return hd == hr


def _fwd_in(x2, g_mix, w_int, w_ft, bf_pad, B, S, comm=None):
    T = B * S
    TB = min(512, S)
    nb = S // TB

    def body(x_ref, g_ref, w_ref, wf_ref, bf_ref, h_ref, u_ref, gt_ref, q_ref, k_ref, v_ref, zf_ref, c_ref, cq_ref,
             qx_ref, kx_ref, carry):
        j = pl.program_id(1)

        @pl.when(j == 0)
        def _():
            carry[...] = jnp.zeros_like(carry)

        _, xh = _rms(x_ref[...])
        h = (xh * g_ref[...]).astype(BF16)
        h_ref[...] = h
        u_ref[...] = _dot_nt(h, w_ref[0:512, :])
        gt_ref[...] = _dot_nt(h, w_ref[512:1024, :])
        qb = _dot_nt(h, w_ref[1024:1536, :]).astype(BF16)
        kb = _dot_nt(h, w_ref[1536:2048, :]).astype(BF16)
        q_ref[...] = qb
        k_ref[...] = kb
        v_ref[...] = _dot_nt(h, w_ref[2048:2560, :]).astype(BF16)
        zf = _dot_nt(h, wf_ref[...]) + bf_ref[...]
        zf_ref[...] = zf
        lane = lax.broadcasted_iota(jnp.int32, zf.shape, 1)
        logf = jnp.where(lane < 8, jnp.minimum(zf, 0.0) - jnp.log(1.0 + jnp.exp(-jnp.abs(zf))), 0.0)
        row = lax.broadcasted_iota(jnp.int32, (TB, TB), 0)
        col = lax.broadcasted_iota(jnp.int32, (TB, TB), 1)
        c = _dot_01(row >= col, logf) + carry[0:1, :]
        carry[0:1, :] = c[TB - 1:TB, :]
        c_ref[...] = c
        cq = _dot_01(c, _head_expand(LANES, FOX_W))
        cq_ref[...] = cq
        hl = lax.broadcasted_iota(jnp.int32, (TB, LANES), 1)
        for hd in range(2 * N_PAIR):
            grp = slice((hd // 2) * LANES, (hd // 2 + 1) * LANES)
            swap = (lambda t: t) if hd % 2 == 0 else (lambda t: pltpu.roll(t, HEAD_D, 1))
            qf = swap(qb[:, grp].astype(F32) * (1.0 / math.sqrt(HEAD_D)))
            kf = swap(kb[:, grp].astype(F32))
            cv = cq[:, grp] if hd % 2 == 1 else pltpu.roll(cq[:, grp], HEAD_D, 1)
            hi = cv.astype(BF16).astype(F32)
            mid = (cv - hi).astype(BF16).astype(F32)
            lo = (cv - hi - mid).astype(BF16).astype(F32)
            pick = lambda a, b, c3, one_from, one_to: jnp.where(hl == a[0], a[1], jnp.where(hl == b[0], b[1], jnp.where(
                hl == c3[0], c3[1], jnp.where((hl >= one_from) & (hl < one_to), 1.0, 0.0))))
            qx = jnp.where(hl < HEAD_D, qf, pick((67, hi), (68, mid), (69, lo), 64, 67))
            kx = jnp.where(hl < HEAD_D, kf, pick((64, -hi), (65, -mid), (66, -lo), 67, 70))
            qx_ref[:, hd * LANES:(hd + 1) * LANES] = qx.astype(BF16)
            kx_ref[:, hd * LANES:(hd + 1) * LANES] = kx.astype(BF16)

    tok = lambda w: pl.BlockSpec((TB, w), lambda b, j: (b * nb + j, 0))
    outs = [(D, BF16), (512, F32), (512, F32), (512, BF16), (512, BF16), (512, BF16), (LANES, F32),
            (LANES, F32), (FOX_W, F32), (2 * FOX_W, BF16), (2 * FOX_W, BF16)]
    return _call(
        body, comm, name="fwd_in", grid=(B, nb),
        in_specs=[tok(D), _resident(g_mix), _resident(w_int), _resident(w_ft), _resident(bf_pad)],
        out_specs=[tok(w) for w, _ in outs],
        out_shape=[_sds((T, w), dt) for w, dt in outs],
        scratch_shapes=[pltpu.VMEM((8, LANES), F32)],
        args=(x2, g_mix, w_int, w_ft, bf_pad))


def _head_sum(n):
    hc = lax.broadcasted_iota(jnp.int32, (n, n), 1) // HEAD_D
    hr = lax.broadcasted_iota(jnp.int32, (n, n), 0) // HEAD_D
    return hc == hr


def _layernorm_silu(y, lg, lb):
    mu = jnp.mean(y, axis=-1, keepdims=True)
    yc = y - mu
    rs = lax.rsqrt(jnp.mean(yc * yc, axis=-1, keepdims=True) + EPS)
    n = yc * rs
    l = n * lg + lb
    return rs, n, l


SUB = 8


def _shifted_copies(cat, sh, rows):
    for r in range(1, SUB):
        sh[r, 0:rows, :] = cat[r:r + rows, :]


def _tap(cat, sh, off, rows, cols=slice(None)):
    r = off % SUB
    return cat[off:off + rows, cols] if r == 0 else sh[r, off - r:off - r + rows, cols]


CONV_ROWS = 128


def _conv_pieces(CB):
    rows = min(CONV_ROWS, CB)
    return [(r0, rows, slice(c0, c0 + LANES)) for c0 in range(0, CONV_CH, LANES) for r0 in range(0, CB, rows)]


def _conv_fwd(u, gt, cw, cb, lng, lnb, B, S, comm=None):
    T = B * S
    CB = min(256, S)
    nb = S // CB

    def body(u_ref, gt_ref, w_ref, cb_ref, lg_ref, lb_ref, y_ref, co_ref, acat, ash):
        j = pl.program_id(1)

        @pl.when(j == 0)
        def _():
            acat[0:CONV_HALO, :] = jnp.zeros((CONV_HALO, CONV_CH), F32)

        acat[CONV_HALO:CONV_HALO + CB, :] = u_ref[...] * _sig(gt_ref[...])
        _shifted_copies(acat, ash, CB + CONV_HALO - SUB)
        for r0, rows, cs in _conv_pieces(CB):
            acc = jnp.zeros((rows, LANES), F32) + cb_ref[:, cs]
            for k in range(CONV_K):
                acc = acc + w_ref[k:k + 1, cs] * _tap(acat, ash, r0 + CONV_HALO - (CONV_K - 1) + k, rows, cs)
            y_ref[r0:r0 + rows, cs] = acc
        acat[0:CONV_HALO, :] = acat[CB:CB + CONV_HALO, :]
        _, _, l = _layernorm_silu(y_ref[...], lg_ref[...], lb_ref[...])
        co_ref[...] = (l * _sig(l)).astype(BF16)

    tok = lambda w: pl.BlockSpec((CB, w), lambda b, j: (b * nb + j, 0))
    return _call(
        body, comm, name="conv_fwd", grid=(B, nb),
        in_specs=[tok(CONV_CH), tok(CONV_CH), _resident(cw), _resident(cb), _resident(lng), _resident(lnb)],
        out_specs=[tok(CONV_CH), tok(CONV_CH)],
        out_shape=[_sds((T, CONV_CH), F32), _sds((T, CONV_CH), BF16)],
        scratch_shapes=[pltpu.VMEM((CONV_HALO + CB, CONV_CH), F32),
                        pltpu.VMEM((SUB, CB + CONV_HALO - SUB, CONV_CH), F32)],
        args=(u, gt, cw, cb, lng, lnb))


def _fox_fwd(qx, kx, v, cq, B, S, comm=None):
    T = B * S
    TQ = min(256, S)
    nq = S // TQ
    one_lane = (HEAD_D, 0)

    def body(qa_ref, qb_ref, ka_ref, kb_ref, v_ref, cq_ref, o_ref, lse_ref, s_scr, s_odd, m_scr, acc_scr):
        i = pl.program_id(2)
        lane = lax.broadcasted_iota(jnp.int32, (TQ, LANES), 1)
        lo = lane < HEAD_D
        qh = (qa_ref[...], qb_ref[...])
        kh = (ka_ref, kb_ref)
        m_scr[...] = jnp.full(m_scr.shape, -1e30, F32)
        acc_scr[...] = jnp.zeros_like(acc_scr)
        row = lax.broadcasted_iota(jnp.int32, (TQ, TQ), 0)
        col = lax.broadcasted_iota(jnp.int32, (TQ, TQ), 1)
        wide = lambda x: jnp.concatenate([x, x], axis=1) if TQ == 2 * LANES else jnp.tile(x, (1, TQ // LANES))

        def scores(j, s_buf):
            start = pl.multiple_of(j * TQ, TQ)
            for h in range(2):
                s_buf[h] = _dot_nt(qh[h], kh[h][pl.ds(start, TQ), :])

        def softmax_step(j, s_buf, diagonal):
            start = pl.multiple_of(j * TQ, TQ)
            vj = v_ref[pl.ds(start, TQ), :]
            for h in range(2):
                def logits():
                    return jnp.where(col <= row, s_buf[h], -1e30) if diagonal else s_buf[h]

                m_old = m_scr[h]
                m_new = jnp.maximum(m_old, jnp.max(logits(), axis=-1, keepdims=True))
                alpha = jnp.exp(m_old - m_new)
                m_scr[h] = m_new
                p = jnp.exp(logits() - wide(m_new)).astype(BF16)
                vx = jnp.where(lane == one_lane[h], jnp.ones_like(vj), jnp.where(lo if h == 0 else ~lo, vj, jnp.zeros_like(vj)))
                acc_scr[h] = alpha * acc_scr[h] + _dot(p, vx)

        def two_blocks(jj, carry):
            j = 2 * jj
            scores(j + 1, s_odd)
            softmax_step(j, s_scr, False)
            scores(j + 2, s_scr)
            softmax_step(j + 1, s_odd, False)
            return carry

        scores(0, s_scr)
        lax.fori_loop(0, i // 2, two_blocks, 0)

        @pl.when(i % 2 == 0)
        def _():
            softmax_step(i, s_scr, True)

        @pl.when(i % 2 == 1)
        def _():
            scores(i, s_odd)
            softmax_step(i - 1, s_scr, False)
            softmax_step(i, s_odd, True)

        acc_a, acc_b = acc_scr[0], acc_scr[1]
        l_a = acc_a[:, one_lane[0]:one_lane[0] + 1]
        l_b = acc_b[:, one_lane[1]:one_lane[1] + 1]
        o_ref[...] = jnp.where(lo, acc_a / l_a, acc_b / l_b)
        lse_ref[...] = cq_ref[...] - jnp.where(lo, m_scr[0] + jnp.log(l_a), m_scr[1] + jnp.log(l_b))

    qspec = pl.BlockSpec((TQ, LANES), lambda b, p, i: (b * nq + i, p))
    kspec = pl.BlockSpec((S, LANES), lambda b, p, i: (b, p))
    qhead = lambda h: pl.BlockSpec((TQ, LANES), lambda b, p, i: (b * nq + i, 2 * p + h))
    khead = lambda h: pl.BlockSpec((S, LANES), lambda b, p, i: (b, 2 * p + h))
    return _call(
        body, comm, name="fox_fwd", grid=(B, N_PAIR, nq),
        in_specs=[qhead(0), qhead(1), khead(0), khead(1), kspec, qspec],
        out_specs=[qspec, qspec],
        out_shape=[_sds((T, FOX_W), F32), _sds((T, FOX_W), F32)],
        scratch_shapes=[pltpu.VMEM((2, TQ, TQ), F32), pltpu.VMEM((2, TQ, TQ), F32),
                        pltpu.VMEM((2, TQ, LANES), F32), pltpu.VMEM((2, TQ, LANES), F32)],
        args=(qx, qx, kx, kx, v, cq))


def _mem_kv(mem2, g_mem, w_mkv, B):
    def body(m_ref, g_ref, w_ref, mn_ref, km_ref, vm_ref):
        _, xh = _rms(m_ref[...])
        mn = (xh * g_ref[...]).astype(BF16)
        mn_ref[...] = mn
        for s in range(2):
            km_ref[:, 512 * s:512 * (s + 1)] = _dot(mn, w_ref[s]).astype(BF16)
            vm_ref[:, 512 * s:512 * (s + 1)] = _dot(mn, w_ref[2 + s]).astype(BF16)

    blk = pl.BlockSpec((MEM_LEN, D), lambda b: (b, 0))
    return pl.pallas_call(
        body, name="mem_kv", grid=(B,),
        in_specs=[blk, _resident(g_mem), _resident(w_mkv)],
        out_specs=[blk, blk, blk],
        out_shape=[_sds((B * MEM_LEN, D), BF16)] * 3,
        compiler_params=_params(1),
    )(mem2, g_mem, w_mkv)


def _mem_probs(qm, km):
    ps = []
    for h in range(MEM_HEADS):
        hs = slice(h * MEM_HD, (h + 1) * MEM_HD)
        lg = _dot_nt(qm[:, hs], km[:, hs]) * (1.0 / math.sqrt(MEM_HD))
        e = jnp.exp(lg - jnp.max(lg, axis=-1, keepdims=True))
        ps.append(e / jnp.sum(e, axis=-1, keepdims=True))
    return ps


def _fwd_mid(x2, co, o, km, vm, w_out, w_mq, w_mo, g_x, B, S, comm=None):
    T = B * S
    TB = min(512, S)
    nb = S // TB

    def body(x_ref, co_ref, o_ref, km_ref, vm_ref, wo_ref, wq_ref, wm_ref, g_ref,
             x1_ref, hx_ref, qm_ref, om_ref, x2_ref, cat_ref):
        cat_ref[:, 0:CONV_CH] = co_ref[...]
        cat_ref[:, CONV_CH:D] = o_ref[...].astype(BF16)
        x1 = x_ref[...] + _dot(cat_ref[...], wo_ref[...])
        x1_ref[...] = x1
        _, xh = _rms(x1)
        hx = (xh * g_ref[...]).astype(BF16)
        hx_ref[...] = hx
        qm = _dot(hx, wq_ref[...]).astype(BF16)
        qm_ref[...] = qm
        ps = _mem_probs(qm, km_ref[...])
        vmv = vm_ref[...]
        for h in range(MEM_HEADS):
            hs = slice(h * MEM_HD, (h + 1) * MEM_HD)
            om_ref[:, hs] = _dot(ps[h].astype(BF16), vmv[:, hs]).astype(BF16)
        x2_ref[...] = x1 + _dot(om_ref[...], wm_ref[...])

    tok = lambda w: pl.BlockSpec((TB, w), lambda b, j: (b * nb + j, 0))
    memb = pl.BlockSpec((MEM_LEN, D), lambda b, j: (b, 0))
    outs = [(D, F32), (D, BF16), (D, BF16), (D, BF16), (D, F32), (D, BF16)]
    return _call(
        body, comm, name="fwd_mid", grid=(B, nb),
        in_specs=[tok(D), tok(CONV_CH), tok(FOX_W), memb, memb, _resident(w_out), _resident(w_mq), _resident(w_mo),
                  _resident(g_x)],
        out_specs=[tok(w) for w, _ in outs],
        out_shape=[_sds((T, w), dt) for w, dt in outs],
        scratch_shapes=[],
        args=(x2, co, o, km, vm, w_out, w_mq, w_mo, g_x))


def _fwd_ffn(x2, tgt, w_gu, w_down, g_ffn, g_final, T):
    TB = min(256, T)
    nb = T // TB

    def body(x_ref, t_ref, wgu_ref, wd_ref, gf_ref, gl_ref, hf_ref, gu_ref, act_ref, dx3_ref, loss_ref, dgl_ref):
        i = pl.program_id(0)

        @pl.when(i == 0)
        def _():
            loss_ref[...] = jnp.zeros_like(loss_ref)
            dgl_ref[...] = jnp.zeros_like(dgl_ref)

        x2v = x_ref[...]
        _, xh = _rms(x2v)
        hf = (xh * gf_ref[...]).astype(BF16)
        hf_ref[...] = hf
        x3 = x2v
        for ch in range(D_FF // FF_CHUNK):
            c0 = ch * FF_CHUNK
            g = _dot(hf, wgu_ref[ch])
            u = _dot(hf, wgu_ref[2 + ch])
            gu_ref[:, c0:c0 + FF_CHUNK] = g
            gu_ref[:, D_FF + c0:D_FF + c0 + FF_CHUNK] = u
            act = (g * _sig(g) * u).astype(BF16)
            act_ref[:, c0:c0 + FF_CHUNK] = act
            x3 = x3 + _dot(act, wd_ref[c0:c0 + FF_CHUNK, :])
        r3, xh3 = _rms(x3)
        gl = gl_ref[...]
        e = xh3 * gl - t_ref[...]
        loss_ref[...] += jnp.sum(e * e) * (0.5 / D)
        dy = e * (1.0 / D)
        dx3, dgl = _rms_bwd(dy, xh3, r3, gl)
        dx3_ref[...] = dx3
        dgl_ref[...] += jnp.sum(dgl, axis=0, keepdims=True)

    tok = lambda w: pl.BlockSpec((TB, w), lambda i: (i, 0))
    return pl.pallas_call(
        body, name="fwd_ffn", grid=(nb,),
        in_specs=[tok(D), tok(D), _resident(w_gu), _resident(w_down), _resident(g_ffn), _resident(g_final)],
        out_specs=[tok(D), tok(2 * D_FF), tok(D_FF), tok(D), _acc_spec((1, LANES)), _acc_spec((1, D))],
        out_shape=[_sds((T, D), BF16), _sds((T, 2 * D_FF), F32), _sds((T, D_FF), BF16), _sds((T, D), F32),
                   _sds((1, LANES), F32), _sds((1, D), F32)],
        compiler_params=_params(1),
    )(x2, tgt, w_gu, w_down, g_ffn, g_final)


def _bwd_ffn(dx3, gu, x2, w_gu, w_down, g_ffn, T):
    TB = min(256, T)
    nb = T // TB

    def body(d_ref, gu_ref, x_ref, wgu_ref, wd_ref, gf_ref, dgu_ref, dx2_ref, dgf_ref):
        i = pl.program_id(0)

        @pl.when(i == 0)
        def _():
            dgf_ref[...] = jnp.zeros_like(dgf_ref)

        dx3v = d_ref[...]
        db = dx3v.astype(BF16)
        dhf = jnp.zeros((TB, D), F32)
        for ch in range(D_FF // FF_CHUNK):
            c0 = ch * FF_CHUNK
            dact = _dot_nt(db, wd_ref[c0:c0 + FF_CHUNK, :])
            g = gu_ref[:, c0:c0 + FF_CHUNK]
            u = gu_ref[:, D_FF + c0:D_FF + c0 + FF_CHUNK]
            sg = _sig(g)
            dg = (dact * u * sg * (1.0 + g * (1.0 - sg))).astype(BF16)
            du = (dact * g * sg).astype(BF16)
            dgu_ref[:, c0:c0 + FF_CHUNK] = dg
            dgu_ref[:, D_FF + c0:D_FF + c0 + FF_CHUNK] = du
            dhf = dhf + _dot_nt(dg, wgu_ref[ch]) + _dot_nt(du, wgu_ref[2 + ch])
        r2, xh2 = _rms(x_ref[...])
        dx, dg_tok = _rms_bwd(dhf, xh2, r2, gf_ref[...])
        dx2_ref[...] = dx3v + dx
        dgf_ref[...] += jnp.sum(dg_tok, axis=0, keepdims=True)

    tok = lambda w: pl.BlockSpec((TB, w), lambda i: (i, 0))
    return pl.pallas_call(
        body, name="bwd_ffn", grid=(nb,),
        in_specs=[tok(D), tok(2 * D_FF), tok(D), _resident(w_gu), _resident(w_down), _resident(g_ffn)],
        out_specs=[tok(2 * D_FF), tok(D), _acc_spec((1, D))],
        out_shape=[_sds((T, 2 * D_FF), BF16), _sds((T, D), F32), _sds((1, D), F32)],
        compiler_params=_params(1),
    )(dx3, gu, x2, w_gu, w_down, g_ffn)


def _bwd_mid(dx2, x1, qm, km, vm, o, w_mo, w_mq, w_out, g_x, B, S, comm=None):
    T = B * S
    TB = min(512, S)
    nb = S // TB
    inv = 1.0 / math.sqrt(MEM_HD)

    def body(d_ref, x1_ref, qm_ref, km_ref, vm_ref, o_ref, wm_ref, wq_ref, wo_ref, g_ref,
             dx1_ref, dqm_ref, dco_ref, do_ref, dd_ref, dkm_ref, dvm_ref, dgx_ref):
        b = pl.program_id(0)
        j = pl.program_id(1)

        @pl.when((b == 0) & (j == 0))
        def _():
            dgx_ref[...] = jnp.zeros_like(dgx_ref)

        @pl.when(j == 0)
        def _():
            dkm_ref[...] = jnp.zeros_like(dkm_ref)
            dvm_ref[...] = jnp.zeros_like(dvm_ref)

        dx2v = d_ref[...]
        dom = _dot_nt(dx2v.astype(BF16), wm_ref[...]).astype(BF16)
        qmv = qm_ref[...]
        kmv = km_ref[...]
        vmv = vm_ref[...]
        ps = _mem_probs(qmv, kmv)
        for h in range(MEM_HEADS):
            hs = slice(h * MEM_HD, (h + 1) * MEM_HD)
            p = ps[h]
            dp = _dot_nt(dom[:, hs], vmv[:, hs])
            ds = (p * (dp - jnp.sum(p * dp, axis=-1, keepdims=True))).astype(BF16)
            dqm_ref[:, hs] = (_dot(ds, kmv[:, hs]) * inv).astype(BF16)
            dkm_ref[:, hs] += _dot_tn(ds, qmv[:, hs]) * inv
            dvm_ref[:, hs] += _dot_tn(p.astype(BF16), dom[:, hs])
        dhx = _dot_nt(dqm_ref[...], wq_ref[...])
        r1, xh1 = _rms(x1_ref[...])
        dx, dg_tok = _rms_bwd(dhx, xh1, r1, g_ref[...])
        dx1 = dx2v + dx
        dx1_ref[...] = dx1
        dgx_ref[...] += jnp.sum(dg_tok, axis=0, keepdims=True)
        d1b = dx1.astype(BF16)
        dco_ref[...] = _dot_nt(d1b, wo_ref[0:CONV_CH, :])
        do = _dot_nt(d1b, wo_ref[CONV_CH:D, :])
        dob = do.astype(BF16)
        do_ref[...] = dob
        dd_ref[...] = _dot_01(dob.astype(F32) * o_ref[...], _head_sum(FOX_W))

    tok = lambda w: pl.BlockSpec((TB, w), lambda b, j: (b * nb + j, 0))
    memb = pl.BlockSpec((MEM_LEN, D), lambda b, j: (b, 0))
    outs = [(D, F32), (D, BF16), (CONV_CH, F32), (FOX_W, BF16), (FOX_W, F32)]
    return _call(
        body, comm, name="bwd_mid", grid=(B, nb),
        in_specs=[tok(D), tok(D), tok(D), memb, memb, tok(FOX_W), _resident(w_mo), _resident(w_mq), _resident(w_out),
                  _resident(g_x)],
        out_specs=[tok(w) for w, _ in outs] + [memb, memb, _acc_spec((1, D))],
        out_shape=[_sds((T, w), dt) for w, dt in outs] + [_sds((B * MEM_LEN, D), F32)] * 2 + [_sds((1, D), F32)],
        scratch_shapes=[],
        args=(dx2, x1, qm, km, vm, o, w_mo, w_mq, w_out, g_x))


def _mem_bwd(dkm, dvm, mem2, w_mkv, g_mem, B):
    def body(dk_ref, dv_ref, m_ref, w_ref, g_ref, dkv_ref, dg_ref):
        b = pl.program_id(0)

        @pl.when(b == 0)
        def _():
            dg_ref[...] = jnp.zeros_like(dg_ref)

        dk = dk_ref[...].astype(BF16)
        dv = dv_ref[...].astype(BF16)
        dkv_ref[:, 0:D] = dk
        dkv_ref[:, D:2 * D] = dv
        dmn = jnp.zeros((MEM_LEN, D), F32)
        for s in range(2):
            dmn = dmn + _dot_nt(dk[:, 512 * s:512 * (s + 1)], w_ref[s]) + _dot_nt(dv[:, 512 * s:512 * (s + 1)], w_ref[2 + s])
        _, xh = _rms(m_ref[...])
        dg_ref[...] += jnp.sum(dmn * xh, axis=0, keepdims=True)

    blk = pl.BlockSpec((MEM_LEN, D), lambda b: (b, 0))
    return pl.pallas_call(
        body, name="mem_bwd", grid=(B,),
        in_specs=[blk, blk, blk, _resident(w_mkv), _resident(g_mem)],
        out_specs=[pl.BlockSpec((MEM_LEN, 2 * D), lambda b: (b, 0)), _acc_spec((1, D))],
        out_shape=[_sds((B * MEM_LEN, 2 * D), BF16), _sds((1, D), F32)],
        compiler_params=_params(1),
    )(dkm, dvm, mem2, w_mkv, g_mem)


def _fox_bwd(q, k, v, do, bias, dd, ckT, qT, doT, B, S, comm=None):
    T = B * S
    TK = min(256, S)
    nk = S // TK
    scale = 1.0 / math.sqrt(HEAD_D)

    def body(q_ref, k_ref, v_ref, do_ref, bias_ref, dd_ref, ck_ref, qt_ref, dot_ref, dq_ref, dk_ref, dv_ref, dc_ref,
             dcq_ref, dq_acc, rs_acc, s_scr, dp_scr, s_odd, dp_odd, dk_acc, dv_acc, dc_acc):
        j = pl.program_id(2)

        @pl.when(j == 0)
        def _():
            dq_acc[...] = jnp.zeros_like(dq_acc)
            rs_acc[...] = jnp.zeros_like(rs_acc)

        dk_acc[...] = jnp.zeros_like(dk_acc)
        dv_acc[...] = jnp.zeros_like(dv_acc)
        dc_acc[...] = jnp.zeros_like(dc_acc)
        lane = lax.broadcasted_iota(jnp.int32, (TK, LANES), 1)
        lo = lane < HEAD_D
        ks = k_ref[...] * jnp.asarray(scale, BF16)
        v2 = v_ref[...]
        zero = jnp.zeros_like(ks)
        kh = (jnp.where(lo, ks, zero), jnp.where(lo, zero, ks))
        vh = (jnp.where(lo, v2, zero), jnp.where(lo, zero, v2))
        kstart = pl.multiple_of(j * TK, TK)
        ckh = tuple(ck_ref[0, 0, h:h + 1, pl.ds(kstart, TK)] for h in range(2))
        row = lax.broadcasted_iota(jnp.int32, (TK, TK), 0)
        col = lax.broadcasted_iota(jnp.int32, (TK, TK), 1)
        wide = lambda x: jnp.concatenate([x, x], axis=1) if TK == 2 * LANES else jnp.tile(x, (1, TK // LANES))

        def scores(i, s_buf, dp_buf):
            start = pl.multiple_of(i * TK, TK)
            qi = q_ref[pl.ds(start, TK), :]
            doi = do_ref[pl.ds(start, TK), :]
            for h in range(2):
                s_buf[h] = _dot_nt(qi, kh[h])
                dp_buf[h] = _dot_nt(doi, vh[h])

        def grads(i, s_buf, dp_buf, diagonal):
            start = pl.multiple_of(i * TK, TK)
            bias2 = bias_ref[pl.ds(start, TK), :]
            dd2 = dd_ref[pl.ds(start, TK), :]
            for h in range(2):
                hc = slice(h * HEAD_D, h * HEAD_D + 1)
                bias = jnp.broadcast_to(bias2[:, hc], (TK, LANES))
                ddh = jnp.broadcast_to(dd2[:, hc], (TK, LANES))
                p = jnp.exp((s_buf[h] - ckh[h]) + wide(bias))
                if diagonal:
                    p = jnp.where(col <= row, p, 0.0)
                ds = p * (dp_buf[h] - wide(ddh))
                dc_acc[h, 0:1, :] += jnp.sum(ds, axis=0, keepdims=True)
                rs_acc[h, pl.ds(start, TK), :] += jnp.sum(ds, axis=1, keepdims=True)
                pb = p.astype(BF16)
                dsb = ds.astype(BF16)
                feat = slice(h * HEAD_D, (h + 1) * HEAD_D)
                dv_acc[feat, :] += _dot(dot_ref[0, i, feat, :], pb)
                dk_acc[feat, :] += _dot(qt_ref[0, i, feat, :], dsb)
                dq_acc[pl.ds(start, TK), :] += _dot(dsb, kh[h])

        n_off = nk - 1 - j
        block = lambda t: jnp.where(t < n_off, j + 1 + t, j)

        def two_blocks(tt, carry):
            t = 2 * tt
            scores(block(t + 1), s_odd, dp_odd)
            grads(block(t), s_scr, dp_scr, False)
            scores(block(t + 2), s_scr, dp_scr)
            grads(block(t + 1), s_odd, dp_odd, False)
            return carry

        scores(block(0), s_scr, dp_scr)
        lax.fori_loop(0, n_off // 2, two_blocks, 0)

        @pl.when(n_off % 2 == 0)
        def _():
            grads(j, s_scr, dp_scr, True)

        @pl.when(n_off % 2 == 1)
        def _():
            scores(j, s_odd, dp_odd)
            grads(nk - 1, s_scr, dp_scr, False)
            grads(j, s_odd, dp_odd, True)

        dk_ref[...] = (dk_acc[...].T * scale).astype(BF16)
        dv_ref[...] = dv_acc[...].T.astype(BF16)
        sub = lax.broadcasted_iota(jnp.int32, (8, TK), 0)
        dca = dc_acc[0, 0:1, :]
        dcb = dc_acc[1, 0:1, :]
        dc_ref[0, 0] = jnp.where(sub == 0, -dca, jnp.where(sub == 1, -dcb, 0.0))

        @pl.when(j == nk - 1)
        def _():
            dq_ref[...] = dq_acc[...].astype(BF16)
            lo_s = lax.broadcasted_iota(jnp.int32, (S, LANES), 1) < HEAD_D
            dcq_ref[...] = jnp.where(lo_s, rs_acc[0], rs_acc[1])

    full = pl.BlockSpec((S, LANES), lambda b, p, j: (b, p))
    blk = pl.BlockSpec((TK, LANES), lambda b, p, j: (b * nk + j, p))
    featT = pl.BlockSpec((1, nk, LANES, TK), lambda b, p, j: (b, 0, p, 0))
    return _call(
        body, comm, name="fox_bwd", grid=(B, N_PAIR, nk),
        in_specs=[full, blk, blk, full, full, full, pl.BlockSpec((1, 1, 8, S), lambda b, p, j: (b, p, 0, 0)),
                  featT, featT],
        out_specs=[full, blk, blk, pl.BlockSpec((1, 1, 8, TK), lambda b, p, j: (b, p, 0, j)), full],
        out_shape=[_sds((T, FOX_W), BF16), _sds((T, FOX_W), BF16), _sds((T, FOX_W), BF16),
                   _sds((B, N_PAIR, 8, S), F32), _sds((T, FOX_W), F32)],
        scratch_shapes=[pltpu.VMEM((S, LANES), F32), pltpu.VMEM((2, S, 1), F32),
                        pltpu.VMEM((2, TK, TK), F32), pltpu.VMEM((2, TK, TK), F32),
                        pltpu.VMEM((2, TK, TK), F32), pltpu.VMEM((2, TK, TK), F32),
                        pltpu.VMEM((LANES, TK), F32), pltpu.VMEM((LANES, TK), F32), pltpu.VMEM((2, 8, TK), F32)],
        args=(q, k, v, do, bias, dd, ckT, qT, doT))


def _fgate_bwd(dc8, zf, B, S):
    T = B * S
    TB = min(512, S)
    nb = S // TB

    def body(dc_ref, zf_ref, dzf_ref, dbf_ref, carry):
        b = pl.program_id(0)
        j = pl.program_id(1)

        @pl.when((b == 0) & (j == 0))
        def _():
            dbf_ref[...] = jnp.zeros_like(dbf_ref)

        @pl.when(j == 0)
        def _():
            carry[...] = jnp.zeros_like(carry)

        dc = dc_ref[...]
        row = lax.broadcasted_iota(jnp.int32, (TB, TB), 0)
        col = lax.broadcasted_iota(jnp.int32, (TB, TB), 1)
        dlogf = _dot_01(col >= row, dc) + carry[0:1, :]
        carry[0:1, :] = dlogf[0:1, :]
        lane = lax.broadcasted_iota(jnp.int32, dc.shape, 1)
        dzf = jnp.where(lane < 8, dlogf * _sig(-zf_ref[...]), 0.0)
        dzf_ref[...] = dzf.astype(BF16)
        dbf_ref[...] += jnp.sum(dzf, axis=0, keepdims=True)

    tok = pl.BlockSpec((TB, LANES), lambda b, j: (b * nb + (nb - 1 - j), 0))
    return pl.pallas_call(
        body, name="fgate_bwd", grid=(B, nb),
        in_specs=[tok, tok],
        out_specs=[tok, _acc_spec((1, LANES))],
        out_shape=[_sds((T, LANES), BF16), _sds((1, LANES), F32)],
        scratch_shapes=[pltpu.VMEM((8, LANES), F32)],
        compiler_params=_params(2),
    )(dc8, zf)


def _conv_bwd(dco, y, u, gt, cw, lng, lnb, B, S, comm=None):
    T = B * S
    CB = min(256, S)
    nb = S // CB
    hb = CB // CONV_HALO

    def body(dco_ref, y_ref, u_ref, gt_ref, up_ref, gp_ref, w_ref, lg_ref, lb_ref,
             du_ref, dgt_ref, dw_ref, vec_ref, acat, dycat, ash, dysh):
        b = pl.program_id(0)
        j = pl.program_id(1)
        jr = nb - 1 - j

        @pl.when((b == 0) & (j == 0))
        def _():
            dw_ref[...] = jnp.zeros_like(dw_ref)
            vec_ref[...] = jnp.zeros_like(vec_ref)

        @pl.when(j == 0)
        def _():
            dycat[CB:CB + CONV_HALO, :] = jnp.zeros((CONV_HALO, CONV_CH), F32)

        lg = lg_ref[...]
        rs, n, l = _layernorm_silu(y_ref[...], lg, lb_ref[...])
        sg = _sig(l)
        dl = dco_ref[...] * (sg * (1.0 + l * (1.0 - sg)))
        dn = dl * lg
        dy = rs * (dn - jnp.mean(dn, axis=-1, keepdims=True) - n * jnp.mean(dn * n, axis=-1, keepdims=True))
        vec_ref[0:1, :] += jnp.sum(dy, axis=0, keepdims=True)
        vec_ref[1:2, :] += jnp.sum(dl * n, axis=0, keepdims=True)
        vec_ref[2:3, :] += jnp.sum(dl, axis=0, keepdims=True)
        dycat[0:CB, :] = dy
        acat[0:CONV_HALO, :] = jnp.where(jr > 0, up_ref[...] * _sig(gp_ref[...]), 0.0)
        acat[CONV_HALO:CONV_HALO + CB, :] = u_ref[...] * _sig(gt_ref[...])
        _shifted_copies(acat, ash, CB + CONV_HALO - SUB)
        _shifted_copies(dycat, dysh, CB + CONV_HALO - SUB)
        for r0, rows, cs in _conv_pieces(CB):
            dyp = dycat[r0:r0 + rows, cs]
            da = jnp.zeros((rows, LANES), F32)
            for k in range(CONV_K):
                da = da + w_ref[k:k + 1, cs] * _tap(dycat, dysh, r0 + CONV_K - 1 - k, rows, cs)
                dw_ref[k:k + 1, cs] += jnp.sum(dyp * _tap(acat, ash, r0 + CONV_HALO - (CONV_K - 1) + k, rows, cs),
                                               axis=0, keepdims=True)
            uv = u_ref[r0:r0 + rows, cs]
            sgt = _sig(gt_ref[r0:r0 + rows, cs])
            du_ref[r0:r0 + rows, cs] = (da * sgt).astype(BF16)
            dgt_ref[r0:r0 + rows, cs] = (da * uv * sgt * (1.0 - sgt)).astype(BF16)
        dycat[CB:CB + CONV_HALO, :] = dycat[0:CONV_HALO, :]

    tok = lambda w: pl.BlockSpec((CB, w), lambda b, j: (b * nb + (nb - 1 - j), 0))
    prev = pl.BlockSpec((CONV_HALO, CONV_CH), lambda b, j: (jnp.maximum((b * nb + (nb - 1 - j)) * hb - 1, 0), 0))
    return _call(
        body, comm, name="conv_bwd", grid=(B, nb),
        in_specs=[tok(CONV_CH), tok(CONV_CH), tok(CONV_CH), tok(CONV_CH), prev, prev, _resident(cw), _resident(lng),
                  _resident(lnb)],
        out_specs=[tok(CONV_CH), tok(CONV_CH), _acc_spec((CONV_HALO, CONV_CH)), _acc_spec((8, CONV_CH))],
        out_shape=[_sds((T, CONV_CH), BF16), _sds((T, CONV_CH), BF16), _sds((CONV_HALO, CONV_CH), F32),
                   _sds((8, CONV_CH), F32)],
        scratch_shapes=[pltpu.VMEM((CONV_HALO + CB, CONV_CH), F32), pltpu.VMEM((CB + CONV_HALO, CONV_CH), F32),
                        pltpu.VMEM((SUB, CB + CONV_HALO - SUB, CONV_CH), F32),
                        pltpu.VMEM((SUB, CB + CONV_HALO - SUB, CONV_CH), F32)],
        args=(dco, y, u, gt, u, gt, cw, lng, lnb))


def _bwd_in(dz, w_int, w_ft, x2, dx1, g_mix, T, comm=None):
    TB = min(512, T)
    nb = T // TB

    def body(dz_ref, w_ref, wf_ref, x_ref, d1_ref, g_ref, gx_ref, dg_ref):
        i = pl.program_id(0)

        @pl.when(i == 0)
        def _():
            dg_ref[...] = jnp.zeros_like(dg_ref)

        dh = _dot(dz_ref[:, 0:OFF_F], w_ref[0:OFF_F, :]) + _dot(dz_ref[:, OFF_F:D_IN_PAD], wf_ref[...])
        r0, xh0 = _rms(x_ref[...])
        dx, dg_tok = _rms_bwd(dh, xh0, r0, g_ref[...])
        gx_ref[...] = d1_ref[...] + dx
        dg_ref[...] += jnp.sum(dg_tok, axis=0, keepdims=True)

    tok = lambda w: pl.BlockSpec((TB, w), lambda i: (i, 0))
    return _call(
        body, comm, name="bwd_in", grid=(nb,),
        in_specs=[tok(D_IN_PAD), _resident(w_int), _resident(w_ft), tok(D), tok(D), _resident(g_mix)],
        out_specs=[tok(D), _acc_spec((1, D))],
        out_shape=[_sds((T, D), F32), _sds((1, D), F32)],
        scratch_shapes=[],
        args=(dz, w_int, w_ft, x2, dx1, g_mix))


def _dw(a, b, name, tn, slabs=False, tk=None, rows=None):
    T, K = a.shape
    N = b.shape[1]
    tk = tk or (K if K <= 1024 else K // 2)
    tt = min(1024, T)
    nt = T // tt

    def body(a_ref, b_ref, o_ref, acc):
        t = pl.program_id(2)

        @pl.when(t == 0)
        def _():
            acc[...] = jnp.zeros_like(acc)

        acc[...] += _dot_tn(a_ref[...].astype(BF16), b_ref[...].astype(BF16))

        @pl.when(t == nt - 1)
        def _():
            o_ref[...] = acc[...]

    return pl.pallas_call(
        body, name=name, grid=(K // tk, N // tn, nt),
        in_specs=[pl.BlockSpec((tt, tk), lambda i, j, t: (t, i)), pl.BlockSpec((tt, tn), lambda i, j, t: (t, j))],
        out_specs=(pl.BlockSpec((None, tk, tn), lambda i, j, t: (j, i, 0)) if slabs
                   else pl.BlockSpec((tk, tn), lambda i, j, t: (i, j))),
        out_shape=_sds((N // tn, K, tn) if slabs else (rows or K, N), F32),
        scratch_shapes=[pltpu.VMEM((tk, tn), F32)],
        compiler_params=_params(3),
    )(a, b)


def _pos():
    return lax.axis_index("x"), lax.axis_index("y"), lax.axis_index("c")


def _remote(src, dst, ssem, rsem, to):
    return pltpu.make_async_remote_copy(src_ref=src, dst_ref=dst, send_sem=ssem, recv_sem=rsem, device_id=to,
                                        device_id_type=MESH)


def _split_axis(shape):
    return 0 if shape[0] % 32 == 0 else 1


def _half_shape(shape, parts=2):
    return (shape[0] // parts, shape[1]) if _split_axis(shape) == 0 else (shape[0], shape[1] // parts)


def _half(shape, c):
    R, C = shape
    if _split_axis(shape) == 0:
        return (pl.ds(pl.multiple_of(c * (R // 2), 16), R // 2), slice(None))
    return (slice(None), pl.ds(pl.multiple_of(c * (C // 2), LANES), C // 2))


def _half_block(shape, parts, lead, which):
    blk = _half_shape(shape, parts)
    idx = (which, 0) if _split_axis(shape) == 0 else (0, which)
    return blk, tuple(lead) + idx


class _Comm:
    def __init__(self, ins, out_shapes, sems, start, finish):
        self.ins, self.out_shapes, self.sems, self.start, self.finish = list(ins), list(out_shapes), list(sems), start, finish


def _ag_comm(shards):
    n = len(shards)

    def parts(ins, outs, sems):
        send_sems, recv_sems, local_sems = sems
        x, y, c = _pos()
        me, sib = (x, y, c), (x, y, 1 - c)
        chips = [(1 - x, y), (x, 1 - y), (1 - x, 1 - y)]

        def rows(w, px, py, pc):
            return outs[w].at[(2 * px + py,) + _half(shards[w].shape, pc)]

        def copy(w, k, block, to, src=None):
            return _remote(rows(w, *block) if src is None else src, rows(w, *block), send_sems.at[w, k],
                           recv_sems.at[w, k], to)

        mine, first = [], []
        for w in range(n):
            src = ins[w].at[_half(shards[w].shape, c)]
            mine.append(pltpu.make_async_copy(src, rows(w, *me), local_sems.at[w]))
            first += [copy(w, 0, me, sib, src=src)] + [copy(w, 1 + j, me, (*chip, c), src=src) for j, chip in enumerate(chips)]
        return c, me, sib, chips, copy, mine, first

    def start(ins, outs, sems):
        _, _, _, _, _, mine, first = parts(ins, outs, sems)
        for cp in mine + first:
            cp.start()

    def finish(ins, outs, sems):
        c, me, sib, chips, copy, mine, first = parts(ins, outs, sems)
        passed = []
        for w in range(n):
            for j, chip in enumerate(chips):
                copy(w, 1 + j, (*chip, c), me).wait_recv()
                passed.append(copy(w, 4 + j, (*chip, c), sib))
                passed[-1].start()
        for w in range(n):
            copy(w, 0, sib, me).wait_recv()
            for j, chip in enumerate(chips):
                copy(w, 4 + j, (*chip, 1 - c), me).wait_recv()
        for cp in first + passed:
            cp.wait_send()
        for cp in mine:
            cp.wait()

    D7 = pltpu.SemaphoreType.DMA((n, 7))
    return _Comm(shards, [_sds((4,) + s.shape, s.dtype) for s in shards], [D7, D7, pltpu.SemaphoreType.DMA((n,))],
                 start, finish)


def _sibling_comm(gs):
    n = len(gs)

    def copies(ins, outs, sems):
        send_sems, recv_sems = sems
        x, y, c = _pos()
        return [_remote(ins[w].at[(s,) + _half(gs[w].shape[1:], 1 - c)], outs[w].at[s], send_sems.at[w, s],
                        recv_sems.at[w, s], (x, y, 1 - c)) for w in range(n) for s in range(4)]

    def start(ins, outs, sems):
        for cp in copies(ins, outs, sems):
            cp.start()

    def finish(ins, outs, sems):
        for cp in copies(ins, outs, sems):
            cp.wait()

    D4 = pltpu.SemaphoreType.DMA((n, 4))
    return _Comm(gs, [_sds((4,) + _half_shape(g.shape[1:]), F32) for g in gs], [D4, D4], start, finish)


def _ici_comm(pbs):
    n = len(pbs)

    def copies(ins, outs, sems):
        send_sems, recv_sems = sems
        x, y, c = _pos()
        return [_remote(ins[w].at[2 * tx + ty], outs[w].at[j], send_sems.at[w, j], recv_sems.at[w, j], (tx, ty, c))
                for w in range(n) for j, (tx, ty) in enumerate([(1 - x, y), (x, 1 - y), (1 - x, 1 - y)])]

    def start(ins, outs, sems):
        for cp in copies(ins, outs, sems):
            cp.start()

    def finish(ins, outs, sems):
        for cp in copies(ins, outs, sems):
            cp.wait()

    D3 = pltpu.SemaphoreType.DMA((n, 3))
    return _Comm(pbs, [_sds((3,) + p.shape[1:], BF16) for p in pbs], [D3, D3], start, finish)


def _join(*comms):
    counts = [(len(c.ins), len(c.out_shapes), len(c.sems)) for c in comms]

    def each(which):
        def run(ins, outs, sems):
            i = o = k = 0
            for c, (ni, no, nk) in zip(comms, counts):
                getattr(c, which)(ins[i:i + ni], outs[o:o + no], sems[k:k + nk])
                i, o, k = i + ni, o + no, k + nk
        return run

    return _Comm(sum((c.ins for c in comms), []), sum((c.out_shapes for c in comms), []),
                 sum((c.sems for c in comms), []), each("start"), each("finish"))


def _run_comm(comm, name):
    ni, no = len(comm.ins), len(comm.out_shapes)

    def body(*refs):
        ins, outs, sems = refs[:ni], refs[ni:ni + no], refs[ni + no:]
        comm.start(ins, outs, sems)
        comm.finish(ins, outs, sems)

    return pl.pallas_call(body, name=name, out_shape=comm.out_shapes, in_specs=[ANY] * ni, out_specs=[ANY] * no,
                          scratch_shapes=comm.sems)(*comm.ins)


def _call(body, comm, *, name, grid, in_specs, out_specs, out_shape, scratch_shapes, args):
    n_grid = len(grid)
    if comm is None:
        res = pl.pallas_call(body, name=name, grid=grid, in_specs=in_specs, out_specs=out_specs, out_shape=out_shape,
                             scratch_shapes=scratch_shapes, compiler_params=_params(n_grid))(*args)
        return list(res), []
    n_in, n_out, n_scr = len(in_specs), len(out_specs), len(scratch_shapes)
    ni, no = len(comm.ins), len(comm.out_shapes)

    def carried(*refs):
        ins, refs = refs[:n_in], refs[n_in:]
        cins, refs = refs[:ni], refs[ni:]
        outs, refs = refs[:n_out], refs[n_out:]
        couts, refs = refs[:no], refs[no:]
        scr, csems = refs[:n_scr], refs[n_scr:]
        ids = [pl.program_id(ax) for ax in range(n_grid)]
        first = functools.reduce(jnp.logical_and, [i == 0 for i in ids])
        last = functools.reduce(jnp.logical_and, [i == g - 1 for i, g in zip(ids, grid)])

        @pl.when(first)
        def _():
            comm.start(cins, couts, csems)

        body(*ins, *outs, *scr)

        @pl.when(last)
        def _():
            comm.finish(cins, couts, csems)

    res = pl.pallas_call(
        carried, name=name, grid=grid, in_specs=list(in_specs) + [ANY] * ni, out_specs=list(out_specs) + [ANY] * no,
        out_shape=list(out_shape) + comm.out_shapes, scratch_shapes=list(scratch_shapes) + comm.sems,
        compiler_params=_params(n_grid))(*args, *comm.ins)
    return list(res[:n_out]), list(res[n_out:])


def _small_allreduce(v, name, halves=()):
    P = v.shape[0]
    n = len(halves)
    vm = pl.BlockSpec(memory_space=pltpu.VMEM)

    def body(v_ref, *refs):
        o_ref, outs = refs[n], refs[n + 1:2 * n + 1]
        gath, send_sems, recv_sems, half_send, half_recv = refs[2 * n + 1:]
        x, y, c = _pos()
        me = 4 * x + 2 * y + c
        gath[me] = v_ref[...]
        cps = []
        for r in range(1, 8):
            tx = (1 - x) if r & 4 else x
            ty = (1 - y) if r & 2 else y
            tc = (1 - c) if r & 1 else c
            cps.append(_remote(v_ref, gath.at[me], send_sems.at[r - 1], recv_sems.at[r - 1], (tx, ty, tc)))
        for w in range(n):
            mine = outs[w].at[_half(halves[w].shape, c)]
            cps.append(_remote(mine, mine, half_send.at[w], half_recv.at[w], (x, y, 1 - c)))
        for cp in cps:
            cp.start()
        for cp in cps:
            cp.wait()
        acc = gath[0]
        for d in range(1, 8):
            acc = acc + gath[d]
        o_ref[...] = acc

    res = pl.pallas_call(
        body, name=name, out_shape=[_sds((P, LANES), F32)] + [_sds(g.shape, F32) for g in halves],
        in_specs=[vm] + [ANY] * n, out_specs=[vm] + [ANY] * n, input_output_aliases={1 + w: 1 + w for w in range(n)},
        scratch_shapes=[pltpu.VMEM((8, P, LANES), F32), pltpu.SemaphoreType.DMA((7,)), pltpu.SemaphoreType.DMA((7,)),
                        pltpu.SemaphoreType.DMA((max(n, 1),)), pltpu.SemaphoreType.DMA((max(n, 1),))],
    )(v, *halves)
    return res[0], list(res[1:])


def _chip_sum(g, rcv, pos, name):
    shard = g.shape[1:]
    hs = _half_shape(shard)

    def body(pos_ref, g_ref, r_ref, o_ref):
        o_ref[...] = (g_ref[...] + r_ref[...]).astype(BF16)

    return pl.pallas_call(
        body, name=name, out_shape=_sds((4,) + hs, BF16),
        grid_spec=pltpu.PrefetchScalarGridSpec(
            num_scalar_prefetch=1, grid=(4,),
            in_specs=[pl.BlockSpec((1,) + hs, lambda s, pos: _half_block(shard, 2, (s,), pos[0])[1]),
                      pl.BlockSpec((1,) + hs, lambda s, pos: (s, 0, 0))],
            out_specs=pl.BlockSpec((1,) + hs, lambda s, pos: (s, 0, 0))),
        compiler_params=_params(1),
    )(pos, g, rcv)


def _final_sum(g, rcv, rc, pos, name):
    shard = g.shape[1:]
    qs = _half_shape(shard, 4)

    def body(pos_ref, g_ref, r_ref, rc_ref, o_ref):
        acc = g_ref[0] + r_ref[0]
        for j in range(3):
            acc = acc + rc_ref[j].astype(F32)
        o_ref[...] = acc

    return pl.pallas_call(
        body, name=name, out_shape=_sds(shard, F32),
        grid_spec=pltpu.PrefetchScalarGridSpec(
            num_scalar_prefetch=1, grid=(2,),
            in_specs=[pl.BlockSpec((1,) + qs, lambda i, pos: _half_block(shard, 4, (pos[1],), pos[0] * 2 + i)[1]),
                      pl.BlockSpec((1,) + qs, lambda i, pos: _half_block(shard, 4, (pos[1],), i)[1]),
                      pl.BlockSpec((3,) + qs, lambda i, pos: _half_block(shard, 4, (0,), i)[1])],
            out_specs=pl.BlockSpec(qs, lambda i, pos: _half_block(shard, 4, (), pos[0] * 2 + i)[1])),
        compiler_params=_params(1),
    )(pos, g, rcv, rc)


def _adamw_math(w, g, m, v):
    m = ADAM_B1 * m + (1.0 - ADAM_B1) * g
    v = ADAM_B2 * v + (1.0 - ADAM_B2) * (g * g)
    m_hat = m / (1.0 - ADAM_B1 ** ADAM_STEP)
    v_hat = v / (1.0 - ADAM_B2 ** ADAM_STEP)
    delta = -ADAM_LR * (m_hat / (jnp.sqrt(v_hat) + ADAM_EPS) + ADAM_WD * w)
    return delta, m, v


def _adamw(w, g, m, v, name, blk_shape):
    R, C = w.shape

    def body(w_ref, g_ref, m_ref, v_ref, go_ref, d_ref, nm_ref, nv_ref):
        g = g_ref[...]
        d, nm, nv = _adamw_math(w_ref[...], g, m_ref[...], v_ref[...])
        go_ref[...] = g
        d_ref[...] = d
        nm_ref[...] = nm
        nv_ref[...] = nv

    blk = pl.BlockSpec(blk_shape, lambda i, j: (i, j))
    return pl.pallas_call(
        body, name=name, grid=(R // blk_shape[0], C // blk_shape[1]), in_specs=[blk] * 4, out_specs=[blk] * 4,
        out_shape=[_sds((R, C), F32)] * 4, compiler_params=_params(2),
    )(w, g, m, v)


SMALL = (("g_mix", 8), ("b_f", 8), ("conv_w", None), ("conv_b", 8), ("ln_g", 8), ("ln_b", 8), ("g_x", 8), ("g_mem", 8),
         ("g_ffn", 8), ("g_final", 8), ("loss", 8))


def _pack_small(parts, conv_rows):
    rows = []
    for name, n in SMALL:
        if name not in parts:
            continue
        n = conv_rows if n is None else n
        flat = parts[name].reshape(-1).astype(F32)
        flat = jnp.pad(flat, (0, n * LANES - flat.shape[0]))
        rows.append(flat.reshape(n, LANES))
    return jnp.concatenate(rows, axis=0)


def _unpack_small(p, shapes, conv_rows):
    out, off = {}, 0
    for name, n in SMALL:
        if name not in shapes:
            continue
        n = conv_rows if n is None else n
        size = math.prod(shapes[name])
        out[name] = p[off:off + n].reshape(-1)[:size].reshape(shapes[name])
        off += n
    return out


def kernel(x, mem, g_mix, w_in, b_f, conv_w, conv_b, ln_g, ln_b, w_out, g_x, g_mem, w_mq, w_mkv, w_mo, g_ffn, w_gu, w_down, g_final, loss_target, m_g_mix, m_w_in, m_b_f, m_conv_w, m_conv_b, m_ln_g, m_ln_b, m_w_out, m_g_x, m_g_mem, m_w_mq, m_w_mkv, m_w_mo, m_g_ffn, m_w_gu, m_w_down, m_g_final, v_g_mix, v_w_in, v_b_f, v_conv_w, v_conv_b, v_ln_g, v_ln_b, v_w_out, v_g_x, v_g_mem, v_w_mq, v_w_mkv, v_w_mo, v_g_ffn, v_w_gu, v_w_down, v_g_final):
    names = ["g_mix", "w_in", "b_f", "conv_w", "conv_b", "ln_g", "ln_b", "w_out", "g_x", "g_mem", "w_mq", "w_mkv",
             "w_mo", "g_ffn", "w_gu", "w_down", "g_final"]
    W = dict(zip(names, [g_mix, w_in, b_f, conv_w, conv_b, ln_g, ln_b, w_out, g_x, g_mem, w_mq, w_mkv, w_mo, g_ffn,
                         w_gu, w_down, g_final]))
    M = dict(zip(names, [m_g_mix, m_w_in, m_b_f, m_conv_w, m_conv_b, m_ln_g, m_ln_b, m_w_out, m_g_x, m_g_mem, m_w_mq,
                         m_w_mkv, m_w_mo, m_g_ffn, m_w_gu, m_w_down, m_g_final]))
    V = dict(zip(names, [v_g_mix, v_w_in, v_b_f, v_conv_w, v_conv_b, v_ln_g, v_ln_b, v_w_out, v_g_x, v_g_mem, v_w_mq,
                         v_w_mkv, v_w_mo, v_g_ffn, v_w_gu, v_w_down, v_g_final]))
    big_names = [n for n, _, _, _ in BIG]
    B, S, _ = x.shape
    T = B * S
    mx, my, mc = _pos()
    chip = 2 * mx + my
    pos = jnp.stack([mc, chip]).astype(jnp.int32)

    shard2d = lambda a: a.reshape(a.shape[-2], a.shape[-1])
    big2d = lambda d, n: shard2d(d[n]).T if n == "w_in" else shard2d(d[n])
    shard_bf = {n: big2d(W, n).astype(BF16) for n in big_names}
    ag_mid = ["w_mkv", "w_out", "w_mq", "w_mo"]
    ag_ffn = ["w_gu", "w_down"]
    cw_mine = jnp.pad(shard2d(conv_w), ((0, 1), (0, 0)))
    w_in_slab, cw_slab = _run_comm(_ag_comm([shard_bf["w_in"], cw_mine]), "ag_w_in")
    slab = {"w_in": w_in_slab}
    w_int = w_in_slab.reshape(D_IN, D)
    w_ft = jnp.pad(w_int[OFF_F:D_IN], ((0, D_IN_PAD - D_IN), (0, 0)))
    cw = jnp.transpose(cw_slab, (1, 0, 2)).reshape(CONV_HALO, CONV_CH)

    row = lambda a: a.reshape(1, -1)
    bf_pad = jnp.pad(row(b_f), ((0, 0), (0, LANES - 8)))
    x2d = x.reshape(T, D)
    mem2d = mem.reshape(B * MEM_LEN, D)
    tgt = loss_target.reshape(T, D)

    (h, u, gt, q, k, v, zf, c, cq, qx, kx), got = _fwd_in(x2d, row(g_mix), w_int, w_ft, bf_pad, B, S,
                                                  comm=_ag_comm([shard_bf[n] for n in ag_mid[:2]]))
    slab.update(zip(ag_mid[:2], got))
    ckT = jnp.transpose(c.reshape(B, S, LANES)[:, :, :8], (0, 2, 1)).reshape(B, N_PAIR, 2, S)
    ckT = jnp.pad(ckT, ((0, 0), (0, 0), (0, 6), (0, 0)))
    (y, co), got = _conv_fwd(u, gt, cw, row(conv_b), row(ln_g), row(ln_b), B, S,
                             comm=_ag_comm([shard_bf[n] for n in ag_mid[2:]]))
    slab.update(zip(ag_mid[2:], got))
    (o, fox_bias), got = _fox_fwd(qx, kx, v, cq, B, S, comm=_ag_comm([shard_bf[n] for n in ag_ffn]))
    slab.update(zip(ag_ffn, got))
    full = {n: slab[n] if by_col else slab[n].reshape(4 * r, c) for n, r, c, by_col in BIG}
    mn, km, vm = _mem_kv(mem2d, row(g_mem), full["w_mkv"], B)
    (x1, hx, qm, om, x2, cat), _ = _fwd_mid(x2d, co, o, km, vm, full["w_out"], full["w_mq"], full["w_mo"], row(g_x), B, S)
    hf, gu, act, dx3, loss_p, dg_final = _fwd_ffn(x2, tgt, full["w_gu"], full["w_down"], row(g_ffn), row(g_final), T)

    pos_sum = lambda gs, rcvs, ns: [_chip_sum(g, r, pos, "rs_chip_sum_" + n) for g, r, n in zip(gs, rcvs, ns)]
    fin_sum = lambda gs, rcvs, rcs, ns: [_final_sum(g, r, q3, pos, "rs_final_sum_" + n)
                                         for g, r, q3, n in zip(gs, rcvs, rcs, ns)]
    RH = {}
    dgu, dx2, dg_ffn = _bwd_ffn(dx3, gu, x2, full["w_gu"], full["w_down"], row(g_ffn), T)
    g_ffn_w = [_dw(hf, dgu, "dw_gu", FF_CHUNK, slabs=True), _dw(act, dx3, "dw_down", 512).reshape(4, D_FF // 4, D)]
    (dx1, dqm, dco, do, dd, dkm, dvm, dg_x), rcv_ffn = _bwd_mid(dx2, x1, qm, km, vm, o, full["w_mo"], full["w_mq"],
                                                                full["w_out"], row(g_x), B, S, comm=_sibling_comm(g_ffn_w))
    pb_ffn = pos_sum(g_ffn_w, rcv_ffn, ag_ffn)
    dkv, dg_mem = _mem_bwd(dkm, dvm, mem2d, full["w_mkv"], row(g_mem), B)
    g_mid_w = [_dw(mn, dkv, "dw_mkv", 512, slabs=True), _dw(cat, dx1, "dw_out", 512).reshape(4, 256, D),
               _dw(hx, dqm, "dw_mq", 512).reshape(4, 256, D), _dw(om, dx2, "dw_mo", 512).reshape(4, 256, D)]
    featT = lambda a: jnp.transpose(a.reshape(B, S // min(256, S), min(256, S), FOX_W), (0, 1, 3, 2))
    (dq, dk, dv, dc, dcq), got = _fox_bwd(q, k, v, do, fox_bias, dd, ckT, featT(q), featT(do), B, S,
                                          comm=_join(_ici_comm(pb_ffn), _sibling_comm(g_mid_w)))
    rc_ffn, rcv_mid = got[:len(pb_ffn)], got[len(pb_ffn):]
    RH.update(zip(ag_ffn, fin_sum(g_ffn_w, rcv_ffn, rc_ffn, ag_ffn)))
    pb_mid = pos_sum(g_mid_w, rcv_mid, ag_mid)
    dc8 = jnp.transpose(dc[:, :, :2, :].reshape(B, 8, S), (0, 2, 1)).reshape(T, 8)
    dc8 = dc8 + dcq.reshape(T, 8, HEAD_D)[:, :, 0]
    dzf, dbf = _fgate_bwd(jnp.pad(dc8, ((0, 0), (0, LANES - 8))), zf, B, S)
    (du, dgt, dcw, dvec), rc_mid = _conv_bwd(dco, y, u, gt, cw, row(ln_g), row(ln_b), B, S, comm=_ici_comm(pb_mid))
    RH.update(zip(ag_mid, fin_sum(g_mid_w, rcv_mid, rc_mid, ag_mid)))
    dz = jnp.concatenate([du, dgt, dq, dk, dv, dzf], axis=1)
    g_in_w = [_dw(dz, h, "dw_in", 512, tk=D_IN_PAD // 3, rows=D_IN).reshape(4, D_IN // 4, D)]
    rcv_in = _run_comm(_sibling_comm(g_in_w), "rs_sibling_in")
    (grad_x, dg_mix), rc_in = _bwd_in(dz, w_int, w_ft, x2d, dx1, row(g_mix), T,
                                      comm=_ici_comm(pos_sum(g_in_w, rcv_in, ["w_in"])))
    RH.update(zip(["w_in"], fin_sum(g_in_w, rcv_in, rc_in, ["w_in"])))

    small_g = {"g_mix": dg_mix, "b_f": dbf[:, :8], "conv_w": dcw, "conv_b": dvec[0], "ln_g": dvec[1], "ln_b": dvec[2],
               "g_x": dg_x, "g_mem": dg_mem, "g_ffn": dg_ffn, "g_final": dg_final, "loss": loss_p[:, :1]}
    sg, filled = _small_allreduce(_pack_small(small_g, CONV_HALO * 4), "allreduce_small", [RH[n] for n in big_names])
    shared = dict(zip(big_names, filled))
    G, DL, NM, NV = {}, {}, {}, {}
    for n in big_names:
        G[n], DL[n], NM[n], NV[n] = _adamw(big2d(W, n), shared[n], big2d(M, n), big2d(V, n), "adamw_" + n,
                                           _half_shape(shared[n].shape))
    shapes = {n: W[n].shape for n in names if n not in big_names}
    shapes["conv_w"] = (CONV_HALO, CONV_CH)
    shapes["loss"] = (1,)
    sgrads = _unpack_small(sg, shapes, CONV_HALO * 4)
    loss = sgrads.pop("loss")[0]
    sgrads["conv_w"] = lax.dynamic_slice(sgrads["conv_w"], (0, chip * LANES), (CONV_K, LANES)).reshape(W["conv_w"].shape)
    spack = lambda d: _pack_small({n: d[n] for n in sgrads}, CONV_HALO)
    _, sd, snm, snv = _adamw(spack(W), spack(sgrads), spack(M), spack(V), "adamw_small", (8, LANES))
    sshapes = {n: W[n].shape for n in sgrads}
    SD, SNM, SNV = (_unpack_small(a, sshapes, CONV_HALO) for a in (sd, snm, snv))

    def collect(bigs, smalls):
        back = lambda n: (bigs[n].T if n == "w_in" else bigs[n]).reshape(W[n].shape)
        return [back(n) if n in big_names else smalls[n] for n in names]

    return (loss, grad_x.reshape(x.shape), *collect(G, sgrads), *collect(DL, SD), *collect(NM, SNM), *collect(NV, SNV))
```

```python
import functools
import math

import jax
import jax.numpy as jnp
from jax import lax
from jax.experimental import pallas as pl
from jax.experimental.pallas import tpu as pltpu

F32, BF16 = jnp.float32, jnp.bfloat16
MESH = pl.DeviceIdType.MESH

D = 1024
CONV_CH = 512
CONV_K = 31
CONV_HALO = 32
FOX_W = 512
HEAD_D = 64
N_PAIR = 4
MEM_LEN = 256
MEM_HEADS = 4
MEM_HD = 256
D_FF = 2816
FF_CHUNK = 1408
D_IN = 2568
D_IN_PAD = 2688
OFF_F = 2560
EPS = 1e-6
LANES = 128

ADAM_LR, ADAM_B1, ADAM_B2, ADAM_EPS, ADAM_WD, ADAM_STEP = 0.001, 0.9, 0.999, 1e-08, 0.01, 10

VMEM_LIMIT = 60 * 1024 * 1024

BIG = (("w_out", 256, 1024, False), ("w_mq", 256, 1024, False), ("w_mkv", 1024, 512, True),
       ("w_mo", 256, 1024, False), ("w_gu", 1024, 1408, True), ("w_down", 704, 1024, False),
       ("w_in", 642, 1024, False))

ANY = pl.BlockSpec(memory_space=pl.ANY)


def _sig(x):
    return 1.0 / (1.0 + jnp.exp(-x))


def _dot(a, b):
    return jnp.dot(a, b, preferred_element_type=F32)


def _dot_nt(a, b):
    return lax.dot_general(a, b, (((1,), (1,)), ((), ())), preferred_element_type=F32)


def _dot_tn(a, b):
    return lax.dot_general(a, b, (((0,), (0,)), ((), ())), preferred_element_type=F32)


def _split3(x):
    hi = x.astype(BF16)
    r = x - hi.astype(F32)
    mid = r.astype(BF16)
    return hi, mid, (r - mid.astype(F32)).astype(BF16)


def _dot_01(a, b):
    if a.dtype == jnp.bool_:
        return sum(_dot(a.astype(BF16), t) for t in _split3(b))
    return sum(_dot(t, b.astype(BF16)) for t in _split3(a))


def _resident(a):
    nd = a.ndim
    return pl.BlockSpec(a.shape, lambda *_: (0,) * nd, pipeline_mode=pl.Buffered(1))


def _acc_spec(shape):
    nd = len(shape)
    return pl.BlockSpec(shape, lambda *_: (0,) * nd)


def _params(n_grid):
    return pltpu.CompilerParams(dimension_semantics=("arbitrary",) * n_grid, vmem_limit_bytes=VMEM_LIMIT)


def _sds(shape, dtype):
    return jax.ShapeDtypeStruct(shape, dtype)


def _rms(x):
    r = lax.rsqrt(jnp.mean(x * x, axis=-1, keepdims=True) + EPS)
    return r, x * r


def _rms_bwd(dy, xh, r, g):
    dxh = dy * g
    dx = r * (dxh - xh * jnp.mean(dxh * xh, axis=-1, keepdims=True))
    return dx, dy * xh


def _head_expand(rows, cols):
    hd = lax.broadcasted_iota(jnp.int32, (rows, cols), 1) // HEAD_D
    hr = lax.broadcasted_iota(jnp.int32, (rows, cols), 0)
    return hd == hr


def _feat_major_spec(TB, FOX_T, nb):
    return pl.BlockSpec((1, TB // FOX_T, FOX_W, FOX_T), lambda b, j: (b, j, 0, 0))


def _fwd_in(x2, g_mix, w_int, w_ft, bf_pad, B, S, comm=None):
    T = B * S
    TB = min(512, S)
    nb = S // TB
    FOX_T = min(256, S)

    def body(x_ref, g_ref, w_ref, wf_ref, bf_ref, h_ref, u_ref, gt_ref, q_ref, k_ref, v_ref, zf_ref, c_ref, cq_ref,
             qx_ref, kx_ref, qt_ref, carry):
        j = pl.program_id(1)

        @pl.when(j == 0)
        def _():
            carry[...] = jnp.zeros_like(carry)

        _, xh = _rms(x_ref[...])
        h = (xh * g_ref[...]).astype(BF16)
        h_ref[...] = h
        u_ref[...] = _dot_nt(h, w_ref[0:512, :])
        gt_ref[...] = _dot_nt(h, w_ref[512:1024, :])
        qf = _dot_nt(h, w_ref[1024:1536, :])
        qb = qf.astype(BF16)
        kb = _dot_nt(h, w_ref[1536:2048, :]).astype(BF16)
        q_ref[...] = qb
        k_ref[...] = kb
        for t in range(TB // FOX_T):
            qt_ref[0, t] = qf[t * FOX_T:(t + 1) * FOX_T, :].T.astype(BF16)
        v_ref[...] = _dot_nt(h, w_ref[2048:2560, :]).astype(BF16)
        zf = _dot_nt(h, wf_ref[...]) + bf_ref[...]
        zf_ref[...] = zf
        lane = lax.broadcasted_iota(jnp.int32, zf.shape, 1)
        logf = jnp.where(lane < 8, jnp.minimum(zf, 0.0) - jnp.log(1.0 + jnp.exp(-jnp.abs(zf))), 0.0)
        row = lax.broadcasted_iota(jnp.int32, (TB, TB), 0)
        col = lax.broadcasted_iota(jnp.int32, (TB, TB), 1)
        c = _dot_01(row >= col, logf) + carry[0:1, :]
        carry[0:1, :] = c[TB - 1:TB, :]
        c_ref[...] = c
        cq = _dot_01(c, _head_expand(LANES, FOX_W))
        cq_ref[...] = cq
        hl = lax.broadcasted_iota(jnp.int32, (TB, LANES), 1)
        for hd in range(2 * N_PAIR):
            grp = slice((hd // 2) * LANES, (hd // 2 + 1) * LANES)
            swap = (lambda t: t) if hd % 2 == 0 else (lambda t: pltpu.roll(t, HEAD_D, 1))
            qf = swap(qb[:, grp].astype(F32) * (1.0 / math.sqrt(HEAD_D)))
            kf = swap(kb[:, grp].astype(F32))
            cv = cq[:, grp] if hd % 2 == 1 else pltpu.roll(cq[:, grp], HEAD_D, 1)
            hi = cv.astype(BF16).astype(F32)
            mid = (cv - hi).astype(BF16).astype(F32)
            lo = (cv - hi - mid).astype(BF16).astype(F32)
            pick = lambda a, b, c3, one_from, one_to: jnp.where(hl == a[0], a[1], jnp.where(hl == b[0], b[1], jnp.where(
                hl == c3[0], c3[1], jnp.where((hl >= one_from) & (hl < one_to), 1.0, 0.0))))
            qx = jnp.where(hl < HEAD_D, qf, pick((67, hi), (68, mid), (69, lo), 64, 67))
            kx = jnp.where(hl < HEAD_D, kf, pick((64, -hi), (65, -mid), (66, -lo), 67, 70))
            qx_ref[:, hd * LANES:(hd + 1) * LANES] = qx.astype(BF16)
            kx_ref[:, hd * LANES:(hd + 1) * LANES] = kx.astype(BF16)

    tok = lambda w: pl.BlockSpec((TB, w), lambda b, j: (b * nb + j, 0))
    outs = [(D, BF16), (512, F32), (512, F32), (512, BF16), (512, BF16), (512, BF16), (LANES, F32),
            (LANES, F32), (FOX_W, F32), (2 * FOX_W, BF16), (2 * FOX_W, BF16)]
    return _call(
        body, comm, name="fwd_in", grid=(B, nb),
        in_specs=[tok(D), _resident(g_mix), _resident(w_int), _resident(w_ft), _resident(bf_pad)],
        out_specs=[tok(w) for w, _ in outs] + [_feat_major_spec(TB, FOX_T, nb)],
        out_shape=[_sds((T, w), dt) for w, dt in outs] + [_sds((B, S // FOX_T, FOX_W, FOX_T), BF16)],
        scratch_shapes=[pltpu.VMEM((8, LANES), F32)],
        args=(x2, g_mix, w_int, w_ft, bf_pad))


def _head_sum(n):
    hc = lax.broadcasted_iota(jnp.int32, (n, n), 1) // HEAD_D
    hr = lax.broadcasted_iota(jnp.int32, (n, n), 0) // HEAD_D
    return hc == hr


def _layernorm_silu(y, lg, lb):
    mu = jnp.mean(y, axis=-1, keepdims=True)
    yc = y - mu
    rs = lax.rsqrt(jnp.mean(yc * yc, axis=-1, keepdims=True) + EPS)
    n = yc * rs
    l = n * lg + lb
    return rs, n, l


SUB = 8


def _shifted_copies(cat, sh, rows):
    for r in range(1, SUB):
        sh[r, 0:rows, :] = cat[r:r + rows, :]


def _tap(cat, sh, off, rows, cols=slice(None)):
    r = off % SUB
    return cat[off:off + rows, cols] if r == 0 else sh[r, off - r:off - r + rows, cols]


CONV_ROWS = 128


def _conv_pieces(CB):
    rows = min(CONV_ROWS, CB)
    return [(r0, rows, slice(c0, c0 + LANES)) for c0 in range(0, CONV_CH, LANES) for r0 in range(0, CB, rows)]


def _conv_fwd(u, gt, cw, cb, lng, lnb, B, S, comm=None):
    T = B * S
    CB = min(256, S)
    nb = S // CB

    def body(u_ref, gt_ref, w_ref, cb_ref, lg_ref, lb_ref, y_ref, co_ref, acat, ash):
        j = pl.program_id(1)

        @pl.when(j == 0)
        def _():
            acat[0:CONV_HALO, :] = jnp.zeros((CONV_HALO, CONV_CH), F32)

        acat[CONV_HALO:CONV_HALO + CB, :] = u_ref[...] * _sig(gt_ref[...])
        _shifted_copies(acat, ash, CB + CONV_HALO - SUB)
        for r0, rows, cs in _conv_pieces(CB):
            acc = jnp.zeros((rows, LANES), F32) + cb_ref[:, cs]
            for k in range(CONV_K):
                acc = acc + w_ref[k:k + 1, cs] * _tap(acat, ash, r0 + CONV_HALO - (CONV_K - 1) + k, rows, cs)
            y_ref[r0:r0 + rows, cs] = acc
        acat[0:CONV_HALO, :] = acat[CB:CB + CONV_HALO, :]
        _, _, l = _layernorm_silu(y_ref[...], lg_ref[...], lb_ref[...])
        co_ref[...] = (l * _sig(l)).astype(BF16)

    tok = lambda w: pl.BlockSpec((CB, w), lambda b, j: (b * nb + j, 0))
    return _call(
        body, comm, name="conv_fwd", grid=(B, nb),
        in_specs=[tok(CONV_CH), tok(CONV_CH), _resident(cw), _resident(cb), _resident(lng), _resident(lnb)],
        out_specs=[tok(CONV_CH), tok(CONV_CH)],
        out_shape=[_sds((T, CONV_CH), F32), _sds((T, CONV_CH), BF16)],
        scratch_shapes=[pltpu.VMEM((CONV_HALO + CB, CONV_CH), F32),
                        pltpu.VMEM((SUB, CB + CONV_HALO - SUB, CONV_CH), F32)],
        args=(u, gt, cw, cb, lng, lnb))


def _fox_fwd(qx, kx, v, cq, B, S, comm=None):
    T = B * S
    TQ = min(256, S)
    nq = S // TQ
    one_lane = (HEAD_D, 0)

    def body(qa_ref, qb_ref, ka_ref, kb_ref, v_ref, cq_ref, o_ref, lse_ref, s_scr, s_odd, m_scr, acc_scr):
        i = pl.program_id(2)
        lane = lax.broadcasted_iota(jnp.int32, (TQ, LANES), 1)
        lo = lane < HEAD_D
        qh = (qa_ref[...], qb_ref[...])
        kh = (ka_ref, kb_ref)
        m_scr[...] = jnp.full(m_scr.shape, -1e30, F32)
        acc_scr[...] = jnp.zeros_like(acc_scr)
        row = lax.broadcasted_iota(jnp.int32, (TQ, TQ), 0)
        col = lax.broadcasted_iota(jnp.int32, (TQ, TQ), 1)
        wide = lambda x: jnp.concatenate([x, x], axis=1) if TQ == 2 * LANES else jnp.tile(x, (1, TQ // LANES))

        def scores(j, s_buf):
            start = pl.multiple_of(j * TQ, TQ)
            for h in range(2):
                s_buf[h] = _dot_nt(qh[h], kh[h][pl.ds(start, TQ), :])

        def softmax_step(j, s_buf, diagonal):
            start = pl.multiple_of(j * TQ, TQ)
            vj = v_ref[pl.ds(start, TQ), :]
            for h in range(2):
                def logits():
                    return jnp.where(col <= row, s_buf[h], -1e30) if diagonal else s_buf[h]

                m_old = m_scr[h]
                m_new = jnp.maximum(m_old, jnp.max(logits(), axis=-1, keepdims=True))
                alpha = jnp.exp(m_old - m_new)
                m_scr[h] = m_new
                p = jnp.exp(logits() - wide(m_new)).astype(BF16)
                vx = jnp.where(lane == one_lane[h], jnp.ones_like(vj), jnp.where(lo if h == 0 else ~lo, vj, jnp.zeros_like(vj)))
                acc_scr[h] = alpha * acc_scr[h] + _dot(p, vx)

        def two_blocks(jj, carry):
            j = 2 * jj
            scores(j + 1, s_odd)
            softmax_step(j, s_scr, False)
            scores(j + 2, s_scr)
            softmax_step(j + 1, s_odd, False)
            return carry

        scores(0, s_scr)
        lax.fori_loop(0, i // 2, two_blocks, 0)

        @pl.when(i % 2 == 0)
        def _():
            softmax_step(i, s_scr, True)

        @pl.when(i % 2 == 1)
        def _():
            scores(i, s_odd)
            softmax_step(i - 1, s_scr, False)
            softmax_step(i, s_odd, True)

        acc_a, acc_b = acc_scr[0], acc_scr[1]
        l_a = acc_a[:, one_lane[0]:one_lane[0] + 1]
        l_b = acc_b[:, one_lane[1]:one_lane[1] + 1]
        o_ref[...] = jnp.where(lo, acc_a / l_a, acc_b / l_b)
        lse_ref[...] = cq_ref[...] - jnp.where(lo, m_scr[0] + jnp.log(l_a), m_scr[1] + jnp.log(l_b))

    qspec = pl.BlockSpec((TQ, LANES), lambda b, p, i: (b * nq + i, p))
    kspec = pl.BlockSpec((S, LANES), lambda b, p, i: (b, p))
    qhead = lambda h: pl.BlockSpec((TQ, LANES), lambda b, p, i: (b * nq + i, 2 * p + h))
    khead = lambda h: pl.BlockSpec((S, LANES), lambda b, p, i: (b, 2 * p + h))
    return _call(
        body, comm, name="fox_fwd", grid=(B, N_PAIR, nq),
        in_specs=[qhead(0), qhead(1), khead(0), khead(1), kspec, qspec],
        out_specs=[qspec, qspec],
        out_shape=[_sds((T, FOX_W), F32), _sds((T, FOX_W), F32)],
        scratch_shapes=[pltpu.VMEM((2, TQ, TQ), F32), pltpu.VMEM((2, TQ, TQ), F32),
                        pltpu.VMEM((2, TQ, LANES), F32), pltpu.VMEM((2, TQ, LANES), F32)],
        args=(qx, qx, kx, kx, v, cq))


def _mem_kv(mem2, g_mem, w_mkv, B):
    def body(m_ref, g_ref, w_ref, mn_ref, km_ref, vm_ref):
        _, xh = _rms(m_ref[...])
        mn = (xh * g_ref[...]).astype(BF16)
        mn_ref[...] = mn
        for s in range(2):
            km_ref[:, 512 * s:512 * (s + 1)] = _dot(mn, w_ref[s]).astype(BF16)
            vm_ref[:, 512 * s:512 * (s + 1)] = _dot(mn, w_ref[2 + s]).astype(BF16)

    blk = pl.BlockSpec((MEM_LEN, D), lambda b: (b, 0))
    return pl.pallas_call(
        body, name="mem_kv", grid=(B,),
        in_specs=[blk, _resident(g_mem), _resident(w_mkv)],
        out_specs=[blk, blk, blk],
        out_shape=[_sds((B * MEM_LEN, D), BF16)] * 3,
        compiler_params=_params(1),
    )(mem2, g_mem, w_mkv)


def _mem_probs(qm, km):
    ps = []
    for h in range(MEM_HEADS):
        hs = slice(h * MEM_HD, (h + 1) * MEM_HD)
        lg = _dot_nt(qm[:, hs], km[:, hs]) * (1.0 / math.sqrt(MEM_HD))
        e = jnp.exp(lg - jnp.max(lg, axis=-1, keepdims=True))
        ps.append(e / jnp.sum(e, axis=-1, keepdims=True))
    return ps


def _fwd_mid(x2, co, o, km, vm, w_out, w_mq, w_mo, g_x, B, S, comm=None):
    T = B * S
    TB = min(512, S)
    nb = S // TB

    def body(x_ref, co_ref, o_ref, km_ref, vm_ref, wo_ref, wq_ref, wm_ref, g_ref,
             x1_ref, hx_ref, qm_ref, om_ref, x2_ref, cat_ref):
        cat_ref[:, 0:CONV_CH] = co_ref[...]
        cat_ref[:, CONV_CH:D] = o_ref[...].astype(BF16)
        x1 = x_ref[...] + _dot(cat_ref[...], wo_ref[...])
        x1_ref[...] = x1
        _, xh = _rms(x1)
        hx = (xh * g_ref[...]).astype(BF16)
        hx_ref[...] = hx
        qm = _dot(hx, wq_ref[...]).astype(BF16)
        qm_ref[...] = qm
        ps = _mem_probs(qm, km_ref[...])
        vmv = vm_ref[...]
        for h in range(MEM_HEADS):
            hs = slice(h * MEM_HD, (h + 1) * MEM_HD)
            om_ref[:, hs] = _dot(ps[h].astype(BF16), vmv[:, hs]).astype(BF16)
        x2_ref[...] = x1 + _dot(om_ref[...], wm_ref[...])

    tok = lambda w: pl.BlockSpec((TB, w), lambda b, j: (b * nb + j, 0))
    memb = pl.BlockSpec((MEM_LEN, D), lambda b, j: (b, 0))
    outs = [(D, F32), (D, BF16), (D, BF16), (D, BF16), (D, F32), (D, BF16)]
    return _call(
        body, comm, name="fwd_mid", grid=(B, nb),
        in_specs=[tok(D), tok(CONV_CH), tok(FOX_W), memb, memb, _resident(w_out), _resident(w_mq), _resident(w_mo),
                  _resident(g_x)],
        out_specs=[tok(w) for w, _ in outs],
        out_shape=[_sds((T, w), dt) for w, dt in outs],
        scratch_shapes=[],
        args=(x2, co, o, km, vm, w_out, w_mq, w_mo, g_x))


def _fwd_ffn(x2, tgt, w_gu, w_down, g_ffn, g_final, T):
    TB = min(256, T)
    nb = T // TB

    def body(x_ref, t_ref, wgu_ref, wd_ref, gf_ref, gl_ref, hf_ref, gu_ref, act_ref, dx3_ref, loss_ref, dgl_ref):
        i = pl.program_id(0)

        @pl.when(i == 0)
        def _():
            loss_ref[...] = jnp.zeros_like(loss_ref)
            dgl_ref[...] = jnp.zeros_like(dgl_ref)

        x2v = x_ref[...]
        _, xh = _rms(x2v)
        hf = (xh * gf_ref[...]).astype(BF16)
        hf_ref[...] = hf
        x3 = x2v
        for ch in range(D_FF // FF_CHUNK):
            c0 = ch * FF_CHUNK
            g = _dot(hf, wgu_ref[ch])
            u = _dot(hf, wgu_ref[2 + ch])
            gu_ref[:, c0:c0 + FF_CHUNK] = g
            gu_ref[:, D_FF + c0:D_FF + c0 + FF_CHUNK] = u
            act = (g * _sig(g) * u).astype(BF16)
            act_ref[:, c0:c0 + FF_CHUNK] = act
            x3 = x3 + _dot(act, wd_ref[c0:c0 + FF_CHUNK, :])
        r3, xh3 = _rms(x3)
        gl = gl_ref[...]
        e = xh3 * gl - t_ref[...]
        loss_ref[...] += jnp.sum(e * e) * (0.5 / D)
        dy = e * (1.0 / D)
        dx3, dgl = _rms_bwd(dy, xh3, r3, gl)
        dx3_ref[...] = dx3
        dgl_ref[...] += jnp.sum(dgl, axis=0, keepdims=True)

    tok = lambda w: pl.BlockSpec((TB, w), lambda i: (i, 0))
    return pl.pallas_call(
        body, name="fwd_ffn", grid=(nb,),
        in_specs=[tok(D), tok(D), _resident(w_gu), _resident(w_down), _resident(g_ffn), _resident(g_final)],
        out_specs=[tok(D), tok(2 * D_FF), tok(D_FF), tok(D), _acc_spec((1, LANES)), _acc_spec((1, D))],
        out_shape=[_sds((T, D), BF16), _sds((T, 2 * D_FF), F32), _sds((T, D_FF), BF16), _sds((T, D), F32),
                   _sds((1, LANES), F32), _sds((1, D), F32)],
        compiler_params=_params(1),
    )(x2, tgt, w_gu, w_down, g_ffn, g_final)


def _bwd_ffn(dx3, gu, x2, w_gu, w_down, g_ffn, T):
    TB = min(256, T)
    nb = T // TB

    def body(d_ref, gu_ref, x_ref, wgu_ref, wd_ref, gf_ref, dgu_ref, dx2_ref, dgf_ref):
        i = pl.program_id(0)

        @pl.when(i == 0)
        def _():
            dgf_ref[...] = jnp.zeros_like(dgf_ref)

        dx3v = d_ref[...]
        db = dx3v.astype(BF16)
        dhf = jnp.zeros((TB, D), F32)
        for ch in range(D_FF // FF_CHUNK):
            c0 = ch * FF_CHUNK
            dact = _dot_nt(db, wd_ref[c0:c0 + FF_CHUNK, :])
            g = gu_ref[:, c0:c0 + FF_CHUNK]
            u = gu_ref[:, D_FF + c0:D_FF + c0 + FF_CHUNK]
            sg = _sig(g)
            dg = (dact * u * sg * (1.0 + g * (1.0 - sg))).astype(BF16)
            du = (dact * g * sg).astype(BF16)
            dgu_ref[:, c0:c0 + FF_CHUNK] = dg
            dgu_ref[:, D_FF + c0:D_FF + c0 + FF_CHUNK] = du
            dhf = dhf + _dot_nt(dg, wgu_ref[ch]) + _dot_nt(du, wgu_ref[2 + ch])
        r2, xh2 = _rms(x_ref[...])
        dx, dg_tok = _rms_bwd(dhf, xh2, r2, gf_ref[...])
        dx2_ref[...] = dx3v + dx
        dgf_ref[...] += jnp.sum(dg_tok, axis=0, keepdims=True)

    tok = lambda w: pl.BlockSpec((TB, w), lambda i: (i, 0))
    return pl.pallas_call(
        body, name="bwd_ffn", grid=(nb,),
        in_specs=[tok(D), tok(2 * D_FF), tok(D), _resident(w_gu), _resident(w_down), _resident(g_ffn)],
        out_specs=[tok(2 * D_FF), tok(D), _acc_spec((1, D))],
        out_shape=[_sds((T, 2 * D_FF), BF16), _sds((T, D), F32), _sds((1, D), F32)],
        compiler_params=_params(1),
    )(dx3, gu, x2, w_gu, w_down, g_ffn)


def _bwd_mid(dx2, x1, qm, km, vm, o, w_mo, w_mq, w_out, g_x, B, S, comm=None):
    T = B * S
    TB = min(512, S)
    nb = S // TB
    FOX_T = min(256, S)
    inv = 1.0 / math.sqrt(MEM_HD)

    def body(d_ref, x1_ref, qm_ref, km_ref, vm_ref, o_ref, wm_ref, wq_ref, wo_ref, g_ref,
             dx1_ref, dqm_ref, dco_ref, do_ref, dd_ref, dkm_ref, dvm_ref, dgx_ref, dot_ref):
        b = pl.program_id(0)
        j = pl.program_id(1)

        @pl.when((b == 0) & (j == 0))
        def _():
            dgx_ref[...] = jnp.zeros_like(dgx_ref)

        @pl.when(j == 0)
        def _():
            dkm_ref[...] = jnp.zeros_like(dkm_ref)
            dvm_ref[...] = jnp.zeros_like(dvm_ref)

        dx2v = d_ref[...]
        dom = _dot_nt(dx2v.astype(BF16), wm_ref[...]).astype(BF16)
        qmv = qm_ref[...]
        kmv = km_ref[...]
        vmv = vm_ref[...]
        ps = _mem_probs(qmv, kmv)
        for h in range(MEM_HEADS):
            hs = slice(h * MEM_HD, (h + 1) * MEM_HD)
            p = ps[h]
            dp = _dot_nt(dom[:, hs], vmv[:, hs])
            ds = (p * (dp - jnp.sum(p * dp, axis=-1, keepdims=True))).astype(BF16)
            dqm_ref[:, hs] = (_dot(ds, kmv[:, hs]) * inv).astype(BF16)
            dkm_ref[:, hs] += _dot_tn(ds, qmv[:, hs]) * inv
            dvm_ref[:, hs] += _dot_tn(p.astype(BF16), dom[:, hs])
        dhx = _dot_nt(dqm_ref[...], wq_ref[...])
        r1, xh1 = _rms(x1_ref[...])
        dx, dg_tok = _rms_bwd(dhx, xh1, r1, g_ref[...])
        dx1 = dx2v + dx
        dx1_ref[...] = dx1
        dgx_ref[...] += jnp.sum(dg_tok, axis=0, keepdims=True)
        d1b = dx1.astype(BF16)
        dco_ref[...] = _dot_nt(d1b, wo_ref[0:CONV_CH, :])
        do = _dot_nt(d1b, wo_ref[CONV_CH:D, :])
        dob = do.astype(BF16)
        do_ref[...] = dob
        for t in range(TB // FOX_T):
            dot_ref[0, t] = do[t * FOX_T:(t + 1) * FOX_T, :].T.astype(BF16)
        dd_ref[...] = _dot_01(dob.astype(F32) * o_ref[...], _head_sum(FOX_W))

    tok = lambda w: pl.BlockSpec((TB, w), lambda b, j: (b * nb + j, 0))
    memb = pl.BlockSpec((MEM_LEN, D), lambda b, j: (b, 0))
    outs = [(D, F32), (D, BF16), (CONV_CH, F32), (FOX_W, BF16), (FOX_W, F32)]
    return _call(
        body, comm, name="bwd_mid", grid=(B, nb),
        in_specs=[tok(D), tok(D), tok(D), memb, memb, tok(FOX_W), _resident(w_mo), _resident(w_mq), _resident(w_out),
                  _resident(g_x)],
        out_specs=[tok(w) for w, _ in outs] + [memb, memb, _acc_spec((1, D)), _feat_major_spec(TB, FOX_T, nb)],
        out_shape=[_sds((T, w), dt) for w, dt in outs] + [_sds((B * MEM_LEN, D), F32)] * 2 + [_sds((1, D), F32)]
        + [_sds((B, S // FOX_T, FOX_W, FOX_T), BF16)],
        scratch_shapes=[],
        args=(dx2, x1, qm, km, vm, o, w_mo, w_mq, w_out, g_x))


def _mem_bwd(dkm, dvm, mem2, w_mkv, g_mem, B):
    def body(dk_ref, dv_ref, m_ref, w_ref, g_ref, dkv_ref, dg_ref):
        b = pl.program_id(0)

        @pl.when(b == 0)
        def _():
            dg_ref[...] = jnp.zeros_like(dg_ref)

        dk = dk_ref[...].astype(BF16)
        dv = dv_ref[...].astype(BF16)
        dkv_ref[:, 0:D] = dk
        dkv_ref[:, D:2 * D] = dv
        dmn = jnp.zeros((MEM_LEN, D), F32)
        for s in range(2):
            dmn = dmn + _dot_nt(dk[:, 512 * s:512 * (s + 1)], w_ref[s]) + _dot_nt(dv[:, 512 * s:512 * (s + 1)], w_ref[2 + s])
        _, xh = _rms(m_ref[...])
        dg_ref[...] += jnp.sum(dmn * xh, axis=0, keepdims=True)

    blk = pl.BlockSpec((MEM_LEN, D), lambda b: (b, 0))
    return pl.pallas_call(
        body, name="mem_bwd", grid=(B,),
        in_specs=[blk, blk, blk, _resident(w_mkv), _resident(g_mem)],
        out_specs=[pl.BlockSpec((MEM_LEN, 2 * D), lambda b: (b, 0)), _acc_spec((1, D))],
        out_shape=[_sds((B * MEM_LEN, 2 * D), BF16), _sds((1, D), F32)],
        compiler_params=_params(1),
    )(dkm, dvm, mem2, w_mkv, g_mem)


def _fox_bwd(q, k, v, do, bias, dd, ckT, qT, doT, B, S, comm=None):
    T = B * S
    TK = min(256, S)
    nk = S // TK
    scale = 1.0 / math.sqrt(HEAD_D)

    def body(q_ref, k_ref, v_ref, do_ref, bias_ref, dd_ref, ck_ref, qt_ref, dot_ref, dq_ref, dk_ref, dv_ref, dc_ref,
             dcq_ref, dq_acc, rs_acc, s_scr, dp_scr, s_odd, dp_odd, dk_acc, dv_acc, dc_acc):
        j = pl.program_id(2)

        @pl.when(j == 0)
        def _():
            dq_acc[...] = jnp.zeros_like(dq_acc)
            rs_acc[...] = jnp.zeros_like(rs_acc)

        dk_acc[...] = jnp.zeros_like(dk_acc)
        dv_acc[...] = jnp.zeros_like(dv_acc)
        dc_acc[...] = jnp.zeros_like(dc_acc)
        lane = lax.broadcasted_iota(jnp.int32, (TK, LANES), 1)
        lo = lane < HEAD_D
        ks = k_ref[...] * jnp.asarray(scale, BF16)
        v2 = v_ref[...]
        zero = jnp.zeros_like(ks)
        kh = (jnp.where(lo, ks, zero), jnp.where(lo, zero, ks))
        vh = (jnp.where(lo, v2, zero), jnp.where(lo, zero, v2))
        kstart = pl.multiple_of(j * TK, TK)
        ckh = tuple(ck_ref[0, 0, h:h + 1, pl.ds(kstart, TK)] for h in range(2))
        row = lax.broadcasted_iota(jnp.int32, (TK, TK), 0)
        col = lax.broadcasted_iota(jnp.int32, (TK, TK), 1)
        wide = lambda x: jnp.concatenate([x, x], axis=1) if TK == 2 * LANES else jnp.tile(x, (1, TK // LANES))

        def scores(i, s_buf, dp_buf):
            start = pl.multiple_of(i * TK, TK)
            qi = q_ref[pl.ds(start, TK), :]
            doi = do_ref[pl.ds(start, TK), :]
            for h in range(2):
                s_buf[h] = _dot_nt(qi, kh[h])
                dp_buf[h] = _dot_nt(doi, vh[h])

        def grads(i, s_buf, dp_buf, diagonal):
            start = pl.multiple_of(i * TK, TK)
            bias2 = bias_ref[pl.ds(start, TK), :]
            dd2 = dd_ref[pl.ds(start, TK), :]
            for h in range(2):
                hc = slice(h * HEAD_D, h * HEAD_D + 1)
                bias = jnp.broadcast_to(bias2[:, hc], (TK, LANES))
                ddh = jnp.broadcast_to(dd2[:, hc], (TK, LANES))
                p = jnp.exp((s_buf[h] - ckh[h]) + wide(bias))
                if diagonal:
                    p = jnp.where(col <= row, p, 0.0)
                ds = p * (dp_buf[h] - wide(ddh))
                dc_acc[h, 0:1, :] += jnp.sum(ds, axis=0, keepdims=True)
                rs_acc[h, pl.ds(start, TK), :] += jnp.sum(ds, axis=1, keepdims=True)
                pb = p.astype(BF16)
                dsb = ds.astype(BF16)
                feat = slice(h * HEAD_D, (h + 1) * HEAD_D)
                dv_acc[feat, :] += _dot(dot_ref[0, i, feat, :], pb)
                dk_acc[feat, :] += _dot(qt_ref[0, i, feat, :], dsb)
                dq_acc[pl.ds(start, TK), :] += _dot(dsb, kh[h])

        n_off = nk - 1 - j
        block = lambda t: jnp.where(t < n_off, j + 1 + t, j)

        def two_blocks(tt, carry):
            t = 2 * tt
            scores(block(t + 1), s_odd, dp_odd)
            grads(block(t), s_scr, dp_scr, False)
            scores(block(t + 2), s_scr, dp_scr)
            grads(block(t + 1), s_odd, dp_odd, False)
            return carry

        scores(block(0), s_scr, dp_scr)
        lax.fori_loop(0, n_off // 2, two_blocks, 0)

        @pl.when(n_off % 2 == 0)
        def _():
            grads(j, s_scr, dp_scr, True)

        @pl.when(n_off % 2 == 1)
        def _():
            scores(j, s_odd, dp_odd)
            grads(nk - 1, s_scr, dp_scr, False)
            grads(j, s_odd, dp_odd, True)

        dk_ref[...] = (dk_acc[...].T * scale).astype(BF16)
        dv_ref[...] = dv_acc[...].T.astype(BF16)
        sub = lax.broadcasted_iota(jnp.int32, (8, TK), 0)
        dca = dc_acc[0, 0:1, :]
        dcb = dc_acc[1, 0:1, :]
        dc_ref[0, 0] = jnp.where(sub == 0, -dca, jnp.where(sub == 1, -dcb, 0.0))

        @pl.when(j == nk - 1)
        def _():
            dq_ref[...] = dq_acc[...].astype(BF16)
            lo_s = lax.broadcasted_iota(jnp.int32, (S, LANES), 1) < HEAD_D
            dcq_ref[...] = jnp.where(lo_s, rs_acc[0], rs_acc[1])

    full = pl.BlockSpec((S, LANES), lambda b, p, j: (b, p))
    blk = pl.BlockSpec((TK, LANES), lambda b, p, j: (b * nk + j, p))
    featT = pl.BlockSpec((1, nk, LANES, TK), lambda b, p, j: (b, 0, p, 0))
    return _call(
        body, comm, name="fox_bwd", grid=(B, N_PAIR, nk),
        in_specs=[full, blk, blk, full, full, full, pl.BlockSpec((1, 1, 8, S), lambda b, p, j: (b, p, 0, 0)),
                  featT, featT],
        out_specs=[full, blk, blk, pl.BlockSpec((1, 1, 8, TK), lambda b, p, j: (b, p, 0, j)), full],
        out_shape=[_sds((T, FOX_W), BF16), _sds((T, FOX_W), BF16), _sds((T, FOX_W), BF16),
                   _sds((B, N_PAIR, 8, S), F32), _sds((T, FOX_W), F32)],
        scratch_shapes=[pltpu.VMEM((S, LANES), F32), pltpu.VMEM((2, S, 1), F32),
                        pltpu.VMEM((2, TK, TK), F32), pltpu.VMEM((2, TK, TK), F32),
                        pltpu.VMEM((2, TK, TK), F32), pltpu.VMEM((2, TK, TK), F32),
                        pltpu.VMEM((LANES, TK), F32), pltpu.VMEM((LANES, TK), F32), pltpu.VMEM((2, 8, TK), F32)],
        args=(q, k, v, do, bias, dd, ckT, qT, doT))


def _fgate_bwd(dc8, zf, B, S):
    T = B * S
    TB = min(512, S)
    nb = S // TB

    def body(dc_ref, zf_ref, dzf_ref, dbf_ref, carry):
        b = pl.program_id(0)
        j = pl.program_id(1)

        @pl.when((b == 0) & (j == 0))
        def _():
            dbf_ref[...] = jnp.zeros_like(dbf_ref)

        @pl.when(j == 0)
        def _():
            carry[...] = jnp.zeros_like(carry)

        dc = dc_ref[...]
        row = lax.broadcasted_iota(jnp.int32, (TB, TB), 0)
        col = lax.broadcasted_iota(jnp.int32, (TB, TB), 1)
        dlogf = _dot_01(col >= row, dc) + carry[0:1, :]
        carry[0:1, :] = dlogf[0:1, :]
        lane = lax.broadcasted_iota(jnp.int32, dc.shape, 1)
        dzf = jnp.where(lane < 8, dlogf * _sig(-zf_ref[...]), 0.0)
        dzf_ref[...] = dzf.astype(BF16)
        dbf_ref[...] += jnp.sum(dzf, axis=0, keepdims=True)

    tok = pl.BlockSpec((TB, LANES), lambda b, j: (b * nb + (nb - 1 - j), 0))
    return pl.pallas_call(
        body, name="fgate_bwd", grid=(B, nb),
        in_specs=[tok, tok],
        out_specs=[tok, _acc_spec((1, LANES))],
        out_shape=[_sds((T, LANES), BF16), _sds((1, LANES), F32)],
        scratch_shapes=[pltpu.VMEM((8, LANES), F32)],
        compiler_params=_params(2),
    )(dc8, zf)


def _conv_bwd(dco, y, u, gt, cw, lng, lnb, B, S, comm=None):
    T = B * S
    CB = min(256, S)
    nb = S // CB
    hb = CB // CONV_HALO

    def body(dco_ref, y_ref, u_ref, gt_ref, up_ref, gp_ref, w_ref, lg_ref, lb_ref,
             du_ref, dgt_ref, dw_ref, vec_ref, acat, dycat, ash, dysh):
        b = pl.program_id(0)
        j = pl.program_id(1)
        jr = nb - 1 - j

        @pl.when((b == 0) & (j == 0))
        def _():
            dw_ref[...] = jnp.zeros_like(dw_ref)
            vec_ref[...] = jnp.zeros_like(vec_ref)

        @pl.when(j == 0)
        def _():
            dycat[CB:CB + CONV_HALO, :] = jnp.zeros((CONV_HALO, CONV_CH), F32)

        lg = lg_ref[...]
        rs, n, l = _layernorm_silu(y_ref[...], lg, lb_ref[...])
        sg = _sig(l)
        dl = dco_ref[...] * (sg * (1.0 + l * (1.0 - sg)))
        dn = dl * lg
        dy = rs * (dn - jnp.mean(dn, axis=-1, keepdims=True) - n * jnp.mean(dn * n, axis=-1, keepdims=True))
        vec_ref[0:1, :] += jnp.sum(dy, axis=0, keepdims=True)
        vec_ref[1:2, :] += jnp.sum(dl * n, axis=0, keepdims=True)
        vec_ref[2:3, :] += jnp.sum(dl, axis=0, keepdims=True)
        dycat[0:CB, :] = dy
        acat[0:CONV_HALO, :] = jnp.where(jr > 0, up_ref[...] * _sig(gp_ref[...]), 0.0)
        acat[CONV_HALO:CONV_HALO + CB, :] = u_ref[...] * _sig(gt_ref[...])
        _shifted_copies(acat, ash, CB + CONV_HALO - SUB)
        _shifted_copies(dycat, dysh, CB + CONV_HALO - SUB)
        for r0, rows, cs in _conv_pieces(CB):
            dyp = dycat[r0:r0 + rows, cs]
            da = jnp.zeros((rows, LANES), F32)
            for k in range(CONV_K):
                da = da + w_ref[k:k + 1, cs] * _tap(dycat, dysh, r0 + CONV_K - 1 - k, rows, cs)
                dw_ref[k:k + 1, cs] += jnp.sum(dyp * _tap(acat, ash, r0 + CONV_HALO - (CONV_K - 1) + k, rows, cs),
                                               axis=0, keepdims=True)
            uv = u_ref[r0:r0 + rows, cs]
            sgt = _sig(gt_ref[r0:r0 + rows, cs])
            du_ref[r0:r0 + rows, cs] = (da * sgt).astype(BF16)
            dgt_ref[r0:r0 + rows, cs] = (da * uv * sgt * (1.0 - sgt)).astype(BF16)
        dycat[CB:CB + CONV_HALO, :] = dycat[0:CONV_HALO, :]

    tok = lambda w: pl.BlockSpec((CB, w), lambda b, j: (b * nb + (nb - 1 - j), 0))
    prev = pl.BlockSpec((CONV_HALO, CONV_CH), lambda b, j: (jnp.maximum((b * nb + (nb - 1 - j)) * hb - 1, 0), 0))
    return _call(
        body, comm, name="conv_bwd", grid=(B, nb),
        in_specs=[tok(CONV_CH), tok(CONV_CH), tok(CONV_CH), tok(CONV_CH), prev, prev, _resident(cw), _resident(lng),
                  _resident(lnb)],
        out_specs=[tok(CONV_CH), tok(CONV_CH), _acc_spec((CONV_HALO, CONV_CH)), _acc_spec((8, CONV_CH))],
        out_shape=[_sds((T, CONV_CH), BF16), _sds((T, CONV_CH), BF16), _sds((CONV_HALO, CONV_CH), F32),
                   _sds((8, CONV_CH), F32)],
        scratch_shapes=[pltpu.VMEM((CONV_HALO + CB, CONV_CH), F32), pltpu.VMEM((CB + CONV_HALO, CONV_CH), F32),
                        pltpu.VMEM((SUB, CB + CONV_HALO - SUB, CONV_CH), F32),
                        pltpu.VMEM((SUB, CB + CONV_HALO - SUB, CONV_CH), F32)],
        args=(dco, y, u, gt, u, gt, cw, lng, lnb))


def _bwd_in(dz, w_int, w_ft, x2, dx1, g_mix, T, comm=None):
    TB = min(512, T)
    nb = T // TB

    def body(dz_ref, w_ref, wf_ref, x_ref, d1_ref, g_ref, gx_ref, dg_ref):
        i = pl.program_id(0)

        @pl.when(i == 0)
        def _():
            dg_ref[...] = jnp.zeros_like(dg_ref)

        dh = _dot(dz_ref[:, 0:OFF_F], w_ref[0:OFF_F, :]) + _dot(dz_ref[:, OFF_F:D_IN_PAD], wf_ref[...])
        r0, xh0 = _rms(x_ref[...])
        dx, dg_tok = _rms_bwd(dh, xh0, r0, g_ref[...])
        gx_ref[...] = d1_ref[...] + dx
        dg_ref[...] += jnp.sum(dg_tok, axis=0, keepdims=True)

    tok = lambda w: pl.BlockSpec((TB, w), lambda i: (i, 0))
    return _call(
        body, comm, name="bwd_in", grid=(nb,),
        in_specs=[tok(D_IN_PAD), _resident(w_int), _resident(w_ft), tok(D), tok(D), _resident(g_mix)],
        out_specs=[tok(D), _acc_spec((1, D))],
        out_shape=[_sds((T, D), F32), _sds((1, D), F32)],
        scratch_shapes=[],
        args=(dz, w_int, w_ft, x2, dx1, g_mix))


def _dw(a, b, name, tn, slabs=False, tk=None, rows=None):
    T, K = a.shape
    N = b.shape[1]
    tk = tk or (K if K <= 1024 else K // 2)
    tt = min(1024, T)
    nt = T // tt

    def body(a_ref, b_ref, o_ref, acc):
        t = pl.program_id(2)

        @pl.when(t == 0)
        def _():
            acc[...] = jnp.zeros_like(acc)

        acc[...] += _dot_tn(a_ref[...].astype(BF16), b_ref[...].astype(BF16))

        @pl.when(t == nt - 1)
        def _():
            o_ref[...] = acc[...]

    return pl.pallas_call(
        body, name=name, grid=(K // tk, N // tn, nt),
        in_specs=[pl.BlockSpec((tt, tk), lambda i, j, t: (t, i)), pl.BlockSpec((tt, tn), lambda i, j, t: (t, j))],
        out_specs=(pl.BlockSpec((None, tk, tn), lambda i, j, t: (j, i, 0)) if slabs
                   else pl.BlockSpec((tk, tn), lambda i, j, t: (i, j))),
        out_shape=_sds((N // tn, K, tn) if slabs else (rows or K, N), F32),
        scratch_shapes=[pltpu.VMEM((tk, tn), F32)],
        compiler_params=_params(3),
    )(a, b)


def _pos():
    return lax.axis_index("x"), lax.axis_index("y"), lax.axis_index("c")


def _remote(src, dst, ssem, rsem, to):
    return pltpu.make_async_remote_copy(src_ref=src, dst_ref=dst, send_sem=ssem, recv_sem=rsem, device_id=to,
                                        device_id_type=MESH)


def _split_axis(shape):
    return 0 if shape[0] % 32 == 0 else 1


def _half_shape(shape, parts=2):
    return (shape[0] // parts, shape[1]) if _split_axis(shape) == 0 else (shape[0], shape[1] // parts)


def _half(shape, c):
    R, C = shape
    if _split_axis(shape) == 0:
        return (pl.ds(pl.multiple_of(c * (R // 2), 16), R // 2), slice(None))
    return (slice(None), pl.ds(pl.multiple_of(c * (C // 2), LANES), C // 2))


def _half_block(shape, parts, lead, which):
    blk = _half_shape(shape, parts)
    idx = (which, 0) if _split_axis(shape) == 0 else (0, which)
    return blk, tuple(lead) + idx


class _Comm:
    def __init__(self, ins, out_shapes, sems, start, finish):
        self.ins, self.out_shapes, self.sems, self.start, self.finish = list(ins), list(out_shapes), list(sems), start, finish


def _ag_comm(shards):
    n = len(shards)

    def parts(ins, outs, sems):
        send_sems, recv_sems, local_sems = sems
        x, y, c = _pos()
        me, sib = (x, y, c), (x, y, 1 - c)
        chips = [(1 - x, y), (x, 1 - y), (1 - x, 1 - y)]

        def rows(w, px, py, pc):
            return outs[w].at[(2 * px + py,) + _half(shards[w].shape, pc)]

        def copy(w, k, block, to, src=None):
            return _remote(rows(w, *block) if src is None else src, rows(w, *block), send_sems.at[w, k],
                           recv_sems.at[w, k], to)

        mine, first = [], []
        for w in range(n):
            src = ins[w].at[_half(shards[w].shape, c)]
            mine.append(pltpu.make_async_copy(src, rows(w, *me), local_sems.at[w]))
            first += [copy(w, 0, me, sib, src=src)] + [copy(w, 1 + j, me, (*chip, c), src=src) for j, chip in enumerate(chips)]
        return c, me, sib, chips, copy, mine, first

    def start(ins, outs, sems):
        _, _, _, _, _, mine, first = parts(ins, outs, sems)
        for cp in mine + first:
            cp.start()

    def finish(ins, outs, sems):
        c, me, sib, chips, copy, mine, first = parts(ins, outs, sems)
        passed = []
        for w in range(n):
            for j, chip in enumerate(chips):
                copy(w, 1 + j, (*chip, c), me).wait_recv()
                passed.append(copy(w, 4 + j, (*chip, c), sib))
                passed[-1].start()
        for w in range(n):
            copy(w, 0, sib, me).wait_recv()
            for j, chip in enumerate(chips):
                copy(w, 4 + j, (*chip, 1 - c), me).wait_recv()
        for cp in first + passed:
            cp.wait_send()
        for cp in mine:
            cp.wait()

    D7 = pltpu.SemaphoreType.DMA((n, 7))
    return _Comm(shards, [_sds((4,) + s.shape, s.dtype) for s in shards], [D7, D7, pltpu.SemaphoreType.DMA((n,))],
                 start, finish)


def _sibling_comm(gs):
    n = len(gs)

    def copies(ins, outs, sems):
        send_sems, recv_sems = sems
        x, y, c = _pos()
        return [_remote(ins[w].at[(s,) + _half(gs[w].shape[1:], 1 - c)], outs[w].at[s], send_sems.at[w, s],
                        recv_sems.at[w, s], (x, y, 1 - c)) for w in range(n) for s in range(4)]

    def start(ins, outs, sems):
        for cp in copies(ins, outs, sems):
            cp.start()

    def finish(ins, outs, sems):
        for cp in copies(ins, outs, sems):
            cp.wait()

    D4 = pltpu.SemaphoreType.DMA((n, 4))
    return _Comm(gs, [_sds((4,) + _half_shape(g.shape[1:]), F32) for g in gs], [D4, D4], start, finish)


def _ici_comm(pbs):
    n = len(pbs)

    def copies(ins, outs, sems):
        send_sems, recv_sems = sems
        x, y, c = _pos()
        return [_remote(ins[w].at[2 * tx + ty], outs[w].at[j], send_sems.at[w, j], recv_sems.at[w, j], (tx, ty, c))
                for w in range(n) for j, (tx, ty) in enumerate([(1 - x, y), (x, 1 - y), (1 - x, 1 - y)])]

    def start(ins, outs, sems):
        for cp in copies(ins, outs, sems):
            cp.start()

    def finish(ins, outs, sems):
        for cp in copies(ins, outs, sems):
            cp.wait()

    D3 = pltpu.SemaphoreType.DMA((n, 3))
    return _Comm(pbs, [_sds((3,) + p.shape[1:], BF16) for p in pbs], [D3, D3], start, finish)


def _join(*comms):
    counts = [(len(c.ins), len(c.out_shapes), len(c.sems)) for c in comms]

    def each(which):
        def run(ins, outs, sems):
            i = o = k = 0
            for c, (ni, no, nk) in zip(comms, counts):
                getattr(c, which)(ins[i:i + ni], outs[o:o + no], sems[k:k + nk])
                i, o, k = i + ni, o + no, k + nk
        return run

    return _Comm(sum((c.ins for c in comms), []), sum((c.out_shapes for c in comms), []),
                 sum((c.sems for c in comms), []), each("start"), each("finish"))


def _run_comm(comm, name):
    ni, no = len(comm.ins), len(comm.out_shapes)

    def body(*refs):
        ins, outs, sems = refs[:ni], refs[ni:ni + no], refs[ni + no:]
        comm.start(ins, outs, sems)
        comm.finish(ins, outs, sems)

    return pl.pallas_call(body, name=name, out_shape=comm.out_shapes, in_specs=[ANY] * ni, out_specs=[ANY] * no,
                          scratch_shapes=comm.sems)(*comm.ins)


def _call(body, comm, *, name, grid, in_specs, out_specs, out_shape, scratch_shapes, args):
    n_grid = len(grid)
    if comm is None:
        res = pl.pallas_call(body, name=name, grid=grid, in_specs=in_specs, out_specs=out_specs, out_shape=out_shape,
                             scratch_shapes=scratch_shapes, compiler_params=_params(n_grid))(*args)
        return list(res), []
    n_in, n_out, n_scr = len(in_specs), len(out_specs), len(scratch_shapes)
    ni, no = len(comm.ins), len(comm.out_shapes)

    def carried(*refs):
        ins, refs = refs[:n_in], refs[n_in:]
        cins, refs = refs[:ni], refs[ni:]
        outs, refs = refs[:n_out], refs[n_out:]
        couts, refs = refs[:no], refs[no:]
        scr, csems = refs[:n_scr], refs[n_scr:]
        ids = [pl.program_id(ax) for ax in range(n_grid)]
        first = functools.reduce(jnp.logical_and, [i == 0 for i in ids])
        last = functools.reduce(jnp.logical_and, [i == g - 1 for i, g in zip(ids, grid)])

        @pl.when(first)
        def _():
            comm.start(cins, couts, csems)

        body(*ins, *outs, *scr)

        @pl.when(last)
        def _():
            comm.finish(cins, couts, csems)

    res = pl.pallas_call(
        carried, name=name, grid=grid, in_specs=list(in_specs) + [ANY] * ni, out_specs=list(out_specs) + [ANY] * no,
        out_shape=list(out_shape) + comm.out_shapes, scratch_shapes=list(scratch_shapes) + comm.sems,
        compiler_params=_params(n_grid))(*args, *comm.ins)
    return list(res[:n_out]), list(res[n_out:])


def _small_allreduce(v, name, halves=()):
    P = v.shape[0]
    n = len(halves)
    vm = pl.BlockSpec(memory_space=pltpu.VMEM)

    def body(v_ref, *refs):
        o_ref, outs = refs[n], refs[n + 1:2 * n + 1]
        gath, send_sems, recv_sems, half_send, half_recv = refs[2 * n + 1:]
        x, y, c = _pos()
        me = 4 * x + 2 * y + c
        gath[me] = v_ref[...]
        cps = []
        for r in range(1, 8):
            tx = (1 - x) if r & 4 else x
            ty = (1 - y) if r & 2 else y
            tc = (1 - c) if r & 1 else c
            cps.append(_remote(v_ref, gath.at[me], send_sems.at[r - 1], recv_sems.at[r - 1], (tx, ty, tc)))
        for w in range(n):
            mine = outs[w].at[_half(halves[w].shape, c)]
            cps.append(_remote(mine, mine, half_send.at[w], half_recv.at[w], (x, y, 1 - c)))
        for cp in cps:
            cp.start()
        for cp in cps:
            cp.wait()
        acc = gath[0]
        for d in range(1, 8):
            acc = acc + gath[d]
        o_ref[...] = acc

    res = pl.pallas_call(
        body, name=name, out_shape=[_sds((P, LANES), F32)] + [_sds(g.shape, F32) for g in halves],
        in_specs=[vm] + [ANY] * n, out_specs=[vm] + [ANY] * n, input_output_aliases={1 + w: 1 + w for w in range(n)},
        scratch_shapes=[pltpu.VMEM((8, P, LANES), F32), pltpu.SemaphoreType.DMA((7,)), pltpu.SemaphoreType.DMA((7,)),
                        pltpu.SemaphoreType.DMA((max(n, 1),)), pltpu.SemaphoreType.DMA((max(n, 1),))],
    )(v, *halves)
    return res[0], list(res[1:])


def _chip_sum(g, rcv, pos, name):
    shard = g.shape[1:]
    hs = _half_shape(shard)

    def body(pos_ref, g_ref, r_ref, o_ref):
        o_ref[...] = (g_ref[...] + r_ref[...]).astype(BF16)

    return pl.pallas_call(
        body, name=name, out_shape=_sds((4,) + hs, BF16),
        grid_spec=pltpu.PrefetchScalarGridSpec(
            num_scalar_prefetch=1, grid=(4,),
            in_specs=[pl.BlockSpec((1,) + hs, lambda s, pos: _half_block(shard, 2, (s,), pos[0])[1]),
                      pl.BlockSpec((1,) + hs, lambda s, pos: (s, 0, 0))],
            out_specs=pl.BlockSpec((1,) + hs, lambda s, pos: (s, 0, 0))),
        compiler_params=_params(1),
    )(pos, g, rcv)


def _final_sum(g, rcv, rc, pos, name):
    shard = g.shape[1:]
    qs = _half_shape(shard, 4)

    def body(pos_ref, g_ref, r_ref, rc_ref, o_ref):
        acc = g_ref[0] + r_ref[0]
        for j in range(3):
            acc = acc + rc_ref[j].astype(F32)
        o_ref[...] = acc

    return pl.pallas_call(
        body, name=name, out_shape=_sds(shard, F32),
        grid_spec=pltpu.PrefetchScalarGridSpec(
            num_scalar_prefetch=1, grid=(2,),
            in_specs=[pl.BlockSpec((1,) + qs, lambda i, pos: _half_block(shard, 4, (pos[1],), pos[0] * 2 + i)[1]),
                      pl.BlockSpec((1,) + qs, lambda i, pos: _half_block(shard, 4, (pos[1],), i)[1]),
                      pl.BlockSpec((3,) + qs, lambda i, pos: _half_block(shard, 4, (0,), i)[1])],
            out_specs=pl.BlockSpec(qs, lambda i, pos: _half_block(shard, 4, (), pos[0] * 2 + i)[1])),
        compiler_params=_params(1),
    )(pos, g, rcv, rc)


def _adamw_math(w, g, m, v):
    m = ADAM_B1 * m + (1.0 - ADAM_B1) * g
    v = ADAM_B2 * v + (1.0 - ADAM_B2) * (g * g)
    m_hat = m / (1.0 - ADAM_B1 ** ADAM_STEP)
    v_hat = v / (1.0 - ADAM_B2 ** ADAM_STEP)
    delta = -ADAM_LR * (m_hat / (jnp.sqrt(v_hat) + ADAM_EPS) + ADAM_WD * w)
    return delta, m, v


def _adamw(w, g, m, v, name, blk_shape):
    R, C = w.shape

    def body(w_ref, g_ref, m_ref, v_ref, go_ref, d_ref, nm_ref, nv_ref):
        g = g_ref[...]
        d, nm, nv = _adamw_math(w_ref[...], g, m_ref[...], v_ref[...])
        go_ref[...] = g
        d_ref[...] = d
        nm_ref[...] = nm
        nv_ref[...] = nv

    blk = pl.BlockSpec(blk_shape, lambda i, j: (i, j))
    return pl.pallas_call(
        body, name=name, grid=(R // blk_shape[0], C // blk_shape[1]), in_specs=[blk] * 4, out_specs=[blk] * 4,
        out_shape=[_sds((R, C), F32)] * 4, compiler_params=_params(2),
    )(w, g, m, v)


SMALL = (("g_mix", 8), ("b_f", 8), ("conv_w", None), ("conv_b", 8), ("ln_g", 8), ("ln_b", 8), ("g_x", 8), ("g_mem", 8),
         ("g_ffn", 8), ("g_final", 8), ("loss", 8))


def _pack_small(parts, conv_rows):
    rows = []
    for name, n in SMALL:
        if name not in parts:
            continue
        n = conv_rows if n is None else n
        flat = parts[name].reshape(-1).astype(F32)
        flat = jnp.pad(flat, (0, n * LANES - flat.shape[0]))
        rows.append(flat.reshape(n, LANES))
    return jnp.concatenate(rows, axis=0)


def _unpack_small(p, shapes, conv_rows):
    out, off = {}, 0
    for name, n in SMALL:
        if name not in shapes:
            continue
        n = conv_rows if n is None else n
        size = math.prod(shapes[name])
        out[name] = p[off:off + n].reshape(-1)[:size].reshape(shapes[name])
        off += n
    return out


def kernel(x, mem, g_mix, w_in, b_f, conv_w, conv_b, ln_g, ln_b, w_out, g_x, g_mem, w_mq, w_mkv, w_mo, g_ffn, w_gu, w_down, g_final, loss_target, m_g_mix, m_w_in, m_b_f, m_conv_w, m_conv_b, m_ln_g, m_ln_b, m_w_out, m_g_x, m_g_mem, m_w_mq, m_w_mkv, m_w_mo, m_g_ffn, m_w_gu, m_w_down, m_g_final, v_g_mix, v_w_in, v_b_f, v_conv_w, v_conv_b, v_ln_g, v_ln_b, v_w_out, v_g_x, v_g_mem, v_w_mq, v_w_mkv, v_w_mo, v_g_ffn, v_w_gu, v_w_down, v_g_final):
    names = ["g_mix", "w_in", "b_f", "conv_w", "conv_b", "ln_g", "ln_b", "w_out", "g_x", "g_mem", "w_mq", "w_mkv",
             "w_mo", "g_ffn", "w_gu", "w_down", "g_final"]
    W = dict(zip(names, [g_mix, w_in, b_f, conv_w, conv_b, ln_g, ln_b, w_out, g_x, g_mem, w_mq, w_mkv, w_mo, g_ffn,
                         w_gu, w_down, g_final]))
    M = dict(zip(names, [m_g_mix, m_w_in, m_b_f, m_conv_w, m_conv_b, m_ln_g, m_ln_b, m_w_out, m_g_x, m_g_mem, m_w_mq,
                         m_w_mkv, m_w_mo, m_g_ffn, m_w_gu, m_w_down, m_g_final]))
    V = dict(zip(names, [v_g_mix, v_w_in, v_b_f, v_conv_w, v_conv_b, v_ln_g, v_ln_b, v_w_out, v_g_x, v_g_mem, v_w_mq,
                         v_w_mkv, v_w_mo, v_g_ffn, v_w_gu, v_w_down, v_g_final]))
    big_names = [n for n, _, _, _ in BIG]
    B, S, _ = x.shape
    T = B * S
    mx, my, mc = _pos()
    chip = 2 * mx + my
    pos = jnp.stack([mc, chip]).astype(jnp.int32)

    shard2d = lambda a: a.reshape(a.shape[-2], a.shape[-1])
    big2d = lambda d, n: shard2d(d[n]).T if n == "w_in" else shard2d(d[n])
    shard_bf = {n: big2d(W, n).astype(BF16) for n in big_names}
    ag_mid = ["w_mkv", "w_out", "w_mq", "w_mo"]
    ag_ffn = ["w_gu", "w_down"]
    cw_mine = jnp.pad(shard2d(conv_w), ((0, 1), (0, 0)))
    w_in_slab, cw_slab = _run_comm(_ag_comm([shard_bf["w_in"], cw_mine]), "ag_w_in")
    slab = {"w_in": w_in_slab}
    w_int = w_in_slab.reshape(D_IN, D)
    w_ft = jnp.pad(w_int[OFF_F:D_IN], ((0, D_IN_PAD - D_IN), (0, 0)))
    cw = jnp.transpose(cw_slab, (1, 0, 2)).reshape(CONV_HALO, CONV_CH)

    row = lambda a: a.reshape(1, -1)
    bf_pad = jnp.pad(row(b_f), ((0, 0), (0, LANES - 8)))
    x2d = x.reshape(T, D)
    mem2d = mem.reshape(B * MEM_LEN, D)
    tgt = loss_target.reshape(T, D)

    (h, u, gt, q, k, v, zf, c, cq, qx, kx, qT), got = _fwd_in(x2d, row(g_mix), w_int, w_ft, bf_pad, B, S,
                                                  comm=_ag_comm([shard_bf[n] for n in ag_mid[:2]]))
    slab.update(zip(ag_mid[:2], got))
    ckT = jnp.transpose(c.reshape(B, S, LANES)[:, :, :8], (0, 2, 1)).reshape(B, N_PAIR, 2, S)
    ckT = jnp.pad(ckT, ((0, 0), (0, 0), (0, 6), (0, 0)))
    (y, co), got = _conv_fwd(u, gt, cw, row(conv_b), row(ln_g), row(ln_b), B, S,
                             comm=_ag_comm([shard_bf[n] for n in ag_mid[2:]]))
    slab.update(zip(ag_mid[2:], got))
    (o, fox_bias), got = _fox_fwd(qx, kx, v, cq, B, S, comm=_ag_comm([shard_bf[n] for n in ag_ffn]))
    slab.update(zip(ag_ffn, got))
    full = {n: slab[n] if by_col else slab[n].reshape(4 * r, c) for n, r, c, by_col in BIG}
    mn, km, vm = _mem_kv(mem2d, row(g_mem), full["w_mkv"], B)
    (x1, hx, qm, om, x2, cat), _ = _fwd_mid(x2d, co, o, km, vm, full["w_out"], full["w_mq"], full["w_mo"], row(g_x), B, S)
    hf, gu, act, dx3, loss_p, dg_final = _fwd_ffn(x2, tgt, full["w_gu"], full["w_down"], row(g_ffn), row(g_final), T)

    pos_sum = lambda gs, rcvs, ns: [_chip_sum(g, r, pos, "rs_chip_sum_" + n) for g, r, n in zip(gs, rcvs, ns)]
    fin_sum = lambda gs, rcvs, rcs, ns: [_final_sum(g, r, q3, pos, "rs_final_sum_" + n)
                                         for g, r, q3, n in zip(gs, rcvs, rcs, ns)]
    RH = {}
    dgu, dx2, dg_ffn = _bwd_ffn(dx3, gu, x2, full["w_gu"], full["w_down"], row(g_ffn), T)
    g_ffn_w = [_dw(hf, dgu, "dw_gu", FF_CHUNK, slabs=True), _dw(act, dx3, "dw_down", 512).reshape(4, D_FF // 4, D)]
    (dx1, dqm, dco, do, dd, dkm, dvm, dg_x, doT), rcv_ffn = _bwd_mid(dx2, x1, qm, km, vm, o, full["w_mo"], full["w_mq"],
                                                                full["w_out"], row(g_x), B, S, comm=_sibling_comm(g_ffn_w))
    pb_ffn = pos_sum(g_ffn_w, rcv_ffn, ag_ffn)
    dkv, dg_mem = _mem_bwd(dkm, dvm, mem2d, full["w_mkv"], row(g_mem), B)
    g_mid_w = [_dw(mn, dkv, "dw_mkv", 512, slabs=True), _dw(cat, dx1, "dw_out", 512).reshape(4, 256, D),
               _dw(hx, dqm, "dw_mq", 512).reshape(4, 256, D), _dw(om, dx2, "dw_mo", 512).reshape(4, 256, D)]
    (dq, dk, dv, dc, dcq), got = _fox_bwd(q, k, v, do, fox_bias, dd, ckT, qT, doT, B, S,
                                          comm=_join(_ici_comm(pb_ffn), _sibling_comm(g_mid_w)))
    rc_ffn, rcv_mid = got[:len(pb_ffn)], got[len(pb_ffn):]
    RH.update(zip(ag_ffn, fin_sum(g_ffn_w, rcv_ffn, rc_ffn, ag_ffn)))
    pb_mid = pos_sum(g_mid_w, rcv_mid, ag_mid)
    dc8 = jnp.transpose(dc[:, :, :2, :].reshape(B, 8, S), (0, 2, 1)).reshape(T, 8)
    dc8 = dc8 + dcq.reshape(T, 8, HEAD_D)[:, :, 0]
    dzf, dbf = _fgate_bwd(jnp.pad(dc8, ((0, 0), (0, LANES - 8))), zf, B, S)
    (du, dgt, dcw, dvec), rc_mid = _conv_bwd(dco, y, u, gt, cw, row(ln_g), row(ln_b), B, S, comm=_ici_comm(pb_mid))
    RH.update(zip(ag_mid, fin_sum(g_mid_w, rcv_mid, rc_mid, ag_mid)))
    dz = jnp.concatenate([du, dgt, dq, dk, dv, dzf], axis=1)
    g_in_w = [_dw(dz, h, "dw_in", 512, tk=D_IN_PAD // 3, rows=D_IN).reshape(4, D_IN // 4, D)]
    rcv_in = _run_comm(_sibling_comm(g_in_w), "rs_sibling_in")
    (grad_x, dg_mix), rc_in = _bwd_in(dz, w_int, w_ft, x2d, dx1, row(g_mix), T,
                                      comm=_ici_comm(pos_sum(g_in_w, rcv_in, ["w_in"])))
    RH.update(zip(["w_in"], fin_sum(g_in_w, rcv_in, rc_in, ["w_in"])))

    small_g = {"g_mix": dg_mix, "b_f": dbf[:, :8], "conv_w": dcw, "conv_b": dvec[0], "ln_g": dvec[1], "ln_b": dvec[2],
               "g_x": dg_x, "g_mem": dg_mem, "g_ffn": dg_ffn, "g_final": dg_final, "loss": loss_p[:, :1]}
    sg, filled = _small_allreduce(_pack_small(small_g, CONV_HALO * 4), "allreduce_small", [RH[n] for n in big_names])
    shared = dict(zip(big_names, filled))
    G, DL, NM, NV = {}, {}, {}, {}
    for n in big_names:
        G[n], DL[n], NM[n], NV[n] = _adamw(big2d(W, n), shared[n], big2d(M, n), big2d(V, n), "adamw_" + n,
                                           _half_shape(shared[n].shape))
    shapes = {n: W[n].shape for n in names if n not in big_names}
    shapes["conv_w"] = (CONV_HALO, CONV_CH)
    shapes["loss"] = (1,)
    sgrads = _unpack_small(sg, shapes, CONV_HALO * 4)
    loss = sgrads.pop("loss")[0]
    sgrads["conv_w"] = lax.dynamic_slice(sgrads["conv_w"], (0, chip * LANES), (CONV_K, LANES)).reshape(W["conv_w"].shape)
    spack = lambda d: _pack_small({n: d[n] for n in sgrads}, CONV_HALO)
    _, sd, snm, snv = _adamw(spack(W), spack(sgrads), spack(M), spack(V), "adamw_small", (8, LANES))
    sshapes = {n: W[n].shape for n in sgrads}
    SD, SNM, SNV = (_unpack_small(a, sshapes, CONV_HALO) for a in (sd, snm, snv))

    def collect(bigs, smalls):
        back = lambda n: (bigs[n].T if n == "w_in" else bigs[n]).reshape(W[n].shape)
        return [back(n) if n in big_names else smalls[n] for n in names]

    return (loss, grad_x.reshape(x.shape), *collect(G, sgrads), *collect(DL, SD), *collect(NM, SNM), *collect(NV, SNV))
```

```python
import functools
import math

import jax
import jax.numpy as jnp
from jax import lax
from jax.experimental import pallas as pl
from jax.experimental.pallas import tpu as pltpu

F32, BF16 = jnp.float32, jnp.bfloat16
MESH = pl.DeviceIdType.MESH

D = 1024
CONV_CH = 512
CONV_K = 31
CONV_HALO = 32
FOX_W = 512
HEAD_D = 64
N_PAIR = 4
MEM_LEN = 256
MEM_HEADS = 4
MEM_HD = 256
D_FF = 2816
FF_CHUNK = 1408
D_IN = 2568
D_IN_PAD = 2688
OFF_F = 2560
EPS = 1e-6
LANES = 128

ADAM_LR, ADAM_B1, ADAM_B2, ADAM_EPS, ADAM_WD, ADAM_STEP = 0.001, 0.9, 0.999, 1e-08, 0.01, 10

VMEM_LIMIT = 60 * 1024 * 1024

BIG = (("w_out", 256, 1024, False), ("w_mq", 256, 1024, False), ("w_mkv", 1024, 512, True),
       ("w_mo", 256, 1024, False), ("w_gu", 1024, 1408, True), ("w_down", 704, 1024, False),
       ("w_in", 642, 1024, False))

ANY = pl.BlockSpec(memory_space=pl.ANY)


def _sig(x):
    return 1.0 / (1.0 + jnp.exp(-x))


def _dot(a, b):
    return jnp.dot(a, b, preferred_element_type=F32)


def _dot_nt(a, b):
    return lax.dot_general(a, b, (((1,), (1,)), ((), ())), preferred_element_type=F32)


def _dot_tn(a, b):
    return lax.dot_general(a, b, (((0,), (0,)), ((), ())), preferred_element_type=F32)


def _split3(x):
    hi = x.astype(BF16)
    r = x - hi.astype(F32)
    mid = r.astype(BF16)
    return hi, mid, (r - mid.astype(F32)).astype(BF16)


def _dot_01(a, b):
    if a.dtype == jnp.bool_:
        return sum(_dot(a.astype(BF16), t) for t in _split3(b))
    return sum(_dot(t, b.astype(BF16)) for t in _split3(a))


def _resident(a):
    nd = a.ndim
    return pl.BlockSpec(a.shape, lambda *_: (0,) * nd, pipeline_mode=pl.Buffered(1))


def _acc_spec(shape):
    nd = len(shape)
    return pl.BlockSpec(shape, lambda *_: (0,) * nd)


def _params(n_grid):
    return pltpu.CompilerParams(dimension_semantics=("arbitrary",) * n_grid, vmem_limit_bytes=VMEM_LIMIT)


def _sds(shape, dtype):
    return jax.ShapeDtypeStruct(shape, dtype)


def _rms(x):
    r = lax.rsqrt(jnp.mean(x * x, axis=-1, keepdims=True) + EPS)
    return r, x * r


def _rms_bwd(dy, xh, r, g):
    dxh = dy * g
    dx = r * (dxh - xh * jnp.mean(dxh * xh, axis=-1, keepdims=True))
    return dx, dy * xh


def _head_expand(rows, cols):
    hd = lax.broadcasted_iota(jnp.int32, (rows, cols), 1) // HEAD_D
    hr = lax.broadcasted_iota(jnp.int32, (rows, cols), 0)
    return hd == hr


def _feat_major_spec(TB, FOX_T, nb):
    return pl.BlockSpec((1, TB // FOX_T, FOX_W, FOX_T), lambda b, j: (b, j, 0, 0))


def _fwd_in(x2, g_mix, w_int, w_ft, bf_pad, B, S, comm=None):
    T = B * S
    TB = min(512, S)
    nb = S // TB
    FOX_T = min(256, S)

    def body(x_ref, g_ref, w_ref, wf_ref, bf_ref, h_ref, u_ref, gt_ref, q_ref, k_ref, v_ref, zf_ref, c_ref, cq_ref,
             qx_ref, kx_ref, qt_ref, carry):
        j = pl.program_id(1)

        @pl.when(j == 0)
        def _():
            carry[...] = jnp.zeros_like(carry)

        _, xh = _rms(x_ref[...])
        h = (xh * g_ref[...]).astype(BF16)
        h_ref[...] = h
        u_ref[...] = _dot_nt(h, w_ref[0:512, :])
        gt_ref[...] = _dot_nt(h, w_ref[512:1024, :])
        qf = _dot_nt(h, w_ref[1024:1536, :])
        qb = qf.astype(BF16)
        kb = _dot_nt(h, w_ref[1536:2048, :]).astype(BF16)
        q_ref[...] = qb
        k_ref[...] = kb
        for t in range(TB // FOX_T):
            qt_ref[0, t] = qf[t * FOX_T:(t + 1) * FOX_T, :].T.astype(BF16)
        v_ref[...] = _dot_nt(h, w_ref[2048:2560, :]).astype(BF16)
        zf = _dot_nt(h, wf_ref[...]) + bf_ref[...]
        zf_ref[...] = zf
        lane = lax.broadcasted_iota(jnp.int32, zf.shape, 1)
        logf = jnp.where(lane < 8, jnp.minimum(zf, 0.0) - jnp.log(1.0 + jnp.exp(-jnp.abs(zf))), 0.0)
        row = lax.broadcasted_iota(jnp.int32, (TB, TB), 0)
        col = lax.broadcasted_iota(jnp.int32, (TB, TB), 1)
        c = _dot_01(row >= col, logf) + carry[0:1, :]
        carry[0:1, :] = c[TB - 1:TB, :]
        c_ref[...] = c
        cq = _dot_01(c, _head_expand(LANES, FOX_W))
        cq_ref[...] = cq
        hl = lax.broadcasted_iota(jnp.int32, (TB, LANES), 1)
        for hd in range(2 * N_PAIR):
            grp = slice((hd // 2) * LANES, (hd // 2 + 1) * LANES)
            swap = (lambda t: t) if hd % 2 == 0 else (lambda t: pltpu.roll(t, HEAD_D, 1))
            qf = swap(qb[:, grp].astype(F32) * (1.0 / math.sqrt(HEAD_D)))
            kf = swap(kb[:, grp].astype(F32))
            cv = cq[:, grp] if hd % 2 == 1 else pltpu.roll(cq[:, grp], HEAD_D, 1)
            hi = cv.astype(BF16).astype(F32)
            mid = (cv - hi).astype(BF16).astype(F32)
            lo = (cv - hi - mid).astype(BF16).astype(F32)
            pick = lambda a, b, c3, one_from, one_to: jnp.where(hl == a[0], a[1], jnp.where(hl == b[0], b[1], jnp.where(
                hl == c3[0], c3[1], jnp.where((hl >= one_from) & (hl < one_to), 1.0, 0.0))))
            qx = jnp.where(hl < HEAD_D, qf, pick((67, hi), (68, mid), (69, lo), 64, 67))
            kx = jnp.where(hl < HEAD_D, kf, pick((64, -hi), (65, -mid), (66, -lo), 67, 70))
            qx_ref[:, hd * LANES:(hd + 1) * LANES] = qx.astype(BF16)
            kx_ref[:, hd * LANES:(hd + 1) * LANES] = kx.astype(BF16)

    tok = lambda w: pl.BlockSpec((TB, w), lambda b, j: (b * nb + j, 0))
    outs = [(D, BF16), (512, F32), (512, F32), (512, BF16), (512, BF16), (512, BF16), (LANES, F32),
            (LANES, F32), (FOX_W, F32), (2 * FOX_W, BF16), (2 * FOX_W, BF16)]
    return _call(
        body, comm, name="fwd_in", grid=(B, nb),
        in_specs=[tok(D), _resident(g_mix), _resident(w_int), _resident(w_ft), _resident(bf_pad)],
        out_specs=[tok(w) for w, _ in outs] + [_feat_major_spec(TB, FOX_T, nb)],
        out_shape=[_sds((T, w), dt) for w, dt in outs] + [_sds((B, S // FOX_T, FOX_W, FOX_T), BF16)],
        scratch_shapes=[pltpu.VMEM((8, LANES), F32)],
        args=(x2, g_mix, w_int, w_ft, bf_pad))


def _head_sum(n):
    hc = lax.broadcasted_iota(jnp.int32, (n, n), 1) // HEAD_D
    hr = lax.broadcasted_iota(jnp.int32, (n, n), 0) // HEAD_D
    return hc == hr


def _layernorm_silu(y, lg, lb):
    mu = jnp.mean(y, axis=-1, keepdims=True)
    yc = y - mu
    rs = lax.rsqrt(jnp.mean(yc * yc, axis=-1, keepdims=True) + EPS)
    n = yc * rs
    l = n * lg + lb
    return rs, n, l


SUB = 8


def _shifted_copies(cat, sh, rows):
    for r in range(1, SUB):
        sh[r, 0:rows, :] = cat[r:r + rows, :]


def _tap(cat, sh, off, rows, cols=slice(None)):
    r = off % SUB
    return cat[off:off + rows, cols] if r == 0 else sh[r, off - r:off - r + rows, cols]


CONV_ROWS = 128


def _conv_pieces(CB):
    rows = min(CONV_ROWS, CB)
    return [(r0, rows, slice(c0, c0 + LANES)) for c0 in range(0, CONV_CH, LANES) for r0 in range(0, CB, rows)]


def _conv_fwd(u, gt, cw, cb, lng, lnb, B, S, comm=None):
    T = B * S
    CB = min(256, S)
    nb = S // CB

    def body(u_ref, gt_ref, w_ref, cb_ref, lg_ref, lb_ref, y_ref, co_ref, acat, ash):
        j = pl.program_id(1)

        @pl.when(j == 0)
        def _():
            acat[0:CONV_HALO, :] = jnp.zeros((CONV_HALO, CONV_CH), F32)

        acat[CONV_HALO:CONV_HALO + CB, :] = u_ref[...] * _sig(gt_ref[...])
        _shifted_copies(acat, ash, CB + CONV_HALO - SUB)
        for r0, rows, cs in _conv_pieces(CB):
            acc = jnp.zeros((rows, LANES), F32) + cb_ref[:, cs]
            for k in range(CONV_K):
                acc = acc + w_ref[k:k + 1, cs] * _tap(acat, ash, r0 + CONV_HALO - (CONV_K - 1) + k, rows, cs)
            y_ref[r0:r0 + rows, cs] = acc
        acat[0:CONV_HALO, :] = acat[CB:CB + CONV_HALO, :]
        _, _, l = _layernorm_silu(y_ref[...], lg_ref[...], lb_ref[...])
        co_ref[...] = (l * _sig(l)).astype(BF16)

    tok = lambda w: pl.BlockSpec((CB, w), lambda b, j: (b * nb + j, 0))
    return _call(
        body, comm, name="conv_fwd", grid=(B, nb),
        in_specs=[tok(CONV_CH), tok(CONV_CH), _resident(cw), _resident(cb), _resident(lng), _resident(lnb)],
        out_specs=[tok(CONV_CH), tok(CONV_CH)],
        out_shape=[_sds((T, CONV_CH), F32), _sds((T, CONV_CH), BF16)],
        scratch_shapes=[pltpu.VMEM((CONV_HALO + CB, CONV_CH), F32),
                        pltpu.VMEM((SUB, CB + CONV_HALO - SUB, CONV_CH), F32)],
        args=(u, gt, cw, cb, lng, lnb))


def _fox_fwd(qx, kx, v, cq, B, S, comm=None):
    T = B * S
    TQ = min(256, S)
    nq = S // TQ

    def body(qa_ref, qb_ref, ka_ref, kb_ref, v_ref, cq_ref, o_ref, lse_ref, s_scr, s_odd, m_scr, acc_scr):
        i = pl.program_id(2)
        lane = lax.broadcasted_iota(jnp.int32, (TQ, LANES), 1)
        lo = lane < HEAD_D
        qh = (qa_ref[...], qb_ref[...])
        kh = (ka_ref, kb_ref)
        m_scr[...] = jnp.full(m_scr.shape, -1e30, F32)
        acc_scr[...] = jnp.zeros_like(acc_scr)
        row = lax.broadcasted_iota(jnp.int32, (TQ, TQ), 0)
        col = lax.broadcasted_iota(jnp.int32, (TQ, TQ), 1)
        wide = lambda x: jnp.concatenate([x, x], axis=1) if TQ == 2 * LANES else jnp.tile(x, (1, TQ // LANES))

        def scores(j, s_buf):
            start = pl.multiple_of(j * TQ, TQ)
            for h in range(2):
                s_buf[h] = _dot_nt(qh[h], kh[h][pl.ds(start, TQ), :])

        def softmax_step(j, s_buf, diagonal):
            start = pl.multiple_of(j * TQ, TQ)
            vj = v_ref[pl.ds(start, TQ), :]
            for h in range(2):
                def logits():
                    return jnp.where(col <= row, s_buf[h], -1e30) if diagonal else s_buf[h]

                m_old = m_scr[h]
                m_new = jnp.maximum(m_old, jnp.max(logits(), axis=-1, keepdims=True))
                alpha = jnp.exp(m_old - m_new)
                m_scr[h] = m_new
                p = jnp.exp(logits() - wide(m_new)).astype(BF16)
                vx = jnp.where(lo if h == 0 else ~lo, vj, jnp.ones_like(vj))
                acc_scr[h] = alpha * acc_scr[h] + _dot(p, vx)

        def two_blocks(jj, carry):
            j = 2 * jj
            scores(j + 1, s_odd)
            softmax_step(j, s_scr, False)
            scores(j + 2, s_scr)
            softmax_step(j + 1, s_odd, False)
            return carry

        scores(0, s_scr)
        lax.fori_loop(0, i // 2, two_blocks, 0)

        @pl.when(i % 2 == 0)
        def _():
            softmax_step(i, s_scr, True)

        @pl.when(i % 2 == 1)
        def _():
            scores(i, s_odd)
            softmax_step(i - 1, s_scr, False)
            softmax_step(i, s_odd, True)

        acc = jnp.where(lo, acc_scr[0], acc_scr[1])
        den = pltpu.roll(jnp.where(lo, acc_scr[1], acc_scr[0]), HEAD_D, 1)
        o_ref[...] = acc / den
        lse_ref[...] = cq_ref[...] - (jnp.where(lo, m_scr[0], m_scr[1]) + jnp.log(den))

    qspec = pl.BlockSpec((TQ, LANES), lambda b, p, i: (b * nq + i, p))
    kspec = pl.BlockSpec((S, LANES), lambda b, p, i: (b, p))
    qhead = lambda h: pl.BlockSpec((TQ, LANES), lambda b, p, i: (b * nq + i, 2 * p + h))
    khead = lambda h: pl.BlockSpec((S, LANES), lambda b, p, i: (b, 2 * p + h))
    return _call(
        body, comm, name="fox_fwd", grid=(B, N_PAIR, nq),
        in_specs=[qhead(0), qhead(1), khead(0), khead(1), kspec, qspec],
        out_specs=[qspec, qspec],
        out_shape=[_sds((T, FOX_W), F32), _sds((T, FOX_W), F32)],
        scratch_shapes=[pltpu.VMEM((2, TQ, TQ), F32), pltpu.VMEM((2, TQ, TQ), F32),
                        pltpu.VMEM((2, TQ, LANES), F32), pltpu.VMEM((2, TQ, LANES), F32)],
        args=(qx, qx, kx, kx, v, cq))


def _mem_kv(mem2, g_mem, w_mkv, B):
    def body(m_ref, g_ref, w_ref, mn_ref, km_ref, vm_ref):
        _, xh = _rms(m_ref[...])
        mn = (xh * g_ref[...]).astype(BF16)
        mn_ref[...] = mn
        for s in range(2):
            km_ref[:, 512 * s:512 * (s + 1)] = _dot(mn, w_ref[s]).astype(BF16)
            vm_ref[:, 512 * s:512 * (s + 1)] = _dot(mn, w_ref[2 + s]).astype(BF16)

    blk = pl.BlockSpec((MEM_LEN, D), lambda b: (b, 0))
    return pl.pallas_call(
        body, name="mem_kv", grid=(B,),
        in_specs=[blk, _resident(g_mem), _resident(w_mkv)],
        out_specs=[blk, blk, blk],
        out_shape=[_sds((B * MEM_LEN, D), BF16)] * 3,
        compiler_params=_params(1),
    )(mem2, g_mem, w_mkv)


def _mem_probs(qm, km):
    ps = []
    for h in range(MEM_HEADS):
        hs = slice(h * MEM_HD, (h + 1) * MEM_HD)
        lg = _dot_nt(qm[:, hs], km[:, hs]) * (1.0 / math.sqrt(MEM_HD))
        e = jnp.exp(lg - jnp.max(lg, axis=-1, keepdims=True))
        ps.append(e / jnp.sum(e, axis=-1, keepdims=True))
    return ps


def _fwd_mid(x2, co, o, km, vm, w_out, w_mq, w_mo, g_x, B, S, comm=None):
    T = B * S
    TB = min(512, S)
    nb = S // TB

    def body(x_ref, co_ref, o_ref, km_ref, vm_ref, wo_ref, wq_ref, wm_ref, g_ref,
             x1_ref, hx_ref, qm_ref, om_ref, x2_ref, cat_ref):
        cat_ref[:, 0:CONV_CH] = co_ref[...]
        cat_ref[:, CONV_CH:D] = o_ref[...].astype(BF16)
        x1 = x_ref[...] + _dot(cat_ref[...], wo_ref[...])
        x1_ref[...] = x1
        _, xh = _rms(x1)
        hx = (xh * g_ref[...]).astype(BF16)
        hx_ref[...] = hx
        qm = _dot(hx, wq_ref[...]).astype(BF16)
        qm_ref[...] = qm
        ps = _mem_probs(qm, km_ref[...])
        vmv = vm_ref[...]
        for h in range(MEM_HEADS):
            hs = slice(h * MEM_HD, (h + 1) * MEM_HD)
            om_ref[:, hs] = _dot(ps[h].astype(BF16), vmv[:, hs]).astype(BF16)
        x2_ref[...] = x1 + _dot(om_ref[...], wm_ref[...])

    tok = lambda w: pl.BlockSpec((TB, w), lambda b, j: (b * nb + j, 0))
    memb = pl.BlockSpec((MEM_LEN, D), lambda b, j: (b, 0))
    outs = [(D, F32), (D, BF16), (D, BF16), (D, BF16), (D, F32), (D, BF16)]
    return _call(
        body, comm, name="fwd_mid", grid=(B, nb),
        in_specs=[tok(D), tok(CONV_CH), tok(FOX_W), memb, memb, _resident(w_out), _resident(w_mq), _resident(w_mo),
                  _resident(g_x)],
        out_specs=[tok(w) for w, _ in outs],
        out_shape=[_sds((T, w), dt) for w, dt in outs],
        scratch_shapes=[],
        args=(x2, co, o, km, vm, w_out, w_mq, w_mo, g_x))


def _fwd_ffn(x2, tgt, w_gu, w_down, g_ffn, g_final, T):
    TB = min(256, T)
    nb = T // TB

    def body(x_ref, t_ref, wgu_ref, wd_ref, gf_ref, gl_ref, hf_ref, gu_ref, act_ref, dx3_ref, loss_ref, dgl_ref):
        i = pl.program_id(0)

        @pl.when(i == 0)
        def _():
            loss_ref[...] = jnp.zeros_like(loss_ref)
            dgl_ref[...] = jnp.zeros_like(dgl_ref)

        x2v = x_ref[...]
        _, xh = _rms(x2v)
        hf = (xh * gf_ref[...]).astype(BF16)
        hf_ref[...] = hf
        x3 = x2v
        for ch in range(D_FF // FF_CHUNK):
            c0 = ch * FF_CHUNK
            g = _dot(hf, wgu_ref[ch])
            u = _dot(hf, wgu_ref[2 + ch])
            gu_ref[:, c0:c0 + FF_CHUNK] = g
            gu_ref[:, D_FF + c0:D_FF + c0 + FF_CHUNK] = u
            act = (g * _sig(g) * u).astype(BF16)
            act_ref[:, c0:c0 + FF_CHUNK] = act
            x3 = x3 + _dot(act, wd_ref[c0:c0 + FF_CHUNK, :])
        r3, xh3 = _rms(x3)
        gl = gl_ref[...]
        e = xh3 * gl - t_ref[...]
        loss_ref[...] += jnp.sum(e * e) * (0.5 / D)
        dy = e * (1.0 / D)
        dx3, dgl = _rms_bwd(dy, xh3, r3, gl)
        dx3_ref[...] = dx3
        dgl_ref[...] += jnp.sum(dgl, axis=0, keepdims=True)

    tok = lambda w: pl.BlockSpec((TB, w), lambda i: (i, 0))
    return pl.pallas_call(
        body, name="fwd_ffn", grid=(nb,),
        in_specs=[tok(D), tok(D), _resident(w_gu), _resident(w_down), _resident(g_ffn), _resident(g_final)],
        out_specs=[tok(D), tok(2 * D_FF), tok(D_FF), tok(D), _acc_spec((1, LANES)), _acc_spec((1, D))],
        out_shape=[_sds((T, D), BF16), _sds((T, 2 * D_FF), F32), _sds((T, D_FF), BF16), _sds((T, D), F32),
                   _sds((1, LANES), F32), _sds((1, D), F32)],
        compiler_params=_params(1),
    )(x2, tgt, w_gu, w_down, g_ffn, g_final)


def _bwd_ffn(dx3, gu, x2, w_gu, w_down, g_ffn, T):
    TB = min(256, T)
    nb = T // TB

    def body(d_ref, gu_ref, x_ref, wgu_ref, wd_ref, gf_ref, dgu_ref, dx2_ref, dgf_ref):
        i = pl.program_id(0)

        @pl.when(i == 0)
        def _():
            dgf_ref[...] = jnp.zeros_like(dgf_ref)

        dx3v = d_ref[...]
        db = dx3v.astype(BF16)
        dhf = jnp.zeros((TB, D), F32)
        for ch in range(D_FF // FF_CHUNK):
            c0 = ch * FF_CHUNK
            dact = _dot_nt(db, wd_ref[c0:c0 + FF_CHUNK, :])
            g = gu_ref[:, c0:c0 + FF_CHUNK]
            u = gu_ref[:, D_FF + c0:D_FF + c0 + FF_CHUNK]
            sg = _sig(g)
            dg = (dact * u * sg * (1.0 + g * (1.0 - sg))).astype(BF16)
            du = (dact * g * sg).astype(BF16)
            dgu_ref[:, c0:c0 + FF_CHUNK] = dg
            dgu_ref[:, D_FF + c0:D_FF + c0 + FF_CHUNK] = du
            dhf = dhf + _dot_nt(dg, wgu_ref[ch]) + _dot_nt(du, wgu_ref[2 + ch])
        r2, xh2 = _rms(x_ref[...])
        dx, dg_tok = _rms_bwd(dhf, xh2, r2, gf_ref[...])
        dx2_ref[...] = dx3v + dx
        dgf_ref[...] += jnp.sum(dg_tok, axis=0, keepdims=True)

    tok = lambda w: pl.BlockSpec((TB, w), lambda i: (i, 0))
    return pl.pallas_call(
        body, name="bwd_ffn", grid=(nb,),
        in_specs=[tok(D), tok(2 * D_FF), tok(D), _resident(w_gu), _resident(w_down), _resident(g_ffn)],
        out_specs=[tok(2 * D_FF), tok(D), _acc_spec((1, D))],
        out_shape=[_sds((T, 2 * D_FF), BF16), _sds((T, D), F32), _sds((1, D), F32)],
        compiler_params=_params(1),
    )(dx3, gu, x2, w_gu, w_down, g_ffn)


def _bwd_mid(dx2, x1, qm, km, vm, o, w_mo, w_mq, w_out, g_x, B, S, comm=None):
    T = B * S
    TB = min(512, S)
    nb = S // TB
    FOX_T = min(256, S)
    inv = 1.0 / math.sqrt(MEM_HD)

    def body(d_ref, x1_ref, qm_ref, km_ref, vm_ref, o_ref, wm_ref, wq_ref, wo_ref, g_ref,
             dx1_ref, dqm_ref, dco_ref, do_ref, dd_ref, dkm_ref, dvm_ref, dgx_ref, dot_ref):
        b = pl.program_id(0)
        j = pl.program_id(1)

        @pl.when((b == 0) & (j == 0))
        def _():
            dgx_ref[...] = jnp.zeros_like(dgx_ref)

        @pl.when(j == 0)
        def _():
            dkm_ref[...] = jnp.zeros_like(dkm_ref)
            dvm_ref[...] = jnp.zeros_like(dvm_ref)

        dx2v = d_ref[...]
        dom = _dot_nt(dx2v.astype(BF16), wm_ref[...]).astype(BF16)
        qmv = qm_ref[...]
        kmv = km_ref[...]
        vmv = vm_ref[...]
        ps = _mem_probs(qmv, kmv)
        for h in range(MEM_HEADS):
            hs = slice(h * MEM_HD, (h + 1) * MEM_HD)
            p = ps[h]
            dp = _dot_nt(dom[:, hs], vmv[:, hs])
            ds = (p * (dp - jnp.sum(p * dp, axis=-1, keepdims=True))).astype(BF16)
            dqm_ref[:, hs] = (_dot(ds, kmv[:, hs]) * inv).astype(BF16)
            dkm_ref[:, hs] += _dot_tn(ds, qmv[:, hs]) * inv
            dvm_ref[:, hs] += _dot_tn(p.astype(BF16), dom[:, hs])
        dhx = _dot_nt(dqm_ref[...], wq_ref[...])
        r1, xh1 = _rms(x1_ref[...])
        dx, dg_tok = _rms_bwd(dhx, xh1, r1, g_ref[...])
        dx1 = dx2v + dx
        dx1_ref[...] = dx1
        dgx_ref[...] += jnp.sum(dg_tok, axis=0, keepdims=True)
        d1b = dx1.astype(BF16)
        dco_ref[...] = _dot_nt(d1b, wo_ref[0:CONV_CH, :])
        do = _dot_nt(d1b, wo_ref[CONV_CH:D, :])
        dob = do.astype(BF16)
        do_ref[...] = dob
        for t in range(TB // FOX_T):
            dot_ref[0, t] = do[t * FOX_T:(t + 1) * FOX_T, :].T.astype(BF16)
        dd_ref[...] = _dot_01(dob.astype(F32) * o_ref[...], _head_sum(FOX_W))

    tok = lambda w: pl.BlockSpec((TB, w), lambda b, j: (b * nb + j, 0))
    memb = pl.BlockSpec((MEM_LEN, D), lambda b, j: (b, 0))
    outs = [(D, F32), (D, BF16), (CONV_CH, F32), (FOX_W, BF16), (FOX_W, F32)]
    return _call(
        body, comm, name="bwd_mid", grid=(B, nb),
        in_specs=[tok(D), tok(D), tok(D), memb, memb, tok(FOX_W), _resident(w_mo), _resident(w_mq), _resident(w_out),
                  _resident(g_x)],
        out_specs=[tok(w) for w, _ in outs] + [memb, memb, _acc_spec((1, D)), _feat_major_spec(TB, FOX_T, nb)],
        out_shape=[_sds((T, w), dt) for w, dt in outs] + [_sds((B * MEM_LEN, D), F32)] * 2 + [_sds((1, D), F32)]
        + [_sds((B, S // FOX_T, FOX_W, FOX_T), BF16)],
        scratch_shapes=[],
        args=(dx2, x1, qm, km, vm, o, w_mo, w_mq, w_out, g_x))


def _mem_bwd(dkm, dvm, mem2, w_mkv, g_mem, B):
    def body(dk_ref, dv_ref, m_ref, w_ref, g_ref, dkv_ref, dg_ref):
        b = pl.program_id(0)

        @pl.when(b == 0)
        def _():
            dg_ref[...] = jnp.zeros_like(dg_ref)

        dk = dk_ref[...].astype(BF16)
        dv = dv_ref[...].astype(BF16)
        dkv_ref[:, 0:D] = dk
        dkv_ref[:, D:2 * D] = dv
        dmn = jnp.zeros((MEM_LEN, D), F32)
        for s in range(2):
            dmn = dmn + _dot_nt(dk[:, 512 * s:512 * (s + 1)], w_ref[s]) + _dot_nt(dv[:, 512 * s:512 * (s + 1)], w_ref[2 + s])
        _, xh = _rms(m_ref[...])
        dg_ref[...] += jnp.sum(dmn * xh, axis=0, keepdims=True)

    blk = pl.BlockSpec((MEM_LEN, D), lambda b: (b, 0))
    return pl.pallas_call(
        body, name="mem_bwd", grid=(B,),
        in_specs=[blk, blk, blk, _resident(w_mkv), _resident(g_mem)],
        out_specs=[pl.BlockSpec((MEM_LEN, 2 * D), lambda b: (b, 0)), _acc_spec((1, D))],
        out_shape=[_sds((B * MEM_LEN, 2 * D), BF16), _sds((1, D), F32)],
        compiler_params=_params(1),
    )(dkm, dvm, mem2, w_mkv, g_mem)


def _fox_bwd(q, k, v, do, bias, dd, ckT, qT, doT, B, S, comm=None):
    T = B * S
    TK = min(256, S)
    nk = S // TK
    scale = 1.0 / math.sqrt(HEAD_D)

    def body(q_ref, k_ref, v_ref, do_ref, bias_ref, dd_ref, ck_ref, qt_ref, dot_ref, dq_ref, dk_ref, dv_ref, dc_ref,
             dcq_ref, dq_acc, rs_acc, s_scr, dp_scr, s_odd, dp_odd, dk_acc, dv_acc, dc_acc):
        j = pl.program_id(2)

        @pl.when(j == 0)
        def _():
            dq_acc[...] = jnp.zeros_like(dq_acc)
            rs_acc[...] = jnp.zeros_like(rs_acc)

        dk_acc[...] = jnp.zeros_like(dk_acc)
        dv_acc[...] = jnp.zeros_like(dv_acc)
        dc_acc[...] = jnp.zeros_like(dc_acc)
        lane = lax.broadcasted_iota(jnp.int32, (TK, LANES), 1)
        lo = lane < HEAD_D
        ks = k_ref[...] * jnp.asarray(scale, BF16)
        v2 = v_ref[...]
        zero = jnp.zeros_like(ks)
        kh = (jnp.where(lo, ks, zero), jnp.where(lo, zero, ks))
        vh = (jnp.where(lo, v2, zero), jnp.where(lo, zero, v2))
        kstart = pl.multiple_of(j * TK, TK)
        ckh = tuple(ck_ref[0, 0, h:h + 1, pl.ds(kstart, TK)] for h in range(2))
        row = lax.broadcasted_iota(jnp.int32, (TK, TK), 0)
        col = lax.broadcasted_iota(jnp.int32, (TK, TK), 1)
        wide = lambda x: jnp.concatenate([x, x], axis=1) if TK == 2 * LANES else jnp.tile(x, (1, TK // LANES))

        def scores(i, s_buf, dp_buf):
            start = pl.multiple_of(i * TK, TK)
            qi = q_ref[pl.ds(start, TK), :]
            doi = do_ref[pl.ds(start, TK), :]
            for h in range(2):
                s_buf[h] = _dot_nt(qi, kh[h])
                dp_buf[h] = _dot_nt(doi, vh[h])

        def grads(i, s_buf, dp_buf, diagonal):
            start = pl.multiple_of(i * TK, TK)
            bias2 = bias_ref[pl.ds(start, TK), :]
            dd2 = dd_ref[pl.ds(start, TK), :]
            for h in range(2):
                hc = slice(h * HEAD_D, h * HEAD_D + 1)
                bias = jnp.broadcast_to(bias2[:, hc], (TK, LANES))
                ddh = jnp.broadcast_to(dd2[:, hc], (TK, LANES))
                p = jnp.exp((s_buf[h] - ckh[h]) + wide(bias))
                if diagonal:
                    p = jnp.where(col <= row, p, 0.0)
                ds = p * (dp_buf[h] - wide(ddh))
                dc_acc[h, 0:1, :] += jnp.sum(ds, axis=0, keepdims=True)
                rs_acc[h, pl.ds(start, TK), :] += jnp.sum(ds, axis=1, keepdims=True)
                pb = p.astype(BF16)
                dsb = ds.astype(BF16)
                feat = slice(h * HEAD_D, (h + 1) * HEAD_D)
                dv_acc[feat, :] += _dot(dot_ref[0, i, feat, :], pb)
                dk_acc[feat, :] += _dot(qt_ref[0, i, feat, :], dsb)
                dq_acc[pl.ds(start, TK), :] += _dot(dsb, kh[h])

        n_off = nk - 1 - j
        block = lambda t: jnp.where(t < n_off, j + 1 + t, j)

        def two_blocks(tt, carry):
            t = 2 * tt
            scores(block(t + 1), s_odd, dp_odd)
            grads(block(t), s_scr, dp_scr, False)
            scores(block(t + 2), s_scr, dp_scr)
            grads(block(t + 1), s_odd, dp_odd, False)
            return carry

        scores(block(0), s_scr, dp_scr)
        lax.fori_loop(0, n_off // 2, two_blocks, 0)

        @pl.when(n_off % 2 == 0)
        def _():
            grads(j, s_scr, dp_scr, True)

        @pl.when(n_off % 2 == 1)
        def _():
            scores(j, s_odd, dp_odd)
            grads(nk - 1, s_scr, dp_scr, False)
            grads(j, s_odd, dp_odd, True)

        dk_ref[...] = (dk_acc[...].T * scale).astype(BF16)
        dv_ref[...] = dv_acc[...].T.astype(BF16)
        sub = lax.broadcasted_iota(jnp.int32, (8, TK), 0)
        dca = dc_acc[0, 0:1, :]
        dcb = dc_acc[1, 0:1, :]
        dc_ref[0, 0] = jnp.where(sub == 0, -dca, jnp.where(sub == 1, -dcb, 0.0))

        @pl.when(j == nk - 1)
        def _():
            dq_ref[...] = dq_acc[...].astype(BF16)
            lo_s = lax.broadcasted_iota(jnp.int32, (S, LANES), 1) < HEAD_D
            dcq_ref[...] = jnp.where(lo_s, rs_acc[0], rs_acc[1])

    full = pl.BlockSpec((S, LANES), lambda b, p, j: (b, p))
    blk = pl.BlockSpec((TK, LANES), lambda b, p, j: (b * nk + j, p))
    featT = pl.BlockSpec((1, nk, LANES, TK), lambda b, p, j: (b, 0, p, 0))
    return _call(
        body, comm, name="fox_bwd", grid=(B, N_PAIR, nk),
        in_specs=[full, blk, blk, full, full, full, pl.BlockSpec((1, 1, 8, S), lambda b, p, j: (b, p, 0, 0)),
                  featT, featT],
        out_specs=[full, blk, blk, pl.BlockSpec((1, 1, 8, TK), lambda b, p, j: (b, p, 0, j)), full],
        out_shape=[_sds((T, FOX_W), BF16), _sds((T, FOX_W), BF16), _sds((T, FOX_W), BF16),
                   _sds((B, N_PAIR, 8, S), F32), _sds((T, FOX_W), F32)],
        scratch_shapes=[pltpu.VMEM((S, LANES), F32), pltpu.VMEM((2, S, 1), F32),
                        pltpu.VMEM((2, TK, TK), F32), pltpu.VMEM((2, TK, TK), F32),
                        pltpu.VMEM((2, TK, TK), F32), pltpu.VMEM((2, TK, TK), F32),
                        pltpu.VMEM((LANES, TK), F32), pltpu.VMEM((LANES, TK), F32), pltpu.VMEM((2, 8, TK), F32)],
        args=(q, k, v, do, bias, dd, ckT, qT, doT))


def _fgate_bwd(dc8, zf, B, S):
    T = B * S
    TB = min(512, S)
    nb = S // TB

    def body(dc_ref, zf_ref, dzf_ref, dbf_ref, carry):
        b = pl.program_id(0)
        j = pl.program_id(1)

        @pl.when((b == 0) & (j == 0))
        def _():
            dbf_ref[...] = jnp.zeros_like(dbf_ref)

        @pl.when(j == 0)
        def _():
            carry[...] = jnp.zeros_like(carry)

        dc = dc_ref[...]
        row = lax.broadcasted_iota(jnp.int32, (TB, TB), 0)
        col = lax.broadcasted_iota(jnp.int32, (TB, TB), 1)
        dlogf = _dot_01(col >= row, dc) + carry[0:1, :]
        carry[0:1, :] = dlogf[0:1, :]
        lane = lax.broadcasted_iota(jnp.int32, dc.shape, 1)
        dzf = jnp.where(lane < 8, dlogf * _sig(-zf_ref[...]), 0.0)
        dzf_ref[...] = dzf.astype(BF16)
        dbf_ref[...] += jnp.sum(dzf, axis=0, keepdims=True)

    tok = pl.BlockSpec((TB, LANES), lambda b, j: (b * nb + (nb - 1 - j), 0))
    return pl.pallas_call(
        body, name="fgate_bwd", grid=(B, nb),
        in_specs=[tok, tok],
        out_specs=[tok, _acc_spec((1, LANES))],
        out_shape=[_sds((T, LANES), BF16), _sds((1, LANES), F32)],
        scratch_shapes=[pltpu.VMEM((8, LANES), F32)],
        compiler_params=_params(2),
    )(dc8, zf)


def _conv_bwd(dco, y, u, gt, cw, lng, lnb, B, S, comm=None):
    T = B * S
    CB = min(256, S)
    nb = S // CB
    hb = CB // CONV_HALO

    def body(dco_ref, y_ref, u_ref, gt_ref, up_ref, gp_ref, w_ref, lg_ref, lb_ref,
             du_ref, dgt_ref, dw_ref, vec_ref, acat, dycat, ash, dysh):
        b = pl.program_id(0)
        j = pl.program_id(1)
        jr = nb - 1 - j

        @pl.when((b == 0) & (j == 0))
        def _():
            dw_ref[...] = jnp.zeros_like(dw_ref)
            vec_ref[...] = jnp.zeros_like(vec_ref)

        @pl.when(j == 0)
        def _():
            dycat[CB:CB + CONV_HALO, :] = jnp.zeros((CONV_HALO, CONV_CH), F32)

        lg = lg_ref[...]
        rs, n, l = _layernorm_silu(y_ref[...], lg, lb_ref[...])
        sg = _sig(l)
        dl = dco_ref[...] * (sg * (1.0 + l * (1.0 - sg)))
        dn = dl * lg
        dy = rs * (dn - jnp.mean(dn, axis=-1, keepdims=True) - n * jnp.mean(dn * n, axis=-1, keepdims=True))
        vec_ref[0:1, :] += jnp.sum(dy, axis=0, keepdims=True)
        vec_ref[1:2, :] += jnp.sum(dl * n, axis=0, keepdims=True)
        vec_ref[2:3, :] += jnp.sum(dl, axis=0, keepdims=True)
        dycat[0:CB, :] = dy
        acat[0:CONV_HALO, :] = jnp.where(jr > 0, up_ref[...] * _sig(gp_ref[...]), 0.0)
        acat[CONV_HALO:CONV_HALO + CB, :] = u_ref[...] * _sig(gt_ref[...])
        _shifted_copies(acat, ash, CB + CONV_HALO - SUB)
        _shifted_copies(dycat, dysh, CB + CONV_HALO - SUB)
        for r0, rows, cs in _conv_pieces(CB):
            dyp = dycat[r0:r0 + rows, cs]
            da = jnp.zeros((rows, LANES), F32)
            for k in range(CONV_K):
                da = da + w_ref[k:k + 1, cs] * _tap(dycat, dysh, r0 + CONV_K - 1 - k, rows, cs)
                dw_ref[k:k + 1, cs] += jnp.sum(dyp * _tap(acat, ash, r0 + CONV_HALO - (CONV_K - 1) + k, rows, cs),
                                               axis=0, keepdims=True)
            uv = u_ref[r0:r0 + rows, cs]
            sgt = _sig(gt_ref[r0:r0 + rows, cs])
            du_ref[r0:r0 + rows, cs] = (da * sgt).astype(BF16)
            dgt_ref[r0:r0 + rows, cs] = (da * uv * sgt * (1.0 - sgt)).astype(BF16)
        dycat[CB:CB + CONV_HALO, :] = dycat[0:CONV_HALO, :]

    tok = lambda w: pl.BlockSpec((CB, w), lambda b, j: (b * nb + (nb - 1 - j), 0))
    prev = pl.BlockSpec((CONV_HALO, CONV_CH), lambda b, j: (jnp.maximum((b * nb + (nb - 1 - j)) * hb - 1, 0), 0))
    return _call(
        body, comm, name="conv_bwd", grid=(B, nb),
        in_specs=[tok(CONV_CH), tok(CONV_CH), tok(CONV_CH), tok(CONV_CH), prev, prev, _resident(cw), _resident(lng),
                  _resident(lnb)],
        out_specs=[tok(CONV_CH), tok(CONV_CH), _acc_spec((CONV_HALO, CONV_CH)), _acc_spec((8, CONV_CH))],
        out_shape=[_sds((T, CONV_CH), BF16), _sds((T, CONV_CH), BF16), _sds((CONV_HALO, CONV_CH), F32),
                   _sds((8, CONV_CH), F32)],
        scratch_shapes=[pltpu.VMEM((CONV_HALO + CB, CONV_CH), F32), pltpu.VMEM((CB + CONV_HALO, CONV_CH), F32),
                        pltpu.VMEM((SUB, CB + CONV_HALO - SUB, CONV_CH), F32),
                        pltpu.VMEM((SUB, CB + CONV_HALO - SUB, CONV_CH), F32)],
        args=(dco, y, u, gt, u, gt, cw, lng, lnb))


def _bwd_in(dz, w_int, w_ft, x2, dx1, g_mix, T, comm=None):
    TB = min(512, T)
    nb = T // TB

    def body(dz_ref, w_ref, wf_ref, x_ref, d1_ref, g_ref, gx_ref, dg_ref):
        i = pl.program_id(0)

        @pl.when(i == 0)
        def _():
            dg_ref[...] = jnp.zeros_like(dg_ref)

        dh = _dot(dz_ref[:, 0:OFF_F], w_ref[0:OFF_F, :]) + _dot(dz_ref[:, OFF_F:D_IN_PAD], wf_ref[...])
        r0, xh0 = _rms(x_ref[...])
        dx, dg_tok = _rms_bwd(dh, xh0, r0, g_ref[...])
        gx_ref[...] = d1_ref[...] + dx
        dg_ref[...] += jnp.sum(dg_tok, axis=0, keepdims=True)

    tok = lambda w: pl.BlockSpec((TB, w), lambda i: (i, 0))
    return _call(
        body, comm, name="bwd_in", grid=(nb,),
        in_specs=[tok(D_IN_PAD), _resident(w_int), _resident(w_ft), tok(D), tok(D), _resident(g_mix)],
        out_specs=[tok(D), _acc_spec((1, D))],
        out_shape=[_sds((T, D), F32), _sds((1, D), F32)],
        scratch_shapes=[],
        args=(dz, w_int, w_ft, x2, dx1, g_mix))


def _dw(a, b, name, tn, slabs=False, tk=None, rows=None):
    T, K = a.shape
    N = b.shape[1]
    tk = tk or (K if K <= 1024 else K // 2)
    tt = min(1024, T)
    nt = T // tt

    def body(a_ref, b_ref, o_ref, acc):
        t = pl.program_id(2)

        @pl.when(t == 0)
        def _():
            acc[...] = jnp.zeros_like(acc)

        acc[...] += _dot_tn(a_ref[...].astype(BF16), b_ref[...].astype(BF16))

        @pl.when(t == nt - 1)
        def _():
            o_ref[...] = acc[...]

    return pl.pallas_call(
        body, name=name, grid=(K // tk, N // tn, nt),
        in_specs=[pl.BlockSpec((tt, tk), lambda i, j, t: (t, i)), pl.BlockSpec((tt, tn), lambda i, j, t: (t, j))],
        out_specs=(pl.BlockSpec((None, tk, tn), lambda i, j, t: (j, i, 0)) if slabs
                   else pl.BlockSpec((tk, tn), lambda i, j, t: (i, j))),
        out_shape=_sds((N // tn, K, tn) if slabs else (rows or K, N), F32),
        scratch_shapes=[pltpu.VMEM((tk, tn), F32)],
        compiler_params=_params(3),
    )(a, b)


def _pos():
    return lax.axis_index("x"), lax.axis_index("y"), lax.axis_index("c")


def _remote(src, dst, ssem, rsem, to):
    return pltpu.make_async_remote_copy(src_ref=src, dst_ref=dst, send_sem=ssem, recv_sem=rsem, device_id=to,
                                        device_id_type=MESH)


def _split_axis(shape):
    return 0 if shape[0] % 32 == 0 else 1


def _half_shape(shape, parts=2):
    return (shape[0] // parts, shape[1]) if _split_axis(shape) == 0 else (shape[0], shape[1] // parts)


def _half(shape, c):
    R, C = shape
    if _split_axis(shape) == 0:
        return (pl.ds(pl.multiple_of(c * (R // 2), 16), R // 2), slice(None))
    return (slice(None), pl.ds(pl.multiple_of(c * (C // 2), LANES), C // 2))


def _half_block(shape, parts, lead, which):
    blk = _half_shape(shape, parts)
    idx = (which, 0) if _split_axis(shape) == 0 else (0, which)
    return blk, tuple(lead) + idx


class _Comm:
    def __init__(self, ins, out_shapes, sems, start, finish):
        self.ins, self.out_shapes, self.sems, self.start, self.finish = list(ins), list(out_shapes), list(sems), start, finish


def _ag_comm(shards):
    n = len(shards)

    def parts(ins, outs, sems):
        send_sems, recv_sems, local_sems = sems
        x, y, c = _pos()
        me, sib = (x, y, c), (x, y, 1 - c)
        chips = [(1 - x, y), (x, 1 - y), (1 - x, 1 - y)]

        def rows(w, px, py, pc):
            return outs[w].at[(2 * px + py,) + _half(shards[w].shape, pc)]

        def copy(w, k, block, to, src=None):
            return _remote(rows(w, *block) if src is None else src, rows(w, *block), send_sems.at[w, k],
                           recv_sems.at[w, k], to)

        mine, first = [], []
        for w in range(n):
            src = ins[w].at[_half(shards[w].shape, c)]
            mine.append(pltpu.make_async_copy(src, rows(w, *me), local_sems.at[w]))
            first += [copy(w, 0, me, sib, src=src)] + [copy(w, 1 + j, me, (*chip, c), src=src) for j, chip in enumerate(chips)]
        return c, me, sib, chips, copy, mine, first

    def start(ins, outs, sems):
        _, _, _, _, _, mine, first = parts(ins, outs, sems)
        for cp in mine + first:
            cp.start()

    def finish(ins, outs, sems):
        c, me, sib, chips, copy, mine, first = parts(ins, outs, sems)
        passed = []
        for w in range(n):
            for j, chip in enumerate(chips):
                copy(w, 1 + j, (*chip, c), me).wait_recv()
                passed.append(copy(w, 4 + j, (*chip, c), sib))
                passed[-1].start()
        for w in range(n):
            copy(w, 0, sib, me).wait_recv()
            for j, chip in enumerate(chips):
                copy(w, 4 + j, (*chip, 1 - c), me).wait_recv()
        for cp in first + passed:
            cp.wait_send()
        for cp in mine:
            cp.wait()

    D7 = pltpu.SemaphoreType.DMA((n, 7))
    return _Comm(shards, [_sds((4,) + s.shape, s.dtype) for s in shards], [D7, D7, pltpu.SemaphoreType.DMA((n,))],
                 start, finish)


def _sibling_comm(gs):
    n = len(gs)

    def copies(ins, outs, sems):
        send_sems, recv_sems = sems
        x, y, c = _pos()
        return [_remote(ins[w].at[(s,) + _half(gs[w].shape[1:], 1 - c)], outs[w].at[s], send_sems.at[w, s],
                        recv_sems.at[w, s], (x, y, 1 - c)) for w in range(n) for s in range(4)]

    def start(ins, outs, sems):
        for cp in copies(ins, outs, sems):
            cp.start()

    def finish(ins, outs, sems):
        for cp in copies(ins, outs, sems):
            cp.wait()

    D4 = pltpu.SemaphoreType.DMA((n, 4))
    return _Comm(gs, [_sds((4,) + _half_shape(g.shape[1:]), F32) for g in gs], [D4, D4], start, finish)


def _ici_comm(pbs):
    n = len(pbs)

    def copies(ins, outs, sems):
        send_sems, recv_sems = sems
        x, y, c = _pos()
        return [_remote(ins[w].at[2 * tx + ty], outs[w].at[j], send_sems.at[w, j], recv_sems.at[w, j], (tx, ty, c))
                for w in range(n) for j, (tx, ty) in enumerate([(1 - x, y), (x, 1 - y), (1 - x, 1 - y)])]

    def start(ins, outs, sems):
        for cp in copies(ins, outs, sems):
            cp.start()

    def finish(ins, outs, sems):
        for cp in copies(ins, outs, sems):
            cp.wait()

    D3 = pltpu.SemaphoreType.DMA((n, 3))
    return _Comm(pbs, [_sds((3,) + p.shape[1:], BF16) for p in pbs], [D3, D3], start, finish)


def _join(*comms):
    counts = [(len(c.ins), len(c.out_shapes), len(c.sems)) for c in comms]

    def each(which):
        def run(ins, outs, sems):
            i = o = k = 0
            for c, (ni, no, nk) in zip(comms, counts):
                getattr(c, which)(ins[i:i + ni], outs[o:o + no], sems[k:k + nk])
                i, o, k = i + ni, o + no, k + nk
        return run

    return _Comm(sum((c.ins for c in comms), []), sum((c.out_shapes for c in comms), []),
                 sum((c.sems for c in comms), []), each("start"), each("finish"))


def _run_comm(comm, name):
    ni, no = len(comm.ins), len(comm.out_shapes)

    def body(*refs):
        ins, outs, sems = refs[:ni], refs[ni:ni + no], refs[ni + no:]
        comm.start(ins, outs, sems)
        comm.finish(ins, outs, sems)

    return pl.pallas_call(body, name=name, out_shape=comm.out_shapes, in_specs=[ANY] * ni, out_specs=[ANY] * no,
                          scratch_shapes=comm.sems)(*comm.ins)


def _call(body, comm, *, name, grid, in_specs, out_specs, out_shape, scratch_shapes, args):
    n_grid = len(grid)
    if comm is None:
        res = pl.pallas_call(body, name=name, grid=grid, in_specs=in_specs, out_specs=out_specs, out_shape=out_shape,
                             scratch_shapes=scratch_shapes, compiler_params=_params(n_grid))(*args)
        return list(res), []
    n_in, n_out, n_scr = len(in_specs), len(out_specs), len(scratch_shapes)
    ni, no = len(comm.ins), len(comm.out_shapes)

    def carried(*refs):
        ins, refs = refs[:n_in], refs[n_in:]
        cins, refs = refs[:ni], refs[ni:]
        outs, refs = refs[:n_out], refs[n_out:]
        couts, refs = refs[:no], refs[no:]
        scr, csems = refs[:n_scr], refs[n_scr:]
        ids = [pl.program_id(ax) for ax in range(n_grid)]
        first = functools.reduce(jnp.logical_and, [i == 0 for i in ids])
        last = functools.reduce(jnp.logical_and, [i == g - 1 for i, g in zip(ids, grid)])

        @pl.when(first)
        def _():
            comm.start(cins, couts, csems)

        body(*ins, *outs, *scr)

        @pl.when(last)
        def _():
            comm.finish(cins, couts, csems)

    res = pl.pallas_call(
        carried, name=name, grid=grid, in_specs=list(in_specs) + [ANY] * ni, out_specs=list(out_specs) + [ANY] * no,
        out_shape=list(out_shape) + comm.out_shapes, scratch_shapes=list(scratch_shapes) + comm.sems,
        compiler_params=_params(n_grid))(*args, *comm.ins)
    return list(res[:n_out]), list(res[n_out:])


def _small_allreduce(v, name, halves=()):
    P = v.shape[0]
    n = len(halves)
    vm = pl.BlockSpec(memory_space=pltpu.VMEM)

    def body(v_ref, *refs):
        o_ref, outs = refs[n], refs[n + 1:2 * n + 1]
        gath, send_sems, recv_sems, half_send, half_recv = refs[2 * n + 1:]
        x, y, c = _pos()
        me = 4 * x + 2 * y + c
        gath[me] = v_ref[...]
        cps = []
        for r in range(1, 8):
            tx = (1 - x) if r & 4 else x
            ty = (1 - y) if r & 2 else y
            tc = (1 - c) if r & 1 else c
            cps.append(_remote(v_ref, gath.at[me], send_sems.at[r - 1], recv_sems.at[r - 1], (tx, ty, tc)))
        for w in range(n):
            mine = outs[w].at[_half(halves[w].shape, c)]
            cps.append(_remote(mine, mine, half_send.at[w], half_recv.at[w], (x, y, 1 - c)))
        for cp in cps:
            cp.start()
        for cp in cps:
            cp.wait()
        acc = gath[0]
        for d in range(1, 8):
            acc = acc + gath[d]
        o_ref[...] = acc

    res = pl.pallas_call(
        body, name=name, out_shape=[_sds((P, LANES), F32)] + [_sds(g.shape, F32) for g in halves],
        in_specs=[vm] + [ANY] * n, out_specs=[vm] + [ANY] * n, input_output_aliases={1 + w: 1 + w for w in range(n)},
        scratch_shapes=[pltpu.VMEM((8, P, LANES), F32), pltpu.SemaphoreType.DMA((7,)), pltpu.SemaphoreType.DMA((7,)),
                        pltpu.SemaphoreType.DMA((max(n, 1),)), pltpu.SemaphoreType.DMA((max(n, 1),))],
    )(v, *halves)
    return res[0], list(res[1:])


def _chip_sum(g, rcv, pos, name):
    shard = g.shape[1:]
    hs = _half_shape(shard)

    def body(pos_ref, g_ref, r_ref, o_ref):
        o_ref[...] = (g_ref[...] + r_ref[...]).astype(BF16)

    return pl.pallas_call(
        body, name=name, out_shape=_sds((4,) + hs, BF16),
        grid_spec=pltpu.PrefetchScalarGridSpec(
            num_scalar_prefetch=1, grid=(4,),
            in_specs=[pl.BlockSpec((1,) + hs, lambda s, pos: _half_block(shard, 2, (s,), pos[0])[1]),
                      pl.BlockSpec((1,) + hs, lambda s, pos: (s, 0, 0))],
            out_specs=pl.BlockSpec((1,) + hs, lambda s, pos: (s, 0, 0))),
        compiler_params=_params(1),
    )(pos, g, rcv)


def _final_sum(g, rcv, rc, pos, name):
    shard = g.shape[1:]
    qs = _half_shape(shard, 4)

    def body(pos_ref, g_ref, r_ref, rc_ref, o_ref):
        acc = g_ref[0] + r_ref[0]
        for j in range(3):
            acc = acc + rc_ref[j].astype(F32)
        o_ref[...] = acc

    return pl.pallas_call(
        body, name=name, out_shape=_sds(shard, F32),
        grid_spec=pltpu.PrefetchScalarGridSpec(
            num_scalar_prefetch=1, grid=(2,),
            in_specs=[pl.BlockSpec((1,) + qs, lambda i, pos: _half_block(shard, 4, (pos[1],), pos[0] * 2 + i)[1]),
                      pl.BlockSpec((1,) + qs, lambda i, pos: _half_block(shard, 4, (pos[1],), i)[1]),
                      pl.BlockSpec((3,) + qs, lambda i, pos: _half_block(shard, 4, (0,), i)[1])],
            out_specs=pl.BlockSpec(qs, lambda i, pos: _half_block(shard, 4, (), pos[0] * 2 + i)[1])),
        compiler_params=_params(1),
    )(pos, g, rcv, rc)


def _adamw_math(w, g, m, v):
    m = ADAM_B1 * m + (1.0 - ADAM_B1) * g
    v = ADAM_B2 * v + (1.0 - ADAM_B2) * (g * g)
    m_hat = m / (1.0 - ADAM_B1 ** ADAM_STEP)
    v_hat = v / (1.0 - ADAM_B2 ** ADAM_STEP)
    delta = -ADAM_LR * (m_hat / (jnp.sqrt(v_hat) + ADAM_EPS) + ADAM_WD * w)
    return delta, m, v


def _adamw(w, g, m, v, name, blk_shape):
    R, C = w.shape

    def body(w_ref, g_ref, m_ref, v_ref, go_ref, d_ref, nm_ref, nv_ref):
        g = g_ref[...]
        d, nm, nv = _adamw_math(w_ref[...], g, m_ref[...], v_ref[...])
        go_ref[...] = g
        d_ref[...] = d
        nm_ref[...] = nm
        nv_ref[...] = nv

    blk = pl.BlockSpec(blk_shape, lambda i, j: (i, j))
    return pl.pallas_call(
        body, name=name, grid=(R // blk_shape[0], C // blk_shape[1]), in_specs=[blk] * 4, out_specs=[blk] * 4,
        out_shape=[_sds((R, C), F32)] * 4, compiler_params=_params(2),
    )(w, g, m, v)


SMALL = (("g_mix", 8), ("b_f", 8), ("conv_w", None), ("conv_b", 8), ("ln_g", 8), ("ln_b", 8), ("g_x", 8), ("g_mem", 8),
         ("g_ffn", 8), ("g_final", 8), ("loss", 8))


def _pack_small(parts, conv_rows):
    rows = []
    for name, n in SMALL:
        if name not in parts:
            continue
        n = conv_rows if n is None else n
        flat = parts[name].reshape(-1).astype(F32)
        flat = jnp.pad(flat, (0, n * LANES - flat.shape[0]))
        rows.append(flat.reshape(n, LANES))
    return jnp.concatenate(rows, axis=0)


def _unpack_small(p, shapes, conv_rows):
    out, off = {}, 0
    for name, n in SMALL:
        if name not in shapes:
            continue
        n = conv_rows if n is None else n
        size = math.prod(shapes[name])
        out[name] = p[off:off + n].reshape(-1)[:size].reshape(shapes[name])
        off += n
    return out


def kernel(x, mem, g_mix, w_in, b_f, conv_w, conv_b, ln_g, ln_b, w_out, g_x, g_mem, w_mq, w_mkv, w_mo, g_ffn, w_gu, w_down, g_final, loss_target, m_g_mix, m_w_in, m_b_f, m_conv_w, m_conv_b, m_ln_g, m_ln_b, m_w_out, m_g_x, m_g_mem, m_w_mq, m_w_mkv, m_w_mo, m_g_ffn, m_w_gu, m_w_down, m_g_final, v_g_mix, v_w_in, v_b_f, v_conv_w, v_conv_b, v_ln_g, v_ln_b, v_w_out, v_g_x, v_g_mem, v_w_mq, v_w_mkv, v_w_mo, v_g_ffn, v_w_gu, v_w_down, v_g_final):
    names = ["g_mix", "w_in", "b_f", "conv_w", "conv_b", "ln_g", "ln_b", "w_out", "g_x", "g_mem", "w_mq", "w_mkv",
             "w_mo", "g_ffn", "w_gu", "w_down", "g_final"]
    W = dict(zip(names, [g_mix, w_in, b_f, conv_w, conv_b, ln_g, ln_b, w_out, g_x, g_mem, w_mq, w_mkv, w_mo, g_ffn,
                         w_gu, w_down, g_final]))
    M = dict(zip(names, [m_g_mix, m_w_in, m_b_f, m_conv_w, m_conv_b, m_ln_g, m_ln_b, m_w_out, m_g_x, m_g_mem, m_w_mq,
                         m_w_mkv, m_w_mo, m_g_ffn, m_w_gu, m_w_down, m_g_final]))
    V = dict(zip(names, [v_g_mix, v_w_in, v_b_f, v_conv_w, v_conv_b, v_ln_g, v_ln_b, v_w_out, v_g_x, v_g_mem, v_w_mq,
                         v_w_mkv, v_w_mo, v_g_ffn, v_w_gu, v_w_down, v_g_final]))
    big_names = [n for n, _, _, _ in BIG]
    B, S, _ = x.shape
    T = B * S
    mx, my, mc = _pos()
    chip = 2 * mx + my
    pos = jnp.stack([mc, chip]).astype(jnp.int32)

    shard2d = lambda a: a.reshape(a.shape[-2], a.shape[-1])
    big2d = lambda d, n: shard2d(d[n]).T if n == "w_in" else shard2d(d[n])
    shard_bf = {n: big2d(W, n).astype(BF16) for n in big_names}
    ag_mid = ["w_mkv", "w_out", "w_mq", "w_mo"]
    ag_ffn = ["w_gu", "w_down"]
    cw_mine = jnp.pad(shard2d(conv_w), ((0, 1), (0, 0)))
    w_in_slab, cw_slab = _run_comm(_ag_comm([shard_bf["w_in"], cw_mine]), "ag_w_in")
    slab = {"w_in": w_in_slab}
    w_int = w_in_slab.reshape(D_IN, D)
    w_ft = jnp.pad(w_int[OFF_F:D_IN], ((0, D_IN_PAD - D_IN), (0, 0)))
    cw = jnp.transpose(cw_slab, (1, 0, 2)).reshape(CONV_HALO, CONV_CH)

    row = lambda a: a.reshape(1, -1)
    bf_pad = jnp.pad(row(b_f), ((0, 0), (0, LANES - 8)))
    x2d = x.reshape(T, D)
    mem2d = mem.reshape(B * MEM_LEN, D)
    tgt = loss_target.reshape(T, D)

    (h, u, gt, q, k, v, zf, c, cq, qx, kx, qT), got = _fwd_in(x2d, row(g_mix), w_int, w_ft, bf_pad, B, S,
                                                  comm=_ag_comm([shard_bf[n] for n in ag_mid[:2]]))
    slab.update(zip(ag_mid[:2], got))
    ckT = jnp.transpose(c.reshape(B, S, LANES)[:, :, :8], (0, 2, 1)).reshape(B, N_PAIR, 2, S)
    ckT = jnp.pad(ckT, ((0, 0), (0, 0), (0, 6), (0, 0)))
    (y, co), got = _conv_fwd(u, gt, cw, row(conv_b), row(ln_g), row(ln_b), B, S,
                             comm=_ag_comm([shard_bf[n] for n in ag_mid[2:]]))
    slab.update(zip(ag_mid[2:], got))
    (o, fox_bias), got = _fox_fwd(qx, kx, v, cq, B, S, comm=_ag_comm([shard_bf[n] for n in ag_ffn]))
    slab.update(zip(ag_ffn, got))
    full = {n: slab[n] if by_col else slab[n].reshape(4 * r, c) for n, r, c, by_col in BIG}
    mn, km, vm = _mem_kv(mem2d, row(g_mem), full["w_mkv"], B)
    (x1, hx, qm, om, x2, cat), _ = _fwd_mid(x2d, co, o, km, vm, full["w_out"], full["w_mq"], full["w_mo"], row(g_x), B, S)
    hf, gu, act, dx3, loss_p, dg_final = _fwd_ffn(x2, tgt, full["w_gu"], full["w_down"], row(g_ffn), row(g_final), T)

    pos_sum = lambda gs, rcvs, ns: [_chip_sum(g, r, pos, "rs_chip_sum_" + n) for g, r, n in zip(gs, rcvs, ns)]
    fin_sum = lambda gs, rcvs, rcs, ns: [_final_sum(g, r, q3, pos, "rs_final_sum_" + n)
                                         for g, r, q3, n in zip(gs, rcvs, rcs, ns)]
    RH = {}
    dgu, dx2, dg_ffn = _bwd_ffn(dx3, gu, x2, full["w_gu"], full["w_down"], row(g_ffn), T)
    g_ffn_w = [_dw(hf, dgu, "dw_gu", FF_CHUNK, slabs=True), _dw(act, dx3, "dw_down", 512).reshape(4, D_FF // 4, D)]
    (dx1, dqm, dco, do, dd, dkm, dvm, dg_x, doT), rcv_ffn = _bwd_mid(dx2, x1, qm, km, vm, o, full["w_mo"], full["w_mq"],
                                                                full["w_out"], row(g_x), B, S, comm=_sibling_comm(g_ffn_w))
    pb_ffn = pos_sum(g_ffn_w, rcv_ffn, ag_ffn)
    dkv, dg_mem = _mem_bwd(dkm, dvm, mem2d, full["w_mkv"], row(g_mem), B)
    g_mid_w = [_dw(mn, dkv, "dw_mkv", 512, slabs=True), _dw(cat, dx1, "dw_out", 512).reshape(4, 256, D),
               _dw(hx, dqm, "dw_mq", 512).reshape(4, 256, D), _dw(om, dx2, "dw_mo", 512).reshape(4, 256, D)]
    (dq, dk, dv, dc, dcq), got = _fox_bwd(q, k, v, do, fox_bias, dd, ckT, qT, doT, B, S,
                                          comm=_join(_ici_comm(pb_ffn), _sibling_comm(g_mid_w)))
    rc_ffn, rcv_mid = got[:len(pb_ffn)], got[len(pb_ffn):]
    RH.update(zip(ag_ffn, fin_sum(g_ffn_w, rcv_ffn, rc_ffn, ag_ffn)))
    pb_mid = pos_sum(g_mid_w, rcv_mid, ag_mid)
    dc8 = jnp.transpose(dc[:, :, :2, :].reshape(B, 8, S), (0, 2, 1)).reshape(T, 8)
    dc8 = dc8 + dcq.reshape(T, 8, HEAD_D)[:, :, 0]
    dzf, dbf = _fgate_bwd(jnp.pad(dc8, ((0, 0), (0, LANES - 8))), zf, B, S)
    (du, dgt, dcw, dvec), rc_mid = _conv_bwd(dco, y, u, gt, cw, row(ln_g), row(ln_b), B, S, comm=_ici_comm(pb_mid))
    RH.update(zip(ag_mid, fin_sum(g_mid_w, rcv_mid, rc_mid, ag_mid)))
    dz = jnp.concatenate([du, dgt, dq, dk, dv, dzf], axis=1)
    g_in_w = [_dw(dz, h, "dw_in", 512, tk=D_IN_PAD // 3, rows=D_IN).reshape(4, D_IN // 4, D)]
    rcv_in = _run_comm(_sibling_comm(g_in_w), "rs_sibling_in")
    (grad_x, dg_mix), rc_in = _bwd_in(dz, w_int, w_ft, x2d, dx1, row(g_mix), T,
                                      comm=_ici_comm(pos_sum(g_in_w, rcv_in, ["w_in"])))
    RH.update(zip(["w_in"], fin_sum(g_in_w, rcv_in, rc_in, ["w_in"])))

    small_g = {"g_mix": dg_mix, "b_f": dbf[:, :8], "conv_w": dcw, "conv_b": dvec[0], "ln_g": dvec[1], "ln_b": dvec[2],
               "g_x": dg_x, "g_mem": dg_mem, "g_ffn": dg_ffn, "g_final": dg_final, "loss": loss_p[:, :1]}
    sg, filled = _small_allreduce(_pack_small(small_g, CONV_HALO * 4), "allreduce_small", [RH[n] for n in big_names])
    shared = dict(zip(big_names, filled))
    G, DL, NM, NV = {}, {}, {}, {}
    for n in big_names:
        G[n], DL[n], NM[n], NV[n] = _adamw(big2d(W, n), shared[n], big2d(M, n), big2d(V, n), "adamw_" + n,
                                           _half_shape(shared[n].shape))
    shapes = {n: W[n].shape for n in names if n not in big_names}
    shapes["conv_w"] = (CONV_HALO, CONV_CH)
    shapes["loss"] = (1,)
    sgrads = _unpack_small(sg, shapes, CONV_HALO * 4)
    loss = sgrads.pop("loss")[0]
    sgrads["conv_w"] = lax.dynamic_slice(sgrads["conv_w"], (0, chip * LANES), (CONV_K, LANES)).reshape(W["conv_w"].shape)
    spack = lambda d: _pack_small({n: d[n] for n in sgrads}, CONV_HALO)
    _, sd, snm, snv = _adamw(spack(W), spack(sgrads), spack(M), spack(V), "adamw_small", (8, LANES))
    sshapes = {n: W[n].shape for n in sgrads}
    SD, SNM, SNV = (_unpack_small(a, sshapes, CONV_HALO) for a in (sd, snm, snv))

    def collect(bigs, smalls):
        back = lambda n: (bigs[n].T if n == "w_in" else bigs[n]).reshape(W[n].shape)
        return [back(n) if n in big_names else smalls[n] for n in names]

    return (loss, grad_x.reshape(x.shape), *collect(G, sgrads), *collect(DL, SD), *collect(NM, SNM), *collect(NV, SNV))
```

```python
import functools
import math

import jax
import jax.numpy as jnp
from jax import lax
from jax.experimental import pallas as pl
from jax.experimental.pallas import tpu as pltpu

F32, BF16 = jnp.float32, jnp.bfloat16
MESH = pl.DeviceIdType.MESH

D = 1024
CONV_CH = 512
CONV_K = 31
CONV_HALO = 32
FOX_W = 512
HEAD_D = 64
N_PAIR = 4
MEM_LEN = 256
MEM_HEADS = 4
MEM_HD = 256
D_FF = 2816
FF_CHUNK = 1408
D_IN = 2568
D_IN_PAD = 2688
OFF_F = 2560
EPS = 1e-6
LANES = 128

ADAM_LR, ADAM_B1, ADAM_B2, ADAM_EPS, ADAM_WD, ADAM_STEP = 0.001, 0.9, 0.999, 1e-08, 0.01, 10

VMEM_LIMIT = 60 * 1024 * 1024

BIG = (("w_out", 256, 1024, False), ("w_mq", 256, 1024, False), ("w_mkv", 1024, 512, True),
       ("w_mo", 256, 1024, False), ("w_gu", 1024, 1408, True), ("w_down", 704, 1024, False),
       ("w_in", 642, 1024, False))

ANY = pl.BlockSpec(memory_space=pl.ANY)


def _sig(x):
    return 1.0 / (1.0 + jnp.exp(-x))


def _dot(a, b):
    return jnp.dot(a, b, preferred_element_type=F32)


def _dot_nt(a, b):
    return lax.dot_general(a, b, (((1,), (1,)), ((), ())), preferred_element_type=F32)


def _dot_tn(a, b):
    return lax.dot_general(a, b, (((0,), (0,)), ((), ())), preferred_element_type=F32)


def _split3(x):
    hi = x.astype(BF16)
    r = x - hi.astype(F32)
    mid = r.astype(BF16)
    return hi, mid, (r - mid.astype(F32)).astype(BF16)


def _dot_01(a, b):
    if a.dtype == jnp.bool_:
        return sum(_dot(a.astype(BF16), t) for t in _split3(b))
    return sum(_dot(t, b.astype(BF16)) for t in _split3(a))


def _resident(a):
    nd = a.ndim
    return pl.BlockSpec(a.shape, lambda *_: (0,) * nd, pipeline_mode=pl.Buffered(1))


def _acc_spec(shape):
    nd = len(shape)
    return pl.BlockSpec(shape, lambda *_: (0,) * nd)


def _params(n_grid):
    return pltpu.CompilerParams(dimension_semantics=("arbitrary",) * n_grid, vmem_limit_bytes=VMEM_LIMIT)


def _sds(shape, dtype):
    return jax.ShapeDtypeStruct(shape, dtype)


def _rms(x):
    r = lax.rsqrt(jnp.mean(x * x, axis=-1, keepdims=True) + EPS)
    return r, x * r


def _rms_bwd(dy, xh, r, g):
    dxh = dy * g
    dx = r * (dxh - xh * jnp.mean(dxh * xh, axis=-1, keepdims=True))
    return dx, dy * xh


def _head_expand(rows, cols):
    hd = lax.broadcasted_iota(jnp.int32, (rows, cols), 1) // HEAD_D
    hr = lax.broadcasted_iota(jnp.int32, (rows, cols), 0)
    return hd == hr


def _feat_major_spec(TB, FOX_T, nb):
    return pl.BlockSpec((1, TB // FOX_T, FOX_W, FOX_T), lambda b, j: (b, j, 0, 0))


def _fwd_in(x2, g_mix, w_int, w_ft, bf_pad, B, S, comm=None):
    T = B * S
    TB = min(512, S)
    nb = S // TB
    FOX_T = min(256, S)

    def body(x_ref, g_ref, w_ref, wf_ref, bf_ref, h_ref, u_ref, gt_ref, q_ref, k_ref, v_ref, zf_ref, c_ref, cq_ref,
             qx_ref, kx_ref, qt_ref, carry):
        j = pl.program_id(1)

        @pl.when(j == 0)
        def _():
            carry[...] = jnp.zeros_like(carry)

        _, xh = _rms(x_ref[...])
        h = (xh * g_ref[...]).astype(BF16)
        h_ref[...] = h
        u_ref[...] = _dot_nt(h, w_ref[0:512, :])
        gt_ref[...] = _dot_nt(h, w_ref[512:1024, :])
        qf = _dot_nt(h, w_ref[1024:1536, :])
        qb = qf.astype(BF16)
        kb = _dot_nt(h, w_ref[1536:2048, :]).astype(BF16)
        q_ref[...] = qb
        k_ref[...] = kb
        for t in range(TB // FOX_T):
            qt_ref[0, t] = qf[t * FOX_T:(t + 1) * FOX_T, :].T.astype(BF16)
        v_ref[...] = _dot_nt(h, w_ref[2048:2560, :]).astype(BF16)
        zf = _dot_nt(h, wf_ref[...]) + bf_ref[...]
        zf_ref[...] = zf
        lane = lax.broadcasted_iota(jnp.int32, zf.shape, 1)
        logf = jnp.where(lane < 8, jnp.minimum(zf, 0.0) - jnp.log(1.0 + jnp.exp(-jnp.abs(zf))), 0.0)
        row = lax.broadcasted_iota(jnp.int32, (TB, TB), 0)
        col = lax.broadcasted_iota(jnp.int32, (TB, TB), 1)
        c = _dot_01(row >= col, logf) + carry[0:1, :]
        carry[0:1, :] = c[TB - 1:TB, :]
        c_ref[...] = c
        cq = _dot_01(c, _head_expand(LANES, FOX_W))
        cq_ref[...] = cq
        hl = lax.broadcasted_iota(jnp.int32, (TB, LANES), 1)
        for hd in range(2 * N_PAIR):
            grp = slice((hd // 2) * LANES, (hd // 2 + 1) * LANES)
            swap = (lambda t: t) if hd % 2 == 0 else (lambda t: pltpu.roll(t, HEAD_D, 1))
            qf = swap(qb[:, grp].astype(F32) * (1.0 / math.sqrt(HEAD_D)))
            kf = swap(kb[:, grp].astype(F32))
            cv = cq[:, grp] if hd % 2 == 1 else pltpu.roll(cq[:, grp], HEAD_D, 1)
            hi = cv.astype(BF16).astype(F32)
            mid = (cv - hi).astype(BF16).astype(F32)
            lo = (cv - hi - mid).astype(BF16).astype(F32)
            pick = lambda a, b, c3, one_from, one_to: jnp.where(hl == a[0], a[1], jnp.where(hl == b[0], b[1], jnp.where(
                hl == c3[0], c3[1], jnp.where((hl >= one_from) & (hl < one_to), 1.0, 0.0))))
            qx = jnp.where(hl < HEAD_D, qf, pick((67, hi), (68, mid), (69, lo), 64, 67))
            kx = jnp.where(hl < HEAD_D, kf, pick((64, -hi), (65, -mid), (66, -lo), 67, 70))
            qx_ref[:, hd * LANES:(hd + 1) * LANES] = qx.astype(BF16)
            kx_ref[:, hd * LANES:(hd + 1) * LANES] = kx.astype(BF16)

    tok = lambda w: pl.BlockSpec((TB, w), lambda b, j: (b * nb + j, 0))
    outs = [(D, BF16), (512, F32), (512, F32), (512, BF16), (512, BF16), (512, BF16), (LANES, F32),
            (LANES, F32), (FOX_W, F32), (2 * FOX_W, BF16), (2 * FOX_W, BF16)]
    return _call(
        body, comm, name="fwd_in", grid=(B, nb),
        in_specs=[tok(D), _resident(g_mix), _resident(w_int), _resident(w_ft), _resident(bf_pad)],
        out_specs=[tok(w) for w, _ in outs] + [_feat_major_spec(TB, FOX_T, nb)],
        out_shape=[_sds((T, w), dt) for w, dt in outs] + [_sds((B, S // FOX_T, FOX_W, FOX_T), BF16)],
        scratch_shapes=[pltpu.VMEM((8, LANES), F32)],
        args=(x2, g_mix, w_int, w_ft, bf_pad))


def _head_sum(n):
    hc = lax.broadcasted_iota(jnp.int32, (n, n), 1) // HEAD_D
    hr = lax.broadcasted_iota(jnp.int32, (n, n), 0) // HEAD_D
    return hc == hr


def _layernorm_silu(y, lg, lb):
    mu = jnp.mean(y, axis=-1, keepdims=True)
    yc = y - mu
    rs = lax.rsqrt(jnp.mean(yc * yc, axis=-1, keepdims=True) + EPS)
    n = yc * rs
    l = n * lg + lb
    return rs, n, l


SUB = 8


def _shifted_copies(cat, sh, rows):
    for r in range(1, SUB):
        sh[r, 0:rows, :] = cat[r:r + rows, :]


def _tap(cat, sh, off, rows, cols=slice(None)):
    r = off % SUB
    return cat[off:off + rows, cols] if r == 0 else sh[r, off - r:off - r + rows, cols]


CONV_ROWS = 128


def _conv_pieces(CB):
    rows = min(CONV_ROWS, CB)
    return [(r0, rows, slice(c0, c0 + LANES)) for c0 in range(0, CONV_CH, LANES) for r0 in range(0, CB, rows)]


def _conv_fwd(u, gt, cw, cb, lng, lnb, B, S, comm=None):
    T = B * S
    CB = min(256, S)
    nb = S // CB

    def body(u_ref, gt_ref, w_ref, cb_ref, lg_ref, lb_ref, y_ref, co_ref, acat, ash):
        j = pl.program_id(1)

        @pl.when(j == 0)
        def _():
            acat[0:CONV_HALO, :] = jnp.zeros((CONV_HALO, CONV_CH), F32)

        acat[CONV_HALO:CONV_HALO + CB, :] = u_ref[...] * _sig(gt_ref[...])
        _shifted_copies(acat, ash, CB + CONV_HALO - SUB)
        for r0, rows, cs in _conv_pieces(CB):
            acc = jnp.zeros((rows, LANES), F32) + cb_ref[:, cs]
            for k in range(CONV_K):
                acc = acc + w_ref[k:k + 1, cs] * _tap(acat, ash, r0 + CONV_HALO - (CONV_K - 1) + k, rows, cs)
            y_ref[r0:r0 + rows, cs] = acc
        acat[0:CONV_HALO, :] = acat[CB:CB + CONV_HALO, :]
        _, _, l = _layernorm_silu(y_ref[...], lg_ref[...], lb_ref[...])
        co_ref[...] = (l * _sig(l)).astype(BF16)

    tok = lambda w: pl.BlockSpec((CB, w), lambda b, j: (b * nb + j, 0))
    return _call(
        body, comm, name="conv_fwd", grid=(B, nb),
        in_specs=[tok(CONV_CH), tok(CONV_CH), _resident(cw), _resident(cb), _resident(lng), _resident(lnb)],
        out_specs=[tok(CONV_CH), tok(CONV_CH)],
        out_shape=[_sds((T, CONV_CH), F32), _sds((T, CONV_CH), BF16)],
        scratch_shapes=[pltpu.VMEM((CONV_HALO + CB, CONV_CH), F32),
                        pltpu.VMEM((SUB, CB + CONV_HALO - SUB, CONV_CH), F32)],
        args=(u, gt, cw, cb, lng, lnb))


def _fox_fwd(qx, kx, v, cq, B, S, comm=None):
    T = B * S
    TQ = min(256, S)
    nq = S // TQ

    def body(qa_ref, qb_ref, ka_ref, kb_ref, v_ref, cq_ref, o_ref, lse_ref, s_scr, s_odd, m_scr, acc_scr):
        i = pl.program_id(2)
        lane = lax.broadcasted_iota(jnp.int32, (TQ, LANES), 1)
        lo = lane < HEAD_D
        qh = (qa_ref[...], qb_ref[...])
        kh = (ka_ref, kb_ref)
        m_scr[...] = jnp.full(m_scr.shape, -1e30, F32)
        acc_scr[...] = jnp.zeros_like(acc_scr)
        row = lax.broadcasted_iota(jnp.int32, (TQ, TQ), 0)
        col = lax.broadcasted_iota(jnp.int32, (TQ, TQ), 1)
        wide = lambda x: jnp.concatenate([x, x], axis=1) if TQ == 2 * LANES else jnp.tile(x, (1, TQ // LANES))

        def scores(j, s_buf):
            start = pl.multiple_of(j * TQ, TQ)
            for h in range(2):
                s_buf[h] = _dot_nt(qh[h], kh[h][pl.ds(start, TQ), :])

        def softmax_step(j, s_buf, diagonal):
            start = pl.multiple_of(j * TQ, TQ)
            vj = v_ref[pl.ds(start, TQ), :]
            for h in range(2):
                def logits():
                    return jnp.where(col <= row, s_buf[h], -1e30) if diagonal else s_buf[h]

                m_old = m_scr[h]
                m_new = jnp.maximum(m_old, jnp.max(logits(), axis=-1, keepdims=True))
                alpha = jnp.exp(m_old - m_new)
                m_scr[h] = m_new
                p = jnp.exp(logits() - wide(m_new)).astype(BF16)
                vx = jnp.where(lo if h == 0 else ~lo, vj, jnp.ones_like(vj))
                acc_scr[h] = alpha * acc_scr[h] + _dot(p, vx)

        def two_blocks(jj, carry):
            j = 2 * jj
            scores(j + 1, s_odd)
            softmax_step(j, s_scr, False)
            scores(j + 2, s_scr)
            softmax_step(j + 1, s_odd, False)
            return carry

        scores(0, s_scr)
        lax.fori_loop(0, i // 2, two_blocks, 0)

        @pl.when(i % 2 == 0)
        def _():
            softmax_step(i, s_scr, True)

        @pl.when(i % 2 == 1)
        def _():
            scores(i, s_odd)
            softmax_step(i - 1, s_scr, False)
            softmax_step(i, s_odd, True)

        acc = jnp.where(lo, acc_scr[0], acc_scr[1])
        den = pltpu.roll(jnp.where(lo, acc_scr[1], acc_scr[0]), HEAD_D, 1)
        o_ref[...] = acc / den
        lse_ref[...] = cq_ref[...] - (jnp.where(lo, m_scr[0], m_scr[1]) + jnp.log(den))

    qspec = pl.BlockSpec((TQ, LANES), lambda b, p, i: (b * nq + i, p))
    kspec = pl.BlockSpec((S, LANES), lambda b, p, i: (b, p))
    qhead = lambda h: pl.BlockSpec((TQ, LANES), lambda b, p, i: (b * nq + i, 2 * p + h))
    khead = lambda h: pl.BlockSpec((S, LANES), lambda b, p, i: (b, 2 * p + h))
    return _call(
        body, comm, name="fox_fwd", grid=(B, N_PAIR, nq),
        in_specs=[qhead(0), qhead(1), khead(0), khead(1), kspec, qspec],
        out_specs=[qspec, qspec],
        out_shape=[_sds((T, FOX_W), F32), _sds((T, FOX_W), F32)],
        scratch_shapes=[pltpu.VMEM((2, TQ, TQ), F32), pltpu.VMEM((2, TQ, TQ), F32),
                        pltpu.VMEM((2, TQ, LANES), F32), pltpu.VMEM((2, TQ, LANES), F32)],
        args=(qx, qx, kx, kx, v, cq))


def _mem_kv(mem2, g_mem, w_mkv, B):
    def body(m_ref, g_ref, w_ref, mn_ref, km_ref, vm_ref):
        _, xh = _rms(m_ref[...])
        mn = (xh * g_ref[...]).astype(BF16)
        mn_ref[...] = mn
        for s in range(2):
            km_ref[:, 512 * s:512 * (s + 1)] = _dot(mn, w_ref[s]).astype(BF16)
            vm_ref[:, 512 * s:512 * (s + 1)] = _dot(mn, w_ref[2 + s]).astype(BF16)

    blk = pl.BlockSpec((MEM_LEN, D), lambda b: (b, 0))
    return pl.pallas_call(
        body, name="mem_kv", grid=(B,),
        in_specs=[blk, _resident(g_mem), _resident(w_mkv)],
        out_specs=[blk, blk, blk],
        out_shape=[_sds((B * MEM_LEN, D), BF16)] * 3,
        compiler_params=_params(1),
    )(mem2, g_mem, w_mkv)


def _mem_probs(qm, km):
    ps = []
    for h in range(MEM_HEADS):
        hs = slice(h * MEM_HD, (h + 1) * MEM_HD)
        lg = _dot_nt(qm[:, hs], km[:, hs]) * (1.0 / math.sqrt(MEM_HD))
        e = jnp.exp(lg - jnp.max(lg, axis=-1, keepdims=True))
        ps.append(e / jnp.sum(e, axis=-1, keepdims=True))
    return ps


def _fwd_mid(x2, co, o, km, vm, w_out, w_mq, w_mo, g_x, B, S, comm=None):
    T = B * S
    TB = min(512, S)
    nb = S // TB

    def body(x_ref, co_ref, o_ref, km_ref, vm_ref, wo_ref, wq_ref, wm_ref, g_ref,
             x1_ref, hx_ref, qm_ref, om_ref, x2_ref, cat_ref):
        cat_ref[:, 0:CONV_CH] = co_ref[...]
        cat_ref[:, CONV_CH:D] = o_ref[...].astype(BF16)
        x1 = x_ref[...] + _dot(cat_ref[...], wo_ref[...])
        x1_ref[...] = x1
        _, xh = _rms(x1)
        hx = (xh * g_ref[...]).astype(BF16)
        hx_ref[...] = hx
        qm = _dot(hx, wq_ref[...]).astype(BF16)
        qm_ref[...] = qm
        ps = _mem_probs(qm, km_ref[...])
        vmv = vm_ref[...]
        for h in range(MEM_HEADS):
            hs = slice(h * MEM_HD, (h + 1) * MEM_HD)
            om_ref[:, hs] = _dot(ps[h].astype(BF16), vmv[:, hs]).astype(BF16)
        x2_ref[...] = x1 + _dot(om_ref[...], wm_ref[...])

    tok = lambda w: pl.BlockSpec((TB, w), lambda b, j: (b * nb + j, 0))
    memb = pl.BlockSpec((MEM_LEN, D), lambda b, j: (b, 0))
    outs = [(D, F32), (D, BF16), (D, BF16), (D, BF16), (D, F32), (D, BF16)]
    return _call(
        body, comm, name="fwd_mid", grid=(B, nb),
        in_specs=[tok(D), tok(CONV_CH), tok(FOX_W), memb, memb, _resident(w_out), _resident(w_mq), _resident(w_mo),
                  _resident(g_x)],
        out_specs=[tok(w) for w, _ in outs],
        out_shape=[_sds((T, w), dt) for w, dt in outs],
        scratch_shapes=[],
        args=(x2, co, o, km, vm, w_out, w_mq, w_mo, g_x))


def _load_gate_up(wgu_hbm, wg, wu, sems):
    copies = [pltpu.make_async_copy(wgu_hbm.at[s], (wg if s < 2 else wu).at[:, pl.ds((s % 2) * FF_CHUNK, FF_CHUNK)],
                                    sems.at[s]) for s in range(4)]
    for cp in copies:
        cp.start()
    for cp in copies:
        cp.wait()


def _fwd_ffn(x2, tgt, w_gu, w_down, g_ffn, g_final, T):
    TB = min(256, T)
    nb = T // TB

    def body(x_ref, t_ref, wgu_ref, wd_ref, gf_ref, gl_ref, hf_ref, gu_ref, act_ref, dx3_ref, loss_ref, dgl_ref,
             wg, wu, sems):
        i = pl.program_id(0)

        @pl.when(i == 0)
        def _():
            _load_gate_up(wgu_ref, wg, wu, sems)
            loss_ref[...] = jnp.zeros_like(loss_ref)
            dgl_ref[...] = jnp.zeros_like(dgl_ref)

        x2v = x_ref[...]
        _, xh = _rms(x2v)
        hf = (xh * gf_ref[...]).astype(BF16)
        hf_ref[...] = hf
        g = _dot(hf, wg[...])
        u = _dot(hf, wu[...])
        gu_ref[:, 0:D_FF] = g
        gu_ref[:, D_FF:2 * D_FF] = u
        act = (g * _sig(g) * u).astype(BF16)
        act_ref[...] = act
        x3 = x2v + _dot(act, wd_ref[...])
        r3, xh3 = _rms(x3)
        gl = gl_ref[...]
        e = xh3 * gl - t_ref[...]
        loss_ref[...] += jnp.sum(e * e) * (0.5 / D)
        dy = e * (1.0 / D)
        dx3, dgl = _rms_bwd(dy, xh3, r3, gl)
        dx3_ref[...] = dx3
        dgl_ref[...] += jnp.sum(dgl, axis=0, keepdims=True)

    tok = lambda w: pl.BlockSpec((TB, w), lambda i: (i, 0))
    return pl.pallas_call(
        body, name="fwd_ffn", grid=(nb,),
        in_specs=[tok(D), tok(D), ANY, _resident(w_down), _resident(g_ffn), _resident(g_final)],
        out_specs=[tok(D), tok(2 * D_FF), tok(D_FF), tok(D), _acc_spec((1, LANES)), _acc_spec((1, D))],
        out_shape=[_sds((T, D), BF16), _sds((T, 2 * D_FF), F32), _sds((T, D_FF), BF16), _sds((T, D), F32),
                   _sds((1, LANES), F32), _sds((1, D), F32)],
        scratch_shapes=[pltpu.VMEM((D, D_FF), BF16), pltpu.VMEM((D, D_FF), BF16), pltpu.SemaphoreType.DMA((4,))],
        compiler_params=_params(1),
    )(x2, tgt, w_gu, w_down, g_ffn, g_final)


def _bwd_ffn(dx3, gu, x2, w_gu, w_down, g_ffn, T):
    TB = min(256, T)
    nb = T // TB

    def body(d_ref, gu_ref, x_ref, wgu_ref, wd_ref, gf_ref, dgu_ref, dx2_ref, dgf_ref, wg, wu, sems):
        i = pl.program_id(0)

        @pl.when(i == 0)
        def _():
            _load_gate_up(wgu_ref, wg, wu, sems)
            dgf_ref[...] = jnp.zeros_like(dgf_ref)

        dx3v = d_ref[...]
        db = dx3v.astype(BF16)
        dact = _dot_nt(db, wd_ref[...])
        g = gu_ref[:, 0:D_FF]
        u = gu_ref[:, D_FF:2 * D_FF]
        sg = _sig(g)
        dg = (dact * u * sg * (1.0 + g * (1.0 - sg))).astype(BF16)
        du = (dact * g * sg).astype(BF16)
        dgu_ref[:, 0:D_FF] = dg
        dgu_ref[:, D_FF:2 * D_FF] = du
        dhf = _dot_nt(dg, wg[...]) + _dot_nt(du, wu[...])
        r2, xh2 = _rms(x_ref[...])
        dx, dg_tok = _rms_bwd(dhf, xh2, r2, gf_ref[...])
        dx2_ref[...] = dx3v + dx
        dgf_ref[...] += jnp.sum(dg_tok, axis=0, keepdims=True)

    tok = lambda w: pl.BlockSpec((TB, w), lambda i: (i, 0))
    return pl.pallas_call(
        body, name="bwd_ffn", grid=(nb,),
        in_specs=[tok(D), tok(2 * D_FF), tok(D), ANY, _resident(w_down), _resident(g_ffn)],
        out_specs=[tok(2 * D_FF), tok(D), _acc_spec((1, D))],
        out_shape=[_sds((T, 2 * D_FF), BF16), _sds((T, D), F32), _sds((1, D), F32)],
        scratch_shapes=[pltpu.VMEM((D, D_FF), BF16), pltpu.VMEM((D, D_FF), BF16), pltpu.SemaphoreType.DMA((4,))],
        compiler_params=_params(1),
    )(dx3, gu, x2, w_gu, w_down, g_ffn)


def _bwd_mid(dx2, x1, qm, km, vm, o, w_mo, w_mq, w_out, g_x, B, S, comm=None):
    T = B * S
    TB = min(512, S)
    nb = S // TB
    FOX_T = min(256, S)
    inv = 1.0 / math.sqrt(MEM_HD)

    def body(d_ref, x1_ref, qm_ref, km_ref, vm_ref, o_ref, wm_ref, wq_ref, wo_ref, g_ref,
             dx1_ref, dqm_ref, dco_ref, do_ref, dd_ref, dkm_ref, dvm_ref, dgx_ref, dot_ref):
        b = pl.program_id(0)
        j = pl.program_id(1)

        @pl.when((b == 0) & (j == 0))
        def _():
            dgx_ref[...] = jnp.zeros_like(dgx_ref)

        @pl.when(j == 0)
        def _():
            dkm_ref[...] = jnp.zeros_like(dkm_ref)
            dvm_ref[...] = jnp.zeros_like(dvm_ref)

        dx2v = d_ref[...]
        dom = _dot_nt(dx2v.astype(BF16), wm_ref[...]).astype(BF16)
        qmv = qm_ref[...]
        kmv = km_ref[...]
        vmv = vm_ref[...]
        ps = _mem_probs(qmv, kmv)
        for h in range(MEM_HEADS):
            hs = slice(h * MEM_HD, (h + 1) * MEM_HD)
            p = ps[h]
            dp = _dot_nt(dom[:, hs], vmv[:, hs])
            ds = (p * (dp - jnp.sum(p * dp, axis=-1, keepdims=True))).astype(BF16)
            dqm_ref[:, hs] = (_dot(ds, kmv[:, hs]) * inv).astype(BF16)
            dkm_ref[:, hs] += _dot_tn(ds, qmv[:, hs]) * inv
            dvm_ref[:, hs] += _dot_tn(p.astype(BF16), dom[:, hs])
        dhx = _dot_nt(dqm_ref[...], wq_ref[...])
        r1, xh1 = _rms(x1_ref[...])
        dx, dg_tok = _rms_bwd(dhx, xh1, r1, g_ref[...])
        dx1 = dx2v + dx
        dx1_ref[...] = dx1
        dgx_ref[...] += jnp.sum(dg_tok, axis=0, keepdims=True)
        d1b = dx1.astype(BF16)
        dco_ref[...] = _dot_nt(d1b, wo_ref[0:CONV_CH, :])
        do = _dot_nt(d1b, wo_ref[CONV_CH:D, :])
        dob = do.astype(BF16)
        do_ref[...] = dob
        for t in range(TB // FOX_T):
            dot_ref[0, t] = do[t * FOX_T:(t + 1) * FOX_T, :].T.astype(BF16)
        dd_ref[...] = _dot_01(dob.astype(F32) * o_ref[...], _head_sum(FOX_W))

    tok = lambda w: pl.BlockSpec((TB, w), lambda b, j: (b * nb + j, 0))
    memb = pl.BlockSpec((MEM_LEN, D), lambda b, j: (b, 0))
    outs = [(D, F32), (D, BF16), (CONV_CH, F32), (FOX_W, BF16), (FOX_W, F32)]
    return _call(
        body, comm, name="bwd_mid", grid=(B, nb),
        in_specs=[tok(D), tok(D), tok(D), memb, memb, tok(FOX_W), _resident(w_mo), _resident(w_mq), _resident(w_out),
                  _resident(g_x)],
        out_specs=[tok(w) for w, _ in outs] + [memb, memb, _acc_spec((1, D)), _feat_major_spec(TB, FOX_T, nb)],
        out_shape=[_sds((T, w), dt) for w, dt in outs] + [_sds((B * MEM_LEN, D), F32)] * 2 + [_sds((1, D), F32)]
        + [_sds((B, S // FOX_T, FOX_W, FOX_T), BF16)],
        scratch_shapes=[],
        args=(dx2, x1, qm, km, vm, o, w_mo, w_mq, w_out, g_x))


def _mem_bwd(dkm, dvm, mem2, w_mkv, g_mem, B):
    def body(dk_ref, dv_ref, m_ref, w_ref, g_ref, dkv_ref, dg_ref):
        b = pl.program_id(0)

        @pl.when(b == 0)
        def _():
            dg_ref[...] = jnp.zeros_like(dg_ref)

        dk = dk_ref[...].astype(BF16)
        dv = dv_ref[...].astype(BF16)
        dkv_ref[:, 0:D] = dk
        dkv_ref[:, D:2 * D] = dv
        dmn = jnp.zeros((MEM_LEN, D), F32)
        for s in range(2):
            dmn = dmn + _dot_nt(dk[:, 512 * s:512 * (s + 1)], w_ref[s]) + _dot_nt(dv[:, 512 * s:512 * (s + 1)], w_ref[2 + s])
        _, xh = _rms(m_ref[...])
        dg_ref[...] += jnp.sum(dmn * xh, axis=0, keepdims=True)

    blk = pl.BlockSpec((MEM_LEN, D), lambda b: (b, 0))
    return pl.pallas_call(
        body, name="mem_bwd", grid=(B,),
        in_specs=[blk, blk, blk, _resident(w_mkv), _resident(g_mem)],
        out_specs=[pl.BlockSpec((MEM_LEN, 2 * D), lambda b: (b, 0)), _acc_spec((1, D))],
        out_shape=[_sds((B * MEM_LEN, 2 * D), BF16), _sds((1, D), F32)],
        compiler_params=_params(1),
    )(dkm, dvm, mem2, w_mkv, g_mem)


def _fox_bwd(q, k, v, do, bias, dd, ckT, qT, doT, B, S, comm=None):
    T = B * S
    TK = min(256, S)
    nk = S // TK
    scale = 1.0 / math.sqrt(HEAD_D)

    def body(q_ref, k_ref, v_ref, do_ref, bias_ref, dd_ref, ck_ref, qt_ref, dot_ref, dq_ref, dk_ref, dv_ref, dc_ref,
             dcq_ref, dq_acc, rs_acc, s_scr, dp_scr, s_odd, dp_odd, dk_acc, dv_acc, dc_acc):
        j = pl.program_id(2)

        @pl.when(j == 0)
        def _():
            dq_acc[...] = jnp.zeros_like(dq_acc)
            rs_acc[...] = jnp.zeros_like(rs_acc)

        dk_acc[...] = jnp.zeros_like(dk_acc)
        dv_acc[...] = jnp.zeros_like(dv_acc)
        dc_acc[...] = jnp.zeros_like(dc_acc)
        lane = lax.broadcasted_iota(jnp.int32, (TK, LANES), 1)
        lo = lane < HEAD_D
        ks = k_ref[...] * jnp.asarray(scale, BF16)
        v2 = v_ref[...]
        zero = jnp.zeros_like(ks)
        kh = (jnp.where(lo, ks, zero), jnp.where(lo, zero, ks))
        vh = (jnp.where(lo, v2, zero), jnp.where(lo, zero, v2))
        kstart = pl.multiple_of(j * TK, TK)
        ckh = tuple(ck_ref[0, 0, h:h + 1, pl.ds(kstart, TK)] for h in range(2))
        row = lax.broadcasted_iota(jnp.int32, (TK, TK), 0)
        col = lax.broadcasted_iota(jnp.int32, (TK, TK), 1)
        wide = lambda x: jnp.concatenate([x, x], axis=1) if TK == 2 * LANES else jnp.tile(x, (1, TK // LANES))

        def scores(i, s_buf, dp_buf):
            start = pl.multiple_of(i * TK, TK)
            qi = q_ref[pl.ds(start, TK), :]
            doi = do_ref[pl.ds(start, TK), :]
            for h in range(2):
                s_buf[h] = _dot_nt(qi, kh[h])
                dp_buf[h] = _dot_nt(doi, vh[h])

        def grads(i, s_buf, dp_buf, diagonal):
            start = pl.multiple_of(i * TK, TK)
            bias2 = bias_ref[pl.ds(start, TK), :]
            dd2 = dd_ref[pl.ds(start, TK), :]
            for h in range(2):
                hc = slice(h * HEAD_D, h * HEAD_D + 1)
                bias = jnp.broadcast_to(bias2[:, hc], (TK, LANES))
                ddh = jnp.broadcast_to(dd2[:, hc], (TK, LANES))
                p = jnp.exp((s_buf[h] - ckh[h]) + wide(bias))
                if diagonal:
                    p = jnp.where(col <= row, p, 0.0)
                ds = p * (dp_buf[h] - wide(ddh))
                dc_acc[h, 0:1, :] += jnp.sum(ds, axis=0, keepdims=True)
                rs_acc[h, pl.ds(start, TK), :] += jnp.sum(ds, axis=1, keepdims=True)
                pb = p.astype(BF16)
                dsb = ds.astype(BF16)
                feat = slice(h * HEAD_D, (h + 1) * HEAD_D)
                dv_acc[feat, :] += _dot(dot_ref[0, i, feat, :], pb)
                dk_acc[feat, :] += _dot(qt_ref[0, i, feat, :], dsb)
                dq_acc[pl.ds(start, TK), :] += _dot(dsb, kh[h])

        n_off = nk - 1 - j
        block = lambda t: jnp.where(t < n_off, j + 1 + t, j)

        def two_blocks(tt, carry):
            t = 2 * tt
            scores(block(t + 1), s_odd, dp_odd)
            grads(block(t), s_scr, dp_scr, False)
            scores(block(t + 2), s_scr, dp_scr)
            grads(block(t + 1), s_odd, dp_odd, False)
            return carry

        scores(block(0), s_scr, dp_scr)
        lax.fori_loop(0, n_off // 2, two_blocks, 0)

        @pl.when(n_off % 2 == 0)
        def _():
            grads(j, s_scr, dp_scr, True)

        @pl.when(n_off % 2 == 1)
        def _():
            scores(j, s_odd, dp_odd)
            grads(nk - 1, s_scr, dp_scr, False)
            grads(j, s_odd, dp_odd, True)

        dk_ref[...] = (dk_acc[...].T * scale).astype(BF16)
        dv_ref[...] = dv_acc[...].T.astype(BF16)
        sub = lax.broadcasted_iota(jnp.int32, (8, TK), 0)
        dca = dc_acc[0, 0:1, :]
        dcb = dc_acc[1, 0:1, :]
        dc_ref[0, 0] = jnp.where(sub == 0, -dca, jnp.where(sub == 1, -dcb, 0.0))

        @pl.when(j == nk - 1)
        def _():
            dq_ref[...] = dq_acc[...].astype(BF16)
            lo_s = lax.broadcasted_iota(jnp.int32, (S, LANES), 1) < HEAD_D
            dcq_ref[...] = jnp.where(lo_s, rs_acc[0], rs_acc[1])

    full = pl.BlockSpec((S, LANES), lambda b, p, j: (b, p))
    blk = pl.BlockSpec((TK, LANES), lambda b, p, j: (b * nk + j, p))
    featT = pl.BlockSpec((1, nk, LANES, TK), lambda b, p, j: (b, 0, p, 0))
    return _call(
        body, comm, name="fox_bwd", grid=(B, N_PAIR, nk),
        in_specs=[full, blk, blk, full, full, full, pl.BlockSpec((1, 1, 8, S), lambda b, p, j: (b, p, 0, 0)),
                  featT, featT],
        out_specs=[full, blk, blk, pl.BlockSpec((1, 1, 8, TK), lambda b, p, j: (b, p, 0, j)), full],
        out_shape=[_sds((T, FOX_W), BF16), _sds((T, FOX_W), BF16), _sds((T, FOX_W), BF16),
                   _sds((B, N_PAIR, 8, S), F32), _sds((T, FOX_W), F32)],
        scratch_shapes=[pltpu.VMEM((S, LANES), F32), pltpu.VMEM((2, S, 1), F32),
                        pltpu.VMEM((2, TK, TK), F32), pltpu.VMEM((2, TK, TK), F32),
                        pltpu.VMEM((2, TK, TK), F32), pltpu.VMEM((2, TK, TK), F32),
                        pltpu.VMEM((LANES, TK), F32), pltpu.VMEM((LANES, TK), F32), pltpu.VMEM((2, 8, TK), F32)],
        args=(q, k, v, do, bias, dd, ckT, qT, doT))


def _fgate_bwd(dc8, zf, B, S):
    T = B * S
    TB = min(512, S)
    nb = S // TB

    def body(dc_ref, zf_ref, dzf_ref, dbf_ref, carry):
        b = pl.program_id(0)
        j = pl.program_id(1)

        @pl.when((b == 0) & (j == 0))
        def _():
            dbf_ref[...] = jnp.zeros_like(dbf_ref)

        @pl.when(j == 0)
        def _():
            carry[...] = jnp.zeros_like(carry)

        dc = dc_ref[...]
        row = lax.broadcasted_iota(jnp.int32, (TB, TB), 0)
        col = lax.broadcasted_iota(jnp.int32, (TB, TB), 1)
        dlogf = _dot_01(col >= row, dc) + carry[0:1, :]
        carry[0:1, :] = dlogf[0:1, :]
        lane = lax.broadcasted_iota(jnp.int32, dc.shape, 1)
        dzf = jnp.where(lane < 8, dlogf * _sig(-zf_ref[...]), 0.0)
        dzf_ref[...] = dzf.astype(BF16)
        dbf_ref[...] += jnp.sum(dzf, axis=0, keepdims=True)

    tok = pl.BlockSpec((TB, LANES), lambda b, j: (b * nb + (nb - 1 - j), 0))
    return pl.pallas_call(
        body, name="fgate_bwd", grid=(B, nb),
        in_specs=[tok, tok],
        out_specs=[tok, _acc_spec((1, LANES))],
        out_shape=[_sds((T, LANES), BF16), _sds((1, LANES), F32)],
        scratch_shapes=[pltpu.VMEM((8, LANES), F32)],
        compiler_params=_params(2),
    )(dc8, zf)


def _conv_bwd(dco, y, u, gt, cw, lng, lnb, B, S, comm=None):
    T = B * S
    CB = min(256, S)
    nb = S // CB
    hb = CB // CONV_HALO

    def body(dco_ref, y_ref, u_ref, gt_ref, up_ref, gp_ref, w_ref, lg_ref, lb_ref,
             du_ref, dgt_ref, dw_ref, vec_ref, acat, dycat, ash, dysh):
        b = pl.program_id(0)
        j = pl.program_id(1)
        jr = nb - 1 - j

        @pl.when((b == 0) & (j == 0))
        def _():
            dw_ref[...] = jnp.zeros_like(dw_ref)
            vec_ref[...] = jnp.zeros_like(vec_ref)

        @pl.when(j == 0)
        def _():
            dycat[CB:CB + CONV_HALO, :] = jnp.zeros((CONV_HALO, CONV_CH), F32)

        lg = lg_ref[...]
        rs, n, l = _layernorm_silu(y_ref[...], lg, lb_ref[...])
        sg = _sig(l)
        dl = dco_ref[...] * (sg * (1.0 + l * (1.0 - sg)))
        dn = dl * lg
        dy = rs * (dn - jnp.mean(dn, axis=-1, keepdims=True) - n * jnp.mean(dn * n, axis=-1, keepdims=True))
        vec_ref[0:1, :] += jnp.sum(dy, axis=0, keepdims=True)
        vec_ref[1:2, :] += jnp.sum(dl * n, axis=0, keepdims=True)
        vec_ref[2:3, :] += jnp.sum(dl, axis=0, keepdims=True)
        dycat[0:CB, :] = dy
        acat[0:CONV_HALO, :] = jnp.where(jr > 0, up_ref[...] * _sig(gp_ref[...]), 0.0)
        acat[CONV_HALO:CONV_HALO + CB, :] = u_ref[...] * _sig(gt_ref[...])
        _shifted_copies(acat, ash, CB + CONV_HALO - SUB)
        _shifted_copies(dycat, dysh, CB + CONV_HALO - SUB)
        for r0, rows, cs in _conv_pieces(CB):
            dyp = dycat[r0:r0 + rows, cs]
            da = jnp.zeros((rows, LANES), F32)
            for k in range(CONV_K):
                da = da + w_ref[k:k + 1, cs] * _tap(dycat, dysh, r0 + CONV_K - 1 - k, rows, cs)
                dw_ref[k:k + 1, cs] += jnp.sum(dyp * _tap(acat, ash, r0 + CONV_HALO - (CONV_K - 1) + k, rows, cs),
                                               axis=0, keepdims=True)
            uv = u_ref[r0:r0 + rows, cs]
            sgt = _sig(gt_ref[r0:r0 + rows, cs])
            du_ref[r0:r0 + rows, cs] = (da * sgt).astype(BF16)
            dgt_ref[r0:r0 + rows, cs] = (da * uv * sgt * (1.0 - sgt)).astype(BF16)
        dycat[CB:CB + CONV_HALO, :] = dycat[0:CONV_HALO, :]

    tok = lambda w: pl.BlockSpec((CB, w), lambda b, j: (b * nb + (nb - 1 - j), 0))
    prev = pl.BlockSpec((CONV_HALO, CONV_CH), lambda b, j: (jnp.maximum((b * nb + (nb - 1 - j)) * hb - 1, 0), 0))
    return _call(
        body, comm, name="conv_bwd", grid=(B, nb),
        in_specs=[tok(CONV_CH), tok(CONV_CH), tok(CONV_CH), tok(CONV_CH), prev, prev, _resident(cw), _resident(lng),
                  _resident(lnb)],
        out_specs=[tok(CONV_CH), tok(CONV_CH), _acc_spec((CONV_HALO, CONV_CH)), _acc_spec((8, CONV_CH))],
        out_shape=[_sds((T, CONV_CH), BF16), _sds((T, CONV_CH), BF16), _sds((CONV_HALO, CONV_CH), F32),
                   _sds((8, CONV_CH), F32)],
        scratch_shapes=[pltpu.VMEM((CONV_HALO + CB, CONV_CH), F32), pltpu.VMEM((CB + CONV_HALO, CONV_CH), F32),
                        pltpu.VMEM((SUB, CB + CONV_HALO - SUB, CONV_CH), F32),
                        pltpu.VMEM((SUB, CB + CONV_HALO - SUB, CONV_CH), F32)],
        args=(dco, y, u, gt, u, gt, cw, lng, lnb))


def _bwd_in(dz, w_int, w_ft, x2, dx1, g_mix, T, comm=None):
    TB = min(512, T)
    nb = T // TB

    def body(dz_ref, w_ref, wf_ref, x_ref, d1_ref, g_ref, gx_ref, dg_ref):
        i = pl.program_id(0)

        @pl.when(i == 0)
        def _():
            dg_ref[...] = jnp.zeros_like(dg_ref)

        dh = _dot(dz_ref[:, 0:OFF_F], w_ref[0:OFF_F, :]) + _dot(dz_ref[:, OFF_F:D_IN_PAD], wf_ref[...])
        r0, xh0 = _rms(x_ref[...])
        dx, dg_tok = _rms_bwd(dh, xh0, r0, g_ref[...])
        gx_ref[...] = d1_ref[...] + dx
        dg_ref[...] += jnp.sum(dg_tok, axis=0, keepdims=True)

    tok = lambda w: pl.BlockSpec((TB, w), lambda i: (i, 0))
    return _call(
        body, comm, name="bwd_in", grid=(nb,),
        in_specs=[tok(D_IN_PAD), _resident(w_int), _resident(w_ft), tok(D), tok(D), _resident(g_mix)],
        out_specs=[tok(D), _acc_spec((1, D))],
        out_shape=[_sds((T, D), F32), _sds((1, D), F32)],
        scratch_shapes=[],
        args=(dz, w_int, w_ft, x2, dx1, g_mix))


def _dw(a, b, name, tn, slabs=False, tk=None, rows=None):
    T, K = a.shape
    N = b.shape[1]
    tk = tk or (K if K <= 1024 else K // 2)
    tt = min(1024, T)
    nt = T // tt

    def body(a_ref, b_ref, o_ref, acc):
        t = pl.program_id(2)

        @pl.when(t == 0)
        def _():
            acc[...] = jnp.zeros_like(acc)

        acc[...] += _dot_tn(a_ref[...].astype(BF16), b_ref[...].astype(BF16))

        @pl.when(t == nt - 1)
        def _():
            o_ref[...] = acc[...]

    return pl.pallas_call(
        body, name=name, grid=(K // tk, N // tn, nt),
        in_specs=[pl.BlockSpec((tt, tk), lambda i, j, t: (t, i)), pl.BlockSpec((tt, tn), lambda i, j, t: (t, j))],
        out_specs=(pl.BlockSpec((None, tk, tn), lambda i, j, t: (j, i, 0)) if slabs
                   else pl.BlockSpec((tk, tn), lambda i, j, t: (i, j))),
        out_shape=_sds((N // tn, K, tn) if slabs else (rows or K, N), F32),
        scratch_shapes=[pltpu.VMEM((tk, tn), F32)],
        compiler_params=_params(3),
    )(a, b)


def _pos():
    return lax.axis_index("x"), lax.axis_index("y"), lax.axis_index("c")


def _remote(src, dst, ssem, rsem, to):
    return pltpu.make_async_remote_copy(src_ref=src, dst_ref=dst, send_sem=ssem, recv_sem=rsem, device_id=to,
                                        device_id_type=MESH)


def _split_axis(shape):
    return 0 if shape[0] % 32 == 0 else 1


def _half_shape(shape, parts=2):
    return (shape[0] // parts, shape[1]) if _split_axis(shape) == 0 else (shape[0], shape[1] // parts)


def _half(shape, c):
    R, C = shape
    if _split_axis(shape) == 0:
        return (pl.ds(pl.multiple_of(c * (R // 2), 16), R // 2), slice(None))
    return (slice(None), pl.ds(pl.multiple_of(c * (C // 2), LANES), C // 2))


def _half_block(shape, parts, lead, which):
    blk = _half_shape(shape, parts)
    idx = (which, 0) if _split_axis(shape) == 0 else (0, which)
    return blk, tuple(lead) + idx


class _Comm:
    def __init__(self, ins, out_shapes, sems, start, finish):
        self.ins, self.out_shapes, self.sems, self.start, self.finish = list(ins), list(out_shapes), list(sems), start, finish


def _ag_comm(shards):
    n = len(shards)

    def parts(ins, outs, sems):
        send_sems, recv_sems, local_sems = sems
        x, y, c = _pos()
        me, sib = (x, y, c), (x, y, 1 - c)
        chips = [(1 - x, y), (x, 1 - y), (1 - x, 1 - y)]

        def rows(w, px, py, pc):
            return outs[w].at[(2 * px + py,) + _half(shards[w].shape, pc)]

        def copy(w, k, block, to, src=None):
            return _remote(rows(w, *block) if src is None else src, rows(w, *block), send_sems.at[w, k],
                           recv_sems.at[w, k], to)

        mine, first = [], []
        for w in range(n):
            src = ins[w].at[_half(shards[w].shape, c)]
            mine.append(pltpu.make_async_copy(src, rows(w, *me), local_sems.at[w]))
            first += [copy(w, 0, me, sib, src=src)] + [copy(w, 1 + j, me, (*chip, c), src=src) for j, chip in enumerate(chips)]
        return c, me, sib, chips, copy, mine, first

    def start(ins, outs, sems):
        _, _, _, _, _, mine, first = parts(ins, outs, sems)
        for cp in mine + first:
            cp.start()

    def finish(ins, outs, sems):
        c, me, sib, chips, copy, mine, first = parts(ins, outs, sems)
        passed = []
        for w in range(n):
            for j, chip in enumerate(chips):
                copy(w, 1 + j, (*chip, c), me).wait_recv()
                passed.append(copy(w, 4 + j, (*chip, c), sib))
                passed[-1].start()
        for w in range(n):
            copy(w, 0, sib, me).wait_recv()
            for j, chip in enumerate(chips):
                copy(w, 4 + j, (*chip, 1 - c), me).wait_recv()
        for cp in first + passed:
            cp.wait_send()
        for cp in mine:
            cp.wait()

    D7 = pltpu.SemaphoreType.DMA((n, 7))
    return _Comm(shards, [_sds((4,) + s.shape, s.dtype) for s in shards], [D7, D7, pltpu.SemaphoreType.DMA((n,))],
                 start, finish)


def _sibling_comm(gs):
    n = len(gs)

    def copies(ins, outs, sems):
        send_sems, recv_sems = sems
        x, y, c = _pos()
        return [_remote(ins[w].at[(s,) + _half(gs[w].shape[1:], 1 - c)], outs[w].at[s], send_sems.at[w, s],
                        recv_sems.at[w, s], (x, y, 1 - c)) for w in range(n) for s in range(4)]

    def start(ins, outs, sems):
        for cp in copies(ins, outs, sems):
            cp.start()

    def finish(ins, outs, sems):
        for cp in copies(ins, outs, sems):
            cp.wait()

    D4 = pltpu.SemaphoreType.DMA((n, 4))
    return _Comm(gs, [_sds((4,) + _half_shape(g.shape[1:]), F32) for g in gs], [D4, D4], start, finish)


def _ici_comm(pbs):
    n = len(pbs)

    def copies(ins, outs, sems):
        send_sems, recv_sems = sems
        x, y, c = _pos()
        return [_remote(ins[w].at[2 * tx + ty], outs[w].at[j], send_sems.at[w, j], recv_sems.at[w, j], (tx, ty, c))
                for w in range(n) for j, (tx, ty) in enumerate([(1 - x, y), (x, 1 - y), (1 - x, 1 - y)])]

    def start(ins, outs, sems):
        for cp in copies(ins, outs, sems):
            cp.start()

    def finish(ins, outs, sems):
        for cp in copies(ins, outs, sems):
            cp.wait()

    D3 = pltpu.SemaphoreType.DMA((n, 3))
    return _Comm(pbs, [_sds((3,) + p.shape[1:], BF16) for p in pbs], [D3, D3], start, finish)


def _join(*comms):
    counts = [(len(c.ins), len(c.out_shapes), len(c.sems)) for c in comms]

    def each(which):
        def run(ins, outs, sems):
            i = o = k = 0
            for c, (ni, no, nk) in zip(comms, counts):
                getattr(c, which)(ins[i:i + ni], outs[o:o + no], sems[k:k + nk])
                i, o, k = i + ni, o + no, k + nk
        return run

    return _Comm(sum((c.ins for c in comms), []), sum((c.out_shapes for c in comms), []),
                 sum((c.sems for c in comms), []), each("start"), each("finish"))


def _run_comm(comm, name):
    ni, no = len(comm.ins), len(comm.out_shapes)

    def body(*refs):
        ins, outs, sems = refs[:ni], refs[ni:ni + no], refs[ni + no:]
        comm.start(ins, outs, sems)
        comm.finish(ins, outs, sems)

    return pl.pallas_call(body, name=name, out_shape=comm.out_shapes, in_specs=[ANY] * ni, out_specs=[ANY] * no,
                          scratch_shapes=comm.sems)(*comm.ins)


def _call(body, comm, *, name, grid, in_specs, out_specs, out_shape, scratch_shapes, args):
    n_grid = len(grid)
    if comm is None:
        res = pl.pallas_call(body, name=name, grid=grid, in_specs=in_specs, out_specs=out_specs, out_shape=out_shape,
                             scratch_shapes=scratch_shapes, compiler_params=_params(n_grid))(*args)
        return list(res), []
    n_in, n_out, n_scr = len(in_specs), len(out_specs), len(scratch_shapes)
    ni, no = len(comm.ins), len(comm.out_shapes)

    def carried(*refs):
        ins, refs = refs[:n_in], refs[n_in:]
        cins, refs = refs[:ni], refs[ni:]
        outs, refs = refs[:n_out], refs[n_out:]
        couts, refs = refs[:no], refs[no:]
        scr, csems = refs[:n_scr], refs[n_scr:]
        ids = [pl.program_id(ax) for ax in range(n_grid)]
        first = functools.reduce(jnp.logical_and, [i == 0 for i in ids])
        last = functools.reduce(jnp.logical_and, [i == g - 1 for i, g in zip(ids, grid)])

        @pl.when(first)
        def _():
            comm.start(cins, couts, csems)

        body(*ins, *outs, *scr)

        @pl.when(last)
        def _():
            comm.finish(cins, couts, csems)

    res = pl.pallas_call(
        carried, name=name, grid=grid, in_specs=list(in_specs) + [ANY] * ni, out_specs=list(out_specs) + [ANY] * no,
        out_shape=list(out_shape) + comm.out_shapes, scratch_shapes=list(scratch_shapes) + comm.sems,
        compiler_params=_params(n_grid))(*args, *comm.ins)
    return list(res[:n_out]), list(res[n_out:])


def _small_allreduce(v, name, halves=()):
    P = v.shape[0]
    n = len(halves)
    vm = pl.BlockSpec(memory_space=pltpu.VMEM)

    def body(v_ref, *refs):
        o_ref, outs = refs[n], refs[n + 1:2 * n + 1]
        gath, send_sems, recv_sems, half_send, half_recv = refs[2 * n + 1:]
        x, y, c = _pos()
        me = 4 * x + 2 * y + c
        gath[me] = v_ref[...]
        cps = []
        for r in range(1, 8):
            tx = (1 - x) if r & 4 else x
            ty = (1 - y) if r & 2 else y
            tc = (1 - c) if r & 1 else c
            cps.append(_remote(v_ref, gath.at[me], send_sems.at[r - 1], recv_sems.at[r - 1], (tx, ty, tc)))
        for w in range(n):
            mine = outs[w].at[_half(halves[w].shape, c)]
            cps.append(_remote(mine, mine, half_send.at[w], half_recv.at[w], (x, y, 1 - c)))
        for cp in cps:
            cp.start()
        for cp in cps:
            cp.wait()
        acc = gath[0]
        for d in range(1, 8):
            acc = acc + gath[d]
        o_ref[...] = acc

    res = pl.pallas_call(
        body, name=name, out_shape=[_sds((P, LANES), F32)] + [_sds(g.shape, F32) for g in halves],
        in_specs=[vm] + [ANY] * n, out_specs=[vm] + [ANY] * n, input_output_aliases={1 + w: 1 + w for w in range(n)},
        scratch_shapes=[pltpu.VMEM((8, P, LANES), F32), pltpu.SemaphoreType.DMA((7,)), pltpu.SemaphoreType.DMA((7,)),
                        pltpu.SemaphoreType.DMA((max(n, 1),)), pltpu.SemaphoreType.DMA((max(n, 1),))],
    )(v, *halves)
    return res[0], list(res[1:])


def _chip_sum(g, rcv, pos, name):
    shard = g.shape[1:]
    hs = _half_shape(shard)
    other = lambda i, pos: (pos[1] + 1 + i) % 4

    def body(pos_ref, g_ref, r_ref, o_ref):
        o_ref[...] = (g_ref[...] + r_ref[...]).astype(BF16)

    return pl.pallas_call(
        body, name=name, out_shape=_sds((4,) + hs, BF16),
        grid_spec=pltpu.PrefetchScalarGridSpec(
            num_scalar_prefetch=1, grid=(3,),
            in_specs=[pl.BlockSpec((1,) + hs, lambda i, pos: _half_block(shard, 2, (other(i, pos),), pos[0])[1]),
                      pl.BlockSpec((1,) + hs, lambda i, pos: (other(i, pos), 0, 0))],
            out_specs=pl.BlockSpec((1,) + hs, lambda i, pos: (other(i, pos), 0, 0))),
        compiler_params=_params(1),
    )(pos, g, rcv)


def _final_sum(g, rcv, rc, pos, name):
    shard = g.shape[1:]
    qs = _half_shape(shard, 4)

    def body(pos_ref, g_ref, r_ref, rc_ref, o_ref):
        acc = g_ref[0] + r_ref[0]
        for j in range(3):
            acc = acc + rc_ref[j].astype(F32)
        o_ref[...] = acc

    return pl.pallas_call(
        body, name=name, out_shape=_sds(shard, F32),
        grid_spec=pltpu.PrefetchScalarGridSpec(
            num_scalar_prefetch=1, grid=(2,),
            in_specs=[pl.BlockSpec((1,) + qs, lambda i, pos: _half_block(shard, 4, (pos[1],), pos[0] * 2 + i)[1]),
                      pl.BlockSpec((1,) + qs, lambda i, pos: _half_block(shard, 4, (pos[1],), i)[1]),
                      pl.BlockSpec((3,) + qs, lambda i, pos: _half_block(shard, 4, (0,), i)[1])],
            out_specs=pl.BlockSpec(qs, lambda i, pos: _half_block(shard, 4, (), pos[0] * 2 + i)[1])),
        compiler_params=_params(1),
    )(pos, g, rcv, rc)


def _adamw_math(w, g, m, v):
    m = ADAM_B1 * m + (1.0 - ADAM_B1) * g
    v = ADAM_B2 * v + (1.0 - ADAM_B2) * (g * g)
    m_hat = m / (1.0 - ADAM_B1 ** ADAM_STEP)
    v_hat = v / (1.0 - ADAM_B2 ** ADAM_STEP)
    delta = -ADAM_LR * (m_hat / (jnp.sqrt(v_hat) + ADAM_EPS) + ADAM_WD * w)
    return delta, m, v


def _adamw(w, g, m, v, name, blk_shape):
    R, C = w.shape

    def body(w_ref, g_ref, m_ref, v_ref, go_ref, d_ref, nm_ref, nv_ref):
        g = g_ref[...]
        d, nm, nv = _adamw_math(w_ref[...], g, m_ref[...], v_ref[...])
        go_ref[...] = g
        d_ref[...] = d
        nm_ref[...] = nm
        nv_ref[...] = nv

    blk = pl.BlockSpec(blk_shape, lambda i, j: (i, j))
    return pl.pallas_call(
        body, name=name, grid=(R // blk_shape[0], C // blk_shape[1]), in_specs=[blk] * 4, out_specs=[blk] * 4,
        out_shape=[_sds((R, C), F32)] * 4, compiler_params=_params(2),
    )(w, g, m, v)


SMALL = (("g_mix", 8), ("b_f", 8), ("conv_w", None), ("conv_b", 8), ("ln_g", 8), ("ln_b", 8), ("g_x", 8), ("g_mem", 8),
         ("g_ffn", 8), ("g_final", 8), ("loss", 8))


def _pack_small(parts, conv_rows):
    rows = []
    for name, n in SMALL:
        if name not in parts:
            continue
        n = conv_rows if n is None else n
        flat = parts[name].reshape(-1).astype(F32)
        flat = jnp.pad(flat, (0, n * LANES - flat.shape[0]))
        rows.append(flat.reshape(n, LANES))
    return jnp.concatenate(rows, axis=0)


def _unpack_small(p, shapes, conv_rows):
    out, off = {}, 0
    for name, n in SMALL:
        if name not in shapes:
            continue
        n = conv_rows if n is None else n
        size = math.prod(shapes[name])
        out[name] = p[off:off + n].reshape(-1)[:size].reshape(shapes[name])
        off += n
    return out


def kernel(x, mem, g_mix, w_in, b_f, conv_w, conv_b, ln_g, ln_b, w_out, g_x, g_mem, w_mq, w_mkv, w_mo, g_ffn, w_gu, w_down, g_final, loss_target, m_g_mix, m_w_in, m_b_f, m_conv_w, m_conv_b, m_ln_g, m_ln_b, m_w_out, m_g_x, m_g_mem, m_w_mq, m_w_mkv, m_w_mo, m_g_ffn, m_w_gu, m_w_down, m_g_final, v_g_mix, v_w_in, v_b_f, v_conv_w, v_conv_b, v_ln_g, v_ln_b, v_w_out, v_g_x, v_g_mem, v_w_mq, v_w_mkv, v_w_mo, v_g_ffn, v_w_gu, v_w_down, v_g_final):
    names = ["g_mix", "w_in", "b_f", "conv_w", "conv_b", "ln_g", "ln_b", "w_out", "g_x", "g_mem", "w_mq", "w_mkv",
             "w_mo", "g_ffn", "w_gu", "w_down", "g_final"]
    W = dict(zip(names, [g_mix, w_in, b_f, conv_w, conv_b, ln_g, ln_b, w_out, g_x, g_mem, w_mq, w_mkv, w_mo, g_ffn,
                         w_gu, w_down, g_final]))
    M = dict(zip(names, [m_g_mix, m_w_in, m_b_f, m_conv_w, m_conv_b, m_ln_g, m_ln_b, m_w_out, m_g_x, m_g_mem, m_w_mq,
                         m_w_mkv, m_w_mo, m_g_ffn, m_w_gu, m_w_down, m_g_final]))
    V = dict(zip(names, [v_g_mix, v_w_in, v_b_f, v_conv_w, v_conv_b, v_ln_g, v_ln_b, v_w_out, v_g_x, v_g_mem, v_w_mq,
                         v_w_mkv, v_w_mo, v_g_ffn, v_w_gu, v_w_down, v_g_final]))
    big_names = [n for n, _, _, _ in BIG]
    B, S, _ = x.shape
    T = B * S
    mx, my, mc = _pos()
    chip = 2 * mx + my
    pos = jnp.stack([mc, chip]).astype(jnp.int32)

    shard2d = lambda a: a.reshape(a.shape[-2], a.shape[-1])
    big2d = lambda d, n: shard2d(d[n]).T if n == "w_in" else shard2d(d[n])
    shard_bf = {n: big2d(W, n).astype(BF16) for n in big_names}
    ag_mid = ["w_mkv", "w_out", "w_mq", "w_mo"]
    ag_ffn = ["w_gu", "w_down"]
    cw_mine = jnp.pad(shard2d(conv_w), ((0, 1), (0, 0)))
    w_in_slab, cw_slab = _run_comm(_ag_comm([shard_bf["w_in"], cw_mine]), "ag_w_in")
    slab = {"w_in": w_in_slab}
    w_int = w_in_slab.reshape(D_IN, D)
    w_ft = jnp.pad(w_int[OFF_F:D_IN], ((0, D_IN_PAD - D_IN), (0, 0)))
    cw = jnp.transpose(cw_slab, (1, 0, 2)).reshape(CONV_HALO, CONV_CH)

    row = lambda a: a.reshape(1, -1)
    bf_pad = jnp.pad(row(b_f), ((0, 0), (0, LANES - 8)))
    x2d = x.reshape(T, D)
    mem2d = mem.reshape(B * MEM_LEN, D)
    tgt = loss_target.reshape(T, D)

    (h, u, gt, q, k, v, zf, c, cq, qx, kx, qT), got = _fwd_in(x2d, row(g_mix), w_int, w_ft, bf_pad, B, S,
                                                  comm=_ag_comm([shard_bf[n] for n in ag_mid[:2]]))
    slab.update(zip(ag_mid[:2], got))
    ckT = jnp.transpose(c.reshape(B, S, LANES)[:, :, :8], (0, 2, 1)).reshape(B, N_PAIR, 2, S)
    ckT = jnp.pad(ckT, ((0, 0), (0, 0), (0, 6), (0, 0)))
    (y, co), got = _conv_fwd(u, gt, cw, row(conv_b), row(ln_g), row(ln_b), B, S,
                             comm=_ag_comm([shard_bf[n] for n in ag_mid[2:]]))
    slab.update(zip(ag_mid[2:], got))
    (o, fox_bias), got = _fox_fwd(qx, kx, v, cq, B, S, comm=_ag_comm([shard_bf[n] for n in ag_ffn]))
    slab.update(zip(ag_ffn, got))
    full = {n: slab[n] if by_col else slab[n].reshape(4 * r, c) for n, r, c, by_col in BIG}
    mn, km, vm = _mem_kv(mem2d, row(g_mem), full["w_mkv"], B)
    (x1, hx, qm, om, x2, cat), _ = _fwd_mid(x2d, co, o, km, vm, full["w_out"], full["w_mq"], full["w_mo"], row(g_x), B, S)
    hf, gu, act, dx3, loss_p, dg_final = _fwd_ffn(x2, tgt, full["w_gu"], full["w_down"], row(g_ffn), row(g_final), T)

    pos_sum = lambda gs, rcvs, ns: [_chip_sum(g, r, pos, "rs_chip_sum_" + n) for g, r, n in zip(gs, rcvs, ns)]
    fin_sum = lambda gs, rcvs, rcs, ns: [_final_sum(g, r, q3, pos, "rs_final_sum_" + n)
                                         for g, r, q3, n in zip(gs, rcvs, rcs, ns)]
    RH = {}
    dgu, dx2, dg_ffn = _bwd_ffn(dx3, gu, x2, full["w_gu"], full["w_down"], row(g_ffn), T)
    g_ffn_w = [_dw(hf, dgu, "dw_gu", FF_CHUNK, slabs=True), _dw(act, dx3, "dw_down", 512).reshape(4, D_FF // 4, D)]
    (dx1, dqm, dco, do, dd, dkm, dvm, dg_x, doT), rcv_ffn = _bwd_mid(dx2, x1, qm, km, vm, o, full["w_mo"], full["w_mq"],
                                                                full["w_out"], row(g_x), B, S, comm=_sibling_comm(g_ffn_w))
    pb_ffn = pos_sum(g_ffn_w, rcv_ffn, ag_ffn)
    dkv, dg_mem = _mem_bwd(dkm, dvm, mem2d, full["w_mkv"], row(g_mem), B)
    g_mid_w = [_dw(mn, dkv, "dw_mkv", 512, slabs=True), _dw(cat, dx1, "dw_out", 512).reshape(4, 256, D),
               _dw(hx, dqm, "dw_mq", 512).reshape(4, 256, D), _dw(om, dx2, "dw_mo", 512).reshape(4, 256, D)]
    (dq, dk, dv, dc, dcq), got = _fox_bwd(q, k, v, do, fox_bias, dd, ckT, qT, doT, B, S,
                                          comm=_join(_ici_comm(pb_ffn), _sibling_comm(g_mid_w)))
    rc_ffn, rcv_mid = got[:len(pb_ffn)], got[len(pb_ffn):]
    RH.update(zip(ag_ffn, fin_sum(g_ffn_w, rcv_ffn, rc_ffn, ag_ffn)))
    pb_mid = pos_sum(g_mid_w, rcv_mid, ag_mid)
    dc8 = jnp.transpose(dc[:, :, :2, :].reshape(B, 8, S), (0, 2, 1)).reshape(T, 8)
    dc8 = dc8 + dcq.reshape(T, 8, HEAD_D)[:, :, 0]
    dzf, dbf = _fgate_bwd(jnp.pad(dc8, ((0, 0), (0, LANES - 8))), zf, B, S)
    (du, dgt, dcw, dvec), rc_mid = _conv_bwd(dco, y, u, gt, cw, row(ln_g), row(ln_b), B, S, comm=_ici_comm(pb_mid))
    RH.update(zip(ag_mid, fin_sum(g_mid_w, rcv_mid, rc_mid, ag_mid)))
    dz = jnp.concatenate([du, dgt, dq, dk, dv, dzf], axis=1)
    g_in_w = [_dw(dz, h, "dw_in", 512, tk=D_IN_PAD // 3, rows=D_IN).reshape(4, D_IN // 4, D)]
    rcv_in = _run_comm(_sibling_comm(g_in_w), "rs_sibling_in")
    (grad_x, dg_mix), rc_in = _bwd_in(dz, w_int, w_ft, x2d, dx1, row(g_mix), T,
                                      comm=_ici_comm(pos_sum(g_in_w, rcv_in, ["w_in"])))
    RH.update(zip(["w_in"], fin_sum(g_in_w, rcv_in, rc_in, ["w_in"])))

    small_g = {"g_mix": dg_mix, "b_f": dbf[:, :8], "conv_w": dcw, "conv_b": dvec[0], "ln_g": dvec[1], "ln_b": dvec[2],
               "g_x": dg_x, "g_mem": dg_mem, "g_ffn": dg_ffn, "g_final": dg_final, "loss": loss_p[:, :1]}
    sg, filled = _small_allreduce(_pack_small(small_g, CONV_HALO * 4), "allreduce_small", [RH[n] for n in big_names])
    shared = dict(zip(big_names, filled))
    G, DL, NM, NV = {}, {}, {}, {}
    for n in big_names:
        G[n], DL[n], NM[n], NV[n] = _adamw(big2d(W, n), shared[n], big2d(M, n), big2d(V, n), "adamw_" + n,
                                           _half_shape(shared[n].shape))
    shapes = {n: W[n].shape for n in names if n not in big_names}
    shapes["conv_w"] = (CONV_HALO, CONV_CH)
    shapes["loss"] = (1,)
    sgrads = _unpack_small(sg, shapes, CONV_HALO * 4)
    loss = sgrads.pop("loss")[0]
    sgrads["conv_w"] = lax.dynamic_slice(sgrads["conv_w"], (0, chip * LANES), (CONV_K, LANES)).reshape(W["conv_w"].shape)
    spack = lambda d: _pack_small({n: d[n] for n in sgrads}, CONV_HALO)
    _, sd, snm, snv = _adamw(spack(W), spack(sgrads), spack(M), spack(V), "adamw_small", (8, LANES))
    sshapes = {n: W[n].shape for n in sgrads}
    SD, SNM, SNV = (_unpack_small(a, sshapes, CONV_HALO) for a in (sd, snm, snv))

    def collect(bigs, smalls):
        back = lambda n: (bigs[n].T if n == "w_in" else bigs[n]).reshape(W[n].shape)
        return [back(n) if n in big_names else smalls[n] for n in names]

    return (loss, grad_x.reshape(x.shape), *collect(G, sgrads), *collect(DL, SD), *collect(NM, SNM), *collect(NV, SNV))
```

```python
import functools
import math

import jax
import jax.numpy as jnp
from jax import lax
from jax.experimental import pallas as pl
from jax.experimental.pallas import tpu as pltpu

F32, BF16 = jnp.float32, jnp.bfloat16
MESH = pl.DeviceIdType.MESH

D = 1024
CONV_CH = 512
CONV_K = 31
CONV_HALO = 32
FOX_W = 512
HEAD_D = 64
N_PAIR = 4
MEM_LEN = 256
MEM_HEADS = 4
MEM_HD = 256
D_FF = 2816
FF_CHUNK = 1408
D_IN = 2568
D_IN_PAD = 2688
OFF_F = 2560
EPS = 1e-6
LANES = 128

ADAM_LR, ADAM_B1, ADAM_B2, ADAM_EPS, ADAM_WD, ADAM_STEP = 0.001, 0.9, 0.999, 1e-08, 0.01, 10

VMEM_LIMIT = 60 * 1024 * 1024

BIG = (("w_out", 256, 1024, False), ("w_mq", 256, 1024, False), ("w_mkv", 1024, 512, True),
       ("w_mo", 256, 1024, False), ("w_gu", 1024, 1408, True), ("w_down", 704, 1024, False),
       ("w_in", 642, 1024, False))

ANY = pl.BlockSpec(memory_space=pl.ANY)


def _sig(x):
    return 1.0 / (1.0 + jnp.exp(-x))


def _dot(a, b):
    return jnp.dot(a, b, preferred_element_type=F32)


def _dot_nt(a, b):
    return lax.dot_general(a, b, (((1,), (1,)), ((), ())), preferred_element_type=F32)


def _dot_tn(a, b):
    return lax.dot_general(a, b, (((0,), (0,)), ((), ())), preferred_element_type=F32)


def _split3(x):
    hi = x.astype(BF16)
    r = x - hi.astype(F32)
    mid = r.astype(BF16)
    return hi, mid, (r - mid.astype(F32)).astype(BF16)


def _dot_01(a, b):
    if a.dtype == jnp.bool_:
        return sum(_dot(a.astype(BF16), t) for t in _split3(b))
    return sum(_dot(t, b.astype(BF16)) for t in _split3(a))


def _resident(a):
    nd = a.ndim
    return pl.BlockSpec(a.shape, lambda *_: (0,) * nd, pipeline_mode=pl.Buffered(1))


def _acc_spec(shape):
    nd = len(shape)
    return pl.BlockSpec(shape, lambda *_: (0,) * nd)


def _params(n_grid):
    return pltpu.CompilerParams(dimension_semantics=("arbitrary",) * n_grid, vmem_limit_bytes=VMEM_LIMIT)


def _sds(shape, dtype):
    return jax.ShapeDtypeStruct(shape, dtype)


def _rms(x):
    r = lax.rsqrt(jnp.mean(x * x, axis=-1, keepdims=True) + EPS)
    return r, x * r


def _rms_bwd(dy, xh, r, g):
    dxh = dy * g
    dx = r * (dxh - xh * jnp.mean(dxh * xh, axis=-1, keepdims=True))
    return dx, dy * xh


def _head_expand(rows, cols):
    hd = lax.broadcasted_iota(jnp.int32, (rows, cols), 1) // HEAD_D
    hr = lax.broadcasted_iota(jnp.int32, (rows, cols), 0)
    return hd == hr


def _feat_major_spec(TB, FOX_T, nb):
    return pl.BlockSpec((1, TB // FOX_T, FOX_W, FOX_T), lambda b, j: (b, j, 0, 0))


def _fwd_in(x2, g_mix, w_int, w_ft, bf_pad, B, S, comm=None):
    T = B * S
    TB = min(512, S)
    nb = S // TB
    FOX_T = min(256, S)

    def body(x_ref, g_ref, w_ref, wf_ref, bf_ref, h_ref, u_ref, gt_ref, q_ref, k_ref, v_ref, zf_ref, c_ref, cq_ref,
             qx_ref, kx_ref, qt_ref, carry):
        j = pl.program_id(1)

        @pl.when(j == 0)
        def _():
            carry[...] = jnp.zeros_like(carry)

        _, xh = _rms(x_ref[...])
        h = (xh * g_ref[...]).astype(BF16)
        h_ref[...] = h
        u_ref[...] = _dot_nt(h, w_ref[0:512, :])
        gt_ref[...] = _dot_nt(h, w_ref[512:1024, :])
        qf = _dot_nt(h, w_ref[1024:1536, :])
        qb = qf.astype(BF16)
        kb = _dot_nt(h, w_ref[1536:2048, :]).astype(BF16)
        q_ref[...] = qb
        k_ref[...] = kb
        for t in range(TB // FOX_T):
            qt_ref[0, t] = qf[t * FOX_T:(t + 1) * FOX_T, :].T.astype(BF16)
        v_ref[...] = _dot_nt(h, w_ref[2048:2560, :]).astype(BF16)
        zf = _dot_nt(h, wf_ref[...]) + bf_ref[...]
        zf_ref[...] = zf
        lane = lax.broadcasted_iota(jnp.int32, zf.shape, 1)
        logf = jnp.where(lane < 8, jnp.minimum(zf, 0.0) - jnp.log(1.0 + jnp.exp(-jnp.abs(zf))), 0.0)
        row = lax.broadcasted_iota(jnp.int32, (TB, TB), 0)
        col = lax.broadcasted_iota(jnp.int32, (TB, TB), 1)
        c = _dot_01(row >= col, logf) + carry[0:1, :]
        carry[0:1, :] = c[TB - 1:TB, :]
        c_ref[...] = c
        cq = _dot_01(c, _head_expand(LANES, FOX_W))
        cq_ref[...] = cq
        hl = lax.broadcasted_iota(jnp.int32, (TB, LANES), 1)
        for hd in range(2 * N_PAIR):
            grp = slice((hd // 2) * LANES, (hd // 2 + 1) * LANES)
            swap = (lambda t: t) if hd % 2 == 0 else (lambda t: pltpu.roll(t, HEAD_D, 1))
            qf = swap(qb[:, grp].astype(F32) * (1.0 / math.sqrt(HEAD_D)))
            kf = swap(kb[:, grp].astype(F32))
            cv = cq[:, grp] if hd % 2 == 1 else pltpu.roll(cq[:, grp], HEAD_D, 1)
            hi = cv.astype(BF16).astype(F32)
            mid = (cv - hi).astype(BF16).astype(F32)
            lo = (cv - hi - mid).astype(BF16).astype(F32)
            pick = lambda a, b, c3, one_from, one_to: jnp.where(hl == a[0], a[1], jnp.where(hl == b[0], b[1], jnp.where(
                hl == c3[0], c3[1], jnp.where((hl >= one_from) & (hl < one_to), 1.0, 0.0))))
            qx = jnp.where(hl < HEAD_D, qf, pick((67, hi), (68, mid), (69, lo), 64, 67))
            kx = jnp.where(hl < HEAD_D, kf, pick((64, -hi), (65, -mid), (66, -lo), 67, 70))
            qx_ref[:, hd * LANES:(hd + 1) * LANES] = qx.astype(BF16)
            kx_ref[:, hd * LANES:(hd + 1) * LANES] = kx.astype(BF16)

    tok = lambda w: pl.BlockSpec((TB, w), lambda b, j: (b * nb + j, 0))
    outs = [(D, BF16), (512, F32), (512, F32), (512, BF16), (512, BF16), (512, BF16), (LANES, F32),
            (LANES, F32), (FOX_W, F32), (2 * FOX_W, BF16), (2 * FOX_W, BF16)]
    return _call(
        body, comm, name="fwd_in", grid=(B, nb),
        in_specs=[tok(D), _resident(g_mix), _resident(w_int), _resident(w_ft), _resident(bf_pad)],
        out_specs=[tok(w) for w, _ in outs] + [_feat_major_spec(TB, FOX_T, nb)],
        out_shape=[_sds((T, w), dt) for w, dt in outs] + [_sds((B, S // FOX_T, FOX_W, FOX_T), BF16)],
        scratch_shapes=[pltpu.VMEM((8, LANES), F32)],
        args=(x2, g_mix, w_int, w_ft, bf_pad))


def _head_sum(n):
    hc = lax.broadcasted_iota(jnp.int32, (n, n), 1) // HEAD_D
    hr = lax.broadcasted_iota(jnp.int32, (n, n), 0) // HEAD_D
    return hc == hr


def _layernorm_silu(y, lg, lb):
    mu = jnp.mean(y, axis=-1, keepdims=True)
    yc = y - mu
    rs = lax.rsqrt(jnp.mean(yc * yc, axis=-1, keepdims=True) + EPS)
    n = yc * rs
    l = n * lg + lb
    return rs, n, l


SUB = 8


def _shifted_copies(cat, sh, rows):
    for r in range(1, SUB):
        sh[r, 0:rows, :] = cat[r:r + rows, :]


def _tap(cat, sh, off, rows, cols=slice(None)):
    r = off % SUB
    return cat[off:off + rows, cols] if r == 0 else sh[r, off - r:off - r + rows, cols]


CONV_ROWS = 128


def _conv_pieces(CB):
    rows = min(CONV_ROWS, CB)
    return [(r0, rows, slice(c0, c0 + LANES)) for c0 in range(0, CONV_CH, LANES) for r0 in range(0, CB, rows)]


def _conv_fwd(u, gt, cw, cb, lng, lnb, B, S, comm=None):
    T = B * S
    CB = min(256, S)
    nb = S // CB

    def body(u_ref, gt_ref, w_ref, cb_ref, lg_ref, lb_ref, y_ref, co_ref, acat, ash):
        j = pl.program_id(1)

        @pl.when(j == 0)
        def _():
            acat[0:CONV_HALO, :] = jnp.zeros((CONV_HALO, CONV_CH), F32)

        acat[CONV_HALO:CONV_HALO + CB, :] = u_ref[...] * _sig(gt_ref[...])
        _shifted_copies(acat, ash, CB + CONV_HALO - SUB)
        for r0, rows, cs in _conv_pieces(CB):
            acc = jnp.zeros((rows, LANES), F32) + cb_ref[:, cs]
            for k in range(CONV_K):
                acc = acc + w_ref[k:k + 1, cs] * _tap(acat, ash, r0 + CONV_HALO - (CONV_K - 1) + k, rows, cs)
            y_ref[r0:r0 + rows, cs] = acc
        acat[0:CONV_HALO, :] = acat[CB:CB + CONV_HALO, :]
        _, _, l = _layernorm_silu(y_ref[...], lg_ref[...], lb_ref[...])
        co_ref[...] = (l * _sig(l)).astype(BF16)

    tok = lambda w: pl.BlockSpec((CB, w), lambda b, j: (b * nb + j, 0))
    return _call(
        body, comm, name="conv_fwd", grid=(B, nb),
        in_specs=[tok(CONV_CH), tok(CONV_CH), _resident(cw), _resident(cb), _resident(lng), _resident(lnb)],
        out_specs=[tok(CONV_CH), tok(CONV_CH)],
        out_shape=[_sds((T, CONV_CH), F32), _sds((T, CONV_CH), BF16)],
        scratch_shapes=[pltpu.VMEM((CONV_HALO + CB, CONV_CH), F32),
                        pltpu.VMEM((SUB, CB + CONV_HALO - SUB, CONV_CH), F32)],
        args=(u, gt, cw, cb, lng, lnb))


def _fox_fwd(qx, kx, v, cq, B, S, comm=None):
    T = B * S
    TQ = min(256, S)
    nq = S // TQ

    def body(qa_ref, qb_ref, ka_ref, kb_ref, v_ref, cq_ref, o_ref, lse_ref, s_scr, s_odd, m_scr, acc_scr):
        i = pl.program_id(2)
        lane = lax.broadcasted_iota(jnp.int32, (TQ, LANES), 1)
        lo = lane < HEAD_D
        qh = (qa_ref[...], qb_ref[...])
        kh = (ka_ref, kb_ref)
        m_scr[...] = jnp.full(m_scr.shape, -1e30, F32)
        acc_scr[...] = jnp.zeros_like(acc_scr)
        row = lax.broadcasted_iota(jnp.int32, (TQ, TQ), 0)
        col = lax.broadcasted_iota(jnp.int32, (TQ, TQ), 1)
        wide = lambda x: jnp.concatenate([x, x], axis=1) if TQ == 2 * LANES else jnp.tile(x, (1, TQ // LANES))

        def scores(j, s_buf):
            start = pl.multiple_of(j * TQ, TQ)
            for h in range(2):
                s_buf[h] = _dot_nt(qh[h], kh[h][pl.ds(start, TQ), :])

        def softmax_step(j, s_buf, diagonal):
            start = pl.multiple_of(j * TQ, TQ)
            vj = v_ref[pl.ds(start, TQ), :]
            for h in range(2):
                def logits():
                    return jnp.where(col <= row, s_buf[h], -1e30) if diagonal else s_buf[h]

                m_old = m_scr[h]
                m_new = jnp.maximum(m_old, jnp.max(logits(), axis=-1, keepdims=True))
                alpha = jnp.exp(m_old - m_new)
                m_scr[h] = m_new
                p = jnp.exp(logits() - wide(m_new)).astype(BF16)
                vx = jnp.where(lo if h == 0 else ~lo, vj, jnp.ones_like(vj))
                acc_scr[h] = alpha * acc_scr[h] + _dot(p, vx)

        def two_blocks(jj, carry):
            j = 2 * jj
            scores(j + 1, s_odd)
            softmax_step(j, s_scr, False)
            scores(j + 2, s_scr)
            softmax_step(j + 1, s_odd, False)
            return carry

        scores(0, s_scr)
        lax.fori_loop(0, i // 2, two_blocks, 0)

        @pl.when(i % 2 == 0)
        def _():
            softmax_step(i, s_scr, True)

        @pl.when(i % 2 == 1)
        def _():
            scores(i, s_odd)
            softmax_step(i - 1, s_scr, False)
            softmax_step(i, s_odd, True)

        acc = jnp.where(lo, acc_scr[0], acc_scr[1])
        den = pltpu.roll(jnp.where(lo, acc_scr[1], acc_scr[0]), HEAD_D, 1)
        o_ref[...] = acc / den
        lse_ref[...] = cq_ref[...] - (jnp.where(lo, m_scr[0], m_scr[1]) + jnp.log(den))

    qspec = pl.BlockSpec((TQ, LANES), lambda b, p, i: (b * nq + i, p))
    kspec = pl.BlockSpec((S, LANES), lambda b, p, i: (b, p))
    qhead = lambda h: pl.BlockSpec((TQ, LANES), lambda b, p, i: (b * nq + i, 2 * p + h))
    khead = lambda h: pl.BlockSpec((S, LANES), lambda b, p, i: (b, 2 * p + h))
    return _call(
        body, comm, name="fox_fwd", grid=(B, N_PAIR, nq),
        in_specs=[qhead(0), qhead(1), khead(0), khead(1), kspec, qspec],
        out_specs=[qspec, qspec],
        out_shape=[_sds((T, FOX_W), F32), _sds((T, FOX_W), F32)],
        scratch_shapes=[pltpu.VMEM((2, TQ, TQ), F32), pltpu.VMEM((2, TQ, TQ), F32),
                        pltpu.VMEM((2, TQ, LANES), F32), pltpu.VMEM((2, TQ, LANES), F32)],
        args=(qx, qx, kx, kx, v, cq))


def _mem_kv(mem2, g_mem, w_mkv, B):
    def body(m_ref, g_ref, w_ref, mn_ref, km_ref, vm_ref):
        _, xh = _rms(m_ref[...])
        mn = (xh * g_ref[...]).astype(BF16)
        mn_ref[...] = mn
        for s in range(2):
            km_ref[:, 512 * s:512 * (s + 1)] = _dot(mn, w_ref[s]).astype(BF16)
            vm_ref[:, 512 * s:512 * (s + 1)] = _dot(mn, w_ref[2 + s]).astype(BF16)

    blk = pl.BlockSpec((MEM_LEN, D), lambda b: (b, 0))
    return pl.pallas_call(
        body, name="mem_kv", grid=(B,),
        in_specs=[blk, _resident(g_mem), _resident(w_mkv)],
        out_specs=[blk, blk, blk],
        out_shape=[_sds((B * MEM_LEN, D), BF16)] * 3,
        compiler_params=_params(1),
    )(mem2, g_mem, w_mkv)


def _mem_probs(qm, km):
    ps = []
    for h in range(MEM_HEADS):
        hs = slice(h * MEM_HD, (h + 1) * MEM_HD)
        lg = _dot_nt(qm[:, hs], km[:, hs]) * (1.0 / math.sqrt(MEM_HD))
        e = jnp.exp(lg - jnp.max(lg, axis=-1, keepdims=True))
        ps.append(e / jnp.sum(e, axis=-1, keepdims=True))
    return ps


def _fwd_mid(x2, co, o, km, vm, w_out, w_mq, w_mo, g_x, B, S, comm=None):
    T = B * S
    TB = min(512, S)
    nb = S // TB

    def body(x_ref, co_ref, o_ref, km_ref, vm_ref, wo_ref, wq_ref, wm_ref, g_ref,
             x1_ref, hx_ref, qm_ref, om_ref, x2_ref, cat_ref):
        cat_ref[:, 0:CONV_CH] = co_ref[...]
        cat_ref[:, CONV_CH:D] = o_ref[...].astype(BF16)
        x1 = x_ref[...] + _dot(cat_ref[...], wo_ref[...])
        x1_ref[...] = x1
        _, xh = _rms(x1)
        hx = (xh * g_ref[...]).astype(BF16)
        hx_ref[...] = hx
        qm = _dot(hx, wq_ref[...]).astype(BF16)
        qm_ref[...] = qm
        ps = _mem_probs(qm, km_ref[...])
        vmv = vm_ref[...]
        for h in range(MEM_HEADS):
            hs = slice(h * MEM_HD, (h + 1) * MEM_HD)
            om_ref[:, hs] = _dot(ps[h].astype(BF16), vmv[:, hs]).astype(BF16)
        x2_ref[...] = x1 + _dot(om_ref[...], wm_ref[...])

    tok = lambda w: pl.BlockSpec((TB, w), lambda b, j: (b * nb + j, 0))
    memb = pl.BlockSpec((MEM_LEN, D), lambda b, j: (b, 0))
    outs = [(D, F32), (D, BF16), (D, BF16), (D, BF16), (D, F32), (D, BF16)]
    return _call(
        body, comm, name="fwd_mid", grid=(B, nb),
        in_specs=[tok(D), tok(CONV_CH), tok(FOX_W), memb, memb, _resident(w_out), _resident(w_mq), _resident(w_mo),
                  _resident(g_x)],
        out_specs=[tok(w) for w, _ in outs],
        out_shape=[_sds((T, w), dt) for w, dt in outs],
        scratch_shapes=[],
        args=(x2, co, o, km, vm, w_out, w_mq, w_mo, g_x))


def _load_gate_up(wgu_hbm, wg, wu, sems):
    copies = [pltpu.make_async_copy(wgu_hbm.at[s], (wg if s < 2 else wu).at[:, pl.ds((s % 2) * FF_CHUNK, FF_CHUNK)],
                                    sems.at[s]) for s in range(4)]
    for cp in copies:
        cp.start()
    for cp in copies:
        cp.wait()


def _fwd_ffn(x2, tgt, w_gu, w_down, g_ffn, g_final, T):
    TB = min(256, T)
    nb = T // TB

    def body(x_ref, t_ref, wgu_ref, wd_ref, gf_ref, gl_ref, hf_ref, gu_ref, act_ref, dx3_ref, loss_ref, dgl_ref,
             wg, wu, sems):
        i = pl.program_id(0)

        @pl.when(i == 0)
        def _():
            _load_gate_up(wgu_ref, wg, wu, sems)
            loss_ref[...] = jnp.zeros_like(loss_ref)
            dgl_ref[...] = jnp.zeros_like(dgl_ref)

        x2v = x_ref[...]
        _, xh = _rms(x2v)
        hf = (xh * gf_ref[...]).astype(BF16)
        hf_ref[...] = hf
        g = _dot(hf, wg[...])
        u = _dot(hf, wu[...])
        gu_ref[:, 0:D_FF] = g
        gu_ref[:, D_FF:2 * D_FF] = u
        act = (g * _sig(g) * u).astype(BF16)
        act_ref[...] = act
        x3 = x2v + _dot(act, wd_ref[...])
        r3, xh3 = _rms(x3)
        gl = gl_ref[...]
        e = xh3 * gl - t_ref[...]
        loss_ref[...] += jnp.sum(e * e) * (0.5 / D)
        dy = e * (1.0 / D)
        dx3, dgl = _rms_bwd(dy, xh3, r3, gl)
        dx3_ref[...] = dx3
        dgl_ref[...] += jnp.sum(dgl, axis=0, keepdims=True)

    tok = lambda w: pl.BlockSpec((TB, w), lambda i: (i, 0))
    return pl.pallas_call(
        body, name="fwd_ffn", grid=(nb,),
        in_specs=[tok(D), tok(D), ANY, _resident(w_down), _resident(g_ffn), _resident(g_final)],
        out_specs=[tok(D), tok(2 * D_FF), tok(D_FF), tok(D), _acc_spec((1, LANES)), _acc_spec((1, D))],
        out_shape=[_sds((T, D), BF16), _sds((T, 2 * D_FF), F32), _sds((T, D_FF), BF16), _sds((T, D), F32),
                   _sds((1, LANES), F32), _sds((1, D), F32)],
        scratch_shapes=[pltpu.VMEM((D, D_FF), BF16), pltpu.VMEM((D, D_FF), BF16), pltpu.SemaphoreType.DMA((4,))],
        compiler_params=_params(1),
    )(x2, tgt, w_gu, w_down, g_ffn, g_final)


def _bwd_ffn(dx3, gu, x2, w_gu, w_down, g_ffn, T):
    TB = min(256, T)
    nb = T // TB

    def body(d_ref, gu_ref, x_ref, wgu_ref, wd_ref, gf_ref, dgu_ref, dx2_ref, dgf_ref, wg, wu, sems):
        i = pl.program_id(0)

        @pl.when(i == 0)
        def _():
            _load_gate_up(wgu_ref, wg, wu, sems)
            dgf_ref[...] = jnp.zeros_like(dgf_ref)

        dx3v = d_ref[...]
        db = dx3v.astype(BF16)
        dact = _dot_nt(db, wd_ref[...])
        g = gu_ref[:, 0:D_FF]
        u = gu_ref[:, D_FF:2 * D_FF]
        sg = _sig(g)
        dg = (dact * u * sg * (1.0 + g * (1.0 - sg))).astype(BF16)
        du = (dact * g * sg).astype(BF16)
        dgu_ref[:, 0:D_FF] = dg
        dgu_ref[:, D_FF:2 * D_FF] = du
        dhf = _dot_nt(dg, wg[...]) + _dot_nt(du, wu[...])
        r2, xh2 = _rms(x_ref[...])
        dx, dg_tok = _rms_bwd(dhf, xh2, r2, gf_ref[...])
        dx2_ref[...] = dx3v + dx
        dgf_ref[...] += jnp.sum(dg_tok, axis=0, keepdims=True)

    tok = lambda w: pl.BlockSpec((TB, w), lambda i: (i, 0))
    return pl.pallas_call(
        body, name="bwd_ffn", grid=(nb,),
        in_specs=[tok(D), tok(2 * D_FF), tok(D), ANY, _resident(w_down), _resident(g_ffn)],
        out_specs=[tok(2 * D_FF), tok(D), _acc_spec((1, D))],
        out_shape=[_sds((T, 2 * D_FF), BF16), _sds((T, D), F32), _sds((1, D), F32)],
        scratch_shapes=[pltpu.VMEM((D, D_FF), BF16), pltpu.VMEM((D, D_FF), BF16), pltpu.SemaphoreType.DMA((4,))],
        compiler_params=_params(1),
    )(dx3, gu, x2, w_gu, w_down, g_ffn)


def _bwd_mid(dx2, x1, qm, km, vm, o, w_mo, w_mq, w_out, g_x, B, S, comm=None):
    T = B * S
    TB = min(512, S)
    nb = S // TB
    FOX_T = min(256, S)
    inv = 1.0 / math.sqrt(MEM_HD)

    def body(d_ref, x1_ref, qm_ref, km_ref, vm_ref, o_ref, wm_ref, wq_ref, wo_ref, g_ref,
             dx1_ref, dqm_ref, dco_ref, do_ref, dd_ref, dkm_ref, dvm_ref, dgx_ref, dot_ref):
        b = pl.program_id(0)
        j = pl.program_id(1)

        @pl.when((b == 0) & (j == 0))
        def _():
            dgx_ref[...] = jnp.zeros_like(dgx_ref)

        @pl.when(j == 0)
        def _():
            dkm_ref[...] = jnp.zeros_like(dkm_ref)
            dvm_ref[...] = jnp.zeros_like(dvm_ref)

        dx2v = d_ref[...]
        dom = _dot_nt(dx2v.astype(BF16), wm_ref[...]).astype(BF16)
        qmv = qm_ref[...]
        kmv = km_ref[...]
        vmv = vm_ref[...]
        ps = _mem_probs(qmv, kmv)
        for h in range(MEM_HEADS):
            hs = slice(h * MEM_HD, (h + 1) * MEM_HD)
            p = ps[h]
            dp = _dot_nt(dom[:, hs], vmv[:, hs])
            ds = (p * (dp - jnp.sum(p * dp, axis=-1, keepdims=True))).astype(BF16)
            dqm_ref[:, hs] = (_dot(ds, kmv[:, hs]) * inv).astype(BF16)
            dkm_ref[:, hs] += _dot_tn(ds, qmv[:, hs]) * inv
            dvm_ref[:, hs] += _dot_tn(p.astype(BF16), dom[:, hs])
        dhx = _dot_nt(dqm_ref[...], wq_ref[...])
        r1, xh1 = _rms(x1_ref[...])
        dx, dg_tok = _rms_bwd(dhx, xh1, r1, g_ref[...])
        dx1 = dx2v + dx
        dx1_ref[...] = dx1
        dgx_ref[...] += jnp.sum(dg_tok, axis=0, keepdims=True)
        d1b = dx1.astype(BF16)
        dco_ref[...] = _dot_nt(d1b, wo_ref[0:CONV_CH, :])
        do = _dot_nt(d1b, wo_ref[CONV_CH:D, :])
        dob = do.astype(BF16)
        do_ref[...] = dob
        for t in range(TB // FOX_T):
            dot_ref[0, t] = do[t * FOX_T:(t + 1) * FOX_T, :].T.astype(BF16)
        dd_ref[...] = _dot_01(dob.astype(F32) * o_ref[...], _head_sum(FOX_W))

    tok = lambda w: pl.BlockSpec((TB, w), lambda b, j: (b * nb + j, 0))
    memb = pl.BlockSpec((MEM_LEN, D), lambda b, j: (b, 0))
    outs = [(D, F32), (D, BF16), (CONV_CH, F32), (FOX_W, BF16), (FOX_W, F32)]
    return _call(
        body, comm, name="bwd_mid", grid=(B, nb),
        in_specs=[tok(D), tok(D), tok(D), memb, memb, tok(FOX_W), _resident(w_mo), _resident(w_mq), _resident(w_out),
                  _resident(g_x)],
        out_specs=[tok(w) for w, _ in outs] + [memb, memb, _acc_spec((1, D)), _feat_major_spec(TB, FOX_T, nb)],
        out_shape=[_sds((T, w), dt) for w, dt in outs] + [_sds((B * MEM_LEN, D), F32)] * 2 + [_sds((1, D), F32)]
        + [_sds((B, S // FOX_T, FOX_W, FOX_T), BF16)],
        scratch_shapes=[],
        args=(dx2, x1, qm, km, vm, o, w_mo, w_mq, w_out, g_x))


def _mem_bwd(dkm, dvm, mem2, w_mkv, g_mem, B):
    def body(dk_ref, dv_ref, m_ref, w_ref, g_ref, dkv_ref, dg_ref):
        b = pl.program_id(0)

        @pl.when(b == 0)
        def _():
            dg_ref[...] = jnp.zeros_like(dg_ref)

        dk = dk_ref[...].astype(BF16)
        dv = dv_ref[...].astype(BF16)
        dkv_ref[:, 0:D] = dk
        dkv_ref[:, D:2 * D] = dv
        dmn = jnp.zeros((MEM_LEN, D), F32)
        for s in range(2):
            dmn = dmn + _dot_nt(dk[:, 512 * s:512 * (s + 1)], w_ref[s]) + _dot_nt(dv[:, 512 * s:512 * (s + 1)], w_ref[2 + s])
        _, xh = _rms(m_ref[...])
        dg_ref[...] += jnp.sum(dmn * xh, axis=0, keepdims=True)

    blk = pl.BlockSpec((MEM_LEN, D), lambda b: (b, 0))
    return pl.pallas_call(
        body, name="mem_bwd", grid=(B,),
        in_specs=[blk, blk, blk, _resident(w_mkv), _resident(g_mem)],
        out_specs=[pl.BlockSpec((MEM_LEN, 2 * D), lambda b: (b, 0)), _acc_spec((1, D))],
        out_shape=[_sds((B * MEM_LEN, 2 * D), BF16), _sds((1, D), F32)],
        compiler_params=_params(1),
    )(dkm, dvm, mem2, w_mkv, g_mem)


def _fox_bwd(q, k, v, do, bias, dd, ckT, qT, doT, B, S, comm=None):
    T = B * S
    TK = min(256, S)
    nk = S // TK
    scale = 1.0 / math.sqrt(HEAD_D)

    def body(q_ref, k_ref, v_ref, do_ref, bias_ref, dd_ref, ck_ref, qt_ref, dot_ref, dq_ref, dk_ref, dv_ref, dc_ref,
             dcq_ref, dq_acc, rs_acc, s_scr, dp_scr, s_odd, dp_odd, dk_acc, dv_acc, dc_acc):
        j = pl.program_id(2)

        @pl.when(j == 0)
        def _():
            dq_acc[...] = jnp.zeros_like(dq_acc)
            rs_acc[...] = jnp.zeros_like(rs_acc)

        dk_acc[...] = jnp.zeros_like(dk_acc)
        dv_acc[...] = jnp.zeros_like(dv_acc)
        dc_acc[...] = jnp.zeros_like(dc_acc)
        lane = lax.broadcasted_iota(jnp.int32, (TK, LANES), 1)
        lo = lane < HEAD_D
        ks = k_ref[...] * jnp.asarray(scale, BF16)
        v2 = v_ref[...]
        zero = jnp.zeros_like(ks)
        kh = (jnp.where(lo, ks, zero), jnp.where(lo, zero, ks))
        vh = (jnp.where(lo, v2, zero), jnp.where(lo, zero, v2))
        kstart = pl.multiple_of(j * TK, TK)
        ckh = tuple(ck_ref[0, 0, h:h + 1, pl.ds(kstart, TK)] for h in range(2))
        row = lax.broadcasted_iota(jnp.int32, (TK, TK), 0)
        col = lax.broadcasted_iota(jnp.int32, (TK, TK), 1)
        wide = lambda x: jnp.concatenate([x, x], axis=1) if TK == 2 * LANES else jnp.tile(x, (1, TK // LANES))

        def scores(i, s_buf, dp_buf):
            start = pl.multiple_of(i * TK, TK)
            qi = q_ref[pl.ds(start, TK), :]
            doi = do_ref[pl.ds(start, TK), :]
            for h in range(2):
                s_buf[h] = _dot_nt(qi, kh[h])
                dp_buf[h] = _dot_nt(doi, vh[h])

        def grads(i, s_buf, dp_buf, diagonal):
            start = pl.multiple_of(i * TK, TK)
            bias2 = bias_ref[pl.ds(start, TK), :]
            dd2 = dd_ref[pl.ds(start, TK), :]
            for h in range(2):
                hc = slice(h * HEAD_D, h * HEAD_D + 1)
                bias = jnp.broadcast_to(bias2[:, hc], (TK, LANES))
                ddh = jnp.broadcast_to(dd2[:, hc], (TK, LANES))
                p = jnp.exp((s_buf[h] - ckh[h]) + wide(bias))
                if diagonal:
                    p = jnp.where(col <= row, p, 0.0)
                ds = p * (dp_buf[h] - wide(ddh))
                dc_acc[h, 0:1, :] += jnp.sum(ds, axis=0, keepdims=True)
                rs_acc[h, pl.ds(start, TK), :] += jnp.sum(ds, axis=1, keepdims=True)
                pb = p.astype(BF16)
                dsb = ds.astype(BF16)
                feat = slice(h * HEAD_D, (h + 1) * HEAD_D)
                dv_acc[feat, :] += _dot(dot_ref[0, i, feat, :], pb)
                dk_acc[feat, :] += _dot(qt_ref[0, i, feat, :], dsb)
                dq_acc[pl.ds(start, TK), :] += _dot(dsb, kh[h])

        n_off = nk - 1 - j
        block = lambda t: jnp.where(t < n_off, j + 1 + t, j)

        def two_blocks(tt, carry):
            t = 2 * tt
            scores(block(t + 1), s_odd, dp_odd)
            grads(block(t), s_scr, dp_scr, False)
            scores(block(t + 2), s_scr, dp_scr)
            grads(block(t + 1), s_odd, dp_odd, False)
            return carry

        scores(block(0), s_scr, dp_scr)
        lax.fori_loop(0, n_off // 2, two_blocks, 0)

        @pl.when(n_off % 2 == 0)
        def _():
            grads(j, s_scr, dp_scr, True)

        @pl.when(n_off % 2 == 1)
        def _():
            scores(j, s_odd, dp_odd)
            grads(nk - 1, s_scr, dp_scr, False)
            grads(j, s_odd, dp_odd, True)

        dk_ref[...] = (dk_acc[...].T * scale).astype(BF16)
        dv_ref[...] = dv_acc[...].T.astype(BF16)
        sub = lax.broadcasted_iota(jnp.int32, (8, TK), 0)
        dca = dc_acc[0, 0:1, :]
        dcb = dc_acc[1, 0:1, :]
        dc_ref[0, 0] = jnp.where(sub == 0, -dca, jnp.where(sub == 1, -dcb, 0.0))

        @pl.when(j == nk - 1)
        def _():
            dq_ref[...] = dq_acc[...].astype(BF16)
            lo_s = lax.broadcasted_iota(jnp.int32, (S, LANES), 1) < HEAD_D
            dcq_ref[...] = jnp.where(lo_s, rs_acc[0], rs_acc[1])

    full = pl.BlockSpec((S, LANES), lambda b, p, j: (b, p))
    blk = pl.BlockSpec((TK, LANES), lambda b, p, j: (b * nk + j, p))
    featT = pl.BlockSpec((1, nk, LANES, TK), lambda b, p, j: (b, 0, p, 0))
    return _call(
        body, comm, name="fox_bwd", grid=(B, N_PAIR, nk),
        in_specs=[full, blk, blk, full, full, full, pl.BlockSpec((1, 1, 8, S), lambda b, p, j: (b, p, 0, 0)),
                  featT, featT],
        out_specs=[full, blk, blk, pl.BlockSpec((1, 1, 8, TK), lambda b, p, j: (b, p, 0, j)), full],
        out_shape=[_sds((T, FOX_W), BF16), _sds((T, FOX_W), BF16), _sds((T, FOX_W), BF16),
                   _sds((B, N_PAIR, 8, S), F32), _sds((T, FOX_W), F32)],
        scratch_shapes=[pltpu.VMEM((S, LANES), F32), pltpu.VMEM((2, S, 1), F32),
                        pltpu.VMEM((2, TK, TK), F32), pltpu.VMEM((2, TK, TK), F32),
                        pltpu.VMEM((2, TK, TK), F32), pltpu.VMEM((2, TK, TK), F32),
                        pltpu.VMEM((LANES, TK), F32), pltpu.VMEM((LANES, TK), F32), pltpu.VMEM((2, 8, TK), F32)],
        args=(q, k, v, do, bias, dd, ckT, qT, doT))


def _fgate_bwd(dc8, zf, B, S):
    T = B * S
    TB = min(512, S)
    nb = S // TB

    def body(dc_ref, zf_ref, dzf_ref, dbf_ref, carry):
        b = pl.program_id(0)
        j = pl.program_id(1)

        @pl.when((b == 0) & (j == 0))
        def _():
            dbf_ref[...] = jnp.zeros_like(dbf_ref)

        @pl.when(j == 0)
        def _():
            carry[...] = jnp.zeros_like(carry)

        dc = dc_ref[...]
        row = lax.broadcasted_iota(jnp.int32, (TB, TB), 0)
        col = lax.broadcasted_iota(jnp.int32, (TB, TB), 1)
        dlogf = _dot_01(col >= row, dc) + carry[0:1, :]
        carry[0:1, :] = dlogf[0:1, :]
        lane = lax.broadcasted_iota(jnp.int32, dc.shape, 1)
        dzf = jnp.where(lane < 8, dlogf * _sig(-zf_ref[...]), 0.0)
        dzf_ref[...] = dzf.astype(BF16)
        dbf_ref[...] += jnp.sum(dzf, axis=0, keepdims=True)

    tok = pl.BlockSpec((TB, LANES), lambda b, j: (b * nb + (nb - 1 - j), 0))
    return pl.pallas_call(
        body, name="fgate_bwd", grid=(B, nb),
        in_specs=[tok, tok],
        out_specs=[tok, _acc_spec((1, LANES))],
        out_shape=[_sds((T, LANES), BF16), _sds((1, LANES), F32)],
        scratch_shapes=[pltpu.VMEM((8, LANES), F32)],
        compiler_params=_params(2),
    )(dc8, zf)


def _conv_bwd(dco, y, u, gt, cw, lng, lnb, B, S, comm=None):
    T = B * S
    CB = min(256, S)
    nb = S // CB
    hb = CB // CONV_HALO

    def body(dco_ref, y_ref, u_ref, gt_ref, up_ref, gp_ref, w_ref, lg_ref, lb_ref,
             du_ref, dgt_ref, dw_ref, vec_ref, acat, dycat, ash, dysh):
        b = pl.program_id(0)
        j = pl.program_id(1)
        jr = nb - 1 - j

        @pl.when((b == 0) & (j == 0))
        def _():
            dw_ref[...] = jnp.zeros_like(dw_ref)
            vec_ref[...] = jnp.zeros_like(vec_ref)

        @pl.when(j == 0)
        def _():
            dycat[CB:CB + CONV_HALO, :] = jnp.zeros((CONV_HALO, CONV_CH), F32)

        lg = lg_ref[...]
        rs, n, l = _layernorm_silu(y_ref[...], lg, lb_ref[...])
        sg = _sig(l)
        dl = dco_ref[...] * (sg * (1.0 + l * (1.0 - sg)))
        dn = dl * lg
        dy = rs * (dn - jnp.mean(dn, axis=-1, keepdims=True) - n * jnp.mean(dn * n, axis=-1, keepdims=True))
        vec_ref[0:1, :] += jnp.sum(dy, axis=0, keepdims=True)
        vec_ref[1:2, :] += jnp.sum(dl * n, axis=0, keepdims=True)
        vec_ref[2:3, :] += jnp.sum(dl, axis=0, keepdims=True)
        dycat[0:CB, :] = dy
        acat[0:CONV_HALO, :] = jnp.where(jr > 0, up_ref[...] * _sig(gp_ref[...]), 0.0)
        acat[CONV_HALO:CONV_HALO + CB, :] = u_ref[...] * _sig(gt_ref[...])
        _shifted_copies(acat, ash, CB + CONV_HALO - SUB)
        _shifted_copies(dycat, dysh, CB + CONV_HALO - SUB)
        for r0, rows, cs in _conv_pieces(CB):
            dyp = dycat[r0:r0 + rows, cs]
            da = jnp.zeros((rows, LANES), F32)
            for k in range(CONV_K):
                da = da + w_ref[k:k + 1, cs] * _tap(dycat, dysh, r0 + CONV_K - 1 - k, rows, cs)
                dw_ref[k:k + 1, cs] += jnp.sum(dyp * _tap(acat, ash, r0 + CONV_HALO - (CONV_K - 1) + k, rows, cs),
                                               axis=0, keepdims=True)
            uv = u_ref[r0:r0 + rows, cs]
            sgt = _sig(gt_ref[r0:r0 + rows, cs])
            du_ref[r0:r0 + rows, cs] = (da * sgt).astype(BF16)
            dgt_ref[r0:r0 + rows, cs] = (da * uv * sgt * (1.0 - sgt)).astype(BF16)
        dycat[CB:CB + CONV_HALO, :] = dycat[0:CONV_HALO, :]

    tok = lambda w: pl.BlockSpec((CB, w), lambda b, j: (b * nb + (nb - 1 - j), 0))
    prev = pl.BlockSpec((CONV_HALO, CONV_CH), lambda b, j: (jnp.maximum((b * nb + (nb - 1 - j)) * hb - 1, 0), 0))
    return _call(
        body, comm, name="conv_bwd", grid=(B, nb),
        in_specs=[tok(CONV_CH), tok(CONV_CH), tok(CONV_CH), tok(CONV_CH), prev, prev, _resident(cw), _resident(lng),
                  _resident(lnb)],
        out_specs=[tok(CONV_CH), tok(CONV_CH), _acc_spec((CONV_HALO, CONV_CH)), _acc_spec((8, CONV_CH))],
        out_shape=[_sds((T, CONV_CH), BF16), _sds((T, CONV_CH), BF16), _sds((CONV_HALO, CONV_CH), F32),
                   _sds((8, CONV_CH), F32)],
        scratch_shapes=[pltpu.VMEM((CONV_HALO + CB, CONV_CH), F32), pltpu.VMEM((CB + CONV_HALO, CONV_CH), F32),
                        pltpu.VMEM((SUB, CB + CONV_HALO - SUB, CONV_CH), F32),
                        pltpu.VMEM((SUB, CB + CONV_HALO - SUB, CONV_CH), F32)],
        args=(dco, y, u, gt, u, gt, cw, lng, lnb))


def _bwd_in(dz, w_int, w_ft, x2, dx1, g_mix, T, comm=None):
    TB = min(512, T)
    nb = T // TB

    def body(dz_ref, w_ref, wf_ref, x_ref, d1_ref, g_ref, gx_ref, dg_ref):
        i = pl.program_id(0)

        @pl.when(i == 0)
        def _():
            dg_ref[...] = jnp.zeros_like(dg_ref)

        dh = _dot(dz_ref[:, 0:OFF_F], w_ref[0:OFF_F, :]) + _dot(dz_ref[:, OFF_F:D_IN_PAD], wf_ref[...])
        r0, xh0 = _rms(x_ref[...])
        dx, dg_tok = _rms_bwd(dh, xh0, r0, g_ref[...])
        gx_ref[...] = d1_ref[...] + dx
        dg_ref[...] += jnp.sum(dg_tok, axis=0, keepdims=True)

    tok = lambda w: pl.BlockSpec((TB, w), lambda i: (i, 0))
    return _call(
        body, comm, name="bwd_in", grid=(nb,),
        in_specs=[tok(D_IN_PAD), _resident(w_int), _resident(w_ft), tok(D), tok(D), _resident(g_mix)],
        out_specs=[tok(D), _acc_spec((1, D))],
        out_shape=[_sds((T, D), F32), _sds((1, D), F32)],
        scratch_shapes=[],
        args=(dz, w_int, w_ft, x2, dx1, g_mix))


def _dw(a, b, name, tn, slabs=0, tk=None, rows=None):
    T, K = a.shape
    N = b.shape[1]
    per = tn // slabs if slabs else 1
    tk = tk or (K if K <= 1024 else K // 2)
    tt = min(1024, T)
    nt = T // tt

    def body(a_ref, b_ref, o_ref, acc):
        t = pl.program_id(2)

        @pl.when(t == 0)
        def _():
            acc[...] = jnp.zeros_like(acc)

        acc[...] += _dot_tn(a_ref[...].astype(BF16), b_ref[...].astype(BF16))

        @pl.when(t == nt - 1)
        def _():
            if slabs:
                for sl in range(per):
                    o_ref[sl] = acc[:, sl * slabs:(sl + 1) * slabs]
            else:
                o_ref[...] = acc[...]

    return pl.pallas_call(
        body, name=name, grid=(K // tk, N // tn, nt),
        in_specs=[pl.BlockSpec((tt, tk), lambda i, j, t: (t, i)), pl.BlockSpec((tt, tn), lambda i, j, t: (t, j))],
        out_specs=(pl.BlockSpec((per, tk, slabs), lambda i, j, t: (j, i, 0)) if slabs
                   else pl.BlockSpec((tk, tn), lambda i, j, t: (i, j))),
        out_shape=_sds((N // slabs, K, slabs) if slabs else (rows or K, N), F32),
        scratch_shapes=[pltpu.VMEM((tk, tn), F32)],
        compiler_params=_params(3),
    )(a, b)


def _pos():
    return lax.axis_index("x"), lax.axis_index("y"), lax.axis_index("c")


def _remote(src, dst, ssem, rsem, to):
    return pltpu.make_async_remote_copy(src_ref=src, dst_ref=dst, send_sem=ssem, recv_sem=rsem, device_id=to,
                                        device_id_type=MESH)


def _split_axis(shape):
    return 0 if shape[0] % 32 == 0 else 1


def _half_shape(shape, parts=2):
    return (shape[0] // parts, shape[1]) if _split_axis(shape) == 0 else (shape[0], shape[1] // parts)


def _half(shape, c):
    R, C = shape
    if _split_axis(shape) == 0:
        return (pl.ds(pl.multiple_of(c * (R // 2), 16), R // 2), slice(None))
    return (slice(None), pl.ds(pl.multiple_of(c * (C // 2), LANES), C // 2))


def _half_block(shape, parts, lead, which):
    blk = _half_shape(shape, parts)
    idx = (which, 0) if _split_axis(shape) == 0 else (0, which)
    return blk, tuple(lead) + idx


class _Comm:
    def __init__(self, ins, out_shapes, sems, start, finish):
        self.ins, self.out_shapes, self.sems, self.start, self.finish = list(ins), list(out_shapes), list(sems), start, finish


def _ag_comm(shards):
    n = len(shards)

    def parts(ins, outs, sems):
        send_sems, recv_sems, local_sems = sems
        x, y, c = _pos()
        me, sib = (x, y, c), (x, y, 1 - c)
        chips = [(1 - x, y), (x, 1 - y), (1 - x, 1 - y)]

        def rows(w, px, py, pc):
            return outs[w].at[(2 * px + py,) + _half(shards[w].shape, pc)]

        def copy(w, k, block, to, src=None):
            return _remote(rows(w, *block) if src is None else src, rows(w, *block), send_sems.at[w, k],
                           recv_sems.at[w, k], to)

        mine, first = [], []
        for w in range(n):
            src = ins[w].at[_half(shards[w].shape, c)]
            mine.append(pltpu.make_async_copy(src, rows(w, *me), local_sems.at[w]))
            first += [copy(w, 0, me, sib, src=src)] + [copy(w, 1 + j, me, (*chip, c), src=src) for j, chip in enumerate(chips)]
        return c, me, sib, chips, copy, mine, first

    def start(ins, outs, sems):
        _, _, _, _, _, mine, first = parts(ins, outs, sems)
        for cp in mine + first:
            cp.start()

    def finish(ins, outs, sems):
        c, me, sib, chips, copy, mine, first = parts(ins, outs, sems)
        passed = []
        for w in range(n):
            for j, chip in enumerate(chips):
                copy(w, 1 + j, (*chip, c), me).wait_recv()
                passed.append(copy(w, 4 + j, (*chip, c), sib))
                passed[-1].start()
        for w in range(n):
            copy(w, 0, sib, me).wait_recv()
            for j, chip in enumerate(chips):
                copy(w, 4 + j, (*chip, 1 - c), me).wait_recv()
        for cp in first + passed:
            cp.wait_send()
        for cp in mine:
            cp.wait()

    D7 = pltpu.SemaphoreType.DMA((n, 7))
    return _Comm(shards, [_sds((4,) + s.shape, s.dtype) for s in shards], [D7, D7, pltpu.SemaphoreType.DMA((n,))],
                 start, finish)


def _sibling_comm(gs):
    n = len(gs)

    def copies(ins, outs, sems):
        send_sems, recv_sems = sems
        x, y, c = _pos()
        return [_remote(ins[w].at[(s,) + _half(gs[w].shape[1:], 1 - c)], outs[w].at[s], send_sems.at[w, s],
                        recv_sems.at[w, s], (x, y, 1 - c)) for w in range(n) for s in range(4)]

    def start(ins, outs, sems):
        for cp in copies(ins, outs, sems):
            cp.start()

    def finish(ins, outs, sems):
        for cp in copies(ins, outs, sems):
            cp.wait()

    D4 = pltpu.SemaphoreType.DMA((n, 4))
    return _Comm(gs, [_sds((4,) + _half_shape(g.shape[1:]), F32) for g in gs], [D4, D4], start, finish)


def _ici_comm(pbs):
    n = len(pbs)

    def copies(ins, outs, sems):
        send_sems, recv_sems = sems
        x, y, c = _pos()
        return [_remote(ins[w].at[2 * tx + ty], outs[w].at[j], send_sems.at[w, j], recv_sems.at[w, j], (tx, ty, c))
                for w in range(n) for j, (tx, ty) in enumerate([(1 - x, y), (x, 1 - y), (1 - x, 1 - y)])]

    def start(ins, outs, sems):
        for cp in copies(ins, outs, sems):
            cp.start()

    def finish(ins, outs, sems):
        for cp in copies(ins, outs, sems):
            cp.wait()

    D3 = pltpu.SemaphoreType.DMA((n, 3))
    return _Comm(pbs, [_sds((3,) + p.shape[1:], BF16) for p in pbs], [D3, D3], start, finish)


def _join(*comms):
    counts = [(len(c.ins), len(c.out_shapes), len(c.sems)) for c in comms]

    def each(which):
        def run(ins, outs, sems):
            i = o = k = 0
            for c, (ni, no, nk) in zip(comms, counts):
                getattr(c, which)(ins[i:i + ni], outs[o:o + no], sems[k:k + nk])
                i, o, k = i + ni, o + no, k + nk
        return run

    return _Comm(sum((c.ins for c in comms), []), sum((c.out_shapes for c in comms), []),
                 sum((c.sems for c in comms), []), each("start"), each("finish"))


def _run_comm(comm, name):
    ni, no = len(comm.ins), len(comm.out_shapes)

    def body(*refs):
        ins, outs, sems = refs[:ni], refs[ni:ni + no], refs[ni + no:]
        comm.start(ins, outs, sems)
        comm.finish(ins, outs, sems)

    return pl.pallas_call(body, name=name, out_shape=comm.out_shapes, in_specs=[ANY] * ni, out_specs=[ANY] * no,
                          scratch_shapes=comm.sems)(*comm.ins)


def _call(body, comm, *, name, grid, in_specs, out_specs, out_shape, scratch_shapes, args):
    n_grid = len(grid)
    if comm is None:
        res = pl.pallas_call(body, name=name, grid=grid, in_specs=in_specs, out_specs=out_specs, out_shape=out_shape,
                             scratch_shapes=scratch_shapes, compiler_params=_params(n_grid))(*args)
        return list(res), []
    n_in, n_out, n_scr = len(in_specs), len(out_specs), len(scratch_shapes)
    ni, no = len(comm.ins), len(comm.out_shapes)

    def carried(*refs):
        ins, refs = refs[:n_in], refs[n_in:]
        cins, refs = refs[:ni], refs[ni:]
        outs, refs = refs[:n_out], refs[n_out:]
        couts, refs = refs[:no], refs[no:]
        scr, csems = refs[:n_scr], refs[n_scr:]
        ids = [pl.program_id(ax) for ax in range(n_grid)]
        first = functools.reduce(jnp.logical_and, [i == 0 for i in ids])
        last = functools.reduce(jnp.logical_and, [i == g - 1 for i, g in zip(ids, grid)])

        @pl.when(first)
        def _():
            comm.start(cins, couts, csems)

        body(*ins, *outs, *scr)

        @pl.when(last)
        def _():
            comm.finish(cins, couts, csems)

    res = pl.pallas_call(
        carried, name=name, grid=grid, in_specs=list(in_specs) + [ANY] * ni, out_specs=list(out_specs) + [ANY] * no,
        out_shape=list(out_shape) + comm.out_shapes, scratch_shapes=list(scratch_shapes) + comm.sems,
        compiler_params=_params(n_grid))(*args, *comm.ins)
    return list(res[:n_out]), list(res[n_out:])


def _small_allreduce(v, name, halves=()):
    P = v.shape[0]
    n = len(halves)
    vm = pl.BlockSpec(memory_space=pltpu.VMEM)

    def body(v_ref, *refs):
        o_ref, outs = refs[n], refs[n + 1:2 * n + 1]
        gath, send_sems, recv_sems, half_send, half_recv = refs[2 * n + 1:]
        x, y, c = _pos()
        me = 4 * x + 2 * y + c
        gath[me] = v_ref[...]
        cps = []
        for r in range(1, 8):
            tx = (1 - x) if r & 4 else x
            ty = (1 - y) if r & 2 else y
            tc = (1 - c) if r & 1 else c
            cps.append(_remote(v_ref, gath.at[me], send_sems.at[r - 1], recv_sems.at[r - 1], (tx, ty, tc)))
        for w in range(n):
            mine = outs[w].at[_half(halves[w].shape, c)]
            cps.append(_remote(mine, mine, half_send.at[w], half_recv.at[w], (x, y, 1 - c)))
        for cp in cps:
            cp.start()
        for cp in cps:
            cp.wait()
        acc = gath[0]
        for d in range(1, 8):
            acc = acc + gath[d]
        o_ref[...] = acc

    res = pl.pallas_call(
        body, name=name, out_shape=[_sds((P, LANES), F32)] + [_sds(g.shape, F32) for g in halves],
        in_specs=[vm] + [ANY] * n, out_specs=[vm] + [ANY] * n, input_output_aliases={1 + w: 1 + w for w in range(n)},
        scratch_shapes=[pltpu.VMEM((8, P, LANES), F32), pltpu.SemaphoreType.DMA((7,)), pltpu.SemaphoreType.DMA((7,)),
                        pltpu.SemaphoreType.DMA((max(n, 1),)), pltpu.SemaphoreType.DMA((max(n, 1),))],
    )(v, *halves)
    return res[0], list(res[1:])


def _chip_sum(g, rcv, pos, name):
    shard = g.shape[1:]
    hs = _half_shape(shard)
    other = lambda i, pos: (pos[1] + 1 + i) % 4

    def body(pos_ref, g_ref, r_ref, o_ref):
        o_ref[...] = (g_ref[...] + r_ref[...]).astype(BF16)

    return pl.pallas_call(
        body, name=name, out_shape=_sds((4,) + hs, BF16),
        grid_spec=pltpu.PrefetchScalarGridSpec(
            num_scalar_prefetch=1, grid=(3,),
            in_specs=[pl.BlockSpec((1,) + hs, lambda i, pos: _half_block(shard, 2, (other(i, pos),), pos[0])[1]),
                      pl.BlockSpec((1,) + hs, lambda i, pos: (other(i, pos), 0, 0))],
            out_specs=pl.BlockSpec((1,) + hs, lambda i, pos: (other(i, pos), 0, 0))),
        compiler_params=_params(1),
    )(pos, g, rcv)


def _final_sum(g, rcv, rc, pos, name):
    shard = g.shape[1:]
    qs = _half_shape(shard, 4)

    def body(pos_ref, g_ref, r_ref, rc_ref, o_ref):
        acc = g_ref[0] + r_ref[0]
        for j in range(3):
            acc = acc + rc_ref[j].astype(F32)
        o_ref[...] = acc

    return pl.pallas_call(
        body, name=name, out_shape=_sds(shard, F32),
        grid_spec=pltpu.PrefetchScalarGridSpec(
            num_scalar_prefetch=1, grid=(2,),
            in_specs=[pl.BlockSpec((1,) + qs, lambda i, pos: _half_block(shard, 4, (pos[1],), pos[0] * 2 + i)[1]),
                      pl.BlockSpec((1,) + qs, lambda i, pos: _half_block(shard, 4, (pos[1],), i)[1]),
                      pl.BlockSpec((3,) + qs, lambda i, pos: _half_block(shard, 4, (0,), i)[1])],
            out_specs=pl.BlockSpec(qs, lambda i, pos: _half_block(shard, 4, (), pos[0] * 2 + i)[1])),
        compiler_params=_params(1),
    )(pos, g, rcv, rc)


def _adamw_math(w, g, m, v):
    m = ADAM_B1 * m + (1.0 - ADAM_B1) * g
    v = ADAM_B2 * v + (1.0 - ADAM_B2) * (g * g)
    m_hat = m / (1.0 - ADAM_B1 ** ADAM_STEP)
    v_hat = v / (1.0 - ADAM_B2 ** ADAM_STEP)
    delta = -ADAM_LR * (m_hat / (jnp.sqrt(v_hat) + ADAM_EPS) + ADAM_WD * w)
    return delta, m, v


def _adamw(w, g, m, v, name, blk_shape):
    R, C = w.shape

    def body(w_ref, g_ref, m_ref, v_ref, go_ref, d_ref, nm_ref, nv_ref):
        g = g_ref[...]
        d, nm, nv = _adamw_math(w_ref[...], g, m_ref[...], v_ref[...])
        go_ref[...] = g
        d_ref[...] = d
        nm_ref[...] = nm
        nv_ref[...] = nv

    blk = pl.BlockSpec(blk_shape, lambda i, j: (i, j))
    return pl.pallas_call(
        body, name=name, grid=(R // blk_shape[0], C // blk_shape[1]), in_specs=[blk] * 4, out_specs=[blk] * 4,
        out_shape=[_sds((R, C), F32)] * 4, compiler_params=_params(2),
    )(w, g, m, v)


SMALL = (("g_mix", 8), ("b_f", 8), ("conv_w", None), ("conv_b", 8), ("ln_g", 8), ("ln_b", 8), ("g_x", 8), ("g_mem", 8),
         ("g_ffn", 8), ("g_final", 8), ("loss", 8))


def _pack_small(parts, conv_rows):
    rows = []
    for name, n in SMALL:
        if name not in parts:
            continue
        n = conv_rows if n is None else n
        flat = parts[name].reshape(-1).astype(F32)
        flat = jnp.pad(flat, (0, n * LANES - flat.shape[0]))
        rows.append(flat.reshape(n, LANES))
    return jnp.concatenate(rows, axis=0)


def _unpack_small(p, shapes, conv_rows):
    out, off = {}, 0
    for name, n in SMALL:
        if name not in shapes:
            continue
        n = conv_rows if n is None else n
        size = math.prod(shapes[name])
        out[name] = p[off:off + n].reshape(-1)[:size].reshape(shapes[name])
        off += n
    return out


def kernel(x, mem, g_mix, w_in, b_f, conv_w, conv_b, ln_g, ln_b, w_out, g_x, g_mem, w_mq, w_mkv, w_mo, g_ffn, w_gu, w_down, g_final, loss_target, m_g_mix, m_w_in, m_b_f, m_conv_w, m_conv_b, m_ln_g, m_ln_b, m_w_out, m_g_x, m_g_mem, m_w_mq, m_w_mkv, m_w_mo, m_g_ffn, m_w_gu, m_w_down, m_g_final, v_g_mix, v_w_in, v_b_f, v_conv_w, v_conv_b, v_ln_g, v_ln_b, v_w_out, v_g_x, v_g_mem, v_w_mq, v_w_mkv, v_w_mo, v_g_ffn, v_w_gu, v_w_down, v_g_final):
    names = ["g_mix", "w_in", "b_f", "conv_w", "conv_b", "ln_g", "ln_b", "w_out", "g_x", "g_mem", "w_mq", "w_mkv",
             "w_mo", "g_ffn", "w_gu", "w_down", "g_final"]
    W = dict(zip(names, [g_mix, w_in, b_f, conv_w, conv_b, ln_g, ln_b, w_out, g_x, g_mem, w_mq, w_mkv, w_mo, g_ffn,
                         w_gu, w_down, g_final]))
    M = dict(zip(names, [m_g_mix, m_w_in, m_b_f, m_conv_w, m_conv_b, m_ln_g, m_ln_b, m_w_out, m_g_x, m_g_mem, m_w_mq,
                         m_w_mkv, m_w_mo, m_g_ffn, m_w_gu, m_w_down, m_g_final]))
    V = dict(zip(names, [v_g_mix, v_w_in, v_b_f, v_conv_w, v_conv_b, v_ln_g, v_ln_b, v_w_out, v_g_x, v_g_mem, v_w_mq,
                         v_w_mkv, v_w_mo, v_g_ffn, v_w_gu, v_w_down, v_g_final]))
    big_names = [n for n, _, _, _ in BIG]
    B, S, _ = x.shape
    T = B * S
    mx, my, mc = _pos()
    chip = 2 * mx + my
    pos = jnp.stack([mc, chip]).astype(jnp.int32)

    shard2d = lambda a: a.reshape(a.shape[-2], a.shape[-1])
    big2d = lambda d, n: shard2d(d[n]).T if n == "w_in" else shard2d(d[n])
    shard_bf = {n: big2d(W, n).astype(BF16) for n in big_names}
    ag_mid = ["w_mkv", "w_out", "w_mq", "w_mo"]
    ag_ffn = ["w_gu", "w_down"]
    cw_mine = jnp.pad(shard2d(conv_w), ((0, 1), (0, 0)))
    w_in_slab, cw_slab = _run_comm(_ag_comm([shard_bf["w_in"], cw_mine]), "ag_w_in")
    slab = {"w_in": w_in_slab}
    w_int = w_in_slab.reshape(D_IN, D)
    w_ft = jnp.pad(w_int[OFF_F:D_IN], ((0, D_IN_PAD - D_IN), (0, 0)))
    cw = jnp.transpose(cw_slab, (1, 0, 2)).reshape(CONV_HALO, CONV_CH)

    row = lambda a: a.reshape(1, -1)
    bf_pad = jnp.pad(row(b_f), ((0, 0), (0, LANES - 8)))
    x2d = x.reshape(T, D)
    mem2d = mem.reshape(B * MEM_LEN, D)
    tgt = loss_target.reshape(T, D)

    (h, u, gt, q, k, v, zf, c, cq, qx, kx, qT), got = _fwd_in(x2d, row(g_mix), w_int, w_ft, bf_pad, B, S,
                                                  comm=_ag_comm([shard_bf[n] for n in ag_mid[:2]]))
    slab.update(zip(ag_mid[:2], got))
    ckT = jnp.transpose(c.reshape(B, S, LANES)[:, :, :8], (0, 2, 1)).reshape(B, N_PAIR, 2, S)
    ckT = jnp.pad(ckT, ((0, 0), (0, 0), (0, 6), (0, 0)))
    (y, co), got = _conv_fwd(u, gt, cw, row(conv_b), row(ln_g), row(ln_b), B, S,
                             comm=_ag_comm([shard_bf[n] for n in ag_mid[2:]]))
    slab.update(zip(ag_mid[2:], got))
    (o, fox_bias), got = _fox_fwd(qx, kx, v, cq, B, S, comm=_ag_comm([shard_bf[n] for n in ag_ffn]))
    slab.update(zip(ag_ffn, got))
    full = {n: slab[n] if by_col else slab[n].reshape(4 * r, c) for n, r, c, by_col in BIG}
    mn, km, vm = _mem_kv(mem2d, row(g_mem), full["w_mkv"], B)
    (x1, hx, qm, om, x2, cat), _ = _fwd_mid(x2d, co, o, km, vm, full["w_out"], full["w_mq"], full["w_mo"], row(g_x), B, S)
    hf, gu, act, dx3, loss_p, dg_final = _fwd_ffn(x2, tgt, full["w_gu"], full["w_down"], row(g_ffn), row(g_final), T)

    pos_sum = lambda gs, rcvs, ns: [_chip_sum(g, r, pos, "rs_chip_sum_" + n) for g, r, n in zip(gs, rcvs, ns)]
    fin_sum = lambda gs, rcvs, rcs, ns: [_final_sum(g, r, q3, pos, "rs_final_sum_" + n)
                                         for g, r, q3, n in zip(gs, rcvs, rcs, ns)]
    RH = {}
    dgu, dx2, dg_ffn = _bwd_ffn(dx3, gu, x2, full["w_gu"], full["w_down"], row(g_ffn), T)
    g_ffn_w = [_dw(hf, dgu, "dw_gu", D_FF, slabs=FF_CHUNK, tk=512), _dw(act, dx3, "dw_down", 512).reshape(4, D_FF // 4, D)]
    (dx1, dqm, dco, do, dd, dkm, dvm, dg_x, doT), rcv_ffn = _bwd_mid(dx2, x1, qm, km, vm, o, full["w_mo"], full["w_mq"],
                                                                full["w_out"], row(g_x), B, S, comm=_sibling_comm(g_ffn_w))
    pb_ffn = pos_sum(g_ffn_w, rcv_ffn, ag_ffn)
    dkv, dg_mem = _mem_bwd(dkm, dvm, mem2d, full["w_mkv"], row(g_mem), B)
    g_mid_w = [_dw(mn, dkv, "dw_mkv", 512, slabs=512), _dw(cat, dx1, "dw_out", 512).reshape(4, 256, D),
               _dw(hx, dqm, "dw_mq", 512).reshape(4, 256, D), _dw(om, dx2, "dw_mo", 512).reshape(4, 256, D)]
    (dq, dk, dv, dc, dcq), got = _fox_bwd(q, k, v, do, fox_bias, dd, ckT, qT, doT, B, S,
                                          comm=_join(_ici_comm(pb_ffn), _sibling_comm(g_mid_w)))
    rc_ffn, rcv_mid = got[:len(pb_ffn)], got[len(pb_ffn):]
    RH.update(zip(ag_ffn, fin_sum(g_ffn_w, rcv_ffn, rc_ffn, ag_ffn)))
    pb_mid = pos_sum(g_mid_w, rcv_mid, ag_mid)
    dc8 = jnp.transpose(dc[:, :, :2, :].reshape(B, 8, S), (0, 2, 1)).reshape(T, 8)
    dc8 = dc8 + dcq.reshape(T, 8, HEAD_D)[:, :, 0]
    dzf, dbf = _fgate_bwd(jnp.pad(dc8, ((0, 0), (0, LANES - 8))), zf, B, S)
    (du, dgt, dcw, dvec), rc_mid = _conv_bwd(dco, y, u, gt, cw, row(ln_g), row(ln_b), B, S, comm=_ici_comm(pb_mid))
    RH.update(zip(ag_mid, fin_sum(g_mid_w, rcv_mid, rc_mid, ag_mid)))
    dz = jnp.concatenate([du, dgt, dq, dk, dv, dzf], axis=1)
    g_in_w = [_dw(dz, h, "dw_in", 512, tk=D_IN_PAD // 3, rows=D_IN).reshape(4, D_IN // 4, D)]
    rcv_in = _run_comm(_sibling_comm(g_in_w), "rs_sibling_in")
    (grad_x, dg_mix), rc_in = _bwd_in(dz, w_int, w_ft, x2d, dx1, row(g_mix), T,
                                      comm=_ici_comm(pos_sum(g_in_w, rcv_in, ["w_in"])))
    RH.update(zip(["w_in"], fin_sum(g_in_w, rcv_in, rc_in, ["w_in"])))

    small_g = {"g_mix": dg_mix, "b_f": dbf[:, :8], "conv_w": dcw, "conv_b": dvec[0], "ln_g": dvec[1], "ln_b": dvec[2],
               "g_x": dg_x, "g_mem": dg_mem, "g_ffn": dg_ffn, "g_final": dg_final, "loss": loss_p[:, :1]}
    sg, filled = _small_allreduce(_pack_small(small_g, CONV_HALO * 4), "allreduce_small", [RH[n] for n in big_names])
    shared = dict(zip(big_names, filled))
    G, DL, NM, NV = {}, {}, {}, {}
    for n in big_names:
        G[n], DL[n], NM[n], NV[n] = _adamw(big2d(W, n), shared[n], big2d(M, n), big2d(V, n), "adamw_" + n,
                                           _half_shape(shared[n].shape))
    shapes = {n: W[n].shape for n in names if n not in big_names}
    shapes["conv_w"] = (CONV_HALO, CONV_CH)
    shapes["loss"] = (1,)
    sgrads = _unpack_small(sg, shapes, CONV_HALO * 4)
    loss = sgrads.pop("loss")[0]
    sgrads["conv_w"] = lax.dynamic_slice(sgrads["conv_w"], (0, chip * LANES), (CONV_K, LANES)).reshape(W["conv_w"].shape)
    spack = lambda d: _pack_small({n: d[n] for n in sgrads}, CONV_HALO)
    _, sd, snm, snv = _adamw(spack(W), spack(sgrads), spack(M), spack(V), "adamw_small", (8, LANES))
    sshapes = {n: W[n].shape for n in sgrads}
    SD, SNM, SNV = (_unpack_small(a, sshapes, CONV_HALO) for a in (sd, snm, snv))

    def collect(bigs, smalls):
        back = lambda n: (bigs[n].T if n == "w_in" else bigs[n]).reshape(W[n].shape)
        return [back(n) if n in big_names else smalls[n] for n in names]

    return (loss, grad_x.reshape(x.shape), *collect(G, sgrads), *collect(DL, SD), *collect(NM, SNM), *collect(NV, SNV))
```

```python
import functools
import math

import jax
import jax.numpy as jnp
from jax import lax
from jax.experimental import pallas as pl
from jax.experimental.pallas import tpu as pltpu

F32, BF16 = jnp.float32, jnp.bfloat16
MESH = pl.DeviceIdType.MESH

D = 1024
CONV_CH = 512
CONV_K = 31
CONV_HALO = 32
FOX_W = 512
HEAD_D = 64
N_PAIR = 4
MEM_LEN = 256
MEM_HEADS = 4
MEM_HD = 256
D_FF = 2816
FF_CHUNK = 1408
D_IN = 2568
D_IN_PAD = 2688
OFF_F = 2560
EPS = 1e-6
LANES = 128

ADAM_LR, ADAM_B1, ADAM_B2, ADAM_EPS, ADAM_WD, ADAM_STEP = 0.001, 0.9, 0.999, 1e-08, 0.01, 10

VMEM_LIMIT = 60 * 1024 * 1024

BIG = (("w_out", 256, 1024, False), ("w_mq", 256, 1024, False), ("w_mkv", 1024, 512, True),
       ("w_mo", 256, 1024, False), ("w_gu", 1024, 1408, True), ("w_down", 704, 1024, False),
       ("w_in", 642, 1024, False))

ANY = pl.BlockSpec(memory_space=pl.ANY)


def _sig(x):
    return 1.0 / (1.0 + jnp.exp(-x))


def _dot(a, b):
    return jnp.dot(a, b, preferred_element_type=F32)


def _dot_nt(a, b):
    return lax.dot_general(a, b, (((1,), (1,)), ((), ())), preferred_element_type=F32)


def _dot_tn(a, b):
    return lax.dot_general(a, b, (((0,), (0,)), ((), ())), preferred_element_type=F32)


def _split3(x):
    hi = x.astype(BF16)
    r = x - hi.astype(F32)
    mid = r.astype(BF16)
    return hi, mid, (r - mid.astype(F32)).astype(BF16)


def _dot_01(a, b):
    if a.dtype == jnp.bool_:
        return sum(_dot(a.astype(BF16), t) for t in _split3(b))
    return sum(_dot(t, b.astype(BF16)) for t in _split3(a))


def _resident(a):
    nd = a.ndim
    return pl.BlockSpec(a.shape, lambda *_: (0,) * nd, pipeline_mode=pl.Buffered(1))


def _acc_spec(shape):
    nd = len(shape)
    return pl.BlockSpec(shape, lambda *_: (0,) * nd)


def _params(n_grid):
    return pltpu.CompilerParams(dimension_semantics=("arbitrary",) * n_grid, vmem_limit_bytes=VMEM_LIMIT)


def _sds(shape, dtype):
    return jax.ShapeDtypeStruct(shape, dtype)


def _rms(x):
    r = lax.rsqrt(jnp.mean(x * x, axis=-1, keepdims=True) + EPS)
    return r, x * r


def _rms_bwd(dy, xh, r, g):
    dxh = dy * g
    dx = r * (dxh - xh * jnp.mean(dxh * xh, axis=-1, keepdims=True))
    return dx, dy * xh


def _head_expand(rows, cols):
    hd = lax.broadcasted_iota(jnp.int32, (rows, cols), 1) // HEAD_D
    hr = lax.broadcasted_iota(jnp.int32, (rows, cols), 0)
    return hd == hr


def _feat_major_spec(TB, FOX_T, nb):
    return pl.BlockSpec((1, TB // FOX_T, FOX_W, FOX_T), lambda b, j: (b, j, 0, 0))


def _norm_in(x2, g_mix, T, comm=None):
    TB = min(1024, T)

    def body(x_ref, g_ref, h_ref):
        _, xh = _rms(x_ref[...])
        h_ref[...] = (xh * g_ref[...]).astype(BF16)

    tok = pl.BlockSpec((TB, D), lambda i: (i, 0))
    return _call(body, comm, name="norm_in", grid=(T // TB,), in_specs=[tok, _resident(g_mix)], out_specs=[tok],
                 out_shape=[_sds((T, D), BF16)], scratch_shapes=[], args=(x2, g_mix))


def _fwd_in(h, w_int, w_ft, bf_pad, B, S, comm=None):
    T = B * S
    TB = min(512, S)
    nb = S // TB
    FOX_T = min(256, S)

    def body(h_ref, w_ref, wf_ref, bf_ref, u_ref, gt_ref, q_ref, k_ref, v_ref, zf_ref, c_ref, cq_ref,
             qx_ref, kx_ref, qt_ref, carry):
        j = pl.program_id(1)

        @pl.when(j == 0)
        def _():
            carry[...] = jnp.zeros_like(carry)

        h = h_ref[...]
        u_ref[...] = _dot_nt(h, w_ref[0:512, :])
        gt_ref[...] = _dot_nt(h, w_ref[512:1024, :])
        qf = _dot_nt(h, w_ref[1024:1536, :])
        qb = qf.astype(BF16)
        kb = _dot_nt(h, w_ref[1536:2048, :]).astype(BF16)
        q_ref[...] = qb
        k_ref[...] = kb
        for t in range(TB // FOX_T):
            qt_ref[0, t] = qf[t * FOX_T:(t + 1) * FOX_T, :].T.astype(BF16)
        v_ref[...] = _dot_nt(h, w_ref[2048:2560, :]).astype(BF16)
        zf = _dot_nt(h, wf_ref[...]) + bf_ref[...]
        zf_ref[...] = zf
        lane = lax.broadcasted_iota(jnp.int32, zf.shape, 1)
        logf = jnp.where(lane < 8, jnp.minimum(zf, 0.0) - jnp.log(1.0 + jnp.exp(-jnp.abs(zf))), 0.0)
        row = lax.broadcasted_iota(jnp.int32, (TB, TB), 0)
        col = lax.broadcasted_iota(jnp.int32, (TB, TB), 1)
        c = _dot_01(row >= col, logf) + carry[0:1, :]
        carry[0:1, :] = c[TB - 1:TB, :]
        c_ref[...] = c
        cq = _dot_01(c, _head_expand(LANES, FOX_W))
        cq_ref[...] = cq
        hl = lax.broadcasted_iota(jnp.int32, (TB, LANES), 1)
        for hd in range(2 * N_PAIR):
            grp = slice((hd // 2) * LANES, (hd // 2 + 1) * LANES)
            swap = (lambda t: t) if hd % 2 == 0 else (lambda t: pltpu.roll(t, HEAD_D, 1))
            qf = swap(qb[:, grp].astype(F32) * (1.0 / math.sqrt(HEAD_D)))
            kf = swap(kb[:, grp].astype(F32))
            cv = cq[:, grp] if hd % 2 == 1 else pltpu.roll(cq[:, grp], HEAD_D, 1)
            hi = cv.astype(BF16).astype(F32)
            mid = (cv - hi).astype(BF16).astype(F32)
            lo = (cv - hi - mid).astype(BF16).astype(F32)
            pick = lambda a, b, c3, one_from, one_to: jnp.where(hl == a[0], a[1], jnp.where(hl == b[0], b[1], jnp.where(
                hl == c3[0], c3[1], jnp.where((hl >= one_from) & (hl < one_to), 1.0, 0.0))))
            qx = jnp.where(hl < HEAD_D, qf, pick((67, hi), (68, mid), (69, lo), 64, 67))
            kx = jnp.where(hl < HEAD_D, kf, pick((64, -hi), (65, -mid), (66, -lo), 67, 70))
            qx_ref[:, hd * LANES:(hd + 1) * LANES] = qx.astype(BF16)
            kx_ref[:, hd * LANES:(hd + 1) * LANES] = kx.astype(BF16)

    tok = lambda w: pl.BlockSpec((TB, w), lambda b, j: (b * nb + j, 0))
    outs = [(512, F32), (512, F32), (512, BF16), (512, BF16), (512, BF16), (LANES, F32),
            (LANES, F32), (FOX_W, F32), (2 * FOX_W, BF16), (2 * FOX_W, BF16)]
    return _call(
        body, comm, name="fwd_in", grid=(B, nb),
        in_specs=[tok(D), _resident(w_int), _resident(w_ft), _resident(bf_pad)],
        out_specs=[tok(w) for w, _ in outs] + [_feat_major_spec(TB, FOX_T, nb)],
        out_shape=[_sds((T, w), dt) for w, dt in outs] + [_sds((B, S // FOX_T, FOX_W, FOX_T), BF16)],
        scratch_shapes=[pltpu.VMEM((8, LANES), F32)],
        args=(h, w_int, w_ft, bf_pad))


def _head_sum(n):
    hc = lax.broadcasted_iota(jnp.int32, (n, n), 1) // HEAD_D
    hr = lax.broadcasted_iota(jnp.int32, (n, n), 0) // HEAD_D
    return hc == hr


def _layernorm_silu(y, lg, lb):
    mu = jnp.mean(y, axis=-1, keepdims=True)
    yc = y - mu
    rs = lax.rsqrt(jnp.mean(yc * yc, axis=-1, keepdims=True) + EPS)
    n = yc * rs
    l = n * lg + lb
    return rs, n, l


SUB = 8


def _shifted_copies(cat, sh, rows):
    for r in range(1, SUB):
        sh[r, 0:rows, :] = cat[r:r + rows, :]


def _tap(cat, sh, off, rows, cols=slice(None)):
    r = off % SUB
    return cat[off:off + rows, cols] if r == 0 else sh[r, off - r:off - r + rows, cols]


CONV_ROWS = 128


def _conv_pieces(CB):
    rows = min(CONV_ROWS, CB)
    return [(r0, rows, slice(c0, c0 + LANES)) for c0 in range(0, CONV_CH, LANES) for r0 in range(0, CB, rows)]


def _conv_fwd(u, gt, cw, cb, lng, lnb, B, S, comm=None):
    T = B * S
    CB = min(256, S)
    nb = S // CB

    def body(u_ref, gt_ref, w_ref, cb_ref, lg_ref, lb_ref, y_ref, co_ref, acat, ash):
        j = pl.program_id(1)

        @pl.when(j == 0)
        def _():
            acat[0:CONV_HALO, :] = jnp.zeros((CONV_HALO, CONV_CH), F32)

        acat[CONV_HALO:CONV_HALO + CB, :] = u_ref[...] * _sig(gt_ref[...])
        _shifted_copies(acat, ash, CB + CONV_HALO - SUB)
        for r0, rows, cs in _conv_pieces(CB):
            acc = jnp.zeros((rows, LANES), F32) + cb_ref[:, cs]
            for k in range(CONV_K):
                acc = acc + w_ref[k:k + 1, cs] * _tap(acat, ash, r0 + CONV_HALO - (CONV_K - 1) + k, rows, cs)
            y_ref[r0:r0 + rows, cs] = acc
        acat[0:CONV_HALO, :] = acat[CB:CB + CONV_HALO, :]
        _, _, l = _layernorm_silu(y_ref[...], lg_ref[...], lb_ref[...])
        co_ref[...] = (l * _sig(l)).astype(BF16)

    tok = lambda w: pl.BlockSpec((CB, w), lambda b, j: (b * nb + j, 0))
    return _call(
        body, comm, name="conv_fwd", grid=(B, nb),
        in_specs=[tok(CONV_CH), tok(CONV_CH), _resident(cw), _resident(cb), _resident(lng), _resident(lnb)],
        out_specs=[tok(CONV_CH), tok(CONV_CH)],
        out_shape=[_sds((T, CONV_CH), F32), _sds((T, CONV_CH), BF16)],
        scratch_shapes=[pltpu.VMEM((CONV_HALO + CB, CONV_CH), F32),
                        pltpu.VMEM((SUB, CB + CONV_HALO - SUB, CONV_CH), F32)],
        args=(u, gt, cw, cb, lng, lnb))


def _fox_fwd(qx, kx, v, cq, B, S, comm=None):
    T = B * S
    TQ = min(256, S)
    nq = S // TQ

    def body(qa_ref, qb_ref, ka_ref, kb_ref, v_ref, cq_ref, o_ref, lse_ref, s_scr, s_odd, m_scr, acc_scr):
        i = pl.program_id(2)
        lane = lax.broadcasted_iota(jnp.int32, (TQ, LANES), 1)
        lo = lane < HEAD_D
        qh = (qa_ref[...], qb_ref[...])
        kh = (ka_ref, kb_ref)
        m_scr[...] = jnp.full(m_scr.shape, -1e30, F32)
        acc_scr[...] = jnp.zeros_like(acc_scr)
        row = lax.broadcasted_iota(jnp.int32, (TQ, TQ), 0)
        col = lax.broadcasted_iota(jnp.int32, (TQ, TQ), 1)
        wide = lambda x: jnp.concatenate([x, x], axis=1) if TQ == 2 * LANES else jnp.tile(x, (1, TQ // LANES))

        def scores(j, s_buf):
            start = pl.multiple_of(j * TQ, TQ)
            for h in range(2):
                s_buf[h] = _dot_nt(qh[h], kh[h][pl.ds(start, TQ), :])

        def softmax_step(j, s_buf, diagonal):
            start = pl.multiple_of(j * TQ, TQ)
            vj = v_ref[pl.ds(start, TQ), :]
            for h in range(2):
                def logits():
                    return jnp.where(col <= row, s_buf[h], -1e30) if diagonal else s_buf[h]

                m_old = m_scr[h]
                m_new = jnp.maximum(m_old, jnp.max(logits(), axis=-1, keepdims=True))
                alpha = jnp.exp(m_old - m_new)
                m_scr[h] = m_new
                p = jnp.exp(logits() - wide(m_new)).astype(BF16)
                vx = jnp.where(lo if h == 0 else ~lo, vj, jnp.ones_like(vj))
                acc_scr[h] = alpha * acc_scr[h] + _dot(p, vx)

        def two_blocks(jj, carry):
            j = 2 * jj
            scores(j + 1, s_odd)
            softmax_step(j, s_scr, False)
            scores(j + 2, s_scr)
            softmax_step(j + 1, s_odd, False)
            return carry

        scores(0, s_scr)
        lax.fori_loop(0, i // 2, two_blocks, 0)

        @pl.when(i % 2 == 0)
        def _():
            softmax_step(i, s_scr, True)

        @pl.when(i % 2 == 1)
        def _():
            scores(i, s_odd)
            softmax_step(i - 1, s_scr, False)
            softmax_step(i, s_odd, True)

        acc = jnp.where(lo, acc_scr[0], acc_scr[1])
        den = pltpu.roll(jnp.where(lo, acc_scr[1], acc_scr[0]), HEAD_D, 1)
        o_ref[...] = acc / den
        lse_ref[...] = cq_ref[...] - (jnp.where(lo, m_scr[0], m_scr[1]) + jnp.log(den))

    qspec = pl.BlockSpec((TQ, LANES), lambda b, p, i: (b * nq + i, p))
    kspec = pl.BlockSpec((S, LANES), lambda b, p, i: (b, p))
    qhead = lambda h: pl.BlockSpec((TQ, LANES), lambda b, p, i: (b * nq + i, 2 * p + h))
    khead = lambda h: pl.BlockSpec((S, LANES), lambda b, p, i: (b, 2 * p + h))
    return _call(
        body, comm, name="fox_fwd", grid=(B, N_PAIR, nq),
        in_specs=[qhead(0), qhead(1), khead(0), khead(1), kspec, qspec],
        out_specs=[qspec, qspec],
        out_shape=[_sds((T, FOX_W), F32), _sds((T, FOX_W), F32)],
        scratch_shapes=[pltpu.VMEM((2, TQ, TQ), F32), pltpu.VMEM((2, TQ, TQ), F32),
                        pltpu.VMEM((2, TQ, LANES), F32), pltpu.VMEM((2, TQ, LANES), F32)],
        args=(qx, qx, kx, kx, v, cq))


def _mem_kv(mem2, g_mem, w_mkv, B):
    def body(m_ref, g_ref, w_ref, mn_ref, km_ref, vm_ref):
        _, xh = _rms(m_ref[...])
        mn = (xh * g_ref[...]).astype(BF16)
        mn_ref[...] = mn
        for s in range(2):
            km_ref[:, 512 * s:512 * (s + 1)] = _dot(mn, w_ref[s]).astype(BF16)
            vm_ref[:, 512 * s:512 * (s + 1)] = _dot(mn, w_ref[2 + s]).astype(BF16)

    blk = pl.BlockSpec((MEM_LEN, D), lambda b: (b, 0))
    return pl.pallas_call(
        body, name="mem_kv", grid=(B,),
        in_specs=[blk, _resident(g_mem), _resident(w_mkv)],
        out_specs=[blk, blk, blk],
        out_shape=[_sds((B * MEM_LEN, D), BF16)] * 3,
        compiler_params=_params(1),
    )(mem2, g_mem, w_mkv)


def _mem_probs(qm, km):
    ps = []
    for h in range(MEM_HEADS):
        hs = slice(h * MEM_HD, (h + 1) * MEM_HD)
        lg = _dot_nt(qm[:, hs], km[:, hs]) * (1.0 / math.sqrt(MEM_HD))
        e = jnp.exp(lg - jnp.max(lg, axis=-1, keepdims=True))
        ps.append(e / jnp.sum(e, axis=-1, keepdims=True))
    return ps


def _fwd_mid(x2, co, o, km, vm, w_out, w_mq, w_mo, g_x, B, S, comm=None):
    T = B * S
    TB = min(512, S)
    nb = S // TB

    def body(x_ref, co_ref, o_ref, km_ref, vm_ref, wo_ref, wq_ref, wm_ref, g_ref,
             x1_ref, hx_ref, qm_ref, om_ref, x2_ref, cat_ref):
        cat_ref[:, 0:CONV_CH] = co_ref[...]
        cat_ref[:, CONV_CH:D] = o_ref[...].astype(BF16)
        x1 = x_ref[...] + _dot(cat_ref[...], wo_ref[...])
        x1_ref[...] = x1
        _, xh = _rms(x1)
        hx = (xh * g_ref[...]).astype(BF16)
        hx_ref[...] = hx
        qm = _dot(hx, wq_ref[...]).astype(BF16)
        qm_ref[...] = qm
        ps = _mem_probs(qm, km_ref[...])
        vmv = vm_ref[...]
        for h in range(MEM_HEADS):
            hs = slice(h * MEM_HD, (h + 1) * MEM_HD)
            om_ref[:, hs] = _dot(ps[h].astype(BF16), vmv[:, hs]).astype(BF16)
        x2_ref[...] = x1 + _dot(om_ref[...], wm_ref[...])

    tok = lambda w: pl.BlockSpec((TB, w), lambda b, j: (b * nb + j, 0))
    memb = pl.BlockSpec((MEM_LEN, D), lambda b, j: (b, 0))
    outs = [(D, F32), (D, BF16), (D, BF16), (D, BF16), (D, F32), (D, BF16)]
    return _call(
        body, comm, name="fwd_mid", grid=(B, nb),
        in_specs=[tok(D), tok(CONV_CH), tok(FOX_W), memb, memb, _resident(w_out), _resident(w_mq), _resident(w_mo),
                  _resident(g_x)],
        out_specs=[tok(w) for w, _ in outs],
        out_shape=[_sds((T, w), dt) for w, dt in outs],
        scratch_shapes=[],
        args=(x2, co, o, km, vm, w_out, w_mq, w_mo, g_x))


def _load_gate_up(wgu_hbm, wg, wu, sems):
    copies = [pltpu.make_async_copy(wgu_hbm.at[s], (wg if s < 2 else wu).at[:, pl.ds((s % 2) * FF_CHUNK, FF_CHUNK)],
                                    sems.at[s]) for s in range(4)]
    for cp in copies:
        cp.start()
    for cp in copies:
        cp.wait()


def _fwd_ffn(x2, tgt, w_gu, w_down, g_ffn, g_final, T):
    TB = min(256, T)
    nb = T // TB

    def body(x_ref, t_ref, wgu_ref, wd_ref, gf_ref, gl_ref, hf_ref, gu_ref, act_ref, dx3_ref, loss_ref, dgl_ref,
             wg, wu, sems):
        i = pl.program_id(0)

        @pl.when(i == 0)
        def _():
            _load_gate_up(wgu_ref, wg, wu, sems)
            loss_ref[...] = jnp.zeros_like(loss_ref)
            dgl_ref[...] = jnp.zeros_like(dgl_ref)

        x2v = x_ref[...]
        _, xh = _rms(x2v)
        hf = (xh * gf_ref[...]).astype(BF16)
        hf_ref[...] = hf
        g = _dot(hf, wg[...])
        u = _dot(hf, wu[...])
        gu_ref[:, 0:D_FF] = g
        gu_ref[:, D_FF:2 * D_FF] = u
        act = (g * _sig(g) * u).astype(BF16)
        act_ref[...] = act
        x3 = x2v + _dot(act, wd_ref[...])
        r3, xh3 = _rms(x3)
        gl = gl_ref[...]
        e = xh3 * gl - t_ref[...]
        loss_ref[...] += jnp.sum(e * e) * (0.5 / D)
        dy = e * (1.0 / D)
        dx3, dgl = _rms_bwd(dy, xh3, r3, gl)
        dx3_ref[...] = dx3
        dgl_ref[...] += jnp.sum(dgl, axis=0, keepdims=True)

    tok = lambda w: pl.BlockSpec((TB, w), lambda i: (i, 0))
    return pl.pallas_call(
        body, name="fwd_ffn", grid=(nb,),
        in_specs=[tok(D), tok(D), ANY, _resident(w_down), _resident(g_ffn), _resident(g_final)],
        out_specs=[tok(D), tok(2 * D_FF), tok(D_FF), tok(D), _acc_spec((1, LANES)), _acc_spec((1, D))],
        out_shape=[_sds((T, D), BF16), _sds((T, 2 * D_FF), F32), _sds((T, D_FF), BF16), _sds((T, D), F32),
                   _sds((1, LANES), F32), _sds((1, D), F32)],
        scratch_shapes=[pltpu.VMEM((D, D_FF), BF16), pltpu.VMEM((D, D_FF), BF16), pltpu.SemaphoreType.DMA((4,))],
        compiler_params=_params(1),
    )(x2, tgt, w_gu, w_down, g_ffn, g_final)


def _bwd_ffn(dx3, gu, x2, w_gu, w_down, g_ffn, T):
    TB = min(256, T)
    nb = T // TB

    def body(d_ref, gu_ref, x_ref, wgu_ref, wd_ref, gf_ref, dgu_ref, dx2_ref, dgf_ref, wg, wu, sems):
        i = pl.program_id(0)

        @pl.when(i == 0)
        def _():
            _load_gate_up(wgu_ref, wg, wu, sems)
            dgf_ref[...] = jnp.zeros_like(dgf_ref)

        dx3v = d_ref[...]
        db = dx3v.astype(BF16)
        dact = _dot_nt(db, wd_ref[...])
        g = gu_ref[:, 0:D_FF]
        u = gu_ref[:, D_FF:2 * D_FF]
        sg = _sig(g)
        dg = (dact * u * sg * (1.0 + g * (1.0 - sg))).astype(BF16)
        du = (dact * g * sg).astype(BF16)
        dgu_ref[:, 0:D_FF] = dg
        dgu_ref[:, D_FF:2 * D_FF] = du
        dhf = _dot_nt(dg, wg[...]) + _dot_nt(du, wu[...])
        r2, xh2 = _rms(x_ref[...])
        dx, dg_tok = _rms_bwd(dhf, xh2, r2, gf_ref[...])
        dx2_ref[...] = dx3v + dx
        dgf_ref[...] += jnp.sum(dg_tok, axis=0, keepdims=True)

    tok = lambda w: pl.BlockSpec((TB, w), lambda i: (i, 0))
    return pl.pallas_call(
        body, name="bwd_ffn", grid=(nb,),
        in_specs=[tok(D), tok(2 * D_FF), tok(D), ANY, _resident(w_down), _resident(g_ffn)],
        out_specs=[tok(2 * D_FF), tok(D), _acc_spec((1, D))],
        out_shape=[_sds((T, 2 * D_FF), BF16), _sds((T, D), F32), _sds((1, D), F32)],
        scratch_shapes=[pltpu.VMEM((D, D_FF), BF16), pltpu.VMEM((D, D_FF), BF16), pltpu.SemaphoreType.DMA((4,))],
        compiler_params=_params(1),
    )(dx3, gu, x2, w_gu, w_down, g_ffn)


def _bwd_mid(dx2, x1, qm, km, vm, o, w_mo, w_mq, w_out, g_x, B, S, comm=None):
    T = B * S
    TB = min(512, S)
    nb = S // TB
    FOX_T = min(256, S)
    inv = 1.0 / math.sqrt(MEM_HD)

    def body(d_ref, x1_ref, qm_ref, km_ref, vm_ref, o_ref, wm_ref, wq_ref, wo_ref, g_ref,
             dx1_ref, dqm_ref, dco_ref, do_ref, dd_ref, dkm_ref, dvm_ref, dgx_ref, dot_ref):
        b = pl.program_id(0)
        j = pl.program_id(1)

        @pl.when((b == 0) & (j == 0))
        def _():
            dgx_ref[...] = jnp.zeros_like(dgx_ref)

        @pl.when(j == 0)
        def _():
            dkm_ref[...] = jnp.zeros_like(dkm_ref)
            dvm_ref[...] = jnp.zeros_like(dvm_ref)

        dx2v = d_ref[...]
        dom = _dot_nt(dx2v.astype(BF16), wm_ref[...]).astype(BF16)
        qmv = qm_ref[...]
        kmv = km_ref[...]
        vmv = vm_ref[...]
        ps = _mem_probs(qmv, kmv)
        for h in range(MEM_HEADS):
            hs = slice(h * MEM_HD, (h + 1) * MEM_HD)
            p = ps[h]
            dp = _dot_nt(dom[:, hs], vmv[:, hs])
            ds = (p * (dp - jnp.sum(p * dp, axis=-1, keepdims=True))).astype(BF16)
            dqm_ref[:, hs] = (_dot(ds, kmv[:, hs]) * inv).astype(BF16)
            dkm_ref[:, hs] += _dot_tn(ds, qmv[:, hs]) * inv
            dvm_ref[:, hs] += _dot_tn(p.astype(BF16), dom[:, hs])
        dhx = _dot_nt(dqm_ref[...], wq_ref[...])
        r1, xh1 = _rms(x1_ref[...])
        dx, dg_tok = _rms_bwd(dhx, xh1, r1, g_ref[...])
        dx1 = dx2v + dx
        dx1_ref[...] = dx1
        dgx_ref[...] += jnp.sum(dg_tok, axis=0, keepdims=True)
        d1b = dx1.astype(BF16)
        dco_ref[...] = _dot_nt(d1b, wo_ref[0:CONV_CH, :])
        do = _dot_nt(d1b, wo_ref[CONV_CH:D, :])
        dob = do.astype(BF16)
        do_ref[...] = dob
        for t in range(TB // FOX_T):
            dot_ref[0, t] = do[t * FOX_T:(t + 1) * FOX_T, :].T.astype(BF16)
        dd_ref[...] = _dot_01(dob.astype(F32) * o_ref[...], _head_sum(FOX_W))

    tok = lambda w: pl.BlockSpec((TB, w), lambda b, j: (b * nb + j, 0))
    memb = pl.BlockSpec((MEM_LEN, D), lambda b, j: (b, 0))
    outs = [(D, F32), (D, BF16), (CONV_CH, F32), (FOX_W, BF16), (FOX_W, F32)]
    return _call(
        body, comm, name="bwd_mid", grid=(B, nb),
        in_specs=[tok(D), tok(D), tok(D), memb, memb, tok(FOX_W), _resident(w_mo), _resident(w_mq), _resident(w_out),
                  _resident(g_x)],
        out_specs=[tok(w) for w, _ in outs] + [memb, memb, _acc_spec((1, D)), _feat_major_spec(TB, FOX_T, nb)],
        out_shape=[_sds((T, w), dt) for w, dt in outs] + [_sds((B * MEM_LEN, D), F32)] * 2 + [_sds((1, D), F32)]
        + [_sds((B, S // FOX_T, FOX_W, FOX_T), BF16)],
        scratch_shapes=[],
        args=(dx2, x1, qm, km, vm, o, w_mo, w_mq, w_out, g_x))


def _mem_bwd(dkm, dvm, mem2, w_mkv, g_mem, B):
    def body(dk_ref, dv_ref, m_ref, w_ref, g_ref, dkv_ref, dg_ref):
        b = pl.program_id(0)

        @pl.when(b == 0)
        def _():
            dg_ref[...] = jnp.zeros_like(dg_ref)

        dk = dk_ref[...].astype(BF16)
        dv = dv_ref[...].astype(BF16)
        dkv_ref[:, 0:D] = dk
        dkv_ref[:, D:2 * D] = dv
        dmn = jnp.zeros((MEM_LEN, D), F32)
        for s in range(2):
            dmn = dmn + _dot_nt(dk[:, 512 * s:512 * (s + 1)], w_ref[s]) + _dot_nt(dv[:, 512 * s:512 * (s + 1)], w_ref[2 + s])
        _, xh = _rms(m_ref[...])
        dg_ref[...] += jnp.sum(dmn * xh, axis=0, keepdims=True)

    blk = pl.BlockSpec((MEM_LEN, D), lambda b: (b, 0))
    return pl.pallas_call(
        body, name="mem_bwd", grid=(B,),
        in_specs=[blk, blk, blk, _resident(w_mkv), _resident(g_mem)],
        out_specs=[pl.BlockSpec((MEM_LEN, 2 * D), lambda b: (b, 0)), _acc_spec((1, D))],
        out_shape=[_sds((B * MEM_LEN, 2 * D), BF16), _sds((1, D), F32)],
        compiler_params=_params(1),
    )(dkm, dvm, mem2, w_mkv, g_mem)


def _fox_bwd(q, k, v, do, bias, dd, ckT, qT, doT, B, S, comm=None):
    T = B * S
    TK = min(256, S)
    nk = S // TK
    scale = 1.0 / math.sqrt(HEAD_D)

    def body(q_ref, k_ref, v_ref, do_ref, bias_ref, dd_ref, ck_ref, qt_ref, dot_ref, dq_ref, dk_ref, dv_ref, dc_ref,
             dcq_ref, dq_acc, rs_acc, s_scr, dp_scr, s_odd, dp_odd, dk_acc, dv_acc, dc_acc):
        j = pl.program_id(2)

        @pl.when(j == 0)
        def _():
            dq_acc[...] = jnp.zeros_like(dq_acc)
            rs_acc[...] = jnp.zeros_like(rs_acc)

        dk_acc[...] = jnp.zeros_like(dk_acc)
        dv_acc[...] = jnp.zeros_like(dv_acc)
        dc_acc[...] = jnp.zeros_like(dc_acc)
        lane = lax.broadcasted_iota(jnp.int32, (TK, LANES), 1)
        lo = lane < HEAD_D
        ks = k_ref[...] * jnp.asarray(scale, BF16)
        v2 = v_ref[...]
        zero = jnp.zeros_like(ks)
        kh = (jnp.where(lo, ks, zero), jnp.where(lo, zero, ks))
        vh = (jnp.where(lo, v2, zero), jnp.where(lo, zero, v2))
        kstart = pl.multiple_of(j * TK, TK)
        ckh = tuple(ck_ref[0, 0, h:h + 1, pl.ds(kstart, TK)] for h in range(2))
        row = lax.broadcasted_iota(jnp.int32, (TK, TK), 0)
        col = lax.broadcasted_iota(jnp.int32, (TK, TK), 1)
        wide = lambda x: jnp.concatenate([x, x], axis=1) if TK == 2 * LANES else jnp.tile(x, (1, TK // LANES))

        def scores(i, s_buf, dp_buf):
            start = pl.multiple_of(i * TK, TK)
            qi = q_ref[pl.ds(start, TK), :]
            doi = do_ref[pl.ds(start, TK), :]
            for h in range(2):
                s_buf[h] = _dot_nt(qi, kh[h])
                dp_buf[h] = _dot_nt(doi, vh[h])

        def grads(i, s_buf, dp_buf, diagonal):
            start = pl.multiple_of(i * TK, TK)
            bias2 = bias_ref[pl.ds(start, TK), :]
            dd2 = dd_ref[pl.ds(start, TK), :]
            for h in range(2):
                hc = slice(h * HEAD_D, h * HEAD_D + 1)
                bias = jnp.broadcast_to(bias2[:, hc], (TK, LANES))
                ddh = jnp.broadcast_to(dd2[:, hc], (TK, LANES))
                p = jnp.exp((s_buf[h] - ckh[h]) + wide(bias))
                if diagonal:
                    p = jnp.where(col <= row, p, 0.0)
                ds = p * (dp_buf[h] - wide(ddh))
                dc_acc[h, 0:1, :] += jnp.sum(ds, axis=0, keepdims=True)
                rs_acc[h, pl.ds(start, TK), :] += jnp.sum(ds, axis=1, keepdims=True)
                pb = p.astype(BF16)
                dsb = ds.astype(BF16)
                feat = slice(h * HEAD_D, (h + 1) * HEAD_D)
                dv_acc[feat, :] += _dot(dot_ref[0, i, feat, :], pb)
                dk_acc[feat, :] += _dot(qt_ref[0, i, feat, :], dsb)
                dq_acc[pl.ds(start, TK), :] += _dot(dsb, kh[h])

        n_off = nk - 1 - j
        block = lambda t: jnp.where(t < n_off, j + 1 + t, j)

        def two_blocks(tt, carry):
            t = 2 * tt
            scores(block(t + 1), s_odd, dp_odd)
            grads(block(t), s_scr, dp_scr, False)
            scores(block(t + 2), s_scr, dp_scr)
            grads(block(t + 1), s_odd, dp_odd, False)
            return carry

        scores(block(0), s_scr, dp_scr)
        lax.fori_loop(0, n_off // 2, two_blocks, 0)

        @pl.when(n_off % 2 == 0)
        def _():
            grads(j, s_scr, dp_scr, True)

        @pl.when(n_off % 2 == 1)
        def _():
            scores(j, s_odd, dp_odd)
            grads(nk - 1, s_scr, dp_scr, False)
            grads(j, s_odd, dp_odd, True)

        dk_ref[...] = (dk_acc[...].T * scale).astype(BF16)
        dv_ref[...] = dv_acc[...].T.astype(BF16)
        sub = lax.broadcasted_iota(jnp.int32, (8, TK), 0)
        dca = dc_acc[0, 0:1, :]
        dcb = dc_acc[1, 0:1, :]
        dc_ref[0, 0] = jnp.where(sub == 0, -dca, jnp.where(sub == 1, -dcb, 0.0))

        @pl.when(j == nk - 1)
        def _():
            dq_ref[...] = dq_acc[...].astype(BF16)
            lo_s = lax.broadcasted_iota(jnp.int32, (S, LANES), 1) < HEAD_D
            dcq_ref[...] = jnp.where(lo_s, rs_acc[0], rs_acc[1])

    full = pl.BlockSpec((S, LANES), lambda b, p, j: (b, p))
    blk = pl.BlockSpec((TK, LANES), lambda b, p, j: (b * nk + j, p))
    featT = pl.BlockSpec((1, nk, LANES, TK), lambda b, p, j: (b, 0, p, 0))
    return _call(
        body, comm, name="fox_bwd", grid=(B, N_PAIR, nk),
        in_specs=[full, blk, blk, full, full, full, pl.BlockSpec((1, 1, 8, S), lambda b, p, j: (b, p, 0, 0)),
                  featT, featT],
        out_specs=[full, blk, blk, pl.BlockSpec((1, 1, 8, TK), lambda b, p, j: (b, p, 0, j)), full],
        out_shape=[_sds((T, FOX_W), BF16), _sds((T, FOX_W), BF16), _sds((T, FOX_W), BF16),
                   _sds((B, N_PAIR, 8, S), F32), _sds((T, FOX_W), F32)],
        scratch_shapes=[pltpu.VMEM((S, LANES), F32), pltpu.VMEM((2, S, 1), F32),
                        pltpu.VMEM((2, TK, TK), F32), pltpu.VMEM((2, TK, TK), F32),
                        pltpu.VMEM((2, TK, TK), F32), pltpu.VMEM((2, TK, TK), F32),
                        pltpu.VMEM((LANES, TK), F32), pltpu.VMEM((LANES, TK), F32), pltpu.VMEM((2, 8, TK), F32)],
        args=(q, k, v, do, bias, dd, ckT, qT, doT))


def _fgate_bwd(dc8, zf, B, S):
    T = B * S
    TB = min(512, S)
    nb = S // TB

    def body(dc_ref, zf_ref, dzf_ref, dbf_ref, carry):
        b = pl.program_id(0)
        j = pl.program_id(1)

        @pl.when((b == 0) & (j == 0))
        def _():
            dbf_ref[...] = jnp.zeros_like(dbf_ref)

        @pl.when(j == 0)
        def _():
            carry[...] = jnp.zeros_like(carry)

        dc = dc_ref[...]
        row = lax.broadcasted_iota(jnp.int32, (TB, TB), 0)
        col = lax.broadcasted_iota(jnp.int32, (TB, TB), 1)
        dlogf = _dot_01(col >= row, dc) + carry[0:1, :]
        carry[0:1, :] = dlogf[0:1, :]
        lane = lax.broadcasted_iota(jnp.int32, dc.shape, 1)
        dzf = jnp.where(lane < 8, dlogf * _sig(-zf_ref[...]), 0.0)
        dzf_ref[...] = dzf.astype(BF16)
        dbf_ref[...] += jnp.sum(dzf, axis=0, keepdims=True)

    tok = pl.BlockSpec((TB, LANES), lambda b, j: (b * nb + (nb - 1 - j), 0))
    return pl.pallas_call(
        body, name="fgate_bwd", grid=(B, nb),
        in_specs=[tok, tok],
        out_specs=[tok, _acc_spec((1, LANES))],
        out_shape=[_sds((T, LANES), BF16), _sds((1, LANES), F32)],
        scratch_shapes=[pltpu.VMEM((8, LANES), F32)],
        compiler_params=_params(2),
    )(dc8, zf)


def _conv_bwd(dco, y, u, gt, cw, lng, lnb, B, S, comm=None):
    T = B * S
    CB = min(256, S)
    nb = S // CB
    hb = CB // CONV_HALO

    def body(dco_ref, y_ref, u_ref, gt_ref, up_ref, gp_ref, w_ref, lg_ref, lb_ref,
             du_ref, dgt_ref, dw_ref, vec_ref, acat, dycat, ash, dysh):
        b = pl.program_id(0)
        j = pl.program_id(1)
        jr = nb - 1 - j

        @pl.when((b == 0) & (j == 0))
        def _():
            dw_ref[...] = jnp.zeros_like(dw_ref)
            vec_ref[...] = jnp.zeros_like(vec_ref)

        @pl.when(j == 0)
        def _():
            dycat[CB:CB + CONV_HALO, :] = jnp.zeros((CONV_HALO, CONV_CH), F32)

        lg = lg_ref[...]
        rs, n, l = _layernorm_silu(y_ref[...], lg, lb_ref[...])
        sg = _sig(l)
        dl = dco_ref[...] * (sg * (1.0 + l * (1.0 - sg)))
        dn = dl * lg
        dy = rs * (dn - jnp.mean(dn, axis=-1, keepdims=True) - n * jnp.mean(dn * n, axis=-1, keepdims=True))
        vec_ref[0:1, :] += jnp.sum(dy, axis=0, keepdims=True)
        vec_ref[1:2, :] += jnp.sum(dl * n, axis=0, keepdims=True)
        vec_ref[2:3, :] += jnp.sum(dl, axis=0, keepdims=True)
        dycat[0:CB, :] = dy
        acat[0:CONV_HALO, :] = jnp.where(jr > 0, up_ref[...] * _sig(gp_ref[...]), 0.0)
        acat[CONV_HALO:CONV_HALO + CB, :] = u_ref[...] * _sig(gt_ref[...])
        _shifted_copies(acat, ash, CB + CONV_HALO - SUB)
        _shifted_copies(dycat, dysh, CB + CONV_HALO - SUB)
        for r0, rows, cs in _conv_pieces(CB):
            dyp = dycat[r0:r0 + rows, cs]
            da = jnp.zeros((rows, LANES), F32)
            for k in range(CONV_K):
                da = da + w_ref[k:k + 1, cs] * _tap(dycat, dysh, r0 + CONV_K - 1 - k, rows, cs)
                dw_ref[k:k + 1, cs] += jnp.sum(dyp * _tap(acat, ash, r0 + CONV_HALO - (CONV_K - 1) + k, rows, cs),
                                               axis=0, keepdims=True)
            uv = u_ref[r0:r0 + rows, cs]
            sgt = _sig(gt_ref[r0:r0 + rows, cs])
            du_ref[r0:r0 + rows, cs] = (da * sgt).astype(BF16)
            dgt_ref[r0:r0 + rows, cs] = (da * uv * sgt * (1.0 - sgt)).astype(BF16)
        dycat[CB:CB + CONV_HALO, :] = dycat[0:CONV_HALO, :]

    tok = lambda w: pl.BlockSpec((CB, w), lambda b, j: (b * nb + (nb - 1 - j), 0))
    prev = pl.BlockSpec((CONV_HALO, CONV_CH), lambda b, j: (jnp.maximum((b * nb + (nb - 1 - j)) * hb - 1, 0), 0))
    return _call(
        body, comm, name="conv_bwd", grid=(B, nb),
        in_specs=[tok(CONV_CH), tok(CONV_CH), tok(CONV_CH), tok(CONV_CH), prev, prev, _resident(cw), _resident(lng),
                  _resident(lnb)],
        out_specs=[tok(CONV_CH), tok(CONV_CH), _acc_spec((CONV_HALO, CONV_CH)), _acc_spec((8, CONV_CH))],
        out_shape=[_sds((T, CONV_CH), BF16), _sds((T, CONV_CH), BF16), _sds((CONV_HALO, CONV_CH), F32),
                   _sds((8, CONV_CH), F32)],
        scratch_shapes=[pltpu.VMEM((CONV_HALO + CB, CONV_CH), F32), pltpu.VMEM((CB + CONV_HALO, CONV_CH), F32),
                        pltpu.VMEM((SUB, CB + CONV_HALO - SUB, CONV_CH), F32),
                        pltpu.VMEM((SUB, CB + CONV_HALO - SUB, CONV_CH), F32)],
        args=(dco, y, u, gt, u, gt, cw, lng, lnb))


def _bwd_in(dz, w_int, w_ft, x2, dx1, g_mix, T, comm=None):
    TB = min(512, T)
    nb = T // TB

    def body(dz_ref, w_ref, wf_ref, x_ref, d1_ref, g_ref, gx_ref, dg_ref):
        i = pl.program_id(0)

        @pl.when(i == 0)
        def _():
            dg_ref[...] = jnp.zeros_like(dg_ref)

        dh = _dot(dz_ref[:, 0:OFF_F], w_ref[0:OFF_F, :]) + _dot(dz_ref[:, OFF_F:D_IN_PAD], wf_ref[...])
        r0, xh0 = _rms(x_ref[...])
        dx, dg_tok = _rms_bwd(dh, xh0, r0, g_ref[...])
        gx_ref[...] = d1_ref[...] + dx
        dg_ref[...] += jnp.sum(dg_tok, axis=0, keepdims=True)

    tok = lambda w: pl.BlockSpec((TB, w), lambda i: (i, 0))
    return _call(
        body, comm, name="bwd_in", grid=(nb,),
        in_specs=[tok(D_IN_PAD), _resident(w_int), _resident(w_ft), tok(D), tok(D), _resident(g_mix)],
        out_specs=[tok(D), _acc_spec((1, D))],
        out_shape=[_sds((T, D), F32), _sds((1, D), F32)],
        scratch_shapes=[],
        args=(dz, w_int, w_ft, x2, dx1, g_mix))


def _dw(a, b, name, tn, slabs=0, tk=None, rows=None):
    T, K = a.shape
    N = b.shape[1]
    per = tn // slabs if slabs else 1
    tk = tk or (K if K <= 1024 else K // 2)
    tt = min(1024, T)
    nt = T // tt

    def body(a_ref, b_ref, o_ref, acc):
        t = pl.program_id(2)

        @pl.when(t == 0)
        def _():
            acc[...] = jnp.zeros_like(acc)

        acc[...] += _dot_tn(a_ref[...].astype(BF16), b_ref[...].astype(BF16))

        @pl.when(t == nt - 1)
        def _():
            if slabs:
                for sl in range(per):
                    o_ref[sl] = acc[:, sl * slabs:(sl + 1) * slabs]
            else:
                o_ref[...] = acc[...]

    return pl.pallas_call(
        body, name=name, grid=(K // tk, N // tn, nt),
        in_specs=[pl.BlockSpec((tt, tk), lambda i, j, t: (t, i)), pl.BlockSpec((tt, tn), lambda i, j, t: (t, j))],
        out_specs=(pl.BlockSpec((per, tk, slabs), lambda i, j, t: (j, i, 0)) if slabs
                   else pl.BlockSpec((tk, tn), lambda i, j, t: (i, j))),
        out_shape=_sds((N // slabs, K, slabs) if slabs else (rows or K, N), F32),
        scratch_shapes=[pltpu.VMEM((tk, tn), F32)],
        compiler_params=_params(3),
    )(a, b)


def _pos():
    return lax.axis_index("x"), lax.axis_index("y"), lax.axis_index("c")


def _remote(src, dst, ssem, rsem, to):
    return pltpu.make_async_remote_copy(src_ref=src, dst_ref=dst, send_sem=ssem, recv_sem=rsem, device_id=to,
                                        device_id_type=MESH)


def _split_axis(shape):
    return 0 if shape[0] % 32 == 0 else 1


def _half_shape(shape, parts=2):
    return (shape[0] // parts, shape[1]) if _split_axis(shape) == 0 else (shape[0], shape[1] // parts)


def _half(shape, c):
    R, C = shape
    if _split_axis(shape) == 0:
        return (pl.ds(pl.multiple_of(c * (R // 2), 16), R // 2), slice(None))
    return (slice(None), pl.ds(pl.multiple_of(c * (C // 2), LANES), C // 2))


def _half_block(shape, parts, lead, which):
    blk = _half_shape(shape, parts)
    idx = (which, 0) if _split_axis(shape) == 0 else (0, which)
    return blk, tuple(lead) + idx


class _Comm:
    def __init__(self, ins, out_shapes, sems, start, finish):
        self.ins, self.out_shapes, self.sems, self.start, self.finish = list(ins), list(out_shapes), list(sems), start, finish


def _ag_comm(shards):
    n = len(shards)

    def parts(ins, outs, sems):
        send_sems, recv_sems, local_sems = sems
        x, y, c = _pos()
        me, sib = (x, y, c), (x, y, 1 - c)
        chips = [(1 - x, y), (x, 1 - y), (1 - x, 1 - y)]

        def rows(w, px, py, pc):
            return outs[w].at[(2 * px + py,) + _half(shards[w].shape, pc)]

        def copy(w, k, block, to, src=None):
            return _remote(rows(w, *block) if src is None else src, rows(w, *block), send_sems.at[w, k],
                           recv_sems.at[w, k], to)

        mine, first = [], []
        for w in range(n):
            src = ins[w].at[_half(shards[w].shape, c)]
            mine.append(pltpu.make_async_copy(src, rows(w, *me), local_sems.at[w]))
            first += [copy(w, 0, me, sib, src=src)] + [copy(w, 1 + j, me, (*chip, c), src=src) for j, chip in enumerate(chips)]
        return c, me, sib, chips, copy, mine, first

    def start(ins, outs, sems):
        _, _, _, _, _, mine, first = parts(ins, outs, sems)
        for cp in mine + first:
            cp.start()

    def finish(ins, outs, sems):
        c, me, sib, chips, copy, mine, first = parts(ins, outs, sems)
        passed = []
        for w in range(n):
            for j, chip in enumerate(chips):
                copy(w, 1 + j, (*chip, c), me).wait_recv()
                passed.append(copy(w, 4 + j, (*chip, c), sib))
                passed[-1].start()
        for w in range(n):
            copy(w, 0, sib, me).wait_recv()
            for j, chip in enumerate(chips):
                copy(w, 4 + j, (*chip, 1 - c), me).wait_recv()
        for cp in first + passed:
            cp.wait_send()
        for cp in mine:
            cp.wait()

    D7 = pltpu.SemaphoreType.DMA((n, 7))
    return _Comm(shards, [_sds((4,) + s.shape, s.dtype) for s in shards], [D7, D7, pltpu.SemaphoreType.DMA((n,))],
                 start, finish)


def _sibling_comm(gs):
    n = len(gs)

    def copies(ins, outs, sems):
        send_sems, recv_sems = sems
        x, y, c = _pos()
        return [_remote(ins[w].at[(s,) + _half(gs[w].shape[1:], 1 - c)], outs[w].at[s], send_sems.at[w, s],
                        recv_sems.at[w, s], (x, y, 1 - c)) for w in range(n) for s in range(4)]

    def start(ins, outs, sems):
        for cp in copies(ins, outs, sems):
            cp.start()

    def finish(ins, outs, sems):
        for cp in copies(ins, outs, sems):
            cp.wait()

    D4 = pltpu.SemaphoreType.DMA((n, 4))
    return _Comm(gs, [_sds((4,) + _half_shape(g.shape[1:]), F32) for g in gs], [D4, D4], start, finish)


def _ici_comm(pbs):
    n = len(pbs)

    def copies(ins, outs, sems):
        send_sems, recv_sems = sems
        x, y, c = _pos()
        return [_remote(ins[w].at[2 * tx + ty], outs[w].at[j], send_sems.at[w, j], recv_sems.at[w, j], (tx, ty, c))
                for w in range(n) for j, (tx, ty) in enumerate([(1 - x, y), (x, 1 - y), (1 - x, 1 - y)])]

    def start(ins, outs, sems):
        for cp in copies(ins, outs, sems):
            cp.start()

    def finish(ins, outs, sems):
        for cp in copies(ins, outs, sems):
            cp.wait()

    D3 = pltpu.SemaphoreType.DMA((n, 3))
    return _Comm(pbs, [_sds((3,) + p.shape[1:], BF16) for p in pbs], [D3, D3], start, finish)


def _join(*comms):
    counts = [(len(c.ins), len(c.out_shapes), len(c.sems)) for c in comms]

    def each(which):
        def run(ins, outs, sems):
            i = o = k = 0
            for c, (ni, no, nk) in zip(comms, counts):
                getattr(c, which)(ins[i:i + ni], outs[o:o + no], sems[k:k + nk])
                i, o, k = i + ni, o + no, k + nk
        return run

    return _Comm(sum((c.ins for c in comms), []), sum((c.out_shapes for c in comms), []),
                 sum((c.sems for c in comms), []), each("start"), each("finish"))


def _run_comm(comm, name):
    ni, no = len(comm.ins), len(comm.out_shapes)

    def body(*refs):
        ins, outs, sems = refs[:ni], refs[ni:ni + no], refs[ni + no:]
        comm.start(ins, outs, sems)
        comm.finish(ins, outs, sems)

    return pl.pallas_call(body, name=name, out_shape=comm.out_shapes, in_specs=[ANY] * ni, out_specs=[ANY] * no,
                          scratch_shapes=comm.sems)(*comm.ins)


def _call(body, comm, *, name, grid, in_specs, out_specs, out_shape, scratch_shapes, args):
    n_grid = len(grid)
    if comm is None:
        res = pl.pallas_call(body, name=name, grid=grid, in_specs=in_specs, out_specs=out_specs, out_shape=out_shape,
                             scratch_shapes=scratch_shapes, compiler_params=_params(n_grid))(*args)
        return list(res), []
    n_in, n_out, n_scr = len(in_specs), len(out_specs), len(scratch_shapes)
    ni, no = len(comm.ins), len(comm.out_shapes)

    def carried(*refs):
        ins, refs = refs[:n_in], refs[n_in:]
        cins, refs = refs[:ni], refs[ni:]
        outs, refs = refs[:n_out], refs[n_out:]
        couts, refs = refs[:no], refs[no:]
        scr, csems = refs[:n_scr], refs[n_scr:]
        ids = [pl.program_id(ax) for ax in range(n_grid)]
        first = functools.reduce(jnp.logical_and, [i == 0 for i in ids])
        last = functools.reduce(jnp.logical_and, [i == g - 1 for i, g in zip(ids, grid)])

        @pl.when(first)
        def _():
            comm.start(cins, couts, csems)

        body(*ins, *outs, *scr)

        @pl.when(last)
        def _():
            comm.finish(cins, couts, csems)

    res = pl.pallas_call(
        carried, name=name, grid=grid, in_specs=list(in_specs) + [ANY] * ni, out_specs=list(out_specs) + [ANY] * no,
        out_shape=list(out_shape) + comm.out_shapes, scratch_shapes=list(scratch_shapes) + comm.sems,
        compiler_params=_params(n_grid))(*args, *comm.ins)
    return list(res[:n_out]), list(res[n_out:])


def _small_allreduce(v, name, halves=()):
    P = v.shape[0]
    n = len(halves)
    vm = pl.BlockSpec(memory_space=pltpu.VMEM)

    def body(v_ref, *refs):
        o_ref, outs = refs[n], refs[n + 1:2 * n + 1]
        gath, send_sems, recv_sems, half_send, half_recv = refs[2 * n + 1:]
        x, y, c = _pos()
        me = 4 * x + 2 * y + c
        gath[me] = v_ref[...]
        cps = []
        for r in range(1, 8):
            tx = (1 - x) if r & 4 else x
            ty = (1 - y) if r & 2 else y
            tc = (1 - c) if r & 1 else c
            cps.append(_remote(v_ref, gath.at[me], send_sems.at[r - 1], recv_sems.at[r - 1], (tx, ty, tc)))
        for w in range(n):
            mine = outs[w].at[_half(halves[w].shape, c)]
            cps.append(_remote(mine, mine, half_send.at[w], half_recv.at[w], (x, y, 1 - c)))
        for cp in cps:
            cp.start()
        for cp in cps:
            cp.wait()
        acc = gath[0]
        for d in range(1, 8):
            acc = acc + gath[d]
        o_ref[...] = acc

    res = pl.pallas_call(
        body, name=name, out_shape=[_sds((P, LANES), F32)] + [_sds(g.shape, F32) for g in halves],
        in_specs=[vm] + [ANY] * n, out_specs=[vm] + [ANY] * n, input_output_aliases={1 + w: 1 + w for w in range(n)},
        scratch_shapes=[pltpu.VMEM((8, P, LANES), F32), pltpu.SemaphoreType.DMA((7,)), pltpu.SemaphoreType.DMA((7,)),
                        pltpu.SemaphoreType.DMA((max(n, 1),)), pltpu.SemaphoreType.DMA((max(n, 1),))],
    )(v, *halves)
    return res[0], list(res[1:])


def _chip_sum(g, rcv, pos, name):
    shard = g.shape[1:]
    hs = _half_shape(shard)
    other = lambda i, pos: (pos[1] + 1 + i) % 4

    def body(pos_ref, g_ref, r_ref, o_ref):
        o_ref[...] = (g_ref[...] + r_ref[...]).astype(BF16)

    return pl.pallas_call(
        body, name=name, out_shape=_sds((4,) + hs, BF16),
        grid_spec=pltpu.PrefetchScalarGridSpec(
            num_scalar_prefetch=1, grid=(3,),
            in_specs=[pl.BlockSpec((1,) + hs, lambda i, pos: _half_block(shard, 2, (other(i, pos),), pos[0])[1]),
                      pl.BlockSpec((1,) + hs, lambda i, pos: (other(i, pos), 0, 0))],
            out_specs=pl.BlockSpec((1,) + hs, lambda i, pos: (other(i, pos), 0, 0))),
        compiler_params=_params(1),
    )(pos, g, rcv)


def _final_sum(g, rcv, rc, pos, name):
    shard = g.shape[1:]
    qs = _half_shape(shard, 4)

    def body(pos_ref, g_ref, r_ref, rc_ref, o_ref):
        acc = g_ref[0] + r_ref[0]
        for j in range(3):
            acc = acc + rc_ref[j].astype(F32)
        o_ref[...] = acc

    return pl.pallas_call(
        body, name=name, out_shape=_sds(shard, F32),
        grid_spec=pltpu.PrefetchScalarGridSpec(
            num_scalar_prefetch=1, grid=(2,),
            in_specs=[pl.BlockSpec((1,) + qs, lambda i, pos: _half_block(shard, 4, (pos[1],), pos[0] * 2 + i)[1]),
                      pl.BlockSpec((1,) + qs, lambda i, pos: _half_block(shard, 4, (pos[1],), i)[1]),
                      pl.BlockSpec((3,) + qs, lambda i, pos: _half_block(shard, 4, (0,), i)[1])],
            out_specs=pl.BlockSpec(qs, lambda i, pos: _half_block(shard, 4, (), pos[0] * 2 + i)[1])),
        compiler_params=_params(1),
    )(pos, g, rcv, rc)


def _adamw_math(w, g, m, v):
    m = ADAM_B1 * m + (1.0 - ADAM_B1) * g
    v = ADAM_B2 * v + (1.0 - ADAM_B2) * (g * g)
    m_hat = m / (1.0 - ADAM_B1 ** ADAM_STEP)
    v_hat = v / (1.0 - ADAM_B2 ** ADAM_STEP)
    delta = -ADAM_LR * (m_hat / (jnp.sqrt(v_hat) + ADAM_EPS) + ADAM_WD * w)
    return delta, m, v


def _adamw(w, g, m, v, name, blk_shape):
    R, C = w.shape

    def body(w_ref, g_ref, m_ref, v_ref, go_ref, d_ref, nm_ref, nv_ref):
        g = g_ref[...]
        d, nm, nv = _adamw_math(w_ref[...], g, m_ref[...], v_ref[...])
        go_ref[...] = g
        d_ref[...] = d
        nm_ref[...] = nm
        nv_ref[...] = nv

    blk = pl.BlockSpec(blk_shape, lambda i, j: (i, j))
    return pl.pallas_call(
        body, name=name, grid=(R // blk_shape[0], C // blk_shape[1]), in_specs=[blk] * 4, out_specs=[blk] * 4,
        out_shape=[_sds((R, C), F32)] * 4, compiler_params=_params(2),
    )(w, g, m, v)


SMALL = (("g_mix", 8), ("b_f", 8), ("conv_w", None), ("conv_b", 8), ("ln_g", 8), ("ln_b", 8), ("g_x", 8), ("g_mem", 8),
         ("g_ffn", 8), ("g_final", 8), ("loss", 8))


def _pack_small(parts, conv_rows):
    rows = []
    for name, n in SMALL:
        if name not in parts:
            continue
        n = conv_rows if n is None else n
        flat = parts[name].reshape(-1).astype(F32)
        flat = jnp.pad(flat, (0, n * LANES - flat.shape[0]))
        rows.append(flat.reshape(n, LANES))
    return jnp.concatenate(rows, axis=0)


def _unpack_small(p, shapes, conv_rows):
    out, off = {}, 0
    for name, n in SMALL:
        if name not in shapes:
            continue
        n = conv_rows if n is None else n
        size = math.prod(shapes[name])
        out[name] = p[off:off + n].reshape(-1)[:size].reshape(shapes[name])
        off += n
    return out


def kernel(x, mem, g_mix, w_in, b_f, conv_w, conv_b, ln_g, ln_b, w_out, g_x, g_mem, w_mq, w_mkv, w_mo, g_ffn, w_gu, w_down, g_final, loss_target, m_g_mix, m_w_in, m_b_f, m_conv_w, m_conv_b, m_ln_g, m_ln_b, m_w_out, m_g_x, m_g_mem, m_w_mq, m_w_mkv, m_w_mo, m_g_ffn, m_w_gu, m_w_down, m_g_final, v_g_mix, v_w_in, v_b_f, v_conv_w, v_conv_b, v_ln_g, v_ln_b, v_w_out, v_g_x, v_g_mem, v_w_mq, v_w_mkv, v_w_mo, v_g_ffn, v_w_gu, v_w_down, v_g_final):
    names = ["g_mix", "w_in", "b_f", "conv_w", "conv_b", "ln_g", "ln_b", "w_out", "g_x", "g_mem", "w_mq", "w_mkv",
             "w_mo", "g_ffn", "w_gu", "w_down", "g_final"]
    W = dict(zip(names, [g_mix, w_in, b_f, conv_w, conv_b, ln_g, ln_b, w_out, g_x, g_mem, w_mq, w_mkv, w_mo, g_ffn,
                         w_gu, w_down, g_final]))
    M = dict(zip(names, [m_g_mix, m_w_in, m_b_f, m_conv_w, m_conv_b, m_ln_g, m_ln_b, m_w_out, m_g_x, m_g_mem, m_w_mq,
                         m_w_mkv, m_w_mo, m_g_ffn, m_w_gu, m_w_down, m_g_final]))
    V = dict(zip(names, [v_g_mix, v_w_in, v_b_f, v_conv_w, v_conv_b, v_ln_g, v_ln_b, v_w_out, v_g_x, v_g_mem, v_w_mq,
                         v_w_mkv, v_w_mo, v_g_ffn, v_w_gu, v_w_down, v_g_final]))
    big_names = [n for n, _, _, _ in BIG]
    B, S, _ = x.shape
    T = B * S
    mx, my, mc = _pos()
    chip = 2 * mx + my
    pos = jnp.stack([mc, chip]).astype(jnp.int32)

    shard2d = lambda a: a.reshape(a.shape[-2], a.shape[-1])
    big2d = lambda d, n: shard2d(d[n]).T if n == "w_in" else shard2d(d[n])
    shard_bf = {n: big2d(W, n).astype(BF16) for n in big_names}
    ag_mid = ["w_mkv", "w_out", "w_mq", "w_mo"]
    ag_ffn = ["w_gu", "w_down"]
    cw_mine = jnp.pad(shard2d(conv_w), ((0, 1), (0, 0)))
    row = lambda a: a.reshape(1, -1)
    x2d = x.reshape(T, D)
    (h,), (w_in_slab, cw_slab) = _norm_in(x2d, row(g_mix), T, comm=_ag_comm([shard_bf["w_in"], cw_mine]))
    slab = {"w_in": w_in_slab}
    w_int = w_in_slab.reshape(D_IN, D)
    w_ft = jnp.pad(w_int[OFF_F:D_IN], ((0, D_IN_PAD - D_IN), (0, 0)))
    cw = jnp.transpose(cw_slab, (1, 0, 2)).reshape(CONV_HALO, CONV_CH)

    bf_pad = jnp.pad(row(b_f), ((0, 0), (0, LANES - 8)))
    mem2d = mem.reshape(B * MEM_LEN, D)
    tgt = loss_target.reshape(T, D)

    (u, gt, q, k, v, zf, c, cq, qx, kx, qT), got = _fwd_in(h, w_int, w_ft, bf_pad, B, S,
                                                  comm=_ag_comm([shard_bf[n] for n in ag_mid[:2]]))
    slab.update(zip(ag_mid[:2], got))
    ckT = jnp.transpose(c.reshape(B, S, LANES)[:, :, :8], (0, 2, 1)).reshape(B, N_PAIR, 2, S)
    ckT = jnp.pad(ckT, ((0, 0), (0, 0), (0, 6), (0, 0)))
    (y, co), got = _conv_fwd(u, gt, cw, row(conv_b), row(ln_g), row(ln_b), B, S,
                             comm=_ag_comm([shard_bf[n] for n in ag_mid[2:]]))
    slab.update(zip(ag_mid[2:], got))
    (o, fox_bias), got = _fox_fwd(qx, kx, v, cq, B, S, comm=_ag_comm([shard_bf[n] for n in ag_ffn]))
    slab.update(zip(ag_ffn, got))
    full = {n: slab[n] if by_col else slab[n].reshape(4 * r, c) for n, r, c, by_col in BIG}
    mn, km, vm = _mem_kv(mem2d, row(g_mem), full["w_mkv"], B)
    (x1, hx, qm, om, x2, cat), _ = _fwd_mid(x2d, co, o, km, vm, full["w_out"], full["w_mq"], full["w_mo"], row(g_x), B, S)
    hf, gu, act, dx3, loss_p, dg_final = _fwd_ffn(x2, tgt, full["w_gu"], full["w_down"], row(g_ffn), row(g_final), T)

    pos_sum = lambda gs, rcvs, ns: [_chip_sum(g, r, pos, "rs_chip_sum_" + n) for g, r, n in zip(gs, rcvs, ns)]
    fin_sum = lambda gs, rcvs, rcs, ns: [_final_sum(g, r, q3, pos, "rs_final_sum_" + n)
                                         for g, r, q3, n in zip(gs, rcvs, rcs, ns)]
    RH = {}
    dgu, dx2, dg_ffn = _bwd_ffn(dx3, gu, x2, full["w_gu"], full["w_down"], row(g_ffn), T)
    g_ffn_w = [_dw(hf, dgu, "dw_gu", D_FF, slabs=FF_CHUNK, tk=512), _dw(act, dx3, "dw_down", 512).reshape(4, D_FF // 4, D)]
    (dx1, dqm, dco, do, dd, dkm, dvm, dg_x, doT), rcv_ffn = _bwd_mid(dx2, x1, qm, km, vm, o, full["w_mo"], full["w_mq"],
                                                                full["w_out"], row(g_x), B, S, comm=_sibling_comm(g_ffn_w))
    pb_ffn = pos_sum(g_ffn_w, rcv_ffn, ag_ffn)
    dkv, dg_mem = _mem_bwd(dkm, dvm, mem2d, full["w_mkv"], row(g_mem), B)
    g_mid_w = [_dw(mn, dkv, "dw_mkv", 512, slabs=512), _dw(cat, dx1, "dw_out", 512).reshape(4, 256, D),
               _dw(hx, dqm, "dw_mq", 512).reshape(4, 256, D), _dw(om, dx2, "dw_mo", 512).reshape(4, 256, D)]
    (dq, dk, dv, dc, dcq), got = _fox_bwd(q, k, v, do, fox_bias, dd, ckT, qT, doT, B, S,
                                          comm=_join(_ici_comm(pb_ffn), _sibling_comm(g_mid_w)))
    rc_ffn, rcv_mid = got[:len(pb_ffn)], got[len(pb_ffn):]
    RH.update(zip(ag_ffn, fin_sum(g_ffn_w, rcv_ffn, rc_ffn, ag_ffn)))
    pb_mid = pos_sum(g_mid_w, rcv_mid, ag_mid)
    dc8 = jnp.transpose(dc[:, :, :2, :].reshape(B, 8, S), (0, 2, 1)).reshape(T, 8)
    dc8 = dc8 + dcq.reshape(T, 8, HEAD_D)[:, :, 0]
    dzf, dbf = _fgate_bwd(jnp.pad(dc8, ((0, 0), (0, LANES - 8))), zf, B, S)
    (du, dgt, dcw, dvec), rc_mid = _conv_bwd(dco, y, u, gt, cw, row(ln_g), row(ln_b), B, S, comm=_ici_comm(pb_mid))
    RH.update(zip(ag_mid, fin_sum(g_mid_w, rcv_mid, rc_mid, ag_mid)))
    dz = jnp.concatenate([du, dgt, dq, dk, dv, dzf], axis=1)
    g_in_w = [_dw(dz, h, "dw_in", 512, tk=D_IN_PAD // 3, rows=D_IN).reshape(4, D_IN // 4, D)]
    rcv_in = _run_comm(_sibling_comm(g_in_w), "rs_sibling_in")
    (grad_x, dg_mix), rc_in = _bwd_in(dz, w_int, w_ft, x2d, dx1, row(g_mix), T,
                                      comm=_ici_comm(pos_sum(g_in_w, rcv_in, ["w_in"])))
    RH.update(zip(["w_in"], fin_sum(g_in_w, rcv_in, rc_in, ["w_in"])))

    small_g = {"g_mix": dg_mix, "b_f": dbf[:, :8], "conv_w": dcw, "conv_b": dvec[0], "ln_g": dvec[1], "ln_b": dvec[2],
               "g_x": dg_x, "g_mem": dg_mem, "g_ffn": dg_ffn, "g_final": dg_final, "loss": loss_p[:, :1]}
    sg, filled = _small_allreduce(_pack_small(small_g, CONV_HALO * 4), "allreduce_small", [RH[n] for n in big_names])
    shared = dict(zip(big_names, filled))
    G, DL, NM, NV = {}, {}, {}, {}
    for n in big_names:
        G[n], DL[n], NM[n], NV[n] = _adamw(big2d(W, n), shared[n], big2d(M, n), big2d(V, n), "adamw_" + n,
                                           _half_shape(shared[n].shape))
    shapes = {n: W[n].shape for n in names if n not in big_names}
    shapes["conv_w"] = (CONV_HALO, CONV_CH)
    shapes["loss"] = (1,)
    sgrads = _unpack_small(sg, shapes, CONV_HALO * 4)
    loss = sgrads.pop("loss")[0]
    sgrads["conv_w"] = lax.dynamic_slice(sgrads["conv_w"], (0, chip * LANES), (CONV_K, LANES)).reshape(W["conv_w"].shape)
    spack = lambda d: _pack_small({n: d[n] for n in sgrads}, CONV_HALO)
    _, sd, snm, snv = _adamw(spack(W), spack(sgrads), spack(M), spack(V), "adamw_small", (8, LANES))
    sshapes = {n: W[n].shape for n in sgrads}
    SD, SNM, SNV = (_unpack_small(a, sshapes, CONV_HALO) for a in (sd, snm, snv))

    def collect(bigs, smalls):
        back = lambda n: (bigs[n].T if n == "w_in" else bigs[n]).reshape(W[n].shape)
        return [back(n) if n in big_names else smalls[n] for n in names]

    return (loss, grad_x.reshape(x.shape), *collect(G, sgrads), *collect(DL, SD), *collect(NM, SNM), *collect(NV, SNV))
```

```python
import functools
import math

import jax
import jax.numpy as jnp
from jax import lax
from jax.experimental import pallas as pl
from jax.experimental.pallas import tpu as pltpu

F32, BF16 = jnp.float32, jnp.bfloat16
MESH = pl.DeviceIdType.MESH

D = 1024
CONV_CH = 512
CONV_K = 31
CONV_HALO = 32
FOX_W = 512
HEAD_D = 64
N_PAIR = 4
MEM_LEN = 256
MEM_HEADS = 4
MEM_HD = 256
D_FF = 2816
FF_CHUNK = 1408
D_IN = 2568
D_IN_PAD = 2688
OFF_F = 2560
EPS = 1e-6
LANES = 128

ADAM_LR, ADAM_B1, ADAM_B2, ADAM_EPS, ADAM_WD, ADAM_STEP = 0.001, 0.9, 0.999, 1e-08, 0.01, 10

VMEM_LIMIT = 60 * 1024 * 1024

BIG = (("w_out", 256, 1024, False), ("w_mq", 256, 1024, False), ("w_mkv", 1024, 512, True),
       ("w_mo", 256, 1024, False), ("w_gu", 1024, 1408, True), ("w_down", 704, 1024, False),
       ("w_in", 642, 1024, False))

ANY = pl.BlockSpec(memory_space=pl.ANY)


def _sig(x):
    return 1.0 / (1.0 + jnp.exp(-x))


def _dot(a, b):
    return jnp.dot(a, b, preferred_element_type=F32)


def _dot_nt(a, b):
    return lax.dot_general(a, b, (((1,), (1,)), ((), ())), preferred_element_type=F32)


def _dot_tn(a, b):
    return lax.dot_general(a, b, (((0,), (0,)), ((), ())), preferred_element_type=F32)


def _split3(x):
    hi = x.astype(BF16)
    r = x - hi.astype(F32)
    mid = r.astype(BF16)
    return hi, mid, (r - mid.astype(F32)).astype(BF16)


def _dot_01(a, b):
    if a.dtype == jnp.bool_:
        return sum(_dot(a.astype(BF16), t) for t in _split3(b))
    return sum(_dot(t, b.astype(BF16)) for t in _split3(a))


def _resident(a):
    nd = a.ndim
    return pl.BlockSpec(a.shape, lambda *_: (0,) * nd, pipeline_mode=pl.Buffered(1))


def _acc_spec(shape):
    nd = len(shape)
    return pl.BlockSpec(shape, lambda *_: (0,) * nd)


def _params(n_grid):
    return pltpu.CompilerParams(dimension_semantics=("arbitrary",) * n_grid, vmem_limit_bytes=VMEM_LIMIT)


def _sds(shape, dtype):
    return jax.ShapeDtypeStruct(shape, dtype)


def _rms(x):
    r = lax.rsqrt(jnp.mean(x * x, axis=-1, keepdims=True) + EPS)
    return r, x * r


def _rms_bwd(dy, xh, r, g):
    dxh = dy * g
    dx = r * (dxh - xh * jnp.mean(dxh * xh, axis=-1, keepdims=True))
    return dx, dy * xh


def _head_expand(rows, cols):
    hd = lax.broadcasted_iota(jnp.int32, (rows, cols), 1) // HEAD_D
    hr = lax.broadcasted_iota(jnp.int32, (rows, cols), 0)
    return hd == hr


def _feat_major_spec(TB, FOX_T, nb):
    return pl.BlockSpec((1, TB // FOX_T, FOX_W, FOX_T), lambda b, j: (b, j, 0, 0))


def _fwd_in(x2, g_mix, w_int, w_ft, bf_pad, B, S, comm=None):
    T = B * S
    TB = min(512, S)
    nb = S // TB
    FOX_T = min(256, S)

    def body(x_ref, g_ref, w_ref, wf_ref, bf_ref, h_ref, u_ref, gt_ref, q_ref, k_ref, v_ref, zf_ref, c_ref, cq_ref,
             qx_ref, kx_ref, qt_ref, carry):
        j = pl.program_id(1)

        @pl.when(j == 0)
        def _():
            carry[...] = jnp.zeros_like(carry)

        _, xh = _rms(x_ref[...])
        h = (xh * g_ref[...]).astype(BF16)
        h_ref[...] = h
        u_ref[...] = _dot_nt(h, w_ref[0:512, :])
        gt_ref[...] = _dot_nt(h, w_ref[512:1024, :])
        qf = _dot_nt(h, w_ref[1024:1536, :])
        qb = qf.astype(BF16)
        kb = _dot_nt(h, w_ref[1536:2048, :]).astype(BF16)
        q_ref[...] = qb
        k_ref[...] = kb
        for t in range(TB // FOX_T):
            qt_ref[0, t] = qf[t * FOX_T:(t + 1) * FOX_T, :].T.astype(BF16)
        v_ref[...] = _dot_nt(h, w_ref[2048:2560, :]).astype(BF16)
        zf = _dot_nt(h, wf_ref[...]) + bf_ref[...]
        zf_ref[...] = zf
        lane = lax.broadcasted_iota(jnp.int32, zf.shape, 1)
        logf = jnp.where(lane < 8, jnp.minimum(zf, 0.0) - jnp.log(1.0 + jnp.exp(-jnp.abs(zf))), 0.0)
        row = lax.broadcasted_iota(jnp.int32, (TB, TB), 0)
        col = lax.broadcasted_iota(jnp.int32, (TB, TB), 1)
        c = _dot_01(row >= col, logf) + carry[0:1, :]
        carry[0:1, :] = c[TB - 1:TB, :]
        c_ref[...] = c
        cq = _dot_01(c, _head_expand(LANES, FOX_W))
        cq_ref[...] = cq
        hl = lax.broadcasted_iota(jnp.int32, (TB, LANES), 1)
        for hd in range(2 * N_PAIR):
            grp = slice((hd // 2) * LANES, (hd // 2 + 1) * LANES)
            swap = (lambda t: t) if hd % 2 == 0 else (lambda t: pltpu.roll(t, HEAD_D, 1))
            qf = swap(qb[:, grp].astype(F32) * (1.0 / math.sqrt(HEAD_D)))
            kf = swap(kb[:, grp].astype(F32))
            cv = cq[:, grp] if hd % 2 == 1 else pltpu.roll(cq[:, grp], HEAD_D, 1)
            hi = cv.astype(BF16).astype(F32)
            mid = (cv - hi).astype(BF16).astype(F32)
            lo = (cv - hi - mid).astype(BF16).astype(F32)
            pick = lambda a, b, c3, one_from, one_to: jnp.where(hl == a[0], a[1], jnp.where(hl == b[0], b[1], jnp.where(
                hl == c3[0], c3[1], jnp.where((hl >= one_from) & (hl < one_to), 1.0, 0.0))))
            qx = jnp.where(hl < HEAD_D, qf, pick((67, hi), (68, mid), (69, lo), 64, 67))
            kx = jnp.where(hl < HEAD_D, kf, pick((64, -hi), (65, -mid), (66, -lo), 67, 70))
            qx_ref[:, hd * LANES:(hd + 1) * LANES] = qx.astype(BF16)
            kx_ref[:, hd * LANES:(hd + 1) * LANES] = kx.astype(BF16)

    tok = lambda w: pl.BlockSpec((TB, w), lambda b, j: (b * nb + j, 0))
    outs = [(D, BF16), (512, F32), (512, F32), (512, BF16), (512, BF16), (512, BF16), (LANES, F32),
            (LANES, F32), (FOX_W, F32), (2 * FOX_W, BF16), (2 * FOX_W, BF16)]
    return _call(
        body, comm, name="fwd_in", grid=(B, nb),
        in_specs=[tok(D), _resident(g_mix), _resident(w_int), _resident(w_ft), _resident(bf_pad)],
        out_specs=[tok(w) for w, _ in outs] + [_feat_major_spec(TB, FOX_T, nb)],
        out_shape=[_sds((T, w), dt) for w, dt in outs] + [_sds((B, S // FOX_T, FOX_W, FOX_T), BF16)],
        scratch_shapes=[pltpu.VMEM((8, LANES), F32)],
        args=(x2, g_mix, w_int, w_ft, bf_pad))


def _head_sum(n):
    hc = lax.broadcasted_iota(jnp.int32, (n, n), 1) // HEAD_D
    hr = lax.broadcasted_iota(jnp.int32, (n, n), 0) // HEAD_D
    return hc == hr


def _layernorm_silu(y, lg, lb):
    mu = jnp.mean(y, axis=-1, keepdims=True)
    yc = y - mu
    rs = lax.rsqrt(jnp.mean(yc * yc, axis=-1, keepdims=True) + EPS)
    n = yc * rs
    l = n * lg + lb
    return rs, n, l


SUB = 8


def _shifted_copies(cat, sh, rows):
    for r in range(1, SUB):
        sh[r, 0:rows, :] = cat[r:r + rows, :]


def _tap(cat, sh, off, rows, cols=slice(None)):
    r = off % SUB
    return cat[off:off + rows, cols] if r == 0 else sh[r, off - r:off - r + rows, cols]


CONV_ROWS = 128


def _conv_pieces(CB):
    rows = min(CONV_ROWS, CB)
    return [(r0, rows, slice(c0, c0 + LANES)) for c0 in range(0, CONV_CH, LANES) for r0 in range(0, CB, rows)]


def _conv_fwd(u, gt, cw, cb, lng, lnb, B, S, comm=None):
    T = B * S
    CB = min(256, S)
    nb = S // CB

    def body(u_ref, gt_ref, w_ref, cb_ref, lg_ref, lb_ref, y_ref, co_ref, acat, ash):
        j = pl.program_id(1)

        @pl.when(j == 0)
        def _():
            acat[0:CONV_HALO, :] = jnp.zeros((CONV_HALO, CONV_CH), F32)

        acat[CONV_HALO:CONV_HALO + CB, :] = u_ref[...] * _sig(gt_ref[...])
        _shifted_copies(acat, ash, CB + CONV_HALO - SUB)
        for r0, rows, cs in _conv_pieces(CB):
            acc = jnp.zeros((rows, LANES), F32) + cb_ref[:, cs]
            for k in range(CONV_K):
                acc = acc + w_ref[k:k + 1, cs] * _tap(acat, ash, r0 + CONV_HALO - (CONV_K - 1) + k, rows, cs)
            y_ref[r0:r0 + rows, cs] = acc
        acat[0:CONV_HALO, :] = acat[CB:CB + CONV_HALO, :]
        _, _, l = _layernorm_silu(y_ref[...], lg_ref[...], lb_ref[...])
        co_ref[...] = (l * _sig(l)).astype(BF16)

    tok = lambda w: pl.BlockSpec((CB, w), lambda b, j: (b * nb + j, 0))
    return _call(
        body, comm, name="conv_fwd", grid=(B, nb),
        in_specs=[tok(CONV_CH), tok(CONV_CH), _resident(cw), _resident(cb), _resident(lng), _resident(lnb)],
        out_specs=[tok(CONV_CH), tok(CONV_CH)],
        out_shape=[_sds((T, CONV_CH), F32), _sds((T, CONV_CH), BF16)],
        scratch_shapes=[pltpu.VMEM((CONV_HALO + CB, CONV_CH), F32),
                        pltpu.VMEM((SUB, CB + CONV_HALO - SUB, CONV_CH), F32)],
        args=(u, gt, cw, cb, lng, lnb))


def _fox_fwd(qx, kx, v, cq, B, S, comm=None):
    T = B * S
    TQ = min(256, S)
    nq = S // TQ

    def body(qa_ref, qb_ref, ka_ref, kb_ref, v_ref, cq_ref, o_ref, lse_ref, s_scr, s_odd, m_scr, acc_scr):
        i = pl.program_id(2)
        lane = lax.broadcasted_iota(jnp.int32, (TQ, LANES), 1)
        lo = lane < HEAD_D
        qh = (qa_ref[...], qb_ref[...])
        kh = (ka_ref, kb_ref)
        m_scr[...] = jnp.full(m_scr.shape, -1e30, F32)
        acc_scr[...] = jnp.zeros_like(acc_scr)
        row = lax.broadcasted_iota(jnp.int32, (TQ, TQ), 0)
        col = lax.broadcasted_iota(jnp.int32, (TQ, TQ), 1)
        wide = lambda x: jnp.concatenate([x, x], axis=1) if TQ == 2 * LANES else jnp.tile(x, (1, TQ // LANES))

        def scores(j, s_buf):
            start = pl.multiple_of(j * TQ, TQ)
            for h in range(2):
                s_buf[h] = _dot_nt(qh[h], kh[h][pl.ds(start, TQ), :])

        def softmax_step(j, s_buf, diagonal):
            start = pl.multiple_of(j * TQ, TQ)
            vj = v_ref[pl.ds(start, TQ), :]
            for h in range(2):
                def logits():
                    return jnp.where(col <= row, s_buf[h], -1e30) if diagonal else s_buf[h]

                m_old = m_scr[h]
                m_new = jnp.maximum(m_old, jnp.max(logits(), axis=-1, keepdims=True))
                alpha = jnp.exp(m_old - m_new)
                m_scr[h] = m_new
                p = jnp.exp(logits() - wide(m_new)).astype(BF16)
                vx = jnp.where(lo if h == 0 else ~lo, vj, jnp.ones_like(vj))
                acc_scr[h] = alpha * acc_scr[h] + _dot(p, vx)

        def two_blocks(jj, carry):
            j = 2 * jj
            scores(j + 1, s_odd)
            softmax_step(j, s_scr, False)
            scores(j + 2, s_scr)
            softmax_step(j + 1, s_odd, False)
            return carry

        scores(0, s_scr)
        lax.fori_loop(0, i // 2, two_blocks, 0)

        @pl.when(i % 2 == 0)
        def _():
            softmax_step(i, s_scr, True)

        @pl.when(i % 2 == 1)
        def _():
            scores(i, s_odd)
            softmax_step(i - 1, s_scr, False)
            softmax_step(i, s_odd, True)

        acc = jnp.where(lo, acc_scr[0], acc_scr[1])
        den = pltpu.roll(jnp.where(lo, acc_scr[1], acc_scr[0]), HEAD_D, 1)
        o_ref[...] = acc / den
        lse_ref[...] = cq_ref[...] - (jnp.where(lo, m_scr[0], m_scr[1]) + jnp.log(den))

    qspec = pl.BlockSpec((TQ, LANES), lambda b, p, i: (b * nq + i, p))
    kspec = pl.BlockSpec((S, LANES), lambda b, p, i: (b, p))
    qhead = lambda h: pl.BlockSpec((TQ, LANES), lambda b, p, i: (b * nq + i, 2 * p + h))
    khead = lambda h: pl.BlockSpec((S, LANES), lambda b, p, i: (b, 2 * p + h))
    return _call(
        body, comm, name="fox_fwd", grid=(B, N_PAIR, nq),
        in_specs=[qhead(0), qhead(1), khead(0), khead(1), kspec, qspec],
        out_specs=[qspec, qspec],
        out_shape=[_sds((T, FOX_W), F32), _sds((T, FOX_W), F32)],
        scratch_shapes=[pltpu.VMEM((2, TQ, TQ), F32), pltpu.VMEM((2, TQ, TQ), F32),
                        pltpu.VMEM((2, TQ, LANES), F32), pltpu.VMEM((2, TQ, LANES), F32)],
        args=(qx, qx, kx, kx, v, cq))


def _mem_kv(mem2, g_mem, w_mkv, B):
    def body(m_ref, g_ref, w_ref, mn_ref, km_ref, vm_ref):
        _, xh = _rms(m_ref[...])
        mn = (xh * g_ref[...]).astype(BF16)
        mn_ref[...] = mn
        for s in range(2):
            km_ref[:, 512 * s:512 * (s + 1)] = _dot(mn, w_ref[s]).astype(BF16)
            vm_ref[:, 512 * s:512 * (s + 1)] = _dot(mn, w_ref[2 + s]).astype(BF16)

    blk = pl.BlockSpec((MEM_LEN, D), lambda b: (b, 0))
    return pl.pallas_call(
        body, name="mem_kv", grid=(B,),
        in_specs=[blk, _resident(g_mem), _resident(w_mkv)],
        out_specs=[blk, blk, blk],
        out_shape=[_sds((B * MEM_LEN, D), BF16)] * 3,
        compiler_params=_params(1),
    )(mem2, g_mem, w_mkv)


def _mem_probs(qm, km):
    ps = []
    for h in range(MEM_HEADS):
        hs = slice(h * MEM_HD, (h + 1) * MEM_HD)
        lg = _dot_nt(qm[:, hs], km[:, hs]) * (1.0 / math.sqrt(MEM_HD))
        e = jnp.exp(lg - jnp.max(lg, axis=-1, keepdims=True))
        ps.append(e / jnp.sum(e, axis=-1, keepdims=True))
    return ps


def _fwd_mid(x2, co, o, km, vm, w_out, w_mq, w_mo, g_x, B, S, comm=None):
    T = B * S
    TB = min(512, S)
    nb = S // TB

    def body(x_ref, co_ref, o_ref, km_ref, vm_ref, wo_ref, wq_ref, wm_ref, g_ref,
             x1_ref, hx_ref, qm_ref, om_ref, x2_ref, cat_ref):
        cat_ref[:, 0:CONV_CH] = co_ref[...]
        cat_ref[:, CONV_CH:D] = o_ref[...].astype(BF16)
        x1 = x_ref[...] + _dot(cat_ref[...], wo_ref[...])
        x1_ref[...] = x1
        _, xh = _rms(x1)
        hx = (xh * g_ref[...]).astype(BF16)
        hx_ref[...] = hx
        qm = _dot(hx, wq_ref[...]).astype(BF16)
        qm_ref[...] = qm
        ps = _mem_probs(qm, km_ref[...])
        vmv = vm_ref[...]
        for h in range(MEM_HEADS):
            hs = slice(h * MEM_HD, (h + 1) * MEM_HD)
            om_ref[:, hs] = _dot(ps[h].astype(BF16), vmv[:, hs]).astype(BF16)
        x2_ref[...] = x1 + _dot(om_ref[...], wm_ref[...])

    tok = lambda w: pl.BlockSpec((TB, w), lambda b, j: (b * nb + j, 0))
    memb = pl.BlockSpec((MEM_LEN, D), lambda b, j: (b, 0))
    outs = [(D, F32), (D, BF16), (D, BF16), (D, BF16), (D, F32), (D, BF16)]
    return _call(
        body, comm, name="fwd_mid", grid=(B, nb),
        in_specs=[tok(D), tok(CONV_CH), tok(FOX_W), memb, memb, _resident(w_out), _resident(w_mq), _resident(w_mo),
                  _resident(g_x)],
        out_specs=[tok(w) for w, _ in outs],
        out_shape=[_sds((T, w), dt) for w, dt in outs],
        scratch_shapes=[],
        args=(x2, co, o, km, vm, w_out, w_mq, w_mo, g_x))


def _load_gate_up(wgu_hbm, wg, wu, sems):
    copies = [pltpu.make_async_copy(wgu_hbm.at[s], (wg if s < 2 else wu).at[:, pl.ds((s % 2) * FF_CHUNK, FF_CHUNK)],
                                    sems.at[s]) for s in range(4)]
    for cp in copies:
        cp.start()
    for cp in copies:
        cp.wait()


def _fwd_ffn(x2, tgt, w_gu, w_down, g_ffn, g_final, T):
    TB = min(256, T)
    nb = T // TB

    def body(x_ref, t_ref, wgu_ref, wd_ref, gf_ref, gl_ref, hf_ref, gu_ref, act_ref, dx3_ref, loss_ref, dgl_ref,
             wg, wu, sems):
        i = pl.program_id(0)

        @pl.when(i == 0)
        def _():
            _load_gate_up(wgu_ref, wg, wu, sems)
            loss_ref[...] = jnp.zeros_like(loss_ref)
            dgl_ref[...] = jnp.zeros_like(dgl_ref)

        x2v = x_ref[...]
        _, xh = _rms(x2v)
        hf = (xh * gf_ref[...]).astype(BF16)
        hf_ref[...] = hf
        g = _dot(hf, wg[...])
        u = _dot(hf, wu[...])
        gu_ref[:, 0:D_FF] = g
        gu_ref[:, D_FF:2 * D_FF] = u
        act = (g * _sig(g) * u).astype(BF16)
        act_ref[...] = act
        x3 = x2v + _dot(act, wd_ref[...])
        r3, xh3 = _rms(x3)
        gl = gl_ref[...]
        e = xh3 * gl - t_ref[...]
        loss_ref[...] += jnp.sum(e * e) * (0.5 / D)
        dy = e * (1.0 / D)
        dx3, dgl = _rms_bwd(dy, xh3, r3, gl)
        dx3_ref[...] = dx3
        dgl_ref[...] += jnp.sum(dgl, axis=0, keepdims=True)

    tok = lambda w: pl.BlockSpec((TB, w), lambda i: (i, 0))
    return pl.pallas_call(
        body, name="fwd_ffn", grid=(nb,),
        in_specs=[tok(D), tok(D), ANY, _resident(w_down), _resident(g_ffn), _resident(g_final)],
        out_specs=[tok(D), tok(2 * D_FF), tok(D_FF), tok(D), _acc_spec((1, LANES)), _acc_spec((1, D))],
        out_shape=[_sds((T, D), BF16), _sds((T, 2 * D_FF), F32), _sds((T, D_FF), BF16), _sds((T, D), F32),
                   _sds((1, LANES), F32), _sds((1, D), F32)],
        scratch_shapes=[pltpu.VMEM((D, D_FF), BF16), pltpu.VMEM((D, D_FF), BF16), pltpu.SemaphoreType.DMA((4,))],
        compiler_params=_params(1),
    )(x2, tgt, w_gu, w_down, g_ffn, g_final)


def _bwd_ffn(dx3, gu, x2, w_gu, w_down, g_ffn, T):
    TB = min(256, T)
    nb = T // TB

    def body(d_ref, gu_ref, x_ref, wgu_ref, wd_ref, gf_ref, dgu_ref, dx2_ref, dgf_ref, wg, wu, sems):
        i = pl.program_id(0)

        @pl.when(i == 0)
        def _():
            _load_gate_up(wgu_ref, wg, wu, sems)
            dgf_ref[...] = jnp.zeros_like(dgf_ref)

        dx3v = d_ref[...]
        db = dx3v.astype(BF16)
        dact = _dot_nt(db, wd_ref[...])
        g = gu_ref[:, 0:D_FF]
        u = gu_ref[:, D_FF:2 * D_FF]
        sg = _sig(g)
        dg = (dact * u * sg * (1.0 + g * (1.0 - sg))).astype(BF16)
        du = (dact * g * sg).astype(BF16)
        dgu_ref[:, 0:D_FF] = dg
        dgu_ref[:, D_FF:2 * D_FF] = du
        dhf = _dot_nt(dg, wg[...]) + _dot_nt(du, wu[...])
        r2, xh2 = _rms(x_ref[...])
        dx, dg_tok = _rms_bwd(dhf, xh2, r2, gf_ref[...])
        dx2_ref[...] = dx3v + dx
        dgf_ref[...] += jnp.sum(dg_tok, axis=0, keepdims=True)

    tok = lambda w: pl.BlockSpec((TB, w), lambda i: (i, 0))
    return pl.pallas_call(
        body, name="bwd_ffn", grid=(nb,),
        in_specs=[tok(D), tok(2 * D_FF), tok(D), ANY, _resident(w_down), _resident(g_ffn)],
        out_specs=[tok(2 * D_FF), tok(D), _acc_spec((1, D))],
        out_shape=[_sds((T, 2 * D_FF), BF16), _sds((T, D), F32), _sds((1, D), F32)],
        scratch_shapes=[pltpu.VMEM((D, D_FF), BF16), pltpu.VMEM((D, D_FF), BF16), pltpu.SemaphoreType.DMA((4,))],
        compiler_params=_params(1),
    )(dx3, gu, x2, w_gu, w_down, g_ffn)


def _bwd_mid(dx2, x1, qm, km, vm, o, w_mo, w_mq, w_out, g_x, B, S, comm=None):
    T = B * S
    TB = min(512, S)
    nb = S // TB
    FOX_T = min(256, S)
    inv = 1.0 / math.sqrt(MEM_HD)

    def body(d_ref, x1_ref, qm_ref, km_ref, vm_ref, o_ref, wm_ref, wq_ref, wo_ref, g_ref,
             dx1_ref, dqm_ref, dco_ref, do_ref, dd_ref, dkm_ref, dvm_ref, dgx_ref, dot_ref):
        b = pl.program_id(0)
        j = pl.program_id(1)

        @pl.when((b == 0) & (j == 0))
        def _():
            dgx_ref[...] = jnp.zeros_like(dgx_ref)

        @pl.when(j == 0)
        def _():
            dkm_ref[...] = jnp.zeros_like(dkm_ref)
            dvm_ref[...] = jnp.zeros_like(dvm_ref)

        dx2v = d_ref[...]
        dom = _dot_nt(dx2v.astype(BF16), wm_ref[...]).astype(BF16)
        qmv = qm_ref[...]
        kmv = km_ref[...]
        vmv = vm_ref[...]
        ps = _mem_probs(qmv, kmv)
        for h in range(MEM_HEADS):
            hs = slice(h * MEM_HD, (h + 1) * MEM_HD)
            p = ps[h]
            dp = _dot_nt(dom[:, hs], vmv[:, hs])
            ds = (p * (dp - jnp.sum(p * dp, axis=-1, keepdims=True))).astype(BF16)
            dqm_ref[:, hs] = (_dot(ds, kmv[:, hs]) * inv).astype(BF16)
            dkm_ref[:, hs] += _dot_tn(ds, qmv[:, hs]) * inv
            dvm_ref[:, hs] += _dot_tn(p.astype(BF16), dom[:, hs])
        dhx = _dot_nt(dqm_ref[...], wq_ref[...])
        r1, xh1 = _rms(x1_ref[...])
        dx, dg_tok = _rms_bwd(dhx, xh1, r1, g_ref[...])
        dx1 = dx2v + dx
        dx1_ref[...] = dx1
        dgx_ref[...] += jnp.sum(dg_tok, axis=0, keepdims=True)
        d1b = dx1.astype(BF16)
        dco_ref[...] = _dot_nt(d1b, wo_ref[0:CONV_CH, :])
        do = _dot_nt(d1b, wo_ref[CONV_CH:D, :])
        dob = do.astype(BF16)
        do_ref[...] = dob
        for t in range(TB // FOX_T):
            dot_ref[0, t] = do[t * FOX_T:(t + 1) * FOX_T, :].T.astype(BF16)
        dd_ref[...] = _dot_01(dob.astype(F32) * o_ref[...], _head_sum(FOX_W))

    tok = lambda w: pl.BlockSpec((TB, w), lambda b, j: (b * nb + j, 0))
    memb = pl.BlockSpec((MEM_LEN, D), lambda b, j: (b, 0))
    outs = [(D, F32), (D, BF16), (CONV_CH, F32), (FOX_W, BF16), (FOX_W, F32)]
    return _call(
        body, comm, name="bwd_mid", grid=(B, nb),
        in_specs=[tok(D), tok(D), tok(D), memb, memb, tok(FOX_W), _resident(w_mo), _resident(w_mq), _resident(w_out),
                  _resident(g_x)],
        out_specs=[tok(w) for w, _ in outs] + [memb, memb, _acc_spec((1, D)), _feat_major_spec(TB, FOX_T, nb)],
        out_shape=[_sds((T, w), dt) for w, dt in outs] + [_sds((B * MEM_LEN, D), F32)] * 2 + [_sds((1, D), F32)]
        + [_sds((B, S // FOX_T, FOX_W, FOX_T), BF16)],
        scratch_shapes=[],
        args=(dx2, x1, qm, km, vm, o, w_mo, w_mq, w_out, g_x))


def _mem_bwd(dkm, dvm, mem2, w_mkv, g_mem, B):
    def body(dk_ref, dv_ref, m_ref, w_ref, g_ref, dkv_ref, dg_ref):
        b = pl.program_id(0)

        @pl.when(b == 0)
        def _():
            dg_ref[...] = jnp.zeros_like(dg_ref)

        dk = dk_ref[...].astype(BF16)
        dv = dv_ref[...].astype(BF16)
        dkv_ref[:, 0:D] = dk
        dkv_ref[:, D:2 * D] = dv
        dmn = jnp.zeros((MEM_LEN, D), F32)
        for s in range(2):
            dmn = dmn + _dot_nt(dk[:, 512 * s:512 * (s + 1)], w_ref[s]) + _dot_nt(dv[:, 512 * s:512 * (s + 1)], w_ref[2 + s])
        _, xh = _rms(m_ref[...])
        dg_ref[...] += jnp.sum(dmn * xh, axis=0, keepdims=True)

    blk = pl.BlockSpec((MEM_LEN, D), lambda b: (b, 0))
    return pl.pallas_call(
        body, name="mem_bwd", grid=(B,),
        in_specs=[blk, blk, blk, _resident(w_mkv), _resident(g_mem)],
        out_specs=[pl.BlockSpec((MEM_LEN, 2 * D), lambda b: (b, 0)), _acc_spec((1, D))],
        out_shape=[_sds((B * MEM_LEN, 2 * D), BF16), _sds((1, D), F32)],
        compiler_params=_params(1),
    )(dkm, dvm, mem2, w_mkv, g_mem)


def _fox_bwd(q, k, v, do, bias, dd, ckT, qT, doT, B, S, comm=None):
    T = B * S
    TK = min(256, S)
    nk = S // TK
    scale = 1.0 / math.sqrt(HEAD_D)

    def body(q_ref, k_ref, v_ref, do_ref, bias_ref, dd_ref, ck_ref, qt_ref, dot_ref, dq_ref, dk_ref, dv_ref, dc_ref,
             dcq_ref, dq_acc, rs_acc, s_scr, dp_scr, s_odd, dp_odd, dk_acc, dv_acc, dc_acc):
        j = pl.program_id(2)

        @pl.when(j == 0)
        def _():
            dq_acc[...] = jnp.zeros_like(dq_acc)
            rs_acc[...] = jnp.zeros_like(rs_acc)

        dk_acc[...] = jnp.zeros_like(dk_acc)
        dv_acc[...] = jnp.zeros_like(dv_acc)
        dc_acc[...] = jnp.zeros_like(dc_acc)
        lane = lax.broadcasted_iota(jnp.int32, (TK, LANES), 1)
        lo = lane < HEAD_D
        ks = k_ref[...] * jnp.asarray(scale, BF16)
        v2 = v_ref[...]
        zero = jnp.zeros_like(ks)
        kh = (jnp.where(lo, ks, zero), jnp.where(lo, zero, ks))
        vh = (jnp.where(lo, v2, zero), jnp.where(lo, zero, v2))
        kstart = pl.multiple_of(j * TK, TK)
        ckh = tuple(ck_ref[0, 0, h:h + 1, pl.ds(kstart, TK)] for h in range(2))
        row = lax.broadcasted_iota(jnp.int32, (TK, TK), 0)
        col = lax.broadcasted_iota(jnp.int32, (TK, TK), 1)
        wide = lambda x: jnp.concatenate([x, x], axis=1) if TK == 2 * LANES else jnp.tile(x, (1, TK // LANES))

        def scores(i, s_buf, dp_buf):
            start = pl.multiple_of(i * TK, TK)
            qi = q_ref[pl.ds(start, TK), :]
            doi = do_ref[pl.ds(start, TK), :]
            for h in range(2):
                s_buf[h] = _dot_nt(qi, kh[h])
                dp_buf[h] = _dot_nt(doi, vh[h])

        def grads(i, s_buf, dp_buf, diagonal):
            start = pl.multiple_of(i * TK, TK)
            bias2 = bias_ref[pl.ds(start, TK), :]
            dd2 = dd_ref[pl.ds(start, TK), :]
            for h in range(2):
                hc = slice(h * HEAD_D, h * HEAD_D + 1)
                bias = jnp.broadcast_to(bias2[:, hc], (TK, LANES))
                ddh = jnp.broadcast_to(dd2[:, hc], (TK, LANES))
                p = jnp.exp((s_buf[h] - ckh[h]) + wide(bias))
                if diagonal:
                    p = jnp.where(col <= row, p, 0.0)
                ds = p * (dp_buf[h] - wide(ddh))
                dc_acc[h, 0:1, :] += jnp.sum(ds, axis=0, keepdims=True)
                rs_acc[h, pl.ds(start, TK), :] += jnp.sum(ds, axis=1, keepdims=True)
                pb = p.astype(BF16)
                dsb = ds.astype(BF16)
                feat = slice(h * HEAD_D, (h + 1) * HEAD_D)
                dv_acc[feat, :] += _dot(dot_ref[0, i, feat, :], pb)
                dk_acc[feat, :] += _dot(qt_ref[0, i, feat, :], dsb)
                dq_acc[pl.ds(start, TK), :] += _dot(dsb, kh[h])

        n_off = nk - 1 - j
        block = lambda t: jnp.where(t < n_off, j + 1 + t, j)

        def two_blocks(tt, carry):
            t = 2 * tt
            scores(block(t + 1), s_odd, dp_odd)
            grads(block(t), s_scr, dp_scr, False)
            scores(block(t + 2), s_scr, dp_scr)
            grads(block(t + 1), s_odd, dp_odd, False)
            return carry

        scores(block(0), s_scr, dp_scr)
        lax.fori_loop(0, n_off // 2, two_blocks, 0)

        @pl.when(n_off % 2 == 0)
        def _():
            grads(j, s_scr, dp_scr, True)

        @pl.when(n_off % 2 == 1)
        def _():
            scores(j, s_odd, dp_odd)
            grads(nk - 1, s_scr, dp_scr, False)
            grads(j, s_odd, dp_odd, True)

        dk_ref[...] = (dk_acc[...].T * scale).astype(BF16)
        dv_ref[...] = dv_acc[...].T.astype(BF16)
        sub = lax.broadcasted_iota(jnp.int32, (8, TK), 0)
        dca = dc_acc[0, 0:1, :]
        dcb = dc_acc[1, 0:1, :]
        dc_ref[0, 0] = jnp.where(sub == 0, -dca, jnp.where(sub == 1, -dcb, 0.0))

        @pl.when(j == nk - 1)
        def _():
            dq_ref[...] = dq_acc[...].astype(BF16)
            lo_s = lax.broadcasted_iota(jnp.int32, (S, LANES), 1) < HEAD_D
            dcq_ref[...] = jnp.where(lo_s, rs_acc[0], rs_acc[1])

    full = pl.BlockSpec((S, LANES), lambda b, p, j: (b, p))
    blk = pl.BlockSpec((TK, LANES), lambda b, p, j: (b * nk + j, p))
    featT = pl.BlockSpec((1, nk, LANES, TK), lambda b, p, j: (b, 0, p, 0))
    return _call(
        body, comm, name="fox_bwd", grid=(B, N_PAIR, nk),
        in_specs=[full, blk, blk, full, full, full, pl.BlockSpec((1, 1, 8, S), lambda b, p, j: (b, p, 0, 0)),
                  featT, featT],
        out_specs=[full, blk, blk, pl.BlockSpec((1, 1, 8, TK), lambda b, p, j: (b, p, 0, j)), full],
        out_shape=[_sds((T, FOX_W), BF16), _sds((T, FOX_W), BF16), _sds((T, FOX_W), BF16),
                   _sds((B, N_PAIR, 8, S), F32), _sds((T, FOX_W), F32)],
        scratch_shapes=[pltpu.VMEM((S, LANES), F32), pltpu.VMEM((2, S, 1), F32),
                        pltpu.VMEM((2, TK, TK), F32), pltpu.VMEM((2, TK, TK), F32),
                        pltpu.VMEM((2, TK, TK), F32), pltpu.VMEM((2, TK, TK), F32),
                        pltpu.VMEM((LANES, TK), F32), pltpu.VMEM((LANES, TK), F32), pltpu.VMEM((2, 8, TK), F32)],
        args=(q, k, v, do, bias, dd, ckT, qT, doT))


def _fgate_bwd(dc8, zf, B, S):
    T = B * S
    TB = min(512, S)
    nb = S // TB

    def body(dc_ref, zf_ref, dzf_ref, dbf_ref, carry):
        b = pl.program_id(0)
        j = pl.program_id(1)

        @pl.when((b == 0) & (j == 0))
        def _():
            dbf_ref[...] = jnp.zeros_like(dbf_ref)

        @pl.when(j == 0)
        def _():
            carry[...] = jnp.zeros_like(carry)

        dc = dc_ref[...]
        row = lax.broadcasted_iota(jnp.int32, (TB, TB), 0)
        col = lax.broadcasted_iota(jnp.int32, (TB, TB), 1)
        dlogf = _dot_01(col >= row, dc) + carry[0:1, :]
        carry[0:1, :] = dlogf[0:1, :]
        lane = lax.broadcasted_iota(jnp.int32, dc.shape, 1)
        dzf = jnp.where(lane < 8, dlogf * _sig(-zf_ref[...]), 0.0)
        dzf_ref[...] = dzf.astype(BF16)
        dbf_ref[...] += jnp.sum(dzf, axis=0, keepdims=True)

    tok = pl.BlockSpec((TB, LANES), lambda b, j: (b * nb + (nb - 1 - j), 0))
    return pl.pallas_call(
        body, name="fgate_bwd", grid=(B, nb),
        in_specs=[tok, tok],
        out_specs=[tok, _acc_spec((1, LANES))],
        out_shape=[_sds((T, LANES), BF16), _sds((1, LANES), F32)],
        scratch_shapes=[pltpu.VMEM((8, LANES), F32)],
        compiler_params=_params(2),
    )(dc8, zf)


def _conv_bwd(dco, y, u, gt, cw, lng, lnb, B, S, comm=None):
    T = B * S
    CB = min(256, S)
    nb = S // CB
    hb = CB // CONV_HALO

    def body(dco_ref, y_ref, u_ref, gt_ref, up_ref, gp_ref, w_ref, lg_ref, lb_ref,
             du_ref, dgt_ref, dw_ref, vec_ref, acat, dycat, ash, dysh):
        b = pl.program_id(0)
        j = pl.program_id(1)
        jr = nb - 1 - j

        @pl.when((b == 0) & (j == 0))
        def _():
            dw_ref[...] = jnp.zeros_like(dw_ref)
            vec_ref[...] = jnp.zeros_like(vec_ref)

        @pl.when(j == 0)
        def _():
            dycat[CB:CB + CONV_HALO, :] = jnp.zeros((CONV_HALO, CONV_CH), F32)

        lg = lg_ref[...]
        rs, n, l = _layernorm_silu(y_ref[...], lg, lb_ref[...])
        sg = _sig(l)
        dl = dco_ref[...] * (sg * (1.0 + l * (1.0 - sg)))
        dn = dl * lg
        dy = rs * (dn - jnp.mean(dn, axis=-1, keepdims=True) - n * jnp.mean(dn * n, axis=-1, keepdims=True))
        vec_ref[0:1, :] += jnp.sum(dy, axis=0, keepdims=True)
        vec_ref[1:2, :] += jnp.sum(dl * n, axis=0, keepdims=True)
        vec_ref[2:3, :] += jnp.sum(dl, axis=0, keepdims=True)
        dycat[0:CB, :] = dy
        acat[0:CONV_HALO, :] = jnp.where(jr > 0, up_ref[...] * _sig(gp_ref[...]), 0.0)
        acat[CONV_HALO:CONV_HALO + CB, :] = u_ref[...] * _sig(gt_ref[...])
        _shifted_copies(acat, ash, CB + CONV_HALO - SUB)
        _shifted_copies(dycat, dysh, CB + CONV_HALO - SUB)
        for r0, rows, cs in _conv_pieces(CB):
            dyp = dycat[r0:r0 + rows, cs]
            da = jnp.zeros((rows, LANES), F32)
            for k in range(CONV_K):
                da = da + w_ref[k:k + 1, cs] * _tap(dycat, dysh, r0 + CONV_K - 1 - k, rows, cs)
                dw_ref[k:k + 1, cs] += jnp.sum(dyp * _tap(acat, ash, r0 + CONV_HALO - (CONV_K - 1) + k, rows, cs),
                                               axis=0, keepdims=True)
            uv = u_ref[r0:r0 + rows, cs]
            sgt = _sig(gt_ref[r0:r0 + rows, cs])
            du_ref[r0:r0 + rows, cs] = (da * sgt).astype(BF16)
            dgt_ref[r0:r0 + rows, cs] = (da * uv * sgt * (1.0 - sgt)).astype(BF16)
        dycat[CB:CB + CONV_HALO, :] = dycat[0:CONV_HALO, :]

    tok = lambda w: pl.BlockSpec((CB, w), lambda b, j: (b * nb + (nb - 1 - j), 0))
    prev = pl.BlockSpec((CONV_HALO, CONV_CH), lambda b, j: (jnp.maximum((b * nb + (nb - 1 - j)) * hb - 1, 0), 0))
    return _call(
        body, comm, name="conv_bwd", grid=(B, nb),
        in_specs=[tok(CONV_CH), tok(CONV_CH), tok(CONV_CH), tok(CONV_CH), prev, prev, _resident(cw), _resident(lng),
                  _resident(lnb)],
        out_specs=[tok(CONV_CH), tok(CONV_CH), _acc_spec((CONV_HALO, CONV_CH)), _acc_spec((8, CONV_CH))],
        out_shape=[_sds((T, CONV_CH), BF16), _sds((T, CONV_CH), BF16), _sds((CONV_HALO, CONV_CH), F32),
                   _sds((8, CONV_CH), F32)],
        scratch_shapes=[pltpu.VMEM((CONV_HALO + CB, CONV_CH), F32), pltpu.VMEM((CB + CONV_HALO, CONV_CH), F32),
                        pltpu.VMEM((SUB, CB + CONV_HALO - SUB, CONV_CH), F32),
                        pltpu.VMEM((SUB, CB + CONV_HALO - SUB, CONV_CH), F32)],
        args=(dco, y, u, gt, u, gt, cw, lng, lnb))


def _bwd_in(dz, w_int, w_ft, x2, dx1, g_mix, T, comm=None):
    TB = min(512, T)
    nb = T // TB

    def body(dz_ref, w_ref, wf_ref, x_ref, d1_ref, g_ref, gx_ref, dg_ref):
        i = pl.program_id(0)

        @pl.when(i == 0)
        def _():
            dg_ref[...] = jnp.zeros_like(dg_ref)

        dh = _dot(dz_ref[:, 0:OFF_F], w_ref[0:OFF_F, :]) + _dot(dz_ref[:, OFF_F:D_IN_PAD], wf_ref[...])
        r0, xh0 = _rms(x_ref[...])
        dx, dg_tok = _rms_bwd(dh, xh0, r0, g_ref[...])
        gx_ref[...] = d1_ref[...] + dx
        dg_ref[...] += jnp.sum(dg_tok, axis=0, keepdims=True)

    tok = lambda w: pl.BlockSpec((TB, w), lambda i: (i, 0))
    return _call(
        body, comm, name="bwd_in", grid=(nb,),
        in_specs=[tok(D_IN_PAD), _resident(w_int), _resident(w_ft), tok(D), tok(D), _resident(g_mix)],
        out_specs=[tok(D), _acc_spec((1, D))],
        out_shape=[_sds((T, D), F32), _sds((1, D), F32)],
        scratch_shapes=[],
        args=(dz, w_int, w_ft, x2, dx1, g_mix))


def _dw(a, b, name, tn, slabs=0, tk=None, rows=None):
    T, K = a.shape
    N = b.shape[1]
    per = tn // slabs if slabs else 1
    tk = tk or (K if K <= 1024 else K // 2)
    tt = min(1024, T)
    nt = T // tt

    def body(a_ref, b_ref, o_ref, acc):
        t = pl.program_id(2)

        @pl.when(t == 0)
        def _():
            acc[...] = jnp.zeros_like(acc)

        acc[...] += _dot_tn(a_ref[...].astype(BF16), b_ref[...].astype(BF16))

        @pl.when(t == nt - 1)
        def _():
            if slabs:
                for sl in range(per):
                    o_ref[sl] = acc[:, sl * slabs:(sl + 1) * slabs]
            else:
                o_ref[...] = acc[...]

    return pl.pallas_call(
        body, name=name, grid=(K // tk, N // tn, nt),
        in_specs=[pl.BlockSpec((tt, tk), lambda i, j, t: (t, i)), pl.BlockSpec((tt, tn), lambda i, j, t: (t, j))],
        out_specs=(pl.BlockSpec((per, tk, slabs), lambda i, j, t: (j, i, 0)) if slabs
                   else pl.BlockSpec((tk, tn), lambda i, j, t: (i, j))),
        out_shape=_sds((N // slabs, K, slabs) if slabs else (rows or K, N), F32),
        scratch_shapes=[pltpu.VMEM((tk, tn), F32)],
        compiler_params=_params(3),
    )(a, b)


def _pos():
    return lax.axis_index("x"), lax.axis_index("y"), lax.axis_index("c")


def _remote(src, dst, ssem, rsem, to):
    return pltpu.make_async_remote_copy(src_ref=src, dst_ref=dst, send_sem=ssem, recv_sem=rsem, device_id=to,
                                        device_id_type=MESH)


def _split_axis(shape):
    return 0 if shape[0] % 32 == 0 else 1


def _half_shape(shape, parts=2):
    return (shape[0] // parts, shape[1]) if _split_axis(shape) == 0 else (shape[0], shape[1] // parts)


def _half(shape, c):
    R, C = shape
    if _split_axis(shape) == 0:
        return (pl.ds(pl.multiple_of(c * (R // 2), 16), R // 2), slice(None))
    return (slice(None), pl.ds(pl.multiple_of(c * (C // 2), LANES), C // 2))


def _half_block(shape, parts, lead, which):
    blk = _half_shape(shape, parts)
    idx = (which, 0) if _split_axis(shape) == 0 else (0, which)
    return blk, tuple(lead) + idx


class _Comm:
    def __init__(self, ins, out_shapes, sems, start, finish):
        self.ins, self.out_shapes, self.sems, self.start, self.finish = list(ins), list(out_shapes), list(sems), start, finish


def _ag_comm(shards):
    n = len(shards)

    def parts(ins, outs, sems):
        send_sems, recv_sems, local_sems = sems
        x, y, c = _pos()
        me, sib = (x, y, c), (x, y, 1 - c)
        chips = [(1 - x, y), (x, 1 - y), (1 - x, 1 - y)]

        def rows(w, px, py, pc):
            return outs[w].at[(2 * px + py,) + _half(shards[w].shape, pc)]

        def copy(w, k, block, to, src=None):
            return _remote(rows(w, *block) if src is None else src, rows(w, *block), send_sems.at[w, k],
                           recv_sems.at[w, k], to)

        mine, first = [], []
        for w in range(n):
            src = ins[w].at[_half(shards[w].shape, c)]
            mine.append(pltpu.make_async_copy(src, rows(w, *me), local_sems.at[w]))
            first += [copy(w, 0, me, sib, src=src)] + [copy(w, 1 + j, me, (*chip, c), src=src) for j, chip in enumerate(chips)]
        return c, me, sib, chips, copy, mine, first

    def start(ins, outs, sems):
        _, _, _, _, _, mine, first = parts(ins, outs, sems)
        for cp in mine + first:
            cp.start()

    def finish(ins, outs, sems):
        c, me, sib, chips, copy, mine, first = parts(ins, outs, sems)
        passed = []
        for w in range(n):
            for j, chip in enumerate(chips):
                copy(w, 1 + j, (*chip, c), me).wait_recv()
                passed.append(copy(w, 4 + j, (*chip, c), sib))
                passed[-1].start()
        for w in range(n):
            copy(w, 0, sib, me).wait_recv()
            for j, chip in enumerate(chips):
                copy(w, 4 + j, (*chip, 1 - c), me).wait_recv()
        for cp in first + passed:
            cp.wait_send()
        for cp in mine:
            cp.wait()

    D7 = pltpu.SemaphoreType.DMA((n, 7))
    return _Comm(shards, [_sds((4,) + s.shape, s.dtype) for s in shards], [D7, D7, pltpu.SemaphoreType.DMA((n,))],
                 start, finish)


def _sibling_comm(gs):
    n = len(gs)

    def copies(ins, outs, sems):
        send_sems, recv_sems = sems
        x, y, c = _pos()
        return [_remote(ins[w].at[(s,) + _half(gs[w].shape[1:], 1 - c)], outs[w].at[s], send_sems.at[w, s],
                        recv_sems.at[w, s], (x, y, 1 - c)) for w in range(n) for s in range(4)]

    def start(ins, outs, sems):
        for cp in copies(ins, outs, sems):
            cp.start()

    def finish(ins, outs, sems):
        for cp in copies(ins, outs, sems):
            cp.wait()

    D4 = pltpu.SemaphoreType.DMA((n, 4))
    return _Comm(gs, [_sds((4,) + _half_shape(g.shape[1:]), F32) for g in gs], [D4, D4], start, finish)


def _ici_comm(pbs):
    n = len(pbs)

    def copies(ins, outs, sems):
        send_sems, recv_sems = sems
        x, y, c = _pos()
        return [_remote(ins[w].at[2 * tx + ty], outs[w].at[j], send_sems.at[w, j], recv_sems.at[w, j], (tx, ty, c))
                for w in range(n) for j, (tx, ty) in enumerate([(1 - x, y), (x, 1 - y), (1 - x, 1 - y)])]

    def start(ins, outs, sems):
        for cp in copies(ins, outs, sems):
            cp.start()

    def finish(ins, outs, sems):
        for cp in copies(ins, outs, sems):
            cp.wait()

    D3 = pltpu.SemaphoreType.DMA((n, 3))
    return _Comm(pbs, [_sds((3,) + p.shape[1:], BF16) for p in pbs], [D3, D3], start, finish)


def _join(*comms):
    counts = [(len(c.ins), len(c.out_shapes), len(c.sems)) for c in comms]

    def each(which):
        def run(ins, outs, sems):
            i = o = k = 0
            for c, (ni, no, nk) in zip(comms, counts):
                getattr(c, which)(ins[i:i + ni], outs[o:o + no], sems[k:k + nk])
                i, o, k = i + ni, o + no, k + nk
        return run

    return _Comm(sum((c.ins for c in comms), []), sum((c.out_shapes for c in comms), []),
                 sum((c.sems for c in comms), []), each("start"), each("finish"))


def _run_comm(comm, name):
    ni, no = len(comm.ins), len(comm.out_shapes)

    def body(*refs):
        ins, outs, sems = refs[:ni], refs[ni:ni + no], refs[ni + no:]
        comm.start(ins, outs, sems)
        comm.finish(ins, outs, sems)

    return pl.pallas_call(body, name=name, out_shape=comm.out_shapes, in_specs=[ANY] * ni, out_specs=[ANY] * no,
                          scratch_shapes=comm.sems)(*comm.ins)


def _call(body, comm, *, name, grid, in_specs, out_specs, out_shape, scratch_shapes, args):
    n_grid = len(grid)
    if comm is None:
        res = pl.pallas_call(body, name=name, grid=grid, in_specs=in_specs, out_specs=out_specs, out_shape=out_shape,
                             scratch_shapes=scratch_shapes, compiler_params=_params(n_grid))(*args)
        return list(res), []
    n_in, n_out, n_scr = len(in_specs), len(out_specs), len(scratch_shapes)
    ni, no = len(comm.ins), len(comm.out_shapes)

    def carried(*refs):
        ins, refs = refs[:n_in], refs[n_in:]
        cins, refs = refs[:ni], refs[ni:]
        outs, refs = refs[:n_out], refs[n_out:]
        couts, refs = refs[:no], refs[no:]
        scr, csems = refs[:n_scr], refs[n_scr:]
        ids = [pl.program_id(ax) for ax in range(n_grid)]
        first = functools.reduce(jnp.logical_and, [i == 0 for i in ids])
        last = functools.reduce(jnp.logical_and, [i == g - 1 for i, g in zip(ids, grid)])

        @pl.when(first)
        def _():
            comm.start(cins, couts, csems)

        body(*ins, *outs, *scr)

        @pl.when(last)
        def _():
            comm.finish(cins, couts, csems)

    res = pl.pallas_call(
        carried, name=name, grid=grid, in_specs=list(in_specs) + [ANY] * ni, out_specs=list(out_specs) + [ANY] * no,
        out_shape=list(out_shape) + comm.out_shapes, scratch_shapes=list(scratch_shapes) + comm.sems,
        compiler_params=_params(n_grid))(*args, *comm.ins)
    return list(res[:n_out]), list(res[n_out:])


def _small_allreduce(v, name, halves=()):
    P = v.shape[0]
    n = len(halves)
    vm = pl.BlockSpec(memory_space=pltpu.VMEM)

    def body(v_ref, *refs):
        o_ref, outs = refs[n], refs[n + 1:2 * n + 1]
        gath, send_sems, recv_sems, half_send, half_recv = refs[2 * n + 1:]
        x, y, c = _pos()
        me = 4 * x + 2 * y + c
        gath[me] = v_ref[...]
        cps = []
        for r in range(1, 8):
            tx = (1 - x) if r & 4 else x
            ty = (1 - y) if r & 2 else y
            tc = (1 - c) if r & 1 else c
            cps.append(_remote(v_ref, gath.at[me], send_sems.at[r - 1], recv_sems.at[r - 1], (tx, ty, tc)))
        for w in range(n):
            mine = outs[w].at[_half(halves[w].shape, c)]
            cps.append(_remote(mine, mine, half_send.at[w], half_recv.at[w], (x, y, 1 - c)))
        for cp in cps:
            cp.start()
        for cp in cps:
            cp.wait()
        acc = gath[0]
        for d in range(1, 8):
            acc = acc + gath[d]
        o_ref[...] = acc

    res = pl.pallas_call(
        body, name=name, out_shape=[_sds((P, LANES), F32)] + [_sds(g.shape, F32) for g in halves],
        in_specs=[vm] + [ANY] * n, out_specs=[vm] + [ANY] * n, input_output_aliases={1 + w: 1 + w for w in range(n)},
        scratch_shapes=[pltpu.VMEM((8, P, LANES), F32), pltpu.SemaphoreType.DMA((7,)), pltpu.SemaphoreType.DMA((7,)),
                        pltpu.SemaphoreType.DMA((max(n, 1),)), pltpu.SemaphoreType.DMA((max(n, 1),))],
    )(v, *halves)
    return res[0], list(res[1:])


def _chip_sum(gs, rcvs, pos, name):
    n = len(gs)
    shards = [g.shape[1:] for g in gs]
    hss = [_half_shape(sh) for sh in shards]
    other = lambda i, pos: (pos[1] + 1 + i) % 4

    def body(pos_ref, *refs):
        for w in range(n):
            refs[2 * n + w][...] = (refs[w][...] + refs[n + w][...]).astype(BF16)

    mine = lambda w: pl.BlockSpec((1,) + hss[w], lambda i, pos: _half_block(shards[w], 2, (other(i, pos),), pos[0])[1])
    whole = lambda w: pl.BlockSpec((1,) + hss[w], lambda i, pos: (other(i, pos), 0, 0))
    return pl.pallas_call(
        body, name=name, out_shape=[_sds((4,) + hs, BF16) for hs in hss],
        grid_spec=pltpu.PrefetchScalarGridSpec(
            num_scalar_prefetch=1, grid=(3,),
            in_specs=[mine(w) for w in range(n)] + [whole(w) for w in range(n)],
            out_specs=[whole(w) for w in range(n)]),
        compiler_params=_params(1),
    )(pos, *gs, *rcvs)


def _final_sum(gs, rcvs, rcs, pos, name):
    n = len(gs)
    shards = [g.shape[1:] for g in gs]
    qss = [_half_shape(sh, 4) for sh in shards]

    def body(pos_ref, *refs):
        for w in range(n):
            acc = refs[w][0] + refs[n + w][0]
            for j in range(3):
                acc = acc + refs[2 * n + w][j].astype(F32)
            refs[3 * n + w][...] = acc

    def spec(w, lead_block, lead_index, mine):
        return pl.BlockSpec(lead_block + qss[w], lambda i, pos: _half_block(
            shards[w], 4, lead_index(pos), pos[0] * 2 + i if mine else i)[1])

    own_slab, first, none = (lambda pos: (pos[1],)), (lambda pos: (0,)), (lambda pos: ())
    return pl.pallas_call(
        body, name=name, out_shape=[_sds(sh, F32) for sh in shards],
        grid_spec=pltpu.PrefetchScalarGridSpec(
            num_scalar_prefetch=1, grid=(2,),
            in_specs=[spec(w, (1,), own_slab, True) for w in range(n)] + [spec(w, (1,), own_slab, False) for w in range(n)]
            + [spec(w, (3,), first, False) for w in range(n)],
            out_specs=[spec(w, (), none, True) for w in range(n)]),
        compiler_params=_params(1),
    )(pos, *gs, *rcvs, *rcs)


def _adamw_math(w, g, m, v):
    m = ADAM_B1 * m + (1.0 - ADAM_B1) * g
    v = ADAM_B2 * v + (1.0 - ADAM_B2) * (g * g)
    m_hat = m / (1.0 - ADAM_B1 ** ADAM_STEP)
    v_hat = v / (1.0 - ADAM_B2 ** ADAM_STEP)
    delta = -ADAM_LR * (m_hat / (jnp.sqrt(v_hat) + ADAM_EPS) + ADAM_WD * w)
    return delta, m, v


ADAM_PARTS = 8


def _adamw(ws, gs, ms, vs, name, parts=ADAM_PARTS):
    n = len(ws)
    shapes = [w.shape for w in ws]

    def body(*refs):
        for k in range(n):
            w_ref, g_ref, m_ref, v_ref = (refs[j * n + k] for j in range(4))
            go_ref, d_ref, nm_ref, nv_ref = (refs[(4 + j) * n + k] for j in range(4))
            g = g_ref[...]
            d, nm, nv = _adamw_math(w_ref[...], g, m_ref[...], v_ref[...])
            go_ref[...] = g
            d_ref[...] = d
            nm_ref[...] = nm
            nv_ref[...] = nv

    blk = lambda k: pl.BlockSpec(_half_shape(shapes[k], parts), lambda i: _half_block(shapes[k], parts, (), i)[1])
    res = pl.pallas_call(
        body, name=name, grid=(parts,), in_specs=[blk(k) for k in range(n)] * 4, out_specs=[blk(k) for k in range(n)] * 4,
        out_shape=[_sds(sh, F32) for sh in shapes] * 4, compiler_params=_params(1),
    )(*ws, *gs, *ms, *vs)
    return [tuple(res[j * n + k] for j in range(4)) for k in range(n)]


SMALL = (("g_mix", 8), ("b_f", 8), ("conv_w", None), ("conv_b", 8), ("ln_g", 8), ("ln_b", 8), ("g_x", 8), ("g_mem", 8),
         ("g_ffn", 8), ("g_final", 8), ("loss", 8))


def _pack_small(parts, conv_rows):
    rows = []
    for name, n in SMALL:
        if name not in parts:
            continue
        n = conv_rows if n is None else n
        flat = parts[name].reshape(-1).astype(F32)
        flat = jnp.pad(flat, (0, n * LANES - flat.shape[0]))
        rows.append(flat.reshape(n, LANES))
    return jnp.concatenate(rows, axis=0)


def _unpack_small(p, shapes, conv_rows):
    out, off = {}, 0
    for name, n in SMALL:
        if name not in shapes:
            continue
        n = conv_rows if n is None else n
        size = math.prod(shapes[name])
        out[name] = p[off:off + n].reshape(-1)[:size].reshape(shapes[name])
        off += n
    return out


def kernel(x, mem, g_mix, w_in, b_f, conv_w, conv_b, ln_g, ln_b, w_out, g_x, g_mem, w_mq, w_mkv, w_mo, g_ffn, w_gu, w_down, g_final, loss_target, m_g_mix, m_w_in, m_b_f, m_conv_w, m_conv_b, m_ln_g, m_ln_b, m_w_out, m_g_x, m_g_mem, m_w_mq, m_w_mkv, m_w_mo, m_g_ffn, m_w_gu, m_w_down, m_g_final, v_g_mix, v_w_in, v_b_f, v_conv_w, v_conv_b, v_ln_g, v_ln_b, v_w_out, v_g_x, v_g_mem, v_w_mq, v_w_mkv, v_w_mo, v_g_ffn, v_w_gu, v_w_down, v_g_final):
    names = ["g_mix", "w_in", "b_f", "conv_w", "conv_b", "ln_g", "ln_b", "w_out", "g_x", "g_mem", "w_mq", "w_mkv",
             "w_mo", "g_ffn", "w_gu", "w_down", "g_final"]
    W = dict(zip(names, [g_mix, w_in, b_f, conv_w, conv_b, ln_g, ln_b, w_out, g_x, g_mem, w_mq, w_mkv, w_mo, g_ffn,
                         w_gu, w_down, g_final]))
    M = dict(zip(names, [m_g_mix, m_w_in, m_b_f, m_conv_w, m_conv_b, m_ln_g, m_ln_b, m_w_out, m_g_x, m_g_mem, m_w_mq,
                         m_w_mkv, m_w_mo, m_g_ffn, m_w_gu, m_w_down, m_g_final]))
    V = dict(zip(names, [v_g_mix, v_w_in, v_b_f, v_conv_w, v_conv_b, v_ln_g, v_ln_b, v_w_out, v_g_x, v_g_mem, v_w_mq,
                         v_w_mkv, v_w_mo, v_g_ffn, v_w_gu, v_w_down, v_g_final]))
    big_names = [n for n, _, _, _ in BIG]
    B, S, _ = x.shape
    T = B * S
    mx, my, mc = _pos()
    chip = 2 * mx + my
    pos = jnp.stack([mc, chip]).astype(jnp.int32)

    shard2d = lambda a: a.reshape(a.shape[-2], a.shape[-1])
    big2d = lambda d, n: shard2d(d[n]).T if n == "w_in" else shard2d(d[n])
    shard_bf = {n: big2d(W, n).astype(BF16) for n in big_names}
    ag_mid = ["w_mkv", "w_out", "w_mq", "w_mo"]
    ag_ffn = ["w_gu", "w_down"]
    cw_mine = jnp.pad(shard2d(conv_w), ((0, 1), (0, 0)))
    w_in_slab, cw_slab = _run_comm(_ag_comm([shard_bf["w_in"], cw_mine]), "ag_w_in")
    slab = {"w_in": w_in_slab}
    w_int = w_in_slab.reshape(D_IN, D)
    w_ft = jnp.pad(w_int[OFF_F:D_IN], ((0, D_IN_PAD - D_IN), (0, 0)))
    cw = jnp.transpose(cw_slab, (1, 0, 2)).reshape(CONV_HALO, CONV_CH)

    row = lambda a: a.reshape(1, -1)
    bf_pad = jnp.pad(row(b_f), ((0, 0), (0, LANES - 8)))
    x2d = x.reshape(T, D)
    mem2d = mem.reshape(B * MEM_LEN, D)
    tgt = loss_target.reshape(T, D)

    (h, u, gt, q, k, v, zf, c, cq, qx, kx, qT), got = _fwd_in(x2d, row(g_mix), w_int, w_ft, bf_pad, B, S,
                                                  comm=_ag_comm([shard_bf[n] for n in ag_mid[:2]]))
    slab.update(zip(ag_mid[:2], got))
    ckT = jnp.transpose(c.reshape(B, S, LANES)[:, :, :8], (0, 2, 1)).reshape(B, N_PAIR, 2, S)
    ckT = jnp.pad(ckT, ((0, 0), (0, 0), (0, 6), (0, 0)))
    (y, co), got = _conv_fwd(u, gt, cw, row(conv_b), row(ln_g), row(ln_b), B, S,
                             comm=_ag_comm([shard_bf[n] for n in ag_mid[2:]]))
    slab.update(zip(ag_mid[2:], got))
    (o, fox_bias), got = _fox_fwd(qx, kx, v, cq, B, S, comm=_ag_comm([shard_bf[n] for n in ag_ffn]))
    slab.update(zip(ag_ffn, got))
    full = {n: slab[n] if by_col else slab[n].reshape(4 * r, c) for n, r, c, by_col in BIG}
    mn, km, vm = _mem_kv(mem2d, row(g_mem), full["w_mkv"], B)
    (x1, hx, qm, om, x2, cat), _ = _fwd_mid(x2d, co, o, km, vm, full["w_out"], full["w_mq"], full["w_mo"], row(g_x), B, S)
    hf, gu, act, dx3, loss_p, dg_final = _fwd_ffn(x2, tgt, full["w_gu"], full["w_down"], row(g_ffn), row(g_final), T)

    pos_sum = lambda gs, rcvs, ns: _chip_sum(gs, rcvs, pos, "rs_chip_sum_" + ns[0])
    fin_sum = lambda gs, rcvs, rcs, ns: _final_sum(gs, rcvs, rcs, pos, "rs_final_sum_" + ns[0])
    RH = {}
    dgu, dx2, dg_ffn = _bwd_ffn(dx3, gu, x2, full["w_gu"], full["w_down"], row(g_ffn), T)
    g_ffn_w = [_dw(hf, dgu, "dw_gu", D_FF, slabs=FF_CHUNK, tk=512), _dw(act, dx3, "dw_down", 512).reshape(4, D_FF // 4, D)]
    (dx1, dqm, dco, do, dd, dkm, dvm, dg_x, doT), rcv_ffn = _bwd_mid(dx2, x1, qm, km, vm, o, full["w_mo"], full["w_mq"],
                                                                full["w_out"], row(g_x), B, S, comm=_sibling_comm(g_ffn_w))
    pb_ffn = pos_sum(g_ffn_w, rcv_ffn, ag_ffn)
    dkv, dg_mem = _mem_bwd(dkm, dvm, mem2d, full["w_mkv"], row(g_mem), B)
    g_mid_w = [_dw(mn, dkv, "dw_mkv", 512, slabs=512), _dw(cat, dx1, "dw_out", 512).reshape(4, 256, D),
               _dw(hx, dqm, "dw_mq", 512).reshape(4, 256, D), _dw(om, dx2, "dw_mo", 512).reshape(4, 256, D)]
    (dq, dk, dv, dc, dcq), got = _fox_bwd(q, k, v, do, fox_bias, dd, ckT, qT, doT, B, S,
                                          comm=_join(_ici_comm(pb_ffn), _sibling_comm(g_mid_w)))
    rc_ffn, rcv_mid = got[:len(pb_ffn)], got[len(pb_ffn):]
    RH.update(zip(ag_ffn, fin_sum(g_ffn_w, rcv_ffn, rc_ffn, ag_ffn)))
    pb_mid = pos_sum(g_mid_w, rcv_mid, ag_mid)
    dc8 = jnp.transpose(dc[:, :, :2, :].reshape(B, 8, S), (0, 2, 1)).reshape(T, 8)
    dc8 = dc8 + dcq.reshape(T, 8, HEAD_D)[:, :, 0]
    dzf, dbf = _fgate_bwd(jnp.pad(dc8, ((0, 0), (0, LANES - 8))), zf, B, S)
    (du, dgt, dcw, dvec), rc_mid = _conv_bwd(dco, y, u, gt, cw, row(ln_g), row(ln_b), B, S, comm=_ici_comm(pb_mid))
    RH.update(zip(ag_mid, fin_sum(g_mid_w, rcv_mid, rc_mid, ag_mid)))
    dz = jnp.concatenate([du, dgt, dq, dk, dv, dzf], axis=1)
    g_in_w = [_dw(dz, h, "dw_in", 512, tk=D_IN_PAD // 3, rows=D_IN).reshape(4, D_IN // 4, D)]
    rcv_in = _run_comm(_sibling_comm(g_in_w), "rs_sibling_in")
    (grad_x, dg_mix), rc_in = _bwd_in(dz, w_int, w_ft, x2d, dx1, row(g_mix), T,
                                      comm=_ici_comm(pos_sum(g_in_w, rcv_in, ["w_in"])))
    RH.update(zip(["w_in"], fin_sum(g_in_w, rcv_in, rc_in, ["w_in"])))

    small_g = {"g_mix": dg_mix, "b_f": dbf[:, :8], "conv_w": dcw, "conv_b": dvec[0], "ln_g": dvec[1], "ln_b": dvec[2],
               "g_x": dg_x, "g_mem": dg_mem, "g_ffn": dg_ffn, "g_final": dg_final, "loss": loss_p[:, :1]}
    sg, filled = _small_allreduce(_pack_small(small_g, CONV_HALO * 4), "allreduce_small", [RH[n] for n in big_names])
    shared = dict(zip(big_names, filled))
    stepped = _adamw([big2d(W, n) for n in big_names], [shared[n] for n in big_names], [big2d(M, n) for n in big_names],
                     [big2d(V, n) for n in big_names], "adamw_big")
    G, DL, NM, NV = (dict(zip(big_names, col)) for col in zip(*stepped))
    shapes = {n: W[n].shape for n in names if n not in big_names}
    shapes["conv_w"] = (CONV_HALO, CONV_CH)
    shapes["loss"] = (1,)
    sgrads = _unpack_small(sg, shapes, CONV_HALO * 4)
    loss = sgrads.pop("loss")[0]
    sgrads["conv_w"] = lax.dynamic_slice(sgrads["conv_w"], (0, chip * LANES), (CONV_K, LANES)).reshape(W["conv_w"].shape)
    spack = lambda d: _pack_small({n: d[n] for n in sgrads}, CONV_HALO)
    (_, sd, snm, snv), = _adamw([spack(W)], [spack(sgrads)], [spack(M)], [spack(V)], "adamw_small", parts=1)
    sshapes = {n: W[n].shape for n in sgrads}
    SD, SNM, SNV = (_unpack_small(a, sshapes, CONV_HALO) for a in (sd, snm, snv))

    def collect(bigs, smalls):
        back = lambda n: (bigs[n].T if n == "w_in" else bigs[n]).reshape(W[n].shape)
        return [back(n) if n in big_names else smalls[n] for n in names]

    return (loss, grad_x.reshape(x.shape), *collect(G, sgrads), *collect(DL, SD), *collect(NM, SNM), *collect(NV, SNV))
```

```python
import functools
import math

import jax
import jax.numpy as jnp
from jax import lax
from jax.experimental import pallas as pl
from jax.experimental.pallas import tpu as pltpu

F32, BF16 = jnp.float32, jnp.bfloat16
MESH = pl.DeviceIdType.MESH

D = 1024
CONV_CH = 512
CONV_K = 31
CONV_HALO = 32
FOX_W = 512
HEAD_D = 64
N_PAIR = 4
MEM_LEN = 256
MEM_HEADS = 4
MEM_HD = 256
D_FF = 2816
FF_CHUNK = 1408
D_IN = 2568
D_IN_PAD = 2688
OFF_F = 2560
EPS = 1e-6
LANES = 128

ADAM_LR, ADAM_B1, ADAM_B2, ADAM_EPS, ADAM_WD, ADAM_STEP = 0.001, 0.9, 0.999, 1e-08, 0.01, 10

VMEM_LIMIT = 60 * 1024 * 1024

BIG = (("w_out", 256, 1024, False), ("w_mq", 256, 1024, False), ("w_mkv", 1024, 512, True),
       ("w_mo", 256, 1024, False), ("w_gu", 1024, 1408, True), ("w_down", 704, 1024, False),
       ("w_in", 642, 1024, False))

ANY = pl.BlockSpec(memory_space=pl.ANY)


def _sig(x):
    return 1.0 / (1.0 + jnp.exp(-x))


def _dot(a, b):
    return jnp.dot(a, b, preferred_element_type=F32)


def _dot_nt(a, b):
    return lax.dot_general(a, b, (((1,), (1,)), ((), ())), preferred_element_type=F32)


def _dot_tn(a, b):
    return lax.dot_general(a, b, (((0,), (0,)), ((), ())), preferred_element_type=F32)


def _split3(x):
    hi = x.astype(BF16)
    r = x - hi.astype(F32)
    mid = r.astype(BF16)
    return hi, mid, (r - mid.astype(F32)).astype(BF16)


def _dot_01(a, b):
    if a.dtype == jnp.bool_:
        return sum(_dot(a.astype(BF16), t) for t in _split3(b))
    return sum(_dot(t, b.astype(BF16)) for t in _split3(a))


def _resident(a):
    nd = a.ndim
    return pl.BlockSpec(a.shape, lambda *_: (0,) * nd, pipeline_mode=pl.Buffered(1))


def _acc_spec(shape):
    nd = len(shape)
    return pl.BlockSpec(shape, lambda *_: (0,) * nd)


def _params(n_grid):
    return pltpu.CompilerParams(dimension_semantics=("arbitrary",) * n_grid, vmem_limit_bytes=VMEM_LIMIT)


def _sds(shape, dtype):
    return jax.ShapeDtypeStruct(shape, dtype)


def _rms(x):
    r = lax.rsqrt(jnp.mean(x * x, axis=-1, keepdims=True) + EPS)
    return r, x * r


def _rms_bwd(dy, xh, r, g):
    dxh = dy * g
    dx = r * (dxh - xh * jnp.mean(dxh * xh, axis=-1, keepdims=True))
    return dx, dy * xh


def _head_expand(rows, cols):
    hd = lax.broadcasted_iota(jnp.int32, (rows, cols), 1) // HEAD_D
    hr = lax.broadcasted_iota(jnp.int32, (rows, cols), 0)
    return hd == hr


def _feat_major_spec(TB, FOX_T, nb):
    return pl.BlockSpec((1, TB // FOX_T, FOX_W, FOX_T), lambda b, j: (b, j, 0, 0))


def _fwd_in(x2, g_mix, w_int, w_ft, bf_pad, B, S, comm=None):
    T = B * S
    TB = min(512, S)
    nb = S // TB
    FOX_T = min(256, S)

    def body(x_ref, g_ref, w_ref, wf_ref, bf_ref, h_ref, u_ref, gt_ref, q_ref, k_ref, v_ref, zf_ref, c_ref, cq_ref,
             qx_ref, kx_ref, qt_ref, carry):
        j = pl.program_id(1)

        @pl.when(j == 0)
        def _():
            carry[...] = jnp.zeros_like(carry)

        _, xh = _rms(x_ref[...])
        h = (xh * g_ref[...]).astype(BF16)
        h_ref[...] = h
        u_ref[...] = _dot_nt(h, w_ref[0:512, :])
        gt_ref[...] = _dot_nt(h, w_ref[512:1024, :])
        qf = _dot_nt(h, w_ref[1024:1536, :])
        qb = qf.astype(BF16)
        kb = _dot_nt(h, w_ref[1536:2048, :]).astype(BF16)
        q_ref[...] = qb
        k_ref[...] = kb
        for t in range(TB // FOX_T):
            qt_ref[0, t] = qf[t * FOX_T:(t + 1) * FOX_T, :].T.astype(BF16)
        v_ref[...] = _dot_nt(h, w_ref[2048:2560, :]).astype(BF16)
        zf = _dot_nt(h, wf_ref[...]) + bf_ref[...]
        zf_ref[...] = zf
        lane = lax.broadcasted_iota(jnp.int32, zf.shape, 1)
        logf = jnp.where(lane < 8, jnp.minimum(zf, 0.0) - jnp.log(1.0 + jnp.exp(-jnp.abs(zf))), 0.0)
        row = lax.broadcasted_iota(jnp.int32, (TB, TB), 0)
        col = lax.broadcasted_iota(jnp.int32, (TB, TB), 1)
        c = _dot_01(row >= col, logf) + carry[0:1, :]
        carry[0:1, :] = c[TB - 1:TB, :]
        c_ref[...] = c
        cq = _dot_01(c, _head_expand(LANES, FOX_W))
        cq_ref[...] = cq
        hl = lax.broadcasted_iota(jnp.int32, (TB, LANES), 1)
        for hd in range(2 * N_PAIR):
            grp = slice((hd // 2) * LANES, (hd // 2 + 1) * LANES)
            swap = (lambda t: t) if hd % 2 == 0 else (lambda t: pltpu.roll(t, HEAD_D, 1))
            qf = swap(qb[:, grp].astype(F32) * (1.0 / math.sqrt(HEAD_D)))
            kf = swap(kb[:, grp].astype(F32))
            cv = cq[:, grp] if hd % 2 == 1 else pltpu.roll(cq[:, grp], HEAD_D, 1)
            hi = cv.astype(BF16).astype(F32)
            mid = (cv - hi).astype(BF16).astype(F32)
            lo = (cv - hi - mid).astype(BF16).astype(F32)
            pick = lambda a, b, c3, one_from, one_to: jnp.where(hl == a[0], a[1], jnp.where(hl == b[0], b[1], jnp.where(
                hl == c3[0], c3[1], jnp.where((hl >= one_from) & (hl < one_to), 1.0, 0.0))))
            qx = jnp.where(hl < HEAD_D, qf, pick((67, hi), (68, mid), (69, lo), 64, 67))
            kx = jnp.where(hl < HEAD_D, kf, pick((64, -hi), (65, -mid), (66, -lo), 67, 70))
            qx_ref[:, hd * LANES:(hd + 1) * LANES] = qx.astype(BF16)
            kx_ref[:, hd * LANES:(hd + 1) * LANES] = kx.astype(BF16)

    tok = lambda w: pl.BlockSpec((TB, w), lambda b, j: (b * nb + j, 0))
    outs = [(D, BF16), (512, F32), (512, F32), (512, BF16), (512, BF16), (512, BF16), (LANES, F32),
            (LANES, F32), (FOX_W, F32), (2 * FOX_W, BF16), (2 * FOX_W, BF16)]
    return _call(
        body, comm, name="fwd_in", grid=(B, nb),
        in_specs=[tok(D), _resident(g_mix), _resident(w_int), _resident(w_ft), _resident(bf_pad)],
        out_specs=[tok(w) for w, _ in outs] + [_feat_major_spec(TB, FOX_T, nb)],
        out_shape=[_sds((T, w), dt) for w, dt in outs] + [_sds((B, S // FOX_T, FOX_W, FOX_T), BF16)],
        scratch_shapes=[pltpu.VMEM((8, LANES), F32)],
        args=(x2, g_mix, w_int, w_ft, bf_pad))


def _head_sum(n):
    hc = lax.broadcasted_iota(jnp.int32, (n, n), 1) // HEAD_D
    hr = lax.broadcasted_iota(jnp.int32, (n, n), 0) // HEAD_D
    return hc == hr


def _layernorm_silu(y, lg, lb):
    mu = jnp.mean(y, axis=-1, keepdims=True)
    yc = y - mu
    rs = lax.rsqrt(jnp.mean(yc * yc, axis=-1, keepdims=True) + EPS)
    n = yc * rs
    l = n * lg + lb
    return rs, n, l


SUB = 8


def _shifted_copies(cat, sh, rows):
    for r in range(1, SUB):
        sh[r, 0:rows, :] = cat[r:r + rows, :]


def _tap(cat, sh, off, rows, cols=slice(None)):
    r = off % SUB
    return cat[off:off + rows, cols] if r == 0 else sh[r, off - r:off - r + rows, cols]


CONV_ROWS = 128


def _conv_pieces(CB):
    rows = min(CONV_ROWS, CB)
    return [(r0, rows, slice(c0, c0 + LANES)) for c0 in range(0, CONV_CH, LANES) for r0 in range(0, CB, rows)]


def _conv_fwd(u, gt, cw, cb, lng, lnb, B, S, comm=None):
    T = B * S
    CB = min(256, S)
    nb = S // CB

    def body(u_ref, gt_ref, w_ref, cb_ref, lg_ref, lb_ref, y_ref, co_ref, acat, ash):
        j = pl.program_id(1)

        @pl.when(j == 0)
        def _():
            acat[0:CONV_HALO, :] = jnp.zeros((CONV_HALO, CONV_CH), F32)

        acat[CONV_HALO:CONV_HALO + CB, :] = u_ref[...] * _sig(gt_ref[...])
        _shifted_copies(acat, ash, CB + CONV_HALO - SUB)
        for r0, rows, cs in _conv_pieces(CB):
            acc = jnp.zeros((rows, LANES), F32) + cb_ref[:, cs]
            for k in range(CONV_K):
                acc = acc + w_ref[k:k + 1, cs] * _tap(acat, ash, r0 + CONV_HALO - (CONV_K - 1) + k, rows, cs)
            y_ref[r0:r0 + rows, cs] = acc
        acat[0:CONV_HALO, :] = acat[CB:CB + CONV_HALO, :]
        _, _, l = _layernorm_silu(y_ref[...], lg_ref[...], lb_ref[...])
        co_ref[...] = (l * _sig(l)).astype(BF16)

    tok = lambda w: pl.BlockSpec((CB, w), lambda b, j: (b * nb + j, 0))
    return _call(
        body, comm, name="conv_fwd", grid=(B, nb),
        in_specs=[tok(CONV_CH), tok(CONV_CH), _resident(cw), _resident(cb), _resident(lng), _resident(lnb)],
        out_specs=[tok(CONV_CH), tok(CONV_CH)],
        out_shape=[_sds((T, CONV_CH), F32), _sds((T, CONV_CH), BF16)],
        scratch_shapes=[pltpu.VMEM((CONV_HALO + CB, CONV_CH), F32),
                        pltpu.VMEM((SUB, CB + CONV_HALO - SUB, CONV_CH), F32)],
        args=(u, gt, cw, cb, lng, lnb))


def _fox_fwd(qx, kx, v, cq, B, S, comm=None):
    T = B * S
    TQ = min(256, S)
    nq = S // TQ

    def body(qa_ref, qb_ref, ka_ref, kb_ref, v_ref, cq_ref, o_ref, lse_ref, s_scr, s_odd, m_scr, acc_scr):
        i = pl.program_id(2)
        lane = lax.broadcasted_iota(jnp.int32, (TQ, LANES), 1)
        lo = lane < HEAD_D
        qh = (qa_ref[...], qb_ref[...])
        kh = (ka_ref, kb_ref)
        m_scr[...] = jnp.full(m_scr.shape, -1e30, F32)
        acc_scr[...] = jnp.zeros_like(acc_scr)
        row = lax.broadcasted_iota(jnp.int32, (TQ, TQ), 0)
        col = lax.broadcasted_iota(jnp.int32, (TQ, TQ), 1)
        wide = lambda x: jnp.concatenate([x, x], axis=1) if TQ == 2 * LANES else jnp.tile(x, (1, TQ // LANES))

        def scores(j, s_buf):
            start = pl.multiple_of(j * TQ, TQ)
            for h in range(2):
                s_buf[h] = _dot_nt(qh[h], kh[h][pl.ds(start, TQ), :])

        def softmax_step(j, s_buf, diagonal):
            start = pl.multiple_of(j * TQ, TQ)
            vj = v_ref[pl.ds(start, TQ), :]
            for h in range(2):
                def logits():
                    return jnp.where(col <= row, s_buf[h], -1e30) if diagonal else s_buf[h]

                m_old = m_scr[h]
                m_new = jnp.maximum(m_old, jnp.max(logits(), axis=-1, keepdims=True))
                alpha = jnp.exp(m_old - m_new)
                m_scr[h] = m_new
                p = jnp.exp(logits() - wide(m_new)).astype(BF16)
                vx = jnp.where(lo if h == 0 else ~lo, vj, jnp.ones_like(vj))
                acc_scr[h] = alpha * acc_scr[h] + _dot(p, vx)

        def two_blocks(jj, carry):
            j = 2 * jj
            scores(j + 1, s_odd)
            softmax_step(j, s_scr, False)
            scores(j + 2, s_scr)
            softmax_step(j + 1, s_odd, False)
            return carry

        scores(0, s_scr)
        lax.fori_loop(0, i // 2, two_blocks, 0)

        @pl.when(i % 2 == 0)
        def _():
            softmax_step(i, s_scr, True)

        @pl.when(i % 2 == 1)
        def _():
            scores(i, s_odd)
            softmax_step(i - 1, s_scr, False)
            softmax_step(i, s_odd, True)

        acc = jnp.where(lo, acc_scr[0], acc_scr[1])
        den = pltpu.roll(jnp.where(lo, acc_scr[1], acc_scr[0]), HEAD_D, 1)
        o_ref[...] = acc / den
        lse_ref[...] = cq_ref[...] - (jnp.where(lo, m_scr[0], m_scr[1]) + jnp.log(den))

    qspec = pl.BlockSpec((TQ, LANES), lambda b, p, i: (b * nq + i, p))
    kspec = pl.BlockSpec((S, LANES), lambda b, p, i: (b, p))
    qhead = lambda h: pl.BlockSpec((TQ, LANES), lambda b, p, i: (b * nq + i, 2 * p + h))
    khead = lambda h: pl.BlockSpec((S, LANES), lambda b, p, i: (b, 2 * p + h))
    return _call(
        body, comm, name="fox_fwd", grid=(B, N_PAIR, nq),
        in_specs=[qhead(0), qhead(1), khead(0), khead(1), kspec, qspec],
        out_specs=[qspec, qspec],
        out_shape=[_sds((T, FOX_W), F32), _sds((T, FOX_W), F32)],
        scratch_shapes=[pltpu.VMEM((2, TQ, TQ), F32), pltpu.VMEM((2, TQ, TQ), F32),
                        pltpu.VMEM((2, TQ, LANES), F32), pltpu.VMEM((2, TQ, LANES), F32)],
        args=(qx, qx, kx, kx, v, cq))


def _mem_kv(mem2, g_mem, w_mkv, B):
    def body(m_ref, g_ref, w_ref, mn_ref, km_ref, vm_ref):
        _, xh = _rms(m_ref[...])
        mn = (xh * g_ref[...]).astype(BF16)
        mn_ref[...] = mn
        for s in range(2):
            km_ref[:, 512 * s:512 * (s + 1)] = _dot(mn, w_ref[s]).astype(BF16)
            vm_ref[:, 512 * s:512 * (s + 1)] = _dot(mn, w_ref[2 + s]).astype(BF16)

    blk = pl.BlockSpec((MEM_LEN, D), lambda b: (b, 0))
    return pl.pallas_call(
        body, name="mem_kv", grid=(B,),
        in_specs=[blk, _resident(g_mem), _resident(w_mkv)],
        out_specs=[blk, blk, blk],
        out_shape=[_sds((B * MEM_LEN, D), BF16)] * 3,
        compiler_params=_params(1),
    )(mem2, g_mem, w_mkv)


def _mem_probs(qm, km):
    ps = []
    for h in range(MEM_HEADS):
        hs = slice(h * MEM_HD, (h + 1) * MEM_HD)
        lg = _dot_nt(qm[:, hs], km[:, hs]) * (1.0 / math.sqrt(MEM_HD))
        e = jnp.exp(lg - jnp.max(lg, axis=-1, keepdims=True))
        ps.append(e / jnp.sum(e, axis=-1, keepdims=True))
    return ps


def _fwd_mid(x2, co, o, km, vm, w_out, w_mq, w_mo, g_x, B, S, comm=None):
    T = B * S
    TB = min(512, S)
    nb = S // TB

    def body(x_ref, co_ref, o_ref, km_ref, vm_ref, wo_ref, wq_ref, wm_ref, g_ref,
             x1_ref, hx_ref, qm_ref, om_ref, x2_ref, cat_ref):
        cat_ref[:, 0:CONV_CH] = co_ref[...]
        cat_ref[:, CONV_CH:D] = o_ref[...].astype(BF16)
        x1 = x_ref[...] + _dot(cat_ref[...], wo_ref[...])
        x1_ref[...] = x1
        _, xh = _rms(x1)
        hx = (xh * g_ref[...]).astype(BF16)
        hx_ref[...] = hx
        qm = _dot(hx, wq_ref[...]).astype(BF16)
        qm_ref[...] = qm
        ps = _mem_probs(qm, km_ref[...])
        vmv = vm_ref[...]
        for h in range(MEM_HEADS):
            hs = slice(h * MEM_HD, (h + 1) * MEM_HD)
            om_ref[:, hs] = _dot(ps[h].astype(BF16), vmv[:, hs]).astype(BF16)
        x2_ref[...] = x1 + _dot(om_ref[...], wm_ref[...])

    tok = lambda w: pl.BlockSpec((TB, w), lambda b, j: (b * nb + j, 0))
    memb = pl.BlockSpec((MEM_LEN, D), lambda b, j: (b, 0))
    outs = [(D, F32), (D, BF16), (D, BF16), (D, BF16), (D, F32), (D, BF16)]
    return _call(
        body, comm, name="fwd_mid", grid=(B, nb),
        in_specs=[tok(D), tok(CONV_CH), tok(FOX_W), memb, memb, _resident(w_out), _resident(w_mq), _resident(w_mo),
                  _resident(g_x)],
        out_specs=[tok(w) for w, _ in outs],
        out_shape=[_sds((T, w), dt) for w, dt in outs],
        scratch_shapes=[],
        args=(x2, co, o, km, vm, w_out, w_mq, w_mo, g_x))


def _load_gate_up(wgu_hbm, wg, wu, sems):
    copies = [pltpu.make_async_copy(wgu_hbm.at[s], (wg if s < 2 else wu).at[:, pl.ds((s % 2) * FF_CHUNK, FF_CHUNK)],
                                    sems.at[s]) for s in range(4)]
    for cp in copies:
        cp.start()
    for cp in copies:
        cp.wait()


def _fwd_ffn(x2, tgt, w_gu, w_down, g_ffn, g_final, T):
    TB = min(256, T)
    nb = T // TB

    def body(x_ref, t_ref, wgu_ref, wd_ref, gf_ref, gl_ref, hf_ref, gu_ref, act_ref, dx3_ref, loss_ref, dgl_ref,
             wg, wu, sems):
        i = pl.program_id(0)

        @pl.when(i == 0)
        def _():
            _load_gate_up(wgu_ref, wg, wu, sems)
            loss_ref[...] = jnp.zeros_like(loss_ref)
            dgl_ref[...] = jnp.zeros_like(dgl_ref)

        x2v = x_ref[...]
        _, xh = _rms(x2v)
        hf = (xh * gf_ref[...]).astype(BF16)
        hf_ref[...] = hf
        g = _dot(hf, wg[...])
        u = _dot(hf, wu[...])
        gu_ref[:, 0:D_FF] = g
        gu_ref[:, D_FF:2 * D_FF] = u
        act = (g * _sig(g) * u).astype(BF16)
        act_ref[...] = act
        x3 = x2v + _dot(act, wd_ref[...])
        r3, xh3 = _rms(x3)
        gl = gl_ref[...]
        e = xh3 * gl - t_ref[...]
        loss_ref[...] += jnp.sum(e * e) * (0.5 / D)
        dy = e * (1.0 / D)
        dx3, dgl = _rms_bwd(dy, xh3, r3, gl)
        dx3_ref[...] = dx3
        dgl_ref[...] += jnp.sum(dgl, axis=0, keepdims=True)

    tok = lambda w: pl.BlockSpec((TB, w), lambda i: (i, 0))
    return pl.pallas_call(
        body, name="fwd_ffn", grid=(nb,),
        in_specs=[tok(D), tok(D), ANY, _resident(w_down), _resident(g_ffn), _resident(g_final)],
        out_specs=[tok(D), tok(2 * D_FF), tok(D_FF), tok(D), _acc_spec((1, LANES)), _acc_spec((1, D))],
        out_shape=[_sds((T, D), BF16), _sds((T, 2 * D_FF), F32), _sds((T, D_FF), BF16), _sds((T, D), F32),
                   _sds((1, LANES), F32), _sds((1, D), F32)],
        scratch_shapes=[pltpu.VMEM((D, D_FF), BF16), pltpu.VMEM((D, D_FF), BF16), pltpu.SemaphoreType.DMA((4,))],
        compiler_params=_params(1),
    )(x2, tgt, w_gu, w_down, g_ffn, g_final)


def _bwd_ffn(dx3, gu, x2, w_gu, w_down, g_ffn, T):
    TB = min(256, T)
    nb = T // TB

    def body(d_ref, gu_ref, x_ref, wgu_ref, wd_ref, gf_ref, dgu_ref, dx2_ref, dgf_ref, wg, wu, sems):
        i = pl.program_id(0)

        @pl.when(i == 0)
        def _():
            _load_gate_up(wgu_ref, wg, wu, sems)
            dgf_ref[...] = jnp.zeros_like(dgf_ref)

        dx3v = d_ref[...]
        db = dx3v.astype(BF16)
        dact = _dot_nt(db, wd_ref[...])
        g = gu_ref[:, 0:D_FF]
        u = gu_ref[:, D_FF:2 * D_FF]
        sg = _sig(g)
        dg = (dact * u * sg * (1.0 + g * (1.0 - sg))).astype(BF16)
        du = (dact * g * sg).astype(BF16)
        dgu_ref[:, 0:D_FF] = dg
        dgu_ref[:, D_FF:2 * D_FF] = du
        dhf = _dot_nt(dg, wg[...]) + _dot_nt(du, wu[...])
        r2, xh2 = _rms(x_ref[...])
        dx, dg_tok = _rms_bwd(dhf, xh2, r2, gf_ref[...])
        dx2_ref[...] = dx3v + dx
        dgf_ref[...] += jnp.sum(dg_tok, axis=0, keepdims=True)

    tok = lambda w: pl.BlockSpec((TB, w), lambda i: (i, 0))
    return pl.pallas_call(
        body, name="bwd_ffn", grid=(nb,),
        in_specs=[tok(D), tok(2 * D_FF), tok(D), ANY, _resident(w_down), _resident(g_ffn)],
        out_specs=[tok(2 * D_FF), tok(D), _acc_spec((1, D))],
        out_shape=[_sds((T, 2 * D_FF), BF16), _sds((T, D), F32), _sds((1, D), F32)],
        scratch_shapes=[pltpu.VMEM((D, D_FF), BF16), pltpu.VMEM((D, D_FF), BF16), pltpu.SemaphoreType.DMA((4,))],
        compiler_params=_params(1),
    )(dx3, gu, x2, w_gu, w_down, g_ffn)


def _bwd_mid(dx2, x1, qm, km, vm, o, w_mo, w_mq, w_out, g_x, B, S, comm=None):
    T = B * S
    TB = min(512, S)
    nb = S // TB
    FOX_T = min(256, S)
    inv = 1.0 / math.sqrt(MEM_HD)

    def body(d_ref, x1_ref, qm_ref, km_ref, vm_ref, o_ref, wm_ref, wq_ref, wo_ref, g_ref,
             dx1_ref, dqm_ref, dco_ref, do_ref, dd_ref, dkm_ref, dvm_ref, dgx_ref, dot_ref):
        b = pl.program_id(0)
        j = pl.program_id(1)

        @pl.when((b == 0) & (j == 0))
        def _():
            dgx_ref[...] = jnp.zeros_like(dgx_ref)

        @pl.when(j == 0)
        def _():
            dkm_ref[...] = jnp.zeros_like(dkm_ref)
            dvm_ref[...] = jnp.zeros_like(dvm_ref)

        dx2v = d_ref[...]
        dom = _dot_nt(dx2v.astype(BF16), wm_ref[...]).astype(BF16)
        qmv = qm_ref[...]
        kmv = km_ref[...]
        vmv = vm_ref[...]
        ps = _mem_probs(qmv, kmv)
        for h in range(MEM_HEADS):
            hs = slice(h * MEM_HD, (h + 1) * MEM_HD)
            p = ps[h]
            dp = _dot_nt(dom[:, hs], vmv[:, hs])
            ds = (p * (dp - jnp.sum(p * dp, axis=-1, keepdims=True))).astype(BF16)
            dqm_ref[:, hs] = (_dot(ds, kmv[:, hs]) * inv).astype(BF16)
            dkm_ref[:, hs] += _dot_tn(ds, qmv[:, hs]) * inv
            dvm_ref[:, hs] += _dot_tn(p.astype(BF16), dom[:, hs])
        dhx = _dot_nt(dqm_ref[...], wq_ref[...])
        r1, xh1 = _rms(x1_ref[...])
        dx, dg_tok = _rms_bwd(dhx, xh1, r1, g_ref[...])
        dx1 = dx2v + dx
        dx1_ref[...] = dx1
        dgx_ref[...] += jnp.sum(dg_tok, axis=0, keepdims=True)
        d1b = dx1.astype(BF16)
        dco_ref[...] = _dot_nt(d1b, wo_ref[0:CONV_CH, :])
        do = _dot_nt(d1b, wo_ref[CONV_CH:D, :])
        dob = do.astype(BF16)
        do_ref[...] = dob
        for t in range(TB // FOX_T):
            dot_ref[0, t] = do[t * FOX_T:(t + 1) * FOX_T, :].T.astype(BF16)
        dd_ref[...] = _dot_01(dob.astype(F32) * o_ref[...], _head_sum(FOX_W))

    tok = lambda w: pl.BlockSpec((TB, w), lambda b, j: (b * nb + j, 0))
    memb = pl.BlockSpec((MEM_LEN, D), lambda b, j: (b, 0))
    outs = [(D, F32), (D, BF16), (CONV_CH, F32), (FOX_W, BF16), (FOX_W, F32)]
    return _call(
        body, comm, name="bwd_mid", grid=(B, nb),
        in_specs=[tok(D), tok(D), tok(D), memb, memb, tok(FOX_W), _resident(w_mo), _resident(w_mq), _resident(w_out),
                  _resident(g_x)],
        out_specs=[tok(w) for w, _ in outs] + [memb, memb, _acc_spec((1, D)), _feat_major_spec(TB, FOX_T, nb)],
        out_shape=[_sds((T, w), dt) for w, dt in outs] + [_sds((B * MEM_LEN, D), F32)] * 2 + [_sds((1, D), F32)]
        + [_sds((B, S // FOX_T, FOX_W, FOX_T), BF16)],
        scratch_shapes=[],
        args=(dx2, x1, qm, km, vm, o, w_mo, w_mq, w_out, g_x))


def _mem_bwd(dkm, dvm, mem2, w_mkv, g_mem, B):
    def body(dk_ref, dv_ref, m_ref, w_ref, g_ref, dkv_ref, dg_ref):
        b = pl.program_id(0)

        @pl.when(b == 0)
        def _():
            dg_ref[...] = jnp.zeros_like(dg_ref)

        dk = dk_ref[...].astype(BF16)
        dv = dv_ref[...].astype(BF16)
        dkv_ref[:, 0:D] = dk
        dkv_ref[:, D:2 * D] = dv
        dmn = jnp.zeros((MEM_LEN, D), F32)
        for s in range(2):
            dmn = dmn + _dot_nt(dk[:, 512 * s:512 * (s + 1)], w_ref[s]) + _dot_nt(dv[:, 512 * s:512 * (s + 1)], w_ref[2 + s])
        _, xh = _rms(m_ref[...])
        dg_ref[...] += jnp.sum(dmn * xh, axis=0, keepdims=True)

    blk = pl.BlockSpec((MEM_LEN, D), lambda b: (b, 0))
    return pl.pallas_call(
        body, name="mem_bwd", grid=(B,),
        in_specs=[blk, blk, blk, _resident(w_mkv), _resident(g_mem)],
        out_specs=[pl.BlockSpec((MEM_LEN, 2 * D), lambda b: (b, 0)), _acc_spec((1, D))],
        out_shape=[_sds((B * MEM_LEN, 2 * D), BF16), _sds((1, D), F32)],
        compiler_params=_params(1),
    )(dkm, dvm, mem2, w_mkv, g_mem)


def _fox_bwd(q, k, v, do, bias, dd, ckT, qT, doT, B, S, comm=None):
    T = B * S
    TK = min(256, S)
    nk = S // TK
    scale = 1.0 / math.sqrt(HEAD_D)

    def body(q_ref, k_ref, v_ref, do_ref, bias_ref, dd_ref, ck_ref, qt_ref, dot_ref, dq_ref, dk_ref, dv_ref, dc_ref,
             dcq_ref, dq_acc, rs_acc, s_scr, dp_scr, s_odd, dp_odd, dk_acc, dv_acc, dc_acc):
        j = pl.program_id(2)

        @pl.when(j == 0)
        def _():
            dq_acc[...] = jnp.zeros_like(dq_acc)
            rs_acc[...] = jnp.zeros_like(rs_acc)

        dk_acc[...] = jnp.zeros_like(dk_acc)
        dv_acc[...] = jnp.zeros_like(dv_acc)
        dc_acc[...] = jnp.zeros_like(dc_acc)
        lane = lax.broadcasted_iota(jnp.int32, (TK, LANES), 1)
        lo = lane < HEAD_D
        ks = k_ref[...] * jnp.asarray(scale, BF16)
        v2 = v_ref[...]
        zero = jnp.zeros_like(ks)
        kh = (jnp.where(lo, ks, zero), jnp.where(lo, zero, ks))
        vh = (jnp.where(lo, v2, zero), jnp.where(lo, zero, v2))
        kstart = pl.multiple_of(j * TK, TK)
        ckh = tuple(ck_ref[0, 0, h:h + 1, pl.ds(kstart, TK)] for h in range(2))
        row = lax.broadcasted_iota(jnp.int32, (TK, TK), 0)
        col = lax.broadcasted_iota(jnp.int32, (TK, TK), 1)
        wide = lambda x: jnp.concatenate([x, x], axis=1) if TK == 2 * LANES else jnp.tile(x, (1, TK // LANES))

        def scores(i, s_buf, dp_buf):
            start = pl.multiple_of(i * TK, TK)
            qi = q_ref[pl.ds(start, TK), :]
            doi = do_ref[pl.ds(start, TK), :]
            for h in range(2):
                s_buf[h] = _dot_nt(qi, kh[h])
                dp_buf[h] = _dot_nt(doi, vh[h])

        def grads(i, s_buf, dp_buf, diagonal):
            start = pl.multiple_of(i * TK, TK)
            bias2 = bias_ref[pl.ds(start, TK), :]
            dd2 = dd_ref[pl.ds(start, TK), :]
            for h in range(2):
                hc = slice(h * HEAD_D, h * HEAD_D + 1)
                bias = jnp.broadcast_to(bias2[:, hc], (TK, LANES))
                ddh = jnp.broadcast_to(dd2[:, hc], (TK, LANES))
                p = jnp.exp((s_buf[h] - ckh[h]) + wide(bias))
                if diagonal:
                    p = jnp.where(col <= row, p, 0.0)
                ds = p * (dp_buf[h] - wide(ddh))
                dc_acc[h, 0:1, :] += jnp.sum(ds, axis=0, keepdims=True)
                rs_acc[h, pl.ds(start, TK), :] += jnp.sum(ds, axis=1, keepdims=True)
                pb = p.astype(BF16)
                dsb = ds.astype(BF16)
                feat = slice(h * HEAD_D, (h + 1) * HEAD_D)
                dv_acc[feat, :] += _dot(dot_ref[0, i, feat, :], pb)
                dk_acc[feat, :] += _dot(qt_ref[0, i, feat, :], dsb)
                dq_acc[pl.ds(start, TK), :] += _dot(dsb, kh[h])

        n_off = nk - 1 - j
        block = lambda t: jnp.where(t < n_off, j + 1 + t, j)

        def two_blocks(tt, carry):
            t = 2 * tt
            scores(block(t + 1), s_odd, dp_odd)
            grads(block(t), s_scr, dp_scr, False)
            scores(block(t + 2), s_scr, dp_scr)
            grads(block(t + 1), s_odd, dp_odd, False)
            return carry

        scores(block(0), s_scr, dp_scr)
        lax.fori_loop(0, n_off // 2, two_blocks, 0)

        @pl.when(n_off % 2 == 0)
        def _():
            grads(j, s_scr, dp_scr, True)

        @pl.when(n_off % 2 == 1)
        def _():
            scores(j, s_odd, dp_odd)
            grads(nk - 1, s_scr, dp_scr, False)
            grads(j, s_odd, dp_odd, True)

        dk_ref[...] = (dk_acc[...].T * scale).astype(BF16)
        dv_ref[...] = dv_acc[...].T.astype(BF16)
        sub = lax.broadcasted_iota(jnp.int32, (8, TK), 0)
        dca = dc_acc[0, 0:1, :]
        dcb = dc_acc[1, 0:1, :]
        dc_ref[0, 0] = jnp.where(sub == 0, -dca, jnp.where(sub == 1, -dcb, 0.0))

        @pl.when(j == nk - 1)
        def _():
            dq_ref[...] = dq_acc[...].astype(BF16)
            lo_s = lax.broadcasted_iota(jnp.int32, (S, LANES), 1) < HEAD_D
            dcq_ref[...] = jnp.where(lo_s, rs_acc[0], rs_acc[1])

    full = pl.BlockSpec((S, LANES), lambda b, p, j: (b, p))
    blk = pl.BlockSpec((TK, LANES), lambda b, p, j: (b * nk + j, p))
    featT = pl.BlockSpec((1, nk, LANES, TK), lambda b, p, j: (b, 0, p, 0))
    return _call(
        body, comm, name="fox_bwd", grid=(B, N_PAIR, nk),
        in_specs=[full, blk, blk, full, full, full, pl.BlockSpec((1, 1, 8, S), lambda b, p, j: (b, p, 0, 0)),
                  featT, featT],
        out_specs=[full, blk, blk, pl.BlockSpec((1, 1, 8, TK), lambda b, p, j: (b, p, 0, j)), full],
        out_shape=[_sds((T, FOX_W), BF16), _sds((T, FOX_W), BF16), _sds((T, FOX_W), BF16),
                   _sds((B, N_PAIR, 8, S), F32), _sds((T, FOX_W), F32)],
        scratch_shapes=[pltpu.VMEM((S, LANES), F32), pltpu.VMEM((2, S, 1), F32),
                        pltpu.VMEM((2, TK, TK), F32), pltpu.VMEM((2, TK, TK), F32),
                        pltpu.VMEM((2, TK, TK), F32), pltpu.VMEM((2, TK, TK), F32),
                        pltpu.VMEM((LANES, TK), F32), pltpu.VMEM((LANES, TK), F32), pltpu.VMEM((2, 8, TK), F32)],
        args=(q, k, v, do, bias, dd, ckT, qT, doT))


def _fgate_bwd(dc8, zf, B, S):
    T = B * S
    TB = min(512, S)
    nb = S // TB

    def body(dc_ref, zf_ref, dzf_ref, dbf_ref, carry):
        b = pl.program_id(0)
        j = pl.program_id(1)

        @pl.when((b == 0) & (j == 0))
        def _():
            dbf_ref[...] = jnp.zeros_like(dbf_ref)

        @pl.when(j == 0)
        def _():
            carry[...] = jnp.zeros_like(carry)

        dc = dc_ref[...]
        row = lax.broadcasted_iota(jnp.int32, (TB, TB), 0)
        col = lax.broadcasted_iota(jnp.int32, (TB, TB), 1)
        dlogf = _dot_01(col >= row, dc) + carry[0:1, :]
        carry[0:1, :] = dlogf[0:1, :]
        lane = lax.broadcasted_iota(jnp.int32, dc.shape, 1)
        dzf = jnp.where(lane < 8, dlogf * _sig(-zf_ref[...]), 0.0)
        dzf_ref[...] = dzf.astype(BF16)
        dbf_ref[...] += jnp.sum(dzf, axis=0, keepdims=True)

    tok = pl.BlockSpec((TB, LANES), lambda b, j: (b * nb + (nb - 1 - j), 0))
    return pl.pallas_call(
        body, name="fgate_bwd", grid=(B, nb),
        in_specs=[tok, tok],
        out_specs=[tok, _acc_spec((1, LANES))],
        out_shape=[_sds((T, LANES), BF16), _sds((1, LANES), F32)],
        scratch_shapes=[pltpu.VMEM((8, LANES), F32)],
        compiler_params=_params(2),
    )(dc8, zf)


def _conv_bwd(dco, y, u, gt, cw, lng, lnb, B, S, comm=None):
    T = B * S
    CB = min(256, S)
    nb = S // CB
    hb = CB // CONV_HALO

    def body(dco_ref, y_ref, u_ref, gt_ref, up_ref, gp_ref, w_ref, lg_ref, lb_ref,
             du_ref, dgt_ref, dw_ref, vec_ref, acat, dycat, ash, dysh):
        b = pl.program_id(0)
        j = pl.program_id(1)
        jr = nb - 1 - j

        @pl.when((b == 0) & (j == 0))
        def _():
            dw_ref[...] = jnp.zeros_like(dw_ref)
            vec_ref[...] = jnp.zeros_like(vec_ref)

        @pl.when(j == 0)
        def _():
            dycat[CB:CB + CONV_HALO, :] = jnp.zeros((CONV_HALO, CONV_CH), F32)

        lg = lg_ref[...]
        rs, n, l = _layernorm_silu(y_ref[...], lg, lb_ref[...])
        sg = _sig(l)
        dl = dco_ref[...] * (sg * (1.0 + l * (1.0 - sg)))
        dn = dl * lg
        dy = rs * (dn - jnp.mean(dn, axis=-1, keepdims=True) - n * jnp.mean(dn * n, axis=-1, keepdims=True))
        vec_ref[0:1, :] += jnp.sum(dy, axis=0, keepdims=True)
        vec_ref[1:2, :] += jnp.sum(dl * n, axis=0, keepdims=True)
        vec_ref[2:3, :] += jnp.sum(dl, axis=0, keepdims=True)
        dycat[0:CB, :] = dy
        acat[0:CONV_HALO, :] = jnp.where(jr > 0, up_ref[...] * _sig(gp_ref[...]), 0.0)
        acat[CONV_HALO:CONV_HALO + CB, :] = u_ref[...] * _sig(gt_ref[...])
        _shifted_copies(acat, ash, CB + CONV_HALO - SUB)
        _shifted_copies(dycat, dysh, CB + CONV_HALO - SUB)
        for r0, rows, cs in _conv_pieces(CB):
            dyp = dycat[r0:r0 + rows, cs]
            da = jnp.zeros((rows, LANES), F32)
            for k in range(CONV_K):
                da = da + w_ref[k:k + 1, cs] * _tap(dycat, dysh, r0 + CONV_K - 1 - k, rows, cs)
                dw_ref[k:k + 1, cs] += jnp.sum(dyp * _tap(acat, ash, r0 + CONV_HALO - (CONV_K - 1) + k, rows, cs),
                                               axis=0, keepdims=True)
            uv = u_ref[r0:r0 + rows, cs]
            sgt = _sig(gt_ref[r0:r0 + rows, cs])
            du_ref[r0:r0 + rows, cs] = (da * sgt).astype(BF16)
            dgt_ref[r0:r0 + rows, cs] = (da * uv * sgt * (1.0 - sgt)).astype(BF16)
        dycat[CB:CB + CONV_HALO, :] = dycat[0:CONV_HALO, :]

    tok = lambda w: pl.BlockSpec((CB, w), lambda b, j: (b * nb + (nb - 1 - j), 0))
    prev = pl.BlockSpec((CONV_HALO, CONV_CH), lambda b, j: (jnp.maximum((b * nb + (nb - 1 - j)) * hb - 1, 0), 0))
    return _call(
        body, comm, name="conv_bwd", grid=(B, nb),
        in_specs=[tok(CONV_CH), tok(CONV_CH), tok(CONV_CH), tok(CONV_CH), prev, prev, _resident(cw), _resident(lng),
                  _resident(lnb)],
        out_specs=[tok(CONV_CH), tok(CONV_CH), _acc_spec((CONV_HALO, CONV_CH)), _acc_spec((8, CONV_CH))],
        out_shape=[_sds((T, CONV_CH), BF16), _sds((T, CONV_CH), BF16), _sds((CONV_HALO, CONV_CH), F32),
                   _sds((8, CONV_CH), F32)],
        scratch_shapes=[pltpu.VMEM((CONV_HALO + CB, CONV_CH), F32), pltpu.VMEM((CB + CONV_HALO, CONV_CH), F32),
                        pltpu.VMEM((SUB, CB + CONV_HALO - SUB, CONV_CH), F32),
                        pltpu.VMEM((SUB, CB + CONV_HALO - SUB, CONV_CH), F32)],
        args=(dco, y, u, gt, u, gt, cw, lng, lnb))


def _bwd_in(dz, w_int, w_ft, x2, dx1, g_mix, T, comm=None):
    TB = min(512, T)
    nb = T // TB

    def body(dz_ref, w_ref, wf_ref, x_ref, d1_ref, g_ref, gx_ref, dg_ref):
        i = pl.program_id(0)

        @pl.when(i == 0)
        def _():
            dg_ref[...] = jnp.zeros_like(dg_ref)

        dh = _dot(dz_ref[:, 0:OFF_F], w_ref[0:OFF_F, :]) + _dot(dz_ref[:, OFF_F:D_IN_PAD], wf_ref[...])
        r0, xh0 = _rms(x_ref[...])
        dx, dg_tok = _rms_bwd(dh, xh0, r0, g_ref[...])
        gx_ref[...] = d1_ref[...] + dx
        dg_ref[...] += jnp.sum(dg_tok, axis=0, keepdims=True)

    tok = lambda w: pl.BlockSpec((TB, w), lambda i: (i, 0))
    return _call(
        body, comm, name="bwd_in", grid=(nb,),
        in_specs=[tok(D_IN_PAD), _resident(w_int), _resident(w_ft), tok(D), tok(D), _resident(g_mix)],
        out_specs=[tok(D), _acc_spec((1, D))],
        out_shape=[_sds((T, D), F32), _sds((1, D), F32)],
        scratch_shapes=[],
        args=(dz, w_int, w_ft, x2, dx1, g_mix))


def _dw(a, b, name, tn, slabs=0, tk=None, rows=None):
    T, K = a.shape
    N = b.shape[1]
    per = tn // slabs if slabs else 1
    tk = tk or (K if K <= 1024 else K // 2)
    tt = min(1024, T)
    nt = T // tt

    def body(a_ref, b_ref, o_ref, acc):
        t = pl.program_id(2)

        @pl.when(t == 0)
        def _():
            acc[...] = jnp.zeros_like(acc)

        acc[...] += _dot_tn(a_ref[...].astype(BF16), b_ref[...].astype(BF16))

        @pl.when(t == nt - 1)
        def _():
            if slabs:
                for sl in range(per):
                    o_ref[sl] = acc[:, sl * slabs:(sl + 1) * slabs]
            else:
                o_ref[...] = acc[...]

    return pl.pallas_call(
        body, name=name, grid=(K // tk, N // tn, nt),
        in_specs=[pl.BlockSpec((tt, tk), lambda i, j, t: (t, i)), pl.BlockSpec((tt, tn), lambda i, j, t: (t, j))],
        out_specs=(pl.BlockSpec((per, tk, slabs), lambda i, j, t: (j, i, 0)) if slabs
                   else pl.BlockSpec((tk, tn), lambda i, j, t: (i, j))),
        out_shape=_sds((N // slabs, K, slabs) if slabs else (rows or K, N), F32),
        scratch_shapes=[pltpu.VMEM((tk, tn), F32)],
        compiler_params=_params(3),
    )(a, b)


def _pos():
    return lax.axis_index("x"), lax.axis_index("y"), lax.axis_index("c")


def _remote(src, dst, ssem, rsem, to):
    return pltpu.make_async_remote_copy(src_ref=src, dst_ref=dst, send_sem=ssem, recv_sem=rsem, device_id=to,
                                        device_id_type=MESH)


def _split_axis(shape):
    return 0 if shape[0] % 32 == 0 else 1


def _half_shape(shape, parts=2):
    return (shape[0] // parts, shape[1]) if _split_axis(shape) == 0 else (shape[0], shape[1] // parts)


def _half(shape, c):
    R, C = shape
    if _split_axis(shape) == 0:
        return (pl.ds(pl.multiple_of(c * (R // 2), 16), R // 2), slice(None))
    return (slice(None), pl.ds(pl.multiple_of(c * (C // 2), LANES), C // 2))


def _half_block(shape, parts, lead, which):
    blk = _half_shape(shape, parts)
    idx = (which, 0) if _split_axis(shape) == 0 else (0, which)
    return blk, tuple(lead) + idx


class _Comm:
    def __init__(self, ins, out_shapes, sems, start, finish):
        self.ins, self.out_shapes, self.sems, self.start, self.finish = list(ins), list(out_shapes), list(sems), start, finish


def _ag_comm(shards):
    n = len(shards)

    def parts(ins, outs, sems):
        send_sems, recv_sems, local_sems = sems
        x, y, c = _pos()
        me, sib = (x, y, c), (x, y, 1 - c)
        chips = [(1 - x, y), (x, 1 - y), (1 - x, 1 - y)]

        def rows(w, px, py, pc):
            return outs[w].at[(2 * px + py,) + _half(shards[w].shape, pc)]

        def copy(w, k, block, to, src=None):
            return _remote(rows(w, *block) if src is None else src, rows(w, *block), send_sems.at[w, k],
                           recv_sems.at[w, k], to)

        mine, first = [], []
        for w in range(n):
            src = ins[w].at[_half(shards[w].shape, c)]
            mine.append(pltpu.make_async_copy(src, rows(w, *me), local_sems.at[w]))
            first += [copy(w, 0, me, sib, src=src)] + [copy(w, 1 + j, me, (*chip, c), src=src) for j, chip in enumerate(chips)]
        return c, me, sib, chips, copy, mine, first

    def start(ins, outs, sems):
        _, _, _, _, _, mine, first = parts(ins, outs, sems)
        for cp in mine + first:
            cp.start()

    def finish(ins, outs, sems):
        c, me, sib, chips, copy, mine, first = parts(ins, outs, sems)
        passed = []
        for w in range(n):
            for j, chip in enumerate(chips):
                copy(w, 1 + j, (*chip, c), me).wait_recv()
                passed.append(copy(w, 4 + j, (*chip, c), sib))
                passed[-1].start()
        for w in range(n):
            copy(w, 0, sib, me).wait_recv()
            for j, chip in enumerate(chips):
                copy(w, 4 + j, (*chip, 1 - c), me).wait_recv()
        for cp in first + passed:
            cp.wait_send()
        for cp in mine:
            cp.wait()

    D7 = pltpu.SemaphoreType.DMA((n, 7))
    return _Comm(shards, [_sds((4,) + s.shape, s.dtype) for s in shards], [D7, D7, pltpu.SemaphoreType.DMA((n,))],
                 start, finish)


def _sibling_comm(gs):
    n = len(gs)

    def copies(ins, outs, sems):
        send_sems, recv_sems = sems
        x, y, c = _pos()
        return [_remote(ins[w].at[(s,) + _half(gs[w].shape[1:], 1 - c)], outs[w].at[s], send_sems.at[w, s],
                        recv_sems.at[w, s], (x, y, 1 - c)) for w in range(n) for s in range(4)]

    def start(ins, outs, sems):
        for cp in copies(ins, outs, sems):
            cp.start()

    def finish(ins, outs, sems):
        for cp in copies(ins, outs, sems):
            cp.wait()

    D4 = pltpu.SemaphoreType.DMA((n, 4))
    return _Comm(gs, [_sds((4,) + _half_shape(g.shape[1:]), F32) for g in gs], [D4, D4], start, finish)


def _ici_comm(pbs):
    n = len(pbs)

    def copies(ins, outs, sems):
        send_sems, recv_sems = sems
        x, y, c = _pos()
        return [_remote(ins[w].at[2 * tx + ty], outs[w].at[j], send_sems.at[w, j], recv_sems.at[w, j], (tx, ty, c))
                for w in range(n) for j, (tx, ty) in enumerate([(1 - x, y), (x, 1 - y), (1 - x, 1 - y)])]

    def start(ins, outs, sems):
        for cp in copies(ins, outs, sems):
            cp.start()

    def finish(ins, outs, sems):
        for cp in copies(ins, outs, sems):
            cp.wait()

    D3 = pltpu.SemaphoreType.DMA((n, 3))
    return _Comm(pbs, [_sds((3,) + p.shape[1:], BF16) for p in pbs], [D3, D3], start, finish)


def _join(*comms):
    counts = [(len(c.ins), len(c.out_shapes), len(c.sems)) for c in comms]

    def each(which):
        def run(ins, outs, sems):
            i = o = k = 0
            for c, (ni, no, nk) in zip(comms, counts):
                getattr(c, which)(ins[i:i + ni], outs[o:o + no], sems[k:k + nk])
                i, o, k = i + ni, o + no, k + nk
        return run

    return _Comm(sum((c.ins for c in comms), []), sum((c.out_shapes for c in comms), []),
                 sum((c.sems for c in comms), []), each("start"), each("finish"))


def _run_comm(comm, name):
    ni, no = len(comm.ins), len(comm.out_shapes)

    def body(*refs):
        ins, outs, sems = refs[:ni], refs[ni:ni + no], refs[ni + no:]
        comm.start(ins, outs, sems)
        comm.finish(ins, outs, sems)

    return pl.pallas_call(body, name=name, out_shape=comm.out_shapes, in_specs=[ANY] * ni, out_specs=[ANY] * no,
                          scratch_shapes=comm.sems)(*comm.ins)


def _call(body, comm, *, name, grid, in_specs, out_specs, out_shape, scratch_shapes, args):
    n_grid = len(grid)
    if comm is None:
        res = pl.pallas_call(body, name=name, grid=grid, in_specs=in_specs, out_specs=out_specs, out_shape=out_shape,
                             scratch_shapes=scratch_shapes, compiler_params=_params(n_grid))(*args)
        return list(res), []
    n_in, n_out, n_scr = len(in_specs), len(out_specs), len(scratch_shapes)
    ni, no = len(comm.ins), len(comm.out_shapes)

    def carried(*refs):
        ins, refs = refs[:n_in], refs[n_in:]
        cins, refs = refs[:ni], refs[ni:]
        outs, refs = refs[:n_out], refs[n_out:]
        couts, refs = refs[:no], refs[no:]
        scr, csems = refs[:n_scr], refs[n_scr:]
        ids = [pl.program_id(ax) for ax in range(n_grid)]
        first = functools.reduce(jnp.logical_and, [i == 0 for i in ids])
        last = functools.reduce(jnp.logical_and, [i == g - 1 for i, g in zip(ids, grid)])

        @pl.when(first)
        def _():
            comm.start(cins, couts, csems)

        body(*ins, *outs, *scr)

        @pl.when(last)
        def _():
            comm.finish(cins, couts, csems)

    res = pl.pallas_call(
        carried, name=name, grid=grid, in_specs=list(in_specs) + [ANY] * ni, out_specs=list(out_specs) + [ANY] * no,
        out_shape=list(out_shape) + comm.out_shapes, scratch_shapes=list(scratch_shapes) + comm.sems,
        compiler_params=_params(n_grid))(*args, *comm.ins)
    return list(res[:n_out]), list(res[n_out:])


def _small_allreduce(v, name, halves=()):
    P = v.shape[0]
    n = len(halves)
    vm = pl.BlockSpec(memory_space=pltpu.VMEM)

    def body(v_ref, *refs):
        o_ref, outs = refs[n], refs[n + 1:2 * n + 1]
        gath, send_sems, recv_sems, half_send, half_recv = refs[2 * n + 1:]
        x, y, c = _pos()
        me = 4 * x + 2 * y + c
        gath[me] = v_ref[...]
        cps = []
        for r in range(1, 8):
            tx = (1 - x) if r & 4 else x
            ty = (1 - y) if r & 2 else y
            tc = (1 - c) if r & 1 else c
            cps.append(_remote(v_ref, gath.at[me], send_sems.at[r - 1], recv_sems.at[r - 1], (tx, ty, tc)))
        for w in range(n):
            mine = outs[w].at[_half(halves[w].shape, c)]
            cps.append(_remote(mine, mine, half_send.at[w], half_recv.at[w], (x, y, 1 - c)))
        for cp in cps:
            cp.start()
        for cp in cps:
            cp.wait()
        acc = gath[0]
        for d in range(1, 8):
            acc = acc + gath[d]
        o_ref[...] = acc

    res = pl.pallas_call(
        body, name=name, out_shape=[_sds((P, LANES), F32)] + [_sds(g.shape, F32) for g in halves],
        in_specs=[vm] + [ANY] * n, out_specs=[vm] + [ANY] * n, input_output_aliases={1 + w: 1 + w for w in range(n)},
        scratch_shapes=[pltpu.VMEM((8, P, LANES), F32), pltpu.SemaphoreType.DMA((7,)), pltpu.SemaphoreType.DMA((7,)),
                        pltpu.SemaphoreType.DMA((max(n, 1),)), pltpu.SemaphoreType.DMA((max(n, 1),))],
    )(v, *halves)
    return res[0], list(res[1:])


def _chip_sum(gs, rcvs, pos, name):
    n = len(gs)
    shards = [g.shape[1:] for g in gs]
    hss = [_half_shape(sh) for sh in shards]
    other = lambda i, pos: (pos[1] + 1 + i) % 4

    def body(pos_ref, *refs):
        for w in range(n):
            refs[2 * n + w][...] = (refs[w][...] + refs[n + w][...]).astype(BF16)

    mine = lambda w: pl.BlockSpec((1,) + hss[w], lambda i, pos: _half_block(shards[w], 2, (other(i, pos),), pos[0])[1])
    whole = lambda w: pl.BlockSpec((1,) + hss[w], lambda i, pos: (other(i, pos), 0, 0))
    return pl.pallas_call(
        body, name=name, out_shape=[_sds((4,) + hs, BF16) for hs in hss],
        grid_spec=pltpu.PrefetchScalarGridSpec(
            num_scalar_prefetch=1, grid=(3,),
            in_specs=[mine(w) for w in range(n)] + [whole(w) for w in range(n)],
            out_specs=[whole(w) for w in range(n)]),
        compiler_params=_params(1),
    )(pos, *gs, *rcvs)


def _final_sum(gs, rcvs, rcs, pos, name):
    n = len(gs)
    shards = [g.shape[1:] for g in gs]
    qss = [_half_shape(sh, 4) for sh in shards]

    def body(pos_ref, *refs):
        for w in range(n):
            acc = refs[w][0] + refs[n + w][0]
            for j in range(3):
                acc = acc + refs[2 * n + w][j].astype(F32)
            refs[3 * n + w][...] = acc

    def spec(w, lead_block, lead_index, mine):
        return pl.BlockSpec(lead_block + qss[w], lambda i, pos: _half_block(
            shards[w], 4, lead_index(pos), pos[0] * 2 + i if mine else i)[1])

    own_slab, first, none = (lambda pos: (pos[1],)), (lambda pos: (0,)), (lambda pos: ())
    return pl.pallas_call(
        body, name=name, out_shape=[_sds(sh, F32) for sh in shards],
        grid_spec=pltpu.PrefetchScalarGridSpec(
            num_scalar_prefetch=1, grid=(2,),
            in_specs=[spec(w, (1,), own_slab, True) for w in range(n)] + [spec(w, (1,), own_slab, False) for w in range(n)]
            + [spec(w, (3,), first, False) for w in range(n)],
            out_specs=[spec(w, (), none, True) for w in range(n)]),
        compiler_params=_params(1),
    )(pos, *gs, *rcvs, *rcs)


def _adamw_math(w, g, m, v):
    m = ADAM_B1 * m + (1.0 - ADAM_B1) * g
    v = ADAM_B2 * v + (1.0 - ADAM_B2) * (g * g)
    m_hat = m / (1.0 - ADAM_B1 ** ADAM_STEP)
    v_hat = v / (1.0 - ADAM_B2 ** ADAM_STEP)
    delta = -ADAM_LR * (m_hat / (jnp.sqrt(v_hat) + ADAM_EPS) + ADAM_WD * w)
    return delta, m, v


ADAM_PARTS = 8


def _adamw(ws, gs, ms, vs, name, parts=ADAM_PARTS):
    n = len(ws)
    shapes = [w.shape for w in ws]

    def body(*refs):
        for k in range(n):
            w_ref, g_ref, m_ref, v_ref = (refs[j * n + k] for j in range(4))
            go_ref, d_ref, nm_ref, nv_ref = (refs[(4 + j) * n + k] for j in range(4))
            g = g_ref[...]
            d, nm, nv = _adamw_math(w_ref[...], g, m_ref[...], v_ref[...])
            go_ref[...] = g
            d_ref[...] = d
            nm_ref[...] = nm
            nv_ref[...] = nv

    blk = lambda k: pl.BlockSpec(_half_shape(shapes[k], parts), lambda i: _half_block(shapes[k], parts, (), i)[1])
    res = pl.pallas_call(
        body, name=name, grid=(parts,), in_specs=[blk(k) for k in range(n)] * 4, out_specs=[blk(k) for k in range(n)] * 4,
        out_shape=[_sds(sh, F32) for sh in shapes] * 4, compiler_params=_params(1),
    )(*ws, *gs, *ms, *vs)
    return [tuple(res[j * n + k] for j in range(4)) for k in range(n)]


SMALL = (("g_mix", 8), ("b_f", 8), ("conv_w", None), ("conv_b", 8), ("ln_g", 8), ("ln_b", 8), ("g_x", 8), ("g_mem", 8),
         ("g_ffn", 8), ("g_final", 8), ("loss", 8))


def _pack_small(parts, conv_rows):
    rows = []
    for name, n in SMALL:
        if name not in parts:
            continue
        n = conv_rows if n is None else n
        flat = parts[name].reshape(-1).astype(F32)
        flat = jnp.pad(flat, (0, n * LANES - flat.shape[0]))
        rows.append(flat.reshape(n, LANES))
    return jnp.concatenate(rows, axis=0)


def _unpack_small(p, shapes, conv_rows):
    out, off = {}, 0
    for name, n in SMALL:
        if name not in shapes:
            continue
        n = conv_rows if n is None else n
        size = math.prod(shapes[name])
        out[name] = p[off:off + n].reshape(-1)[:size].reshape(shapes[name])
        off += n
    return out


def kernel(x, mem, g_mix, w_in, b_f, conv_w, conv_b, ln_g, ln_b, w_out, g_x, g_mem, w_mq, w_mkv, w_mo, g_ffn, w_gu, w_down, g_final, loss_target, m_g_mix, m_w_in, m_b_f, m_conv_w, m_conv_b, m_ln_g, m_ln_b, m_w_out, m_g_x, m_g_mem, m_w_mq, m_w_mkv, m_w_mo, m_g_ffn, m_w_gu, m_w_down, m_g_final, v_g_mix, v_w_in, v_b_f, v_conv_w, v_conv_b, v_ln_g, v_ln_b, v_w_out, v_g_x, v_g_mem, v_w_mq, v_w_mkv, v_w_mo, v_g_ffn, v_w_gu, v_w_down, v_g_final):
    names = ["g_mix", "w_in", "b_f", "conv_w", "conv_b", "ln_g", "ln_b", "w_out", "g_x", "g_mem", "w_mq", "w_mkv",
             "w_mo", "g_ffn", "w_gu", "w_down", "g_final"]
    W = dict(zip(names, [g_mix, w_in, b_f, conv_w, conv_b, ln_g, ln_b, w_out, g_x, g_mem, w_mq, w_mkv, w_mo, g_ffn,
                         w_gu, w_down, g_final]))
    M = dict(zip(names, [m_g_mix, m_w_in, m_b_f, m_conv_w, m_conv_b, m_ln_g, m_ln_b, m_w_out, m_g_x, m_g_mem, m_w_mq,
                         m_w_mkv, m_w_mo, m_g_ffn, m_w_gu, m_w_down, m_g_final]))
    V = dict(zip(names, [v_g_mix, v_w_in, v_b_f, v_conv_w, v_conv_b, v_ln_g, v_ln_b, v_w_out, v_g_x, v_g_mem, v_w_mq,
                         v_w_mkv, v_w_mo, v_g_ffn, v_w_gu, v_w_down, v_g_final]))
    big_names = [n for n, _, _, _ in BIG]
    B, S, _ = x.shape
    T = B * S
    mx, my, mc = _pos()
    chip = 2 * mx + my
    pos = jnp.stack([mc, chip]).astype(jnp.int32)

    shard2d = lambda a: a.reshape(a.shape[-2], a.shape[-1])
    big2d = lambda d, n: shard2d(d[n]).T if n == "w_in" else shard2d(d[n])
    shard_bf = {n: big2d(W, n).astype(BF16) for n in big_names}
    ag_mid = ["w_mkv", "w_out", "w_mq", "w_mo"]
    ag_ffn = ["w_gu", "w_down"]
    cw_mine = jnp.pad(shard2d(conv_w), ((0, 1), (0, 0)))
    w_in_slab, cw_slab = _run_comm(_ag_comm([shard_bf["w_in"], cw_mine]), "ag_w_in")
    slab = {"w_in": w_in_slab}
    w_int = w_in_slab.reshape(D_IN, D)
    w_ft = jnp.pad(w_int[OFF_F:D_IN], ((0, D_IN_PAD - D_IN), (0, 0)))
    cw = jnp.transpose(cw_slab, (1, 0, 2)).reshape(CONV_HALO, CONV_CH)

    row = lambda a: a.reshape(1, -1)
    bf_pad = jnp.pad(row(b_f), ((0, 0), (0, LANES - 8)))
    x2d = x.reshape(T, D)
    mem2d = mem.reshape(B * MEM_LEN, D)
    tgt = loss_target.reshape(T, D)

    (h, u, gt, q, k, v, zf, c, cq, qx, kx, qT), got = _fwd_in(x2d, row(g_mix), w_int, w_ft, bf_pad, B, S,
                                                  comm=_ag_comm([shard_bf[n] for n in ag_mid[:2]]))
    slab.update(zip(ag_mid[:2], got))
    ckT = jnp.transpose(c.reshape(B, S, LANES)[:, :, :8], (0, 2, 1)).reshape(B, N_PAIR, 2, S)
    ckT = jnp.pad(ckT, ((0, 0), (0, 0), (0, 6), (0, 0)))
    (y, co), got = _conv_fwd(u, gt, cw, row(conv_b), row(ln_g), row(ln_b), B, S,
                             comm=_ag_comm([shard_bf[n] for n in ag_mid[2:]]))
    slab.update(zip(ag_mid[2:], got))
    (o, fox_bias), got = _fox_fwd(qx, kx, v, cq, B, S, comm=_ag_comm([shard_bf[n] for n in ag_ffn]))
    slab.update(zip(ag_ffn, got))
    full = {n: slab[n] if by_col else slab[n].reshape(4 * r, c) for n, r, c, by_col in BIG}
    mn, km, vm = _mem_kv(mem2d, row(g_mem), full["w_mkv"], B)
    (x1, hx, qm, om, x2, cat), _ = _fwd_mid(x2d, co, o, km, vm, full["w_out"], full["w_mq"], full["w_mo"], row(g_x), B, S)
    hf, gu, act, dx3, loss_p, dg_final = _fwd_ffn(x2, tgt, full["w_gu"], full["w_down"], row(g_ffn), row(g_final), T)

    pos_sum = lambda gs, rcvs, ns: _chip_sum(gs, rcvs, pos, "rs_chip_sum_" + ns[0])
    fin_sum = lambda gs, rcvs, rcs, ns: _final_sum(gs, rcvs, rcs, pos, "rs_final_sum_" + ns[0])
    RH = {}
    dgu, dx2, dg_ffn = _bwd_ffn(dx3, gu, x2, full["w_gu"], full["w_down"], row(g_ffn), T)
    g_ffn_w = [_dw(hf, dgu, "dw_gu", D_FF, slabs=FF_CHUNK, tk=512), _dw(act, dx3, "dw_down", D).reshape(4, D_FF // 4, D)]
    (dx1, dqm, dco, do, dd, dkm, dvm, dg_x, doT), rcv_ffn = _bwd_mid(dx2, x1, qm, km, vm, o, full["w_mo"], full["w_mq"],
                                                                full["w_out"], row(g_x), B, S, comm=_sibling_comm(g_ffn_w))
    pb_ffn = pos_sum(g_ffn_w, rcv_ffn, ag_ffn)
    dkv, dg_mem = _mem_bwd(dkm, dvm, mem2d, full["w_mkv"], row(g_mem), B)
    g_mid_w = [_dw(mn, dkv, "dw_mkv", 512, slabs=512), _dw(cat, dx1, "dw_out", D).reshape(4, 256, D),
               _dw(hx, dqm, "dw_mq", D).reshape(4, 256, D), _dw(om, dx2, "dw_mo", D).reshape(4, 256, D)]
    (dq, dk, dv, dc, dcq), got = _fox_bwd(q, k, v, do, fox_bias, dd, ckT, qT, doT, B, S,
                                          comm=_join(_ici_comm(pb_ffn), _sibling_comm(g_mid_w)))
    rc_ffn, rcv_mid = got[:len(pb_ffn)], got[len(pb_ffn):]
    RH.update(zip(ag_ffn, fin_sum(g_ffn_w, rcv_ffn, rc_ffn, ag_ffn)))
    pb_mid = pos_sum(g_mid_w, rcv_mid, ag_mid)
    dc8 = jnp.transpose(dc[:, :, :2, :].reshape(B, 8, S), (0, 2, 1)).reshape(T, 8)
    dc8 = dc8 + dcq.reshape(T, 8, HEAD_D)[:, :, 0]
    dzf, dbf = _fgate_bwd(jnp.pad(dc8, ((0, 0), (0, LANES - 8))), zf, B, S)
    (du, dgt, dcw, dvec), rc_mid = _conv_bwd(dco, y, u, gt, cw, row(ln_g), row(ln_b), B, S, comm=_ici_comm(pb_mid))
    RH.update(zip(ag_mid, fin_sum(g_mid_w, rcv_mid, rc_mid, ag_mid)))
    dz = jnp.concatenate([du, dgt, dq, dk, dv, dzf], axis=1)
    g_in_w = [_dw(dz, h, "dw_in", D, tk=D_IN_PAD // 3, rows=D_IN).reshape(4, D_IN // 4, D)]
    rcv_in = _run_comm(_sibling_comm(g_in_w), "rs_sibling_in")
    (grad_x, dg_mix), rc_in = _bwd_in(dz, w_int, w_ft, x2d, dx1, row(g_mix), T,
                                      comm=_ici_comm(pos_sum(g_in_w, rcv_in, ["w_in"])))
    RH.update(zip(["w_in"], fin_sum(g_in_w, rcv_in, rc_in, ["w_in"])))

    small_g = {"g_mix": dg_mix, "b_f": dbf[:, :8], "conv_w": dcw, "conv_b": dvec[0], "ln_g": dvec[1], "ln_b": dvec[2],
               "g_x": dg_x, "g_mem": dg_mem, "g_ffn": dg_ffn, "g_final": dg_final, "loss": loss_p[:, :1]}
    sg, filled = _small_allreduce(_pack_small(small_g, CONV_HALO * 4), "allreduce_small", [RH[n] for n in big_names])
    shared = dict(zip(big_names, filled))
    stepped = _adamw([big2d(W, n) for n in big_names], [shared[n] for n in big_names], [big2d(M, n) for n in big_names],
                     [big2d(V, n) for n in big_names], "adamw_big")
    G, DL, NM, NV = (dict(zip(big_names, col)) for col in zip(*stepped))
    shapes = {n: W[n].shape for n in names if n not in big_names}
    shapes["conv_w"] = (CONV_HALO, CONV_CH)
    shapes["loss"] = (1,)
    sgrads = _unpack_small(sg, shapes, CONV_HALO * 4)
    loss = sgrads.pop("loss")[0]
    sgrads["conv_w"] = lax.dynamic_slice(sgrads["conv_w"], (0, chip * LANES), (CONV_K, LANES)).reshape(W["conv_w"].shape)
    spack = lambda d: _pack_small({n: d[n] for n in sgrads}, CONV_HALO)
    (_, sd, snm, snv), = _adamw([spack(W)], [spack(sgrads)], [spack(M)], [spack(V)], "adamw_small", parts=1)
    sshapes = {n: W[n].shape for n in sgrads}
    SD, SNM, SNV = (_unpack_small(a, sshapes, CONV_HALO) for a in (sd, snm, snv))

    def collect(bigs, smalls):
        back = lambda n: (bigs[n].T if n == "w_in" else bigs[n]).reshape(W[n].shape)
        return [back(n) if n in big_names else smalls[n] for n in names]

    return (loss, grad_x.reshape(x.shape), *collect(G, sgrads), *collect(DL, SD), *collect(NM, SNM), *collect(NV, SNV))
```

```python
import functools
import math

import jax
import jax.numpy as jnp
from jax import lax
from jax.experimental import pallas as pl
from jax.experimental.pallas import tpu as pltpu

F32, BF16 = jnp.float32, jnp.bfloat16
MESH = pl.DeviceIdType.MESH

D = 1024
CONV_CH = 512
CONV_K = 31
CONV_HALO = 32
FOX_W = 512
HEAD_D = 64
N_PAIR = 4
MEM_LEN = 256
MEM_HEADS = 4
MEM_HD = 256
D_FF = 2816
FF_CHUNK = 1408
D_IN = 2568
D_IN_PAD = 2688
OFF_F = 2560
EPS = 1e-6
LANES = 128

ADAM_LR, ADAM_B1, ADAM_B2, ADAM_EPS, ADAM_WD, ADAM_STEP = 0.001, 0.9, 0.999, 1e-08, 0.01, 10

VMEM_LIMIT = 60 * 1024 * 1024

BIG = (("w_out", 256, 1024, False), ("w_mq", 256, 1024, False), ("w_mkv", 1024, 512, True),
       ("w_mo", 256, 1024, False), ("w_gu", 1024, 1408, True), ("w_down", 704, 1024, False),
       ("w_in", 642, 1024, False))

ANY = pl.BlockSpec(memory_space=pl.ANY)


def _sig(x):
    return 1.0 / (1.0 + jnp.exp(-x))


def _dot(a, b):
    return jnp.dot(a, b, preferred_element_type=F32)


def _dot_nt(a, b):
    return lax.dot_general(a, b, (((1,), (1,)), ((), ())), preferred_element_type=F32)


def _dot_tn(a, b):
    return lax.dot_general(a, b, (((0,), (0,)), ((), ())), preferred_element_type=F32)


def _split3(x):
    hi = x.astype(BF16)
    r = x - hi.astype(F32)
    mid = r.astype(BF16)
    return hi, mid, (r - mid.astype(F32)).astype(BF16)


def _dot_01(a, b):
    if a.dtype == jnp.bool_:
        return sum(_dot(a.astype(BF16), t) for t in _split3(b))
    return sum(_dot(t, b.astype(BF16)) for t in _split3(a))


def _resident(a):
    nd = a.ndim
    return pl.BlockSpec(a.shape, lambda *_: (0,) * nd, pipeline_mode=pl.Buffered(1))


def _acc_spec(shape):
    nd = len(shape)
    return pl.BlockSpec(shape, lambda *_: (0,) * nd)


def _params(n_grid):
    return pltpu.CompilerParams(dimension_semantics=("arbitrary",) * n_grid, vmem_limit_bytes=VMEM_LIMIT)


def _sds(shape, dtype):
    return jax.ShapeDtypeStruct(shape, dtype)


def _rms(x):
    r = lax.rsqrt(jnp.mean(x * x, axis=-1, keepdims=True) + EPS)
    return r, x * r


def _rms_bwd(dy, xh, r, g):
    dxh = dy * g
    dx = r * (dxh - xh * jnp.mean(dxh * xh, axis=-1, keepdims=True))
    return dx, dy * xh


def _head_expand(rows, cols):
    hd = lax.broadcasted_iota(jnp.int32, (rows, cols), 1) // HEAD_D
    hr = lax.broadcasted_iota(jnp.int32, (rows, cols), 0)
    return hd == hr


def _feat_major_spec(TB, FOX_T, nb):
    return pl.BlockSpec((1, TB // FOX_T, FOX_W, FOX_T), lambda b, j: (b, j, 0, 0))


def _fwd_in(x2, g_mix, w_int, w_ft, bf_pad, B, S, comm=None):
    T = B * S
    TB = min(512, S)
    nb = S // TB
    FOX_T = min(256, S)

    def body(x_ref, g_ref, w_ref, wf_ref, bf_ref, h_ref, u_ref, gt_ref, q_ref, k_ref, v_ref, zf_ref, c_ref, cq_ref,
             qx_ref, kx_ref, qt_ref, carry):
        j = pl.program_id(1)

        @pl.when(j == 0)
        def _():
            carry[...] = jnp.zeros_like(carry)

        _, xh = _rms(x_ref[...])
        h = (xh * g_ref[...]).astype(BF16)
        h_ref[...] = h
        u_ref[...] = _dot_nt(h, w_ref[0:512, :])
        gt_ref[...] = _dot_nt(h, w_ref[512:1024, :])
        qf = _dot_nt(h, w_ref[1024:1536, :])
        qb = qf.astype(BF16)
        kb = _dot_nt(h, w_ref[1536:2048, :]).astype(BF16)
        q_ref[...] = qb
        k_ref[...] = kb
        for t in range(TB // FOX_T):
            qt_ref[0, t] = qf[t * FOX_T:(t + 1) * FOX_T, :].T.astype(BF16)
        v_ref[...] = _dot_nt(h, w_ref[2048:2560, :]).astype(BF16)
        zf = _dot_nt(h, wf_ref[...]) + bf_ref[...]
        zf_ref[...] = zf
        lane = lax.broadcasted_iota(jnp.int32, zf.shape, 1)
        logf = jnp.where(lane < 8, jnp.minimum(zf, 0.0) - jnp.log(1.0 + jnp.exp(-jnp.abs(zf))), 0.0)
        row = lax.broadcasted_iota(jnp.int32, (TB, TB), 0)
        col = lax.broadcasted_iota(jnp.int32, (TB, TB), 1)
        c = _dot_01(row >= col, logf) + carry[0:1, :]
        carry[0:1, :] = c[TB - 1:TB, :]
        c_ref[...] = c
        cq = _dot_01(c, _head_expand(LANES, FOX_W))
        cq_ref[...] = cq
        hl = lax.broadcasted_iota(jnp.int32, (TB, LANES), 1)
        for hd in range(2 * N_PAIR):
            grp = slice((hd // 2) * LANES, (hd // 2 + 1) * LANES)
            swap = (lambda t: t) if hd % 2 == 0 else (lambda t: pltpu.roll(t, HEAD_D, 1))
            qf = swap(qb[:, grp].astype(F32) * (1.0 / math.sqrt(HEAD_D)))
            kf = swap(kb[:, grp].astype(F32))
            cv = cq[:, grp] if hd % 2 == 1 else pltpu.roll(cq[:, grp], HEAD_D, 1)
            hi = cv.astype(BF16).astype(F32)
            mid = (cv - hi).astype(BF16).astype(F32)
            lo = (cv - hi - mid).astype(BF16).astype(F32)
            pick = lambda a, b, c3, one_from, one_to: jnp.where(hl == a[0], a[1], jnp.where(hl == b[0], b[1], jnp.where(
                hl == c3[0], c3[1], jnp.where((hl >= one_from) & (hl < one_to), 1.0, 0.0))))
            qx = jnp.where(hl < HEAD_D, qf, pick((67, hi), (68, mid), (69, lo), 64, 67))
            kx = jnp.where(hl < HEAD_D, kf, pick((64, -hi), (65, -mid), (66, -lo), 67, 70))
            qx_ref[:, hd * LANES:(hd + 1) * LANES] = qx.astype(BF16)
            kx_ref[:, hd * LANES:(hd + 1) * LANES] = kx.astype(BF16)

    tok = lambda w: pl.BlockSpec((TB, w), lambda b, j: (b * nb + j, 0))
    outs = [(D, BF16), (512, F32), (512, F32), (512, BF16), (512, BF16), (512, BF16), (LANES, F32),
            (LANES, F32), (FOX_W, F32), (2 * FOX_W, BF16), (2 * FOX_W, BF16)]
    return _call(
        body, comm, name="fwd_in", grid=(B, nb),
        in_specs=[tok(D), _resident(g_mix), _resident(w_int), _resident(w_ft), _resident(bf_pad)],
        out_specs=[tok(w) for w, _ in outs] + [_feat_major_spec(TB, FOX_T, nb)],
        out_shape=[_sds((T, w), dt) for w, dt in outs] + [_sds((B, S // FOX_T, FOX_W, FOX_T), BF16)],
        scratch_shapes=[pltpu.VMEM((8, LANES), F32)],
        args=(x2, g_mix, w_int, w_ft, bf_pad))


def _head_sum(n):
    hc = lax.broadcasted_iota(jnp.int32, (n, n), 1) // HEAD_D
    hr = lax.broadcasted_iota(jnp.int32, (n, n), 0) // HEAD_D
    return hc == hr


def _layernorm_silu(y, lg, lb):
    mu = jnp.mean(y, axis=-1, keepdims=True)
    yc = y - mu
    rs = lax.rsqrt(jnp.mean(yc * yc, axis=-1, keepdims=True) + EPS)
    n = yc * rs
    l = n * lg + lb
    return rs, n, l


SUB = 8


def _shifted_copies(cat, sh, rows):
    for r in range(1, SUB):
        sh[r, 0:rows, :] = cat[r:r + rows, :]


def _tap(cat, sh, off, rows, cols=slice(None)):
    r = off % SUB
    return cat[off:off + rows, cols] if r == 0 else sh[r, off - r:off - r + rows, cols]


CONV_ROWS = 128


def _conv_pieces(CB):
    rows = min(CONV_ROWS, CB)
    return [(r0, rows, slice(c0, c0 + LANES)) for c0 in range(0, CONV_CH, LANES) for r0 in range(0, CB, rows)]


def _conv_fwd(u, gt, cw, cb, lng, lnb, B, S, comm=None):
    T = B * S
    CB = min(256, S)
    nb = S // CB

    def body(u_ref, gt_ref, w_ref, cb_ref, lg_ref, lb_ref, y_ref, co_ref, acat, ash):
        j = pl.program_id(1)

        @pl.when(j == 0)
        def _():
            acat[0:CONV_HALO, :] = jnp.zeros((CONV_HALO, CONV_CH), F32)

        acat[CONV_HALO:CONV_HALO + CB, :] = u_ref[...] * _sig(gt_ref[...])
        _shifted_copies(acat, ash, CB + CONV_HALO - SUB)
        for r0, rows, cs in _conv_pieces(CB):
            acc = jnp.zeros((rows, LANES), F32) + cb_ref[:, cs]
            for k in range(CONV_K):
                acc = acc + w_ref[k:k + 1, cs] * _tap(acat, ash, r0 + CONV_HALO - (CONV_K - 1) + k, rows, cs)
            y_ref[r0:r0 + rows, cs] = acc
        acat[0:CONV_HALO, :] = acat[CB:CB + CONV_HALO, :]
        _, _, l = _layernorm_silu(y_ref[...], lg_ref[...], lb_ref[...])
        co_ref[...] = (l * _sig(l)).astype(BF16)

    tok = lambda w: pl.BlockSpec((CB, w), lambda b, j: (b * nb + j, 0))
    return _call(
        body, comm, name="conv_fwd", grid=(B, nb),
        in_specs=[tok(CONV_CH), tok(CONV_CH), _resident(cw), _resident(cb), _resident(lng), _resident(lnb)],
        out_specs=[tok(CONV_CH), tok(CONV_CH)],
        out_shape=[_sds((T, CONV_CH), F32), _sds((T, CONV_CH), BF16)],
        scratch_shapes=[pltpu.VMEM((CONV_HALO + CB, CONV_CH), F32),
                        pltpu.VMEM((SUB, CB + CONV_HALO - SUB, CONV_CH), F32)],
        args=(u, gt, cw, cb, lng, lnb))


def _fox_fwd(qx, kx, v, cq, B, S, comm=None):
    T = B * S
    TQ = min(256, S)
    nq = S // TQ

    def body(qa_ref, qb_ref, ka_ref, kb_ref, v_ref, cq_ref, o_ref, lse_ref, s_scr, s_odd, m_scr, acc_scr):
        i = pl.program_id(2)
        lane = lax.broadcasted_iota(jnp.int32, (TQ, LANES), 1)
        lo = lane < HEAD_D
        qh = (qa_ref[...], qb_ref[...])
        kh = (ka_ref, kb_ref)
        m_scr[...] = jnp.full(m_scr.shape, -1e30, F32)
        acc_scr[...] = jnp.zeros_like(acc_scr)
        row = lax.broadcasted_iota(jnp.int32, (TQ, TQ), 0)
        col = lax.broadcasted_iota(jnp.int32, (TQ, TQ), 1)
        wide = lambda x: jnp.concatenate([x, x], axis=1) if TQ == 2 * LANES else jnp.tile(x, (1, TQ // LANES))

        def scores(j, s_buf):
            start = pl.multiple_of(j * TQ, TQ)
            for h in range(2):
                s_buf[h] = _dot_nt(qh[h], kh[h][pl.ds(start, TQ), :])

        def softmax_step(j, s_buf, diagonal):
            start = pl.multiple_of(j * TQ, TQ)
            vj = v_ref[pl.ds(start, TQ), :]
            for h in range(2):
                def logits():
                    return jnp.where(col <= row, s_buf[h], -1e30) if diagonal else s_buf[h]

                m_old = m_scr[h]
                m_new = jnp.maximum(m_old, jnp.max(logits(), axis=-1, keepdims=True))
                alpha = jnp.exp(m_old - m_new)
                m_scr[h] = m_new
                p = jnp.exp(logits() - wide(m_new)).astype(BF16)
                vx = jnp.where(lo if h == 0 else ~lo, vj, jnp.ones_like(vj))
                acc_scr[h] = alpha * acc_scr[h] + _dot(p, vx)

        def two_blocks(jj, carry):
            j = 2 * jj
            scores(j + 1, s_odd)
            softmax_step(j, s_scr, False)
            scores(j + 2, s_scr)
            softmax_step(j + 1, s_odd, False)
            return carry

        scores(0, s_scr)
        lax.fori_loop(0, i // 2, two_blocks, 0)

        @pl.when(i % 2 == 0)
        def _():
            softmax_step(i, s_scr, True)

        @pl.when(i % 2 == 1)
        def _():
            scores(i, s_odd)
            softmax_step(i - 1, s_scr, False)
            softmax_step(i, s_odd, True)

        acc = jnp.where(lo, acc_scr[0], acc_scr[1])
        den = pltpu.roll(jnp.where(lo, acc_scr[1], acc_scr[0]), HEAD_D, 1)
        o_ref[...] = acc / den
        lse_ref[...] = cq_ref[...] - (jnp.where(lo, m_scr[0], m_scr[1]) + jnp.log(den))

    qspec = pl.BlockSpec((TQ, LANES), lambda b, p, i: (b * nq + i, p))
    kspec = pl.BlockSpec((S, LANES), lambda b, p, i: (b, p))
    qhead = lambda h: pl.BlockSpec((TQ, LANES), lambda b, p, i: (b * nq + i, 2 * p + h))
    khead = lambda h: pl.BlockSpec((S, LANES), lambda b, p, i: (b, 2 * p + h))
    return _call(
        body, comm, name="fox_fwd", grid=(B, N_PAIR, nq),
        in_specs=[qhead(0), qhead(1), khead(0), khead(1), kspec, qspec],
        out_specs=[qspec, qspec],
        out_shape=[_sds((T, FOX_W), F32), _sds((T, FOX_W), F32)],
        scratch_shapes=[pltpu.VMEM((2, TQ, TQ), F32), pltpu.VMEM((2, TQ, TQ), F32),
                        pltpu.VMEM((2, TQ, LANES), F32), pltpu.VMEM((2, TQ, LANES), F32)],
        args=(qx, qx, kx, kx, v, cq))


def _mem_kv(mem2, g_mem, w_mkv, B):
    def body(m_ref, g_ref, w_ref, mn_ref, km_ref, vm_ref):
        _, xh = _rms(m_ref[...])
        mn = (xh * g_ref[...]).astype(BF16)
        mn_ref[...] = mn
        for s in range(2):
            km_ref[:, 512 * s:512 * (s + 1)] = _dot(mn, w_ref[s]).astype(BF16)
            vm_ref[:, 512 * s:512 * (s + 1)] = _dot(mn, w_ref[2 + s]).astype(BF16)

    blk = pl.BlockSpec((MEM_LEN, D), lambda b: (b, 0))
    return pl.pallas_call(
        body, name="mem_kv", grid=(B,),
        in_specs=[blk, _resident(g_mem), _resident(w_mkv)],
        out_specs=[blk, blk, blk],
        out_shape=[_sds((B * MEM_LEN, D), BF16)] * 3,
        compiler_params=_params(1),
    )(mem2, g_mem, w_mkv)


def _mem_probs(qm, km):
    ps = []
    for h in range(MEM_HEADS):
        hs = slice(h * MEM_HD, (h + 1) * MEM_HD)
        lg = _dot_nt(qm[:, hs], km[:, hs]) * (1.0 / math.sqrt(MEM_HD))
        e = jnp.exp(lg - jnp.max(lg, axis=-1, keepdims=True))
        ps.append(e / jnp.sum(e, axis=-1, keepdims=True))
    return ps


def _fwd_mid(x2, co, o, km, vm, w_out, w_mq, w_mo, g_x, B, S, comm=None):
    T = B * S
    TB = min(512, S)
    nb = S // TB

    def body(x_ref, co_ref, o_ref, km_ref, vm_ref, wo_ref, wq_ref, wm_ref, g_ref,
             x1_ref, hx_ref, qm_ref, om_ref, x2_ref, cat_ref):
        cat_ref[:, 0:CONV_CH] = co_ref[...]
        cat_ref[:, CONV_CH:D] = o_ref[...].astype(BF16)
        x1 = x_ref[...] + _dot(cat_ref[...], wo_ref[...])
        x1_ref[...] = x1
        _, xh = _rms(x1)
        hx = (xh * g_ref[...]).astype(BF16)
        hx_ref[...] = hx
        qm = _dot(hx, wq_ref[...]).astype(BF16)
        qm_ref[...] = qm
        ps = _mem_probs(qm, km_ref[...])
        vmv = vm_ref[...]
        for h in range(MEM_HEADS):
            hs = slice(h * MEM_HD, (h + 1) * MEM_HD)
            om_ref[:, hs] = _dot(ps[h].astype(BF16), vmv[:, hs]).astype(BF16)
        x2_ref[...] = x1 + _dot(om_ref[...], wm_ref[...])

    tok = lambda w: pl.BlockSpec((TB, w), lambda b, j: (b * nb + j, 0))
    memb = pl.BlockSpec((MEM_LEN, D), lambda b, j: (b, 0))
    outs = [(D, F32), (D, BF16), (D, BF16), (D, BF16), (D, F32), (D, BF16)]
    return _call(
        body, comm, name="fwd_mid", grid=(B, nb),
        in_specs=[tok(D), tok(CONV_CH), tok(FOX_W), memb, memb, _resident(w_out), _resident(w_mq), _resident(w_mo),
                  _resident(g_x)],
        out_specs=[tok(w) for w, _ in outs],
        out_shape=[_sds((T, w), dt) for w, dt in outs],
        scratch_shapes=[],
        args=(x2, co, o, km, vm, w_out, w_mq, w_mo, g_x))


def _load_gate_up(wgu_hbm, wg, wu, sems):
    copies = [pltpu.make_async_copy(wgu_hbm.at[s], (wg if s < 2 else wu).at[:, pl.ds((s % 2) * FF_CHUNK, FF_CHUNK)],
                                    sems.at[s]) for s in range(4)]
    for cp in copies:
        cp.start()
    for cp in copies:
        cp.wait()


def _fwd_ffn(x2, tgt, w_gu, w_down, g_ffn, g_final, T):
    TB = min(256, T)
    nb = T // TB

    def body(x_ref, t_ref, wgu_ref, wd_ref, gf_ref, gl_ref, hf_ref, gu_ref, act_ref, dx3_ref, loss_ref, dgl_ref,
             wg, wu, sems):
        i = pl.program_id(0)

        @pl.when(i == 0)
        def _():
            _load_gate_up(wgu_ref, wg, wu, sems)
            loss_ref[...] = jnp.zeros_like(loss_ref)
            dgl_ref[...] = jnp.zeros_like(dgl_ref)

        x2v = x_ref[...]
        _, xh = _rms(x2v)
        hf = (xh * gf_ref[...]).astype(BF16)
        hf_ref[...] = hf
        g = _dot(hf, wg[...])
        u = _dot(hf, wu[...])
        gu_ref[:, 0:D_FF] = g
        gu_ref[:, D_FF:2 * D_FF] = u
        act = (g * _sig(g) * u).astype(BF16)
        act_ref[...] = act
        x3 = x2v + _dot(act, wd_ref[...])
        r3, xh3 = _rms(x3)
        gl = gl_ref[...]
        e = xh3 * gl - t_ref[...]
        loss_ref[...] += jnp.sum(e * e) * (0.5 / D)
        dy = e * (1.0 / D)
        dx3, dgl = _rms_bwd(dy, xh3, r3, gl)
        dx3_ref[...] = dx3
        dgl_ref[...] += jnp.sum(dgl, axis=0, keepdims=True)

    tok = lambda w: pl.BlockSpec((TB, w), lambda i: (i, 0))
    return pl.pallas_call(
        body, name="fwd_ffn", grid=(nb,),
        in_specs=[tok(D), tok(D), ANY, _resident(w_down), _resident(g_ffn), _resident(g_final)],
        out_specs=[tok(D), tok(2 * D_FF), tok(D_FF), tok(D), _acc_spec((1, LANES)), _acc_spec((1, D))],
        out_shape=[_sds((T, D), BF16), _sds((T, 2 * D_FF), F32), _sds((T, D_FF), BF16), _sds((T, D), F32),
                   _sds((1, LANES), F32), _sds((1, D), F32)],
        scratch_shapes=[pltpu.VMEM((D, D_FF), BF16), pltpu.VMEM((D, D_FF), BF16), pltpu.SemaphoreType.DMA((4,))],
        compiler_params=_params(1),
    )(x2, tgt, w_gu, w_down, g_ffn, g_final)


def _bwd_ffn(dx3, gu, x2, w_gu, w_down, g_ffn, T):
    TB = min(256, T)
    nb = T // TB

    def body(d_ref, gu_ref, x_ref, wgu_ref, wd_ref, gf_ref, dgu_ref, dx2_ref, dgf_ref, wg, wu, sems):
        i = pl.program_id(0)

        @pl.when(i == 0)
        def _():
            _load_gate_up(wgu_ref, wg, wu, sems)
            dgf_ref[...] = jnp.zeros_like(dgf_ref)

        dx3v = d_ref[...]
        db = dx3v.astype(BF16)
        dact = _dot_nt(db, wd_ref[...])
        g = gu_ref[:, 0:D_FF]
        u = gu_ref[:, D_FF:2 * D_FF]
        sg = _sig(g)
        dg = (dact * u * sg * (1.0 + g * (1.0 - sg))).astype(BF16)
        du = (dact * g * sg).astype(BF16)
        dgu_ref[:, 0:D_FF] = dg
        dgu_ref[:, D_FF:2 * D_FF] = du
        dhf = _dot_nt(dg, wg[...]) + _dot_nt(du, wu[...])
        r2, xh2 = _rms(x_ref[...])
        dx, dg_tok = _rms_bwd(dhf, xh2, r2, gf_ref[...])
        dx2_ref[...] = dx3v + dx
        dgf_ref[...] += jnp.sum(dg_tok, axis=0, keepdims=True)

    tok = lambda w: pl.BlockSpec((TB, w), lambda i: (i, 0))
    return pl.pallas_call(
        body, name="bwd_ffn", grid=(nb,),
        in_specs=[tok(D), tok(2 * D_FF), tok(D), ANY, _resident(w_down), _resident(g_ffn)],
        out_specs=[tok(2 * D_FF), tok(D), _acc_spec((1, D))],
        out_shape=[_sds((T, 2 * D_FF), BF16), _sds((T, D), F32), _sds((1, D), F32)],
        scratch_shapes=[pltpu.VMEM((D, D_FF), BF16), pltpu.VMEM((D, D_FF), BF16), pltpu.SemaphoreType.DMA((4,))],
        compiler_params=_params(1),
    )(dx3, gu, x2, w_gu, w_down, g_ffn)


def _bwd_mid(dx2, x1, qm, km, vm, o, w_mo, w_mq, w_out, g_x, B, S, comm=None):
    T = B * S
    TB = min(512, S)
    nb = S // TB
    FOX_T = min(256, S)
    inv = 1.0 / math.sqrt(MEM_HD)

    def body(d_ref, x1_ref, qm_ref, km_ref, vm_ref, o_ref, wm_ref, wq_ref, wo_ref, g_ref,
             dx1_ref, dqm_ref, dco_ref, do_ref, dd_ref, dkm_ref, dvm_ref, dgx_ref, dot_ref):
        b = pl.program_id(0)
        j = pl.program_id(1)

        @pl.when((b == 0) & (j == 0))
        def _():
            dgx_ref[...] = jnp.zeros_like(dgx_ref)

        @pl.when(j == 0)
        def _():
            dkm_ref[...] = jnp.zeros_like(dkm_ref)
            dvm_ref[...] = jnp.zeros_like(dvm_ref)

        dx2v = d_ref[...]
        dom = _dot_nt(dx2v.astype(BF16), wm_ref[...]).astype(BF16)
        qmv = qm_ref[...]
        kmv = km_ref[...]
        vmv = vm_ref[...]
        ps = _mem_probs(qmv, kmv)
        for h in range(MEM_HEADS):
            hs = slice(h * MEM_HD, (h + 1) * MEM_HD)
            p = ps[h]
            dp = _dot_nt(dom[:, hs], vmv[:, hs])
            ds = (p * (dp - jnp.sum(p * dp, axis=-1, keepdims=True))).astype(BF16)
            dqm_ref[:, hs] = (_dot(ds, kmv[:, hs]) * inv).astype(BF16)
            dkm_ref[:, hs] += _dot_tn(ds, qmv[:, hs]) * inv
            dvm_ref[:, hs] += _dot_tn(p.astype(BF16), dom[:, hs])
        dhx = _dot_nt(dqm_ref[...], wq_ref[...])
        r1, xh1 = _rms(x1_ref[...])
        dx, dg_tok = _rms_bwd(dhx, xh1, r1, g_ref[...])
        dx1 = dx2v + dx
        dx1_ref[...] = dx1
        dgx_ref[...] += jnp.sum(dg_tok, axis=0, keepdims=True)
        d1b = dx1.astype(BF16)
        dco_ref[...] = _dot_nt(d1b, wo_ref[0:CONV_CH, :])
        do = _dot_nt(d1b, wo_ref[CONV_CH:D, :])
        dob = do.astype(BF16)
        do_ref[...] = dob
        for t in range(TB // FOX_T):
            dot_ref[0, t] = do[t * FOX_T:(t + 1) * FOX_T, :].T.astype(BF16)
        dd_ref[...] = _dot_01(dob.astype(F32) * o_ref[...], _head_sum(FOX_W))

    tok = lambda w: pl.BlockSpec((TB, w), lambda b, j: (b * nb + j, 0))
    memb = pl.BlockSpec((MEM_LEN, D), lambda b, j: (b, 0))
    outs = [(D, F32), (D, BF16), (CONV_CH, F32), (FOX_W, BF16), (FOX_W, F32)]
    return _call(
        body, comm, name="bwd_mid", grid=(B, nb),
        in_specs=[tok(D), tok(D), tok(D), memb, memb, tok(FOX_W), _resident(w_mo), _resident(w_mq), _resident(w_out),
                  _resident(g_x)],
        out_specs=[tok(w) for w, _ in outs] + [memb, memb, _acc_spec((1, D)), _feat_major_spec(TB, FOX_T, nb)],
        out_shape=[_sds((T, w), dt) for w, dt in outs] + [_sds((B * MEM_LEN, D), F32)] * 2 + [_sds((1, D), F32)]
        + [_sds((B, S // FOX_T, FOX_W, FOX_T), BF16)],
        scratch_shapes=[],
        args=(dx2, x1, qm, km, vm, o, w_mo, w_mq, w_out, g_x))


def _mem_bwd(dkm, dvm, mem2, w_mkv, g_mem, B):
    def body(dk_ref, dv_ref, m_ref, w_ref, g_ref, dkv_ref, dg_ref):
        b = pl.program_id(0)

        @pl.when(b == 0)
        def _():
            dg_ref[...] = jnp.zeros_like(dg_ref)

        dk = dk_ref[...].astype(BF16)
        dv = dv_ref[...].astype(BF16)
        dkv_ref[:, 0:D] = dk
        dkv_ref[:, D:2 * D] = dv
        dmn = jnp.zeros((MEM_LEN, D), F32)
        for s in range(2):
            dmn = dmn + _dot_nt(dk[:, 512 * s:512 * (s + 1)], w_ref[s]) + _dot_nt(dv[:, 512 * s:512 * (s + 1)], w_ref[2 + s])
        _, xh = _rms(m_ref[...])
        dg_ref[...] += jnp.sum(dmn * xh, axis=0, keepdims=True)

    blk = pl.BlockSpec((MEM_LEN, D), lambda b: (b, 0))
    return pl.pallas_call(
        body, name="mem_bwd", grid=(B,),
        in_specs=[blk, blk, blk, _resident(w_mkv), _resident(g_mem)],
        out_specs=[pl.BlockSpec((MEM_LEN, 2 * D), lambda b: (b, 0)), _acc_spec((1, D))],
        out_shape=[_sds((B * MEM_LEN, 2 * D), BF16), _sds((1, D), F32)],
        compiler_params=_params(1),
    )(dkm, dvm, mem2, w_mkv, g_mem)


def _fox_bwd(q, k, v, do, bias, dd, ckT, qT, doT, B, S, comm=None):
    T = B * S
    TK = min(256, S)
    nk = S // TK
    scale = 1.0 / math.sqrt(HEAD_D)

    def body(q_ref, k_ref, v_ref, do_ref, bias_ref, dd_ref, ck_ref, qt_ref, dot_ref, dq_ref, dk_ref, dv_ref, dc_ref,
             dcq_ref, dq_acc, rs_acc, s_scr, dp_scr, s_odd, dp_odd, dk_acc, dv_acc, dc_acc):
        j = pl.program_id(2)

        @pl.when(j == 0)
        def _():
            dq_acc[...] = jnp.zeros_like(dq_acc)
            rs_acc[...] = jnp.zeros_like(rs_acc)

        dk_acc[...] = jnp.zeros_like(dk_acc)
        dv_acc[...] = jnp.zeros_like(dv_acc)
        dc_acc[...] = jnp.zeros_like(dc_acc)
        lane = lax.broadcasted_iota(jnp.int32, (TK, LANES), 1)
        lo = lane < HEAD_D
        ks = k_ref[...] * jnp.asarray(scale, BF16)
        v2 = v_ref[...]
        zero = jnp.zeros_like(ks)
        kh = (jnp.where(lo, ks, zero), jnp.where(lo, zero, ks))
        vh = (jnp.where(lo, v2, zero), jnp.where(lo, zero, v2))
        kstart = pl.multiple_of(j * TK, TK)
        ckh = tuple(ck_ref[0, 0, h:h + 1, pl.ds(kstart, TK)] for h in range(2))
        row = lax.broadcasted_iota(jnp.int32, (TK, TK), 0)
        col = lax.broadcasted_iota(jnp.int32, (TK, TK), 1)
        wide = lambda x: jnp.concatenate([x, x], axis=1) if TK == 2 * LANES else jnp.tile(x, (1, TK // LANES))

        def scores(i, s_buf, dp_buf):
            start = pl.multiple_of(i * TK, TK)
            qi = q_ref[pl.ds(start, TK), :]
            doi = do_ref[pl.ds(start, TK), :]
            for h in range(2):
                s_buf[h] = _dot_nt(qi, kh[h])
                dp_buf[h] = _dot_nt(doi, vh[h])

        def grads(i, s_buf, dp_buf, diagonal):
            start = pl.multiple_of(i * TK, TK)
            bias2 = bias_ref[pl.ds(start, TK), :]
            dd2 = dd_ref[pl.ds(start, TK), :]
            for h in range(2):
                hc = slice(h * HEAD_D, h * HEAD_D + 1)
                bias = jnp.broadcast_to(bias2[:, hc], (TK, LANES))
                ddh = jnp.broadcast_to(dd2[:, hc], (TK, LANES))
                p = jnp.exp((s_buf[h] - ckh[h]) + wide(bias))
                if diagonal:
                    p = jnp.where(col <= row, p, 0.0)
                ds = p * (dp_buf[h] - wide(ddh))
                dc_acc[h, 0:1, :] += jnp.sum(ds, axis=0, keepdims=True)
                rs_acc[h, pl.ds(start, TK), :] += jnp.sum(ds, axis=1, keepdims=True)
                pb = p.astype(BF16)
                dsb = ds.astype(BF16)
                feat = slice(h * HEAD_D, (h + 1) * HEAD_D)
                dv_acc[feat, :] += _dot(dot_ref[0, i, feat, :], pb)
                dk_acc[feat, :] += _dot(qt_ref[0, i, feat, :], dsb)
                dq_acc[pl.ds(start, TK), :] += _dot(dsb, kh[h])

        n_off = nk - 1 - j
        block = lambda t: jnp.where(t < n_off, j + 1 + t, j)

        def two_blocks(tt, carry):
            t = 2 * tt
            scores(block(t + 1), s_odd, dp_odd)
            grads(block(t), s_scr, dp_scr, False)
            scores(block(t + 2), s_scr, dp_scr)
            grads(block(t + 1), s_odd, dp_odd, False)
            return carry

        scores(block(0), s_scr, dp_scr)
        lax.fori_loop(0, n_off // 2, two_blocks, 0)

        @pl.when(n_off % 2 == 0)
        def _():
            grads(j, s_scr, dp_scr, True)

        @pl.when(n_off % 2 == 1)
        def _():
            scores(j, s_odd, dp_odd)
            grads(nk - 1, s_scr, dp_scr, False)
            grads(j, s_odd, dp_odd, True)

        dk_ref[...] = (dk_acc[...].T * scale).astype(BF16)
        dv_ref[...] = dv_acc[...].T.astype(BF16)
        sub = lax.broadcasted_iota(jnp.int32, (8, TK), 0)
        dca = dc_acc[0, 0:1, :]
        dcb = dc_acc[1, 0:1, :]
        dc_ref[0, 0] = jnp.where(sub == 0, -dca, jnp.where(sub == 1, -dcb, 0.0))

        @pl.when(j == nk - 1)
        def _():
            dq_ref[...] = dq_acc[...].astype(BF16)
            lo_s = lax.broadcasted_iota(jnp.int32, (S, LANES), 1) < HEAD_D
            dcq_ref[...] = jnp.where(lo_s, rs_acc[0], rs_acc[1])

    full = pl.BlockSpec((S, LANES), lambda b, p, j: (b, p))
    blk = pl.BlockSpec((TK, LANES), lambda b, p, j: (b * nk + j, p))
    featT = pl.BlockSpec((1, nk, LANES, TK), lambda b, p, j: (b, 0, p, 0))
    return _call(
        body, comm, name="fox_bwd", grid=(B, N_PAIR, nk),
        in_specs=[full, blk, blk, full, full, full, pl.BlockSpec((1, 1, 8, S), lambda b, p, j: (b, p, 0, 0)),
                  featT, featT],
        out_specs=[full, blk, blk, pl.BlockSpec((1, 1, 8, TK), lambda b, p, j: (b, p, 0, j)), full],
        out_shape=[_sds((T, FOX_W), BF16), _sds((T, FOX_W), BF16), _sds((T, FOX_W), BF16),
                   _sds((B, N_PAIR, 8, S), F32), _sds((T, FOX_W), F32)],
        scratch_shapes=[pltpu.VMEM((S, LANES), F32), pltpu.VMEM((2, S, 1), F32),
                        pltpu.VMEM((2, TK, TK), F32), pltpu.VMEM((2, TK, TK), F32),
                        pltpu.VMEM((2, TK, TK), F32), pltpu.VMEM((2, TK, TK), F32),
                        pltpu.VMEM((LANES, TK), F32), pltpu.VMEM((LANES, TK), F32), pltpu.VMEM((2, 8, TK), F32)],
        args=(q, k, v, do, bias, dd, ckT, qT, doT))


def _fgate_bwd(dc8, zf, B, S):
    T = B * S
    TB = min(512, S)
    nb = S // TB

    def body(dc_ref, zf_ref, dzf_ref, dbf_ref, carry):
        b = pl.program_id(0)
        j = pl.program_id(1)

        @pl.when((b == 0) & (j == 0))
        def _():
            dbf_ref[...] = jnp.zeros_like(dbf_ref)

        @pl.when(j == 0)
        def _():
            carry[...] = jnp.zeros_like(carry)

        dc = dc_ref[...]
        row = lax.broadcasted_iota(jnp.int32, (TB, TB), 0)
        col = lax.broadcasted_iota(jnp.int32, (TB, TB), 1)
        dlogf = _dot_01(col >= row, dc) + carry[0:1, :]
        carry[0:1, :] = dlogf[0:1, :]
        lane = lax.broadcasted_iota(jnp.int32, dc.shape, 1)
        dzf = jnp.where(lane < 8, dlogf * _sig(-zf_ref[...]), 0.0)
        dzf_ref[...] = dzf.astype(BF16)
        dbf_ref[...] += jnp.sum(dzf, axis=0, keepdims=True)

    tok = pl.BlockSpec((TB, LANES), lambda b, j: (b * nb + (nb - 1 - j), 0))
    return pl.pallas_call(
        body, name="fgate_bwd", grid=(B, nb),
        in_specs=[tok, tok],
        out_specs=[tok, _acc_spec((1, LANES))],
        out_shape=[_sds((T, LANES), BF16), _sds((1, LANES), F32)],
        scratch_shapes=[pltpu.VMEM((8, LANES), F32)],
        compiler_params=_params(2),
    )(dc8, zf)


def _conv_bwd(dco, y, u, gt, cw, lng, lnb, B, S, comm=None):
    T = B * S
    CB = min(256, S)
    nb = S // CB
    hb = CB // CONV_HALO

    def body(dco_ref, y_ref, u_ref, gt_ref, up_ref, gp_ref, w_ref, lg_ref, lb_ref,
             du_ref, dgt_ref, dw_ref, vec_ref, acat, dycat, ash, dysh):
        b = pl.program_id(0)
        j = pl.program_id(1)
        jr = nb - 1 - j

        @pl.when((b == 0) & (j == 0))
        def _():
            dw_ref[...] = jnp.zeros_like(dw_ref)
            vec_ref[...] = jnp.zeros_like(vec_ref)

        @pl.when(j == 0)
        def _():
            dycat[CB:CB + CONV_HALO, :] = jnp.zeros((CONV_HALO, CONV_CH), F32)

        lg = lg_ref[...]
        rs, n, l = _layernorm_silu(y_ref[...], lg, lb_ref[...])
        sg = _sig(l)
        dl = dco_ref[...] * (sg * (1.0 + l * (1.0 - sg)))
        dn = dl * lg
        dy = rs * (dn - jnp.mean(dn, axis=-1, keepdims=True) - n * jnp.mean(dn * n, axis=-1, keepdims=True))
        vec_ref[0:1, :] += jnp.sum(dy, axis=0, keepdims=True)
        vec_ref[1:2, :] += jnp.sum(dl * n, axis=0, keepdims=True)
        vec_ref[2:3, :] += jnp.sum(dl, axis=0, keepdims=True)
        dycat[0:CB, :] = dy
        acat[0:CONV_HALO, :] = jnp.where(jr > 0, up_ref[...] * _sig(gp_ref[...]), 0.0)
        acat[CONV_HALO:CONV_HALO + CB, :] = u_ref[...] * _sig(gt_ref[...])
        _shifted_copies(acat, ash, CB + CONV_HALO - SUB)
        _shifted_copies(dycat, dysh, CB + CONV_HALO - SUB)
        for r0, rows, cs in _conv_pieces(CB):
            dyp = dycat[r0:r0 + rows, cs]
            da = jnp.zeros((rows, LANES), F32)
            for k in range(CONV_K):
                da = da + w_ref[k:k + 1, cs] * _tap(dycat, dysh, r0 + CONV_K - 1 - k, rows, cs)
                dw_ref[k:k + 1, cs] += jnp.sum(dyp * _tap(acat, ash, r0 + CONV_HALO - (CONV_K - 1) + k, rows, cs),
                                               axis=0, keepdims=True)
            uv = u_ref[r0:r0 + rows, cs]
            sgt = _sig(gt_ref[r0:r0 + rows, cs])
            du_ref[r0:r0 + rows, cs] = (da * sgt).astype(BF16)
            dgt_ref[r0:r0 + rows, cs] = (da * uv * sgt * (1.0 - sgt)).astype(BF16)
        dycat[CB:CB + CONV_HALO, :] = dycat[0:CONV_HALO, :]

    tok = lambda w: pl.BlockSpec((CB, w), lambda b, j: (b * nb + (nb - 1 - j), 0))
    prev = pl.BlockSpec((CONV_HALO, CONV_CH), lambda b, j: (jnp.maximum((b * nb + (nb - 1 - j)) * hb - 1, 0), 0))
    return _call(
        body, comm, name="conv_bwd", grid=(B, nb),
        in_specs=[tok(CONV_CH), tok(CONV_CH), tok(CONV_CH), tok(CONV_CH), prev, prev, _resident(cw), _resident(lng),
                  _resident(lnb)],
        out_specs=[tok(CONV_CH), tok(CONV_CH), _acc_spec((CONV_HALO, CONV_CH)), _acc_spec((8, CONV_CH))],
        out_shape=[_sds((T, CONV_CH), BF16), _sds((T, CONV_CH), BF16), _sds((CONV_HALO, CONV_CH), F32),
                   _sds((8, CONV_CH), F32)],
        scratch_shapes=[pltpu.VMEM((CONV_HALO + CB, CONV_CH), F32), pltpu.VMEM((CB + CONV_HALO, CONV_CH), F32),
                        pltpu.VMEM((SUB, CB + CONV_HALO - SUB, CONV_CH), F32),
                        pltpu.VMEM((SUB, CB + CONV_HALO - SUB, CONV_CH), F32)],
        args=(dco, y, u, gt, u, gt, cw, lng, lnb))


def _bwd_in(dz, w_int, w_ft, x2, dx1, g_mix, T, comm=None):
    TB = min(512, T)
    nb = T // TB

    def body(dz_ref, w_ref, wf_ref, x_ref, d1_ref, g_ref, gx_ref, dg_ref):
        i = pl.program_id(0)

        @pl.when(i == 0)
        def _():
            dg_ref[...] = jnp.zeros_like(dg_ref)

        dh = _dot(dz_ref[:, 0:OFF_F], w_ref[0:OFF_F, :]) + _dot(dz_ref[:, OFF_F:D_IN_PAD], wf_ref[...])
        r0, xh0 = _rms(x_ref[...])
        dx, dg_tok = _rms_bwd(dh, xh0, r0, g_ref[...])
        gx_ref[...] = d1_ref[...] + dx
        dg_ref[...] += jnp.sum(dg_tok, axis=0, keepdims=True)

    tok = lambda w: pl.BlockSpec((TB, w), lambda i: (i, 0))
    return _call(
        body, comm, name="bwd_in", grid=(nb,),
        in_specs=[tok(D_IN_PAD), _resident(w_int), _resident(w_ft), tok(D), tok(D), _resident(g_mix)],
        out_specs=[tok(D), _acc_spec((1, D))],
        out_shape=[_sds((T, D), F32), _sds((1, D), F32)],
        scratch_shapes=[],
        args=(dz, w_int, w_ft, x2, dx1, g_mix))


def _dw(a, b, name, tn, slabs=0, tk=None, rows=None):
    T, K = a.shape
    N = b.shape[1]
    per = tn // slabs if slabs else 1
    tk = tk or (K if K <= 1024 else K // 2)
    tt = min(1024, T)
    nt = T // tt

    def body(a_ref, b_ref, o_ref, acc):
        t = pl.program_id(2)

        @pl.when(t == 0)
        def _():
            acc[...] = jnp.zeros_like(acc)

        acc[...] += _dot_tn(a_ref[...].astype(BF16), b_ref[...].astype(BF16))

        @pl.when(t == nt - 1)
        def _():
            if slabs:
                for sl in range(per):
                    o_ref[sl] = acc[:, sl * slabs:(sl + 1) * slabs]
            else:
                o_ref[...] = acc[...]

    return pl.pallas_call(
        body, name=name, grid=(K // tk, N // tn, nt),
        in_specs=[pl.BlockSpec((tt, tk), lambda i, j, t: (t, i)), pl.BlockSpec((tt, tn), lambda i, j, t: (t, j))],
        out_specs=(pl.BlockSpec((per, tk, slabs), lambda i, j, t: (j, i, 0)) if slabs
                   else pl.BlockSpec((tk, tn), lambda i, j, t: (i, j))),
        out_shape=_sds((N // slabs, K, slabs) if slabs else (rows or K, N), F32),
        scratch_shapes=[pltpu.VMEM((tk, tn), F32)],
        compiler_params=_params(3),
    )(a, b)


def _pos():
    return lax.axis_index("x"), lax.axis_index("y"), lax.axis_index("c")


def _remote(src, dst, ssem, rsem, to):
    return pltpu.make_async_remote_copy(src_ref=src, dst_ref=dst, send_sem=ssem, recv_sem=rsem, device_id=to,
                                        device_id_type=MESH)


def _split_axis(shape):
    return 0 if shape[0] % 32 == 0 else 1


def _half_shape(shape, parts=2):
    return (shape[0] // parts, shape[1]) if _split_axis(shape) == 0 else (shape[0], shape[1] // parts)


def _half(shape, c):
    R, C = shape
    if _split_axis(shape) == 0:
        return (pl.ds(pl.multiple_of(c * (R // 2), 16), R // 2), slice(None))
    return (slice(None), pl.ds(pl.multiple_of(c * (C // 2), LANES), C // 2))


def _half_block(shape, parts, lead, which):
    blk = _half_shape(shape, parts)
    idx = (which, 0) if _split_axis(shape) == 0 else (0, which)
    return blk, tuple(lead) + idx


class _Comm:
    def __init__(self, ins, out_shapes, sems, start, finish):
        self.ins, self.out_shapes, self.sems, self.start, self.finish = list(ins), list(out_shapes), list(sems), start, finish


def _ag_comm(shards):
    n = len(shards)

    def parts(ins, outs, sems):
        send_sems, recv_sems, local_sems = sems
        x, y, c = _pos()
        me, sib = (x, y, c), (x, y, 1 - c)
        chips = [(1 - x, y), (x, 1 - y), (1 - x, 1 - y)]

        def rows(w, px, py, pc):
            return outs[w].at[(2 * px + py,) + _half(shards[w].shape, pc)]

        def copy(w, k, block, to, src=None):
            return _remote(rows(w, *block) if src is None else src, rows(w, *block), send_sems.at[w, k],
                           recv_sems.at[w, k], to)

        mine, first = [], []
        for w in range(n):
            src = ins[w].at[_half(shards[w].shape, c)]
            mine.append(pltpu.make_async_copy(src, rows(w, *me), local_sems.at[w]))
            first += [copy(w, 0, me, sib, src=src)] + [copy(w, 1 + j, me, (*chip, c), src=src) for j, chip in enumerate(chips)]
        return c, me, sib, chips, copy, mine, first

    def start(ins, outs, sems):
        _, _, _, _, _, mine, first = parts(ins, outs, sems)
        for cp in mine + first:
            cp.start()

    def finish(ins, outs, sems):
        c, me, sib, chips, copy, mine, first = parts(ins, outs, sems)
        passed = []
        for w in range(n):
            for j, chip in enumerate(chips):
                copy(w, 1 + j, (*chip, c), me).wait_recv()
                passed.append(copy(w, 4 + j, (*chip, c), sib))
                passed[-1].start()
        for w in range(n):
            copy(w, 0, sib, me).wait_recv()
            for j, chip in enumerate(chips):
                copy(w, 4 + j, (*chip, 1 - c), me).wait_recv()
        for cp in first + passed:
            cp.wait_send()
        for cp in mine:
            cp.wait()

    D7 = pltpu.SemaphoreType.DMA((n, 7))
    return _Comm(shards, [_sds((4,) + s.shape, s.dtype) for s in shards], [D7, D7, pltpu.SemaphoreType.DMA((n,))],
                 start, finish)


def _sibling_comm(gs):
    n = len(gs)

    def copies(ins, outs, sems):
        send_sems, recv_sems = sems
        x, y, c = _pos()
        return [_remote(ins[w].at[(s,) + _half(gs[w].shape[1:], 1 - c)], outs[w].at[s], send_sems.at[w, s],
                        recv_sems.at[w, s], (x, y, 1 - c)) for w in range(n) for s in range(4)]

    def start(ins, outs, sems):
        for cp in copies(ins, outs, sems):
            cp.start()

    def finish(ins, outs, sems):
        for cp in copies(ins, outs, sems):
            cp.wait()

    D4 = pltpu.SemaphoreType.DMA((n, 4))
    return _Comm(gs, [_sds((4,) + _half_shape(g.shape[1:]), F32) for g in gs], [D4, D4], start, finish)


def _ici_comm(pbs):
    n = len(pbs)

    def copies(ins, outs, sems):
        send_sems, recv_sems = sems
        x, y, c = _pos()
        return [_remote(ins[w].at[2 * tx + ty], outs[w].at[j], send_sems.at[w, j], recv_sems.at[w, j], (tx, ty, c))
                for w in range(n) for j, (tx, ty) in enumerate([(1 - x, y), (x, 1 - y), (1 - x, 1 - y)])]

    def start(ins, outs, sems):
        for cp in copies(ins, outs, sems):
            cp.start()

    def finish(ins, outs, sems):
        for cp in copies(ins, outs, sems):
            cp.wait()

    D3 = pltpu.SemaphoreType.DMA((n, 3))
    return _Comm(pbs, [_sds((3,) + p.shape[1:], BF16) for p in pbs], [D3, D3], start, finish)


def _join(*comms):
    counts = [(len(c.ins), len(c.out_shapes), len(c.sems)) for c in comms]

    def each(which):
        def run(ins, outs, sems):
            i = o = k = 0
            for c, (ni, no, nk) in zip(comms, counts):
                getattr(c, which)(ins[i:i + ni], outs[o:o + no], sems[k:k + nk])
                i, o, k = i + ni, o + no, k + nk
        return run

    return _Comm(sum((c.ins for c in comms), []), sum((c.out_shapes for c in comms), []),
                 sum((c.sems for c in comms), []), each("start"), each("finish"))


def _run_comm(comm, name):
    ni, no = len(comm.ins), len(comm.out_shapes)

    def body(*refs):
        ins, outs, sems = refs[:ni], refs[ni:ni + no], refs[ni + no:]
        comm.start(ins, outs, sems)
        comm.finish(ins, outs, sems)

    return pl.pallas_call(body, name=name, out_shape=comm.out_shapes, in_specs=[ANY] * ni, out_specs=[ANY] * no,
                          scratch_shapes=comm.sems)(*comm.ins)


def _call(body, comm, *, name, grid, in_specs, out_specs, out_shape, scratch_shapes, args):
    n_grid = len(grid)
    if comm is None:
        res = pl.pallas_call(body, name=name, grid=grid, in_specs=in_specs, out_specs=out_specs, out_shape=out_shape,
                             scratch_shapes=scratch_shapes, compiler_params=_params(n_grid))(*args)
        return list(res), []
    n_in, n_out, n_scr = len(in_specs), len(out_specs), len(scratch_shapes)
    ni, no = len(comm.ins), len(comm.out_shapes)

    def carried(*refs):
        ins, refs = refs[:n_in], refs[n_in:]
        cins, refs = refs[:ni], refs[ni:]
        outs, refs = refs[:n_out], refs[n_out:]
        couts, refs = refs[:no], refs[no:]
        scr, csems = refs[:n_scr], refs[n_scr:]
        ids = [pl.program_id(ax) for ax in range(n_grid)]
        first = functools.reduce(jnp.logical_and, [i == 0 for i in ids])
        last = functools.reduce(jnp.logical_and, [i == g - 1 for i, g in zip(ids, grid)])

        @pl.when(first)
        def _():
            comm.start(cins, couts, csems)

        body(*ins, *outs, *scr)

        @pl.when(last)
        def _():
            comm.finish(cins, couts, csems)

    res = pl.pallas_call(
        carried, name=name, grid=grid, in_specs=list(in_specs) + [ANY] * ni, out_specs=list(out_specs) + [ANY] * no,
        out_shape=list(out_shape) + comm.out_shapes, scratch_shapes=list(scratch_shapes) + comm.sems,
        compiler_params=_params(n_grid))(*args, *comm.ins)
    return list(res[:n_out]), list(res[n_out:])


def _small_allreduce(v, name, halves=()):
    P = v.shape[0]
    n = len(halves)
    vm = pl.BlockSpec(memory_space=pltpu.VMEM)

    def body(v_ref, *refs):
        o_ref, outs = refs[n], refs[n + 1:2 * n + 1]
        gath, send_sems, recv_sems, half_send, half_recv = refs[2 * n + 1:]
        x, y, c = _pos()
        me = 4 * x + 2 * y + c
        gath[me] = v_ref[...]
        cps = []
        for r in range(1, 8):
            tx = (1 - x) if r & 4 else x
            ty = (1 - y) if r & 2 else y
            tc = (1 - c) if r & 1 else c
            cps.append(_remote(v_ref, gath.at[me], send_sems.at[r - 1], recv_sems.at[r - 1], (tx, ty, tc)))
        for w in range(n):
            mine = outs[w].at[_half(halves[w].shape, c)]
            cps.append(_remote(mine, mine, half_send.at[w], half_recv.at[w], (x, y, 1 - c)))
        for cp in cps:
            cp.start()
        for cp in cps:
            cp.wait()
        acc = gath[0]
        for d in range(1, 8):
            acc = acc + gath[d]
        o_ref[...] = acc

    res = pl.pallas_call(
        body, name=name, out_shape=[_sds((P, LANES), F32)] + [_sds(g.shape, F32) for g in halves],
        in_specs=[vm] + [ANY] * n, out_specs=[vm] + [ANY] * n, input_output_aliases={1 + w: 1 + w for w in range(n)},
        scratch_shapes=[pltpu.VMEM((8, P, LANES), F32), pltpu.SemaphoreType.DMA((7,)), pltpu.SemaphoreType.DMA((7,)),
                        pltpu.SemaphoreType.DMA((max(n, 1),)), pltpu.SemaphoreType.DMA((max(n, 1),))],
    )(v, *halves)
    return res[0], list(res[1:])


def _chip_sum(gs, rcvs, pos, name):
    n = len(gs)
    shards = [g.shape[1:] for g in gs]
    hss = [_half_shape(sh) for sh in shards]
    other = lambda i, pos: (pos[1] + 1 + i) % 4

    def body(pos_ref, *refs):
        for w in range(n):
            refs[2 * n + w][...] = (refs[w][...] + refs[n + w][...]).astype(BF16)

    mine = lambda w: pl.BlockSpec((1,) + hss[w], lambda i, pos: _half_block(shards[w], 2, (other(i, pos),), pos[0])[1])
    whole = lambda w: pl.BlockSpec((1,) + hss[w], lambda i, pos: (other(i, pos), 0, 0))
    return pl.pallas_call(
        body, name=name, out_shape=[_sds((4,) + hs, BF16) for hs in hss],
        grid_spec=pltpu.PrefetchScalarGridSpec(
            num_scalar_prefetch=1, grid=(3,),
            in_specs=[mine(w) for w in range(n)] + [whole(w) for w in range(n)],
            out_specs=[whole(w) for w in range(n)]),
        compiler_params=_params(1),
    )(pos, *gs, *rcvs)


def _final_sum(gs, rcvs, rcs, pos, name):
    n = len(gs)
    shards = [g.shape[1:] for g in gs]
    qss = [_half_shape(sh, 4) for sh in shards]

    def body(pos_ref, *refs):
        for w in range(n):
            acc = refs[w][0] + refs[n + w][0]
            for j in range(3):
                acc = acc + refs[2 * n + w][j].astype(F32)
            refs[3 * n + w][...] = acc

    def spec(w, lead_block, lead_index, mine):
        return pl.BlockSpec(lead_block + qss[w], lambda i, pos: _half_block(
            shards[w], 4, lead_index(pos), pos[0] * 2 + i if mine else i)[1])

    own_slab, first, none = (lambda pos: (pos[1],)), (lambda pos: (0,)), (lambda pos: ())
    return pl.pallas_call(
        body, name=name, out_shape=[_sds(sh, F32) for sh in shards],
        grid_spec=pltpu.PrefetchScalarGridSpec(
            num_scalar_prefetch=1, grid=(2,),
            in_specs=[spec(w, (1,), own_slab, True) for w in range(n)] + [spec(w, (1,), own_slab, False) for w in range(n)]
            + [spec(w, (3,), first, False) for w in range(n)],
            out_specs=[spec(w, (), none, True) for w in range(n)]),
        compiler_params=_params(1),
    )(pos, *gs, *rcvs, *rcs)


def _adamw_math(w, g, m, v):
    m = ADAM_B1 * m + (1.0 - ADAM_B1) * g
    v = ADAM_B2 * v + (1.0 - ADAM_B2) * (g * g)
    m_hat = m / (1.0 - ADAM_B1 ** ADAM_STEP)
    v_hat = v / (1.0 - ADAM_B2 ** ADAM_STEP)
    delta = -ADAM_LR * (m_hat / (jnp.sqrt(v_hat) + ADAM_EPS) + ADAM_WD * w)
    return delta, m, v


ADAM_PARTS = 8


def _adamw(ws, gs, ms, vs, name, parts=ADAM_PARTS):
    n = len(ws)
    shapes = [w.shape for w in ws]

    def body(*refs):
        for k in range(n):
            w_ref, g_ref, m_ref, v_ref = (refs[j * n + k] for j in range(4))
            go_ref, d_ref, nm_ref, nv_ref = (refs[(4 + j) * n + k] for j in range(4))
            g = g_ref[...]
            d, nm, nv = _adamw_math(w_ref[...], g, m_ref[...], v_ref[...])
            go_ref[...] = g
            d_ref[...] = d
            nm_ref[...] = nm
            nv_ref[...] = nv

    blk = lambda k: pl.BlockSpec(_half_shape(shapes[k], parts), lambda i: _half_block(shapes[k], parts, (), i)[1])
    res = pl.pallas_call(
        body, name=name, grid=(parts,), in_specs=[blk(k) for k in range(n)] * 4, out_specs=[blk(k) for k in range(n)] * 4,
        out_shape=[_sds(sh, F32) for sh in shapes] * 4, compiler_params=_params(1),
    )(*ws, *gs, *ms, *vs)
    return [tuple(res[j * n + k] for j in range(4)) for k in range(n)]


SMALL = (("g_mix", 8), ("b_f", 8), ("conv_w", None), ("conv_b", 8), ("ln_g", 8), ("ln_b", 8), ("g_x", 8), ("g_mem", 8),
         ("g_ffn", 8), ("g_final", 8), ("loss", 8))


def _pack_small(parts, conv_rows):
    rows = []
    for name, n in SMALL:
        if name not in parts:
            continue
        n = conv_rows if n is None else n
        flat = parts[name].reshape(-1).astype(F32)
        flat = jnp.pad(flat, (0, n * LANES - flat.shape[0]))
        rows.append(flat.reshape(n, LANES))
    return jnp.concatenate(rows, axis=0)


def _unpack_small(p, shapes, conv_rows):
    out, off = {}, 0
    for name, n in SMALL:
        if name not in shapes:
            continue
        n = conv_rows if n is None else n
        size = math.prod(shapes[name])
        out[name] = p[off:off + n].reshape(-1)[:size].reshape(shapes[name])
        off += n
    return out


def kernel(x, mem, g_mix, w_in, b_f, conv_w, conv_b, ln_g, ln_b, w_out, g_x, g_mem, w_mq, w_mkv, w_mo, g_ffn, w_gu, w_down, g_final, loss_target, m_g_mix, m_w_in, m_b_f, m_conv_w, m_conv_b, m_ln_g, m_ln_b, m_w_out, m_g_x, m_g_mem, m_w_mq, m_w_mkv, m_w_mo, m_g_ffn, m_w_gu, m_w_down, m_g_final, v_g_mix, v_w_in, v_b_f, v_conv_w, v_conv_b, v_ln_g, v_ln_b, v_w_out, v_g_x, v_g_mem, v_w_mq, v_w_mkv, v_w_mo, v_g_ffn, v_w_gu, v_w_down, v_g_final):
    names = ["g_mix", "w_in", "b_f", "conv_w", "conv_b", "ln_g", "ln_b", "w_out", "g_x", "g_mem", "w_mq", "w_mkv",
             "w_mo", "g_ffn", "w_gu", "w_down", "g_final"]
    W = dict(zip(names, [g_mix, w_in, b_f, conv_w, conv_b, ln_g, ln_b, w_out, g_x, g_mem, w_mq, w_mkv, w_mo, g_ffn,
                         w_gu, w_down, g_final]))
    M = dict(zip(names, [m_g_mix, m_w_in, m_b_f, m_conv_w, m_conv_b, m_ln_g, m_ln_b, m_w_out, m_g_x, m_g_mem, m_w_mq,
                         m_w_mkv, m_w_mo, m_g_ffn, m_w_gu, m_w_down, m_g_final]))
    V = dict(zip(names, [v_g_mix, v_w_in, v_b_f, v_conv_w, v_conv_b, v_ln_g, v_ln_b, v_w_out, v_g_x, v_g_mem, v_w_mq,
                         v_w_mkv, v_w_mo, v_g_ffn, v_w_gu, v_w_down, v_g_final]))
    big_names = [n for n, _, _, _ in BIG]
    B, S, _ = x.shape
    T = B * S
    mx, my, mc = _pos()
    chip = 2 * mx + my
    pos = jnp.stack([mc, chip]).astype(jnp.int32)

    shard2d = lambda a: a.reshape(a.shape[-2], a.shape[-1])
    big2d = lambda d, n: shard2d(d[n]).T if n == "w_in" else shard2d(d[n])
    shard_bf = {n: big2d(W, n).astype(BF16) for n in big_names}
    ag_mid = ["w_mkv", "w_out", "w_mq", "w_mo"]
    ag_ffn = ["w_gu", "w_down"]
    cw_mine = jnp.pad(shard2d(conv_w), ((0, 1), (0, 0)))
    w_in_slab, cw_slab = _run_comm(_ag_comm([shard_bf["w_in"], cw_mine]), "ag_w_in")
    slab = {"w_in": w_in_slab}
    w_int = w_in_slab.reshape(D_IN, D)
    w_ft = jnp.pad(w_int[OFF_F:D_IN], ((0, D_IN_PAD - D_IN), (0, 0)))
    cw = jnp.transpose(cw_slab, (1, 0, 2)).reshape(CONV_HALO, CONV_CH)

    row = lambda a: a.reshape(1, -1)
    bf_pad = jnp.pad(row(b_f), ((0, 0), (0, LANES - 8)))
    x2d = x.reshape(T, D)
    mem2d = mem.reshape(B * MEM_LEN, D)
    tgt = loss_target.reshape(T, D)

    (h, u, gt, q, k, v, zf, c, cq, qx, kx, qT), got = _fwd_in(x2d, row(g_mix), w_int, w_ft, bf_pad, B, S,
                                                  comm=_ag_comm([shard_bf[n] for n in ag_mid[:2]]))
    slab.update(zip(ag_mid[:2], got))
    ckT = jnp.transpose(c.reshape(B, S, LANES)[:, :, :8], (0, 2, 1)).reshape(B, N_PAIR, 2, S)
    ckT = jnp.pad(ckT, ((0, 0), (0, 0), (0, 6), (0, 0)))
    (y, co), got = _conv_fwd(u, gt, cw, row(conv_b), row(ln_g), row(ln_b), B, S,
                             comm=_ag_comm([shard_bf[n] for n in ag_mid[2:]]))
    slab.update(zip(ag_mid[2:], got))
    (o, fox_bias), got = _fox_fwd(qx, kx, v, cq, B, S, comm=_ag_comm([shard_bf[n] for n in ag_ffn]))
    slab.update(zip(ag_ffn, got))
    full = {n: slab[n] if by_col else slab[n].reshape(4 * r, c) for n, r, c, by_col in BIG}
    mn, km, vm = _mem_kv(mem2d, row(g_mem), full["w_mkv"], B)
    (x1, hx, qm, om, x2, cat), _ = _fwd_mid(x2d, co, o, km, vm, full["w_out"], full["w_mq"], full["w_mo"], row(g_x), B, S)
    hf, gu, act, dx3, loss_p, dg_final = _fwd_ffn(x2, tgt, full["w_gu"], full["w_down"], row(g_ffn), row(g_final), T)

    pos_sum = lambda gs, rcvs, ns: _chip_sum(gs, rcvs, pos, "rs_chip_sum_" + ns[0])
    fin_sum = lambda gs, rcvs, rcs, ns: _final_sum(gs, rcvs, rcs, pos, "rs_final_sum_" + ns[0])
    RH = {}
    dgu, dx2, dg_ffn = _bwd_ffn(dx3, gu, x2, full["w_gu"], full["w_down"], row(g_ffn), T)
    g_ffn_w = [_dw(hf, dgu, "dw_gu", D_FF, slabs=FF_CHUNK), _dw(act, dx3, "dw_down", D).reshape(4, D_FF // 4, D)]
    (dx1, dqm, dco, do, dd, dkm, dvm, dg_x, doT), rcv_ffn = _bwd_mid(dx2, x1, qm, km, vm, o, full["w_mo"], full["w_mq"],
                                                                full["w_out"], row(g_x), B, S, comm=_sibling_comm(g_ffn_w))
    pb_ffn = pos_sum(g_ffn_w, rcv_ffn, ag_ffn)
    dkv, dg_mem = _mem_bwd(dkm, dvm, mem2d, full["w_mkv"], row(g_mem), B)
    g_mid_w = [_dw(mn, dkv, "dw_mkv", 512, slabs=512), _dw(cat, dx1, "dw_out", D).reshape(4, 256, D),
               _dw(hx, dqm, "dw_mq", D).reshape(4, 256, D), _dw(om, dx2, "dw_mo", D).reshape(4, 256, D)]
    (dq, dk, dv, dc, dcq), got = _fox_bwd(q, k, v, do, fox_bias, dd, ckT, qT, doT, B, S,
                                          comm=_join(_ici_comm(pb_ffn), _sibling_comm(g_mid_w)))
    rc_ffn, rcv_mid = got[:len(pb_ffn)], got[len(pb_ffn):]
    RH.update(zip(ag_ffn, fin_sum(g_ffn_w, rcv_ffn, rc_ffn, ag_ffn)))
    pb_mid = pos_sum(g_mid_w, rcv_mid, ag_mid)
    dc8 = jnp.transpose(dc[:, :, :2, :].reshape(B, 8, S), (0, 2, 1)).reshape(T, 8)
    dc8 = dc8 + dcq.reshape(T, 8, HEAD_D)[:, :, 0]
    dzf, dbf = _fgate_bwd(jnp.pad(dc8, ((0, 0), (0, LANES - 8))), zf, B, S)
    (du, dgt, dcw, dvec), rc_mid = _conv_bwd(dco, y, u, gt, cw, row(ln_g), row(ln_b), B, S, comm=_ici_comm(pb_mid))
    RH.update(zip(ag_mid, fin_sum(g_mid_w, rcv_mid, rc_mid, ag_mid)))
    dz = jnp.concatenate([du, dgt, dq, dk, dv, dzf], axis=1)
    g_in_w = [_dw(dz, h, "dw_in", D, tk=D_IN_PAD // 3, rows=D_IN).reshape(4, D_IN // 4, D)]
    rcv_in = _run_comm(_sibling_comm(g_in_w), "rs_sibling_in")
    (grad_x, dg_mix), rc_in = _bwd_in(dz, w_int, w_ft, x2d, dx1, row(g_mix), T,
                                      comm=_ici_comm(pos_sum(g_in_w, rcv_in, ["w_in"])))
    RH.update(zip(["w_in"], fin_sum(g_in_w, rcv_in, rc_in, ["w_in"])))

    small_g = {"g_mix": dg_mix, "b_f": dbf[:, :8], "conv_w": dcw, "conv_b": dvec[0], "ln_g": dvec[1], "ln_b": dvec[2],
               "g_x": dg_x, "g_mem": dg_mem, "g_ffn": dg_ffn, "g_final": dg_final, "loss": loss_p[:, :1]}
    sg, filled = _small_allreduce(_pack_small(small_g, CONV_HALO * 4), "allreduce_small", [RH[n] for n in big_names])
    shared = dict(zip(big_names, filled))
    stepped = _adamw([big2d(W, n) for n in big_names], [shared[n] for n in big_names], [big2d(M, n) for n in big_names],
                     [big2d(V, n) for n in big_names], "adamw_big")
    G, DL, NM, NV = (dict(zip(big_names, col)) for col in zip(*stepped))
    shapes = {n: W[n].shape for n in names if n not in big_names}
    shapes["conv_w"] = (CONV_HALO, CONV_CH)
    shapes["loss"] = (1,)
    sgrads = _unpack_small(sg, shapes, CONV_HALO * 4)
    loss = sgrads.pop("loss")[0]
    sgrads["conv_w"] = lax.dynamic_slice(sgrads["conv_w"], (0, chip * LANES), (CONV_K, LANES)).reshape(W["conv_w"].shape)
    spack = lambda d: _pack_small({n: d[n] for n in sgrads}, CONV_HALO)
    (_, sd, snm, snv), = _adamw([spack(W)], [spack(sgrads)], [spack(M)], [spack(V)], "adamw_small", parts=1)
    sshapes = {n: W[n].shape for n in sgrads}
    SD, SNM, SNV = (_unpack_small(a, sshapes, CONV_HALO) for a in (sd, snm, snv))

    def collect(bigs, smalls):
        back = lambda n: (bigs[n].T if n == "w_in" else bigs[n]).reshape(W[n].shape)
        return [back(n) if n in big_names else smalls[n] for n in names]

    return (loss, grad_x.reshape(x.shape), *collect(G, sgrads), *collect(DL, SD), *collect(NM, SNM), *collect(NV, SNV))
```

```python
import functools
import math

import jax
import jax.numpy as jnp
from jax import lax
from jax.experimental import pallas as pl
from jax.experimental.pallas import tpu as pltpu

F32, BF16 = jnp.float32, jnp.bfloat16
MESH = pl.DeviceIdType.MESH

D = 1024
CONV_CH = 512
CONV_K = 31
CONV_HALO = 32
FOX_W = 512
HEAD_D = 64
N_PAIR = 4
MEM_LEN = 256
MEM_HEADS = 4
MEM_HD = 256
D_FF = 2816
FF_CHUNK = 1408
D_IN = 2568
D_IN_PAD = 2688
OFF_F = 2560
EPS = 1e-6
LANES = 128

ADAM_LR, ADAM_B1, ADAM_B2, ADAM_EPS, ADAM_WD, ADAM_STEP = 0.001, 0.9, 0.999, 1e-08, 0.01, 10

VMEM_LIMIT = 60 * 1024 * 1024

BIG = (("w_out", 256, 1024, False), ("w_mq", 256, 1024, False), ("w_mkv", 1024, 512, True),
       ("w_mo", 256, 1024, False), ("w_gu", 1024, 1408, True), ("w_down", 704, 1024, False),
       ("w_in", 642, 1024, False))

ANY = pl.BlockSpec(memory_space=pl.ANY)


def _sig(x):
    return 1.0 / (1.0 + jnp.exp(-x))


def _dot(a, b):
    return jnp.dot(a, b, preferred_element_type=F32)


def _dot_nt(a, b):
    return lax.dot_general(a, b, (((1,), (1,)), ((), ())), preferred_element_type=F32)


def _dot_tn(a, b):
    return lax.dot_general(a, b, (((0,), (0,)), ((), ())), preferred_element_type=F32)


def _split3(x):
    hi = x.astype(BF16)
    r = x - hi.astype(F32)
    mid = r.astype(BF16)
    return hi, mid, (r - mid.astype(F32)).astype(BF16)


def _dot_01(a, b):
    if a.dtype == jnp.bool_:
        return sum(_dot(a.astype(BF16), t) for t in _split3(b))
    return sum(_dot(t, b.astype(BF16)) for t in _split3(a))


def _resident(a):
    nd = a.ndim
    return pl.BlockSpec(a.shape, lambda *_: (0,) * nd, pipeline_mode=pl.Buffered(1))


def _acc_spec(shape):
    nd = len(shape)
    return pl.BlockSpec(shape, lambda *_: (0,) * nd)


def _params(n_grid):
    return pltpu.CompilerParams(dimension_semantics=("arbitrary",) * n_grid, vmem_limit_bytes=VMEM_LIMIT)


def _sds(shape, dtype):
    return jax.ShapeDtypeStruct(shape, dtype)


def _rms(x):
    r = lax.rsqrt(jnp.mean(x * x, axis=-1, keepdims=True) + EPS)
    return r, x * r


def _rms_bwd(dy, xh, r, g):
    dxh = dy * g
    dx = r * (dxh - xh * jnp.mean(dxh * xh, axis=-1, keepdims=True))
    return dx, dy * xh


def _head_expand(rows, cols):
    hd = lax.broadcasted_iota(jnp.int32, (rows, cols), 1) // HEAD_D
    hr = lax.broadcasted_iota(jnp.int32, (rows, cols), 0)
    return hd == hr


def _feat_major_spec(TB, FOX_T, nb):
    return pl.BlockSpec((1, TB // FOX_T, FOX_W, FOX_T), lambda b, j: (b, j, 0, 0))


def _fwd_in(x2, g_mix, w_int, w_ft, bf_pad, B, S, comm=None):
    T = B * S
    TB = min(512, S)
    nb = S // TB
    FOX_T = min(256, S)

    def body(x_ref, g_ref, w_ref, wf_ref, bf_ref, h_ref, u_ref, gt_ref, q_ref, k_ref, v_ref, zf_ref, c_ref, cq_ref,
             qx_ref, kx_ref, qt_ref, carry):
        j = pl.program_id(1)

        @pl.when(j == 0)
        def _():
            carry[...] = jnp.zeros_like(carry)

        _, xh = _rms(x_ref[...])
        h = (xh * g_ref[...]).astype(BF16)
        h_ref[...] = h
        u_ref[...] = _dot_nt(h, w_ref[0:512, :])
        gt_ref[...] = _dot_nt(h, w_ref[512:1024, :])
        qf = _dot_nt(h, w_ref[1024:1536, :])
        qb = qf.astype(BF16)
        kb = _dot_nt(h, w_ref[1536:2048, :]).astype(BF16)
        q_ref[...] = qb
        k_ref[...] = kb
        for t in range(TB // FOX_T):
            qt_ref[0, t] = qf[t * FOX_T:(t + 1) * FOX_T, :].T.astype(BF16)
        v_ref[...] = _dot_nt(h, w_ref[2048:2560, :]).astype(BF16)
        zf = _dot_nt(h, wf_ref[...]) + bf_ref[...]
        zf_ref[...] = zf
        lane = lax.broadcasted_iota(jnp.int32, zf.shape, 1)
        logf = jnp.where(lane < 8, jnp.minimum(zf, 0.0) - jnp.log(1.0 + jnp.exp(-jnp.abs(zf))), 0.0)
        row = lax.broadcasted_iota(jnp.int32, (TB, TB), 0)
        col = lax.broadcasted_iota(jnp.int32, (TB, TB), 1)
        c = _dot_01(row >= col, logf) + carry[0:1, :]
        carry[0:1, :] = c[TB - 1:TB, :]
        c_ref[...] = c
        cq = _dot_01(c, _head_expand(LANES, FOX_W))
        cq_ref[...] = cq
        hl = lax.broadcasted_iota(jnp.int32, (TB, LANES), 1)
        for hd in range(2 * N_PAIR):
            grp = slice((hd // 2) * LANES, (hd // 2 + 1) * LANES)
            swap = (lambda t: t) if hd % 2 == 0 else (lambda t: pltpu.roll(t, HEAD_D, 1))
            qf = swap(qb[:, grp].astype(F32) * (1.0 / math.sqrt(HEAD_D)))
            kf = swap(kb[:, grp].astype(F32))
            cv = cq[:, grp] if hd % 2 == 1 else pltpu.roll(cq[:, grp], HEAD_D, 1)
            hi = cv.astype(BF16).astype(F32)
            mid = (cv - hi).astype(BF16).astype(F32)
            lo = (cv - hi - mid).astype(BF16).astype(F32)
            pick = lambda a, b, c3, one_from, one_to: jnp.where(hl == a[0], a[1], jnp.where(hl == b[0], b[1], jnp.where(
                hl == c3[0], c3[1], jnp.where((hl >= one_from) & (hl < one_to), 1.0, 0.0))))
            qx = jnp.where(hl < HEAD_D, qf, pick((67, hi), (68, mid), (69, lo), 64, 67))
            kx = jnp.where(hl < HEAD_D, kf, pick((64, -hi), (65, -mid), (66, -lo), 67, 70))
            qx_ref[:, hd * LANES:(hd + 1) * LANES] = qx.astype(BF16)
            kx_ref[:, hd * LANES:(hd + 1) * LANES] = kx.astype(BF16)

    tok = lambda w: pl.BlockSpec((TB, w), lambda b, j: (b * nb + j, 0))
    outs = [(D, BF16), (512, F32), (512, F32), (512, BF16), (512, BF16), (512, BF16), (LANES, F32),
            (LANES, F32), (FOX_W, F32), (2 * FOX_W, BF16), (2 * FOX_W, BF16)]
    return _call(
        body, comm, name="fwd_in", grid=(B, nb),
        in_specs=[tok(D), _resident(g_mix), _resident(w_int), _resident(w_ft), _resident(bf_pad)],
        out_specs=[tok(w) for w, _ in outs] + [_feat_major_spec(TB, FOX_T, nb)],
        out_shape=[_sds((T, w), dt) for w, dt in outs] + [_sds((B, S // FOX_T, FOX_W, FOX_T), BF16)],
        scratch_shapes=[pltpu.VMEM((8, LANES), F32)],
        args=(x2, g_mix, w_int, w_ft, bf_pad))


def _head_sum(n):
    hc = lax.broadcasted_iota(jnp.int32, (n, n), 1) // HEAD_D
    hr = lax.broadcasted_iota(jnp.int32, (n, n), 0) // HEAD_D
    return hc == hr


def _layernorm_silu(y, lg, lb):
    mu = jnp.mean(y, axis=-1, keepdims=True)
    yc = y - mu
    rs = lax.rsqrt(jnp.mean(yc * yc, axis=-1, keepdims=True) + EPS)
    n = yc * rs
    l = n * lg + lb
    return rs, n, l


SUB = 8


def _shifted_copies(cat, sh, rows):
    for r in range(1, SUB):
        sh[r, 0:rows, :] = cat[r:r + rows, :]


def _tap(cat, sh, off, rows, cols=slice(None)):
    r = off % SUB
    return cat[off:off + rows, cols] if r == 0 else sh[r, off - r:off - r + rows, cols]


CONV_ROWS = 128


def _conv_pieces(CB):
    rows = min(CONV_ROWS, CB)
    return [(r0, rows, slice(c0, c0 + LANES)) for c0 in range(0, CONV_CH, LANES) for r0 in range(0, CB, rows)]


def _conv_fwd(u, gt, cw, cb, lng, lnb, B, S, comm=None):
    T = B * S
    CB = min(256, S)
    nb = S // CB

    def body(u_ref, gt_ref, w_ref, cb_ref, lg_ref, lb_ref, y_ref, co_ref, acat, ash):
        j = pl.program_id(1)

        @pl.when(j == 0)
        def _():
            acat[0:CONV_HALO, :] = jnp.zeros((CONV_HALO, CONV_CH), F32)

        acat[CONV_HALO:CONV_HALO + CB, :] = u_ref[...] * _sig(gt_ref[...])
        _shifted_copies(acat, ash, CB + CONV_HALO - SUB)
        for r0, rows, cs in _conv_pieces(CB):
            acc = jnp.zeros((rows, LANES), F32) + cb_ref[:, cs]
            for k in range(CONV_K):
                acc = acc + w_ref[k:k + 1, cs] * _tap(acat, ash, r0 + CONV_HALO - (CONV_K - 1) + k, rows, cs)
            y_ref[r0:r0 + rows, cs] = acc
        acat[0:CONV_HALO, :] = acat[CB:CB + CONV_HALO, :]
        _, _, l = _layernorm_silu(y_ref[...], lg_ref[...], lb_ref[...])
        co_ref[...] = (l * _sig(l)).astype(BF16)

    tok = lambda w: pl.BlockSpec((CB, w), lambda b, j: (b * nb + j, 0))
    return _call(
        body, comm, name="conv_fwd", grid=(B, nb),
        in_specs=[tok(CONV_CH), tok(CONV_CH), _resident(cw), _resident(cb), _resident(lng), _resident(lnb)],
        out_specs=[tok(CONV_CH), tok(CONV_CH)],
        out_shape=[_sds((T, CONV_CH), F32), _sds((T, CONV_CH), BF16)],
        scratch_shapes=[pltpu.VMEM((CONV_HALO + CB, CONV_CH), F32),
                        pltpu.VMEM((SUB, CB + CONV_HALO - SUB, CONV_CH), F32)],
        args=(u, gt, cw, cb, lng, lnb))


def _fox_fwd(qx, kx, v, cq, B, S, comm=None):
    T = B * S
    TQ = min(256, S)
    nq = S // TQ

    def body(qa_ref, qb_ref, ka_ref, kb_ref, v_ref, cq_ref, o_ref, lse_ref, s_scr, s_odd, m_scr, acc_scr):
        i = pl.program_id(2)
        lane = lax.broadcasted_iota(jnp.int32, (TQ, LANES), 1)
        lo = lane < HEAD_D
        qh = (qa_ref[...], qb_ref[...])
        kh = (ka_ref, kb_ref)
        m_scr[...] = jnp.full(m_scr.shape, -1e30, F32)
        acc_scr[...] = jnp.zeros_like(acc_scr)
        row = lax.broadcasted_iota(jnp.int32, (TQ, TQ), 0)
        col = lax.broadcasted_iota(jnp.int32, (TQ, TQ), 1)
        wide = lambda x: jnp.concatenate([x, x], axis=1) if TQ == 2 * LANES else jnp.tile(x, (1, TQ // LANES))

        def scores(j, s_buf):
            start = pl.multiple_of(j * TQ, TQ)
            for h in range(2):
                s_buf[h] = _dot_nt(qh[h], kh[h][pl.ds(start, TQ), :])

        def softmax_step(j, s_buf, diagonal):
            start = pl.multiple_of(j * TQ, TQ)
            vj = v_ref[pl.ds(start, TQ), :]
            for h in range(2):
                def logits():
                    return jnp.where(col <= row, s_buf[h], -1e30) if diagonal else s_buf[h]

                m_old = m_scr[h]
                m_new = jnp.maximum(m_old, jnp.max(logits(), axis=-1, keepdims=True))
                alpha = jnp.exp(m_old - m_new)
                m_scr[h] = m_new
                p = jnp.exp(logits() - wide(m_new)).astype(BF16)
                vx = jnp.where(lo if h == 0 else ~lo, vj, jnp.ones_like(vj))
                acc_scr[h] = alpha * acc_scr[h] + _dot(p, vx)

        def two_blocks(jj, carry):
            j = 2 * jj
            scores(j + 1, s_odd)
            softmax_step(j, s_scr, False)
            scores(j + 2, s_scr)
            softmax_step(j + 1, s_odd, False)
            return carry

        scores(0, s_scr)
        lax.fori_loop(0, i // 2, two_blocks, 0)

        @pl.when(i % 2 == 0)
        def _():
            softmax_step(i, s_scr, True)

        @pl.when(i % 2 == 1)
        def _():
            scores(i, s_odd)
            softmax_step(i - 1, s_scr, False)
            softmax_step(i, s_odd, True)

        acc = jnp.where(lo, acc_scr[0], acc_scr[1])
        den = pltpu.roll(jnp.where(lo, acc_scr[1], acc_scr[0]), HEAD_D, 1)
        o_ref[...] = acc / den
        lse_ref[...] = cq_ref[...] - (jnp.where(lo, m_scr[0], m_scr[1]) + jnp.log(den))

    qspec = pl.BlockSpec((TQ, LANES), lambda b, p, i: (b * nq + i, p))
    kspec = pl.BlockSpec((S, LANES), lambda b, p, i: (b, p))
    qhead = lambda h: pl.BlockSpec((TQ, LANES), lambda b, p, i: (b * nq + i, 2 * p + h))
    khead = lambda h: pl.BlockSpec((S, LANES), lambda b, p, i: (b, 2 * p + h))
    return _call(
        body, comm, name="fox_fwd", grid=(B, N_PAIR, nq),
        in_specs=[qhead(0), qhead(1), khead(0), khead(1), kspec, qspec],
        out_specs=[qspec, qspec],
        out_shape=[_sds((T, FOX_W), F32), _sds((T, FOX_W), F32)],
        scratch_shapes=[pltpu.VMEM((2, TQ, TQ), F32), pltpu.VMEM((2, TQ, TQ), F32),
                        pltpu.VMEM((2, TQ, LANES), F32), pltpu.VMEM((2, TQ, LANES), F32)],
        args=(qx, qx, kx, kx, v, cq))


def _mem_kv(mem2, g_mem, w_mkv, B):
    def body(m_ref, g_ref, w_ref, mn_ref, km_ref, vm_ref):
        _, xh = _rms(m_ref[...])
        mn = (xh * g_ref[...]).astype(BF16)
        mn_ref[...] = mn
        for s in range(2):
            km_ref[:, 512 * s:512 * (s + 1)] = _dot(mn, w_ref[s]).astype(BF16)
            vm_ref[:, 512 * s:512 * (s + 1)] = _dot(mn, w_ref[2 + s]).astype(BF16)

    blk = pl.BlockSpec((MEM_LEN, D), lambda b: (b, 0))
    return pl.pallas_call(
        body, name="mem_kv", grid=(B,),
        in_specs=[blk, _resident(g_mem), _resident(w_mkv)],
        out_specs=[blk, blk, blk],
        out_shape=[_sds((B * MEM_LEN, D), BF16)] * 3,
        compiler_params=_params(1),
    )(mem2, g_mem, w_mkv)


def _mem_probs(qm, km):
    ps = []
    for h in range(MEM_HEADS):
        hs = slice(h * MEM_HD, (h + 1) * MEM_HD)
        lg = _dot_nt(qm[:, hs], km[:, hs]) * (1.0 / math.sqrt(MEM_HD))
        e = jnp.exp(lg - jnp.max(lg, axis=-1, keepdims=True))
        ps.append(e / jnp.sum(e, axis=-1, keepdims=True))
    return ps


def _fwd_mid(x2, co, o, km, vm, w_out, w_mq, w_mo, g_x, B, S, comm=None):
    T = B * S
    TB = min(512, S)
    nb = S // TB

    def body(x_ref, co_ref, o_ref, km_ref, vm_ref, wo_ref, wq_ref, wm_ref, g_ref,
             x1_ref, hx_ref, qm_ref, om_ref, x2_ref, cat_ref):
        cat_ref[:, 0:CONV_CH] = co_ref[...]
        cat_ref[:, CONV_CH:D] = o_ref[...].astype(BF16)
        x1 = x_ref[...] + _dot(cat_ref[...], wo_ref[...])
        x1_ref[...] = x1
        _, xh = _rms(x1)
        hx = (xh * g_ref[...]).astype(BF16)
        hx_ref[...] = hx
        qm = _dot(hx, wq_ref[...]).astype(BF16)
        qm_ref[...] = qm
        ps = _mem_probs(qm, km_ref[...])
        vmv = vm_ref[...]
        for h in range(MEM_HEADS):
            hs = slice(h * MEM_HD, (h + 1) * MEM_HD)
            om_ref[:, hs] = _dot(ps[h].astype(BF16), vmv[:, hs]).astype(BF16)
        x2_ref[...] = x1 + _dot(om_ref[...], wm_ref[...])

    tok = lambda w: pl.BlockSpec((TB, w), lambda b, j: (b * nb + j, 0))
    memb = pl.BlockSpec((MEM_LEN, D), lambda b, j: (b, 0))
    outs = [(D, F32), (D, BF16), (D, BF16), (D, BF16), (D, F32), (D, BF16)]
    return _call(
        body, comm, name="fwd_mid", grid=(B, nb),
        in_specs=[tok(D), tok(CONV_CH), tok(FOX_W), memb, memb, _resident(w_out), _resident(w_mq), _resident(w_mo),
                  _resident(g_x)],
        out_specs=[tok(w) for w, _ in outs],
        out_shape=[_sds((T, w), dt) for w, dt in outs],
        scratch_shapes=[],
        args=(x2, co, o, km, vm, w_out, w_mq, w_mo, g_x))


def _load_gate_up(wgu_hbm, wg, wu, sems):
    copies = [pltpu.make_async_copy(wgu_hbm.at[s], (wg if s < 2 else wu).at[:, pl.ds((s % 2) * FF_CHUNK, FF_CHUNK)],
                                    sems.at[s]) for s in range(4)]
    for cp in copies:
        cp.start()
    for cp in copies:
        cp.wait()


def _fwd_ffn(x2, tgt, w_gu, w_down, g_ffn, g_final, T):
    TB = min(256, T)
    nb = T // TB

    def body(x_ref, t_ref, wgu_ref, wd_ref, gf_ref, gl_ref, hf_ref, gu_ref, act_ref, dx3_ref, loss_ref, dgl_ref,
             wg, wu, sems):
        i = pl.program_id(0)

        @pl.when(i == 0)
        def _():
            _load_gate_up(wgu_ref, wg, wu, sems)
            loss_ref[...] = jnp.zeros_like(loss_ref)
            dgl_ref[...] = jnp.zeros_like(dgl_ref)

        x2v = x_ref[...]
        _, xh = _rms(x2v)
        hf = (xh * gf_ref[...]).astype(BF16)
        hf_ref[...] = hf
        g = _dot(hf, wg[...])
        u = _dot(hf, wu[...])
        gu_ref[:, 0:D_FF] = g
        gu_ref[:, D_FF:2 * D_FF] = u
        act = (g * _sig(g) * u).astype(BF16)
        act_ref[...] = act
        x3 = x2v + _dot(act, wd_ref[...])
        r3, xh3 = _rms(x3)
        gl = gl_ref[...]
        e = xh3 * gl - t_ref[...]
        loss_ref[...] += jnp.sum(e * e) * (0.5 / D)
        dy = e * (1.0 / D)
        dx3, dgl = _rms_bwd(dy, xh3, r3, gl)
        dx3_ref[...] = dx3
        dgl_ref[...] += jnp.sum(dgl, axis=0, keepdims=True)

    tok = lambda w: pl.BlockSpec((TB, w), lambda i: (i, 0))
    return pl.pallas_call(
        body, name="fwd_ffn", grid=(nb,),
        in_specs=[tok(D), tok(D), ANY, _resident(w_down), _resident(g_ffn), _resident(g_final)],
        out_specs=[tok(D), tok(2 * D_FF), tok(D_FF), tok(D), _acc_spec((1, LANES)), _acc_spec((1, D))],
        out_shape=[_sds((T, D), BF16), _sds((T, 2 * D_FF), F32), _sds((T, D_FF), BF16), _sds((T, D), F32),
                   _sds((1, LANES), F32), _sds((1, D), F32)],
        scratch_shapes=[pltpu.VMEM((D, D_FF), BF16), pltpu.VMEM((D, D_FF), BF16), pltpu.SemaphoreType.DMA((4,))],
        compiler_params=_params(1),
    )(x2, tgt, w_gu, w_down, g_ffn, g_final)


def _bwd_ffn(dx3, gu, x2, w_gu, w_down, g_ffn, T):
    TB = min(256, T)
    nb = T // TB

    def body(d_ref, gu_ref, x_ref, wgu_ref, wd_ref, gf_ref, dgu_ref, dx2_ref, dgf_ref, wg, wu, sems):
        i = pl.program_id(0)

        @pl.when(i == 0)
        def _():
            _load_gate_up(wgu_ref, wg, wu, sems)
            dgf_ref[...] = jnp.zeros_like(dgf_ref)

        dx3v = d_ref[...]
        db = dx3v.astype(BF16)
        dact = _dot_nt(db, wd_ref[...])
        g = gu_ref[:, 0:D_FF]
        u = gu_ref[:, D_FF:2 * D_FF]
        sg = _sig(g)
        dg = (dact * u * sg * (1.0 + g * (1.0 - sg))).astype(BF16)
        du = (dact * g * sg).astype(BF16)
        dgu_ref[:, 0:D_FF] = dg
        dgu_ref[:, D_FF:2 * D_FF] = du
        dhf = _dot_nt(dg, wg[...]) + _dot_nt(du, wu[...])
        r2, xh2 = _rms(x_ref[...])
        dx, dg_tok = _rms_bwd(dhf, xh2, r2, gf_ref[...])
        dx2_ref[...] = dx3v + dx
        dgf_ref[...] += jnp.sum(dg_tok, axis=0, keepdims=True)

    tok = lambda w: pl.BlockSpec((TB, w), lambda i: (i, 0))
    return pl.pallas_call(
        body, name="bwd_ffn", grid=(nb,),
        in_specs=[tok(D), tok(2 * D_FF), tok(D), ANY, _resident(w_down), _resident(g_ffn)],
        out_specs=[tok(2 * D_FF), tok(D), _acc_spec((1, D))],
        out_shape=[_sds((T, 2 * D_FF), BF16), _sds((T, D), F32), _sds((1, D), F32)],
        scratch_shapes=[pltpu.VMEM((D, D_FF), BF16), pltpu.VMEM((D, D_FF), BF16), pltpu.SemaphoreType.DMA((4,))],
        compiler_params=_params(1),
    )(dx3, gu, x2, w_gu, w_down, g_ffn)


def _bwd_mid(dx2, x1, qm, km, vm, o, w_mo, w_mq, w_out, g_x, B, S, comm=None):
    T = B * S
    TB = min(512, S)
    nb = S // TB
    FOX_T = min(256, S)
    inv = 1.0 / math.sqrt(MEM_HD)

    def body(d_ref, x1_ref, qm_ref, km_ref, vm_ref, o_ref, wm_ref, wq_ref, wo_ref, g_ref,
             dx1_ref, dqm_ref, dco_ref, do_ref, dd_ref, dkm_ref, dvm_ref, dgx_ref, dot_ref):
        b = pl.program_id(0)
        j = pl.program_id(1)

        @pl.when((b == 0) & (j == 0))
        def _():
            dgx_ref[...] = jnp.zeros_like(dgx_ref)

        @pl.when(j == 0)
        def _():
            dkm_ref[...] = jnp.zeros_like(dkm_ref)
            dvm_ref[...] = jnp.zeros_like(dvm_ref)

        dx2v = d_ref[...]
        dom = _dot_nt(dx2v.astype(BF16), wm_ref[...]).astype(BF16)
        qmv = qm_ref[...]
        kmv = km_ref[...]
        vmv = vm_ref[...]
        ps = _mem_probs(qmv, kmv)
        for h in range(MEM_HEADS):
            hs = slice(h * MEM_HD, (h + 1) * MEM_HD)
            p = ps[h]
            dp = _dot_nt(dom[:, hs], vmv[:, hs])
            ds = (p * (dp - jnp.sum(p * dp, axis=-1, keepdims=True))).astype(BF16)
            dqm_ref[:, hs] = (_dot(ds, kmv[:, hs]) * inv).astype(BF16)
            dkm_ref[:, hs] += _dot_tn(ds, qmv[:, hs]) * inv
            dvm_ref[:, hs] += _dot_tn(p.astype(BF16), dom[:, hs])
        dhx = _dot_nt(dqm_ref[...], wq_ref[...])
        r1, xh1 = _rms(x1_ref[...])
        dx, dg_tok = _rms_bwd(dhx, xh1, r1, g_ref[...])
        dx1 = dx2v + dx
        dx1_ref[...] = dx1
        dgx_ref[...] += jnp.sum(dg_tok, axis=0, keepdims=True)
        d1b = dx1.astype(BF16)
        dco_ref[...] = _dot_nt(d1b, wo_ref[0:CONV_CH, :])
        do = _dot_nt(d1b, wo_ref[CONV_CH:D, :])
        dob = do.astype(BF16)
        do_ref[...] = dob
        for t in range(TB // FOX_T):
            dot_ref[0, t] = do[t * FOX_T:(t + 1) * FOX_T, :].T.astype(BF16)
        dd_ref[...] = _dot_01(dob.astype(F32) * o_ref[...], _head_sum(FOX_W))

    tok = lambda w: pl.BlockSpec((TB, w), lambda b, j: (b * nb + j, 0))
    memb = pl.BlockSpec((MEM_LEN, D), lambda b, j: (b, 0))
    outs = [(D, F32), (D, BF16), (CONV_CH, F32), (FOX_W, BF16), (FOX_W, F32)]
    return _call(
        body, comm, name="bwd_mid", grid=(B, nb),
        in_specs=[tok(D), tok(D), tok(D), memb, memb, tok(FOX_W), _resident(w_mo), _resident(w_mq), _resident(w_out),
                  _resident(g_x)],
        out_specs=[tok(w) for w, _ in outs] + [memb, memb, _acc_spec((1, D)), _feat_major_spec(TB, FOX_T, nb)],
        out_shape=[_sds((T, w), dt) for w, dt in outs] + [_sds((B * MEM_LEN, D), F32)] * 2 + [_sds((1, D), F32)]
        + [_sds((B, S // FOX_T, FOX_W, FOX_T), BF16)],
        scratch_shapes=[],
        args=(dx2, x1, qm, km, vm, o, w_mo, w_mq, w_out, g_x))


def _mem_bwd(dkm, dvm, mem2, w_mkv, g_mem, B):
    def body(dk_ref, dv_ref, m_ref, w_ref, g_ref, dkv_ref, dg_ref):
        b = pl.program_id(0)

        @pl.when(b == 0)
        def _():
            dg_ref[...] = jnp.zeros_like(dg_ref)

        dk = dk_ref[...].astype(BF16)
        dv = dv_ref[...].astype(BF16)
        dkv_ref[:, 0:D] = dk
        dkv_ref[:, D:2 * D] = dv
        dmn = jnp.zeros((MEM_LEN, D), F32)
        for s in range(2):
            dmn = dmn + _dot_nt(dk[:, 512 * s:512 * (s + 1)], w_ref[s]) + _dot_nt(dv[:, 512 * s:512 * (s + 1)], w_ref[2 + s])
        _, xh = _rms(m_ref[...])
        dg_ref[...] += jnp.sum(dmn * xh, axis=0, keepdims=True)

    blk = pl.BlockSpec((MEM_LEN, D), lambda b: (b, 0))
    return pl.pallas_call(
        body, name="mem_bwd", grid=(B,),
        in_specs=[blk, blk, blk, _resident(w_mkv), _resident(g_mem)],
        out_specs=[pl.BlockSpec((MEM_LEN, 2 * D), lambda b: (b, 0)), _acc_spec((1, D))],
        out_shape=[_sds((B * MEM_LEN, 2 * D), BF16), _sds((1, D), F32)],
        compiler_params=_params(1),
    )(dkm, dvm, mem2, w_mkv, g_mem)


def _fox_bwd(q, k, v, do, bias, dd, ckT, qT, doT, B, S, comm=None):
    T = B * S
    TK = min(256, S)
    nk = S // TK
    scale = 1.0 / math.sqrt(HEAD_D)

    def body(q_ref, k_ref, v_ref, do_ref, bias_ref, dd_ref, ck_ref, qt_ref, dot_ref, dq_ref, dk_ref, dv_ref, dc_ref,
             dcq_ref, dq_acc, rs_acc, s_scr, dp_scr, s_odd, dp_odd, dk_acc, dv_acc, dc_acc):
        j = pl.program_id(2)

        @pl.when(j == 0)
        def _():
            dq_acc[...] = jnp.zeros_like(dq_acc)
            rs_acc[...] = jnp.zeros_like(rs_acc)

        dk_acc[...] = jnp.zeros_like(dk_acc)
        dv_acc[...] = jnp.zeros_like(dv_acc)
        dc_acc[...] = jnp.zeros_like(dc_acc)
        lane = lax.broadcasted_iota(jnp.int32, (TK, LANES), 1)
        lo = lane < HEAD_D
        ks = k_ref[...] * jnp.asarray(scale, BF16)
        v2 = v_ref[...]
        zero = jnp.zeros_like(ks)
        kh = (jnp.where(lo, ks, zero), jnp.where(lo, zero, ks))
        vh = (jnp.where(lo, v2, zero), jnp.where(lo, zero, v2))
        kstart = pl.multiple_of(j * TK, TK)
        ckh = tuple(ck_ref[0, 0, h:h + 1, pl.ds(kstart, TK)] for h in range(2))
        row = lax.broadcasted_iota(jnp.int32, (TK, TK), 0)
        col = lax.broadcasted_iota(jnp.int32, (TK, TK), 1)
        wide = lambda x: jnp.concatenate([x, x], axis=1) if TK == 2 * LANES else jnp.tile(x, (1, TK // LANES))

        def scores(i, s_buf, dp_buf):
            start = pl.multiple_of(i * TK, TK)
            qi = q_ref[pl.ds(start, TK), :]
            doi = do_ref[pl.ds(start, TK), :]
            for h in range(2):
                s_buf[h] = _dot_nt(qi, kh[h])
                dp_buf[h] = _dot_nt(doi, vh[h])

        def grads(i, s_buf, dp_buf, diagonal):
            start = pl.multiple_of(i * TK, TK)
            bias2 = bias_ref[pl.ds(start, TK), :]
            dd2 = dd_ref[pl.ds(start, TK), :]
            for h in range(2):
                hc = slice(h * HEAD_D, h * HEAD_D + 1)
                bias = jnp.broadcast_to(bias2[:, hc], (TK, LANES))
                ddh = jnp.broadcast_to(dd2[:, hc], (TK, LANES))
                p = jnp.exp((s_buf[h] - ckh[h]) + wide(bias))
                if diagonal:
                    p = jnp.where(col <= row, p, 0.0)
                ds = p * (dp_buf[h] - wide(ddh))
                dc_acc[h, 0:1, :] += jnp.sum(ds, axis=0, keepdims=True)
                rs_acc[h, pl.ds(start, TK), :] += jnp.sum(ds, axis=1, keepdims=True)
                pb = p.astype(BF16)
                dsb = ds.astype(BF16)
                feat = slice(h * HEAD_D, (h + 1) * HEAD_D)
                dv_acc[feat, :] += _dot(dot_ref[0, i, feat, :], pb)
                dk_acc[feat, :] += _dot(qt_ref[0, i, feat, :], dsb)
                dq_acc[pl.ds(start, TK), :] += _dot(dsb, kh[h])

        n_off = nk - 1 - j
        block = lambda t: jnp.where(t < n_off, j + 1 + t, j)

        def two_blocks(tt, carry):
            t = 2 * tt
            scores(block(t + 1), s_odd, dp_odd)
            grads(block(t), s_scr, dp_scr, False)
            scores(block(t + 2), s_scr, dp_scr)
            grads(block(t + 1), s_odd, dp_odd, False)
            return carry

        scores(block(0), s_scr, dp_scr)
        lax.fori_loop(0, n_off // 2, two_blocks, 0)

        @pl.when(n_off % 2 == 0)
        def _():
            grads(j, s_scr, dp_scr, True)

        @pl.when(n_off % 2 == 1)
        def _():
            scores(j, s_odd, dp_odd)
            grads(nk - 1, s_scr, dp_scr, False)
            grads(j, s_odd, dp_odd, True)

        dk_ref[...] = (dk_acc[...].T * scale).astype(BF16)
        dv_ref[...] = dv_acc[...].T.astype(BF16)
        sub = lax.broadcasted_iota(jnp.int32, (8, TK), 0)
        dca = dc_acc[0, 0:1, :]
        dcb = dc_acc[1, 0:1, :]
        dc_ref[0, 0] = jnp.where(sub == 0, -dca, jnp.where(sub == 1, -dcb, 0.0))

        @pl.when(j == nk - 1)
        def _():
            dq_ref[...] = dq_acc[...].astype(BF16)
            lo_s = lax.broadcasted_iota(jnp.int32, (S, LANES), 1) < HEAD_D
            dcq_ref[...] = jnp.where(lo_s, rs_acc[0], rs_acc[1])

    full = pl.BlockSpec((S, LANES), lambda b, p, j: (b, p))
    blk = pl.BlockSpec((TK, LANES), lambda b, p, j: (b * nk + j, p))
    featT = pl.BlockSpec((1, nk, LANES, TK), lambda b, p, j: (b, 0, p, 0))
    return _call(
        body, comm, name="fox_bwd", grid=(B, N_PAIR, nk),
        in_specs=[full, blk, blk, full, full, full, pl.BlockSpec((1, 1, 8, S), lambda b, p, j: (b, p, 0, 0)),
                  featT, featT],
        out_specs=[full, blk, blk, pl.BlockSpec((1, 1, 8, TK), lambda b, p, j: (b, p, 0, j)), full],
        out_shape=[_sds((T, FOX_W), BF16), _sds((T, FOX_W), BF16), _sds((T, FOX_W), BF16),
                   _sds((B, N_PAIR, 8, S), F32), _sds((T, FOX_W), F32)],
        scratch_shapes=[pltpu.VMEM((S, LANES), F32), pltpu.VMEM((2, S, 1), F32),
                        pltpu.VMEM((2, TK, TK), F32), pltpu.VMEM((2, TK, TK), F32),
                        pltpu.VMEM((2, TK, TK), F32), pltpu.VMEM((2, TK, TK), F32),
                        pltpu.VMEM((LANES, TK), F32), pltpu.VMEM((LANES, TK), F32), pltpu.VMEM((2, 8, TK), F32)],
        args=(q, k, v, do, bias, dd, ckT, qT, doT))


def _fgate_bwd(dc8, zf, B, S):
    T = B * S
    TB = min(512, S)
    nb = S // TB

    def body(dc_ref, zf_ref, dzf_ref, dbf_ref, carry):
        b = pl.program_id(0)
        j = pl.program_id(1)

        @pl.when((b == 0) & (j == 0))
        def _():
            dbf_ref[...] = jnp.zeros_like(dbf_ref)

        @pl.when(j == 0)
        def _():
            carry[...] = jnp.zeros_like(carry)

        dc = dc_ref[...]
        row = lax.broadcasted_iota(jnp.int32, (TB, TB), 0)
        col = lax.broadcasted_iota(jnp.int32, (TB, TB), 1)
        dlogf = _dot_01(col >= row, dc) + carry[0:1, :]
        carry[0:1, :] = dlogf[0:1, :]
        lane = lax.broadcasted_iota(jnp.int32, dc.shape, 1)
        dzf = jnp.where(lane < 8, dlogf * _sig(-zf_ref[...]), 0.0)
        dzf_ref[...] = dzf.astype(BF16)
        dbf_ref[...] += jnp.sum(dzf, axis=0, keepdims=True)

    tok = pl.BlockSpec((TB, LANES), lambda b, j: (b * nb + (nb - 1 - j), 0))
    return pl.pallas_call(
        body, name="fgate_bwd", grid=(B, nb),
        in_specs=[tok, tok],
        out_specs=[tok, _acc_spec((1, LANES))],
        out_shape=[_sds((T, LANES), BF16), _sds((1, LANES), F32)],
        scratch_shapes=[pltpu.VMEM((8, LANES), F32)],
        compiler_params=_params(2),
    )(dc8, zf)


def _conv_bwd(dco, y, u, gt, cw, lng, lnb, B, S, comm=None):
    T = B * S
    CB = min(256, S)
    nb = S // CB
    hb = CB // CONV_HALO

    def body(dco_ref, y_ref, u_ref, gt_ref, up_ref, gp_ref, w_ref, lg_ref, lb_ref,
             du_ref, dgt_ref, dw_ref, vec_ref, acat, dycat, ash, dysh):
        b = pl.program_id(0)
        j = pl.program_id(1)
        jr = nb - 1 - j

        @pl.when((b == 0) & (j == 0))
        def _():
            dw_ref[...] = jnp.zeros_like(dw_ref)
            vec_ref[...] = jnp.zeros_like(vec_ref)

        @pl.when(j == 0)
        def _():
            dycat[CB:CB + CONV_HALO, :] = jnp.zeros((CONV_HALO, CONV_CH), F32)

        lg = lg_ref[...]
        rs, n, l = _layernorm_silu(y_ref[...], lg, lb_ref[...])
        sg = _sig(l)
        dl = dco_ref[...] * (sg * (1.0 + l * (1.0 - sg)))
        dn = dl * lg
        dy = rs * (dn - jnp.mean(dn, axis=-1, keepdims=True) - n * jnp.mean(dn * n, axis=-1, keepdims=True))
        vec_ref[0:1, :] += jnp.sum(dy, axis=0, keepdims=True)
        vec_ref[1:2, :] += jnp.sum(dl * n, axis=0, keepdims=True)
        vec_ref[2:3, :] += jnp.sum(dl, axis=0, keepdims=True)
        dycat[0:CB, :] = dy
        acat[0:CONV_HALO, :] = jnp.where(jr > 0, up_ref[...] * _sig(gp_ref[...]), 0.0)
        acat[CONV_HALO:CONV_HALO + CB, :] = u_ref[...] * _sig(gt_ref[...])
        _shifted_copies(acat, ash, CB + CONV_HALO - SUB)
        _shifted_copies(dycat, dysh, CB + CONV_HALO - SUB)
        for r0, rows, cs in _conv_pieces(CB):
            dyp = dycat[r0:r0 + rows, cs]
            da = jnp.zeros((rows, LANES), F32)
            for k in range(CONV_K):
                da = da + w_ref[k:k + 1, cs] * _tap(dycat, dysh, r0 + CONV_K - 1 - k, rows, cs)
                dw_ref[k:k + 1, cs] += jnp.sum(dyp * _tap(acat, ash, r0 + CONV_HALO - (CONV_K - 1) + k, rows, cs),
                                               axis=0, keepdims=True)
            uv = u_ref[r0:r0 + rows, cs]
            sgt = _sig(gt_ref[r0:r0 + rows, cs])
            du_ref[r0:r0 + rows, cs] = (da * sgt).astype(BF16)
            dgt_ref[r0:r0 + rows, cs] = (da * uv * sgt * (1.0 - sgt)).astype(BF16)
        dycat[CB:CB + CONV_HALO, :] = dycat[0:CONV_HALO, :]

    tok = lambda w: pl.BlockSpec((CB, w), lambda b, j: (b * nb + (nb - 1 - j), 0))
    prev = pl.BlockSpec((CONV_HALO, CONV_CH), lambda b, j: (jnp.maximum((b * nb + (nb - 1 - j)) * hb - 1, 0), 0))
    return _call(
        body, comm, name="conv_bwd", grid=(B, nb),
        in_specs=[tok(CONV_CH), tok(CONV_CH), tok(CONV_CH), tok(CONV_CH), prev, prev, _resident(cw), _resident(lng),
                  _resident(lnb)],
        out_specs=[tok(CONV_CH), tok(CONV_CH), _acc_spec((CONV_HALO, CONV_CH)), _acc_spec((8, CONV_CH))],
        out_shape=[_sds((T, CONV_CH), BF16), _sds((T, CONV_CH), BF16), _sds((CONV_HALO, CONV_CH), F32),
                   _sds((8, CONV_CH), F32)],
        scratch_shapes=[pltpu.VMEM((CONV_HALO + CB, CONV_CH), F32), pltpu.VMEM((CB + CONV_HALO, CONV_CH), F32),
                        pltpu.VMEM((SUB, CB + CONV_HALO - SUB, CONV_CH), F32),
                        pltpu.VMEM((SUB, CB + CONV_HALO - SUB, CONV_CH), F32)],
        args=(dco, y, u, gt, u, gt, cw, lng, lnb))


def _bwd_in(dz, w_int, w_ft, x2, dx1, g_mix, T, comm=None):
    TB = min(512, T)
    nb = T // TB

    def body(dz_ref, w_ref, wf_ref, x_ref, d1_ref, g_ref, gx_ref, dg_ref):
        i = pl.program_id(0)

        @pl.when(i == 0)
        def _():
            dg_ref[...] = jnp.zeros_like(dg_ref)

        dh = _dot(dz_ref[:, 0:OFF_F], w_ref[0:OFF_F, :]) + _dot(dz_ref[:, OFF_F:D_IN_PAD], wf_ref[...])
        r0, xh0 = _rms(x_ref[...])
        dx, dg_tok = _rms_bwd(dh, xh0, r0, g_ref[...])
        gx_ref[...] = d1_ref[...] + dx
        dg_ref[...] += jnp.sum(dg_tok, axis=0, keepdims=True)

    tok = lambda w: pl.BlockSpec((TB, w), lambda i: (i, 0))
    return _call(
        body, comm, name="bwd_in", grid=(nb,),
        in_specs=[tok(D_IN_PAD), _resident(w_int), _resident(w_ft), tok(D), tok(D), _resident(g_mix)],
        out_specs=[tok(D), _acc_spec((1, D))],
        out_shape=[_sds((T, D), F32), _sds((1, D), F32)],
        scratch_shapes=[],
        args=(dz, w_int, w_ft, x2, dx1, g_mix))


def _dw(a, b, name, tn, slabs=0, tk=None, rows=None, tt=1024):
    T, K = a.shape
    N = b.shape[1]
    per = tn // slabs if slabs else 1
    tk = tk or (K if K <= 1024 else K // 2)
    tt = min(tt, T)
    nt = T // tt

    def body(a_ref, b_ref, o_ref, acc):
        t = pl.program_id(2)

        @pl.when(t == 0)
        def _():
            acc[...] = jnp.zeros_like(acc)

        acc[...] += _dot_tn(a_ref[...].astype(BF16), b_ref[...].astype(BF16))

        @pl.when(t == nt - 1)
        def _():
            if slabs:
                for sl in range(per):
                    o_ref[sl] = acc[:, sl * slabs:(sl + 1) * slabs]
            else:
                o_ref[...] = acc[...]

    return pl.pallas_call(
        body, name=name, grid=(K // tk, N // tn, nt),
        in_specs=[pl.BlockSpec((tt, tk), lambda i, j, t: (t, i)), pl.BlockSpec((tt, tn), lambda i, j, t: (t, j))],
        out_specs=(pl.BlockSpec((per, tk, slabs), lambda i, j, t: (j, i, 0)) if slabs
                   else pl.BlockSpec((tk, tn), lambda i, j, t: (i, j))),
        out_shape=_sds((N // slabs, K, slabs) if slabs else (rows or K, N), F32),
        scratch_shapes=[pltpu.VMEM((tk, tn), F32)],
        compiler_params=_params(3),
    )(a, b)


def _pos():
    return lax.axis_index("x"), lax.axis_index("y"), lax.axis_index("c")


def _remote(src, dst, ssem, rsem, to):
    return pltpu.make_async_remote_copy(src_ref=src, dst_ref=dst, send_sem=ssem, recv_sem=rsem, device_id=to,
                                        device_id_type=MESH)


def _split_axis(shape):
    return 0 if shape[0] % 32 == 0 else 1


def _half_shape(shape, parts=2):
    return (shape[0] // parts, shape[1]) if _split_axis(shape) == 0 else (shape[0], shape[1] // parts)


def _half(shape, c):
    R, C = shape
    if _split_axis(shape) == 0:
        return (pl.ds(pl.multiple_of(c * (R // 2), 16), R // 2), slice(None))
    return (slice(None), pl.ds(pl.multiple_of(c * (C // 2), LANES), C // 2))


def _half_block(shape, parts, lead, which):
    blk = _half_shape(shape, parts)
    idx = (which, 0) if _split_axis(shape) == 0 else (0, which)
    return blk, tuple(lead) + idx


class _Comm:
    def __init__(self, ins, out_shapes, sems, start, finish):
        self.ins, self.out_shapes, self.sems, self.start, self.finish = list(ins), list(out_shapes), list(sems), start, finish


def _ag_comm(shards):
    n = len(shards)

    def parts(ins, outs, sems):
        send_sems, recv_sems, local_sems = sems
        x, y, c = _pos()
        me, sib = (x, y, c), (x, y, 1 - c)
        chips = [(1 - x, y), (x, 1 - y), (1 - x, 1 - y)]

        def rows(w, px, py, pc):
            return outs[w].at[(2 * px + py,) + _half(shards[w].shape, pc)]

        def copy(w, k, block, to, src=None):
            return _remote(rows(w, *block) if src is None else src, rows(w, *block), send_sems.at[w, k],
                           recv_sems.at[w, k], to)

        mine, first = [], []
        for w in range(n):
            src = ins[w].at[_half(shards[w].shape, c)]
            mine.append(pltpu.make_async_copy(src, rows(w, *me), local_sems.at[w]))
            first += [copy(w, 0, me, sib, src=src)] + [copy(w, 1 + j, me, (*chip, c), src=src) for j, chip in enumerate(chips)]
        return c, me, sib, chips, copy, mine, first

    def start(ins, outs, sems):
        _, _, _, _, _, mine, first = parts(ins, outs, sems)
        for cp in mine + first:
            cp.start()

    def finish(ins, outs, sems):
        c, me, sib, chips, copy, mine, first = parts(ins, outs, sems)
        passed = []
        for w in range(n):
            for j, chip in enumerate(chips):
                copy(w, 1 + j, (*chip, c), me).wait_recv()
                passed.append(copy(w, 4 + j, (*chip, c), sib))
                passed[-1].start()
        for w in range(n):
            copy(w, 0, sib, me).wait_recv()
            for j, chip in enumerate(chips):
                copy(w, 4 + j, (*chip, 1 - c), me).wait_recv()
        for cp in first + passed:
            cp.wait_send()
        for cp in mine:
            cp.wait()

    D7 = pltpu.SemaphoreType.DMA((n, 7))
    return _Comm(shards, [_sds((4,) + s.shape, s.dtype) for s in shards], [D7, D7, pltpu.SemaphoreType.DMA((n,))],
                 start, finish)


def _sibling_comm(gs):
    n = len(gs)

    def copies(ins, outs, sems):
        send_sems, recv_sems = sems
        x, y, c = _pos()
        return [_remote(ins[w].at[(s,) + _half(gs[w].shape[1:], 1 - c)], outs[w].at[s], send_sems.at[w, s],
                        recv_sems.at[w, s], (x, y, 1 - c)) for w in range(n) for s in range(4)]

    def start(ins, outs, sems):
        for cp in copies(ins, outs, sems):
            cp.start()

    def finish(ins, outs, sems):
        for cp in copies(ins, outs, sems):
            cp.wait()

    D4 = pltpu.SemaphoreType.DMA((n, 4))
    return _Comm(gs, [_sds((4,) + _half_shape(g.shape[1:]), F32) for g in gs], [D4, D4], start, finish)


def _ici_comm(pbs):
    n = len(pbs)

    def copies(ins, outs, sems):
        send_sems, recv_sems = sems
        x, y, c = _pos()
        return [_remote(ins[w].at[2 * tx + ty], outs[w].at[j], send_sems.at[w, j], recv_sems.at[w, j], (tx, ty, c))
                for w in range(n) for j, (tx, ty) in enumerate([(1 - x, y), (x, 1 - y), (1 - x, 1 - y)])]

    def start(ins, outs, sems):
        for cp in copies(ins, outs, sems):
            cp.start()

    def finish(ins, outs, sems):
        for cp in copies(ins, outs, sems):
            cp.wait()

    D3 = pltpu.SemaphoreType.DMA((n, 3))
    return _Comm(pbs, [_sds((3,) + p.shape[1:], BF16) for p in pbs], [D3, D3], start, finish)


def _join(*comms):
    counts = [(len(c.ins), len(c.out_shapes), len(c.sems)) for c in comms]

    def each(which):
        def run(ins, outs, sems):
            i = o = k = 0
            for c, (ni, no, nk) in zip(comms, counts):
                getattr(c, which)(ins[i:i + ni], outs[o:o + no], sems[k:k + nk])
                i, o, k = i + ni, o + no, k + nk
        return run

    return _Comm(sum((c.ins for c in comms), []), sum((c.out_shapes for c in comms), []),
                 sum((c.sems for c in comms), []), each("start"), each("finish"))


def _run_comm(comm, name):
    ni, no = len(comm.ins), len(comm.out_shapes)

    def body(*refs):
        ins, outs, sems = refs[:ni], refs[ni:ni + no], refs[ni + no:]
        comm.start(ins, outs, sems)
        comm.finish(ins, outs, sems)

    return pl.pallas_call(body, name=name, out_shape=comm.out_shapes, in_specs=[ANY] * ni, out_specs=[ANY] * no,
                          scratch_shapes=comm.sems)(*comm.ins)


def _call(body, comm, *, name, grid, in_specs, out_specs, out_shape, scratch_shapes, args):
    n_grid = len(grid)
    if comm is None:
        res = pl.pallas_call(body, name=name, grid=grid, in_specs=in_specs, out_specs=out_specs, out_shape=out_shape,
                             scratch_shapes=scratch_shapes, compiler_params=_params(n_grid))(*args)
        return list(res), []
    n_in, n_out, n_scr = len(in_specs), len(out_specs), len(scratch_shapes)
    ni, no = len(comm.ins), len(comm.out_shapes)

    def carried(*refs):
        ins, refs = refs[:n_in], refs[n_in:]
        cins, refs = refs[:ni], refs[ni:]
        outs, refs = refs[:n_out], refs[n_out:]
        couts, refs = refs[:no], refs[no:]
        scr, csems = refs[:n_scr], refs[n_scr:]
        ids = [pl.program_id(ax) for ax in range(n_grid)]
        first = functools.reduce(jnp.logical_and, [i == 0 for i in ids])
        last = functools.reduce(jnp.logical_and, [i == g - 1 for i, g in zip(ids, grid)])

        @pl.when(first)
        def _():
            comm.start(cins, couts, csems)

        body(*ins, *outs, *scr)

        @pl.when(last)
        def _():
            comm.finish(cins, couts, csems)

    res = pl.pallas_call(
        carried, name=name, grid=grid, in_specs=list(in_specs) + [ANY] * ni, out_specs=list(out_specs) + [ANY] * no,
        out_shape=list(out_shape) + comm.out_shapes, scratch_shapes=list(scratch_shapes) + comm.sems,
        compiler_params=_params(n_grid))(*args, *comm.ins)
    return list(res[:n_out]), list(res[n_out:])


def _small_allreduce(v, name, halves=()):
    P = v.shape[0]
    n = len(halves)
    vm = pl.BlockSpec(memory_space=pltpu.VMEM)

    def body(v_ref, *refs):
        o_ref, outs = refs[n], refs[n + 1:2 * n + 1]
        gath, send_sems, recv_sems, half_send, half_recv = refs[2 * n + 1:]
        x, y, c = _pos()
        me = 4 * x + 2 * y + c
        gath[me] = v_ref[...]
        cps = []
        for r in range(1, 8):
            tx = (1 - x) if r & 4 else x
            ty = (1 - y) if r & 2 else y
            tc = (1 - c) if r & 1 else c
            cps.append(_remote(v_ref, gath.at[me], send_sems.at[r - 1], recv_sems.at[r - 1], (tx, ty, tc)))
        for w in range(n):
            mine = outs[w].at[_half(halves[w].shape, c)]
            cps.append(_remote(mine, mine, half_send.at[w], half_recv.at[w], (x, y, 1 - c)))
        for cp in cps:
            cp.start()
        for cp in cps:
            cp.wait()
        acc = gath[0]
        for d in range(1, 8):
            acc = acc + gath[d]
        o_ref[...] = acc

    res = pl.pallas_call(
        body, name=name, out_shape=[_sds((P, LANES), F32)] + [_sds(g.shape, F32) for g in halves],
        in_specs=[vm] + [ANY] * n, out_specs=[vm] + [ANY] * n, input_output_aliases={1 + w: 1 + w for w in range(n)},
        scratch_shapes=[pltpu.VMEM((8, P, LANES), F32), pltpu.SemaphoreType.DMA((7,)), pltpu.SemaphoreType.DMA((7,)),
                        pltpu.SemaphoreType.DMA((max(n, 1),)), pltpu.SemaphoreType.DMA((max(n, 1),))],
    )(v, *halves)
    return res[0], list(res[1:])


def _chip_sum(gs, rcvs, pos, name):
    n = len(gs)
    shards = [g.shape[1:] for g in gs]
    hss = [_half_shape(sh) for sh in shards]
    other = lambda i, pos: (pos[1] + 1 + i) % 4

    def body(pos_ref, *refs):
        for w in range(n):
            refs[2 * n + w][...] = (refs[w][...] + refs[n + w][...]).astype(BF16)

    mine = lambda w: pl.BlockSpec((1,) + hss[w], lambda i, pos: _half_block(shards[w], 2, (other(i, pos),), pos[0])[1])
    whole = lambda w: pl.BlockSpec((1,) + hss[w], lambda i, pos: (other(i, pos), 0, 0))
    return pl.pallas_call(
        body, name=name, out_shape=[_sds((4,) + hs, BF16) for hs in hss],
        grid_spec=pltpu.PrefetchScalarGridSpec(
            num_scalar_prefetch=1, grid=(3,),
            in_specs=[mine(w) for w in range(n)] + [whole(w) for w in range(n)],
            out_specs=[whole(w) for w in range(n)]),
        compiler_params=_params(1),
    )(pos, *gs, *rcvs)


def _final_sum(gs, rcvs, rcs, pos, name):
    n = len(gs)
    shards = [g.shape[1:] for g in gs]
    qss = [_half_shape(sh, 4) for sh in shards]

    def body(pos_ref, *refs):
        for w in range(n):
            acc = refs[w][0] + refs[n + w][0]
            for j in range(3):
                acc = acc + refs[2 * n + w][j].astype(F32)
            refs[3 * n + w][...] = acc

    def spec(w, lead_block, lead_index, mine):
        return pl.BlockSpec(lead_block + qss[w], lambda i, pos: _half_block(
            shards[w], 4, lead_index(pos), pos[0] * 2 + i if mine else i)[1])

    own_slab, first, none = (lambda pos: (pos[1],)), (lambda pos: (0,)), (lambda pos: ())
    return pl.pallas_call(
        body, name=name, out_shape=[_sds(sh, F32) for sh in shards],
        grid_spec=pltpu.PrefetchScalarGridSpec(
            num_scalar_prefetch=1, grid=(2,),
            in_specs=[spec(w, (1,), own_slab, True) for w in range(n)] + [spec(w, (1,), own_slab, False) for w in range(n)]
            + [spec(w, (3,), first, False) for w in range(n)],
            out_specs=[spec(w, (), none, True) for w in range(n)]),
        compiler_params=_params(1),
    )(pos, *gs, *rcvs, *rcs)


def _adamw_math(w, g, m, v):
    m = ADAM_B1 * m + (1.0 - ADAM_B1) * g
    v = ADAM_B2 * v + (1.0 - ADAM_B2) * (g * g)
    m_hat = m / (1.0 - ADAM_B1 ** ADAM_STEP)
    v_hat = v / (1.0 - ADAM_B2 ** ADAM_STEP)
    delta = -ADAM_LR * (m_hat / (jnp.sqrt(v_hat) + ADAM_EPS) + ADAM_WD * w)
    return delta, m, v


ADAM_PARTS = 8


def _adamw(ws, gs, ms, vs, name, parts=ADAM_PARTS):
    n = len(ws)
    shapes = [w.shape for w in ws]

    def body(*refs):
        for k in range(n):
            w_ref, g_ref, m_ref, v_ref = (refs[j * n + k] for j in range(4))
            go_ref, d_ref, nm_ref, nv_ref = (refs[(4 + j) * n + k] for j in range(4))
            g = g_ref[...]
            d, nm, nv = _adamw_math(w_ref[...], g, m_ref[...], v_ref[...])
            go_ref[...] = g
            d_ref[...] = d
            nm_ref[...] = nm
            nv_ref[...] = nv

    blk = lambda k: pl.BlockSpec(_half_shape(shapes[k], parts), lambda i: _half_block(shapes[k], parts, (), i)[1])
    res = pl.pallas_call(
        body, name=name, grid=(parts,), in_specs=[blk(k) for k in range(n)] * 4, out_specs=[blk(k) for k in range(n)] * 4,
        out_shape=[_sds(sh, F32) for sh in shapes] * 4, compiler_params=_params(1),
    )(*ws, *gs, *ms, *vs)
    return [tuple(res[j * n + k] for j in range(4)) for k in range(n)]


SMALL = (("g_mix", 8), ("b_f", 8), ("conv_w", None), ("conv_b", 8), ("ln_g", 8), ("ln_b", 8), ("g_x", 8), ("g_mem", 8),
         ("g_ffn", 8), ("g_final", 8), ("loss", 8))


def _pack_small(parts, conv_rows):
    rows = []
    for name, n in SMALL:
        if name not in parts:
            continue
        n = conv_rows if n is None else n
        flat = parts[name].reshape(-1).astype(F32)
        flat = jnp.pad(flat, (0, n * LANES - flat.shape[0]))
        rows.append(flat.reshape(n, LANES))
    return jnp.concatenate(rows, axis=0)


def _unpack_small(p, shapes, conv_rows):
    out, off = {}, 0
    for name, n in SMALL:
        if name not in shapes:
            continue
        n = conv_rows if n is None else n
        size = math.prod(shapes[name])
        out[name] = p[off:off + n].reshape(-1)[:size].reshape(shapes[name])
        off += n
    return out


def kernel(x, mem, g_mix, w_in, b_f, conv_w, conv_b, ln_g, ln_b, w_out, g_x, g_mem, w_mq, w_mkv, w_mo, g_ffn, w_gu, w_down, g_final, loss_target, m_g_mix, m_w_in, m_b_f, m_conv_w, m_conv_b, m_ln_g, m_ln_b, m_w_out, m_g_x, m_g_mem, m_w_mq, m_w_mkv, m_w_mo, m_g_ffn, m_w_gu, m_w_down, m_g_final, v_g_mix, v_w_in, v_b_f, v_conv_w, v_conv_b, v_ln_g, v_ln_b, v_w_out, v_g_x, v_g_mem, v_w_mq, v_w_mkv, v_w_mo, v_g_ffn, v_w_gu, v_w_down, v_g_final):
    names = ["g_mix", "w_in", "b_f", "conv_w", "conv_b", "ln_g", "ln_b", "w_out", "g_x", "g_mem", "w_mq", "w_mkv",
             "w_mo", "g_ffn", "w_gu", "w_down", "g_final"]
    W = dict(zip(names, [g_mix, w_in, b_f, conv_w, conv_b, ln_g, ln_b, w_out, g_x, g_mem, w_mq, w_mkv, w_mo, g_ffn,
                         w_gu, w_down, g_final]))
    M = dict(zip(names, [m_g_mix, m_w_in, m_b_f, m_conv_w, m_conv_b, m_ln_g, m_ln_b, m_w_out, m_g_x, m_g_mem, m_w_mq,
                         m_w_mkv, m_w_mo, m_g_ffn, m_w_gu, m_w_down, m_g_final]))
    V = dict(zip(names, [v_g_mix, v_w_in, v_b_f, v_conv_w, v_conv_b, v_ln_g, v_ln_b, v_w_out, v_g_x, v_g_mem, v_w_mq,
                         v_w_mkv, v_w_mo, v_g_ffn, v_w_gu, v_w_down, v_g_final]))
    big_names = [n for n, _, _, _ in BIG]
    B, S, _ = x.shape
    T = B * S
    mx, my, mc = _pos()
    chip = 2 * mx + my
    pos = jnp.stack([mc, chip]).astype(jnp.int32)

    shard2d = lambda a: a.reshape(a.shape[-2], a.shape[-1])
    big2d = lambda d, n: shard2d(d[n]).T if n == "w_in" else shard2d(d[n])
    shard_bf = {n: big2d(W, n).astype(BF16) for n in big_names}
    ag_mid = ["w_mkv", "w_out", "w_mq", "w_mo"]
    ag_ffn = ["w_gu", "w_down"]
    cw_mine = jnp.pad(shard2d(conv_w), ((0, 1), (0, 0)))
    w_in_slab, cw_slab = _run_comm(_ag_comm([shard_bf["w_in"], cw_mine]), "ag_w_in")
    slab = {"w_in": w_in_slab}
    w_int = w_in_slab.reshape(D_IN, D)
    w_ft = jnp.pad(w_int[OFF_F:D_IN], ((0, D_IN_PAD - D_IN), (0, 0)))
    cw = jnp.transpose(cw_slab, (1, 0, 2)).reshape(CONV_HALO, CONV_CH)

    row = lambda a: a.reshape(1, -1)
    bf_pad = jnp.pad(row(b_f), ((0, 0), (0, LANES - 8)))
    x2d = x.reshape(T, D)
    mem2d = mem.reshape(B * MEM_LEN, D)
    tgt = loss_target.reshape(T, D)

    (h, u, gt, q, k, v, zf, c, cq, qx, kx, qT), got = _fwd_in(x2d, row(g_mix), w_int, w_ft, bf_pad, B, S,
                                                  comm=_ag_comm([shard_bf[n] for n in ag_mid[:2]]))
    slab.update(zip(ag_mid[:2], got))
    ckT = jnp.transpose(c.reshape(B, S, LANES)[:, :, :8], (0, 2, 1)).reshape(B, N_PAIR, 2, S)
    ckT = jnp.pad(ckT, ((0, 0), (0, 0), (0, 6), (0, 0)))
    (y, co), got = _conv_fwd(u, gt, cw, row(conv_b), row(ln_g), row(ln_b), B, S,
                             comm=_ag_comm([shard_bf[n] for n in ag_mid[2:]]))
    slab.update(zip(ag_mid[2:], got))
    (o, fox_bias), got = _fox_fwd(qx, kx, v, cq, B, S, comm=_ag_comm([shard_bf[n] for n in ag_ffn]))
    slab.update(zip(ag_ffn, got))
    full = {n: slab[n] if by_col else slab[n].reshape(4 * r, c) for n, r, c, by_col in BIG}
    mn, km, vm = _mem_kv(mem2d, row(g_mem), full["w_mkv"], B)
    (x1, hx, qm, om, x2, cat), _ = _fwd_mid(x2d, co, o, km, vm, full["w_out"], full["w_mq"], full["w_mo"], row(g_x), B, S)
    hf, gu, act, dx3, loss_p, dg_final = _fwd_ffn(x2, tgt, full["w_gu"], full["w_down"], row(g_ffn), row(g_final), T)

    pos_sum = lambda gs, rcvs, ns: _chip_sum(gs, rcvs, pos, "rs_chip_sum_" + ns[0])
    fin_sum = lambda gs, rcvs, rcs, ns: _final_sum(gs, rcvs, rcs, pos, "rs_final_sum_" + ns[0])
    RH = {}
    dgu, dx2, dg_ffn = _bwd_ffn(dx3, gu, x2, full["w_gu"], full["w_down"], row(g_ffn), T)
    g_ffn_w = [_dw(hf, dgu, "dw_gu", D_FF, slabs=FF_CHUNK), _dw(act, dx3, "dw_down", D, tt=2048).reshape(4, D_FF // 4, D)]
    (dx1, dqm, dco, do, dd, dkm, dvm, dg_x, doT), rcv_ffn = _bwd_mid(dx2, x1, qm, km, vm, o, full["w_mo"], full["w_mq"],
                                                                full["w_out"], row(g_x), B, S, comm=_sibling_comm(g_ffn_w))
    pb_ffn = pos_sum(g_ffn_w, rcv_ffn, ag_ffn)
    dkv, dg_mem = _mem_bwd(dkm, dvm, mem2d, full["w_mkv"], row(g_mem), B)
    g_mid_w = [_dw(mn, dkv, "dw_mkv", 512, slabs=512), _dw(cat, dx1, "dw_out", D, tt=2048).reshape(4, 256, D),
               _dw(hx, dqm, "dw_mq", D, tt=2048).reshape(4, 256, D), _dw(om, dx2, "dw_mo", D, tt=2048).reshape(4, 256, D)]
    (dq, dk, dv, dc, dcq), got = _fox_bwd(q, k, v, do, fox_bias, dd, ckT, qT, doT, B, S,
                                          comm=_join(_ici_comm(pb_ffn), _sibling_comm(g_mid_w)))
    rc_ffn, rcv_mid = got[:len(pb_ffn)], got[len(pb_ffn):]
    RH.update(zip(ag_ffn, fin_sum(g_ffn_w, rcv_ffn, rc_ffn, ag_ffn)))
    pb_mid = pos_sum(g_mid_w, rcv_mid, ag_mid)
    dc8 = jnp.transpose(dc[:, :, :2, :].reshape(B, 8, S), (0, 2, 1)).reshape(T, 8)
    dc8 = dc8 + dcq.reshape(T, 8, HEAD_D)[:, :, 0]
    dzf, dbf = _fgate_bwd(jnp.pad(dc8, ((0, 0), (0, LANES - 8))), zf, B, S)
    (du, dgt, dcw, dvec), rc_mid = _conv_bwd(dco, y, u, gt, cw, row(ln_g), row(ln_b), B, S, comm=_ici_comm(pb_mid))
    RH.update(zip(ag_mid, fin_sum(g_mid_w, rcv_mid, rc_mid, ag_mid)))
    dz = jnp.concatenate([du, dgt, dq, dk, dv, dzf], axis=1)
    g_in_w = [_dw(dz, h, "dw_in", D, tk=D_IN_PAD // 3, rows=D_IN, tt=2048).reshape(4, D_IN // 4, D)]
    rcv_in = _run_comm(_sibling_comm(g_in_w), "rs_sibling_in")
    (grad_x, dg_mix), rc_in = _bwd_in(dz, w_int, w_ft, x2d, dx1, row(g_mix), T,
                                      comm=_ici_comm(pos_sum(g_in_w, rcv_in, ["w_in"])))
    RH.update(zip(["w_in"], fin_sum(g_in_w, rcv_in, rc_in, ["w_in"])))

    small_g = {"g_mix": dg_mix, "b_f": dbf[:, :8], "conv_w": dcw, "conv_b": dvec[0], "ln_g": dvec[1], "ln_b": dvec[2],
               "g_x": dg_x, "g_mem": dg_mem, "g_ffn": dg_ffn, "g_final": dg_final, "loss": loss_p[:, :1]}
    sg, filled = _small_allreduce(_pack_small(small_g, CONV_HALO * 4), "allreduce_small", [RH[n] for n in big_names])
    shared = dict(zip(big_names, filled))
    stepped = _adamw([big2d(W, n) for n in big_names], [shared[n] for n in big_names], [big2d(M, n) for n in big_names],
                     [big2d(V, n) for n in big_names], "adamw_big")
    G, DL, NM, NV = (dict(zip(big_names, col)) for col in zip(*stepped))
    shapes = {n: W[n].shape for n in names if n not in big_names}
    shapes["conv_w"] = (CONV_HALO, CONV_CH)
    shapes["loss"] = (1,)
    sgrads = _unpack_small(sg, shapes, CONV_HALO * 4)
    loss = sgrads.pop("loss")[0]
    sgrads["conv_w"] = lax.dynamic_slice(sgrads["conv_w"], (0, chip * LANES), (CONV_K, LANES)).reshape(W["conv_w"].shape)
    spack = lambda d: _pack_small({n: d[n] for n in sgrads}, CONV_HALO)
    (_, sd, snm, snv), = _adamw([spack(W)], [spack(sgrads)], [spack(M)], [spack(V)], "adamw_small", parts=1)
    sshapes = {n: W[n].shape for n in sgrads}
    SD, SNM, SNV = (_unpack_small(a, sshapes, CONV_HALO) for a in (sd, snm, snv))

    def collect(bigs, smalls):
        back = lambda n: (bigs[n].T if n == "w_in" else bigs[n]).reshape(W[n].shape)
        return [back(n) if n in big_names else smalls[n] for n in names]

    return (loss, grad_x.reshape(x.shape), *collect(G, sgrads), *collect(DL, SD), *collect(NM, SNM), *collect(NV, SNV))
```

```python
import functools
import math

import jax
import jax.numpy as jnp
from jax import lax
from jax.experimental import pallas as pl
from jax.experimental.pallas import tpu as pltpu

F32, BF16 = jnp.float32, jnp.bfloat16
MESH = pl.DeviceIdType.MESH

D = 1024
CONV_CH = 512
CONV_K = 31
CONV_HALO = 32
FOX_W = 512
HEAD_D = 64
N_PAIR = 4
MEM_LEN = 256
MEM_HEADS = 4
MEM_HD = 256
D_FF = 2816
FF_CHUNK = 1408
D_IN = 2568
D_IN_PAD = 2688
OFF_F = 2560
EPS = 1e-6
LANES = 128

ADAM_LR, ADAM_B1, ADAM_B2, ADAM_EPS, ADAM_WD, ADAM_STEP = 0.001, 0.9, 0.999, 1e-08, 0.01, 10

VMEM_LIMIT = 60 * 1024 * 1024

BIG = (("w_out", 256, 1024, False), ("w_mq", 256, 1024, False), ("w_mkv", 1024, 512, True),
       ("w_mo", 256, 1024, False), ("w_gu", 1024, 1408, True), ("w_down", 704, 1024, False),
       ("w_in", 642, 1024, False))

ANY = pl.BlockSpec(memory_space=pl.ANY)


def _sig(x):
    return 1.0 / (1.0 + jnp.exp(-x))


def _dot(a, b):
    return jnp.dot(a, b, preferred_element_type=F32)


def _dot_nt(a, b):
    return lax.dot_general(a, b, (((1,), (1,)), ((), ())), preferred_element_type=F32)


def _dot_tn(a, b):
    return lax.dot_general(a, b, (((0,), (0,)), ((), ())), preferred_element_type=F32)


def _split3(x):
    hi = x.astype(BF16)
    r = x - hi.astype(F32)
    mid = r.astype(BF16)
    return hi, mid, (r - mid.astype(F32)).astype(BF16)


def _dot_01(a, b):
    if a.dtype == jnp.bool_:
        return sum(_dot(a.astype(BF16), t) for t in _split3(b))
    return sum(_dot(t, b.astype(BF16)) for t in _split3(a))


def _resident(a):
    nd = a.ndim
    return pl.BlockSpec(a.shape, lambda *_: (0,) * nd, pipeline_mode=pl.Buffered(1))


def _acc_spec(shape):
    nd = len(shape)
    return pl.BlockSpec(shape, lambda *_: (0,) * nd)


def _params(n_grid):
    return pltpu.CompilerParams(dimension_semantics=("arbitrary",) * n_grid, vmem_limit_bytes=VMEM_LIMIT)


def _sds(shape, dtype):
    return jax.ShapeDtypeStruct(shape, dtype)


def _rms(x):
    r = lax.rsqrt(jnp.mean(x * x, axis=-1, keepdims=True) + EPS)
    return r, x * r


def _rms_bwd(dy, xh, r, g):
    dxh = dy * g
    dx = r * (dxh - xh * jnp.mean(dxh * xh, axis=-1, keepdims=True))
    return dx, dy * xh


def _head_expand(rows, cols):
    hd = lax.broadcasted_iota(jnp.int32, (rows, cols), 1) // HEAD_D
    hr = lax.broadcasted_iota(jnp.int32, (rows, cols), 0)
    return hd == hr


def _feat_major_spec(TB, FOX_T, nb):
    return pl.BlockSpec((1, TB // FOX_T, FOX_W, FOX_T), lambda b, j: (b, j, 0, 0))


def _fwd_in(x2, g_mix, w_int, w_ft, bf_pad, B, S, comm=None):
    T = B * S
    TB = min(512, S)
    nb = S // TB
    FOX_T = min(256, S)

    def body(x_ref, g_ref, w_ref, wf_ref, bf_ref, h_ref, u_ref, gt_ref, q_ref, k_ref, v_ref, zf_ref, c_ref, cq_ref,
             qx_ref, kx_ref, qt_ref, carry):
        j = pl.program_id(1)

        @pl.when(j == 0)
        def _():
            carry[...] = jnp.zeros_like(carry)

        _, xh = _rms(x_ref[...])
        h = (xh * g_ref[...]).astype(BF16)
        h_ref[...] = h
        u_ref[...] = _dot_nt(h, w_ref[0:512, :])
        gt_ref[...] = _dot_nt(h, w_ref[512:1024, :])
        qf = _dot_nt(h, w_ref[1024:1536, :])
        qb = qf.astype(BF16)
        kb = _dot_nt(h, w_ref[1536:2048, :]).astype(BF16)
        q_ref[...] = qb
        k_ref[...] = kb
        for t in range(TB // FOX_T):
            qt_ref[0, t] = qf[t * FOX_T:(t + 1) * FOX_T, :].T.astype(BF16)
        v_ref[...] = _dot_nt(h, w_ref[2048:2560, :]).astype(BF16)
        zf = _dot_nt(h, wf_ref[...]) + bf_ref[...]
        zf_ref[...] = zf
        lane = lax.broadcasted_iota(jnp.int32, zf.shape, 1)
        logf = jnp.where(lane < 8, jnp.minimum(zf, 0.0) - jnp.log(1.0 + jnp.exp(-jnp.abs(zf))), 0.0)
        row = lax.broadcasted_iota(jnp.int32, (TB, TB), 0)
        col = lax.broadcasted_iota(jnp.int32, (TB, TB), 1)
        c = _dot_01(row >= col, logf) + carry[0:1, :]
        carry[0:1, :] = c[TB - 1:TB, :]
        c_ref[...] = c
        cq = _dot_01(c, _head_expand(LANES, FOX_W))
        cq_ref[...] = cq
        hl = lax.broadcasted_iota(jnp.int32, (TB, LANES), 1)
        for hd in range(2 * N_PAIR):
            grp = slice((hd // 2) * LANES, (hd // 2 + 1) * LANES)
            swap = (lambda t: t) if hd % 2 == 0 else (lambda t: pltpu.roll(t, HEAD_D, 1))
            qf = swap(qb[:, grp].astype(F32) * (1.0 / math.sqrt(HEAD_D)))
            kf = swap(kb[:, grp].astype(F32))
            cv = cq[:, grp] if hd % 2 == 1 else pltpu.roll(cq[:, grp], HEAD_D, 1)
            hi = cv.astype(BF16).astype(F32)
            mid = (cv - hi).astype(BF16).astype(F32)
            lo = (cv - hi - mid).astype(BF16).astype(F32)
            pick = lambda a, b, c3, one_from, one_to: jnp.where(hl == a[0], a[1], jnp.where(hl == b[0], b[1], jnp.where(
                hl == c3[0], c3[1], jnp.where((hl >= one_from) & (hl < one_to), 1.0, 0.0))))
            qx = jnp.where(hl < HEAD_D, qf, pick((67, hi), (68, mid), (69, lo), 64, 67))
            kx = jnp.where(hl < HEAD_D, kf, pick((64, -hi), (65, -mid), (66, -lo), 67, 70))
            qx_ref[:, hd * LANES:(hd + 1) * LANES] = qx.astype(BF16)
            kx_ref[:, hd * LANES:(hd + 1) * LANES] = kx.astype(BF16)

    tok = lambda w: pl.BlockSpec((TB, w), lambda b, j: (b * nb + j, 0))
    outs = [(D, BF16), (512, F32), (512, F32), (512, BF16), (512, BF16), (512, BF16), (LANES, F32),
            (LANES, F32), (FOX_W, F32), (2 * FOX_W, BF16), (2 * FOX_W, BF16)]
    return _call(
        body, comm, name="fwd_in", grid=(B, nb),
        in_specs=[tok(D), _resident(g_mix), _resident(w_int), _resident(w_ft), _resident(bf_pad)],
        out_specs=[tok(w) for w, _ in outs] + [_feat_major_spec(TB, FOX_T, nb)],
        out_shape=[_sds((T, w), dt) for w, dt in outs] + [_sds((B, S // FOX_T, FOX_W, FOX_T), BF16)],
        scratch_shapes=[pltpu.VMEM((8, LANES), F32)],
        args=(x2, g_mix, w_int, w_ft, bf_pad))


def _head_sum(n):
    hc = lax.broadcasted_iota(jnp.int32, (n, n), 1) // HEAD_D
    hr = lax.broadcasted_iota(jnp.int32, (n, n), 0) // HEAD_D
    return hc == hr


def _layernorm_silu(y, lg, lb):
    mu = jnp.mean(y, axis=-1, keepdims=True)
    yc = y - mu
    rs = lax.rsqrt(jnp.mean(yc * yc, axis=-1, keepdims=True) + EPS)
    n = yc * rs
    l = n * lg + lb
    return rs, n, l


SUB = 8


def _shifted_copies(cat, sh, rows):
    for r in range(1, SUB):
        sh[r, 0:rows, :] = cat[r:r + rows, :]


def _tap(cat, sh, off, rows, cols=slice(None)):
    r = off % SUB
    return cat[off:off + rows, cols] if r == 0 else sh[r, off - r:off - r + rows, cols]


CONV_ROWS = 128


def _conv_pieces(CB):
    rows = min(CONV_ROWS, CB)
    return [(r0, rows, slice(c0, c0 + LANES)) for c0 in range(0, CONV_CH, LANES) for r0 in range(0, CB, rows)]


def _conv_fwd(u, gt, cw, cb, lng, lnb, B, S, comm=None):
    T = B * S
    CB = min(256, S)
    nb = S // CB

    def body(u_ref, gt_ref, w_ref, cb_ref, lg_ref, lb_ref, y_ref, co_ref, acat, ash):
        j = pl.program_id(1)

        @pl.when(j == 0)
        def _():
            acat[0:CONV_HALO, :] = jnp.zeros((CONV_HALO, CONV_CH), F32)

        acat[CONV_HALO:CONV_HALO + CB, :] = u_ref[...] * _sig(gt_ref[...])
        _shifted_copies(acat, ash, CB + CONV_HALO - SUB)
        for r0, rows, cs in _conv_pieces(CB):
            acc = jnp.zeros((rows, LANES), F32) + cb_ref[:, cs]
            for k in range(CONV_K):
                acc = acc + w_ref[k:k + 1, cs] * _tap(acat, ash, r0 + CONV_HALO - (CONV_K - 1) + k, rows, cs)
            y_ref[r0:r0 + rows, cs] = acc
        acat[0:CONV_HALO, :] = acat[CB:CB + CONV_HALO, :]
        _, _, l = _layernorm_silu(y_ref[...], lg_ref[...], lb_ref[...])
        co_ref[...] = (l * _sig(l)).astype(BF16)

    tok = lambda w: pl.BlockSpec((CB, w), lambda b, j: (b * nb + j, 0))
    return _call(
        body, comm, name="conv_fwd", grid=(B, nb),
        in_specs=[tok(CONV_CH), tok(CONV_CH), _resident(cw), _resident(cb), _resident(lng), _resident(lnb)],
        out_specs=[tok(CONV_CH), tok(CONV_CH)],
        out_shape=[_sds((T, CONV_CH), F32), _sds((T, CONV_CH), BF16)],
        scratch_shapes=[pltpu.VMEM((CONV_HALO + CB, CONV_CH), F32),
                        pltpu.VMEM((SUB, CB + CONV_HALO - SUB, CONV_CH), F32)],
        args=(u, gt, cw, cb, lng, lnb))


def _fox_fwd(qx, kx, v, cq, B, S, comm=None):
    T = B * S
    TQ = min(256, S)
    nq = S // TQ

    def body(qa_ref, qb_ref, ka_ref, kb_ref, v_ref, cq_ref, o_ref, lse_ref, s_scr, s_odd, m_scr, acc_scr):
        i = pl.program_id(2)
        lane = lax.broadcasted_iota(jnp.int32, (TQ, LANES), 1)
        lo = lane < HEAD_D
        qh = (qa_ref[...], qb_ref[...])
        kh = (ka_ref, kb_ref)
        m_scr[...] = jnp.full(m_scr.shape, -1e30, F32)
        acc_scr[...] = jnp.zeros_like(acc_scr)
        row = lax.broadcasted_iota(jnp.int32, (TQ, TQ), 0)
        col = lax.broadcasted_iota(jnp.int32, (TQ, TQ), 1)
        wide = lambda x: jnp.concatenate([x, x], axis=1) if TQ == 2 * LANES else jnp.tile(x, (1, TQ // LANES))

        def scores(j, s_buf):
            start = pl.multiple_of(j * TQ, TQ)
            for h in range(2):
                s_buf[h] = _dot_nt(qh[h], kh[h][pl.ds(start, TQ), :])

        def softmax_step(j, s_buf, diagonal):
            start = pl.multiple_of(j * TQ, TQ)
            vj = v_ref[pl.ds(start, TQ), :]
            for h in range(2):
                def logits():
                    return jnp.where(col <= row, s_buf[h], -1e30) if diagonal else s_buf[h]

                m_old = m_scr[h]
                m_new = jnp.maximum(m_old, jnp.max(logits(), axis=-1, keepdims=True))
                alpha = jnp.exp(m_old - m_new)
                m_scr[h] = m_new
                p = jnp.exp(logits() - wide(m_new)).astype(BF16)
                vx = jnp.where(lo if h == 0 else ~lo, vj, jnp.ones_like(vj))
                acc_scr[h] = alpha * acc_scr[h] + _dot(p, vx)

        def two_blocks(jj, carry):
            j = 2 * jj
            scores(j + 1, s_odd)
            softmax_step(j, s_scr, False)
            scores(j + 2, s_scr)
            softmax_step(j + 1, s_odd, False)
            return carry

        scores(0, s_scr)
        lax.fori_loop(0, i // 2, two_blocks, 0)

        @pl.when(i % 2 == 0)
        def _():
            softmax_step(i, s_scr, True)

        @pl.when(i % 2 == 1)
        def _():
            scores(i, s_odd)
            softmax_step(i - 1, s_scr, False)
            softmax_step(i, s_odd, True)

        acc = jnp.where(lo, acc_scr[0], acc_scr[1])
        den = pltpu.roll(jnp.where(lo, acc_scr[1], acc_scr[0]), HEAD_D, 1)
        o_ref[...] = acc / den
        lse_ref[...] = cq_ref[...] - (jnp.where(lo, m_scr[0], m_scr[1]) + jnp.log(den))

    qspec = pl.BlockSpec((TQ, LANES), lambda b, p, i: (b * nq + i, p))
    kspec = pl.BlockSpec((S, LANES), lambda b, p, i: (b, p))
    qhead = lambda h: pl.BlockSpec((TQ, LANES), lambda b, p, i: (b * nq + i, 2 * p + h))
    khead = lambda h: pl.BlockSpec((S, LANES), lambda b, p, i: (b, 2 * p + h))
    return _call(
        body, comm, name="fox_fwd", grid=(B, N_PAIR, nq),
        in_specs=[qhead(0), qhead(1), khead(0), khead(1), kspec, qspec],
        out_specs=[qspec, qspec],
        out_shape=[_sds((T, FOX_W), F32), _sds((T, FOX_W), F32)],
        scratch_shapes=[pltpu.VMEM((2, TQ, TQ), F32), pltpu.VMEM((2, TQ, TQ), F32),
                        pltpu.VMEM((2, TQ, LANES), F32), pltpu.VMEM((2, TQ, LANES), F32)],
        args=(qx, qx, kx, kx, v, cq))


def _mem_kv(mem2, g_mem, w_mkv, B):
    def body(m_ref, g_ref, w_ref, mn_ref, km_ref, vm_ref):
        _, xh = _rms(m_ref[...])
        mn = (xh * g_ref[...]).astype(BF16)
        mn_ref[...] = mn
        for s in range(2):
            km_ref[:, 512 * s:512 * (s + 1)] = _dot(mn, w_ref[s]).astype(BF16)
            vm_ref[:, 512 * s:512 * (s + 1)] = _dot(mn, w_ref[2 + s]).astype(BF16)

    blk = pl.BlockSpec((MEM_LEN, D), lambda b: (b, 0))
    return pl.pallas_call(
        body, name="mem_kv", grid=(B,),
        in_specs=[blk, _resident(g_mem), _resident(w_mkv)],
        out_specs=[blk, blk, blk],
        out_shape=[_sds((B * MEM_LEN, D), BF16)] * 3,
        compiler_params=_params(1),
    )(mem2, g_mem, w_mkv)


def _mem_probs(qm, km):
    ps = []
    for h in range(MEM_HEADS):
        hs = slice(h * MEM_HD, (h + 1) * MEM_HD)
        lg = _dot_nt(qm[:, hs], km[:, hs]) * (1.0 / math.sqrt(MEM_HD))
        e = jnp.exp(lg - jnp.max(lg, axis=-1, keepdims=True))
        ps.append(e / jnp.sum(e, axis=-1, keepdims=True))
    return ps


def _fwd_mid(x2, co, o, km, vm, w_out, w_mq, w_mo, g_x, B, S, comm=None):
    T = B * S
    TB = min(512, S)
    nb = S // TB

    def body(x_ref, co_ref, o_ref, km_ref, vm_ref, wo_ref, wq_ref, wm_ref, g_ref,
             x1_ref, hx_ref, qm_ref, om_ref, x2_ref, cat_ref):
        cat_ref[:, 0:CONV_CH] = co_ref[...]
        cat_ref[:, CONV_CH:D] = o_ref[...].astype(BF16)
        x1 = x_ref[...] + _dot(cat_ref[...], wo_ref[...])
        x1_ref[...] = x1
        _, xh = _rms(x1)
        hx = (xh * g_ref[...]).astype(BF16)
        hx_ref[...] = hx
        qm = _dot(hx, wq_ref[...]).astype(BF16)
        qm_ref[...] = qm
        ps = _mem_probs(qm, km_ref[...])
        vmv = vm_ref[...]
        for h in range(MEM_HEADS):
            hs = slice(h * MEM_HD, (h + 1) * MEM_HD)
            om_ref[:, hs] = _dot(ps[h].astype(BF16), vmv[:, hs]).astype(BF16)
        x2_ref[...] = x1 + _dot(om_ref[...], wm_ref[...])

    tok = lambda w: pl.BlockSpec((TB, w), lambda b, j: (b * nb + j, 0))
    memb = pl.BlockSpec((MEM_LEN, D), lambda b, j: (b, 0))
    outs = [(D, F32), (D, BF16), (D, BF16), (D, BF16), (D, F32), (D, BF16)]
    return _call(
        body, comm, name="fwd_mid", grid=(B, nb),
        in_specs=[tok(D), tok(CONV_CH), tok(FOX_W), memb, memb, _resident(w_out), _resident(w_mq), _resident(w_mo),
                  _resident(g_x)],
        out_specs=[tok(w) for w, _ in outs],
        out_shape=[_sds((T, w), dt) for w, dt in outs],
        scratch_shapes=[],
        args=(x2, co, o, km, vm, w_out, w_mq, w_mo, g_x))


def _load_gate_up(wgu_hbm, wg, wu, sems):
    copies = [pltpu.make_async_copy(wgu_hbm.at[s], (wg if s < 2 else wu).at[:, pl.ds((s % 2) * FF_CHUNK, FF_CHUNK)],
                                    sems.at[s]) for s in range(4)]
    for cp in copies:
        cp.start()
    for cp in copies:
        cp.wait()


def _fwd_ffn(x2, tgt, w_gu, w_down, g_ffn, g_final, T):
    TB = min(256, T)
    nb = T // TB

    def body(x_ref, t_ref, wgu_ref, wd_ref, gf_ref, gl_ref, hf_ref, gu_ref, act_ref, dx3_ref, loss_ref, dgl_ref,
             wg, wu, sems):
        i = pl.program_id(0)

        @pl.when(i == 0)
        def _():
            _load_gate_up(wgu_ref, wg, wu, sems)
            loss_ref[...] = jnp.zeros_like(loss_ref)
            dgl_ref[...] = jnp.zeros_like(dgl_ref)

        x2v = x_ref[...]
        _, xh = _rms(x2v)
        hf = (xh * gf_ref[...]).astype(BF16)
        hf_ref[...] = hf
        g = _dot(hf, wg[...])
        u = _dot(hf, wu[...])
        gu_ref[:, 0:D_FF] = g
        gu_ref[:, D_FF:2 * D_FF] = u
        act = (g * _sig(g) * u).astype(BF16)
        act_ref[...] = act
        x3 = x2v + _dot(act, wd_ref[...])
        r3, xh3 = _rms(x3)
        gl = gl_ref[...]
        e = xh3 * gl - t_ref[...]
        loss_ref[...] += jnp.sum(e * e) * (0.5 / D)
        dy = e * (1.0 / D)
        dx3, dgl = _rms_bwd(dy, xh3, r3, gl)
        dx3_ref[...] = dx3
        dgl_ref[...] += jnp.sum(dgl, axis=0, keepdims=True)

    tok = lambda w: pl.BlockSpec((TB, w), lambda i: (i, 0))
    return pl.pallas_call(
        body, name="fwd_ffn", grid=(nb,),
        in_specs=[tok(D), tok(D), ANY, _resident(w_down), _resident(g_ffn), _resident(g_final)],
        out_specs=[tok(D), tok(2 * D_FF), tok(D_FF), tok(D), _acc_spec((1, LANES)), _acc_spec((1, D))],
        out_shape=[_sds((T, D), BF16), _sds((T, 2 * D_FF), F32), _sds((T, D_FF), BF16), _sds((T, D), F32),
                   _sds((1, LANES), F32), _sds((1, D), F32)],
        scratch_shapes=[pltpu.VMEM((D, D_FF), BF16), pltpu.VMEM((D, D_FF), BF16), pltpu.SemaphoreType.DMA((4,))],
        compiler_params=_params(1),
    )(x2, tgt, w_gu, w_down, g_ffn, g_final)


def _bwd_ffn(dx3, gu, x2, w_gu, w_down, g_ffn, T):
    TB = min(256, T)
    nb = T // TB

    def body(d_ref, gu_ref, x_ref, wgu_ref, wd_ref, gf_ref, dgu_ref, dx2_ref, dgf_ref, wg, wu, sems):
        i = pl.program_id(0)

        @pl.when(i == 0)
        def _():
            _load_gate_up(wgu_ref, wg, wu, sems)
            dgf_ref[...] = jnp.zeros_like(dgf_ref)

        dx3v = d_ref[...]
        db = dx3v.astype(BF16)
        dact = _dot_nt(db, wd_ref[...])
        g = gu_ref[:, 0:D_FF]
        u = gu_ref[:, D_FF:2 * D_FF]
        sg = _sig(g)
        dg = (dact * u * sg * (1.0 + g * (1.0 - sg))).astype(BF16)
        du = (dact * g * sg).astype(BF16)
        dgu_ref[:, 0:D_FF] = dg
        dgu_ref[:, D_FF:2 * D_FF] = du
        dhf = _dot_nt(dg, wg[...]) + _dot_nt(du, wu[...])
        r2, xh2 = _rms(x_ref[...])
        dx, dg_tok = _rms_bwd(dhf, xh2, r2, gf_ref[...])
        dx2_ref[...] = dx3v + dx
        dgf_ref[...] += jnp.sum(dg_tok, axis=0, keepdims=True)

    tok = lambda w: pl.BlockSpec((TB, w), lambda i: (i, 0))
    return pl.pallas_call(
        body, name="bwd_ffn", grid=(nb,),
        in_specs=[tok(D), tok(2 * D_FF), tok(D), ANY, _resident(w_down), _resident(g_ffn)],
        out_specs=[tok(2 * D_FF), tok(D), _acc_spec((1, D))],
        out_shape=[_sds((T, 2 * D_FF), BF16), _sds((T, D), F32), _sds((1, D), F32)],
        scratch_shapes=[pltpu.VMEM((D, D_FF), BF16), pltpu.VMEM((D, D_FF), BF16), pltpu.SemaphoreType.DMA((4,))],
        compiler_params=_params(1),
    )(dx3, gu, x2, w_gu, w_down, g_ffn)


def _bwd_mid(dx2, x1, qm, km, vm, o, w_mo, w_mq, w_out, g_x, B, S, comm=None):
    T = B * S
    TB = min(512, S)
    nb = S // TB
    FOX_T = min(256, S)
    inv = 1.0 / math.sqrt(MEM_HD)

    def body(d_ref, x1_ref, qm_ref, km_ref, vm_ref, o_ref, wm_ref, wq_ref, wo_ref, g_ref,
             dx1_ref, dqm_ref, dco_ref, do_ref, dd_ref, dkm_ref, dvm_ref, dgx_ref, dot_ref):
        b = pl.program_id(0)
        j = pl.program_id(1)

        @pl.when((b == 0) & (j == 0))
        def _():
            dgx_ref[...] = jnp.zeros_like(dgx_ref)

        @pl.when(j == 0)
        def _():
            dkm_ref[...] = jnp.zeros_like(dkm_ref)
            dvm_ref[...] = jnp.zeros_like(dvm_ref)

        dx2v = d_ref[...]
        dom = _dot_nt(dx2v.astype(BF16), wm_ref[...]).astype(BF16)
        qmv = qm_ref[...]
        kmv = km_ref[...]
        vmv = vm_ref[...]
        ps = _mem_probs(qmv, kmv)
        for h in range(MEM_HEADS):
            hs = slice(h * MEM_HD, (h + 1) * MEM_HD)
            p = ps[h]
            dp = _dot_nt(dom[:, hs], vmv[:, hs])
            ds = (p * (dp - jnp.sum(p * dp, axis=-1, keepdims=True))).astype(BF16)
            dqm_ref[:, hs] = (_dot(ds, kmv[:, hs]) * inv).astype(BF16)
            dkm_ref[:, hs] += _dot_tn(ds, qmv[:, hs]) * inv
            dvm_ref[:, hs] += _dot_tn(p.astype(BF16), dom[:, hs])
        dhx = _dot_nt(dqm_ref[...], wq_ref[...])
        r1, xh1 = _rms(x1_ref[...])
        dx, dg_tok = _rms_bwd(dhx, xh1, r1, g_ref[...])
        dx1 = dx2v + dx
        dx1_ref[...] = dx1
        dgx_ref[...] += jnp.sum(dg_tok, axis=0, keepdims=True)
        d1b = dx1.astype(BF16)
        dco_ref[...] = _dot_nt(d1b, wo_ref[0:CONV_CH, :])
        do = _dot_nt(d1b, wo_ref[CONV_CH:D, :])
        dob = do.astype(BF16)
        do_ref[...] = dob
        for t in range(TB // FOX_T):
            dot_ref[0, t] = do[t * FOX_T:(t + 1) * FOX_T, :].T.astype(BF16)
        dd_ref[...] = _dot_01(dob.astype(F32) * o_ref[...], _head_sum(FOX_W))

    tok = lambda w: pl.BlockSpec((TB, w), lambda b, j: (b * nb + j, 0))
    memb = pl.BlockSpec((MEM_LEN, D), lambda b, j: (b, 0))
    outs = [(D, F32), (D, BF16), (CONV_CH, F32), (FOX_W, BF16), (FOX_W, F32)]
    return _call(
        body, comm, name="bwd_mid", grid=(B, nb),
        in_specs=[tok(D), tok(D), tok(D), memb, memb, tok(FOX_W), _resident(w_mo), _resident(w_mq), _resident(w_out),
                  _resident(g_x)],
        out_specs=[tok(w) for w, _ in outs] + [memb, memb, _acc_spec((1, D)), _feat_major_spec(TB, FOX_T, nb)],
        out_shape=[_sds((T, w), dt) for w, dt in outs] + [_sds((B * MEM_LEN, D), F32)] * 2 + [_sds((1, D), F32)]
        + [_sds((B, S // FOX_T, FOX_W, FOX_T), BF16)],
        scratch_shapes=[],
        args=(dx2, x1, qm, km, vm, o, w_mo, w_mq, w_out, g_x))


def _mem_bwd(dkm, dvm, mem2, w_mkv, g_mem, B):
    def body(dk_ref, dv_ref, m_ref, w_ref, g_ref, dkv_ref, dg_ref):
        b = pl.program_id(0)

        @pl.when(b == 0)
        def _():
            dg_ref[...] = jnp.zeros_like(dg_ref)

        dk = dk_ref[...].astype(BF16)
        dv = dv_ref[...].astype(BF16)
        dkv_ref[:, 0:D] = dk
        dkv_ref[:, D:2 * D] = dv
        dmn = jnp.zeros((MEM_LEN, D), F32)
        for s in range(2):
            dmn = dmn + _dot_nt(dk[:, 512 * s:512 * (s + 1)], w_ref[s]) + _dot_nt(dv[:, 512 * s:512 * (s + 1)], w_ref[2 + s])
        _, xh = _rms(m_ref[...])
        dg_ref[...] += jnp.sum(dmn * xh, axis=0, keepdims=True)

    blk = pl.BlockSpec((MEM_LEN, D), lambda b: (b, 0))
    return pl.pallas_call(
        body, name="mem_bwd", grid=(B,),
        in_specs=[blk, blk, blk, _resident(w_mkv), _resident(g_mem)],
        out_specs=[pl.BlockSpec((MEM_LEN, 2 * D), lambda b: (b, 0)), _acc_spec((1, D))],
        out_shape=[_sds((B * MEM_LEN, 2 * D), BF16), _sds((1, D), F32)],
        compiler_params=_params(1),
    )(dkm, dvm, mem2, w_mkv, g_mem)


def _fox_bwd(q, k, v, do, bias, dd, ckT, qT, doT, B, S, comm=None):
    T = B * S
    TK = min(256, S)
    nk = S // TK
    scale = 1.0 / math.sqrt(HEAD_D)

    def body(q_ref, k_ref, v_ref, do_ref, bias_ref, dd_ref, ck_ref, qt_ref, dot_ref, dq_ref, dk_ref, dv_ref, dc_ref,
             dcq_ref, dq_acc, rs_acc, s_scr, dp_scr, s_odd, dp_odd, dk_acc, dv_acc, dc_acc):
        j = pl.program_id(2)

        @pl.when(j == 0)
        def _():
            dq_acc[...] = jnp.zeros_like(dq_acc)
            rs_acc[...] = jnp.zeros_like(rs_acc)

        dk_acc[...] = jnp.zeros_like(dk_acc)
        dv_acc[...] = jnp.zeros_like(dv_acc)
        dc_acc[...] = jnp.zeros_like(dc_acc)
        lane = lax.broadcasted_iota(jnp.int32, (TK, LANES), 1)
        lo = lane < HEAD_D
        ks = k_ref[...] * jnp.asarray(scale, BF16)
        v2 = v_ref[...]
        zero = jnp.zeros_like(ks)
        kh = (jnp.where(lo, ks, zero), jnp.where(lo, zero, ks))
        vh = (jnp.where(lo, v2, zero), jnp.where(lo, zero, v2))
        kstart = pl.multiple_of(j * TK, TK)
        ckh = tuple(ck_ref[0, 0, h:h + 1, pl.ds(kstart, TK)] for h in range(2))
        row = lax.broadcasted_iota(jnp.int32, (TK, TK), 0)
        col = lax.broadcasted_iota(jnp.int32, (TK, TK), 1)
        wide = lambda x: jnp.concatenate([x, x], axis=1) if TK == 2 * LANES else jnp.tile(x, (1, TK // LANES))

        def scores(i, s_buf, dp_buf):
            start = pl.multiple_of(i * TK, TK)
            qi = q_ref[pl.ds(start, TK), :]
            doi = do_ref[pl.ds(start, TK), :]
            for h in range(2):
                s_buf[h] = _dot_nt(qi, kh[h])
                dp_buf[h] = _dot_nt(doi, vh[h])

        def grads(i, s_buf, dp_buf, diagonal):
            start = pl.multiple_of(i * TK, TK)
            bias2 = bias_ref[pl.ds(start, TK), :]
            dd2 = dd_ref[pl.ds(start, TK), :]
            for h in range(2):
                hc = slice(h * HEAD_D, h * HEAD_D + 1)
                bias = jnp.broadcast_to(bias2[:, hc], (TK, LANES))
                ddh = jnp.broadcast_to(dd2[:, hc], (TK, LANES))
                p = jnp.exp((s_buf[h] - ckh[h]) + wide(bias))
                if diagonal:
                    p = jnp.where(col <= row, p, 0.0)
                ds = p * (dp_buf[h] - wide(ddh))
                dc_acc[h, 0:1, :] += jnp.sum(ds, axis=0, keepdims=True)
                rs_acc[h, pl.ds(start, TK), :] += jnp.sum(ds, axis=1, keepdims=True)
                pb = p.astype(BF16)
                dsb = ds.astype(BF16)
                feat = slice(h * HEAD_D, (h + 1) * HEAD_D)
                dv_acc[feat, :] += _dot(dot_ref[0, i, feat, :], pb)
                dk_acc[feat, :] += _dot(qt_ref[0, i, feat, :], dsb)
                dq_acc[pl.ds(start, TK), :] += _dot(dsb, kh[h])

        n_off = nk - 1 - j
        block = lambda t: jnp.where(t < n_off, j + 1 + t, j)

        def two_blocks(tt, carry):
            t = 2 * tt
            scores(block(t + 1), s_odd, dp_odd)
            grads(block(t), s_scr, dp_scr, False)
            scores(block(t + 2), s_scr, dp_scr)
            grads(block(t + 1), s_odd, dp_odd, False)
            return carry

        scores(block(0), s_scr, dp_scr)
        lax.fori_loop(0, n_off // 2, two_blocks, 0)

        @pl.when(n_off % 2 == 0)
        def _():
            grads(j, s_scr, dp_scr, True)

        @pl.when(n_off % 2 == 1)
        def _():
            scores(j, s_odd, dp_odd)
            grads(nk - 1, s_scr, dp_scr, False)
            grads(j, s_odd, dp_odd, True)

        dk_ref[...] = (dk_acc[...].T * scale).astype(BF16)
        dv_ref[...] = dv_acc[...].T.astype(BF16)
        sub = lax.broadcasted_iota(jnp.int32, (8, TK), 0)
        dca = dc_acc[0, 0:1, :]
        dcb = dc_acc[1, 0:1, :]
        dc_ref[0, 0] = jnp.where(sub == 0, -dca, jnp.where(sub == 1, -dcb, 0.0))

        @pl.when(j == nk - 1)
        def _():
            dq_ref[...] = dq_acc[...].astype(BF16)
            lo_s = lax.broadcasted_iota(jnp.int32, (S, LANES), 1) < HEAD_D
            dcq_ref[...] = jnp.where(lo_s, rs_acc[0], rs_acc[1])

    full = pl.BlockSpec((S, LANES), lambda b, p, j: (b, p))
    blk = pl.BlockSpec((TK, LANES), lambda b, p, j: (b * nk + j, p))
    featT = pl.BlockSpec((1, nk, LANES, TK), lambda b, p, j: (b, 0, p, 0))
    return _call(
        body, comm, name="fox_bwd", grid=(B, N_PAIR, nk),
        in_specs=[full, blk, blk, full, full, full, pl.BlockSpec((1, 1, 8, S), lambda b, p, j: (b, p, 0, 0)),
                  featT, featT],
        out_specs=[full, blk, blk, pl.BlockSpec((1, 1, 8, TK), lambda b, p, j: (b, p, 0, j)), full],
        out_shape=[_sds((T, FOX_W), BF16), _sds((T, FOX_W), BF16), _sds((T, FOX_W), BF16),
                   _sds((B, N_PAIR, 8, S), F32), _sds((T, FOX_W), F32)],
        scratch_shapes=[pltpu.VMEM((S, LANES), F32), pltpu.VMEM((2, S, 1), F32),
                        pltpu.VMEM((2, TK, TK), F32), pltpu.VMEM((2, TK, TK), F32),
                        pltpu.VMEM((2, TK, TK), F32), pltpu.VMEM((2, TK, TK), F32),
                        pltpu.VMEM((LANES, TK), F32), pltpu.VMEM((LANES, TK), F32), pltpu.VMEM((2, 8, TK), F32)],
        args=(q, k, v, do, bias, dd, ckT, qT, doT))


def _fgate_bwd(dc8, zf, B, S):
    T = B * S
    TB = min(512, S)
    nb = S // TB

    def body(dc_ref, zf_ref, dzf_ref, dbf_ref, carry):
        b = pl.program_id(0)
        j = pl.program_id(1)

        @pl.when((b == 0) & (j == 0))
        def _():
            dbf_ref[...] = jnp.zeros_like(dbf_ref)

        @pl.when(j == 0)
        def _():
            carry[...] = jnp.zeros_like(carry)

        dc = dc_ref[...]
        row = lax.broadcasted_iota(jnp.int32, (TB, TB), 0)
        col = lax.broadcasted_iota(jnp.int32, (TB, TB), 1)
        dlogf = _dot_01(col >= row, dc) + carry[0:1, :]
        carry[0:1, :] = dlogf[0:1, :]
        lane = lax.broadcasted_iota(jnp.int32, dc.shape, 1)
        dzf = jnp.where(lane < 8, dlogf * _sig(-zf_ref[...]), 0.0)
        dzf_ref[...] = dzf.astype(BF16)
        dbf_ref[...] += jnp.sum(dzf, axis=0, keepdims=True)

    tok = pl.BlockSpec((TB, LANES), lambda b, j: (b * nb + (nb - 1 - j), 0))
    return pl.pallas_call(
        body, name="fgate_bwd", grid=(B, nb),
        in_specs=[tok, tok],
        out_specs=[tok, _acc_spec((1, LANES))],
        out_shape=[_sds((T, LANES), BF16), _sds((1, LANES), F32)],
        scratch_shapes=[pltpu.VMEM((8, LANES), F32)],
        compiler_params=_params(2),
    )(dc8, zf)


def _conv_bwd(dco, y, u, gt, cw, lng, lnb, B, S, comm=None):
    T = B * S
    CB = min(256, S)
    nb = S // CB
    hb = CB // CONV_HALO

    def body(dco_ref, y_ref, u_ref, gt_ref, up_ref, gp_ref, w_ref, lg_ref, lb_ref,
             du_ref, dgt_ref, dw_ref, vec_ref, acat, dycat, ash, dysh):
        b = pl.program_id(0)
        j = pl.program_id(1)
        jr = nb - 1 - j

        @pl.when((b == 0) & (j == 0))
        def _():
            dw_ref[...] = jnp.zeros_like(dw_ref)
            vec_ref[...] = jnp.zeros_like(vec_ref)

        @pl.when(j == 0)
        def _():
            dycat[CB:CB + CONV_HALO, :] = jnp.zeros((CONV_HALO, CONV_CH), F32)

        lg = lg_ref[...]
        rs, n, l = _layernorm_silu(y_ref[...], lg, lb_ref[...])
        sg = _sig(l)
        dl = dco_ref[...] * (sg * (1.0 + l * (1.0 - sg)))
        dn = dl * lg
        dy = rs * (dn - jnp.mean(dn, axis=-1, keepdims=True) - n * jnp.mean(dn * n, axis=-1, keepdims=True))
        vec_ref[0:1, :] += jnp.sum(dy, axis=0, keepdims=True)
        vec_ref[1:2, :] += jnp.sum(dl * n, axis=0, keepdims=True)
        vec_ref[2:3, :] += jnp.sum(dl, axis=0, keepdims=True)
        dycat[0:CB, :] = dy
        acat[0:CONV_HALO, :] = jnp.where(jr > 0, up_ref[...] * _sig(gp_ref[...]), 0.0)
        acat[CONV_HALO:CONV_HALO + CB, :] = u_ref[...] * _sig(gt_ref[...])
        _shifted_copies(acat, ash, CB + CONV_HALO - SUB)
        _shifted_copies(dycat, dysh, CB + CONV_HALO - SUB)
        for r0, rows, cs in _conv_pieces(CB):
            dyp = dycat[r0:r0 + rows, cs]
            da = jnp.zeros((rows, LANES), F32)
            for k in range(CONV_K):
                da = da + w_ref[k:k + 1, cs] * _tap(dycat, dysh, r0 + CONV_K - 1 - k, rows, cs)
                dw_ref[k:k + 1, cs] += jnp.sum(dyp * _tap(acat, ash, r0 + CONV_HALO - (CONV_K - 1) + k, rows, cs),
                                               axis=0, keepdims=True)
            uv = u_ref[r0:r0 + rows, cs]
            sgt = _sig(gt_ref[r0:r0 + rows, cs])
            du_ref[r0:r0 + rows, cs] = (da * sgt).astype(BF16)
            dgt_ref[r0:r0 + rows, cs] = (da * uv * sgt * (1.0 - sgt)).astype(BF16)
        dycat[CB:CB + CONV_HALO, :] = dycat[0:CONV_HALO, :]

    tok = lambda w: pl.BlockSpec((CB, w), lambda b, j: (b * nb + (nb - 1 - j), 0))
    prev = pl.BlockSpec((CONV_HALO, CONV_CH), lambda b, j: (jnp.maximum((b * nb + (nb - 1 - j)) * hb - 1, 0), 0))
    return _call(
        body, comm, name="conv_bwd", grid=(B, nb),
        in_specs=[tok(CONV_CH), tok(CONV_CH), tok(CONV_CH), tok(CONV_CH), prev, prev, _resident(cw), _resident(lng),
                  _resident(lnb)],
        out_specs=[tok(CONV_CH), tok(CONV_CH), _acc_spec((CONV_HALO, CONV_CH)), _acc_spec((8, CONV_CH))],
        out_shape=[_sds((T, CONV_CH), BF16), _sds((T, CONV_CH), BF16), _sds((CONV_HALO, CONV_CH), F32),
                   _sds((8, CONV_CH), F32)],
        scratch_shapes=[pltpu.VMEM((CONV_HALO + CB, CONV_CH), F32), pltpu.VMEM((CB + CONV_HALO, CONV_CH), F32),
                        pltpu.VMEM((SUB, CB + CONV_HALO - SUB, CONV_CH), F32),
                        pltpu.VMEM((SUB, CB + CONV_HALO - SUB, CONV_CH), F32)],
        args=(dco, y, u, gt, u, gt, cw, lng, lnb))


def _bwd_in(dz, w_int, w_ft, x2, dx1, g_mix, T, comm=None):
    TB = min(512, T)
    nb = T // TB

    def body(dz_ref, w_ref, wf_ref, x_ref, d1_ref, g_ref, gx_ref, dg_ref):
        i = pl.program_id(0)

        @pl.when(i == 0)
        def _():
            dg_ref[...] = jnp.zeros_like(dg_ref)

        dh = _dot(dz_ref[:, 0:OFF_F], w_ref[0:OFF_F, :]) + _dot(dz_ref[:, OFF_F:D_IN_PAD], wf_ref[...])
        r0, xh0 = _rms(x_ref[...])
        dx, dg_tok = _rms_bwd(dh, xh0, r0, g_ref[...])
        gx_ref[...] = d1_ref[...] + dx
        dg_ref[...] += jnp.sum(dg_tok, axis=0, keepdims=True)

    tok = lambda w: pl.BlockSpec((TB, w), lambda i: (i, 0))
    return _call(
        body, comm, name="bwd_in", grid=(nb,),
        in_specs=[tok(D_IN_PAD), _resident(w_int), _resident(w_ft), tok(D), tok(D), _resident(g_mix)],
        out_specs=[tok(D), _acc_spec((1, D))],
        out_shape=[_sds((T, D), F32), _sds((1, D), F32)],
        scratch_shapes=[],
        args=(dz, w_int, w_ft, x2, dx1, g_mix))


def _dw(a, b, name, tn, slabs=0, tk=None, rows=None):
    T, K = a.shape
    N = b.shape[1]
    per = tn // slabs if slabs else 1
    tk = tk or (K if K <= 1024 else K // 2)
    tt = min(1024, T)
    nt = T // tt

    def body(a_ref, b_ref, o_ref, acc):
        t = pl.program_id(2)

        @pl.when(t == 0)
        def _():
            acc[...] = jnp.zeros_like(acc)

        acc[...] += _dot_tn(a_ref[...].astype(BF16), b_ref[...].astype(BF16))

        @pl.when(t == nt - 1)
        def _():
            if slabs:
                for sl in range(per):
                    o_ref[sl] = acc[:, sl * slabs:(sl + 1) * slabs]
            else:
                o_ref[...] = acc[...]

    return pl.pallas_call(
        body, name=name, grid=(K // tk, N // tn, nt),
        in_specs=[pl.BlockSpec((tt, tk), lambda i, j, t: (t, i)), pl.BlockSpec((tt, tn), lambda i, j, t: (t, j))],
        out_specs=(pl.BlockSpec((per, tk, slabs), lambda i, j, t: (j, i, 0)) if slabs
                   else pl.BlockSpec((tk, tn), lambda i, j, t: (i, j))),
        out_shape=_sds((N // slabs, K, slabs) if slabs else (rows or K, N), F32),
        scratch_shapes=[pltpu.VMEM((tk, tn), F32)],
        compiler_params=_params(3),
    )(a, b)


def _pos():
    return lax.axis_index("x"), lax.axis_index("y"), lax.axis_index("c")


def _remote(src, dst, ssem, rsem, to):
    return pltpu.make_async_remote_copy(src_ref=src, dst_ref=dst, send_sem=ssem, recv_sem=rsem, device_id=to,
                                        device_id_type=MESH)


def _split_axis(shape):
    return 0 if shape[0] % 32 == 0 else 1


def _half_shape(shape, parts=2):
    return (shape[0] // parts, shape[1]) if _split_axis(shape) == 0 else (shape[0], shape[1] // parts)


def _half(shape, c):
    R, C = shape
    if _split_axis(shape) == 0:
        return (pl.ds(pl.multiple_of(c * (R // 2), 16), R // 2), slice(None))
    return (slice(None), pl.ds(pl.multiple_of(c * (C // 2), LANES), C // 2))


def _half_block(shape, parts, lead, which):
    blk = _half_shape(shape, parts)
    idx = (which, 0) if _split_axis(shape) == 0 else (0, which)
    return blk, tuple(lead) + idx


class _Comm:
    def __init__(self, ins, out_shapes, sems, start, finish):
        self.ins, self.out_shapes, self.sems, self.start, self.finish = list(ins), list(out_shapes), list(sems), start, finish


def _ag_comm(shards):
    n = len(shards)

    def parts(ins, outs, sems):
        send_sems, recv_sems, local_sems = sems
        x, y, c = _pos()
        me, sib = (x, y, c), (x, y, 1 - c)
        chips = [(1 - x, y), (x, 1 - y), (1 - x, 1 - y)]

        def rows(w, px, py, pc):
            return outs[w].at[(2 * px + py,) + _half(shards[w].shape, pc)]

        def copy(w, k, block, to, src=None):
            return _remote(rows(w, *block) if src is None else src, rows(w, *block), send_sems.at[w, k],
                           recv_sems.at[w, k], to)

        mine, first = [], []
        for w in range(n):
            src = ins[w].at[_half(shards[w].shape, c)]
            mine.append(pltpu.make_async_copy(src, rows(w, *me), local_sems.at[w]))
            first += [copy(w, 0, me, sib, src=src)] + [copy(w, 1 + j, me, (*chip, c), src=src) for j, chip in enumerate(chips)]
        return c, me, sib, chips, copy, mine, first

    def start(ins, outs, sems):
        _, _, _, _, _, mine, first = parts(ins, outs, sems)
        for cp in mine + first:
            cp.start()

    def finish(ins, outs, sems):
        c, me, sib, chips, copy, mine, first = parts(ins, outs, sems)
        passed = []
        for w in range(n):
            for j, chip in enumerate(chips):
                copy(w, 1 + j, (*chip, c), me).wait_recv()
                passed.append(copy(w, 4 + j, (*chip, c), sib))
                passed[-1].start()
        for w in range(n):
            copy(w, 0, sib, me).wait_recv()
            for j, chip in enumerate(chips):
                copy(w, 4 + j, (*chip, 1 - c), me).wait_recv()
        for cp in first + passed:
            cp.wait_send()
        for cp in mine:
            cp.wait()

    D7 = pltpu.SemaphoreType.DMA((n, 7))
    return _Comm(shards, [_sds((4,) + s.shape, s.dtype) for s in shards], [D7, D7, pltpu.SemaphoreType.DMA((n,))],
                 start, finish)


def _sibling_comm(gs):
    n = len(gs)

    def copies(ins, outs, sems):
        send_sems, recv_sems = sems
        x, y, c = _pos()
        return [_remote(ins[w].at[(s,) + _half(gs[w].shape[1:], 1 - c)], outs[w].at[s], send_sems.at[w, s],
                        recv_sems.at[w, s], (x, y, 1 - c)) for w in range(n) for s in range(4)]

    def start(ins, outs, sems):
        for cp in copies(ins, outs, sems):
            cp.start()

    def finish(ins, outs, sems):
        for cp in copies(ins, outs, sems):
            cp.wait()

    D4 = pltpu.SemaphoreType.DMA((n, 4))
    return _Comm(gs, [_sds((4,) + _half_shape(g.shape[1:]), F32) for g in gs], [D4, D4], start, finish)


def _ici_comm(pbs):
    n = len(pbs)

    def copies(ins, outs, sems):
        send_sems, recv_sems = sems
        x, y, c = _pos()
        return [_remote(ins[w].at[2 * tx + ty], outs[w].at[j], send_sems.at[w, j], recv_sems.at[w, j], (tx, ty, c))
                for w in range(n) for j, (tx, ty) in enumerate([(1 - x, y), (x, 1 - y), (1 - x, 1 - y)])]

    def start(ins, outs, sems):
        for cp in copies(ins, outs, sems):
            cp.start()

    def finish(ins, outs, sems):
        for cp in copies(ins, outs, sems):
            cp.wait()

    D3 = pltpu.SemaphoreType.DMA((n, 3))
    return _Comm(pbs, [_sds((3,) + p.shape[1:], BF16) for p in pbs], [D3, D3], start, finish)


def _join(*comms):
    counts = [(len(c.ins), len(c.out_shapes), len(c.sems)) for c in comms]

    def each(which):
        def run(ins, outs, sems):
            i = o = k = 0
            for c, (ni, no, nk) in zip(comms, counts):
                getattr(c, which)(ins[i:i + ni], outs[o:o + no], sems[k:k + nk])
                i, o, k = i + ni, o + no, k + nk
        return run

    return _Comm(sum((c.ins for c in comms), []), sum((c.out_shapes for c in comms), []),
                 sum((c.sems for c in comms), []), each("start"), each("finish"))


def _run_comm(comm, name):
    ni, no = len(comm.ins), len(comm.out_shapes)

    def body(*refs):
        ins, outs, sems = refs[:ni], refs[ni:ni + no], refs[ni + no:]
        comm.start(ins, outs, sems)
        comm.finish(ins, outs, sems)

    return pl.pallas_call(body, name=name, out_shape=comm.out_shapes, in_specs=[ANY] * ni, out_specs=[ANY] * no,
                          scratch_shapes=comm.sems)(*comm.ins)


def _call(body, comm, *, name, grid, in_specs, out_specs, out_shape, scratch_shapes, args):
    n_grid = len(grid)
    if comm is None:
        res = pl.pallas_call(body, name=name, grid=grid, in_specs=in_specs, out_specs=out_specs, out_shape=out_shape,
                             scratch_shapes=scratch_shapes, compiler_params=_params(n_grid))(*args)
        return list(res), []
    n_in, n_out, n_scr = len(in_specs), len(out_specs), len(scratch_shapes)
    ni, no = len(comm.ins), len(comm.out_shapes)

    def carried(*refs):
        ins, refs = refs[:n_in], refs[n_in:]
        cins, refs = refs[:ni], refs[ni:]
        outs, refs = refs[:n_out], refs[n_out:]
        couts, refs = refs[:no], refs[no:]
        scr, csems = refs[:n_scr], refs[n_scr:]
        ids = [pl.program_id(ax) for ax in range(n_grid)]
        first = functools.reduce(jnp.logical_and, [i == 0 for i in ids])
        last = functools.reduce(jnp.logical_and, [i == g - 1 for i, g in zip(ids, grid)])

        @pl.when(first)
        def _():
            comm.start(cins, couts, csems)

        body(*ins, *outs, *scr)

        @pl.when(last)
        def _():
            comm.finish(cins, couts, csems)

    res = pl.pallas_call(
        carried, name=name, grid=grid, in_specs=list(in_specs) + [ANY] * ni, out_specs=list(out_specs) + [ANY] * no,
        out_shape=list(out_shape) + comm.out_shapes, scratch_shapes=list(scratch_shapes) + comm.sems,
        compiler_params=_params(n_grid))(*args, *comm.ins)
    return list(res[:n_out]), list(res[n_out:])


def _small_allreduce(v, name, halves=()):
    P = v.shape[0]
    n = len(halves)
    vm = pl.BlockSpec(memory_space=pltpu.VMEM)

    def body(v_ref, *refs):
        o_ref, outs = refs[n], refs[n + 1:2 * n + 1]
        gath, send_sems, recv_sems, half_send, half_recv = refs[2 * n + 1:]
        x, y, c = _pos()
        me = 4 * x + 2 * y + c
        gath[me] = v_ref[...]
        cps = []
        for r in range(1, 8):
            tx = (1 - x) if r & 4 else x
            ty = (1 - y) if r & 2 else y
            tc = (1 - c) if r & 1 else c
            cps.append(_remote(v_ref, gath.at[me], send_sems.at[r - 1], recv_sems.at[r - 1], (tx, ty, tc)))
        for w in range(n):
            mine = outs[w].at[_half(halves[w].shape, c)]
            cps.append(_remote(mine, mine, half_send.at[w], half_recv.at[w], (x, y, 1 - c)))
        for cp in cps:
            cp.start()
        for cp in cps:
            cp.wait()
        acc = gath[0]
        for d in range(1, 8):
            acc = acc + gath[d]
        o_ref[...] = acc

    res = pl.pallas_call(
        body, name=name, out_shape=[_sds((P, LANES), F32)] + [_sds(g.shape, F32) for g in halves],
        in_specs=[vm] + [ANY] * n, out_specs=[vm] + [ANY] * n, input_output_aliases={1 + w: 1 + w for w in range(n)},
        scratch_shapes=[pltpu.VMEM((8, P, LANES), F32), pltpu.SemaphoreType.DMA((7,)), pltpu.SemaphoreType.DMA((7,)),
                        pltpu.SemaphoreType.DMA((max(n, 1),)), pltpu.SemaphoreType.DMA((max(n, 1),))],
    )(v, *halves)
    return res[0], list(res[1:])


def _chip_sum(gs, rcvs, pos, name):
    n = len(gs)
    shards = [g.shape[1:] for g in gs]
    hss = [_half_shape(sh) for sh in shards]
    other = lambda i, pos: (pos[1] + 1 + i) % 4

    def body(pos_ref, *refs):
        for w in range(n):
            refs[2 * n + w][...] = (refs[w][...] + refs[n + w][...]).astype(BF16)

    mine = lambda w: pl.BlockSpec((1,) + hss[w], lambda i, pos: _half_block(shards[w], 2, (other(i, pos),), pos[0])[1])
    whole = lambda w: pl.BlockSpec((1,) + hss[w], lambda i, pos: (other(i, pos), 0, 0))
    return pl.pallas_call(
        body, name=name, out_shape=[_sds((4,) + hs, BF16) for hs in hss],
        grid_spec=pltpu.PrefetchScalarGridSpec(
            num_scalar_prefetch=1, grid=(3,),
            in_specs=[mine(w) for w in range(n)] + [whole(w) for w in range(n)],
            out_specs=[whole(w) for w in range(n)]),
        compiler_params=_params(1),
    )(pos, *gs, *rcvs)


def _final_sum(gs, rcvs, rcs, pos, name):
    n = len(gs)
    shards = [g.shape[1:] for g in gs]
    qss = [_half_shape(sh, 4) for sh in shards]

    def body(pos_ref, *refs):
        for w in range(n):
            acc = refs[w][0] + refs[n + w][0]
            for j in range(3):
                acc = acc + refs[2 * n + w][j].astype(F32)
            refs[3 * n + w][...] = acc

    def spec(w, lead_block, lead_index, mine):
        return pl.BlockSpec(lead_block + qss[w], lambda i, pos: _half_block(
            shards[w], 4, lead_index(pos), pos[0] * 2 + i if mine else i)[1])

    own_slab, first, none = (lambda pos: (pos[1],)), (lambda pos: (0,)), (lambda pos: ())
    return pl.pallas_call(
        body, name=name, out_shape=[_sds(sh, F32) for sh in shards],
        grid_spec=pltpu.PrefetchScalarGridSpec(
            num_scalar_prefetch=1, grid=(2,),
            in_specs=[spec(w, (1,), own_slab, True) for w in range(n)] + [spec(w, (1,), own_slab, False) for w in range(n)]
            + [spec(w, (3,), first, False) for w in range(n)],
            out_specs=[spec(w, (), none, True) for w in range(n)]),
        compiler_params=_params(1),
    )(pos, *gs, *rcvs, *rcs)


def _adamw_math(w, g, m, v):
    m = ADAM_B1 * m + (1.0 - ADAM_B1) * g
    v = ADAM_B2 * v + (1.0 - ADAM_B2) * (g * g)
    m_hat = m / (1.0 - ADAM_B1 ** ADAM_STEP)
    v_hat = v / (1.0 - ADAM_B2 ** ADAM_STEP)
    delta = -ADAM_LR * (m_hat / (jnp.sqrt(v_hat) + ADAM_EPS) + ADAM_WD * w)
    return delta, m, v


ADAM_PARTS = 8


def _adamw(ws, gs, ms, vs, name, parts=ADAM_PARTS):
    n = len(ws)
    shapes = [w.shape for w in ws]

    def body(*refs):
        for k in range(n):
            w_ref, g_ref, m_ref, v_ref = (refs[j * n + k] for j in range(4))
            go_ref, d_ref, nm_ref, nv_ref = (refs[(4 + j) * n + k] for j in range(4))
            g = g_ref[...]
            d, nm, nv = _adamw_math(w_ref[...], g, m_ref[...], v_ref[...])
            go_ref[...] = g
            d_ref[...] = d
            nm_ref[...] = nm
            nv_ref[...] = nv

    blk = lambda k: pl.BlockSpec(_half_shape(shapes[k], parts), lambda i: _half_block(shapes[k], parts, (), i)[1])
    res = pl.pallas_call(
        body, name=name, grid=(parts,), in_specs=[blk(k) for k in range(n)] * 4, out_specs=[blk(k) for k in range(n)] * 4,
        out_shape=[_sds(sh, F32) for sh in shapes] * 4, compiler_params=_params(1),
    )(*ws, *gs, *ms, *vs)
    return [tuple(res[j * n + k] for j in range(4)) for k in range(n)]


SMALL = (("g_mix", 8), ("b_f", 8), ("conv_w", None), ("conv_b", 8), ("ln_g", 8), ("ln_b", 8), ("g_x", 8), ("g_mem", 8),
         ("g_ffn", 8), ("g_final", 8), ("loss", 8))


def _pack_small(parts, conv_rows):
    rows = []
    for name, n in SMALL:
        if name not in parts:
            continue
        n = conv_rows if n is None else n
        flat = parts[name].reshape(-1).astype(F32)
        flat = jnp.pad(flat, (0, n * LANES - flat.shape[0]))
        rows.append(flat.reshape(n, LANES))
    return jnp.concatenate(rows, axis=0)


def _unpack_small(p, shapes, conv_rows):
    out, off = {}, 0
    for name, n in SMALL:
        if name not in shapes:
            continue
        n = conv_rows if n is None else n
        size = math.prod(shapes[name])
        out[name] = p[off:off + n].reshape(-1)[:size].reshape(shapes[name])
        off += n
    return out


def kernel(x, mem, g_mix, w_in, b_f, conv_w, conv_b, ln_g, ln_b, w_out, g_x, g_mem, w_mq, w_mkv, w_mo, g_ffn, w_gu, w_down, g_final, loss_target, m_g_mix, m_w_in, m_b_f, m_conv_w, m_conv_b, m_ln_g, m_ln_b, m_w_out, m_g_x, m_g_mem, m_w_mq, m_w_mkv, m_w_mo, m_g_ffn, m_w_gu, m_w_down, m_g_final, v_g_mix, v_w_in, v_b_f, v_conv_w, v_conv_b, v_ln_g, v_ln_b, v_w_out, v_g_x, v_g_mem, v_w_mq, v_w_mkv, v_w_mo, v_g_ffn, v_w_gu, v_w_down, v_g_final):
    names = ["g_mix", "w_in", "b_f", "conv_w", "conv_b", "ln_g", "ln_b", "w_out", "g_x", "g_mem", "w_mq", "w_mkv",
             "w_mo", "g_ffn", "w_gu", "w_down", "g_final"]
    W = dict(zip(names, [g_mix, w_in, b_f, conv_w, conv_b, ln_g, ln_b, w_out, g_x, g_mem, w_mq, w_mkv, w_mo, g_ffn,
                         w_gu, w_down, g_final]))
    M = dict(zip(names, [m_g_mix, m_w_in, m_b_f, m_conv_w, m_conv_b, m_ln_g, m_ln_b, m_w_out, m_g_x, m_g_mem, m_w_mq,
                         m_w_mkv, m_w_mo, m_g_ffn, m_w_gu, m_w_down, m_g_final]))
    V = dict(zip(names, [v_g_mix, v_w_in, v_b_f, v_conv_w, v_conv_b, v_ln_g, v_ln_b, v_w_out, v_g_x, v_g_mem, v_w_mq,
                         v_w_mkv, v_w_mo, v_g_ffn, v_w_gu, v_w_down, v_g_final]))
    big_names = [n for n, _, _, _ in BIG]
    B, S, _ = x.shape
    T = B * S
    mx, my, mc = _pos()
    chip = 2 * mx + my
    pos = jnp.stack([mc, chip]).astype(jnp.int32)

    shard2d = lambda a: a.reshape(a.shape[-2], a.shape[-1])
    big2d = lambda d, n: shard2d(d[n]).T if n == "w_in" else shard2d(d[n])
    shard_bf = {n: big2d(W, n).astype(BF16) for n in big_names}
    ag_mid = ["w_mkv", "w_out", "w_mq", "w_mo"]
    ag_ffn = ["w_gu", "w_down"]
    cw_mine = jnp.pad(shard2d(conv_w), ((0, 1), (0, 0)))
    w_in_slab, cw_slab = _run_comm(_ag_comm([shard_bf["w_in"], cw_mine]), "ag_w_in")
    slab = {"w_in": w_in_slab}
    w_int = w_in_slab.reshape(D_IN, D)
    w_ft = jnp.pad(w_int[OFF_F:D_IN], ((0, D_IN_PAD - D_IN), (0, 0)))
    cw = jnp.transpose(cw_slab, (1, 0, 2)).reshape(CONV_HALO, CONV_CH)

    row = lambda a: a.reshape(1, -1)
    bf_pad = jnp.pad(row(b_f), ((0, 0), (0, LANES - 8)))
    x2d = x.reshape(T, D)
    mem2d = mem.reshape(B * MEM_LEN, D)
    tgt = loss_target.reshape(T, D)

    (h, u, gt, q, k, v, zf, c, cq, qx, kx, qT), got = _fwd_in(x2d, row(g_mix), w_int, w_ft, bf_pad, B, S,
                                                  comm=_ag_comm([shard_bf[n] for n in ag_mid]))
    slab.update(zip(ag_mid, got))
    ckT = jnp.transpose(c.reshape(B, S, LANES)[:, :, :8], (0, 2, 1)).reshape(B, N_PAIR, 2, S)
    ckT = jnp.pad(ckT, ((0, 0), (0, 0), (0, 6), (0, 0)))
    (y, co), _ = _conv_fwd(u, gt, cw, row(conv_b), row(ln_g), row(ln_b), B, S)
    (o, fox_bias), got = _fox_fwd(qx, kx, v, cq, B, S, comm=_ag_comm([shard_bf[n] for n in ag_ffn]))
    slab.update(zip(ag_ffn, got))
    full = {n: slab[n] if by_col else slab[n].reshape(4 * r, c) for n, r, c, by_col in BIG}
    mn, km, vm = _mem_kv(mem2d, row(g_mem), full["w_mkv"], B)
    (x1, hx, qm, om, x2, cat), _ = _fwd_mid(x2d, co, o, km, vm, full["w_out"], full["w_mq"], full["w_mo"], row(g_x), B, S)
    hf, gu, act, dx3, loss_p, dg_final = _fwd_ffn(x2, tgt, full["w_gu"], full["w_down"], row(g_ffn), row(g_final), T)

    pos_sum = lambda gs, rcvs, ns: _chip_sum(gs, rcvs, pos, "rs_chip_sum_" + ns[0])
    fin_sum = lambda gs, rcvs, rcs, ns: _final_sum(gs, rcvs, rcs, pos, "rs_final_sum_" + ns[0])
    RH = {}
    dgu, dx2, dg_ffn = _bwd_ffn(dx3, gu, x2, full["w_gu"], full["w_down"], row(g_ffn), T)
    g_ffn_w = [_dw(hf, dgu, "dw_gu", D_FF, slabs=FF_CHUNK), _dw(act, dx3, "dw_down", D).reshape(4, D_FF // 4, D)]
    (dx1, dqm, dco, do, dd, dkm, dvm, dg_x, doT), rcv_ffn = _bwd_mid(dx2, x1, qm, km, vm, o, full["w_mo"], full["w_mq"],
                                                                full["w_out"], row(g_x), B, S, comm=_sibling_comm(g_ffn_w))
    pb_ffn = pos_sum(g_ffn_w, rcv_ffn, ag_ffn)
    dkv, dg_mem = _mem_bwd(dkm, dvm, mem2d, full["w_mkv"], row(g_mem), B)
    g_mid_w = [_dw(mn, dkv, "dw_mkv", 512, slabs=512), _dw(cat, dx1, "dw_out", D).reshape(4, 256, D),
               _dw(hx, dqm, "dw_mq", D).reshape(4, 256, D), _dw(om, dx2, "dw_mo", D).reshape(4, 256, D)]
    (dq, dk, dv, dc, dcq), got = _fox_bwd(q, k, v, do, fox_bias, dd, ckT, qT, doT, B, S,
                                          comm=_join(_ici_comm(pb_ffn), _sibling_comm(g_mid_w)))
    rc_ffn, rcv_mid = got[:len(pb_ffn)], got[len(pb_ffn):]
    RH.update(zip(ag_ffn, fin_sum(g_ffn_w, rcv_ffn, rc_ffn, ag_ffn)))
    pb_mid = pos_sum(g_mid_w, rcv_mid, ag_mid)
    dc8 = jnp.transpose(dc[:, :, :2, :].reshape(B, 8, S), (0, 2, 1)).reshape(T, 8)
    dc8 = dc8 + dcq.reshape(T, 8, HEAD_D)[:, :, 0]
    dzf, dbf = _fgate_bwd(jnp.pad(dc8, ((0, 0), (0, LANES - 8))), zf, B, S)
    (du, dgt, dcw, dvec), rc_mid = _conv_bwd(dco, y, u, gt, cw, row(ln_g), row(ln_b), B, S, comm=_ici_comm(pb_mid))
    RH.update(zip(ag_mid, fin_sum(g_mid_w, rcv_mid, rc_mid, ag_mid)))
    dz = jnp.concatenate([du, dgt, dq, dk, dv, dzf], axis=1)
    g_in_w = [_dw(dz, h, "dw_in", D, tk=D_IN_PAD // 3, rows=D_IN).reshape(4, D_IN // 4, D)]
    rcv_in = _run_comm(_sibling_comm(g_in_w), "rs_sibling_in")
    (grad_x, dg_mix), rc_in = _bwd_in(dz, w_int, w_ft, x2d, dx1, row(g_mix), T,
                                      comm=_ici_comm(pos_sum(g_in_w, rcv_in, ["w_in"])))
    RH.update(zip(["w_in"], fin_sum(g_in_w, rcv_in, rc_in, ["w_in"])))

    small_g = {"g_mix": dg_mix, "b_f": dbf[:, :8], "conv_w": dcw, "conv_b": dvec[0], "ln_g": dvec[1], "ln_b": dvec[2],
               "g_x": dg_x, "g_mem": dg_mem, "g_ffn": dg_ffn, "g_final": dg_final, "loss": loss_p[:, :1]}
    sg, filled = _small_allreduce(_pack_small(small_g, CONV_HALO * 4), "allreduce_small", [RH[n] for n in big_names])
    shared = dict(zip(big_names, filled))
    stepped = _adamw([big2d(W, n) for n in big_names], [shared[n] for n in big_names], [big2d(M, n) for n in big_names],
                     [big2d(V, n) for n in big_names], "adamw_big")
    G, DL, NM, NV = (dict(zip(big_names, col)) for col in zip(*stepped))
    shapes = {n: W[n].shape for n in names if n not in big_names}
    shapes["conv_w"] = (CONV_HALO, CONV_CH)
    shapes["loss"] = (1,)
    sgrads = _unpack_small(sg, shapes, CONV_HALO * 4)
    loss = sgrads.pop("loss")[0]
    sgrads["conv_w"] = lax.dynamic_slice(sgrads["conv_w"], (0, chip * LANES), (CONV_K, LANES)).reshape(W["conv_w"].shape)
    spack = lambda d: _pack_small({n: d[n] for n in sgrads}, CONV_HALO)
    (_, sd, snm, snv), = _adamw([spack(W)], [spack(sgrads)], [spack(M)], [spack(V)], "adamw_small", parts=1)
    sshapes = {n: W[n].shape for n in sgrads}
    SD, SNM, SNV = (_unpack_small(a, sshapes, CONV_HALO) for a in (sd, snm, snv))

    def collect(bigs, smalls):
        back = lambda n: (bigs[n].T if n == "w_in" else bigs[n]).reshape(W[n].shape)
        return [back(n) if n in big_names else smalls[n] for n in names]

    return (loss, grad_x.reshape(x.shape), *collect(G, sgrads), *collect(DL, SD), *collect(NM, SNM), *collect(NV, SNV))
```
